```python
import jax, jax.numpy as jnp
from jax import lax
import numpy as np

D_MODEL = 1024
BATCH = 4
SEQ = 8192
DEPTH = 1

ATTN_HEADS = 8
ATTN_HD = 64
ATTN_WIDTH = ATTN_HEADS * ATTN_HD
DILATED_PATTERNS = ((128, 1), (512, 4), (2048, 16))
ATTN_BLOCK = 128
DN_HEADS = 4
DN_KD = 128
DN_VD = 128
DN_QK_WIDTH = DN_HEADS * DN_KD
DN_V_WIDTH = DN_HEADS * DN_VD
DN_CONV_DIM = 2 * DN_QK_WIDTH + DN_V_WIDTH
CONV_K = 4
DN_CHUNK = 64
D_MIX = ATTN_WIDTH + DN_V_WIDTH
IN_SPLITS = (ATTN_WIDTH, 2 * ATTN_WIDTH, 3 * ATTN_WIDTH,
             3 * ATTN_WIDTH + DN_CONV_DIM,
             3 * ATTN_WIDTH + DN_CONV_DIM + DN_V_WIDTH,
             3 * ATTN_WIDTH + DN_CONV_DIM + DN_V_WIDTH + DN_HEADS)
IN_COLS = 3 * ATTN_WIDTH + DN_CONV_DIM + DN_V_WIDTH + 2 * DN_HEADS
N_EXPERTS = 32
TOP_K = 4
D_EXPERT = 1024
SWIGLU_LIMIT = 7.0
SWIGLU_ALPHA = 1.702
MOE_BLOCK = 128
NORM_EPS = 1e-6

kernel_name = 'hybrid_dilated_attn_gdn_moe_adaln'


def rms_norm(x, w):
    xf = x.astype(jnp.float32)
    y = xf * lax.rsqrt(jnp.mean(xf * xf, axis=-1, keepdims=True) + NORM_EPS)
    return (y * w.astype(jnp.float32)).astype(x.dtype)


def l2_norm(x):
    return x * lax.rsqrt(jnp.sum(x * x, axis=-1, keepdims=True) + NORM_EPS)


def modulate(x, w, shift, scale):
    return rms_norm(x, w) * (1 + scale[:, None, :]) + shift[:, None, :]


def dilated_branch(q, k, v, window, dilation):
    B, S, H, Dh = q.shape
    L = S // dilation
    span = window // dilation
    nb = -(-L // ATTN_BLOCK)
    Lp = nb * ATTN_BLOCK

    def to_sub(t):
        t = t.astype(jnp.float32).reshape(B, L, dilation, H, Dh).transpose(0, 2, 3, 1, 4)
        return jnp.pad(t, ((0, 0), (0, 0), (0, 0), (0, Lp - L), (0, 0)))

    def window_blocks(t):
        t = jnp.pad(t, ((0, 0), (0, 0), (0, 0), (ATTN_BLOCK, 0), (0, 0)))
        t = t.reshape(B, dilation, H, nb + 1, ATTN_BLOCK, Dh)
        return jnp.concatenate([t[:, :, :, :-1], t[:, :, :, 1:]], axis=-2)

    qb = to_sub(q).reshape(B, dilation, H, nb, ATTN_BLOCK, Dh)
    kw = window_blocks(to_sub(k))
    vw = window_blocks(to_sub(v))
    s = jnp.einsum('bdhnqe,bdhnke->bdhnqk', qb, kw) * (ATTN_HD ** -0.5)
    dist = (jnp.arange(ATTN_BLOCK)[:, None] + ATTN_BLOCK) - jnp.arange(2 * ATTN_BLOCK)[None, :]
    band = (dist >= 0) & (dist <= span)
    kpos = jnp.arange(nb)[:, None] * ATTN_BLOCK - ATTN_BLOCK + jnp.arange(2 * ATTN_BLOCK)[None, :]
    mask = band[None, :, :] & (kpos >= 0)[:, None, :]
    s = jnp.where(mask, s, -jnp.inf)
    m = jnp.max(s, axis=-1, keepdims=True)
    p = jnp.exp(s - m)
    den = jnp.sum(p, axis=-1)
    o = jnp.einsum('bdhnqk,bdhnke->bdhnqe', p, vw) / den[..., None]
    lse = m[..., 0] + jnp.log(den)
    o = o.reshape(B, dilation, H, Lp, Dh)[:, :, :, :L].transpose(0, 3, 1, 2, 4).reshape(B, S, H, Dh)
    lse = lse.reshape(B, dilation, H, Lp)[:, :, :, :L].transpose(0, 3, 1, 2).reshape(B, S, H)
    return o, lse


def dilated_attention(q, k, v):
    outs, lses = [], []
    for window, dilation in DILATED_PATTERNS:
        o, lse = dilated_branch(q, k, v, window, dilation)
        outs.append(o)
        lses.append(lse)
    wts = jax.nn.softmax(jnp.stack(lses, axis=0), axis=0)
    return jnp.einsum('gbsh,gbshe->bshe', wts, jnp.stack(outs, axis=0))


def causal_short_conv(x, w):
    S = x.shape[1]
    xp = jnp.pad(x, ((0, 0), (CONV_K - 1, 0), (0, 0)))
    y = w[CONV_K - 1] * x
    for j in range(CONV_K - 1):
        y = y + w[j] * xp[:, j:j + S]
    return y


def gated_delta_rule(q, k, v, g, beta):
    B, S, H, Dk = q.shape
    Dv = v.shape[-1]
    C = DN_CHUNK
    N = S // C
    q = l2_norm(q) * (Dk ** -0.5)
    k = l2_norm(k)

    def chunk(t):
        return jnp.swapaxes(t.reshape((B, N, C, H) + t.shape[3:]), 2, 3)

    qc, kc, vc, gc, bc = chunk(q), chunk(k), chunk(v), chunk(g), chunk(beta)
    gc = jnp.cumsum(gc, axis=-1)
    tril = jnp.tril(jnp.ones((C, C), dtype=bool))
    strict = jnp.tril(jnp.ones((C, C), dtype=bool), -1)
    decay = jnp.exp(jnp.where(tril, gc[..., :, None] - gc[..., None, :], -jnp.inf))
    kb = kc * bc[..., None]
    vb = vc * bc[..., None]
    a_low = jnp.where(strict, jnp.einsum('bnhie,bnhje->bnhij', kb, kc) * decay, 0.0)
    lhs = a_low + jnp.eye(C, dtype=jnp.float32)
    u = lax.linalg.triangular_solve(lhs, vb, left_side=True, lower=True)
    w = lax.linalg.triangular_solve(lhs, kb * jnp.exp(gc)[..., None], left_side=True, lower=True)
    intra = jnp.einsum('bnhie,bnhje->bnhij', qc, kc) * decay
    q_dec = qc * jnp.exp(gc)[..., None]
    k_dec = kc * jnp.exp(gc[..., -1:] - gc)[..., None]
    g_last = jnp.exp(gc[..., -1])

    def step(state, xs):
        qd, kd, u_i, w_i, intra_i, gl = xs
        v_new = u_i - jnp.einsum('bhce,bhef->bhcf', w_i, state)
        o = jnp.einsum('bhce,bhef->bhcf', qd, state) + jnp.einsum('bhij,bhjf->bhif', intra_i, v_new)
        state = state * gl[..., None, None] + jnp.einsum('bhce,bhcf->bhef', kd, v_new)
        return state, o

    xs = tuple(jnp.moveaxis(t, 1, 0) for t in (q_dec, k_dec, u, w, intra, g_last))
    state0 = jnp.zeros((B, H, Dk, Dv), jnp.float32)
    _, o = lax.scan(step, state0, xs)
    return o.transpose(1, 0, 3, 2, 4).reshape(B, S, H, Dv)


def hybrid_mixer(h, w_in, conv_w, A_log, dt_bias, dn_norm_w, w_out):
    B, S, _ = h.shape
    proj = h @ w_in
    aq, ak, av, dqkv, dz, db, da = jnp.split(proj, IN_SPLITS, axis=-1)
    shp_a = (B, S, ATTN_HEADS, ATTN_HD)
    attn_out = dilated_attention(aq.reshape(shp_a), ak.reshape(shp_a), av.reshape(shp_a))
    attn_out = attn_out.reshape(B, S, ATTN_WIDTH).astype(h.dtype)
    dqkv = jax.nn.silu(causal_short_conv(dqkv, conv_w)).astype(jnp.float32)
    dq, dk, dv = jnp.split(dqkv, (DN_QK_WIDTH, 2 * DN_QK_WIDTH), axis=-1)
    g = -jnp.exp(A_log.astype(jnp.float32)) * jax.nn.softplus(da.astype(jnp.float32) + dt_bias.astype(jnp.float32))
    beta = jax.nn.sigmoid(db.astype(jnp.float32))
    o = gated_delta_rule(dq.reshape(B, S, DN_HEADS, DN_KD), dk.reshape(B, S, DN_HEADS, DN_KD),
                         dv.reshape(B, S, DN_HEADS, DN_VD), g, beta)
    o = rms_norm(o, dn_norm_w) * jax.nn.silu(dz.astype(jnp.float32).reshape(B, S, DN_HEADS, DN_VD))
    dn_out = o.reshape(B, S, DN_V_WIDTH).astype(h.dtype)
    return jnp.concatenate([attn_out, dn_out], axis=-1) @ w_out


def clamped_swiglu(hgu):
    gate, up = jnp.split(hgu, 2, axis=-1)
    gate = jnp.minimum(gate, SWIGLU_LIMIT)
    up = jnp.clip(up, -SWIGLU_LIMIT, SWIGLU_LIMIT)
    return gate * jax.nn.sigmoid(SWIGLU_ALPHA * gate) * (up + 1)


def routed_moe(h, w_router, b_router, w1, b1, w2, b2):
    B, S, D = h.shape
    T = B * S
    xt = h.reshape(T, D)
    logits = (xt @ w_router + b_router).astype(jnp.float32)
    top_v, top_e = lax.top_k(logits, TOP_K)
    gates = jax.nn.softmax(top_v, axis=-1)
    A = T * TOP_K
    e_flat = top_e.reshape(A)
    tok_flat = jnp.repeat(jnp.arange(T, dtype=jnp.int32), TOP_K)
    g_flat = gates.reshape(A)
    order = jnp.argsort(e_flat)
    se, stok, sg = e_flat[order], tok_flat[order], g_flat[order]
    counts = jnp.zeros((N_EXPERTS,), jnp.int32).at[e_flat].add(1)
    padded = (counts + MOE_BLOCK - 1) // MOE_BLOCK * MOE_BLOCK
    pend = jnp.cumsum(padded)
    pstart = pend - padded
    cstart = jnp.cumsum(counts) - counts
    dest = pstart[se] + (jnp.arange(A, dtype=jnp.int32) - cstart[se])
    P = A + N_EXPERTS * MOE_BLOCK
    NB = P // MOE_BLOCK
    row_tok = jnp.zeros((P,), jnp.int32).at[dest].set(stok)
    row_gate = jnp.zeros((P,), jnp.float32).at[dest].set(sg)
    block_exp = jnp.minimum(jnp.searchsorted(pend, jnp.arange(NB) * MOE_BLOCK, side='right'),
                            N_EXPERTS - 1).astype(jnp.int32)
    xs = xt[row_tok].reshape(NB, MOE_BLOCK, D)

    def expert_block(args):
        xb, e = args
        return clamped_swiglu(xb @ w1[e] + b1[e]) @ w2[e] + b2[e]

    yb = lax.map(expert_block, (xs, block_exp)).reshape(P, D)
    y = jnp.zeros((T, D), h.dtype).at[row_tok].add(yb * row_gate[:, None].astype(yb.dtype))
    return y.reshape(B, S, D)


def setup_inputs(seed: int = 0) -> dict:
    key = jax.random.key(seed)
    ks = jax.random.split(key, 20)
    D, L, E, F = D_MODEL, DEPTH, N_EXPERTS, D_EXPERT
    nrm = jax.random.normal
    return {
        'x': nrm(ks[0], (BATCH, SEQ, D), jnp.float32),
        'c': nrm(ks[1], (BATCH, D), jnp.float32),
        'w_ada': nrm(ks[2], (L, D, 6 * D), jnp.float32) * (0.5 * D ** -0.5),
        'b_ada': nrm(ks[3], (L, 6 * D), jnp.float32) * 0.02,
        'norm1_w': 1.0 + 0.05 * nrm(ks[4], (L, D), jnp.float32),
        'w_in': nrm(ks[5], (L, D, IN_COLS), jnp.float32) * (D ** -0.5),
        'conv_w': nrm(ks[6], (L, CONV_K, DN_CONV_DIM), jnp.float32) * 0.5,
        'A_log': jnp.log(jax.random.uniform(ks[7], (L, DN_HEADS), jnp.float32, 1.0, 16.0)),
        'dt_bias': jax.random.uniform(ks[8], (L, DN_HEADS), jnp.float32, -4.5, -2.5),
        'dn_norm_w': 1.0 + 0.05 * nrm(ks[9], (L, DN_VD), jnp.float32),
        'w_out': nrm(ks[10], (L, D_MIX, D), jnp.float32) * (D_MIX ** -0.5),
        'norm2_w': 1.0 + 0.05 * nrm(ks[11], (L, D), jnp.float32),
        'w_router': nrm(ks[12], (L, D, E), jnp.float32) * (D ** -0.5),
        'b_router': nrm(ks[13], (L, E), jnp.float32) * 0.01,
        'w1': nrm(ks[14], (L, E, D, 2 * F), jnp.float32) * (D ** -0.5),
        'b1': nrm(ks[15], (L, E, 2 * F), jnp.float32) * 0.01,
        'w2': nrm(ks[16], (L, E, F, D), jnp.float32) * (F ** -0.5),
        'b2': nrm(ks[17], (L, E, D), jnp.float32) * 0.01,
        'final_norm_w': 1.0 + 0.05 * nrm(ks[18], (D,), jnp.float32),
    }


def reference(x, c, w_ada, b_ada, norm1_w, w_in, conv_w, A_log, dt_bias, dn_norm_w, w_out,
              norm2_w, w_router, b_router, w1, b1, w2, b2, final_norm_w):
    cond = jax.nn.silu(c)
    for l in range(DEPTH):
        mod = cond @ w_ada[l] + b_ada[l]
        shift1, scale1, gate1, shift2, scale2, gate2 = jnp.split(mod, 6, axis=-1)
        h = modulate(x, norm1_w[l], shift1, scale1)
        x = x + gate1[:, None, :] * hybrid_mixer(h, w_in[l], conv_w[l], A_log[l], dt_bias[l],
                                                 dn_norm_w[l], w_out[l])
        h = modulate(x, norm2_w[l], shift2, scale2)
        x = x + gate2[:, None, :] * routed_moe(h, w_router[l], b_router[l], w1[l], b1[l], w2[l], b2[l])
    return rms_norm(x, final_norm_w)
```

```python
import functools

import jax
import jax.numpy as jnp
from jax import lax
from jax.experimental import pallas as pl
from jax.experimental.pallas import tpu as pltpu

F32 = jnp.float32
BF16 = jnp.bfloat16
I32 = jnp.int32
HI = lax.Precision.HIGHEST

LANES = 128
ATTN_HEADS = 8
ATTN_HD = 64
ATTN_W = ATTN_HEADS * ATTN_HD
ATTN_BLK = 128
DILATIONS = (1, 4, 16)
DN_HEADS = 4
DN_D = 128
DN_W = DN_HEADS * DN_D
DN_CONV = 3 * DN_W
CONV_K = 4
DN_CHUNK = 64
N_EXPERTS = 32
TOP_K = 4
SWIGLU_LIMIT = 7.0
SWIGLU_ALPHA = 1.702
EPS = 1e-6
NEG = -1e30
MAIN_COLS = 3 * ATTN_W + DN_CONV + DN_W

VMEM_LIMIT = 56 * 1024 * 1024


def _cparams(sem):
    return pltpu.CompilerParams(dimension_semantics=sem, vmem_limit_bytes=VMEM_LIMIT)


def _nt(a, b, **kw):
    return lax.dot_general(a, b, (((1,), (1,)), ((), ())), preferred_element_type=F32, **kw)


def _sigmoid(x):
    return 1.0 / (1.0 + jnp.exp(-x))


def _ada_kernel(c_ref, w_ref, b_ref, o_ref):
    c = c_ref[...]
    cond = c * _sigmoid(c)
    o_ref[...] = jnp.dot(cond, w_ref[...], preferred_element_type=F32, precision=HI) + b_ref[...]


def _ada(c, w_ada, b_ada):
    B, D = c.shape
    N = w_ada.shape[1]
    cp = jnp.zeros((8, D), F32).at[:B].set(c)
    tn = 1024
    out = pl.pallas_call(
        _ada_kernel,
        grid=(N // tn,),
        in_specs=[pl.BlockSpec((8, D), lambda j: (0, 0)),
                  pl.BlockSpec((D, tn), lambda j: (0, j)),
                  pl.BlockSpec((1, tn), lambda j: (0, j))],
        out_specs=pl.BlockSpec((8, tn), lambda j: (0, j)),
        out_shape=jax.ShapeDtypeStruct((8, N), F32),
        compiler_params=_cparams(("arbitrary",)),
        name="ada",
    )(cp, w_ada, b_ada.reshape(1, N))
    return out[:B].reshape(B, 6, D)


def _inproj_kernel(x_ref, mod_ref, nw_ref, wm_ref, ws_ref, qkv_ref, dqkv_ref, dz_ref, gbc_ref):
    x = x_ref[...]
    shift = mod_ref[0, 0:1, :]
    scale = mod_ref[0, 1:2, :]
    ms = jnp.mean(x * x, axis=-1, keepdims=True)
    h = x * lax.rsqrt(ms + EPS) * nw_ref[...]
    hb = (h * (1.0 + scale) + shift).astype(BF16)
    for j in range(3):
        r = jnp.dot(hb, wm_ref[:, j * ATTN_W:(j + 1) * ATTN_W], preferred_element_type=F32)
        if j == 0:
            r = r * (ATTN_HD ** -0.5)
        qkv_ref[:, j * ATTN_W:(j + 1) * ATTN_W] = r.astype(BF16)
    for j in range(3):
        c0 = 3 * ATTN_W + j * DN_W
        r = jnp.dot(hb, wm_ref[:, c0:c0 + DN_W], preferred_element_type=F32)
        dqkv_ref[:, j * DN_W:(j + 1) * DN_W] = r.astype(BF16)
    c0 = 3 * ATTN_W + DN_CONV
    dz_ref[...] = jnp.dot(hb, wm_ref[:, c0:c0 + DN_W], preferred_element_type=F32).astype(BF16)
    gbc_ref[...] = jnp.dot(hb, ws_ref[...], preferred_element_type=F32)


def _inproj(x2, mod, norm_w, w_in, S):
    T, D = x2.shape
    tm = 512
    wm = w_in[:, :MAIN_COLS].astype(BF16)
    ws = jnp.zeros((D, LANES), F32).at[:, :2 * DN_HEADS].set(w_in[:, MAIN_COLS:]).astype(BF16)
    return pl.pallas_call(
        _inproj_kernel,
        grid=(T // tm,),
        in_specs=[pl.BlockSpec((tm, D), lambda i: (i, 0)),
                  pl.BlockSpec((1, 6, D), lambda i: (i * tm // S, 0, 0)),
                  pl.BlockSpec((1, D), lambda i: (0, 0)),
                  pl.BlockSpec((D, MAIN_COLS), lambda i: (0, 0)),
                  pl.BlockSpec((D, LANES), lambda i: (0, 0))],
        out_specs=[pl.BlockSpec((tm, 3 * ATTN_W), lambda i: (i, 0)),
                   pl.BlockSpec((tm, DN_CONV), lambda i: (i, 0)),
                   pl.BlockSpec((tm, DN_W), lambda i: (i, 0)),
                   pl.BlockSpec((tm, LANES), lambda i: (i, 0))],
        out_shape=[jax.ShapeDtypeStruct((T, 3 * ATTN_W), BF16),
                   jax.ShapeDtypeStruct((T, DN_CONV), BF16),
                   jax.ShapeDtypeStruct((T, DN_W), BF16),
                   jax.ShapeDtypeStruct((T, LANES), F32)],
        compiler_params=_cparams(("arbitrary",)),
        name="inproj",
    )(x2, mod, norm_w.reshape(1, D), wm, ws)


def _attn_kernel(q_ref, kc_ref, kp_ref, vc_ref, vp_ref, o_ref, lse_ref, kf, vf, *, qb):
    n = pl.program_id(2)
    kf[0:ATTN_BLK, :] = kp_ref[0]
    kf[ATTN_BLK:, :] = kc_ref[0]
    vf[0:ATTN_BLK, :] = vp_ref[0]
    vf[ATTN_BLK:, :] = vc_ref[0]
    row = lax.broadcasted_iota(I32, (ATTN_BLK, 2 * ATTN_BLK), 0)
    col = lax.broadcasted_iota(I32, (ATTN_BLK, 2 * ATTN_BLK), 1)
    band = jnp.logical_or(jnp.logical_and(col < ATTN_BLK, col >= row),
                          jnp.logical_and(col >= ATTN_BLK, col - ATTN_BLK <= row))
    lane = lax.broadcasted_iota(I32, (ATTN_BLK, LANES), 1)
    lo = lane < ATTN_HD

    def sub(j, carry):
        r0 = pl.multiple_of(j * ATTN_BLK, ATTN_BLK)
        first_col = jnp.where(jnp.logical_and(n == 0, j == 0), ATTN_BLK, 0)
        mask = jnp.logical_and(band, col >= first_col)
        lse_tile = jnp.zeros((ATTN_BLK, LANES), F32)
        for hp in range(ATTN_W // LANES):
            cs = slice(hp * LANES, (hp + 1) * LANES)
            q2 = q_ref[0, pl.ds(r0, ATTN_BLK), cs]
            k2 = kf[pl.ds(r0, 2 * ATTN_BLK), cs]
            v2 = vf[pl.ds(r0, 2 * ATTN_BLK), cs]
            outs = []
            for half in range(2):
                qm = jnp.where(lo if half == 0 else jnp.logical_not(lo), q2, jnp.zeros_like(q2))
                s = _nt(qm, k2)
                s = jnp.where(mask, s, NEG)
                m = jnp.max(s, axis=-1, keepdims=True)
                p = jnp.exp(s - m)
                den = jnp.sum(p, axis=-1, keepdims=True)
                acc = jnp.dot(p.astype(BF16), v2, preferred_element_type=F32)
                outs.append(acc / den)
                lse_tile = jnp.where(lane == 2 * hp + half, m + jnp.log(den), lse_tile)
            o_ref[0, pl.ds(r0, ATTN_BLK), cs] = jnp.where(lo, outs[0], outs[1]).astype(BF16)
        lse_ref[0, pl.ds(r0, ATTN_BLK), :] = lse_tile
        return carry

    lax.fori_loop(0, qb // ATTN_BLK, sub, 0)


def _attn_branch(qkv, B, S, d):
    L = S // d
    qb = min(512, L)
    nsub = qb // ATTN_BLK
    W3 = 3 * ATTN_W
    view = qkv.reshape(B, L, d * W3)
    cur = lambda c: pl.BlockSpec((1, qb, ATTN_W), lambda b, r, n: (b, n, 3 * r + c))
    prev = lambda c: pl.BlockSpec((1, ATTN_BLK, ATTN_W),
                                  lambda b, r, n: (b, jnp.maximum(n * nsub - 1, 0), 3 * r + c))
    o, lse = pl.pallas_call(
        functools.partial(_attn_kernel, qb=qb),
        grid=(B, d, L // qb),
        in_specs=[cur(0), cur(1), prev(1), cur(2), prev(2)],
        out_specs=[pl.BlockSpec((1, qb, ATTN_W), lambda b, r, n: (b, n, r)),
                   pl.BlockSpec((1, qb, LANES), lambda b, r, n: (b, n, r))],
        out_shape=[jax.ShapeDtypeStruct((B, L, d * ATTN_W), BF16),
                   jax.ShapeDtypeStruct((B, L, d * LANES), F32)],
        scratch_shapes=[pltpu.VMEM((qb + ATTN_BLK, ATTN_W), BF16),
                        pltpu.VMEM((qb + ATTN_BLK, ATTN_W), BF16)],
        compiler_params=_cparams(("arbitrary", "arbitrary", "arbitrary")),
        name=f"attn_d{d}",
    )(view, view, view, view, view)
    return o.reshape(B * S, ATTN_W), lse.reshape(B * S, LANES)


def _gdn_kernel(x_ref, z_ref, g_ref, cw_ref, prm_ref, nw_ref, o_ref, xext, yc, st, *, rb):
    i = pl.program_id(1)

    @pl.when(i == 0)
    def _():
        xext[0:8, :] = jnp.zeros((8, DN_CONV), F32)
        st[...] = jnp.zeros_like(st)

    @pl.when(i > 0)
    def _():
        xext[0:8, :] = xext[rb:rb + 8, :]

    xext[8:, :] = x_ref[0].astype(F32)
    y = cw_ref[CONV_K - 1:CONV_K, :] * xext[8:8 + rb, :]
    for j in range(CONV_K - 1):
        off = 8 - (CONV_K - 1) + j
        y = y + cw_ref[j:j + 1, :] * xext[off:off + rb, :]
    yc[...] = y * _sigmoid(y)

    C = DN_CHUNK
    ri = lax.broadcasted_iota(I32, (C, C), 0)
    ci = lax.broadcasted_iota(I32, (C, C), 1)
    tril = ri >= ci
    strict = ri > ci
    ltri = jnp.where(tril, 1.0, 0.0).astype(F32)
    ones = jnp.ones((C, C), F32)
    eye = jnp.where(ri == ci, 1.0, 0.0).astype(F32)
    lane = lax.broadcasted_iota(I32, (C, LANES), 1)
    neg_a = -jnp.exp(prm_ref[0:1, :])
    dtb = prm_ref[1:2, :]
    nw = nw_ref[...]

    def chunk(c, carry):
        r0 = pl.multiple_of(c * C, C)
        G = g_ref[0, pl.ds(r0, C), :]
        xg = G + dtb
        gv = neg_a * (jnp.maximum(xg, 0.0) + jnp.log1p(jnp.exp(-jnp.abs(xg))))
        be = _sigmoid(G)
        for h in range(DN_HEADS):
            hs = slice(h * DN_D, (h + 1) * DN_D)
            g_col = jnp.sum(jnp.where(lane == DN_HEADS + h, gv, 0.0), axis=1, keepdims=True)
            b_col = jnp.sum(jnp.where(lane == h, be, 0.0), axis=1, keepdims=True)
            gmat = jnp.broadcast_to(g_col, (C, C))
            ccum = jnp.dot(ltri, gmat, preferred_element_type=F32, precision=HI)
            rcum = jnp.dot(ones, jnp.where(ri <= ci, gmat, 0.0),
                           preferred_element_type=F32, precision=HI)
            decay = jnp.exp(jnp.where(tril, ccum - rcum, NEG))
            gc = ccum[:, 0:1]
            gc_last = jnp.sum(g_col, axis=0, keepdims=True)
            q = yc[pl.ds(r0, C), h * DN_D:(h + 1) * DN_D]
            k = yc[pl.ds(r0, C), DN_W + h * DN_D:DN_W + (h + 1) * DN_D]
            v = yc[pl.ds(r0, C), 2 * DN_W + h * DN_D:2 * DN_W + (h + 1) * DN_D]
            q = q * lax.rsqrt(jnp.sum(q * q, axis=-1, keepdims=True) + EPS) * (DN_D ** -0.5)
            k = k * lax.rsqrt(jnp.sum(k * k, axis=-1, keepdims=True) + EPS)
            kb = k * b_col
            vb = v * b_col
            kbf = k.astype(BF16)
            a = jnp.where(strict, _nt(kb.astype(BF16), kbf) * decay, 0.0)
            tinv = eye - a
            pw = a
            for _ in range(5):
                pw = jnp.dot(pw, pw, preferred_element_type=F32, precision=HI)
                tinv = tinv + jnp.dot(tinv, pw, preferred_element_type=F32, precision=HI)
            tb = tinv.astype(BF16)
            u = jnp.dot(tb, vb.astype(BF16), preferred_element_type=F32)
            w = jnp.dot(tb, (kb * jnp.exp(gc)).astype(BF16), preferred_element_type=F32)
            intra = _nt(q.astype(BF16), kbf) * decay
            qd = (q * jnp.exp(gc)).astype(BF16)
            kd = (k * jnp.exp(gc_last - gc)).astype(BF16)
            s_old = st[h]
            sb = s_old.astype(BF16)
            v_new = u - jnp.dot(w.astype(BF16), sb, preferred_element_type=F32)
            vnb = v_new.astype(BF16)
            o = (jnp.dot(qd, sb, preferred_element_type=F32)
                 + jnp.dot(intra.astype(BF16), vnb, preferred_element_type=F32))
            st[h] = s_old * jnp.exp(gc_last) + lax.dot_general(
                kd, vnb, (((0,), (0,)), ((), ())), preferred_element_type=F32)
            o = o * lax.rsqrt(jnp.mean(o * o, axis=-1, keepdims=True) + EPS) * nw
            z = z_ref[0, pl.ds(r0, C), hs].astype(F32)
            o_ref[0, pl.ds(r0, C), hs] = (o * (z * _sigmoid(z))).astype(BF16)
        return carry

    lax.fori_loop(0, rb // C, chunk, 0)


def _gdn(dqkv, dz, gbc, conv_w, A_log, dt_bias, dn_norm_w, B, S):
    rb = 512
    prm = jnp.zeros((2, LANES), F32)
    prm = prm.at[0, DN_HEADS:2 * DN_HEADS].set(A_log.astype(F32))
    prm = prm.at[1, DN_HEADS:2 * DN_HEADS].set(dt_bias.astype(F32))
    out = pl.pallas_call(
        functools.partial(_gdn_kernel, rb=rb),
        grid=(B, S // rb),
        in_specs=[pl.BlockSpec((1, rb, DN_CONV), lambda b, i: (b, i, 0)),
                  pl.BlockSpec((1, rb, DN_W), lambda b, i: (b, i, 0)),
                  pl.BlockSpec((1, rb, LANES), lambda b, i: (b, i, 0)),
                  pl.BlockSpec((CONV_K, DN_CONV), lambda b, i: (0, 0)),
                  pl.BlockSpec((2, LANES), lambda b, i: (0, 0)),
                  pl.BlockSpec((1, DN_D), lambda b, i: (0, 0))],
        out_specs=pl.BlockSpec((1, rb, DN_W), lambda b, i: (b, i, 0)),
        out_shape=jax.ShapeDtypeStruct((B, S, DN_W), BF16),
        scratch_shapes=[pltpu.VMEM((rb + 8, DN_CONV), F32),
                        pltpu.VMEM((rb, DN_CONV), F32),
                        pltpu.VMEM((DN_HEADS, DN_D, DN_D), F32)],
        compiler_params=_cparams(("arbitrary", "arbitrary")),
        name="gdn",
    )(dqkv.reshape(B, S, DN_CONV), dz.reshape(B, S, DN_W), gbc.reshape(B, S, LANES),
      conv_w, prm, dn_norm_w.reshape(1, DN_D))
    return out.reshape(B * S, DN_W)


def _out_kernel(o1_ref, o2_ref, o3_ref, l1_ref, l2_ref, l3_ref, dn_ref, x_ref, mod_ref, wo_ref,
                n2_ref, wr_ref, br_ref,
                x1_ref, h2_ref, te_ref, rk_ref, gtc_ref, cnt_ref, base, *, tm):
    i = pl.program_id(0)

    @pl.when(i == 0)
    def _():
        base[...] = jnp.zeros_like(base)

    l1, l2, l3 = l1_ref[...], l2_ref[...], l3_ref[...]
    mx = jnp.maximum(jnp.maximum(l1, l2), l3)
    e1, e2, e3 = jnp.exp(l1 - mx), jnp.exp(l2 - mx), jnp.exp(l3 - mx)
    zs = e1 + e2 + e3
    er = lax.broadcasted_iota(I32, (LANES, ATTN_W), 0)
    ec = lax.broadcasted_iota(I32, (LANES, ATTN_W), 1)
    expand = jnp.where(ec // ATTN_HD == er, 1.0, 0.0).astype(BF16)
    attn = jnp.zeros((tm, ATTN_W), F32)
    for e, o_ref in ((e1, o1_ref), (e2, o2_ref), (e3, o3_ref)):
        wgt = jnp.dot((e / zs).astype(BF16), expand, preferred_element_type=F32)
        attn = attn + wgt * o_ref[...].astype(F32)
    mix = (jnp.dot(attn.astype(BF16), wo_ref[0:ATTN_W, :], preferred_element_type=F32)
           + jnp.dot(dn_ref[...], wo_ref[ATTN_W:, :], preferred_element_type=F32))
    x1 = x_ref[...] + mod_ref[0, 2:3, :] * mix
    x1_ref[...] = x1
    ms = jnp.mean(x1 * x1, axis=-1, keepdims=True)
    h2 = x1 * lax.rsqrt(ms + EPS) * n2_ref[...]
    h2 = h2 * (1.0 + mod_ref[0, 4:5, :]) + mod_ref[0, 3:4, :]
    h2_ref[...] = h2

    lg = _nt(wr_ref[...], h2, precision=HI) + br_ref[...]
    eidx = lax.broadcasted_iota(I32, (N_EXPERTS, tm), 0)
    vals, idxs, sels = [], [], []
    for _ in range(TOP_K):
        m = jnp.max(lg, axis=0, keepdims=True)
        idx = jnp.min(jnp.where(lg == m, eidx, N_EXPERTS), axis=0, keepdims=True)
        sel = eidx == idx
        vals.append(m)
        idxs.append(idx)
        sels.append(sel)
        lg = jnp.where(sel, -jnp.inf, lg)
    ex = [jnp.exp(v - vals[0]) for v in vals]
    den = ex[0] + ex[1] + ex[2] + ex[3]
    gates = [e / den for e in ex]

    msum = jnp.zeros((N_EXPERTS, tm), F32)
    for sel in sels:
        msum = msum + jnp.where(sel, 1.0, 0.0)
    tr = lax.broadcasted_iota(I32, (tm, tm), 0)
    tc = lax.broadcasted_iota(I32, (tm, tm), 1)
    upper = jnp.where(tr <= tc, 1.0, 0.0).astype(BF16)
    incl = jnp.dot(msum.astype(BF16), upper, preferred_element_type=F32)
    pos = base[:, 0:1] + (incl - msum)
    sub8 = lax.broadcasted_iota(I32, (8, tm), 0)
    te = jnp.zeros((8, tm), I32)
    rk = jnp.zeros((8, tm), I32)
    gt = jnp.zeros((8, tm), F32)
    for k in range(TOP_K):
        rank_k = jnp.sum(jnp.where(sels[k], pos, 0.0), axis=0, keepdims=True).astype(I32)
        te = jnp.where(sub8 == k, idxs[k], te)
        rk = jnp.where(sub8 == k, rank_k, rk)
        gt = jnp.where(sub8 == k, gates[k], gt)
    te_ref[...] = te
    rk_ref[...] = rk
    gtc_ref[...] = jnp.transpose(jnp.concatenate([gt, jnp.zeros((LANES - 8, tm), F32)], axis=0))
    base[...] = base[...] + jnp.sum(msum, axis=1, keepdims=True)
    cnt_ref[...] = base[...].astype(I32)


def _outproj(o1, o2, o3, l1, l2, l3, dn, x2, mod, w_out, norm2_w, w_router, b_router, S):
    T, D = x2.shape
    tm = 512
    row = lambda w: pl.BlockSpec((tm, w), lambda i: (i, 0))
    colb = pl.BlockSpec((8, tm), lambda i: (0, i))
    return pl.pallas_call(
        functools.partial(_out_kernel, tm=tm),
        grid=(T // tm,),
        in_specs=[row(ATTN_W), row(ATTN_W), row(ATTN_W), row(LANES), row(LANES), row(LANES),
                  row(DN_W), row(D),
                  pl.BlockSpec((1, 6, D), lambda i: (i * tm // S, 0, 0)),
                  pl.BlockSpec((D, D), lambda i: (0, 0)),
                  pl.BlockSpec((1, D), lambda i: (0, 0)),
                  pl.BlockSpec((N_EXPERTS, D), lambda i: (0, 0)),
                  pl.BlockSpec((N_EXPERTS, 1), lambda i: (0, 0))],
        out_specs=[row(D), row(D), colb, colb, row(LANES),
                   pl.BlockSpec((N_EXPERTS, LANES), lambda i: (0, 0))],
        out_shape=[jax.ShapeDtypeStruct((T, D), F32),
                   jax.ShapeDtypeStruct((T, D), F32),
                   jax.ShapeDtypeStruct((8, T), I32),
                   jax.ShapeDtypeStruct((8, T), I32),
                   jax.ShapeDtypeStruct((T, LANES), F32),
                   jax.ShapeDtypeStruct((N_EXPERTS, LANES), I32)],
        scratch_shapes=[pltpu.VMEM((N_EXPERTS, LANES), F32)],
        compiler_params=_cparams(("arbitrary",)),
        name="outproj_router",
    )(o1, o2, o3, l1, l2, l3, dn, x2, mod, w_out.astype(BF16), norm2_w.reshape(1, D),
      jnp.transpose(w_router), b_router.reshape(N_EXPERTS, 1))


def _dest_kernel(ps_ref, te_ref, rk_ref, d_ref):
    te = te_ref[...]
    acc = jnp.zeros(te.shape, I32)
    for e in range(N_EXPERTS):
        acc = jnp.where(te == e, ps_ref[e], acc)
    d_ref[...] = acc + rk_ref[...]


def _dest(pstart, te, rk):
    T = te.shape[1]
    tb = 2048
    return pl.pallas_call(
        _dest_kernel,
        grid_spec=pltpu.PrefetchScalarGridSpec(
            num_scalar_prefetch=1,
            grid=(T // tb,),
            in_specs=[pl.BlockSpec((8, tb), lambda i, ps: (0, i)),
                      pl.BlockSpec((8, tb), lambda i, ps: (0, i))],
            out_specs=pl.BlockSpec((8, tb), lambda i, ps: (0, i))),
        out_shape=jax.ShapeDtypeStruct((8, T), I32),
        compiler_params=_cparams(("arbitrary",)),
        name="dest_rows",
    )(pstart, te, rk)


def _row_copy(src, dst, sem):
    return pltpu.make_async_copy(src, dst, sem)


def _disp_kernel(dest_ref, h2_ref, xs_in, xs_ref, sem, *, tmd):
    del xs_in

    def issue(t, carry):
        for k in range(TOP_K):
            d = dest_ref[k, t]
            _row_copy(h2_ref.at[pl.ds(t, 1)], xs_ref.at[pl.ds(d, 1)], sem).start()
        return carry

    lax.fori_loop(0, tmd, issue, 0)
    for _ in range(TOP_K):
        _row_copy(h2_ref, xs_ref.at[pl.ds(0, tmd)], sem).wait()


def _dispatch(dest, h2, P):
    T, D = h2.shape
    tmd = 256
    xs0 = jnp.zeros((P, D), F32)
    return pl.pallas_call(
        functools.partial(_disp_kernel, tmd=tmd),
        grid=(T // tmd,),
        in_specs=[pl.BlockSpec((8, tmd), lambda i: (0, i), memory_space=pltpu.SMEM),
                  pl.BlockSpec((tmd, D), lambda i: (i, 0)),
                  pl.BlockSpec(memory_space=pl.ANY)],
        out_specs=pl.BlockSpec(memory_space=pl.ANY),
        out_shape=jax.ShapeDtypeStruct((P, D), F32),
        scratch_shapes=[pltpu.SemaphoreType.DMA(())],
        input_output_aliases={2: 0},
        compiler_params=_cparams(("arbitrary",)),
        name="dispatch",
    )(dest, h2, xs0)


def _moe_kernel(be_ref, xs_ref, w1_ref, b1_ref, w2_ref, b2_ref, yb_ref, *, F):
    del be_ref
    x = xs_ref[...].astype(BF16)
    hgu = jnp.dot(x, w1_ref[0], preferred_element_type=F32) + b1_ref[0]
    gate = jnp.minimum(hgu[:, :F], SWIGLU_LIMIT)
    up = jnp.clip(hgu[:, F:], -SWIGLU_LIMIT, SWIGLU_LIMIT)
    act = gate * _sigmoid(SWIGLU_ALPHA * gate) * (up + 1.0)
    yb_ref[...] = jnp.dot(act.astype(BF16), w2_ref[0], preferred_element_type=F32) + b2_ref[0]


def _experts(blk_exp, xs, w1, b1, w2, b2, tme):
    P, D = xs.shape
    E, _, F2 = w1.shape
    F = F2 // 2
    return pl.pallas_call(
        functools.partial(_moe_kernel, F=F),
        grid_spec=pltpu.PrefetchScalarGridSpec(
            num_scalar_prefetch=1,
            grid=(P // tme,),
            in_specs=[pl.BlockSpec((tme, D), lambda i, be: (i, 0)),
                      pl.BlockSpec((1, D, F2), lambda i, be: (be[i], 0, 0)),
                      pl.BlockSpec((1, 1, F2), lambda i, be: (be[i], 0, 0)),
                      pl.BlockSpec((1, F, D), lambda i, be: (be[i], 0, 0)),
                      pl.BlockSpec((1, 1, D), lambda i, be: (be[i], 0, 0))],
            out_specs=pl.BlockSpec((tme, D), lambda i, be: (i, 0))),
        out_shape=jax.ShapeDtypeStruct((P, D), F32),
        compiler_params=_cparams(("arbitrary",)),
        name="experts",
    )(blk_exp, xs, w1.astype(BF16), b1.reshape(E, 1, F2), w2.astype(BF16), b2.reshape(E, 1, D))


def _comb_kernel(dest_ref, g_ref, x1_ref, mod_ref, fw_ref, yb_ref, o_ref, buf, sem, *, tmc):
    def issue(t, carry):
        for k in range(TOP_K):
            d = dest_ref[k, t]
            _row_copy(yb_ref.at[pl.ds(d, 1)], buf.at[k, pl.ds(t, 1)], sem).start()
        return carry

    lax.fori_loop(0, tmc, issue, 0)
    for k in range(TOP_K):
        _row_copy(yb_ref.at[pl.ds(0, tmc)], buf.at[k], sem).wait()
    g = g_ref[...]
    y = g[:, 0:1] * buf[0]
    for k in range(1, TOP_K):
        y = y + g[:, k:k + 1] * buf[k]
    x2 = x1_ref[...] + mod_ref[0, 5:6, :] * y
    ms = jnp.mean(x2 * x2, axis=-1, keepdims=True)
    o_ref[...] = x2 * lax.rsqrt(ms + EPS) * fw_ref[...]


def _combine(dest, gtc, x1, mod, final_w, yb, S):
    T, D = x1.shape
    tmc = 256
    return pl.pallas_call(
        functools.partial(_comb_kernel, tmc=tmc),
        grid=(T // tmc,),
        in_specs=[pl.BlockSpec((8, tmc), lambda i: (0, i), memory_space=pltpu.SMEM),
                  pl.BlockSpec((tmc, LANES), lambda i: (i, 0)),
                  pl.BlockSpec((tmc, D), lambda i: (i, 0)),
                  pl.BlockSpec((1, 6, D), lambda i: (i * tmc // S, 0, 0)),
                  pl.BlockSpec((1, D), lambda i: (0, 0)),
                  pl.BlockSpec(memory_space=pl.ANY)],
        out_specs=pl.BlockSpec((tmc, D), lambda i: (i, 0)),
        out_shape=jax.ShapeDtypeStruct((T, D), F32),
        scratch_shapes=[pltpu.VMEM((TOP_K, tmc, D), F32), pltpu.SemaphoreType.DMA(())],
        compiler_params=_cparams(("arbitrary",)),
        name="combine",
    )(dest, gtc, x1, mod, final_w.reshape(1, D), yb)


def _layer(x2, mod, B, S, norm1_w, w_in, conv_w, A_log, dt_bias, dn_norm_w, w_out,
           norm2_w, w_router, b_router, w1, b1, w2, b2):
    T, D = x2.shape
    qkv, dqkv, dz, gbc = _inproj(x2, mod, norm1_w, w_in, S)
    branches = [_attn_branch(qkv, B, S, d) for d in DILATIONS]
    dn = _gdn(dqkv, dz, gbc, conv_w, A_log, dt_bias, dn_norm_w, B, S)
    (o1, l1), (o2, l2), (o3, l3) = branches
    x1, h2, te, rk, gtc, cnt = _outproj(o1, o2, o3, l1, l2, l3, dn, x2, mod, w_out, norm2_w,
                                        w_router, b_router, S)
    tme = 256
    P = T * TOP_K + N_EXPERTS * tme
    counts = cnt[:, 0]
    padded = (counts + tme - 1) // tme * tme
    pend = jnp.cumsum(padded)
    pstart = (pend - padded).astype(I32)
    blk_exp = jnp.minimum(
        jnp.searchsorted(pend, jnp.arange(P // tme, dtype=I32) * tme, side="right"),
        N_EXPERTS - 1).astype(I32)
    dest = _dest(pstart, te, rk)
    xs = _dispatch(dest, h2, P)
    yb = _experts(blk_exp, xs, w1, b1, w2, b2, tme)
    return x1, dest, gtc, yb


def kernel(x, c, w_ada, b_ada, norm1_w, w_in, conv_w, A_log, dt_bias, dn_norm_w, w_out, norm2_w,
           w_router, b_router, w1, b1, w2, b2, final_norm_w):
    B, S, D = x.shape
    depth = w_ada.shape[0]
    assert S % (ATTN_BLK * DILATIONS[-1]) == 0 and depth == 1
    x2 = x.reshape(B * S, D)
    mod = _ada(c, w_ada[0], b_ada[0])
    x1, dest, gtc, yb = _layer(x2, mod, B, S, norm1_w[0], w_in[0], conv_w[0], A_log[0], dt_bias[0],
                               dn_norm_w[0], w_out[0], norm2_w[0], w_router[0], b_router[0],
                               w1[0], b1[0], w2[0], b2[0])
    out = _combine(dest, gtc, x1, mod, final_norm_w, yb, S)
    return out.reshape(B, S, D)
```

```python
import functools

import jax
import jax.numpy as jnp
from jax import lax
from jax.experimental import pallas as pl
from jax.experimental.pallas import tpu as pltpu

F32 = jnp.float32
BF16 = jnp.bfloat16
I32 = jnp.int32
HI = lax.Precision.HIGHEST

LANES = 128
ATTN_HEADS = 8
ATTN_HD = 64
ATTN_W = ATTN_HEADS * ATTN_HD
ATTN_BLK = 128
DILATIONS = (1, 4, 16)
DN_HEADS = 4
DN_D = 128
DN_W = DN_HEADS * DN_D
DN_CONV = 3 * DN_W
CONV_K = 4
DN_CHUNK = 64
N_EXPERTS = 32
TOP_K = 4
SWIGLU_LIMIT = 7.0
SWIGLU_ALPHA = 1.702
EPS = 1e-6
NEG = -1e30
MAIN_COLS = 3 * ATTN_W + DN_CONV + DN_W

VMEM_LIMIT = 56 * 1024 * 1024


def _cparams(sem):
    return pltpu.CompilerParams(dimension_semantics=sem, vmem_limit_bytes=VMEM_LIMIT)


def _nt(a, b, **kw):
    return lax.dot_general(a, b, (((1,), (1,)), ((), ())), preferred_element_type=F32, **kw)


def _sigmoid(x):
    return 1.0 / (1.0 + jnp.exp(-x))


def _ada_kernel(c_ref, w_ref, b_ref, o_ref):
    c = c_ref[...]
    cond = c * _sigmoid(c)
    o_ref[...] = jnp.dot(cond, w_ref[...], preferred_element_type=F32, precision=HI) + b_ref[...]


def _ada(c, w_ada, b_ada):
    B, D = c.shape
    N = w_ada.shape[1]
    cp = jnp.zeros((8, D), F32).at[:B].set(c)
    tn = 1024
    out = pl.pallas_call(
        _ada_kernel,
        grid=(N // tn,),
        in_specs=[pl.BlockSpec((8, D), lambda j: (0, 0)),
                  pl.BlockSpec((D, tn), lambda j: (0, j)),
                  pl.BlockSpec((1, tn), lambda j: (0, j))],
        out_specs=pl.BlockSpec((8, tn), lambda j: (0, j)),
        out_shape=jax.ShapeDtypeStruct((8, N), F32),
        compiler_params=_cparams(("arbitrary",)),
        name="ada",
    )(cp, w_ada, b_ada.reshape(1, N))
    return out[:B].reshape(B, 6, D)


def _inproj_kernel(x_ref, mod_ref, nw_ref, wm_ref, ws_ref, qkv_ref, dqkv_ref, dz_ref, gbc_ref):
    x = x_ref[...]
    shift = mod_ref[0, 0:1, :]
    scale = mod_ref[0, 1:2, :]
    ms = jnp.mean(x * x, axis=-1, keepdims=True)
    h = x * lax.rsqrt(ms + EPS) * nw_ref[...]
    hb = (h * (1.0 + scale) + shift).astype(BF16)
    for j in range(3):
        r = jnp.dot(hb, wm_ref[:, j * ATTN_W:(j + 1) * ATTN_W], preferred_element_type=F32)
        if j == 0:
            r = r * (ATTN_HD ** -0.5)
        qkv_ref[:, j * ATTN_W:(j + 1) * ATTN_W] = r.astype(BF16)
    for j in range(3):
        c0 = 3 * ATTN_W + j * DN_W
        r = jnp.dot(hb, wm_ref[:, c0:c0 + DN_W], preferred_element_type=F32)
        dqkv_ref[:, j * DN_W:(j + 1) * DN_W] = r.astype(BF16)
    c0 = 3 * ATTN_W + DN_CONV
    dz_ref[...] = jnp.dot(hb, wm_ref[:, c0:c0 + DN_W], preferred_element_type=F32).astype(BF16)
    gbc_ref[...] = jnp.dot(hb, ws_ref[...], preferred_element_type=F32)


def _inproj(x2, mod, norm_w, w_in, S):
    T, D = x2.shape
    tm = 512
    wm = w_in[:, :MAIN_COLS].astype(BF16)
    ws = jnp.zeros((D, LANES), F32).at[:, :2 * DN_HEADS].set(w_in[:, MAIN_COLS:]).astype(BF16)
    return pl.pallas_call(
        _inproj_kernel,
        grid=(T // tm,),
        in_specs=[pl.BlockSpec((tm, D), lambda i: (i, 0)),
                  pl.BlockSpec((1, 6, D), lambda i: (i * tm // S, 0, 0)),
                  pl.BlockSpec((1, D), lambda i: (0, 0)),
                  pl.BlockSpec((D, MAIN_COLS), lambda i: (0, 0)),
                  pl.BlockSpec((D, LANES), lambda i: (0, 0))],
        out_specs=[pl.BlockSpec((tm, 3 * ATTN_W), lambda i: (i, 0)),
                   pl.BlockSpec((tm, DN_CONV), lambda i: (i, 0)),
                   pl.BlockSpec((tm, DN_W), lambda i: (i, 0)),
                   pl.BlockSpec((tm, LANES), lambda i: (i, 0))],
        out_shape=[jax.ShapeDtypeStruct((T, 3 * ATTN_W), BF16),
                   jax.ShapeDtypeStruct((T, DN_CONV), BF16),
                   jax.ShapeDtypeStruct((T, DN_W), BF16),
                   jax.ShapeDtypeStruct((T, LANES), F32)],
        compiler_params=_cparams(("arbitrary",)),
        name="inproj",
    )(x2, mod, norm_w.reshape(1, D), wm, ws)


def _attn_kernel(q_ref, kc_ref, kp_ref, vc_ref, vp_ref, o_ref, lse_ref, kf, vf, *, qb):
    n = pl.program_id(2)
    kf[0:ATTN_BLK, :] = kp_ref[0]
    kf[ATTN_BLK:, :] = kc_ref[0]
    vf[0:ATTN_BLK, :] = vp_ref[0]
    vf[ATTN_BLK:, :] = vc_ref[0]
    row = lax.broadcasted_iota(I32, (ATTN_BLK, 2 * ATTN_BLK), 0)
    col = lax.broadcasted_iota(I32, (ATTN_BLK, 2 * ATTN_BLK), 1)
    band = jnp.logical_or(jnp.logical_and(col < ATTN_BLK, col >= row),
                          jnp.logical_and(col >= ATTN_BLK, col - ATTN_BLK <= row))
    lane = lax.broadcasted_iota(I32, (ATTN_BLK, LANES), 1)
    lo = lane < ATTN_HD

    def sub(j, carry):
        r0 = pl.multiple_of(j * ATTN_BLK, ATTN_BLK)
        first_col = jnp.where(jnp.logical_and(n == 0, j == 0), ATTN_BLK, 0)
        mask = jnp.logical_and(band, col >= first_col)
        lse_tile = jnp.zeros((ATTN_BLK, LANES), F32)
        for hp in range(ATTN_W // LANES):
            cs = slice(hp * LANES, (hp + 1) * LANES)
            q2 = q_ref[0, pl.ds(r0, ATTN_BLK), cs]
            k2 = kf[pl.ds(r0, 2 * ATTN_BLK), cs]
            v2 = vf[pl.ds(r0, 2 * ATTN_BLK), cs]
            outs = []
            for half in range(2):
                qm = jnp.where(lo if half == 0 else jnp.logical_not(lo), q2, jnp.zeros_like(q2))
                s = _nt(qm, k2)
                s = jnp.where(mask, s, NEG)
                m = jnp.max(s, axis=-1, keepdims=True)
                p = jnp.exp(s - m)
                den = jnp.sum(p, axis=-1, keepdims=True)
                acc = jnp.dot(p.astype(BF16), v2, preferred_element_type=F32)
                outs.append(acc / den)
                lse_tile = jnp.where(lane == 2 * hp + half, m + jnp.log(den), lse_tile)
            o_ref[0, pl.ds(r0, ATTN_BLK), cs] = jnp.where(lo, outs[0], outs[1]).astype(BF16)
        lse_ref[0, pl.ds(r0, ATTN_BLK), :] = lse_tile
        return carry

    lax.fori_loop(0, qb // ATTN_BLK, sub, 0)


def _attn_branch(qkv, B, S, d):
    L = S // d
    qb = min(512, L)
    nsub = qb // ATTN_BLK
    W3 = 3 * ATTN_W
    view = qkv.reshape(B, L, d * W3)
    cur = lambda c: pl.BlockSpec((1, qb, ATTN_W), lambda b, r, n: (b, n, 3 * r + c))
    prev = lambda c: pl.BlockSpec((1, ATTN_BLK, ATTN_W),
                                  lambda b, r, n: (b, jnp.maximum(n * nsub - 1, 0), 3 * r + c))
    o, lse = pl.pallas_call(
        functools.partial(_attn_kernel, qb=qb),
        grid=(B, d, L // qb),
        in_specs=[cur(0), cur(1), prev(1), cur(2), prev(2)],
        out_specs=[pl.BlockSpec((1, qb, ATTN_W), lambda b, r, n: (b, n, r)),
                   pl.BlockSpec((1, qb, LANES), lambda b, r, n: (b, n, r))],
        out_shape=[jax.ShapeDtypeStruct((B, L, d * ATTN_W), BF16),
                   jax.ShapeDtypeStruct((B, L, d * LANES), F32)],
        scratch_shapes=[pltpu.VMEM((qb + ATTN_BLK, ATTN_W), BF16),
                        pltpu.VMEM((qb + ATTN_BLK, ATTN_W), BF16)],
        compiler_params=_cparams(("arbitrary", "arbitrary", "arbitrary")),
        name=f"attn_d{d}",
    )(view, view, view, view, view)
    return o.reshape(B * S, ATTN_W), lse.reshape(B * S, LANES)


def _gdn_kernel(x_ref, z_ref, g_ref, cw_ref, prm_ref, nw_ref, o_ref, xext, yc, s0, s1, *, rb):
    i = pl.program_id(1)

    @pl.when(i == 0)
    def _():
        xext[0:8, :] = jnp.zeros((8, DN_CONV), F32)
        s0[...] = jnp.zeros_like(s0)
        s1[...] = jnp.zeros_like(s1)

    @pl.when(i > 0)
    def _():
        xext[0:8, :] = xext[rb:rb + 8, :]

    xext[8:, :] = x_ref[0].astype(F32)
    y = cw_ref[CONV_K - 1:CONV_K, :] * xext[8:8 + rb, :]
    for j in range(CONV_K - 1):
        off = 8 - (CONV_K - 1) + j
        y = y + cw_ref[j:j + 1, :] * xext[off:off + rb, :]
    yc[...] = y * _sigmoid(y)

    C = DN_CHUNK
    H = DN_HEADS
    CW = H * C
    dot = functools.partial(jnp.dot, preferred_element_type=F32)

    def iota(shape, d):
        return lax.broadcasted_iota(I32, shape, d)

    ltri_b = jnp.where(iota((C, C), 0) >= iota((C, C), 1), 1.0, 0.0).astype(BF16)
    lane = iota((C, LANES), 1)
    blane = lane < H
    glane = jnp.logical_and(lane >= H, lane < 2 * H)
    e512 = jnp.where(jnp.logical_or(iota((LANES, DN_W), 1) // DN_D == iota((LANES, DN_W), 0),
                                    iota((LANES, DN_W), 1) // DN_D == iota((LANES, DN_W), 0) - H),
                     1.0, 0.0).astype(BF16)
    e256 = jnp.where(iota((LANES, CW), 1) // C == iota((LANES, CW), 0) - H, 1.0, 0.0).astype(BF16)
    row4 = iota((C, CW), 0)
    col4 = iota((C, CW), 1) % C
    eye4 = jnp.where(row4 == col4, 1.0, 0.0).astype(F32)
    blk = [iota((C, CW), 1) // C == h for h in range(H)]
    bd_cc = iota((CW, CW), 0) // C == iota((CW, CW), 1) // C
    bd_pair = iota((CW, CW), 0) // DN_D == iota((CW, CW), 1) // DN_D
    rt_mask = iota((CW, DN_W), 0) // C == iota((CW, DN_W), 1) // DN_D
    neg_a = -jnp.exp(prm_ref[0:1, :])
    dtb = prm_ref[1:2, :]
    nw = nw_ref[...]

    def hilo(x):
        hi = x.astype(BF16)
        return hi, (x - hi.astype(F32)).astype(BF16)

    def heads(a, w):
        return [a[:, h * w:(h + 1) * w] for h in range(H)]

    def l2n(a, mult):
        return jnp.concatenate(
            [p * (lax.rsqrt(jnp.sum(p * p, axis=-1, keepdims=True) + EPS) * mult) for p in heads(a, DN_D)],
            axis=1)

    def stack4(a):
        return jnp.concatenate([a, a, a, a], axis=0)

    for c in range(rb // C):
        rs = slice(c * C, (c + 1) * C)
        G = g_ref[0, rs, :]
        xg = G + dtb
        gv = jnp.where(glane, neg_a * (jnp.maximum(xg, 0.0) + jnp.log1p(jnp.exp(-jnp.abs(xg)))), 0.0)
        be = jnp.where(blane, _sigmoid(G), 0.0)
        g_hi, g_lo = hilo(gv)
        gcum = dot(ltri_b, g_hi) + dot(ltri_b, g_lo)
        gtot = gcum[C - 1:C, :]
        eg = jnp.where(glane, jnp.exp(gcum), 0.0)
        ek = jnp.where(glane, jnp.exp(gtot - gcum), 0.0)
        ex = dot(jnp.concatenate([be, eg, ek], axis=0).astype(BF16), e512)
        bexp, egexp, ekexp = ex[0:C], ex[C:2 * C], ex[2 * C:3 * C]
        gexp = dot(g_hi, e256) + dot(g_lo, e256)
        d_hi, d_lo = hilo(jnp.where(row4 > col4, gexp, 0.0))
        diff = dot(ltri_b, d_hi) + dot(ltri_b, d_lo)
        decay = jnp.exp(jnp.where(row4 >= col4, diff, NEG))

        q4 = l2n(yc[rs, 0:DN_W], DN_D ** -0.5)
        k4 = l2n(yc[rs, DN_W:2 * DN_W], 1.0)
        v4 = yc[rs, 2 * DN_W:3 * DN_W]
        kb4 = k4 * bexp
        vb4 = v4 * bexp
        rt = jnp.where(rt_mask, stack4(k4.astype(BF16)), jnp.zeros((), BF16))
        ai = _nt(jnp.concatenate([kb4, q4], axis=0).astype(BF16), rt)
        a4 = jnp.where(row4 > col4, ai[0:C] * decay, 0.0)
        intra = ai[C:2 * C] * decay
        t4 = eye4 - a4
        pb = a4.astype(BF16)
        bd = jnp.where(bd_cc, stack4(pb), jnp.zeros((), BF16))
        for _ in range(5):
            pb = dot(pb, bd).astype(BF16)
            bd = jnp.where(bd_cc, stack4(pb), jnp.zeros((), BF16))
            t4 = t4 + dot(t4.astype(BF16), bd)
        t4b = t4.astype(BF16)
        zb = jnp.zeros((), BF16)
        lstk = jnp.concatenate([jnp.where(blk[h], t4b, zb) for h in range(H)], axis=0)
        kbg4 = kb4 * egexp
        rstk = jnp.concatenate(
            [jnp.concatenate([vb, kbg], axis=1) for vb, kbg in zip(heads(vb4, DN_D), heads(kbg4, DN_D))],
            axis=0).astype(BF16)
        uw = dot(lstk, rstk)
        u4 = jnp.concatenate([uw[h * C:(h + 1) * C, 0:DN_D] for h in range(H)], axis=1)
        w4 = jnp.concatenate([uw[h * C:(h + 1) * C, DN_D:2 * DN_D] for h in range(H)], axis=1)
        qd4 = q4 * egexp
        kd4 = (k4 * ekexp).astype(BF16)
        ib = intra.astype(BF16)
        lint = jnp.concatenate([jnp.where(blk[h], ib, zb) for h in range(H)], axis=0)
        gl4 = egexp[C - 1:C, :]

        wq = jnp.concatenate([w4, qd4], axis=0).astype(BF16)
        ra = dot(wq[:, 0:CW], s0[...].astype(BF16))
        rc = dot(wq[:, CW:2 * CW], s1[...].astype(BF16))
        vn = u4 - jnp.concatenate([ra[0:C], rc[0:C]], axis=1)
        vnb = vn.astype(BF16)
        oi = dot(lint, jnp.concatenate(heads(vnb, DN_D), axis=0))
        o = (jnp.concatenate([ra[C:2 * C], rc[C:2 * C]], axis=1)
             + jnp.concatenate([oi[h * C:(h + 1) * C] for h in range(H)], axis=1))
        tn = (((0,), (0,)), ((), ()))
        s0[...] = s0[...] * gl4[:, 0:CW] + jnp.where(
            bd_pair, lax.dot_general(kd4[:, 0:CW], vnb[:, 0:CW], tn, preferred_element_type=F32), 0.0)
        s1[...] = s1[...] * gl4[:, CW:2 * CW] + jnp.where(
            bd_pair, lax.dot_general(kd4[:, CW:2 * CW], vnb[:, CW:2 * CW], tn, preferred_element_type=F32), 0.0)

        z = z_ref[0, rs, :].astype(F32)
        on = jnp.concatenate(
            [p * lax.rsqrt(jnp.mean(p * p, axis=-1, keepdims=True) + EPS) * nw for p in heads(o, DN_D)], axis=1)
        o_ref[0, rs, :] = (on * (z * _sigmoid(z))).astype(BF16)


def _gdn(dqkv, dz, gbc, conv_w, A_log, dt_bias, dn_norm_w, B, S):
    rb = 256
    prm = jnp.zeros((2, LANES), F32)
    prm = prm.at[0, DN_HEADS:2 * DN_HEADS].set(A_log.astype(F32))
    prm = prm.at[1, DN_HEADS:2 * DN_HEADS].set(dt_bias.astype(F32))
    out = pl.pallas_call(
        functools.partial(_gdn_kernel, rb=rb),
        grid=(B, S // rb),
        in_specs=[pl.BlockSpec((1, rb, DN_CONV), lambda b, i: (b, i, 0)),
                  pl.BlockSpec((1, rb, DN_W), lambda b, i: (b, i, 0)),
                  pl.BlockSpec((1, rb, LANES), lambda b, i: (b, i, 0)),
                  pl.BlockSpec((CONV_K, DN_CONV), lambda b, i: (0, 0)),
                  pl.BlockSpec((2, LANES), lambda b, i: (0, 0)),
                  pl.BlockSpec((1, DN_D), lambda b, i: (0, 0))],
        out_specs=pl.BlockSpec((1, rb, DN_W), lambda b, i: (b, i, 0)),
        out_shape=jax.ShapeDtypeStruct((B, S, DN_W), BF16),
        scratch_shapes=[pltpu.VMEM((rb + 8, DN_CONV), F32),
                        pltpu.VMEM((rb, DN_CONV), F32),
                        pltpu.VMEM((2 * DN_D, 2 * DN_D), F32),
                        pltpu.VMEM((2 * DN_D, 2 * DN_D), F32)],
        compiler_params=_cparams(("arbitrary", "arbitrary")),
        name="gdn",
    )(dqkv.reshape(B, S, DN_CONV), dz.reshape(B, S, DN_W), gbc.reshape(B, S, LANES),
      conv_w, prm, dn_norm_w.reshape(1, DN_D))
    return out.reshape(B * S, DN_W)


def _out_kernel(o1_ref, o2_ref, o3_ref, l1_ref, l2_ref, l3_ref, dn_ref, x_ref, mod_ref, wo_ref,
                n2_ref, wr_ref, br_ref,
                x1_ref, h2_ref, te_ref, rk_ref, gtc_ref, cnt_ref, base, *, tm):
    i = pl.program_id(0)

    @pl.when(i == 0)
    def _():
        base[...] = jnp.zeros_like(base)

    l1, l2, l3 = l1_ref[...], l2_ref[...], l3_ref[...]
    mx = jnp.maximum(jnp.maximum(l1, l2), l3)
    e1, e2, e3 = jnp.exp(l1 - mx), jnp.exp(l2 - mx), jnp.exp(l3 - mx)
    zs = e1 + e2 + e3
    er = lax.broadcasted_iota(I32, (LANES, ATTN_W), 0)
    ec = lax.broadcasted_iota(I32, (LANES, ATTN_W), 1)
    expand = jnp.where(ec // ATTN_HD == er, 1.0, 0.0).astype(BF16)
    attn = jnp.zeros((tm, ATTN_W), F32)
    for e, o_ref in ((e1, o1_ref), (e2, o2_ref), (e3, o3_ref)):
        wgt = jnp.dot((e / zs).astype(BF16), expand, preferred_element_type=F32)
        attn = attn + wgt * o_ref[...].astype(F32)
    mix = (jnp.dot(attn.astype(BF16), wo_ref[0:ATTN_W, :], preferred_element_type=F32)
           + jnp.dot(dn_ref[...], wo_ref[ATTN_W:, :], preferred_element_type=F32))
    x1 = x_ref[...] + mod_ref[0, 2:3, :] * mix
    x1_ref[...] = x1
    ms = jnp.mean(x1 * x1, axis=-1, keepdims=True)
    h2 = x1 * lax.rsqrt(ms + EPS) * n2_ref[...]
    h2 = h2 * (1.0 + mod_ref[0, 4:5, :]) + mod_ref[0, 3:4, :]
    h2_ref[...] = h2

    lg = _nt(wr_ref[...], h2, precision=HI) + br_ref[...]
    eidx = lax.broadcasted_iota(I32, (N_EXPERTS, tm), 0)
    vals, idxs, sels = [], [], []
    for _ in range(TOP_K):
        m = jnp.max(lg, axis=0, keepdims=True)
        idx = jnp.min(jnp.where(lg == m, eidx, N_EXPERTS), axis=0, keepdims=True)
        sel = eidx == idx
        vals.append(m)
        idxs.append(idx)
        sels.append(sel)
        lg = jnp.where(sel, -jnp.inf, lg)
    ex = [jnp.exp(v - vals[0]) for v in vals]
    den = ex[0] + ex[1] + ex[2] + ex[3]
    gates = [e / den for e in ex]

    msum = jnp.zeros((N_EXPERTS, tm), F32)
    for sel in sels:
        msum = msum + jnp.where(sel, 1.0, 0.0)
    tr = lax.broadcasted_iota(I32, (tm, tm), 0)
    tc = lax.broadcasted_iota(I32, (tm, tm), 1)
    upper = jnp.where(tr <= tc, 1.0, 0.0).astype(BF16)
    incl = jnp.dot(msum.astype(BF16), upper, preferred_element_type=F32)
    pos = base[:, 0:1] + (incl - msum)
    sub8 = lax.broadcasted_iota(I32, (8, tm), 0)
    te = jnp.zeros((8, tm), I32)
    rk = jnp.zeros((8, tm), I32)
    gt = jnp.zeros((8, tm), F32)
    for k in range(TOP_K):
        rank_k = jnp.sum(jnp.where(sels[k], pos, 0.0), axis=0, keepdims=True).astype(I32)
        te = jnp.where(sub8 == k, idxs[k], te)
        rk = jnp.where(sub8 == k, rank_k, rk)
        gt = jnp.where(sub8 == k, gates[k], gt)
    te_ref[...] = te
    rk_ref[...] = rk
    gtc_ref[...] = jnp.transpose(jnp.concatenate([gt, jnp.zeros((LANES - 8, tm), F32)], axis=0))
    base[...] = base[...] + jnp.sum(msum, axis=1, keepdims=True)
    cnt_ref[...] = base[...].astype(I32)


def _outproj(o1, o2, o3, l1, l2, l3, dn, x2, mod, w_out, norm2_w, w_router, b_router, S):
    T, D = x2.shape
    tm = 512
    row = lambda w: pl.BlockSpec((tm, w), lambda i: (i, 0))
    colb = pl.BlockSpec((8, tm), lambda i: (0, i))
    return pl.pallas_call(
        functools.partial(_out_kernel, tm=tm),
        grid=(T // tm,),
        in_specs=[row(ATTN_W), row(ATTN_W), row(ATTN_W), row(LANES), row(LANES), row(LANES),
                  row(DN_W), row(D),
                  pl.BlockSpec((1, 6, D), lambda i: (i * tm // S, 0, 0)),
                  pl.BlockSpec((D, D), lambda i: (0, 0)),
                  pl.BlockSpec((1, D), lambda i: (0, 0)),
                  pl.BlockSpec((N_EXPERTS, D), lambda i: (0, 0)),
                  pl.BlockSpec((N_EXPERTS, 1), lambda i: (0, 0))],
        out_specs=[row(D), row(D), colb, colb, row(LANES),
                   pl.BlockSpec((N_EXPERTS, LANES), lambda i: (0, 0))],
        out_shape=[jax.ShapeDtypeStruct((T, D), F32),
                   jax.ShapeDtypeStruct((T, D), F32),
                   jax.ShapeDtypeStruct((8, T), I32),
                   jax.ShapeDtypeStruct((8, T), I32),
                   jax.ShapeDtypeStruct((T, LANES), F32),
                   jax.ShapeDtypeStruct((N_EXPERTS, LANES), I32)],
        scratch_shapes=[pltpu.VMEM((N_EXPERTS, LANES), F32)],
        compiler_params=_cparams(("arbitrary",)),
        name="outproj_router",
    )(o1, o2, o3, l1, l2, l3, dn, x2, mod, w_out.astype(BF16), norm2_w.reshape(1, D),
      jnp.transpose(w_router), b_router.reshape(N_EXPERTS, 1))


def _dest_kernel(ps_ref, te_ref, rk_ref, d_ref):
    te = te_ref[...]
    acc = jnp.zeros(te.shape, I32)
    for e in range(N_EXPERTS):
        acc = jnp.where(te == e, ps_ref[e], acc)
    d_ref[...] = acc + rk_ref[...]


def _dest(pstart, te, rk):
    T = te.shape[1]
    tb = 2048
    return pl.pallas_call(
        _dest_kernel,
        grid_spec=pltpu.PrefetchScalarGridSpec(
            num_scalar_prefetch=1,
            grid=(T // tb,),
            in_specs=[pl.BlockSpec((8, tb), lambda i, ps: (0, i)),
                      pl.BlockSpec((8, tb), lambda i, ps: (0, i))],
            out_specs=pl.BlockSpec((8, tb), lambda i, ps: (0, i))),
        out_shape=jax.ShapeDtypeStruct((8, T), I32),
        compiler_params=_cparams(("arbitrary",)),
        name="dest_rows",
    )(pstart, te, rk)


def _row_copy(src, dst, sem):
    return pltpu.make_async_copy(src, dst, sem)


def _disp_kernel(dest_ref, h2_ref, xs_in, xs_ref, sem, *, tmd):
    del xs_in

    def issue(t, carry):
        for k in range(TOP_K):
            d = dest_ref[k, t]
            _row_copy(h2_ref.at[pl.ds(t, 1)], xs_ref.at[pl.ds(d, 1)], sem).start()
        return carry

    lax.fori_loop(0, tmd, issue, 0)
    for _ in range(TOP_K):
        _row_copy(h2_ref, xs_ref.at[pl.ds(0, tmd)], sem).wait()


def _dispatch(dest, h2, P):
    T, D = h2.shape
    tmd = 256
    xs0 = jnp.zeros((P, D), F32)
    return pl.pallas_call(
        functools.partial(_disp_kernel, tmd=tmd),
        grid=(T // tmd,),
        in_specs=[pl.BlockSpec((8, tmd), lambda i: (0, i), memory_space=pltpu.SMEM),
                  pl.BlockSpec((tmd, D), lambda i: (i, 0)),
                  pl.BlockSpec(memory_space=pl.ANY)],
        out_specs=pl.BlockSpec(memory_space=pl.ANY),
        out_shape=jax.ShapeDtypeStruct((P, D), F32),
        scratch_shapes=[pltpu.SemaphoreType.DMA(())],
        input_output_aliases={2: 0},
        compiler_params=_cparams(("arbitrary",)),
        name="dispatch",
    )(dest, h2, xs0)


def _moe_kernel(be_ref, xs_ref, w1_ref, b1_ref, w2_ref, b2_ref, yb_ref, *, F):
    del be_ref
    x = xs_ref[...].astype(BF16)
    hgu = jnp.dot(x, w1_ref[0], preferred_element_type=F32) + b1_ref[0]
    gate = jnp.minimum(hgu[:, :F], SWIGLU_LIMIT)
    up = jnp.clip(hgu[:, F:], -SWIGLU_LIMIT, SWIGLU_LIMIT)
    act = gate * _sigmoid(SWIGLU_ALPHA * gate) * (up + 1.0)
    yb_ref[...] = jnp.dot(act.astype(BF16), w2_ref[0], preferred_element_type=F32) + b2_ref[0]


def _experts(blk_exp, xs, w1, b1, w2, b2, tme):
    P, D = xs.shape
    E, _, F2 = w1.shape
    F = F2 // 2
    return pl.pallas_call(
        functools.partial(_moe_kernel, F=F),
        grid_spec=pltpu.PrefetchScalarGridSpec(
            num_scalar_prefetch=1,
            grid=(P // tme,),
            in_specs=[pl.BlockSpec((tme, D), lambda i, be: (i, 0)),
                      pl.BlockSpec((1, D, F2), lambda i, be: (be[i], 0, 0)),
                      pl.BlockSpec((1, 1, F2), lambda i, be: (be[i], 0, 0)),
                      pl.BlockSpec((1, F, D), lambda i, be: (be[i], 0, 0)),
                      pl.BlockSpec((1, 1, D), lambda i, be: (be[i], 0, 0))],
            out_specs=pl.BlockSpec((tme, D), lambda i, be: (i, 0))),
        out_shape=jax.ShapeDtypeStruct((P, D), F32),
        compiler_params=_cparams(("arbitrary",)),
        name="experts",
    )(blk_exp, xs, w1.astype(BF16), b1.reshape(E, 1, F2), w2.astype(BF16), b2.reshape(E, 1, D))


def _comb_kernel(dest_ref, g_ref, x1_ref, mod_ref, fw_ref, yb_ref, o_ref, buf, sem, *, tmc):
    def issue(t, carry):
        for k in range(TOP_K):
            d = dest_ref[k, t]
            _row_copy(yb_ref.at[pl.ds(d, 1)], buf.at[k, pl.ds(t, 1)], sem).start()
        return carry

    lax.fori_loop(0, tmc, issue, 0)
    for k in range(TOP_K):
        _row_copy(yb_ref.at[pl.ds(0, tmc)], buf.at[k], sem).wait()
    g = g_ref[...]
    y = g[:, 0:1] * buf[0]
    for k in range(1, TOP_K):
        y = y + g[:, k:k + 1] * buf[k]
    x2 = x1_ref[...] + mod_ref[0, 5:6, :] * y
    ms = jnp.mean(x2 * x2, axis=-1, keepdims=True)
    o_ref[...] = x2 * lax.rsqrt(ms + EPS) * fw_ref[...]


def _combine(dest, gtc, x1, mod, final_w, yb, S):
    T, D = x1.shape
    tmc = 256
    return pl.pallas_call(
        functools.partial(_comb_kernel, tmc=tmc),
        grid=(T // tmc,),
        in_specs=[pl.BlockSpec((8, tmc), lambda i: (0, i), memory_space=pltpu.SMEM),
                  pl.BlockSpec((tmc, LANES), lambda i: (i, 0)),
                  pl.BlockSpec((tmc, D), lambda i: (i, 0)),
                  pl.BlockSpec((1, 6, D), lambda i: (i * tmc // S, 0, 0)),
                  pl.BlockSpec((1, D), lambda i: (0, 0)),
                  pl.BlockSpec(memory_space=pl.ANY)],
        out_specs=pl.BlockSpec((tmc, D), lambda i: (i, 0)),
        out_shape=jax.ShapeDtypeStruct((T, D), F32),
        scratch_shapes=[pltpu.VMEM((TOP_K, tmc, D), F32), pltpu.SemaphoreType.DMA(())],
        compiler_params=_cparams(("arbitrary",)),
        name="combine",
    )(dest, gtc, x1, mod, final_w.reshape(1, D), yb)


def _layer(x2, mod, B, S, norm1_w, w_in, conv_w, A_log, dt_bias, dn_norm_w, w_out,
           norm2_w, w_router, b_router, w1, b1, w2, b2):
    T, D = x2.shape
    qkv, dqkv, dz, gbc = _inproj(x2, mod, norm1_w, w_in, S)
    branches = [_attn_branch(qkv, B, S, d) for d in DILATIONS]
    dn = _gdn(dqkv, dz, gbc, conv_w, A_log, dt_bias, dn_norm_w, B, S)
    (o1, l1), (o2, l2), (o3, l3) = branches
    x1, h2, te, rk, gtc, cnt = _outproj(o1, o2, o3, l1, l2, l3, dn, x2, mod, w_out, norm2_w,
                                        w_router, b_router, S)
    tme = 256
    P = T * TOP_K + N_EXPERTS * tme
    counts = cnt[:, 0]
    padded = (counts + tme - 1) // tme * tme
    pend = jnp.cumsum(padded)
    pstart = (pend - padded).astype(I32)
    blk_start = jnp.arange(P // tme, dtype=I32) * tme
    blk_exp = jnp.minimum(jnp.sum((pend[None, :] <= blk_start[:, None]).astype(I32), axis=1),
                          N_EXPERTS - 1).astype(I32)
    dest = _dest(pstart, te, rk)
    xs = _dispatch(dest, h2, P)
    yb = _experts(blk_exp, xs, w1, b1, w2, b2, tme)
    return x1, dest, gtc, yb


def kernel(x, c, w_ada, b_ada, norm1_w, w_in, conv_w, A_log, dt_bias, dn_norm_w, w_out, norm2_w,
           w_router, b_router, w1, b1, w2, b2, final_norm_w):
    B, S, D = x.shape
    depth = w_ada.shape[0]
    assert S % (ATTN_BLK * DILATIONS[-1]) == 0 and depth == 1
    x2 = x.reshape(B * S, D)
    mod = _ada(c, w_ada[0], b_ada[0])
    x1, dest, gtc, yb = _layer(x2, mod, B, S, norm1_w[0], w_in[0], conv_w[0], A_log[0], dt_bias[0],
                               dn_norm_w[0], w_out[0], norm2_w[0], w_router[0], b_router[0],
                               w1[0], b1[0], w2[0], b2[0])
    out = _combine(dest, gtc, x1, mod, final_norm_w, yb, S)
    return out.reshape(B, S, D)
```

```python
import functools

import jax
import jax.numpy as jnp
from jax import lax
from jax.experimental import pallas as pl
from jax.experimental.pallas import tpu as pltpu
from jax.experimental.pallas import tpu_sc as plsc

F32 = jnp.float32
BF16 = jnp.bfloat16
I32 = jnp.int32
HI = lax.Precision.HIGHEST

LANES = 128
ATTN_HEADS = 8
ATTN_HD = 64
ATTN_W = ATTN_HEADS * ATTN_HD
ATTN_BLK = 128
DILATIONS = (1, 4, 16)
DN_HEADS = 4
DN_D = 128
DN_W = DN_HEADS * DN_D
DN_CONV = 3 * DN_W
CONV_K = 4
DN_CHUNK = 64
N_EXPERTS = 32
TOP_K = 4
MOE_ROWS = 512
SWIGLU_LIMIT = 7.0
SWIGLU_ALPHA = 1.702
EPS = 1e-6
NEG = -1e30
MAIN_COLS = 3 * ATTN_W + DN_CONV + DN_W

VMEM_LIMIT = 56 * 1024 * 1024


def _cparams(sem):
    return pltpu.CompilerParams(dimension_semantics=sem, vmem_limit_bytes=VMEM_LIMIT)


def _nt(a, b, **kw):
    return lax.dot_general(a, b, (((1,), (1,)), ((), ())), preferred_element_type=F32, **kw)


def _sigmoid(x):
    return 1.0 / (1.0 + jnp.exp(-x))


def _ada_kernel(c_ref, w_ref, b_ref, o_ref):
    c = c_ref[...]
    cond = c * _sigmoid(c)
    o_ref[...] = jnp.dot(cond, w_ref[...], preferred_element_type=F32, precision=HI) + b_ref[...]


def _ada(c, w_ada, b_ada):
    B, D = c.shape
    N = w_ada.shape[1]
    cp = jnp.zeros((8, D), F32).at[:B].set(c)
    tn = 1024
    out = pl.pallas_call(
        _ada_kernel,
        grid=(N // tn,),
        in_specs=[pl.BlockSpec((8, D), lambda j: (0, 0)),
                  pl.BlockSpec((D, tn), lambda j: (0, j)),
                  pl.BlockSpec((1, tn), lambda j: (0, j))],
        out_specs=pl.BlockSpec((8, tn), lambda j: (0, j)),
        out_shape=jax.ShapeDtypeStruct((8, N), F32),
        compiler_params=_cparams(("arbitrary",)),
        name="ada",
    )(cp, w_ada, b_ada.reshape(1, N))
    return out[:B].reshape(B, 6, D)


def _inproj_kernel(x_ref, mod_ref, nw_ref, wm_ref, ws_ref, qkv_ref, dqkv_ref, dz_ref, gbc_ref):
    x = x_ref[...]
    shift = mod_ref[0, 0:1, :]
    scale = mod_ref[0, 1:2, :]
    ms = jnp.mean(x * x, axis=-1, keepdims=True)
    h = x * lax.rsqrt(ms + EPS) * nw_ref[...]
    hb = (h * (1.0 + scale) + shift).astype(BF16)
    for j in range(3):
        r = jnp.dot(hb, wm_ref[:, j * ATTN_W:(j + 1) * ATTN_W], preferred_element_type=F32)
        if j == 0:
            r = r * (ATTN_HD ** -0.5)
        qkv_ref[:, j * ATTN_W:(j + 1) * ATTN_W] = r.astype(BF16)
    for j in range(3):
        c0 = 3 * ATTN_W + j * DN_W
        r = jnp.dot(hb, wm_ref[:, c0:c0 + DN_W], preferred_element_type=F32)
        dqkv_ref[:, j * DN_W:(j + 1) * DN_W] = r.astype(BF16)
    c0 = 3 * ATTN_W + DN_CONV
    dz_ref[...] = jnp.dot(hb, wm_ref[:, c0:c0 + DN_W], preferred_element_type=F32).astype(BF16)
    gbc_ref[...] = jnp.dot(hb, ws_ref[...], preferred_element_type=F32)


def _inproj(x2, mod, norm_w, w_in, S):
    T, D = x2.shape
    tm = 512
    wm = w_in[:, :MAIN_COLS].astype(BF16)
    ws = jnp.zeros((D, LANES), F32).at[:, :2 * DN_HEADS].set(w_in[:, MAIN_COLS:]).astype(BF16)
    return pl.pallas_call(
        _inproj_kernel,
        grid=(T // tm,),
        in_specs=[pl.BlockSpec((tm, D), lambda i: (i, 0)),
                  pl.BlockSpec((1, 6, D), lambda i: (i * tm // S, 0, 0)),
                  pl.BlockSpec((1, D), lambda i: (0, 0)),
                  pl.BlockSpec((D, MAIN_COLS), lambda i: (0, 0)),
                  pl.BlockSpec((D, LANES), lambda i: (0, 0))],
        out_specs=[pl.BlockSpec((tm, 3 * ATTN_W), lambda i: (i, 0)),
                   pl.BlockSpec((tm, DN_CONV), lambda i: (i, 0)),
                   pl.BlockSpec((tm, DN_W), lambda i: (i, 0)),
                   pl.BlockSpec((tm, LANES), lambda i: (i, 0))],
        out_shape=[jax.ShapeDtypeStruct((T, 3 * ATTN_W), BF16),
                   jax.ShapeDtypeStruct((T, DN_CONV), BF16),
                   jax.ShapeDtypeStruct((T, DN_W), BF16),
                   jax.ShapeDtypeStruct((T, LANES), F32)],
        compiler_params=_cparams(("arbitrary",)),
        name="inproj",
    )(x2, mod, norm_w.reshape(1, D), wm, ws)


def _attn_kernel(q_ref, kc_ref, kp_ref, vc_ref, vp_ref, o_ref, lse_ref, kf, vf, *, qb):
    n = pl.program_id(2)
    kf[0:ATTN_BLK, :] = kp_ref[0]
    kf[ATTN_BLK:, :] = kc_ref[0]
    vf[0:ATTN_BLK, :] = vp_ref[0]
    vf[ATTN_BLK:, :] = vc_ref[0]
    row = lax.broadcasted_iota(I32, (ATTN_BLK, 2 * ATTN_BLK), 0)
    col = lax.broadcasted_iota(I32, (ATTN_BLK, 2 * ATTN_BLK), 1)
    band = jnp.logical_or(jnp.logical_and(col < ATTN_BLK, col >= row),
                          jnp.logical_and(col >= ATTN_BLK, col - ATTN_BLK <= row))
    lane = lax.broadcasted_iota(I32, (ATTN_BLK, LANES), 1)
    lo = lane < ATTN_HD

    def sub(j, carry):
        r0 = pl.multiple_of(j * ATTN_BLK, ATTN_BLK)
        first_col = jnp.where(jnp.logical_and(n == 0, j == 0), ATTN_BLK, 0)
        mask = jnp.logical_and(band, col >= first_col)
        lse_tile = jnp.zeros((ATTN_BLK, LANES), F32)
        for hp in range(ATTN_W // LANES):
            cs = slice(hp * LANES, (hp + 1) * LANES)
            q2 = q_ref[0, pl.ds(r0, ATTN_BLK), cs]
            k2 = kf[pl.ds(r0, 2 * ATTN_BLK), cs]
            v2 = vf[pl.ds(r0, 2 * ATTN_BLK), cs]
            outs = []
            for half in range(2):
                qm = jnp.where(lo if half == 0 else jnp.logical_not(lo), q2, jnp.zeros_like(q2))
                s = _nt(qm, k2)
                s = jnp.where(mask, s, NEG)
                m = jnp.max(s, axis=-1, keepdims=True)
                p = jnp.exp(s - m)
                den = jnp.sum(p, axis=-1, keepdims=True)
                acc = jnp.dot(p.astype(BF16), v2, preferred_element_type=F32)
                outs.append(acc / den)
                lse_tile = jnp.where(lane == 2 * hp + half, m + jnp.log(den), lse_tile)
            o_ref[0, pl.ds(r0, ATTN_BLK), cs] = jnp.where(lo, outs[0], outs[1]).astype(BF16)
        lse_ref[0, pl.ds(r0, ATTN_BLK), :] = lse_tile
        return carry

    lax.fori_loop(0, qb // ATTN_BLK, sub, 0)


def _attn_branch(qkv, B, S, d):
    L = S // d
    qb = min(512, L)
    nsub = qb // ATTN_BLK
    W3 = 3 * ATTN_W
    view = qkv.reshape(B, L, d * W3)
    cur = lambda c: pl.BlockSpec((1, qb, ATTN_W), lambda b, r, n: (b, n, 3 * r + c))
    prev = lambda c: pl.BlockSpec((1, ATTN_BLK, ATTN_W),
                                  lambda b, r, n: (b, jnp.maximum(n * nsub - 1, 0), 3 * r + c))
    o, lse = pl.pallas_call(
        functools.partial(_attn_kernel, qb=qb),
        grid=(B, d, L // qb),
        in_specs=[cur(0), cur(1), prev(1), cur(2), prev(2)],
        out_specs=[pl.BlockSpec((1, qb, ATTN_W), lambda b, r, n: (b, n, r)),
                   pl.BlockSpec((1, qb, LANES), lambda b, r, n: (b, n, r))],
        out_shape=[jax.ShapeDtypeStruct((B, L, d * ATTN_W), BF16),
                   jax.ShapeDtypeStruct((B, L, d * LANES), F32)],
        scratch_shapes=[pltpu.VMEM((qb + ATTN_BLK, ATTN_W), BF16),
                        pltpu.VMEM((qb + ATTN_BLK, ATTN_W), BF16)],
        compiler_params=_cparams(("arbitrary", "arbitrary", "arbitrary")),
        name=f"attn_d{d}",
    )(view, view, view, view, view)
    return o.reshape(B * S, ATTN_W), lse.reshape(B * S, LANES)


def _gdn_kernel(x_ref, z_ref, g_ref, cw_ref, prm_ref, nw_ref, o_ref, xext, yc, s0, s1, *, rb):
    i = pl.program_id(1)

    @pl.when(i == 0)
    def _():
        xext[0:8, :] = jnp.zeros((8, DN_CONV), F32)
        s0[...] = jnp.zeros_like(s0)
        s1[...] = jnp.zeros_like(s1)

    @pl.when(i > 0)
    def _():
        xext[0:8, :] = xext[rb:rb + 8, :]

    xext[8:, :] = x_ref[0].astype(F32)
    y = cw_ref[CONV_K - 1:CONV_K, :] * xext[8:8 + rb, :]
    for j in range(CONV_K - 1):
        off = 8 - (CONV_K - 1) + j
        y = y + cw_ref[j:j + 1, :] * xext[off:off + rb, :]
    yc[...] = y * _sigmoid(y)

    C = DN_CHUNK
    H = DN_HEADS
    CW = H * C
    dot = functools.partial(jnp.dot, preferred_element_type=F32)

    def iota(shape, d):
        return lax.broadcasted_iota(I32, shape, d)

    ltri_b = jnp.where(iota((C, C), 0) >= iota((C, C), 1), 1.0, 0.0).astype(BF16)
    lane = iota((C, LANES), 1)
    blane = lane < H
    glane = jnp.logical_and(lane >= H, lane < 2 * H)
    e512 = jnp.where(jnp.logical_or(iota((LANES, DN_W), 1) // DN_D == iota((LANES, DN_W), 0),
                                    iota((LANES, DN_W), 1) // DN_D == iota((LANES, DN_W), 0) - H),
                     1.0, 0.0).astype(BF16)
    e256 = jnp.where(iota((LANES, CW), 1) // C == iota((LANES, CW), 0) - H, 1.0, 0.0).astype(BF16)
    row4 = iota((C, CW), 0)
    col4 = iota((C, CW), 1) % C
    eye4 = jnp.where(row4 == col4, 1.0, 0.0).astype(F32)
    blk = [iota((C, CW), 1) // C == h for h in range(H)]
    bd_cc = iota((CW, CW), 0) // C == iota((CW, CW), 1) // C
    bd_pair = iota((CW, CW), 0) // DN_D == iota((CW, CW), 1) // DN_D
    rt_mask = iota((CW, DN_W), 0) // C == iota((CW, DN_W), 1) // DN_D
    neg_a = -jnp.exp(prm_ref[0:1, :])
    dtb = prm_ref[1:2, :]
    nw = nw_ref[...]

    def hilo(x):
        hi = x.astype(BF16)
        return hi, (x - hi.astype(F32)).astype(BF16)

    def heads(a, w):
        return [a[:, h * w:(h + 1) * w] for h in range(H)]

    def l2n(a, mult):
        return jnp.concatenate(
            [p * (lax.rsqrt(jnp.sum(p * p, axis=-1, keepdims=True) + EPS) * mult) for p in heads(a, DN_D)],
            axis=1)

    def stack4(a):
        return jnp.concatenate([a, a, a, a], axis=0)

    for c in range(rb // C):
        rs = slice(c * C, (c + 1) * C)
        G = g_ref[0, rs, :]
        xg = G + dtb
        gv = jnp.where(glane, neg_a * (jnp.maximum(xg, 0.0) + jnp.log1p(jnp.exp(-jnp.abs(xg)))), 0.0)
        be = jnp.where(blane, _sigmoid(G), 0.0)
        g_hi, g_lo = hilo(gv)
        gcum = dot(ltri_b, g_hi) + dot(ltri_b, g_lo)
        gtot = gcum[C - 1:C, :]
        eg = jnp.where(glane, jnp.exp(gcum), 0.0)
        ek = jnp.where(glane, jnp.exp(gtot - gcum), 0.0)
        ex = dot(jnp.concatenate([be, eg, ek], axis=0).astype(BF16), e512)
        bexp, egexp, ekexp = ex[0:C], ex[C:2 * C], ex[2 * C:3 * C]
        gexp = dot(g_hi, e256) + dot(g_lo, e256)
        d_hi, d_lo = hilo(jnp.where(row4 > col4, gexp, 0.0))
        diff = dot(ltri_b, d_hi) + dot(ltri_b, d_lo)
        decay = jnp.exp(jnp.where(row4 >= col4, diff, NEG))

        q4 = l2n(yc[rs, 0:DN_W], DN_D ** -0.5)
        k4 = l2n(yc[rs, DN_W:2 * DN_W], 1.0)
        v4 = yc[rs, 2 * DN_W:3 * DN_W]
        kb4 = k4 * bexp
        vb4 = v4 * bexp
        rt = jnp.where(rt_mask, stack4(k4.astype(BF16)), jnp.zeros((), BF16))
        ai = _nt(jnp.concatenate([kb4, q4], axis=0).astype(BF16), rt)
        a4 = jnp.where(row4 > col4, ai[0:C] * decay, 0.0)
        intra = ai[C:2 * C] * decay
        t4 = eye4 - a4
        pb = a4.astype(BF16)
        bd = jnp.where(bd_cc, stack4(pb), jnp.zeros((), BF16))
        for _ in range(5):
            pb = dot(pb, bd).astype(BF16)
            bd = jnp.where(bd_cc, stack4(pb), jnp.zeros((), BF16))
            t4 = t4 + dot(t4.astype(BF16), bd)
        t4b = t4.astype(BF16)
        zb = jnp.zeros((), BF16)
        lstk = jnp.concatenate([jnp.where(blk[h], t4b, zb) for h in range(H)], axis=0)
        kbg4 = kb4 * egexp
        rstk = jnp.concatenate(
            [jnp.concatenate([vb, kbg], axis=1) for vb, kbg in zip(heads(vb4, DN_D), heads(kbg4, DN_D))],
            axis=0).astype(BF16)
        uw = dot(lstk, rstk)
        u4 = jnp.concatenate([uw[h * C:(h + 1) * C, 0:DN_D] for h in range(H)], axis=1)
        w4 = jnp.concatenate([uw[h * C:(h + 1) * C, DN_D:2 * DN_D] for h in range(H)], axis=1)
        qd4 = q4 * egexp
        kd4 = (k4 * ekexp).astype(BF16)
        ib = intra.astype(BF16)
        lint = jnp.concatenate([jnp.where(blk[h], ib, zb) for h in range(H)], axis=0)
        gl4 = egexp[C - 1:C, :]

        wq = jnp.concatenate([w4, qd4], axis=0).astype(BF16)
        ra = dot(wq[:, 0:CW], s0[...].astype(BF16))
        rc = dot(wq[:, CW:2 * CW], s1[...].astype(BF16))
        vn = u4 - jnp.concatenate([ra[0:C], rc[0:C]], axis=1)
        vnb = vn.astype(BF16)
        oi = dot(lint, jnp.concatenate(heads(vnb, DN_D), axis=0))
        o = (jnp.concatenate([ra[C:2 * C], rc[C:2 * C]], axis=1)
             + jnp.concatenate([oi[h * C:(h + 1) * C] for h in range(H)], axis=1))
        tn = (((0,), (0,)), ((), ()))
        s0[...] = s0[...] * gl4[:, 0:CW] + jnp.where(
            bd_pair, lax.dot_general(kd4[:, 0:CW], vnb[:, 0:CW], tn, preferred_element_type=F32), 0.0)
        s1[...] = s1[...] * gl4[:, CW:2 * CW] + jnp.where(
            bd_pair, lax.dot_general(kd4[:, CW:2 * CW], vnb[:, CW:2 * CW], tn, preferred_element_type=F32), 0.0)

        z = z_ref[0, rs, :].astype(F32)
        on = jnp.concatenate(
            [p * lax.rsqrt(jnp.mean(p * p, axis=-1, keepdims=True) + EPS) * nw for p in heads(o, DN_D)], axis=1)
        o_ref[0, rs, :] = (on * (z * _sigmoid(z))).astype(BF16)


def _gdn(dqkv, dz, gbc, conv_w, A_log, dt_bias, dn_norm_w, B, S):
    rb = 256
    prm = jnp.zeros((2, LANES), F32)
    prm = prm.at[0, DN_HEADS:2 * DN_HEADS].set(A_log.astype(F32))
    prm = prm.at[1, DN_HEADS:2 * DN_HEADS].set(dt_bias.astype(F32))
    out = pl.pallas_call(
        functools.partial(_gdn_kernel, rb=rb),
        grid=(B, S // rb),
        in_specs=[pl.BlockSpec((1, rb, DN_CONV), lambda b, i: (b, i, 0)),
                  pl.BlockSpec((1, rb, DN_W), lambda b, i: (b, i, 0)),
                  pl.BlockSpec((1, rb, LANES), lambda b, i: (b, i, 0)),
                  pl.BlockSpec((CONV_K, DN_CONV), lambda b, i: (0, 0)),
                  pl.BlockSpec((2, LANES), lambda b, i: (0, 0)),
                  pl.BlockSpec((1, DN_D), lambda b, i: (0, 0))],
        out_specs=pl.BlockSpec((1, rb, DN_W), lambda b, i: (b, i, 0)),
        out_shape=jax.ShapeDtypeStruct((B, S, DN_W), BF16),
        scratch_shapes=[pltpu.VMEM((rb + 8, DN_CONV), F32),
                        pltpu.VMEM((rb, DN_CONV), F32),
                        pltpu.VMEM((2 * DN_D, 2 * DN_D), F32),
                        pltpu.VMEM((2 * DN_D, 2 * DN_D), F32)],
        compiler_params=_cparams(("arbitrary", "arbitrary")),
        name="gdn",
    )(dqkv.reshape(B, S, DN_CONV), dz.reshape(B, S, DN_W), gbc.reshape(B, S, LANES),
      conv_w, prm, dn_norm_w.reshape(1, DN_D))
    return out.reshape(B * S, DN_W)


def _out_kernel(o1_ref, o2_ref, o3_ref, l1_ref, l2_ref, l3_ref, dn_ref, x_ref, mod_ref, wo_ref,
                n2_ref, wr_ref, br_ref,
                x1_ref, h2_ref, te_ref, rk_ref, gtc_ref, cnt_ref, base, *, tm):
    i = pl.program_id(0)

    @pl.when(i == 0)
    def _():
        base[...] = jnp.zeros_like(base)

    l1, l2, l3 = l1_ref[...], l2_ref[...], l3_ref[...]
    mx = jnp.maximum(jnp.maximum(l1, l2), l3)
    e1, e2, e3 = jnp.exp(l1 - mx), jnp.exp(l2 - mx), jnp.exp(l3 - mx)
    zs = e1 + e2 + e3
    er = lax.broadcasted_iota(I32, (LANES, ATTN_W), 0)
    ec = lax.broadcasted_iota(I32, (LANES, ATTN_W), 1)
    expand = jnp.where(ec // ATTN_HD == er, 1.0, 0.0).astype(BF16)
    attn = jnp.zeros((tm, ATTN_W), F32)
    for e, o_ref in ((e1, o1_ref), (e2, o2_ref), (e3, o3_ref)):
        wgt = jnp.dot((e / zs).astype(BF16), expand, preferred_element_type=F32)
        attn = attn + wgt * o_ref[...].astype(F32)
    mix = (jnp.dot(attn.astype(BF16), wo_ref[0:ATTN_W, :], preferred_element_type=F32)
           + jnp.dot(dn_ref[...], wo_ref[ATTN_W:, :], preferred_element_type=F32))
    x1 = x_ref[...] + mod_ref[0, 2:3, :] * mix
    x1_ref[...] = x1
    ms = jnp.mean(x1 * x1, axis=-1, keepdims=True)
    h2 = x1 * lax.rsqrt(ms + EPS) * n2_ref[...]
    h2 = h2 * (1.0 + mod_ref[0, 4:5, :]) + mod_ref[0, 3:4, :]
    h2_ref[...] = h2

    lg = _nt(wr_ref[...], h2, precision=HI) + br_ref[...]
    eidx = lax.broadcasted_iota(I32, (N_EXPERTS, tm), 0)
    vals, idxs, sels = [], [], []
    for _ in range(TOP_K):
        m = jnp.max(lg, axis=0, keepdims=True)
        idx = jnp.min(jnp.where(lg == m, eidx, N_EXPERTS), axis=0, keepdims=True)
        sel = eidx == idx
        vals.append(m)
        idxs.append(idx)
        sels.append(sel)
        lg = jnp.where(sel, -jnp.inf, lg)
    ex = [jnp.exp(v - vals[0]) for v in vals]
    den = ex[0] + ex[1] + ex[2] + ex[3]
    gates = [e / den for e in ex]

    msum = jnp.zeros((N_EXPERTS, tm), F32)
    for sel in sels:
        msum = msum + jnp.where(sel, 1.0, 0.0)
    tr = lax.broadcasted_iota(I32, (tm, tm), 0)
    tc = lax.broadcasted_iota(I32, (tm, tm), 1)
    upper = jnp.where(tr <= tc, 1.0, 0.0).astype(BF16)
    incl = jnp.dot(msum.astype(BF16), upper, preferred_element_type=F32)
    pos = base[:, 0:1] + (incl - msum)
    sub8 = lax.broadcasted_iota(I32, (8, tm), 0)
    te = jnp.zeros((8, tm), I32)
    rk = jnp.zeros((8, tm), I32)
    gt = jnp.zeros((8, tm), F32)
    for k in range(TOP_K):
        rank_k = jnp.sum(jnp.where(sels[k], pos, 0.0), axis=0, keepdims=True).astype(I32)
        te = jnp.where(sub8 == k, idxs[k], te)
        rk = jnp.where(sub8 == k, rank_k, rk)
        gt = jnp.where(sub8 == k, gates[k], gt)
    te_ref[...] = te
    rk_ref[...] = rk
    gtc_ref[...] = jnp.transpose(jnp.concatenate([gt, jnp.zeros((LANES - 8, tm), F32)], axis=0))
    base[...] = base[...] + jnp.sum(msum, axis=1, keepdims=True)
    cnt_ref[...] = base[...].astype(I32)


def _outproj(o1, o2, o3, l1, l2, l3, dn, x2, mod, w_out, norm2_w, w_router, b_router, S):
    T, D = x2.shape
    tm = 512
    row = lambda w: pl.BlockSpec((tm, w), lambda i: (i, 0))
    colb = pl.BlockSpec((8, tm), lambda i: (0, i))
    return pl.pallas_call(
        functools.partial(_out_kernel, tm=tm),
        grid=(T // tm,),
        in_specs=[row(ATTN_W), row(ATTN_W), row(ATTN_W), row(LANES), row(LANES), row(LANES),
                  row(DN_W), row(D),
                  pl.BlockSpec((1, 6, D), lambda i: (i * tm // S, 0, 0)),
                  pl.BlockSpec((D, D), lambda i: (0, 0)),
                  pl.BlockSpec((1, D), lambda i: (0, 0)),
                  pl.BlockSpec((N_EXPERTS, D), lambda i: (0, 0)),
                  pl.BlockSpec((N_EXPERTS, 1), lambda i: (0, 0))],
        out_specs=[row(D), row(D), colb, colb, row(LANES),
                   pl.BlockSpec((N_EXPERTS, LANES), lambda i: (0, 0))],
        out_shape=[jax.ShapeDtypeStruct((T, D), F32),
                   jax.ShapeDtypeStruct((T, D), F32),
                   jax.ShapeDtypeStruct((8, T), I32),
                   jax.ShapeDtypeStruct((8, T), I32),
                   jax.ShapeDtypeStruct((T, LANES), F32),
                   jax.ShapeDtypeStruct((N_EXPERTS, LANES), I32)],
        scratch_shapes=[pltpu.VMEM((N_EXPERTS, LANES), F32)],
        compiler_params=_cparams(("arbitrary",)),
        name="outproj_router",
    )(o1, o2, o3, l1, l2, l3, dn, x2, mod, w_out.astype(BF16), norm2_w.reshape(1, D),
      jnp.transpose(w_router), b_router.reshape(N_EXPERTS, 1))


def _dest_kernel(ps_ref, te_ref, rk_ref, d_ref):
    te = te_ref[...]
    acc = jnp.zeros(te.shape, I32)
    for e in range(N_EXPERTS):
        acc = jnp.where(te == e, ps_ref[e], acc)
    d_ref[...] = acc + rk_ref[...]


def _dest(pstart, te, rk):
    T = te.shape[1]
    tb = 2048
    return pl.pallas_call(
        _dest_kernel,
        grid_spec=pltpu.PrefetchScalarGridSpec(
            num_scalar_prefetch=1,
            grid=(T // tb,),
            in_specs=[pl.BlockSpec((8, tb), lambda i, ps: (0, i)),
                      pl.BlockSpec((8, tb), lambda i, ps: (0, i))],
            out_specs=pl.BlockSpec((8, tb), lambda i, ps: (0, i))),
        out_shape=jax.ShapeDtypeStruct((8, T), I32),
        compiler_params=_cparams(("arbitrary",)),
        name="dest_rows",
    )(pstart, te, rk)


SC_CORES = 2
SC_SUBCORES = 16
SC_IDX_CHUNK = 128


def _invperm(dest_flat, P):
    N = dest_flat.shape[0]
    nw = SC_CORES * SC_SUBCORES
    nch = N // (nw * SC_IDX_CHUNK)
    mesh = plsc.VectorSubcoreMesh(core_axis_name="c", subcore_axis_name="s",
                                  num_cores=SC_CORES, num_subcores=SC_SUBCORES)

    @functools.partial(
        pl.kernel, mesh=mesh, out_type=jax.ShapeDtypeStruct((P,), I32),
        scratch_types=[pltpu.VMEM((nch, SC_IDX_CHUNK), I32), pltpu.VMEM((nch, SC_IDX_CHUNK), I32)])
    def scatter_codes(idx_hbm, val_hbm, out_hbm, idx_v, val_v):
        wid = lax.axis_index("s") * SC_CORES + lax.axis_index("c")
        pltpu.sync_copy(idx_hbm.at[wid], idx_v)
        pltpu.sync_copy(val_hbm.at[wid], val_v)

        @pl.loop(0, nch)
        def _(j):
            pltpu.sync_copy(val_v.at[j], out_hbm.at[idx_v.at[j]])

    vals = jnp.arange(N, dtype=I32)
    return scatter_codes(dest_flat.reshape(nw, nch, SC_IDX_CHUNK), vals.reshape(nw, nch, SC_IDX_CHUNK))


def _row_copy(src, dst, sem):
    return pltpu.make_async_copy(src, dst, sem)


def _moe_kernel(be_ref, nv_ref, cc_ref, cn_ref, h2_ref, w1_ref, b1_ref, w2_ref, b2_ref, y4_ref,
                xbuf, ybuf, w1b, w2b, gsem, ssem, *, F, T, nb):
    i = pl.program_id(0)
    s = i % 2

    def gather_copy(tok, p, slot):
        return _row_copy(h2_ref.at[pl.ds(tok, 1)], xbuf.at[slot, pl.ds(p, 1)], gsem.at[slot])

    def start_gather(code_ref, nv, slot):
        def body(p, c):
            gather_copy(code_ref[0, 0, p] % T, p, slot).start()
            return c
        lax.fori_loop(0, nv, body, 0)

    def wait_rows(make, nv):
        n8 = pl.multiple_of((nv // 8) * 8, 8)

        @pl.when(n8 > 0)
        def _():
            make(n8).wait()

        def one(_, c):
            make(1).wait()
            return c
        lax.fori_loop(0, nv - n8, one, 0)

    @pl.when(i == 0)
    def _():
        xbuf[...] = jnp.zeros_like(xbuf)
        start_gather(cc_ref, nv_ref[0], 0)

    nv = nv_ref[i]
    wait_rows(lambda n: _row_copy(h2_ref.at[pl.ds(0, n)], xbuf.at[s, pl.ds(0, n)], gsem.at[s]), nv)

    @pl.when(i + 1 < nb)
    def _():
        start_gather(cn_ref, nv_ref[i + 1], 1 - s)

    @pl.when(i >= 2)
    def _():
        wait_rows(lambda n: _row_copy(ybuf.at[s, pl.ds(0, n)], y4_ref.at[pl.ds(0, n)], ssem.at[s]),
                  nv_ref[i - 2])

    @pl.when(jnp.logical_or(i == 0, be_ref[i] != be_ref[jnp.maximum(i - 1, 0)]))
    def _():
        w1b[...] = w1_ref[0].astype(BF16)
        w2b[...] = w2_ref[0].astype(BF16)

    @pl.when(nv > 0)
    def _():
        x = xbuf[s].astype(BF16)
        hgu = jnp.dot(x, w1b[...], preferred_element_type=F32) + b1_ref[0]
        gate = jnp.minimum(hgu[:, :F], SWIGLU_LIMIT)
        up = jnp.clip(hgu[:, F:], -SWIGLU_LIMIT, SWIGLU_LIMIT)
        act = gate * _sigmoid(SWIGLU_ALPHA * gate) * (up + 1.0)
        ybuf[s] = jnp.dot(act.astype(BF16), w2b[...], preferred_element_type=F32) + b2_ref[0]

        def body(p, c):
            _row_copy(ybuf.at[s, pl.ds(p, 1)], y4_ref.at[pl.ds(cc_ref[0, 0, p], 1)], ssem.at[s]).start()
            return c
        lax.fori_loop(0, nv, body, 0)

    @pl.when(i == nb - 1)
    def _():
        wait_rows(lambda n: _row_copy(ybuf.at[s, pl.ds(0, n)], y4_ref.at[pl.ds(0, n)], ssem.at[s]), nv)
        if nb > 1:
            wait_rows(lambda n: _row_copy(ybuf.at[1 - s, pl.ds(0, n)], y4_ref.at[pl.ds(0, n)],
                                          ssem.at[1 - s]), nv_ref[i - 1])


def _experts(blk_exp, blk_valid, codes, h2, w1, b1, w2, b2, tme):
    T, D = h2.shape
    E, _, F2 = w1.shape
    F = F2 // 2
    nb = blk_exp.shape[0]
    codes3 = codes.reshape(nb, 1, tme)
    wspec = lambda shape: pl.BlockSpec(shape, lambda i, be, nv: (be[i], 0, 0))
    return pl.pallas_call(
        functools.partial(_moe_kernel, F=F, T=T, nb=nb),
        grid_spec=pltpu.PrefetchScalarGridSpec(
            num_scalar_prefetch=2,
            grid=(nb,),
            in_specs=[pl.BlockSpec((1, 1, tme), lambda i, be, nv: (i, 0, 0), memory_space=pltpu.SMEM),
                      pl.BlockSpec((1, 1, tme), lambda i, be, nv: (jnp.minimum(i + 1, nb - 1), 0, 0),
                                   memory_space=pltpu.SMEM),
                      pl.BlockSpec(memory_space=pl.ANY),
                      wspec((1, D, F2)), wspec((1, 1, F2)), wspec((1, F, D)), wspec((1, 1, D))],
            out_specs=pl.BlockSpec(memory_space=pl.ANY),
            scratch_shapes=[pltpu.VMEM((2, tme, D), F32), pltpu.VMEM((2, tme, D), F32),
                            pltpu.VMEM((D, F2), BF16), pltpu.VMEM((F, D), BF16),
                            pltpu.SemaphoreType.DMA((2,)), pltpu.SemaphoreType.DMA((2,))]),
        out_shape=jax.ShapeDtypeStruct((TOP_K * T, D), F32),
        compiler_params=_cparams(("arbitrary",)),
        name="experts",
    )(blk_exp, blk_valid, codes3, codes3, h2, w1, b1.reshape(E, 1, F2), w2, b2.reshape(E, 1, D))


def _comb_kernel(g_ref, x1_ref, mod_ref, fw_ref, y0_ref, y1_ref, y2_ref, y3_ref, o_ref):
    g = g_ref[...]
    y = g[:, 0:1] * y0_ref[...]
    for k, y_ref in ((1, y1_ref), (2, y2_ref), (3, y3_ref)):
        y = y + g[:, k:k + 1] * y_ref[...]
    x2 = x1_ref[...] + mod_ref[0, 5:6, :] * y
    ms = jnp.mean(x2 * x2, axis=-1, keepdims=True)
    o_ref[...] = x2 * lax.rsqrt(ms + EPS) * fw_ref[...]


def _combine(gtc, x1, mod, final_w, y4, S):
    T, D = x1.shape
    tmc = 512
    nt = T // tmc
    yspec = lambda k: pl.BlockSpec((tmc, D), lambda i: (k * nt + i, 0))
    return pl.pallas_call(
        _comb_kernel,
        grid=(nt,),
        in_specs=[pl.BlockSpec((tmc, LANES), lambda i: (i, 0)),
                  pl.BlockSpec((tmc, D), lambda i: (i, 0)),
                  pl.BlockSpec((1, 6, D), lambda i: (i * tmc // S, 0, 0)),
                  pl.BlockSpec((1, D), lambda i: (0, 0)),
                  yspec(0), yspec(1), yspec(2), yspec(3)],
        out_specs=pl.BlockSpec((tmc, D), lambda i: (i, 0)),
        out_shape=jax.ShapeDtypeStruct((T, D), F32),
        compiler_params=_cparams(("arbitrary",)),
        name="combine",
    )(gtc, x1, mod, final_w.reshape(1, D), y4, y4, y4, y4)


def _layer(x2, mod, B, S, norm1_w, w_in, conv_w, A_log, dt_bias, dn_norm_w, w_out,
           norm2_w, w_router, b_router, w1, b1, w2, b2):
    T, D = x2.shape
    qkv, dqkv, dz, gbc = _inproj(x2, mod, norm1_w, w_in, S)
    branches = [_attn_branch(qkv, B, S, d) for d in DILATIONS]
    dn = _gdn(dqkv, dz, gbc, conv_w, A_log, dt_bias, dn_norm_w, B, S)
    (o1, l1), (o2, l2), (o3, l3) = branches
    x1, h2, te, rk, gtc, cnt = _outproj(o1, o2, o3, l1, l2, l3, dn, x2, mod, w_out, norm2_w,
                                        w_router, b_router, S)
    tme = MOE_ROWS
    P = T * TOP_K + N_EXPERTS * tme
    counts = cnt[:, 0]
    padded = (counts + tme - 1) // tme * tme
    pend = jnp.cumsum(padded)
    pstart = (pend - padded).astype(I32)
    blk_start = jnp.arange(P // tme, dtype=I32) * tme
    blk_exp = jnp.minimum(jnp.sum((pend[None, :] <= blk_start[:, None]).astype(I32), axis=1),
                          N_EXPERTS - 1).astype(I32)
    blk_valid = jnp.clip(pstart[blk_exp] + counts[blk_exp] - blk_start, 0, tme).astype(I32)
    dest = _dest(pstart, te, rk)
    codes = _invperm(dest[:TOP_K].reshape(TOP_K * T), P)
    y4 = _experts(blk_exp, blk_valid, codes, h2, w1, b1, w2, b2, tme)
    return x1, gtc, y4


def kernel(x, c, w_ada, b_ada, norm1_w, w_in, conv_w, A_log, dt_bias, dn_norm_w, w_out, norm2_w,
           w_router, b_router, w1, b1, w2, b2, final_norm_w):
    B, S, D = x.shape
    depth = w_ada.shape[0]
    assert S % (ATTN_BLK * DILATIONS[-1]) == 0 and depth == 1
    x2 = x.reshape(B * S, D)
    mod = _ada(c, w_ada[0], b_ada[0])
    x1, gtc, y4 = _layer(x2, mod, B, S, norm1_w[0], w_in[0], conv_w[0], A_log[0], dt_bias[0],
                         dn_norm_w[0], w_out[0], norm2_w[0], w_router[0], b_router[0],
                         w1[0], b1[0], w2[0], b2[0])
    out = _combine(gtc, x1, mod, final_norm_w, y4, S)
    return out.reshape(B, S, D)
```

```python
import functools

import jax
import jax.numpy as jnp
from jax import lax
from jax.experimental import pallas as pl
from jax.experimental.pallas import tpu as pltpu
from jax.experimental.pallas import tpu_sc as plsc

F32 = jnp.float32
BF16 = jnp.bfloat16
I32 = jnp.int32
HI = lax.Precision.HIGHEST

LANES = 128
ATTN_HEADS = 8
ATTN_HD = 64
ATTN_W = ATTN_HEADS * ATTN_HD
ATTN_BLK = 128
DILATIONS = (1, 4, 16)
DN_HEADS = 4
DN_D = 128
DN_W = DN_HEADS * DN_D
DN_CONV = 3 * DN_W
CONV_K = 4
DN_CHUNK = 64
N_EXPERTS = 32
TOP_K = 4
MOE_ROWS = 512
ROW_UNROLL = 8
SWIGLU_LIMIT = 7.0
SWIGLU_ALPHA = 1.702
EPS = 1e-6
NEG = -1e30
MAIN_COLS = 3 * ATTN_W + DN_CONV + DN_W

VMEM_LIMIT = 56 * 1024 * 1024


def _cparams(sem):
    return pltpu.CompilerParams(dimension_semantics=sem, vmem_limit_bytes=VMEM_LIMIT)


def _nt(a, b, **kw):
    return lax.dot_general(a, b, (((1,), (1,)), ((), ())), preferred_element_type=F32, **kw)


def _sigmoid(x):
    return 1.0 / (1.0 + jnp.exp(-x))


def _store_slabs(ref, val, lead=()):
    n, d = val.shape
    ns = d // LANES
    for c in range(ns):
        ref[lead + (pl.ds(c, n, stride=ns), slice(None))] = val[:, c * LANES:(c + 1) * LANES]


def _load_slabs(ref, n, d, lead=()):
    ns = d // LANES
    return jnp.concatenate([ref[lead + (pl.ds(c, n, stride=ns), slice(None))] for c in range(ns)], axis=1)


def _ada_kernel(c_ref, w_ref, b_ref, o_ref):
    c = c_ref[...]
    cond = c * _sigmoid(c)
    o_ref[...] = jnp.dot(cond, w_ref[...], preferred_element_type=F32, precision=HI) + b_ref[...]


def _ada(c, w_ada, b_ada):
    B, D = c.shape
    N = w_ada.shape[1]
    cp = jnp.zeros((8, D), F32).at[:B].set(c)
    tn = 1024
    out = pl.pallas_call(
        _ada_kernel,
        grid=(N // tn,),
        in_specs=[pl.BlockSpec((8, D), lambda j: (0, 0)),
                  pl.BlockSpec((D, tn), lambda j: (0, j)),
                  pl.BlockSpec((1, tn), lambda j: (0, j))],
        out_specs=pl.BlockSpec((8, tn), lambda j: (0, j)),
        out_shape=jax.ShapeDtypeStruct((8, N), F32),
        compiler_params=_cparams(("arbitrary",)),
        name="ada",
    )(cp, w_ada, b_ada.reshape(1, N))
    return out[:B].reshape(B, 6, D)


def _inproj_kernel(x_ref, mod_ref, nw_ref, wm_ref, ws_ref, qkv_ref, dqkv_ref, dz_ref, gbc_ref):
    x = x_ref[...]
    shift = mod_ref[0, 0:1, :]
    scale = mod_ref[0, 1:2, :]
    ms = jnp.mean(x * x, axis=-1, keepdims=True)
    h = x * lax.rsqrt(ms + EPS) * nw_ref[...]
    hb = (h * (1.0 + scale) + shift).astype(BF16)
    for j in range(3):
        r = jnp.dot(hb, wm_ref[:, j * ATTN_W:(j + 1) * ATTN_W], preferred_element_type=F32)
        if j == 0:
            r = r * (ATTN_HD ** -0.5)
        qkv_ref[:, j * ATTN_W:(j + 1) * ATTN_W] = r.astype(BF16)
    for j in range(3):
        c0 = 3 * ATTN_W + j * DN_W
        r = jnp.dot(hb, wm_ref[:, c0:c0 + DN_W], preferred_element_type=F32)
        dqkv_ref[:, j * DN_W:(j + 1) * DN_W] = r.astype(BF16)
    c0 = 3 * ATTN_W + DN_CONV
    dz_ref[...] = jnp.dot(hb, wm_ref[:, c0:c0 + DN_W], preferred_element_type=F32).astype(BF16)
    gbc_ref[...] = jnp.dot(hb, ws_ref[...], preferred_element_type=F32)


def _inproj(x2, mod, norm_w, w_in, S):
    T, D = x2.shape
    tm = 512
    wm = w_in[:, :MAIN_COLS].astype(BF16)
    ws = jnp.zeros((D, LANES), F32).at[:, :2 * DN_HEADS].set(w_in[:, MAIN_COLS:]).astype(BF16)
    return pl.pallas_call(
        _inproj_kernel,
        grid=(T // tm,),
        in_specs=[pl.BlockSpec((tm, D), lambda i: (i, 0)),
                  pl.BlockSpec((1, 6, D), lambda i: (i * tm // S, 0, 0)),
                  pl.BlockSpec((1, D), lambda i: (0, 0)),
                  pl.BlockSpec((D, MAIN_COLS), lambda i: (0, 0)),
                  pl.BlockSpec((D, LANES), lambda i: (0, 0))],
        out_specs=[pl.BlockSpec((tm, 3 * ATTN_W), lambda i: (i, 0)),
                   pl.BlockSpec((tm, DN_CONV), lambda i: (i, 0)),
                   pl.BlockSpec((tm, DN_W), lambda i: (i, 0)),
                   pl.BlockSpec((tm, LANES), lambda i: (i, 0))],
        out_shape=[jax.ShapeDtypeStruct((T, 3 * ATTN_W), BF16),
                   jax.ShapeDtypeStruct((T, DN_CONV), BF16),
                   jax.ShapeDtypeStruct((T, DN_W), BF16),
                   jax.ShapeDtypeStruct((T, LANES), F32)],
        compiler_params=_cparams(("arbitrary",)),
        name="inproj",
    )(x2, mod, norm_w.reshape(1, D), wm, ws)


def _attn_kernel(q_ref, kc_ref, kp_ref, vc_ref, vp_ref, o_ref, lse_ref, kf, vf, *, qb):
    n = pl.program_id(2)
    kf[0:ATTN_BLK, :] = kp_ref[0]
    kf[ATTN_BLK:, :] = kc_ref[0]
    vf[0:ATTN_BLK, :] = vp_ref[0]
    vf[ATTN_BLK:, :] = vc_ref[0]
    row = lax.broadcasted_iota(I32, (ATTN_BLK, 2 * ATTN_BLK), 0)
    col = lax.broadcasted_iota(I32, (ATTN_BLK, 2 * ATTN_BLK), 1)
    band = jnp.logical_or(jnp.logical_and(col < ATTN_BLK, col >= row),
                          jnp.logical_and(col >= ATTN_BLK, col - ATTN_BLK <= row))
    lane = lax.broadcasted_iota(I32, (ATTN_BLK, LANES), 1)
    lo = lane < ATTN_HD

    def sub(j, carry):
        r0 = pl.multiple_of(j * ATTN_BLK, ATTN_BLK)
        first_col = jnp.where(jnp.logical_and(n == 0, j == 0), ATTN_BLK, 0)
        mask = jnp.logical_and(band, col >= first_col)
        lse_tile = jnp.zeros((ATTN_BLK, LANES), F32)
        for hp in range(ATTN_W // LANES):
            cs = slice(hp * LANES, (hp + 1) * LANES)
            q2 = q_ref[0, pl.ds(r0, ATTN_BLK), cs]
            k2 = kf[pl.ds(r0, 2 * ATTN_BLK), cs]
            v2 = vf[pl.ds(r0, 2 * ATTN_BLK), cs]
            outs = []
            for half in range(2):
                qm = jnp.where(lo if half == 0 else jnp.logical_not(lo), q2, jnp.zeros_like(q2))
                s = _nt(qm, k2)
                s = jnp.where(mask, s, NEG)
                m = jnp.max(s, axis=-1, keepdims=True)
                p = jnp.exp(s - m)
                den = jnp.sum(p, axis=-1, keepdims=True)
                acc = jnp.dot(p.astype(BF16), v2, preferred_element_type=F32)
                outs.append(acc / den)
                lse_tile = jnp.where(lane == 2 * hp + half, m + jnp.log(den), lse_tile)
            o_ref[0, pl.ds(r0, ATTN_BLK), cs] = jnp.where(lo, outs[0], outs[1]).astype(BF16)
        lse_ref[0, pl.ds(r0, ATTN_BLK), :] = lse_tile
        return carry

    lax.fori_loop(0, qb // ATTN_BLK, sub, 0)


def _attn_branch(qkv, B, S, d):
    L = S // d
    qb = min(512, L)
    nsub = qb // ATTN_BLK
    W3 = 3 * ATTN_W
    view = qkv.reshape(B, L, d * W3)
    cur = lambda c: pl.BlockSpec((1, qb, ATTN_W), lambda b, r, n: (b, n, 3 * r + c))
    prev = lambda c: pl.BlockSpec((1, ATTN_BLK, ATTN_W),
                                  lambda b, r, n: (b, jnp.maximum(n * nsub - 1, 0), 3 * r + c))
    o, lse = pl.pallas_call(
        functools.partial(_attn_kernel, qb=qb),
        grid=(B, d, L // qb),
        in_specs=[cur(0), cur(1), prev(1), cur(2), prev(2)],
        out_specs=[pl.BlockSpec((1, qb, ATTN_W), lambda b, r, n: (b, n, r)),
                   pl.BlockSpec((1, qb, LANES), lambda b, r, n: (b, n, r))],
        out_shape=[jax.ShapeDtypeStruct((B, L, d * ATTN_W), BF16),
                   jax.ShapeDtypeStruct((B, L, d * LANES), F32)],
        scratch_shapes=[pltpu.VMEM((qb + ATTN_BLK, ATTN_W), BF16),
                        pltpu.VMEM((qb + ATTN_BLK, ATTN_W), BF16)],
        compiler_params=_cparams(("arbitrary", "arbitrary", "arbitrary")),
        name=f"attn_d{d}",
    )(view, view, view, view, view)
    return o.reshape(B * S, ATTN_W), lse.reshape(B * S, LANES)


def _gdn_kernel(x_ref, z_ref, g_ref, cw_ref, prm_ref, nw_ref, o_ref, xext, yc, s0, s1, *, rb):
    i = pl.program_id(1)

    @pl.when(i == 0)
    def _():
        xext[0:8, :] = jnp.zeros((8, DN_CONV), F32)
        s0[...] = jnp.zeros_like(s0)
        s1[...] = jnp.zeros_like(s1)

    @pl.when(i > 0)
    def _():
        xext[0:8, :] = xext[rb:rb + 8, :]

    xext[8:, :] = x_ref[0].astype(F32)
    y = cw_ref[CONV_K - 1:CONV_K, :] * xext[8:8 + rb, :]
    for j in range(CONV_K - 1):
        off = 8 - (CONV_K - 1) + j
        y = y + cw_ref[j:j + 1, :] * xext[off:off + rb, :]
    yc[...] = y * _sigmoid(y)

    C = DN_CHUNK
    H = DN_HEADS
    CW = H * C
    dot = functools.partial(jnp.dot, preferred_element_type=F32)

    def iota(shape, d):
        return lax.broadcasted_iota(I32, shape, d)

    ltri_b = jnp.where(iota((C, C), 0) >= iota((C, C), 1), 1.0, 0.0).astype(BF16)
    lane = iota((C, LANES), 1)
    blane = lane < H
    glane = jnp.logical_and(lane >= H, lane < 2 * H)
    e512 = jnp.where(jnp.logical_or(iota((LANES, DN_W), 1) // DN_D == iota((LANES, DN_W), 0),
                                    iota((LANES, DN_W), 1) // DN_D == iota((LANES, DN_W), 0) - H),
                     1.0, 0.0).astype(BF16)
    e256 = jnp.where(iota((LANES, CW), 1) // C == iota((LANES, CW), 0) - H, 1.0, 0.0).astype(BF16)
    row4 = iota((C, CW), 0)
    col4 = iota((C, CW), 1) % C
    eye4 = jnp.where(row4 == col4, 1.0, 0.0).astype(F32)
    blk = [iota((C, CW), 1) // C == h for h in range(H)]
    bd_cc = iota((CW, CW), 0) // C == iota((CW, CW), 1) // C
    bd_pair = iota((CW, CW), 0) // DN_D == iota((CW, CW), 1) // DN_D
    rt_mask = iota((CW, DN_W), 0) // C == iota((CW, DN_W), 1) // DN_D
    neg_a = -jnp.exp(prm_ref[0:1, :])
    dtb = prm_ref[1:2, :]
    nw = nw_ref[...]

    def hilo(x):
        hi = x.astype(BF16)
        return hi, (x - hi.astype(F32)).astype(BF16)

    def heads(a, w):
        return [a[:, h * w:(h + 1) * w] for h in range(H)]

    def l2n(a, mult):
        return jnp.concatenate(
            [p * (lax.rsqrt(jnp.sum(p * p, axis=-1, keepdims=True) + EPS) * mult) for p in heads(a, DN_D)],
            axis=1)

    def stack4(a):
        return jnp.concatenate([a, a, a, a], axis=0)

    for c in range(rb // C):
        rs = slice(c * C, (c + 1) * C)
        G = g_ref[0, rs, :]
        xg = G + dtb
        gv = jnp.where(glane, neg_a * (jnp.maximum(xg, 0.0) + jnp.log1p(jnp.exp(-jnp.abs(xg)))), 0.0)
        be = jnp.where(blane, _sigmoid(G), 0.0)
        g_hi, g_lo = hilo(gv)
        gcum = dot(ltri_b, g_hi) + dot(ltri_b, g_lo)
        gtot = gcum[C - 1:C, :]
        eg = jnp.where(glane, jnp.exp(gcum), 0.0)
        ek = jnp.where(glane, jnp.exp(gtot - gcum), 0.0)
        ex = dot(jnp.concatenate([be, eg, ek], axis=0).astype(BF16), e512)
        bexp, egexp, ekexp = ex[0:C], ex[C:2 * C], ex[2 * C:3 * C]
        gexp = dot(g_hi, e256) + dot(g_lo, e256)
        d_hi, d_lo = hilo(jnp.where(row4 > col4, gexp, 0.0))
        diff = dot(ltri_b, d_hi) + dot(ltri_b, d_lo)
        decay = jnp.exp(jnp.where(row4 >= col4, diff, NEG))

        q4 = l2n(yc[rs, 0:DN_W], DN_D ** -0.5)
        k4 = l2n(yc[rs, DN_W:2 * DN_W], 1.0)
        v4 = yc[rs, 2 * DN_W:3 * DN_W]
        kb4 = k4 * bexp
        vb4 = v4 * bexp
        rt = jnp.where(rt_mask, stack4(k4.astype(BF16)), jnp.zeros((), BF16))
        ai = _nt(jnp.concatenate([kb4, q4], axis=0).astype(BF16), rt)
        a4 = jnp.where(row4 > col4, ai[0:C] * decay, 0.0)
        intra = ai[C:2 * C] * decay
        t4 = eye4 - a4
        pb = a4.astype(BF16)
        bd = jnp.where(bd_cc, stack4(pb), jnp.zeros((), BF16))
        for _ in range(5):
            pb = dot(pb, bd).astype(BF16)
            bd = jnp.where(bd_cc, stack4(pb), jnp.zeros((), BF16))
            t4 = t4 + dot(t4.astype(BF16), bd)
        t4b = t4.astype(BF16)
        zb = jnp.zeros((), BF16)
        lstk = jnp.concatenate([jnp.where(blk[h], t4b, zb) for h in range(H)], axis=0)
        kbg4 = kb4 * egexp
        rstk = jnp.concatenate(
            [jnp.concatenate([vb, kbg], axis=1) for vb, kbg in zip(heads(vb4, DN_D), heads(kbg4, DN_D))],
            axis=0).astype(BF16)
        uw = dot(lstk, rstk)
        u4 = jnp.concatenate([uw[h * C:(h + 1) * C, 0:DN_D] for h in range(H)], axis=1)
        w4 = jnp.concatenate([uw[h * C:(h + 1) * C, DN_D:2 * DN_D] for h in range(H)], axis=1)
        qd4 = q4 * egexp
        kd4 = (k4 * ekexp).astype(BF16)
        ib = intra.astype(BF16)
        lint = jnp.concatenate([jnp.where(blk[h], ib, zb) for h in range(H)], axis=0)
        gl4 = egexp[C - 1:C, :]

        wq = jnp.concatenate([w4, qd4], axis=0).astype(BF16)
        ra = dot(wq[:, 0:CW], s0[...].astype(BF16))
        rc = dot(wq[:, CW:2 * CW], s1[...].astype(BF16))
        vn = u4 - jnp.concatenate([ra[0:C], rc[0:C]], axis=1)
        vnb = vn.astype(BF16)
        oi = dot(lint, jnp.concatenate(heads(vnb, DN_D), axis=0))
        o = (jnp.concatenate([ra[C:2 * C], rc[C:2 * C]], axis=1)
             + jnp.concatenate([oi[h * C:(h + 1) * C] for h in range(H)], axis=1))
        tn = (((0,), (0,)), ((), ()))
        s0[...] = s0[...] * gl4[:, 0:CW] + jnp.where(
            bd_pair, lax.dot_general(kd4[:, 0:CW], vnb[:, 0:CW], tn, preferred_element_type=F32), 0.0)
        s1[...] = s1[...] * gl4[:, CW:2 * CW] + jnp.where(
            bd_pair, lax.dot_general(kd4[:, CW:2 * CW], vnb[:, CW:2 * CW], tn, preferred_element_type=F32), 0.0)

        z = z_ref[0, rs, :].astype(F32)
        on = jnp.concatenate(
            [p * lax.rsqrt(jnp.mean(p * p, axis=-1, keepdims=True) + EPS) * nw for p in heads(o, DN_D)], axis=1)
        o_ref[0, rs, :] = (on * (z * _sigmoid(z))).astype(BF16)


def _gdn(dqkv, dz, gbc, conv_w, A_log, dt_bias, dn_norm_w, B, S):
    rb = 256
    prm = jnp.zeros((2, LANES), F32)
    prm = prm.at[0, DN_HEADS:2 * DN_HEADS].set(A_log.astype(F32))
    prm = prm.at[1, DN_HEADS:2 * DN_HEADS].set(dt_bias.astype(F32))
    out = pl.pallas_call(
        functools.partial(_gdn_kernel, rb=rb),
        grid=(B, S // rb),
        in_specs=[pl.BlockSpec((1, rb, DN_CONV), lambda b, i: (b, i, 0)),
                  pl.BlockSpec((1, rb, DN_W), lambda b, i: (b, i, 0)),
                  pl.BlockSpec((1, rb, LANES), lambda b, i: (b, i, 0)),
                  pl.BlockSpec((CONV_K, DN_CONV), lambda b, i: (0, 0)),
                  pl.BlockSpec((2, LANES), lambda b, i: (0, 0)),
                  pl.BlockSpec((1, DN_D), lambda b, i: (0, 0))],
        out_specs=pl.BlockSpec((1, rb, DN_W), lambda b, i: (b, i, 0)),
        out_shape=jax.ShapeDtypeStruct((B, S, DN_W), BF16),
        scratch_shapes=[pltpu.VMEM((rb + 8, DN_CONV), F32),
                        pltpu.VMEM((rb, DN_CONV), F32),
                        pltpu.VMEM((2 * DN_D, 2 * DN_D), F32),
                        pltpu.VMEM((2 * DN_D, 2 * DN_D), F32)],
        compiler_params=_cparams(("arbitrary", "arbitrary")),
        name="gdn",
    )(dqkv.reshape(B, S, DN_CONV), dz.reshape(B, S, DN_W), gbc.reshape(B, S, LANES),
      conv_w, prm, dn_norm_w.reshape(1, DN_D))
    return out.reshape(B * S, DN_W)


def _out_kernel(o1_ref, o2_ref, o3_ref, l1_ref, l2_ref, l3_ref, dn_ref, x_ref, mod_ref, wo_ref,
                n2_ref, wr_ref, br_ref,
                x1_ref, h2_ref, te_ref, rk_ref, gtc_ref, cnt_ref, base, *, tm):
    i = pl.program_id(0)

    @pl.when(i == 0)
    def _():
        base[...] = jnp.zeros_like(base)

    l1, l2, l3 = l1_ref[...], l2_ref[...], l3_ref[...]
    mx = jnp.maximum(jnp.maximum(l1, l2), l3)
    e1, e2, e3 = jnp.exp(l1 - mx), jnp.exp(l2 - mx), jnp.exp(l3 - mx)
    zs = e1 + e2 + e3
    er = lax.broadcasted_iota(I32, (LANES, ATTN_W), 0)
    ec = lax.broadcasted_iota(I32, (LANES, ATTN_W), 1)
    expand = jnp.where(ec // ATTN_HD == er, 1.0, 0.0).astype(BF16)
    attn = jnp.zeros((tm, ATTN_W), F32)
    for e, o_ref in ((e1, o1_ref), (e2, o2_ref), (e3, o3_ref)):
        wgt = jnp.dot((e / zs).astype(BF16), expand, preferred_element_type=F32)
        attn = attn + wgt * o_ref[...].astype(F32)
    mix = (jnp.dot(attn.astype(BF16), wo_ref[0:ATTN_W, :], preferred_element_type=F32)
           + jnp.dot(dn_ref[...], wo_ref[ATTN_W:, :], preferred_element_type=F32))
    x1 = x_ref[...] + mod_ref[0, 2:3, :] * mix
    x1_ref[...] = x1
    ms = jnp.mean(x1 * x1, axis=-1, keepdims=True)
    h2 = x1 * lax.rsqrt(ms + EPS) * n2_ref[...]
    h2 = h2 * (1.0 + mod_ref[0, 4:5, :]) + mod_ref[0, 3:4, :]
    _store_slabs(h2_ref, h2)

    lg = _nt(wr_ref[...], h2, precision=HI) + br_ref[...]
    eidx = lax.broadcasted_iota(I32, (N_EXPERTS, tm), 0)
    vals, idxs, sels = [], [], []
    for _ in range(TOP_K):
        m = jnp.max(lg, axis=0, keepdims=True)
        idx = jnp.min(jnp.where(lg == m, eidx, N_EXPERTS), axis=0, keepdims=True)
        sel = eidx == idx
        vals.append(m)
        idxs.append(idx)
        sels.append(sel)
        lg = jnp.where(sel, -jnp.inf, lg)
    ex = [jnp.exp(v - vals[0]) for v in vals]
    den = ex[0] + ex[1] + ex[2] + ex[3]
    gates = [e / den for e in ex]

    msum = jnp.zeros((N_EXPERTS, tm), F32)
    for sel in sels:
        msum = msum + jnp.where(sel, 1.0, 0.0)
    tr = lax.broadcasted_iota(I32, (tm, tm), 0)
    tc = lax.broadcasted_iota(I32, (tm, tm), 1)
    upper = jnp.where(tr <= tc, 1.0, 0.0).astype(BF16)
    incl = jnp.dot(msum.astype(BF16), upper, preferred_element_type=F32)
    pos = base[:, 0:1] + (incl - msum)
    sub8 = lax.broadcasted_iota(I32, (8, tm), 0)
    te = jnp.zeros((8, tm), I32)
    rk = jnp.zeros((8, tm), I32)
    gt = jnp.zeros((8, tm), F32)
    for k in range(TOP_K):
        rank_k = jnp.sum(jnp.where(sels[k], pos, 0.0), axis=0, keepdims=True).astype(I32)
        te = jnp.where(sub8 == k, idxs[k], te)
        rk = jnp.where(sub8 == k, rank_k, rk)
        gt = jnp.where(sub8 == k, gates[k], gt)
    te_ref[...] = te
    rk_ref[...] = rk
    gtc_ref[...] = jnp.transpose(jnp.concatenate([gt, jnp.zeros((LANES - 8, tm), F32)], axis=0))
    base[...] = base[...] + jnp.sum(msum, axis=1, keepdims=True)
    cnt_ref[...] = base[...].astype(I32)


def _outproj(o1, o2, o3, l1, l2, l3, dn, x2, mod, w_out, norm2_w, w_router, b_router, S):
    T, D = x2.shape
    tm = 512
    row = lambda w: pl.BlockSpec((tm, w), lambda i: (i, 0))
    colb = pl.BlockSpec((8, tm), lambda i: (0, i))
    return pl.pallas_call(
        functools.partial(_out_kernel, tm=tm),
        grid=(T // tm,),
        in_specs=[row(ATTN_W), row(ATTN_W), row(ATTN_W), row(LANES), row(LANES), row(LANES),
                  row(DN_W), row(D),
                  pl.BlockSpec((1, 6, D), lambda i: (i * tm // S, 0, 0)),
                  pl.BlockSpec((D, D), lambda i: (0, 0)),
                  pl.BlockSpec((1, D), lambda i: (0, 0)),
                  pl.BlockSpec((N_EXPERTS, D), lambda i: (0, 0)),
                  pl.BlockSpec((N_EXPERTS, 1), lambda i: (0, 0))],
        out_specs=[row(D), pl.BlockSpec((tm * (D // LANES), LANES), lambda i: (i, 0)), colb, colb,
                   row(LANES), pl.BlockSpec((N_EXPERTS, LANES), lambda i: (0, 0))],
        out_shape=[jax.ShapeDtypeStruct((T, D), F32),
                   jax.ShapeDtypeStruct((T * (D // LANES), LANES), F32),
                   jax.ShapeDtypeStruct((8, T), I32),
                   jax.ShapeDtypeStruct((8, T), I32),
                   jax.ShapeDtypeStruct((T, LANES), F32),
                   jax.ShapeDtypeStruct((N_EXPERTS, LANES), I32)],
        scratch_shapes=[pltpu.VMEM((N_EXPERTS, LANES), F32)],
        compiler_params=_cparams(("arbitrary",)),
        name="outproj_router",
    )(o1, o2, o3, l1, l2, l3, dn, x2, mod, w_out.astype(BF16), norm2_w.reshape(1, D),
      jnp.transpose(w_router), b_router.reshape(N_EXPERTS, 1))


def _dest_kernel(ps_ref, te_ref, rk_ref, d_ref):
    te = te_ref[...]
    acc = jnp.zeros(te.shape, I32)
    for e in range(N_EXPERTS):
        acc = jnp.where(te == e, ps_ref[e], acc)
    d_ref[...] = acc + rk_ref[...]


def _dest(pstart, te, rk):
    T = te.shape[1]
    tb = 2048
    return pl.pallas_call(
        _dest_kernel,
        grid_spec=pltpu.PrefetchScalarGridSpec(
            num_scalar_prefetch=1,
            grid=(T // tb,),
            in_specs=[pl.BlockSpec((8, tb), lambda i, ps: (0, i)),
                      pl.BlockSpec((8, tb), lambda i, ps: (0, i))],
            out_specs=pl.BlockSpec((8, tb), lambda i, ps: (0, i))),
        out_shape=jax.ShapeDtypeStruct((8, T), I32),
        compiler_params=_cparams(("arbitrary",)),
        name="dest_rows",
    )(pstart, te, rk)


SC_CORES = 2
SC_SUBCORES = 16
SC_IDX_CHUNK = 128


def _invperm(dest_flat, P):
    N = dest_flat.shape[0]
    nw = SC_CORES * SC_SUBCORES
    nch = N // (nw * SC_IDX_CHUNK)
    mesh = plsc.VectorSubcoreMesh(core_axis_name="c", subcore_axis_name="s",
                                  num_cores=SC_CORES, num_subcores=SC_SUBCORES)

    @functools.partial(
        pl.kernel, mesh=mesh, out_type=jax.ShapeDtypeStruct((P,), I32),
        scratch_types=[pltpu.VMEM((nch, SC_IDX_CHUNK), I32), pltpu.VMEM((nch, SC_IDX_CHUNK), I32)])
    def scatter_codes(idx_hbm, val_hbm, out_hbm, idx_v, val_v):
        wid = lax.axis_index("s") * SC_CORES + lax.axis_index("c")
        pltpu.sync_copy(idx_hbm.at[wid], idx_v)
        pltpu.sync_copy(val_hbm.at[wid], val_v)

        @pl.loop(0, nch)
        def _(j):
            pltpu.sync_copy(val_v.at[j], out_hbm.at[idx_v.at[j]])

    vals = jnp.arange(N, dtype=I32)
    return scatter_codes(dest_flat.reshape(nw, nch, SC_IDX_CHUNK), vals.reshape(nw, nch, SC_IDX_CHUNK))


def _row_copy(src, dst, sem):
    return pltpu.make_async_copy(src, dst, sem)


def _moe_kernel(be_ref, nv_ref, cc_ref, cn_ref, h2_ref, w1_ref, b1_ref, w2_ref, b2_ref, y4_ref,
                xbuf, ybuf, w1b, w2b, gsem, ssem, *, F, T, D, tme, nb):
    i = pl.program_id(0)
    s = i % 2
    ns = D // LANES

    def rows(first, n):
        return pl.ds(pl.multiple_of(first * ns, ns), n * ns)

    def gather_copy(tok, p, slot):
        return _row_copy(h2_ref.at[rows(tok, 1)], xbuf.at[slot, rows(p, 1)], gsem.at[slot])

    def token_of(code):
        return code & (T - 1) if T & (T - 1) == 0 else code % T

    def issue_rows(start_row, nv):
        ng = nv // ROW_UNROLL

        def grp(g, c):
            for j in range(ROW_UNROLL):
                start_row(g * ROW_UNROLL + j)
            return c
        lax.fori_loop(0, ng, grp, 0)

        def one(p, c):
            start_row(p)
            return c
        lax.fori_loop(ng * ROW_UNROLL, nv, one, 0)

    def start_gather(code_ref, nv, slot):
        issue_rows(lambda p: gather_copy(token_of(code_ref[0, 0, p]), p, slot).start(), nv)

    def wait_rows(make, nv):
        @pl.when(nv > 0)
        def _():
            make(nv).wait()

    def gathered(n, slot):
        return _row_copy(h2_ref.at[rows(0, n)], xbuf.at[slot, rows(0, n)], gsem.at[slot])

    def scattered(n, slot):
        return _row_copy(ybuf.at[slot, rows(0, n)], y4_ref.at[rows(0, n)], ssem.at[slot])

    @pl.when(i == 0)
    def _():
        xbuf[...] = jnp.zeros_like(xbuf)
        start_gather(cc_ref, nv_ref[0], 0)

    nv = nv_ref[i]
    wait_rows(lambda n: gathered(n, s), nv)

    @pl.when(i + 1 < nb)
    def _():
        start_gather(cn_ref, nv_ref[i + 1], 1 - s)

    @pl.when(i >= 2)
    def _():
        wait_rows(lambda n: scattered(n, s), nv_ref[i - 2])

    @pl.when(jnp.logical_or(i == 0, be_ref[i] != be_ref[jnp.maximum(i - 1, 0)]))
    def _():
        w1b[...] = w1_ref[0].astype(BF16)
        w2b[...] = w2_ref[0].astype(BF16)

    @pl.when(nv > 0)
    def _():
        x = _load_slabs(xbuf, tme, D, lead=(s,)).astype(BF16)
        hgu = jnp.dot(x, w1b[...], preferred_element_type=F32) + b1_ref[0]
        gate = jnp.minimum(hgu[:, :F], SWIGLU_LIMIT)
        up = jnp.clip(hgu[:, F:], -SWIGLU_LIMIT, SWIGLU_LIMIT)
        act = gate * _sigmoid(SWIGLU_ALPHA * gate) * (up + 1.0)
        y = jnp.dot(act.astype(BF16), w2b[...], preferred_element_type=F32) + b2_ref[0]
        _store_slabs(ybuf, y, lead=(s,))

        issue_rows(lambda p: _row_copy(ybuf.at[s, rows(p, 1)], y4_ref.at[rows(cc_ref[0, 0, p], 1)],
                                       ssem.at[s]).start(), nv)

    @pl.when(i == nb - 1)
    def _():
        wait_rows(lambda n: scattered(n, s), nv)
        if nb > 1:
            wait_rows(lambda n: scattered(n, 1 - s), nv_ref[i - 1])


def _experts(blk_exp, blk_valid, codes, h2s, w1, b1, w2, b2, tme):
    E, D, F2 = w1.shape
    F = F2 // 2
    ns = D // LANES
    T = h2s.shape[0] // ns
    nb = blk_exp.shape[0]
    codes3 = codes.reshape(nb, 1, tme)
    wspec = lambda shape: pl.BlockSpec(shape, lambda i, be, nv: (be[i], 0, 0))
    return pl.pallas_call(
        functools.partial(_moe_kernel, F=F, T=T, D=D, tme=tme, nb=nb),
        grid_spec=pltpu.PrefetchScalarGridSpec(
            num_scalar_prefetch=2,
            grid=(nb,),
            in_specs=[pl.BlockSpec((1, 1, tme), lambda i, be, nv: (i, 0, 0), memory_space=pltpu.SMEM),
                      pl.BlockSpec((1, 1, tme), lambda i, be, nv: (jnp.minimum(i + 1, nb - 1), 0, 0),
                                   memory_space=pltpu.SMEM),
                      pl.BlockSpec(memory_space=pl.ANY),
                      wspec((1, D, F2)), wspec((1, 1, F2)), wspec((1, F, D)), wspec((1, 1, D))],
            out_specs=pl.BlockSpec(memory_space=pl.ANY),
            scratch_shapes=[pltpu.VMEM((2, tme * ns, LANES), F32), pltpu.VMEM((2, tme * ns, LANES), F32),
                            pltpu.VMEM((D, F2), BF16), pltpu.VMEM((F, D), BF16),
                            pltpu.SemaphoreType.DMA((2,)), pltpu.SemaphoreType.DMA((2,))]),
        out_shape=jax.ShapeDtypeStruct((TOP_K * T * ns, LANES), F32),
        compiler_params=_cparams(("arbitrary",)),
        name="experts",
    )(blk_exp, blk_valid, codes3, codes3, h2s, w1, b1.reshape(E, 1, F2), w2, b2.reshape(E, 1, D))


def _comb_kernel(g_ref, x1_ref, mod_ref, fw_ref, y0_ref, y1_ref, y2_ref, y3_ref, o_ref):
    g = g_ref[...]
    n, d = x1_ref.shape
    y = g[:, 0:1] * _load_slabs(y0_ref, n, d)
    for k, y_ref in ((1, y1_ref), (2, y2_ref), (3, y3_ref)):
        y = y + g[:, k:k + 1] * _load_slabs(y_ref, n, d)
    x2 = x1_ref[...] + mod_ref[0, 5:6, :] * y
    ms = jnp.mean(x2 * x2, axis=-1, keepdims=True)
    o_ref[...] = x2 * lax.rsqrt(ms + EPS) * fw_ref[...]


def _combine(gtc, x1, mod, final_w, y4, S):
    T, D = x1.shape
    tmc = 512
    nt = T // tmc
    yspec = lambda k: pl.BlockSpec((tmc * (D // LANES), LANES), lambda i: (k * nt + i, 0))
    return pl.pallas_call(
        _comb_kernel,
        grid=(nt,),
        in_specs=[pl.BlockSpec((tmc, LANES), lambda i: (i, 0)),
                  pl.BlockSpec((tmc, D), lambda i: (i, 0)),
                  pl.BlockSpec((1, 6, D), lambda i: (i * tmc // S, 0, 0)),
                  pl.BlockSpec((1, D), lambda i: (0, 0)),
                  yspec(0), yspec(1), yspec(2), yspec(3)],
        out_specs=pl.BlockSpec((tmc, D), lambda i: (i, 0)),
        out_shape=jax.ShapeDtypeStruct((T, D), F32),
        compiler_params=_cparams(("arbitrary",)),
        name="combine",
    )(gtc, x1, mod, final_w.reshape(1, D), y4, y4, y4, y4)


def _layer(x2, mod, B, S, norm1_w, w_in, conv_w, A_log, dt_bias, dn_norm_w, w_out,
           norm2_w, w_router, b_router, w1, b1, w2, b2):
    T, D = x2.shape
    qkv, dqkv, dz, gbc = _inproj(x2, mod, norm1_w, w_in, S)
    branches = [_attn_branch(qkv, B, S, d) for d in DILATIONS]
    dn = _gdn(dqkv, dz, gbc, conv_w, A_log, dt_bias, dn_norm_w, B, S)
    (o1, l1), (o2, l2), (o3, l3) = branches
    x1, h2, te, rk, gtc, cnt = _outproj(o1, o2, o3, l1, l2, l3, dn, x2, mod, w_out, norm2_w,
                                        w_router, b_router, S)
    tme = MOE_ROWS
    P = T * TOP_K + N_EXPERTS * tme
    counts = cnt[:, 0]
    padded = (counts + tme - 1) // tme * tme
    pend = jnp.cumsum(padded)
    pstart = (pend - padded).astype(I32)
    blk_start = jnp.arange(P // tme, dtype=I32) * tme
    blk_exp = jnp.minimum(jnp.sum((pend[None, :] <= blk_start[:, None]).astype(I32), axis=1),
                          N_EXPERTS - 1).astype(I32)
    blk_valid = jnp.clip(pstart[blk_exp] + counts[blk_exp] - blk_start, 0, tme).astype(I32)
    dest = _dest(pstart, te, rk)
    codes = _invperm(dest[:TOP_K].reshape(TOP_K * T), P)
    y4 = _experts(blk_exp, blk_valid, codes, h2, w1, b1, w2, b2, tme)
    return x1, gtc, y4


def kernel(x, c, w_ada, b_ada, norm1_w, w_in, conv_w, A_log, dt_bias, dn_norm_w, w_out, norm2_w,
           w_router, b_router, w1, b1, w2, b2, final_norm_w):
    B, S, D = x.shape
    depth = w_ada.shape[0]
    assert S % (ATTN_BLK * DILATIONS[-1]) == 0 and depth == 1
    x2 = x.reshape(B * S, D)
    mod = _ada(c, w_ada[0], b_ada[0])
    x1, gtc, y4 = _layer(x2, mod, B, S, norm1_w[0], w_in[0], conv_w[0], A_log[0], dt_bias[0],
                         dn_norm_w[0], w_out[0], norm2_w[0], w_router[0], b_router[0],
                         w1[0], b1[0], w2[0], b2[0])
    out = _combine(gtc, x1, mod, final_norm_w, y4, S)
    return out.reshape(B, S, D)
```

```python
import functools

import jax
import jax.numpy as jnp
from jax import lax
from jax.experimental import pallas as pl
from jax.experimental.pallas import tpu as pltpu
from jax.experimental.pallas import tpu_sc as plsc

F32 = jnp.float32
BF16 = jnp.bfloat16
I32 = jnp.int32
HI = lax.Precision.HIGHEST

LANES = 128
ATTN_HEADS = 8
ATTN_HD = 64
ATTN_W = ATTN_HEADS * ATTN_HD
ATTN_BLK = 128
DILATIONS = (1, 4, 16)
DN_HEADS = 4
DN_D = 128
DN_W = DN_HEADS * DN_D
DN_CONV = 3 * DN_W
CONV_K = 4
DN_CHUNK = 64
N_EXPERTS = 32
TOP_K = 4
MOE_ROWS = 512
ROW_UNROLL = 8
SWIGLU_LIMIT = 7.0
SWIGLU_ALPHA = 1.702
EPS = 1e-6
NEG = -1e30
MAIN_COLS = 3 * ATTN_W + DN_CONV + DN_W

VMEM_LIMIT = 56 * 1024 * 1024


def _cparams(sem):
    return pltpu.CompilerParams(dimension_semantics=sem, vmem_limit_bytes=VMEM_LIMIT)


def _nt(a, b, **kw):
    return lax.dot_general(a, b, (((1,), (1,)), ((), ())), preferred_element_type=F32, **kw)


def _sigmoid(x):
    return 1.0 / (1.0 + jnp.exp(-x))


def _store_slabs(ref, val, lead=()):
    n, d = val.shape
    ns = d // LANES
    for c in range(ns):
        ref[lead + (pl.ds(c, n, stride=ns), slice(None))] = val[:, c * LANES:(c + 1) * LANES]


def _load_slabs(ref, n, d, lead=()):
    ns = d // LANES
    return jnp.concatenate([ref[lead + (pl.ds(c, n, stride=ns), slice(None))] for c in range(ns)], axis=1)


def _ada_kernel(c_ref, w_ref, b_ref, o_ref):
    c = c_ref[...]
    cond = c * _sigmoid(c)
    o_ref[...] = jnp.dot(cond, w_ref[...], preferred_element_type=F32, precision=HI) + b_ref[...]


def _ada(c, w_ada, b_ada):
    B, D = c.shape
    N = w_ada.shape[1]
    cp = jnp.zeros((8, D), F32).at[:B].set(c)
    tn = 1024
    out = pl.pallas_call(
        _ada_kernel,
        grid=(N // tn,),
        in_specs=[pl.BlockSpec((8, D), lambda j: (0, 0)),
                  pl.BlockSpec((D, tn), lambda j: (0, j)),
                  pl.BlockSpec((1, tn), lambda j: (0, j))],
        out_specs=pl.BlockSpec((8, tn), lambda j: (0, j)),
        out_shape=jax.ShapeDtypeStruct((8, N), F32),
        compiler_params=_cparams(("arbitrary",)),
        name="ada",
    )(cp, w_ada, b_ada.reshape(1, N))
    return out[:B].reshape(B, 6, D)


def _inproj_kernel(x_ref, mod_ref, nw_ref, wm_ref, ws_ref, qkv_ref, dqkv_ref, dz_ref, gbc_ref):
    x = x_ref[...]
    shift = mod_ref[0, 0:1, :]
    scale = mod_ref[0, 1:2, :]
    ms = jnp.mean(x * x, axis=-1, keepdims=True)
    h = x * lax.rsqrt(ms + EPS) * nw_ref[...]
    hb = (h * (1.0 + scale) + shift).astype(BF16)
    for j in range(3):
        r = jnp.dot(hb, wm_ref[:, j * ATTN_W:(j + 1) * ATTN_W], preferred_element_type=F32)
        if j == 0:
            r = r * (ATTN_HD ** -0.5)
        qkv_ref[:, j * ATTN_W:(j + 1) * ATTN_W] = r.astype(BF16)
    for j in range(3):
        c0 = 3 * ATTN_W + j * DN_W
        r = jnp.dot(hb, wm_ref[:, c0:c0 + DN_W], preferred_element_type=F32)
        dqkv_ref[:, j * DN_W:(j + 1) * DN_W] = r.astype(BF16)
    c0 = 3 * ATTN_W + DN_CONV
    dz_ref[...] = jnp.dot(hb, wm_ref[:, c0:c0 + DN_W], preferred_element_type=F32).astype(BF16)
    gbc_ref[...] = jnp.dot(hb, ws_ref[...], preferred_element_type=F32)


def _inproj(x2, mod, norm_w, w_in, S):
    T, D = x2.shape
    tm = 512
    wm = w_in[:, :MAIN_COLS].astype(BF16)
    ws = jnp.zeros((D, LANES), F32).at[:, :2 * DN_HEADS].set(w_in[:, MAIN_COLS:]).astype(BF16)
    return pl.pallas_call(
        _inproj_kernel,
        grid=(T // tm,),
        in_specs=[pl.BlockSpec((tm, D), lambda i: (i, 0)),
                  pl.BlockSpec((1, 6, D), lambda i: (i * tm // S, 0, 0)),
                  pl.BlockSpec((1, D), lambda i: (0, 0)),
                  pl.BlockSpec((D, MAIN_COLS), lambda i: (0, 0)),
                  pl.BlockSpec((D, LANES), lambda i: (0, 0))],
        out_specs=[pl.BlockSpec((tm, 3 * ATTN_W), lambda i: (i, 0)),
                   pl.BlockSpec((tm, DN_CONV), lambda i: (i, 0)),
                   pl.BlockSpec((tm, DN_W), lambda i: (i, 0)),
                   pl.BlockSpec((tm, LANES), lambda i: (i, 0))],
        out_shape=[jax.ShapeDtypeStruct((T, 3 * ATTN_W), BF16),
                   jax.ShapeDtypeStruct((T, DN_CONV), BF16),
                   jax.ShapeDtypeStruct((T, DN_W), BF16),
                   jax.ShapeDtypeStruct((T, LANES), F32)],
        compiler_params=_cparams(("arbitrary",)),
        name="inproj",
    )(x2, mod, norm_w.reshape(1, D), wm, ws)


def _attn_kernel(q_ref, kc_ref, kp_ref, vc_ref, vp_ref, o_ref, lse_ref, kf, vf, *, qb):
    n = pl.program_id(2)
    kf[0:ATTN_BLK, :] = kp_ref[0]
    kf[ATTN_BLK:, :] = kc_ref[0]
    vf[0:ATTN_BLK, :] = vp_ref[0]
    vf[ATTN_BLK:, :] = vc_ref[0]
    row = lax.broadcasted_iota(I32, (ATTN_BLK, 2 * ATTN_BLK), 0)
    col = lax.broadcasted_iota(I32, (ATTN_BLK, 2 * ATTN_BLK), 1)
    band = jnp.logical_or(jnp.logical_and(col < ATTN_BLK, col >= row),
                          jnp.logical_and(col >= ATTN_BLK, col - ATTN_BLK <= row))
    lane = lax.broadcasted_iota(I32, (ATTN_BLK, LANES), 1)
    lo = lane < ATTN_HD

    def sub(j, carry):
        r0 = pl.multiple_of(j * ATTN_BLK, ATTN_BLK)
        first_col = jnp.where(jnp.logical_and(n == 0, j == 0), ATTN_BLK, 0)
        mask = jnp.logical_and(band, col >= first_col)
        lse_tile = jnp.zeros((ATTN_BLK, LANES), F32)
        for hp in range(ATTN_W // LANES):
            cs = slice(hp * LANES, (hp + 1) * LANES)
            q2 = q_ref[0, pl.ds(r0, ATTN_BLK), cs]
            k2 = kf[pl.ds(r0, 2 * ATTN_BLK), cs]
            v2 = vf[pl.ds(r0, 2 * ATTN_BLK), cs]
            outs = []
            for half in range(2):
                qm = jnp.where(lo if half == 0 else jnp.logical_not(lo), q2, jnp.zeros_like(q2))
                s = _nt(qm, k2)
                s = jnp.where(mask, s, NEG)
                m = jnp.max(s, axis=-1, keepdims=True)
                p = jnp.exp(s - m)
                den = jnp.sum(p, axis=-1, keepdims=True)
                acc = jnp.dot(p.astype(BF16), v2, preferred_element_type=F32)
                outs.append(acc / den)
                lse_tile = jnp.where(lane == 2 * hp + half, m + jnp.log(den), lse_tile)
            o_ref[0, pl.ds(r0, ATTN_BLK), cs] = jnp.where(lo, outs[0], outs[1]).astype(BF16)
        lse_ref[0, pl.ds(r0, ATTN_BLK), :] = lse_tile
        return carry

    lax.fori_loop(0, qb // ATTN_BLK, sub, 0)


def _attn_branch(qkv, B, S, d):
    L = S // d
    qb = min(512, L)
    nsub = qb // ATTN_BLK
    W3 = 3 * ATTN_W
    view = qkv.reshape(B, L, d * W3)
    cur = lambda c: pl.BlockSpec((1, qb, ATTN_W), lambda b, r, n: (b, n, 3 * r + c))
    prev = lambda c: pl.BlockSpec((1, ATTN_BLK, ATTN_W),
                                  lambda b, r, n: (b, jnp.maximum(n * nsub - 1, 0), 3 * r + c))
    o, lse = pl.pallas_call(
        functools.partial(_attn_kernel, qb=qb),
        grid=(B, d, L // qb),
        in_specs=[cur(0), cur(1), prev(1), cur(2), prev(2)],
        out_specs=[pl.BlockSpec((1, qb, ATTN_W), lambda b, r, n: (b, n, r)),
                   pl.BlockSpec((1, qb, LANES), lambda b, r, n: (b, n, r))],
        out_shape=[jax.ShapeDtypeStruct((B, L, d * ATTN_W), BF16),
                   jax.ShapeDtypeStruct((B, L, d * LANES), F32)],
        scratch_shapes=[pltpu.VMEM((qb + ATTN_BLK, ATTN_W), BF16),
                        pltpu.VMEM((qb + ATTN_BLK, ATTN_W), BF16)],
        compiler_params=_cparams(("arbitrary", "arbitrary", "arbitrary")),
        name=f"attn_d{d}",
    )(view, view, view, view, view)
    return o.reshape(B * S, ATTN_W), lse.reshape(B * S, LANES)


def _gdn_kernel(x_ref, z_ref, g_ref, cw_ref, prm_ref, nw_ref, o_ref, xext, yc, s0, s1, *, rb):
    i = pl.program_id(1)

    @pl.when(i == 0)
    def _():
        xext[0:8, :] = jnp.zeros((8, DN_CONV), F32)
        s0[...] = jnp.zeros_like(s0)
        s1[...] = jnp.zeros_like(s1)

    @pl.when(i > 0)
    def _():
        xext[0:8, :] = xext[rb:rb + 8, :]

    xext[8:, :] = x_ref[0].astype(F32)
    y = cw_ref[CONV_K - 1:CONV_K, :] * xext[8:8 + rb, :]
    for j in range(CONV_K - 1):
        off = 8 - (CONV_K - 1) + j
        y = y + cw_ref[j:j + 1, :] * xext[off:off + rb, :]
    yc[...] = y * _sigmoid(y)

    C = DN_CHUNK
    H = DN_HEADS
    CW = H * C
    dot = functools.partial(jnp.dot, preferred_element_type=F32)

    def iota(shape, d):
        return lax.broadcasted_iota(I32, shape, d)

    ltri_b = jnp.where(iota((C, C), 0) >= iota((C, C), 1), 1.0, 0.0).astype(BF16)
    lane = iota((C, LANES), 1)
    blane = lane < H
    glane = jnp.logical_and(lane >= H, lane < 2 * H)
    e512 = jnp.where(jnp.logical_or(iota((LANES, DN_W), 1) // DN_D == iota((LANES, DN_W), 0),
                                    iota((LANES, DN_W), 1) // DN_D == iota((LANES, DN_W), 0) - H),
                     1.0, 0.0).astype(BF16)
    e256 = jnp.where(iota((LANES, CW), 1) // C == iota((LANES, CW), 0) - H, 1.0, 0.0).astype(BF16)
    row4 = iota((C, CW), 0)
    col4 = iota((C, CW), 1) % C
    eye4 = jnp.where(row4 == col4, 1.0, 0.0).astype(F32)
    blk = [iota((C, CW), 1) // C == h for h in range(H)]
    bd_cc = iota((CW, CW), 0) // C == iota((CW, CW), 1) // C
    bd_pair = iota((CW, CW), 0) // DN_D == iota((CW, CW), 1) // DN_D
    rt_mask = iota((CW, DN_W), 0) // C == iota((CW, DN_W), 1) // DN_D
    neg_a = -jnp.exp(prm_ref[0:1, :])
    dtb = prm_ref[1:2, :]
    nw = nw_ref[...]

    def hilo(x):
        hi = x.astype(BF16)
        return hi, (x - hi.astype(F32)).astype(BF16)

    def heads(a, w):
        return [a[:, h * w:(h + 1) * w] for h in range(H)]

    def l2n(a, mult):
        return jnp.concatenate(
            [p * (lax.rsqrt(jnp.sum(p * p, axis=-1, keepdims=True) + EPS) * mult) for p in heads(a, DN_D)],
            axis=1)

    def stack4(a):
        return jnp.concatenate([a, a, a, a], axis=0)

    zb = jnp.zeros((), BF16)
    nchunk = rb // C
    chunks = []
    for c in range(nchunk):
        rs = slice(c * C, (c + 1) * C)
        G = g_ref[0, rs, :]
        xg = G + dtb
        gv = jnp.where(glane, neg_a * (jnp.maximum(xg, 0.0) + jnp.log1p(jnp.exp(-jnp.abs(xg)))), 0.0)
        be = jnp.where(blane, _sigmoid(G), 0.0)
        g_hi, g_lo = hilo(gv)
        gcum = dot(ltri_b, g_hi) + dot(ltri_b, g_lo)
        gtot = gcum[C - 1:C, :]
        eg = jnp.where(glane, jnp.exp(gcum), 0.0)
        ek = jnp.where(glane, jnp.exp(gtot - gcum), 0.0)
        ex = dot(jnp.concatenate([be, eg, ek], axis=0).astype(BF16), e512)
        bexp, egexp, ekexp = ex[0:C], ex[C:2 * C], ex[2 * C:3 * C]
        gexp = dot(g_hi, e256) + dot(g_lo, e256)
        d_hi, d_lo = hilo(jnp.where(row4 > col4, gexp, 0.0))
        diff = dot(ltri_b, d_hi) + dot(ltri_b, d_lo)
        decay = jnp.exp(jnp.where(row4 >= col4, diff, NEG))

        q4 = l2n(yc[rs, 0:DN_W], DN_D ** -0.5)
        k4 = l2n(yc[rs, DN_W:2 * DN_W], 1.0)
        v4 = yc[rs, 2 * DN_W:3 * DN_W]
        kb4 = k4 * bexp
        vb4 = v4 * bexp
        rt = jnp.where(rt_mask, stack4(k4.astype(BF16)), jnp.zeros((), BF16))
        ai = _nt(jnp.concatenate([kb4, q4], axis=0).astype(BF16), rt)
        a4 = jnp.where(row4 > col4, ai[0:C] * decay, 0.0)
        pb = a4.astype(BF16)
        chunks.append(dict(
            rs=rs, q4=q4, k4=k4, kb4=kb4, vb4=vb4, egexp=egexp, ekexp=ekexp,
            intra=ai[C:2 * C] * decay, t4=eye4 - a4, pb=pb,
            bd=jnp.where(bd_cc, stack4(pb), zb)))

    for _ in range(5):
        for ch in chunks:
            ch["pb"] = dot(ch["pb"], ch["bd"]).astype(BF16)
        for ch in chunks:
            ch["bd"] = jnp.where(bd_cc, stack4(ch["pb"]), zb)
        for ch in chunks:
            ch["t4"] = ch["t4"] + dot(ch["t4"].astype(BF16), ch["bd"])

    for ch in chunks:
        q4, k4, kb4, vb4, egexp, ekexp = (ch[n] for n in ("q4", "k4", "kb4", "vb4", "egexp", "ekexp"))
        t4b = ch["t4"].astype(BF16)
        lstk = jnp.concatenate([jnp.where(blk[h], t4b, zb) for h in range(H)], axis=0)
        kbg4 = kb4 * egexp
        rstk = jnp.concatenate(
            [jnp.concatenate([vb, kbg], axis=1) for vb, kbg in zip(heads(vb4, DN_D), heads(kbg4, DN_D))],
            axis=0).astype(BF16)
        uw = dot(lstk, rstk)
        u4 = jnp.concatenate([uw[h * C:(h + 1) * C, 0:DN_D] for h in range(H)], axis=1)
        w4 = jnp.concatenate([uw[h * C:(h + 1) * C, DN_D:2 * DN_D] for h in range(H)], axis=1)
        ib = ch["intra"].astype(BF16)
        ch.update(
            u4=u4, wq=jnp.concatenate([w4, q4 * egexp], axis=0).astype(BF16),
            kd4=(k4 * ekexp).astype(BF16), gl4=egexp[C - 1:C, :],
            lint=jnp.concatenate([jnp.where(blk[h], ib, zb) for h in range(H)], axis=0))

    for ch in chunks:
        rs, u4, wq, kd4, gl4, lint = (ch[n] for n in ("rs", "u4", "wq", "kd4", "gl4", "lint"))
        ra = dot(wq[:, 0:CW], s0[...].astype(BF16))
        rc = dot(wq[:, CW:2 * CW], s1[...].astype(BF16))
        vn = u4 - jnp.concatenate([ra[0:C], rc[0:C]], axis=1)
        vnb = vn.astype(BF16)
        oi = dot(lint, jnp.concatenate(heads(vnb, DN_D), axis=0))
        o = (jnp.concatenate([ra[C:2 * C], rc[C:2 * C]], axis=1)
             + jnp.concatenate([oi[h * C:(h + 1) * C] for h in range(H)], axis=1))
        tn = (((0,), (0,)), ((), ()))
        s0[...] = s0[...] * gl4[:, 0:CW] + jnp.where(
            bd_pair, lax.dot_general(kd4[:, 0:CW], vnb[:, 0:CW], tn, preferred_element_type=F32), 0.0)
        s1[...] = s1[...] * gl4[:, CW:2 * CW] + jnp.where(
            bd_pair, lax.dot_general(kd4[:, CW:2 * CW], vnb[:, CW:2 * CW], tn, preferred_element_type=F32), 0.0)

        z = z_ref[0, rs, :].astype(F32)
        on = jnp.concatenate(
            [p * lax.rsqrt(jnp.mean(p * p, axis=-1, keepdims=True) + EPS) * nw for p in heads(o, DN_D)], axis=1)
        o_ref[0, rs, :] = (on * (z * _sigmoid(z))).astype(BF16)


def _gdn(dqkv, dz, gbc, conv_w, A_log, dt_bias, dn_norm_w, B, S):
    rb = 512
    prm = jnp.zeros((2, LANES), F32)
    prm = prm.at[0, DN_HEADS:2 * DN_HEADS].set(A_log.astype(F32))
    prm = prm.at[1, DN_HEADS:2 * DN_HEADS].set(dt_bias.astype(F32))
    out = pl.pallas_call(
        functools.partial(_gdn_kernel, rb=rb),
        grid=(B, S // rb),
        in_specs=[pl.BlockSpec((1, rb, DN_CONV), lambda b, i: (b, i, 0)),
                  pl.BlockSpec((1, rb, DN_W), lambda b, i: (b, i, 0)),
                  pl.BlockSpec((1, rb, LANES), lambda b, i: (b, i, 0)),
                  pl.BlockSpec((CONV_K, DN_CONV), lambda b, i: (0, 0)),
                  pl.BlockSpec((2, LANES), lambda b, i: (0, 0)),
                  pl.BlockSpec((1, DN_D), lambda b, i: (0, 0))],
        out_specs=pl.BlockSpec((1, rb, DN_W), lambda b, i: (b, i, 0)),
        out_shape=jax.ShapeDtypeStruct((B, S, DN_W), BF16),
        scratch_shapes=[pltpu.VMEM((rb + 8, DN_CONV), F32),
                        pltpu.VMEM((rb, DN_CONV), F32),
                        pltpu.VMEM((2 * DN_D, 2 * DN_D), F32),
                        pltpu.VMEM((2 * DN_D, 2 * DN_D), F32)],
        compiler_params=_cparams(("arbitrary", "arbitrary")),
        name="gdn",
    )(dqkv.reshape(B, S, DN_CONV), dz.reshape(B, S, DN_W), gbc.reshape(B, S, LANES),
      conv_w, prm, dn_norm_w.reshape(1, DN_D))
    return out.reshape(B * S, DN_W)


def _out_kernel(o1_ref, o2_ref, o3_ref, l1_ref, l2_ref, l3_ref, dn_ref, x_ref, mod_ref, wo_ref,
                n2_ref, wr_ref, br_ref,
                x1_ref, h2_ref, te_ref, rk_ref, gtc_ref, cnt_ref, base, *, tm):
    i = pl.program_id(0)

    @pl.when(i == 0)
    def _():
        base[...] = jnp.zeros_like(base)

    l1, l2, l3 = l1_ref[...], l2_ref[...], l3_ref[...]
    mx = jnp.maximum(jnp.maximum(l1, l2), l3)
    e1, e2, e3 = jnp.exp(l1 - mx), jnp.exp(l2 - mx), jnp.exp(l3 - mx)
    zs = e1 + e2 + e3
    er = lax.broadcasted_iota(I32, (LANES, ATTN_W), 0)
    ec = lax.broadcasted_iota(I32, (LANES, ATTN_W), 1)
    expand = jnp.where(ec // ATTN_HD == er, 1.0, 0.0).astype(BF16)
    attn = jnp.zeros((tm, ATTN_W), F32)
    for e, o_ref in ((e1, o1_ref), (e2, o2_ref), (e3, o3_ref)):
        wgt = jnp.dot((e / zs).astype(BF16), expand, preferred_element_type=F32)
        attn = attn + wgt * o_ref[...].astype(F32)
    mix = (jnp.dot(attn.astype(BF16), wo_ref[0:ATTN_W, :], preferred_element_type=F32)
           + jnp.dot(dn_ref[...], wo_ref[ATTN_W:, :], preferred_element_type=F32))
    x1 = x_ref[...] + mod_ref[0, 2:3, :] * mix
    x1_ref[...] = x1
    ms = jnp.mean(x1 * x1, axis=-1, keepdims=True)
    h2 = x1 * lax.rsqrt(ms + EPS) * n2_ref[...]
    h2 = h2 * (1.0 + mod_ref[0, 4:5, :]) + mod_ref[0, 3:4, :]
    _store_slabs(h2_ref, h2)

    lg = _nt(wr_ref[...], h2, precision=HI) + br_ref[...]
    eidx = lax.broadcasted_iota(I32, (N_EXPERTS, tm), 0)
    vals, idxs, sels = [], [], []
    for _ in range(TOP_K):
        m = jnp.max(lg, axis=0, keepdims=True)
        idx = jnp.min(jnp.where(lg == m, eidx, N_EXPERTS), axis=0, keepdims=True)
        sel = eidx == idx
        vals.append(m)
        idxs.append(idx)
        sels.append(sel)
        lg = jnp.where(sel, -jnp.inf, lg)
    ex = [jnp.exp(v - vals[0]) for v in vals]
    den = ex[0] + ex[1] + ex[2] + ex[3]
    gates = [e / den for e in ex]

    msum = jnp.zeros((N_EXPERTS, tm), F32)
    for sel in sels:
        msum = msum + jnp.where(sel, 1.0, 0.0)
    tr = lax.broadcasted_iota(I32, (tm, tm), 0)
    tc = lax.broadcasted_iota(I32, (tm, tm), 1)
    upper = jnp.where(tr <= tc, 1.0, 0.0).astype(BF16)
    incl = jnp.dot(msum.astype(BF16), upper, preferred_element_type=F32)
    pos = base[:, 0:1] + (incl - msum)
    sub8 = lax.broadcasted_iota(I32, (8, tm), 0)
    te = jnp.zeros((8, tm), I32)
    rk = jnp.zeros((8, tm), I32)
    gt = jnp.zeros((8, tm), F32)
    for k in range(TOP_K):
        rank_k = jnp.sum(jnp.where(sels[k], pos, 0.0), axis=0, keepdims=True).astype(I32)
        te = jnp.where(sub8 == k, idxs[k], te)
        rk = jnp.where(sub8 == k, rank_k, rk)
        gt = jnp.where(sub8 == k, gates[k], gt)
    te_ref[...] = te
    rk_ref[...] = rk
    gtc_ref[...] = jnp.transpose(jnp.concatenate([gt, jnp.zeros((LANES - 8, tm), F32)], axis=0))
    base[...] = base[...] + jnp.sum(msum, axis=1, keepdims=True)
    cnt_ref[...] = base[...].astype(I32)


def _outproj(o1, o2, o3, l1, l2, l3, dn, x2, mod, w_out, norm2_w, w_router, b_router, S):
    T, D = x2.shape
    tm = 512
    row = lambda w: pl.BlockSpec((tm, w), lambda i: (i, 0))
    colb = pl.BlockSpec((8, tm), lambda i: (0, i))
    return pl.pallas_call(
        functools.partial(_out_kernel, tm=tm),
        grid=(T // tm,),
        in_specs=[row(ATTN_W), row(ATTN_W), row(ATTN_W), row(LANES), row(LANES), row(LANES),
                  row(DN_W), row(D),
                  pl.BlockSpec((1, 6, D), lambda i: (i * tm // S, 0, 0)),
                  pl.BlockSpec((D, D), lambda i: (0, 0)),
                  pl.BlockSpec((1, D), lambda i: (0, 0)),
                  pl.BlockSpec((N_EXPERTS, D), lambda i: (0, 0)),
                  pl.BlockSpec((N_EXPERTS, 1), lambda i: (0, 0))],
        out_specs=[row(D), pl.BlockSpec((tm * (D // LANES), LANES), lambda i: (i, 0)), colb, colb,
                   row(LANES), pl.BlockSpec((N_EXPERTS, LANES), lambda i: (0, 0))],
        out_shape=[jax.ShapeDtypeStruct((T, D), F32),
                   jax.ShapeDtypeStruct((T * (D // LANES), LANES), F32),
                   jax.ShapeDtypeStruct((8, T), I32),
                   jax.ShapeDtypeStruct((8, T), I32),
                   jax.ShapeDtypeStruct((T, LANES), F32),
                   jax.ShapeDtypeStruct((N_EXPERTS, LANES), I32)],
        scratch_shapes=[pltpu.VMEM((N_EXPERTS, LANES), F32)],
        compiler_params=_cparams(("arbitrary",)),
        name="outproj_router",
    )(o1, o2, o3, l1, l2, l3, dn, x2, mod, w_out.astype(BF16), norm2_w.reshape(1, D),
      jnp.transpose(w_router), b_router.reshape(N_EXPERTS, 1))


def _dest_kernel(ps_ref, te_ref, rk_ref, d_ref):
    te = te_ref[...]
    acc = jnp.zeros(te.shape, I32)
    for e in range(N_EXPERTS):
        acc = jnp.where(te == e, ps_ref[e], acc)
    d_ref[...] = acc + rk_ref[...]


def _dest(pstart, te, rk):
    T = te.shape[1]
    tb = 2048
    return pl.pallas_call(
        _dest_kernel,
        grid_spec=pltpu.PrefetchScalarGridSpec(
            num_scalar_prefetch=1,
            grid=(T // tb,),
            in_specs=[pl.BlockSpec((8, tb), lambda i, ps: (0, i)),
                      pl.BlockSpec((8, tb), lambda i, ps: (0, i))],
            out_specs=pl.BlockSpec((8, tb), lambda i, ps: (0, i))),
        out_shape=jax.ShapeDtypeStruct((8, T), I32),
        compiler_params=_cparams(("arbitrary",)),
        name="dest_rows",
    )(pstart, te, rk)


SC_CORES = 2
SC_SUBCORES = 16
SC_IDX_CHUNK = 128


def _invperm(dest_flat, P):
    N = dest_flat.shape[0]
    nw = SC_CORES * SC_SUBCORES
    nch = N // (nw * SC_IDX_CHUNK)
    mesh = plsc.VectorSubcoreMesh(core_axis_name="c", subcore_axis_name="s",
                                  num_cores=SC_CORES, num_subcores=SC_SUBCORES)

    @functools.partial(
        pl.kernel, mesh=mesh, out_type=jax.ShapeDtypeStruct((P,), I32),
        scratch_types=[pltpu.VMEM((nch, SC_IDX_CHUNK), I32), pltpu.VMEM((nch, SC_IDX_CHUNK), I32),
                       pltpu.SemaphoreType.DMA])
    def scatter_codes(idx_hbm, val_hbm, out_hbm, idx_v, val_v, sem):
        wid = lax.axis_index("s") * SC_CORES + lax.axis_index("c")
        pltpu.sync_copy(idx_hbm.at[wid], idx_v)
        pltpu.sync_copy(val_hbm.at[wid], val_v)

        @pl.loop(0, nch)
        def _(j):
            pltpu.async_copy(val_v.at[j], out_hbm.at[idx_v.at[j]], sem)

        @pl.loop(0, nch)
        def _(j):
            pltpu.make_async_copy(val_v.at[j], out_hbm.at[idx_v.at[j]], sem).wait()

    vals = jnp.arange(N, dtype=I32)
    return scatter_codes(dest_flat.reshape(nw, nch, SC_IDX_CHUNK), vals.reshape(nw, nch, SC_IDX_CHUNK))


def _row_copy(src, dst, sem):
    return pltpu.make_async_copy(src, dst, sem)


def _moe_kernel(be_ref, nv_ref, cc_ref, cn_ref, h2_ref, w1_ref, b1_ref, w2_ref, b2_ref, y4_ref,
                xbuf, ybuf, w1b, w2b, gsem, ssem, *, F, T, D, tme, nb):
    i = pl.program_id(0)
    s = i % 2
    ns = D // LANES

    def rows(first, n):
        return pl.ds(pl.multiple_of(first * ns, ns), n * ns)

    def gather_copy(tok, p, slot):
        return _row_copy(h2_ref.at[rows(tok, 1)], xbuf.at[slot, rows(p, 1)], gsem.at[slot])

    def token_of(code):
        return code & (T - 1) if T & (T - 1) == 0 else code % T

    def issue_rows(start_row, nv):
        ng = nv // ROW_UNROLL

        def grp(g, c):
            for j in range(ROW_UNROLL):
                start_row(g * ROW_UNROLL + j)
            return c
        lax.fori_loop(0, ng, grp, 0)

        def one(p, c):
            start_row(p)
            return c
        lax.fori_loop(ng * ROW_UNROLL, nv, one, 0)

    def start_gather(code_ref, nv, slot):
        issue_rows(lambda p: gather_copy(token_of(code_ref[0, 0, p]), p, slot).start(), nv)

    def wait_rows(make, nv):
        @pl.when(nv > 0)
        def _():
            make(nv).wait()

    def gathered(n, slot):
        return _row_copy(h2_ref.at[rows(0, n)], xbuf.at[slot, rows(0, n)], gsem.at[slot])

    def scattered(n, slot):
        return _row_copy(ybuf.at[slot, rows(0, n)], y4_ref.at[rows(0, n)], ssem.at[slot])

    @pl.when(i == 0)
    def _():
        xbuf[...] = jnp.zeros_like(xbuf)
        start_gather(cc_ref, nv_ref[0], 0)

    nv = nv_ref[i]
    wait_rows(lambda n: gathered(n, s), nv)

    @pl.when(i + 1 < nb)
    def _():
        start_gather(cn_ref, nv_ref[i + 1], 1 - s)

    @pl.when(i >= 2)
    def _():
        wait_rows(lambda n: scattered(n, s), nv_ref[i - 2])

    @pl.when(jnp.logical_or(i == 0, be_ref[i] != be_ref[jnp.maximum(i - 1, 0)]))
    def _():
        w1b[...] = w1_ref[0].astype(BF16)
        w2b[...] = w2_ref[0].astype(BF16)

    @pl.when(nv > 0)
    def _():
        x = _load_slabs(xbuf, tme, D, lead=(s,)).astype(BF16)
        hgu = jnp.dot(x, w1b[...], preferred_element_type=F32) + b1_ref[0]
        gate = jnp.minimum(hgu[:, :F], SWIGLU_LIMIT)
        up = jnp.clip(hgu[:, F:], -SWIGLU_LIMIT, SWIGLU_LIMIT)
        act = gate * _sigmoid(SWIGLU_ALPHA * gate) * (up + 1.0)
        y = jnp.dot(act.astype(BF16), w2b[...], preferred_element_type=F32) + b2_ref[0]
        _store_slabs(ybuf, y, lead=(s,))

        issue_rows(lambda p: _row_copy(ybuf.at[s, rows(p, 1)], y4_ref.at[rows(cc_ref[0, 0, p], 1)],
                                       ssem.at[s]).start(), nv)

    @pl.when(i == nb - 1)
    def _():
        wait_rows(lambda n: scattered(n, s), nv)
        if nb > 1:
            wait_rows(lambda n: scattered(n, 1 - s), nv_ref[i - 1])


def _experts(blk_exp, blk_valid, codes, h2s, w1, b1, w2, b2, tme):
    E, D, F2 = w1.shape
    F = F2 // 2
    ns = D // LANES
    T = h2s.shape[0] // ns
    nb = blk_exp.shape[0]
    codes3 = codes.reshape(nb, 1, tme)
    wspec = lambda shape: pl.BlockSpec(shape, lambda i, be, nv: (be[i], 0, 0))
    return pl.pallas_call(
        functools.partial(_moe_kernel, F=F, T=T, D=D, tme=tme, nb=nb),
        grid_spec=pltpu.PrefetchScalarGridSpec(
            num_scalar_prefetch=2,
            grid=(nb,),
            in_specs=[pl.BlockSpec((1, 1, tme), lambda i, be, nv: (i, 0, 0), memory_space=pltpu.SMEM),
                      pl.BlockSpec((1, 1, tme), lambda i, be, nv: (jnp.minimum(i + 1, nb - 1), 0, 0),
                                   memory_space=pltpu.SMEM),
                      pl.BlockSpec(memory_space=pl.ANY),
                      wspec((1, D, F2)), wspec((1, 1, F2)), wspec((1, F, D)), wspec((1, 1, D))],
            out_specs=pl.BlockSpec(memory_space=pl.ANY),
            scratch_shapes=[pltpu.VMEM((2, tme * ns, LANES), F32), pltpu.VMEM((2, tme * ns, LANES), F32),
                            pltpu.VMEM((D, F2), BF16), pltpu.VMEM((F, D), BF16),
                            pltpu.SemaphoreType.DMA((2,)), pltpu.SemaphoreType.DMA((2,))]),
        out_shape=jax.ShapeDtypeStruct((TOP_K * T * ns, LANES), F32),
        compiler_params=_cparams(("arbitrary",)),
        name="experts",
    )(blk_exp, blk_valid, codes3, codes3, h2s, w1, b1.reshape(E, 1, F2), w2, b2.reshape(E, 1, D))


def _comb_kernel(g_ref, x1_ref, mod_ref, fw_ref, y0_ref, y1_ref, y2_ref, y3_ref, o_ref):
    g = g_ref[...]
    n, d = x1_ref.shape
    y = g[:, 0:1] * _load_slabs(y0_ref, n, d)
    for k, y_ref in ((1, y1_ref), (2, y2_ref), (3, y3_ref)):
        y = y + g[:, k:k + 1] * _load_slabs(y_ref, n, d)
    x2 = x1_ref[...] + mod_ref[0, 5:6, :] * y
    ms = jnp.mean(x2 * x2, axis=-1, keepdims=True)
    o_ref[...] = x2 * lax.rsqrt(ms + EPS) * fw_ref[...]


def _combine(gtc, x1, mod, final_w, y4, S):
    T, D = x1.shape
    tmc = 512
    nt = T // tmc
    yspec = lambda k: pl.BlockSpec((tmc * (D // LANES), LANES), lambda i: (k * nt + i, 0))
    return pl.pallas_call(
        _comb_kernel,
        grid=(nt,),
        in_specs=[pl.BlockSpec((tmc, LANES), lambda i: (i, 0)),
                  pl.BlockSpec((tmc, D), lambda i: (i, 0)),
                  pl.BlockSpec((1, 6, D), lambda i: (i * tmc // S, 0, 0)),
                  pl.BlockSpec((1, D), lambda i: (0, 0)),
                  yspec(0), yspec(1), yspec(2), yspec(3)],
        out_specs=pl.BlockSpec((tmc, D), lambda i: (i, 0)),
        out_shape=jax.ShapeDtypeStruct((T, D), F32),
        compiler_params=_cparams(("arbitrary",)),
        name="combine",
    )(gtc, x1, mod, final_w.reshape(1, D), y4, y4, y4, y4)


def _layer(x2, mod, B, S, norm1_w, w_in, conv_w, A_log, dt_bias, dn_norm_w, w_out,
           norm2_w, w_router, b_router, w1, b1, w2, b2):
    T, D = x2.shape
    qkv, dqkv, dz, gbc = _inproj(x2, mod, norm1_w, w_in, S)
    branches = [_attn_branch(qkv, B, S, d) for d in DILATIONS]
    dn = _gdn(dqkv, dz, gbc, conv_w, A_log, dt_bias, dn_norm_w, B, S)
    (o1, l1), (o2, l2), (o3, l3) = branches
    x1, h2, te, rk, gtc, cnt = _outproj(o1, o2, o3, l1, l2, l3, dn, x2, mod, w_out, norm2_w,
                                        w_router, b_router, S)
    tme = MOE_ROWS
    P = T * TOP_K + N_EXPERTS * tme
    counts = cnt[:, 0]
    padded = (counts + tme - 1) // tme * tme
    pend = jnp.cumsum(padded)
    pstart = (pend - padded).astype(I32)
    blk_start = jnp.arange(P // tme, dtype=I32) * tme
    blk_exp = jnp.minimum(jnp.sum((pend[None, :] <= blk_start[:, None]).astype(I32), axis=1),
                          N_EXPERTS - 1).astype(I32)
    blk_valid = jnp.clip(pstart[blk_exp] + counts[blk_exp] - blk_start, 0, tme).astype(I32)
    dest = _dest(pstart, te, rk)
    codes = _invperm(dest[:TOP_K].reshape(TOP_K * T), P)
    y4 = _experts(blk_exp, blk_valid, codes, h2, w1, b1, w2, b2, tme)
    return x1, gtc, y4


def kernel(x, c, w_ada, b_ada, norm1_w, w_in, conv_w, A_log, dt_bias, dn_norm_w, w_out, norm2_w,
           w_router, b_router, w1, b1, w2, b2, final_norm_w):
    B, S, D = x.shape
    depth = w_ada.shape[0]
    assert S % (ATTN_BLK * DILATIONS[-1]) == 0 and depth == 1
    x2 = x.reshape(B * S, D)
    mod = _ada(c, w_ada[0], b_ada[0])
    x1, gtc, y4 = _layer(x2, mod, B, S, norm1_w[0], w_in[0], conv_w[0], A_log[0], dt_bias[0],
                         dn_norm_w[0], w_out[0], norm2_w[0], w_router[0], b_router[0],
                         w1[0], b1[0], w2[0], b2[0])
    out = _combine(gtc, x1, mod, final_norm_w, y4, S)
    return out.reshape(B, S, D)
```

```python
import functools

import jax
import jax.numpy as jnp
from jax import lax
from jax.experimental import pallas as pl
from jax.experimental.pallas import tpu as pltpu
from jax.experimental.pallas import tpu_sc as plsc

F32 = jnp.float32
BF16 = jnp.bfloat16
I32 = jnp.int32
HI = lax.Precision.HIGHEST

LANES = 128
ATTN_HEADS = 8
ATTN_HD = 64
ATTN_W = ATTN_HEADS * ATTN_HD
ATTN_BLK = 128
DILATIONS = (1, 4, 16)
DN_HEADS = 4
DN_D = 128
DN_W = DN_HEADS * DN_D
DN_CONV = 3 * DN_W
CONV_K = 4
DN_CHUNK = 64
N_EXPERTS = 32
TOP_K = 4
MOE_ROWS = 512
ROW_UNROLL = 8
SWIGLU_LIMIT = 7.0
SWIGLU_ALPHA = 1.702
EPS = 1e-6
NEG = -1e30
MAIN_COLS = 3 * ATTN_W + DN_CONV + DN_W

VMEM_LIMIT = 56 * 1024 * 1024


def _cparams(sem):
    return pltpu.CompilerParams(dimension_semantics=sem, vmem_limit_bytes=VMEM_LIMIT)


def _nt(a, b, **kw):
    return lax.dot_general(a, b, (((1,), (1,)), ((), ())), preferred_element_type=F32, **kw)


def _sigmoid(x):
    return 1.0 / (1.0 + jnp.exp(-x))


def _store_slabs(ref, val, lead=()):
    n, d = val.shape
    ns = d // LANES
    for c in range(ns):
        ref[lead + (pl.ds(c, n, stride=ns), slice(None))] = val[:, c * LANES:(c + 1) * LANES]


def _load_slabs(ref, n, d, lead=()):
    ns = d // LANES
    return jnp.concatenate([ref[lead + (pl.ds(c, n, stride=ns), slice(None))] for c in range(ns)], axis=1)


def _ada_kernel(c_ref, w_ref, b_ref, o_ref):
    c = c_ref[...]
    cond = c * _sigmoid(c)
    o_ref[...] = jnp.dot(cond, w_ref[...], preferred_element_type=F32, precision=HI) + b_ref[...]


def _ada(c, w_ada, b_ada):
    B, D = c.shape
    N = w_ada.shape[1]
    cp = jnp.zeros((8, D), F32).at[:B].set(c)
    tn = 1024
    out = pl.pallas_call(
        _ada_kernel,
        grid=(N // tn,),
        in_specs=[pl.BlockSpec((8, D), lambda j: (0, 0)),
                  pl.BlockSpec((D, tn), lambda j: (0, j)),
                  pl.BlockSpec((1, tn), lambda j: (0, j))],
        out_specs=pl.BlockSpec((8, tn), lambda j: (0, j)),
        out_shape=jax.ShapeDtypeStruct((8, N), F32),
        compiler_params=_cparams(("arbitrary",)),
        name="ada",
    )(cp, w_ada, b_ada.reshape(1, N))
    return out[:B].reshape(B, 6, D)


def _inproj_kernel(x_ref, mod_ref, nw_ref, wm_ref, ws_ref, qkv_ref, dqkv_ref, dz_ref, gbc_ref):
    x = x_ref[...]
    shift = mod_ref[0, 0:1, :]
    scale = mod_ref[0, 1:2, :]
    ms = jnp.mean(x * x, axis=-1, keepdims=True)
    h = x * lax.rsqrt(ms + EPS) * nw_ref[...]
    hb = (h * (1.0 + scale) + shift).astype(BF16)
    for j in range(3):
        r = jnp.dot(hb, wm_ref[:, j * ATTN_W:(j + 1) * ATTN_W], preferred_element_type=F32)
        if j == 0:
            r = r * (ATTN_HD ** -0.5)
        qkv_ref[:, j * ATTN_W:(j + 1) * ATTN_W] = r.astype(BF16)
    for j in range(3):
        c0 = 3 * ATTN_W + j * DN_W
        r = jnp.dot(hb, wm_ref[:, c0:c0 + DN_W], preferred_element_type=F32)
        dqkv_ref[:, j * DN_W:(j + 1) * DN_W] = r.astype(BF16)
    c0 = 3 * ATTN_W + DN_CONV
    dz_ref[...] = jnp.dot(hb, wm_ref[:, c0:c0 + DN_W], preferred_element_type=F32).astype(BF16)
    gbc_ref[...] = jnp.dot(hb, ws_ref[...], preferred_element_type=F32)


def _inproj(x2, mod, norm_w, w_in, S):
    T, D = x2.shape
    tm = 512
    wm = w_in[:, :MAIN_COLS].astype(BF16)
    ws = jnp.zeros((D, LANES), F32).at[:, :2 * DN_HEADS].set(w_in[:, MAIN_COLS:]).astype(BF16)
    return pl.pallas_call(
        _inproj_kernel,
        grid=(T // tm,),
        in_specs=[pl.BlockSpec((tm, D), lambda i: (i, 0)),
                  pl.BlockSpec((1, 6, D), lambda i: (i * tm // S, 0, 0)),
                  pl.BlockSpec((1, D), lambda i: (0, 0)),
                  pl.BlockSpec((D, MAIN_COLS), lambda i: (0, 0)),
                  pl.BlockSpec((D, LANES), lambda i: (0, 0))],
        out_specs=[pl.BlockSpec((tm, 3 * ATTN_W), lambda i: (i, 0)),
                   pl.BlockSpec((tm, DN_CONV), lambda i: (i, 0)),
                   pl.BlockSpec((tm, DN_W), lambda i: (i, 0)),
                   pl.BlockSpec((tm, LANES), lambda i: (i, 0))],
        out_shape=[jax.ShapeDtypeStruct((T, 3 * ATTN_W), BF16),
                   jax.ShapeDtypeStruct((T, DN_CONV), BF16),
                   jax.ShapeDtypeStruct((T, DN_W), BF16),
                   jax.ShapeDtypeStruct((T, LANES), F32)],
        compiler_params=_cparams(("arbitrary",)),
        name="inproj",
    )(x2, mod, norm_w.reshape(1, D), wm, ws)


def _attn_kernel(q_ref, kc_ref, kp_ref, vc_ref, vp_ref, o_ref, lse_ref, kf, vf, *, qb):
    n = pl.program_id(2)
    kf[0:ATTN_BLK, :] = kp_ref[0]
    kf[ATTN_BLK:, :] = kc_ref[0]
    vf[0:ATTN_BLK, :] = vp_ref[0]
    vf[ATTN_BLK:, :] = vc_ref[0]
    row = lax.broadcasted_iota(I32, (ATTN_BLK, 2 * ATTN_BLK), 0)
    col = lax.broadcasted_iota(I32, (ATTN_BLK, 2 * ATTN_BLK), 1)
    band = jnp.logical_or(jnp.logical_and(col < ATTN_BLK, col >= row),
                          jnp.logical_and(col >= ATTN_BLK, col - ATTN_BLK <= row))
    lane = lax.broadcasted_iota(I32, (ATTN_BLK, LANES), 1)
    lo = lane < ATTN_HD

    def sub(j, carry):
        r0 = pl.multiple_of(j * ATTN_BLK, ATTN_BLK)
        first_col = jnp.where(jnp.logical_and(n == 0, j == 0), ATTN_BLK, 0)
        mask = jnp.logical_and(band, col >= first_col)
        lse_tile = jnp.zeros((ATTN_BLK, LANES), F32)
        for hp in range(ATTN_W // LANES):
            cs = slice(hp * LANES, (hp + 1) * LANES)
            q2 = q_ref[0, pl.ds(r0, ATTN_BLK), cs]
            k2 = kf[pl.ds(r0, 2 * ATTN_BLK), cs]
            v2 = vf[pl.ds(r0, 2 * ATTN_BLK), cs]
            outs = []
            for half in range(2):
                qm = jnp.where(lo if half == 0 else jnp.logical_not(lo), q2, jnp.zeros_like(q2))
                s = _nt(qm, k2)
                s = jnp.where(mask, s, NEG)
                m = jnp.max(s, axis=-1, keepdims=True)
                p = jnp.exp(s - m)
                den = jnp.sum(p, axis=-1, keepdims=True)
                acc = jnp.dot(p.astype(BF16), v2, preferred_element_type=F32)
                outs.append(acc / den)
                lse_tile = jnp.where(lane == 2 * hp + half, m + jnp.log(den), lse_tile)
            o_ref[0, pl.ds(r0, ATTN_BLK), cs] = jnp.where(lo, outs[0], outs[1]).astype(BF16)
        lse_ref[0, pl.ds(r0, ATTN_BLK), :] = lse_tile
        return carry

    lax.fori_loop(0, qb // ATTN_BLK, sub, 0)


def _attn_branch(qkv, B, S, d):
    L = S // d
    qb = min(512, L)
    nsub = qb // ATTN_BLK
    W3 = 3 * ATTN_W
    view = qkv.reshape(B, L, d * W3)
    cur = lambda c: pl.BlockSpec((1, qb, ATTN_W), lambda b, r, n: (b, n, 3 * r + c))
    prev = lambda c: pl.BlockSpec((1, ATTN_BLK, ATTN_W),
                                  lambda b, r, n: (b, jnp.maximum(n * nsub - 1, 0), 3 * r + c))
    o, lse = pl.pallas_call(
        functools.partial(_attn_kernel, qb=qb),
        grid=(B, d, L // qb),
        in_specs=[cur(0), cur(1), prev(1), cur(2), prev(2)],
        out_specs=[pl.BlockSpec((1, qb, ATTN_W), lambda b, r, n: (b, n, r)),
                   pl.BlockSpec((1, qb, LANES), lambda b, r, n: (b, n, r))],
        out_shape=[jax.ShapeDtypeStruct((B, L, d * ATTN_W), BF16),
                   jax.ShapeDtypeStruct((B, L, d * LANES), F32)],
        scratch_shapes=[pltpu.VMEM((qb + ATTN_BLK, ATTN_W), BF16),
                        pltpu.VMEM((qb + ATTN_BLK, ATTN_W), BF16)],
        compiler_params=_cparams(("arbitrary", "arbitrary", "arbitrary")),
        name=f"attn_d{d}",
    )(view, view, view, view, view)
    return o.reshape(B * S, ATTN_W), lse.reshape(B * S, LANES)


def _gdn_kernel(x_ref, z_ref, g_ref, cw_ref, prm_ref, nw_ref, o_ref, xext, yc, s0, s1, *, rb):
    i = pl.program_id(1)

    @pl.when(i == 0)
    def _():
        xext[0:8, :] = jnp.zeros((8, DN_CONV), F32)
        s0[...] = jnp.zeros_like(s0)
        s1[...] = jnp.zeros_like(s1)

    @pl.when(i > 0)
    def _():
        xext[0:8, :] = xext[rb:rb + 8, :]

    xext[8:, :] = x_ref[0].astype(F32)
    y = cw_ref[CONV_K - 1:CONV_K, :] * xext[8:8 + rb, :]
    for j in range(CONV_K - 1):
        off = 8 - (CONV_K - 1) + j
        y = y + cw_ref[j:j + 1, :] * xext[off:off + rb, :]
    yc[...] = y * _sigmoid(y)

    C = DN_CHUNK
    H = DN_HEADS
    CW = H * C
    dot = functools.partial(jnp.dot, preferred_element_type=F32)

    def iota(shape, d):
        return lax.broadcasted_iota(I32, shape, d)

    ltri_b = jnp.where(iota((C, C), 0) >= iota((C, C), 1), 1.0, 0.0).astype(BF16)
    lane = iota((C, LANES), 1)
    blane = lane < H
    glane = jnp.logical_and(lane >= H, lane < 2 * H)
    e512 = jnp.where(jnp.logical_or(iota((LANES, DN_W), 1) // DN_D == iota((LANES, DN_W), 0),
                                    iota((LANES, DN_W), 1) // DN_D == iota((LANES, DN_W), 0) - H),
                     1.0, 0.0).astype(BF16)
    e256 = jnp.where(iota((LANES, CW), 1) // C == iota((LANES, CW), 0) - H, 1.0, 0.0).astype(BF16)
    row4 = iota((C, CW), 0)
    col4 = iota((C, CW), 1) % C
    eye4 = jnp.where(row4 == col4, 1.0, 0.0).astype(F32)
    blk = [iota((C, CW), 1) // C == h for h in range(H)]
    bd_cc = iota((CW, CW), 0) // C == iota((CW, CW), 1) // C
    bd_pair = iota((CW, CW), 0) // DN_D == iota((CW, CW), 1) // DN_D
    rt_mask = iota((CW, DN_W), 0) // C == iota((CW, DN_W), 1) // DN_D
    neg_a = -jnp.exp(prm_ref[0:1, :])
    dtb = prm_ref[1:2, :]
    nw = nw_ref[...]

    def hilo(x):
        hi = x.astype(BF16)
        return hi, (x - hi.astype(F32)).astype(BF16)

    def heads(a, w):
        return [a[:, h * w:(h + 1) * w] for h in range(H)]

    def l2n(a, mult):
        return jnp.concatenate(
            [p * (lax.rsqrt(jnp.sum(p * p, axis=-1, keepdims=True) + EPS) * mult) for p in heads(a, DN_D)],
            axis=1)

    def stack4(a):
        return jnp.concatenate([a, a, a, a], axis=0)

    zb = jnp.zeros((), BF16)
    nchunk = rb // C
    chunks = []
    for c in range(nchunk):
        rs = slice(c * C, (c + 1) * C)
        G = g_ref[0, rs, :]
        xg = G + dtb
        gv = jnp.where(glane, neg_a * (jnp.maximum(xg, 0.0) + jnp.log1p(jnp.exp(-jnp.abs(xg)))), 0.0)
        be = jnp.where(blane, _sigmoid(G), 0.0)
        g_hi, g_lo = hilo(gv)
        gcum = dot(ltri_b, g_hi) + dot(ltri_b, g_lo)
        gtot = gcum[C - 1:C, :]
        eg = jnp.where(glane, jnp.exp(gcum), 0.0)
        ek = jnp.where(glane, jnp.exp(gtot - gcum), 0.0)
        ex = dot(jnp.concatenate([be, eg, ek], axis=0).astype(BF16), e512)
        bexp, egexp, ekexp = ex[0:C], ex[C:2 * C], ex[2 * C:3 * C]
        gexp = dot(g_hi, e256) + dot(g_lo, e256)
        d_hi, d_lo = hilo(jnp.where(row4 > col4, gexp, 0.0))
        diff = dot(ltri_b, d_hi) + dot(ltri_b, d_lo)
        decay = jnp.exp(jnp.where(row4 >= col4, diff, NEG))

        q4 = l2n(yc[rs, 0:DN_W], DN_D ** -0.5)
        k4 = l2n(yc[rs, DN_W:2 * DN_W], 1.0)
        v4 = yc[rs, 2 * DN_W:3 * DN_W]
        kb4 = k4 * bexp
        vb4 = v4 * bexp
        rt = jnp.where(rt_mask, stack4(k4.astype(BF16)), jnp.zeros((), BF16))
        ai = _nt(jnp.concatenate([kb4, q4], axis=0).astype(BF16), rt)
        a4 = jnp.where(row4 > col4, ai[0:C] * decay, 0.0)
        pb = a4.astype(BF16)
        chunks.append(dict(
            rs=rs, q4=q4, k4=k4, kb4=kb4, vb4=vb4, egexp=egexp, ekexp=ekexp,
            intra=ai[C:2 * C] * decay, t4=eye4 - a4, pb=pb,
            bd=jnp.where(bd_cc, stack4(pb), zb)))

    for _ in range(5):
        for ch in chunks:
            ch["pb"] = dot(ch["pb"], ch["bd"]).astype(BF16)
        for ch in chunks:
            ch["bd"] = jnp.where(bd_cc, stack4(ch["pb"]), zb)
        for ch in chunks:
            ch["t4"] = ch["t4"] + dot(ch["t4"].astype(BF16), ch["bd"])

    for ch in chunks:
        q4, k4, kb4, vb4, egexp, ekexp = (ch[n] for n in ("q4", "k4", "kb4", "vb4", "egexp", "ekexp"))
        t4b = ch["t4"].astype(BF16)
        lstk = jnp.concatenate([jnp.where(blk[h], t4b, zb) for h in range(H)], axis=0)
        kbg4 = kb4 * egexp
        rstk = jnp.concatenate(
            [jnp.concatenate([vb, kbg], axis=1) for vb, kbg in zip(heads(vb4, DN_D), heads(kbg4, DN_D))],
            axis=0).astype(BF16)
        uw = dot(lstk, rstk)
        u4 = jnp.concatenate([uw[h * C:(h + 1) * C, 0:DN_D] for h in range(H)], axis=1)
        w4 = jnp.concatenate([uw[h * C:(h + 1) * C, DN_D:2 * DN_D] for h in range(H)], axis=1)
        ib = ch["intra"].astype(BF16)
        ch.update(
            u4=u4, wq=jnp.concatenate([w4, q4 * egexp], axis=0).astype(BF16),
            kd4=(k4 * ekexp).astype(BF16), gl4=egexp[C - 1:C, :],
            lint=jnp.concatenate([jnp.where(blk[h], ib, zb) for h in range(H)], axis=0))

    for ch in chunks:
        rs, u4, wq, kd4, gl4, lint = (ch[n] for n in ("rs", "u4", "wq", "kd4", "gl4", "lint"))
        ra = dot(wq[:, 0:CW], s0[...].astype(BF16))
        rc = dot(wq[:, CW:2 * CW], s1[...].astype(BF16))
        vn = u4 - jnp.concatenate([ra[0:C], rc[0:C]], axis=1)
        vnb = vn.astype(BF16)
        oi = dot(lint, jnp.concatenate(heads(vnb, DN_D), axis=0))
        o = (jnp.concatenate([ra[C:2 * C], rc[C:2 * C]], axis=1)
             + jnp.concatenate([oi[h * C:(h + 1) * C] for h in range(H)], axis=1))
        tn = (((0,), (0,)), ((), ()))
        s0[...] = s0[...] * gl4[:, 0:CW] + jnp.where(
            bd_pair, lax.dot_general(kd4[:, 0:CW], vnb[:, 0:CW], tn, preferred_element_type=F32), 0.0)
        s1[...] = s1[...] * gl4[:, CW:2 * CW] + jnp.where(
            bd_pair, lax.dot_general(kd4[:, CW:2 * CW], vnb[:, CW:2 * CW], tn, preferred_element_type=F32), 0.0)

        z = z_ref[0, rs, :].astype(F32)
        on = jnp.concatenate(
            [p * lax.rsqrt(jnp.mean(p * p, axis=-1, keepdims=True) + EPS) * nw for p in heads(o, DN_D)], axis=1)
        o_ref[0, rs, :] = (on * (z * _sigmoid(z))).astype(BF16)


def _gdn(dqkv, dz, gbc, conv_w, A_log, dt_bias, dn_norm_w, B, S):
    rb = 512
    prm = jnp.zeros((2, LANES), F32)
    prm = prm.at[0, DN_HEADS:2 * DN_HEADS].set(A_log.astype(F32))
    prm = prm.at[1, DN_HEADS:2 * DN_HEADS].set(dt_bias.astype(F32))
    out = pl.pallas_call(
        functools.partial(_gdn_kernel, rb=rb),
        grid=(B, S // rb),
        in_specs=[pl.BlockSpec((1, rb, DN_CONV), lambda b, i: (b, i, 0)),
                  pl.BlockSpec((1, rb, DN_W), lambda b, i: (b, i, 0)),
                  pl.BlockSpec((1, rb, LANES), lambda b, i: (b, i, 0)),
                  pl.BlockSpec((CONV_K, DN_CONV), lambda b, i: (0, 0)),
                  pl.BlockSpec((2, LANES), lambda b, i: (0, 0)),
                  pl.BlockSpec((1, DN_D), lambda b, i: (0, 0))],
        out_specs=pl.BlockSpec((1, rb, DN_W), lambda b, i: (b, i, 0)),
        out_shape=jax.ShapeDtypeStruct((B, S, DN_W), BF16),
        scratch_shapes=[pltpu.VMEM((rb + 8, DN_CONV), F32),
                        pltpu.VMEM((rb, DN_CONV), F32),
                        pltpu.VMEM((2 * DN_D, 2 * DN_D), F32),
                        pltpu.VMEM((2 * DN_D, 2 * DN_D), F32)],
        compiler_params=_cparams(("arbitrary", "arbitrary")),
        name="gdn",
    )(dqkv.reshape(B, S, DN_CONV), dz.reshape(B, S, DN_W), gbc.reshape(B, S, LANES),
      conv_w, prm, dn_norm_w.reshape(1, DN_D))
    return out.reshape(B * S, DN_W)


def _out_kernel(o1_ref, o2_ref, o3_ref, l1_ref, l2_ref, l3_ref, dn_ref, x_ref, mod_ref, wo_ref,
                n2_ref, wr_ref, br_ref,
                x1_ref, h2_ref, te_ref, rk_ref, gtc_ref, cnt_ref, base, *, tm):
    i = pl.program_id(0)

    @pl.when(i == 0)
    def _():
        base[...] = jnp.zeros_like(base)

    l1, l2, l3 = l1_ref[...], l2_ref[...], l3_ref[...]
    mx = jnp.maximum(jnp.maximum(l1, l2), l3)
    e1, e2, e3 = jnp.exp(l1 - mx), jnp.exp(l2 - mx), jnp.exp(l3 - mx)
    zs = e1 + e2 + e3
    er = lax.broadcasted_iota(I32, (LANES, ATTN_W), 0)
    ec = lax.broadcasted_iota(I32, (LANES, ATTN_W), 1)
    expand = jnp.where(ec // ATTN_HD == er, 1.0, 0.0).astype(BF16)
    attn = jnp.zeros((tm, ATTN_W), F32)
    for e, o_ref in ((e1, o1_ref), (e2, o2_ref), (e3, o3_ref)):
        wgt = jnp.dot((e / zs).astype(BF16), expand, preferred_element_type=F32)
        attn = attn + wgt * o_ref[...].astype(F32)
    mix = (jnp.dot(attn.astype(BF16), wo_ref[0:ATTN_W, :], preferred_element_type=F32)
           + jnp.dot(dn_ref[...], wo_ref[ATTN_W:, :], preferred_element_type=F32))
    x1 = x_ref[...] + mod_ref[0, 2:3, :] * mix
    x1_ref[...] = x1
    ms = jnp.mean(x1 * x1, axis=-1, keepdims=True)
    h2 = x1 * lax.rsqrt(ms + EPS) * n2_ref[...]
    h2 = h2 * (1.0 + mod_ref[0, 4:5, :]) + mod_ref[0, 3:4, :]
    _store_slabs(h2_ref, h2)

    lg = _nt(wr_ref[...], h2, precision=HI) + br_ref[...]
    eidx = lax.broadcasted_iota(I32, (N_EXPERTS, tm), 0)
    vals, idxs, sels = [], [], []
    for _ in range(TOP_K):
        m = jnp.max(lg, axis=0, keepdims=True)
        idx = jnp.min(jnp.where(lg == m, eidx, N_EXPERTS), axis=0, keepdims=True)
        sel = eidx == idx
        vals.append(m)
        idxs.append(idx)
        sels.append(sel)
        lg = jnp.where(sel, -jnp.inf, lg)
    ex = [jnp.exp(v - vals[0]) for v in vals]
    den = ex[0] + ex[1] + ex[2] + ex[3]
    gates = [e / den for e in ex]

    msum = jnp.zeros((N_EXPERTS, tm), F32)
    for sel in sels:
        msum = msum + jnp.where(sel, 1.0, 0.0)
    tr = lax.broadcasted_iota(I32, (tm, tm), 0)
    tc = lax.broadcasted_iota(I32, (tm, tm), 1)
    upper = jnp.where(tr <= tc, 1.0, 0.0).astype(BF16)
    incl = jnp.dot(msum.astype(BF16), upper, preferred_element_type=F32)
    pos = base[:, 0:1] + (incl - msum)
    sub8 = lax.broadcasted_iota(I32, (8, tm), 0)
    te = jnp.zeros((8, tm), I32)
    rk = jnp.zeros((8, tm), I32)
    gt = jnp.zeros((8, tm), F32)
    for k in range(TOP_K):
        rank_k = jnp.sum(jnp.where(sels[k], pos, 0.0), axis=0, keepdims=True).astype(I32)
        te = jnp.where(sub8 == k, idxs[k], te)
        rk = jnp.where(sub8 == k, rank_k, rk)
        gt = jnp.where(sub8 == k, gates[k], gt)
    te_ref[...] = te
    rk_ref[...] = rk
    gtc_ref[...] = jnp.transpose(jnp.concatenate([gt, jnp.zeros((LANES - 8, tm), F32)], axis=0))
    base[...] = base[...] + jnp.sum(msum, axis=1, keepdims=True)
    cnt_ref[...] = base[...].astype(I32)


def _outproj(o1, o2, o3, l1, l2, l3, dn, x2, mod, w_out, norm2_w, w_router, b_router, S):
    T, D = x2.shape
    tm = 512
    row = lambda w: pl.BlockSpec((tm, w), lambda i: (i, 0))
    colb = pl.BlockSpec((8, tm), lambda i: (0, i))
    return pl.pallas_call(
        functools.partial(_out_kernel, tm=tm),
        grid=(T // tm,),
        in_specs=[row(ATTN_W), row(ATTN_W), row(ATTN_W), row(LANES), row(LANES), row(LANES),
                  row(DN_W), row(D),
                  pl.BlockSpec((1, 6, D), lambda i: (i * tm // S, 0, 0)),
                  pl.BlockSpec((D, D), lambda i: (0, 0)),
                  pl.BlockSpec((1, D), lambda i: (0, 0)),
                  pl.BlockSpec((N_EXPERTS, D), lambda i: (0, 0)),
                  pl.BlockSpec((N_EXPERTS, 1), lambda i: (0, 0))],
        out_specs=[row(D), pl.BlockSpec((tm * (D // LANES), LANES), lambda i: (i, 0)), colb, colb,
                   row(LANES), pl.BlockSpec((N_EXPERTS, LANES), lambda i: (0, 0))],
        out_shape=[jax.ShapeDtypeStruct((T, D), F32),
                   jax.ShapeDtypeStruct((T * (D // LANES), LANES), F32),
                   jax.ShapeDtypeStruct((8, T), I32),
                   jax.ShapeDtypeStruct((8, T), I32),
                   jax.ShapeDtypeStruct((T, LANES), F32),
                   jax.ShapeDtypeStruct((N_EXPERTS, LANES), I32)],
        scratch_shapes=[pltpu.VMEM((N_EXPERTS, LANES), F32)],
        compiler_params=_cparams(("arbitrary",)),
        name="outproj_router",
    )(o1, o2, o3, l1, l2, l3, dn, x2, mod, w_out.astype(BF16), norm2_w.reshape(1, D),
      jnp.transpose(w_router), b_router.reshape(N_EXPERTS, 1))


def _dest_kernel(ps_ref, te_ref, rk_ref, d_ref):
    te = te_ref[...]
    acc = jnp.zeros(te.shape, I32)
    for e in range(N_EXPERTS):
        acc = jnp.where(te == e, ps_ref[e], acc)
    d_ref[...] = acc + rk_ref[...]


def _dest(pstart, te, rk):
    T = te.shape[1]
    tb = 2048
    return pl.pallas_call(
        _dest_kernel,
        grid_spec=pltpu.PrefetchScalarGridSpec(
            num_scalar_prefetch=1,
            grid=(T // tb,),
            in_specs=[pl.BlockSpec((8, tb), lambda i, ps: (0, i)),
                      pl.BlockSpec((8, tb), lambda i, ps: (0, i))],
            out_specs=pl.BlockSpec((8, tb), lambda i, ps: (0, i))),
        out_shape=jax.ShapeDtypeStruct((8, T), I32),
        compiler_params=_cparams(("arbitrary",)),
        name="dest_rows",
    )(pstart, te, rk)


SC_CORES = 2
SC_SUBCORES = 16
SC_IDX_CHUNK = 128


def _invperm(dest_flat, P):
    N = dest_flat.shape[0]
    nw = SC_CORES * SC_SUBCORES
    nch = N // (nw * SC_IDX_CHUNK)
    mesh = plsc.VectorSubcoreMesh(core_axis_name="c", subcore_axis_name="s",
                                  num_cores=SC_CORES, num_subcores=SC_SUBCORES)

    @functools.partial(
        pl.kernel, mesh=mesh, out_type=jax.ShapeDtypeStruct((P,), I32),
        scratch_types=[pltpu.VMEM((nch, SC_IDX_CHUNK), I32), pltpu.VMEM((nch, SC_IDX_CHUNK), I32),
                       pltpu.SemaphoreType.DMA])
    def scatter_codes(idx_hbm, val_hbm, out_hbm, idx_v, val_v, sem):
        wid = lax.axis_index("s") * SC_CORES + lax.axis_index("c")
        pltpu.sync_copy(idx_hbm.at[wid], idx_v)
        pltpu.sync_copy(val_hbm.at[wid], val_v)

        @pl.loop(0, nch)
        def _(j):
            pltpu.async_copy(val_v.at[j], out_hbm.at[idx_v.at[j]], sem)

        @pl.loop(0, nch)
        def _(j):
            pltpu.make_async_copy(val_v.at[j], out_hbm.at[idx_v.at[j]], sem).wait()

    vals = jnp.arange(N, dtype=I32)
    return scatter_codes(dest_flat.reshape(nw, nch, SC_IDX_CHUNK), vals.reshape(nw, nch, SC_IDX_CHUNK))


def _row_copy(src, dst, sem):
    return pltpu.make_async_copy(src, dst, sem)


def _moe_kernel(be_ref, nv_ref, cc_ref, cn_ref, h2_ref, w1_ref, b1_ref, w2_ref, b2_ref, y4_ref,
                xbuf, ybuf, w1b, w2b, gsem, ssem, *, F, T, D, tme, nb):
    i = pl.program_id(0)
    s = i % 2
    ns = D // LANES

    def rows(first, n):
        return pl.ds(pl.multiple_of(first * ns, ns), n * ns)

    def gather_copy(tok, p, slot):
        return _row_copy(h2_ref.at[rows(tok, 1)], xbuf.at[slot, rows(p, 1)], gsem.at[slot])

    def token_of(code):
        return code & (T - 1) if T & (T - 1) == 0 else code % T

    def issue_rows(start_row, nv):
        ng = nv // ROW_UNROLL

        def grp(g, c):
            for j in range(ROW_UNROLL):
                start_row(g * ROW_UNROLL + j)
            return c
        lax.fori_loop(0, ng, grp, 0)

        def one(p, c):
            start_row(p)
            return c
        lax.fori_loop(ng * ROW_UNROLL, nv, one, 0)

    def start_gather(code_ref, nv, slot):
        issue_rows(lambda p: gather_copy(token_of(code_ref[0, 0, p]), p, slot).start(), nv)

    def wait_rows(make, nv):
        @pl.when(nv > 0)
        def _():
            make(nv).wait()

    def gathered(n, slot):
        return _row_copy(h2_ref.at[rows(0, n)], xbuf.at[slot, rows(0, n)], gsem.at[slot])

    def scattered(n, slot):
        return _row_copy(ybuf.at[slot, rows(0, n)], y4_ref.at[rows(0, n)], ssem.at[slot])

    @pl.when(i == 0)
    def _():
        start_gather(cc_ref, tme, 0)

    nv = nv_ref[i]
    gathered(tme, s).wait()

    @pl.when(i >= 2)
    def _():
        wait_rows(lambda n: scattered(n, s), nv_ref[i - 2])

    @pl.when(jnp.logical_or(i == 0, be_ref[i] != be_ref[jnp.maximum(i - 1, 0)]))
    def _():
        w1b[...] = w1_ref[0].astype(BF16)
        w2b[...] = w2_ref[0].astype(BF16)

    @pl.when(nv > 0)
    def _():
        x = _load_slabs(xbuf, tme, D, lead=(s,)).astype(BF16)
        for p in range(tme):
            gather_copy(token_of(cn_ref[0, 0, p]), p, 1 - s).start()
        hgu = jnp.dot(x, w1b[...], preferred_element_type=F32) + b1_ref[0]
        gate = jnp.minimum(hgu[:, :F], SWIGLU_LIMIT)
        up = jnp.clip(hgu[:, F:], -SWIGLU_LIMIT, SWIGLU_LIMIT)
        act = gate * _sigmoid(SWIGLU_ALPHA * gate) * (up + 1.0)
        y = jnp.dot(act.astype(BF16), w2b[...], preferred_element_type=F32) + b2_ref[0]
        _store_slabs(ybuf, y, lead=(s,))

        issue_rows(lambda p: _row_copy(ybuf.at[s, rows(p, 1)], y4_ref.at[rows(cc_ref[0, 0, p], 1)],
                                       ssem.at[s]).start(), nv)

    @pl.when(jnp.logical_and(nv == 0, i + 1 < nb))
    def _():
        start_gather(cn_ref, tme, 1 - s)

    @pl.when(i == nb - 1)
    def _():
        @pl.when(nv > 0)
        def _():
            gathered(tme, 1 - s).wait()
        wait_rows(lambda n: scattered(n, s), nv)
        if nb > 1:
            wait_rows(lambda n: scattered(n, 1 - s), nv_ref[i - 1])


def _experts(blk_exp, blk_valid, codes, h2s, w1, b1, w2, b2, tme):
    E, D, F2 = w1.shape
    F = F2 // 2
    ns = D // LANES
    T = h2s.shape[0] // ns
    nb = blk_exp.shape[0]
    codes3 = codes.reshape(nb, 1, tme)
    wspec = lambda shape: pl.BlockSpec(shape, lambda i, be, nv: (be[i], 0, 0))
    cspec = lambda off: pl.BlockSpec((1, 1, tme), lambda i, be, nv: (jnp.minimum(i + off, nb - 1), 0, 0),
                                     memory_space=pltpu.SMEM)
    return pl.pallas_call(
        functools.partial(_moe_kernel, F=F, T=T, D=D, tme=tme, nb=nb),
        grid_spec=pltpu.PrefetchScalarGridSpec(
            num_scalar_prefetch=2,
            grid=(nb,),
            in_specs=[cspec(0), cspec(1),
                      pl.BlockSpec(memory_space=pl.ANY),
                      wspec((1, D, F2)), wspec((1, 1, F2)), wspec((1, F, D)), wspec((1, 1, D))],
            out_specs=pl.BlockSpec(memory_space=pl.ANY),
            scratch_shapes=[pltpu.VMEM((2, tme * ns, LANES), F32), pltpu.VMEM((2, tme * ns, LANES), F32),
                            pltpu.VMEM((D, F2), BF16), pltpu.VMEM((F, D), BF16),
                            pltpu.SemaphoreType.DMA((2,)), pltpu.SemaphoreType.DMA((2,))]),
        out_shape=jax.ShapeDtypeStruct((TOP_K * T * ns, LANES), F32),
        compiler_params=_cparams(("arbitrary",)),
        name="experts",
    )(blk_exp, blk_valid, codes3, codes3, h2s, w1, b1.reshape(E, 1, F2), w2, b2.reshape(E, 1, D))


def _comb_kernel(g_ref, x1_ref, mod_ref, fw_ref, y0_ref, y1_ref, y2_ref, y3_ref, o_ref):
    g = g_ref[...]
    n, d = x1_ref.shape
    y = g[:, 0:1] * _load_slabs(y0_ref, n, d)
    for k, y_ref in ((1, y1_ref), (2, y2_ref), (3, y3_ref)):
        y = y + g[:, k:k + 1] * _load_slabs(y_ref, n, d)
    x2 = x1_ref[...] + mod_ref[0, 5:6, :] * y
    ms = jnp.mean(x2 * x2, axis=-1, keepdims=True)
    o_ref[...] = x2 * lax.rsqrt(ms + EPS) * fw_ref[...]


def _combine(gtc, x1, mod, final_w, y4, S):
    T, D = x1.shape
    tmc = 512
    nt = T // tmc
    yspec = lambda k: pl.BlockSpec((tmc * (D // LANES), LANES), lambda i: (k * nt + i, 0))
    return pl.pallas_call(
        _comb_kernel,
        grid=(nt,),
        in_specs=[pl.BlockSpec((tmc, LANES), lambda i: (i, 0)),
                  pl.BlockSpec((tmc, D), lambda i: (i, 0)),
                  pl.BlockSpec((1, 6, D), lambda i: (i * tmc // S, 0, 0)),
                  pl.BlockSpec((1, D), lambda i: (0, 0)),
                  yspec(0), yspec(1), yspec(2), yspec(3)],
        out_specs=pl.BlockSpec((tmc, D), lambda i: (i, 0)),
        out_shape=jax.ShapeDtypeStruct((T, D), F32),
        compiler_params=_cparams(("arbitrary",)),
        name="combine",
    )(gtc, x1, mod, final_w.reshape(1, D), y4, y4, y4, y4)


def _layer(x2, mod, B, S, norm1_w, w_in, conv_w, A_log, dt_bias, dn_norm_w, w_out,
           norm2_w, w_router, b_router, w1, b1, w2, b2):
    T, D = x2.shape
    qkv, dqkv, dz, gbc = _inproj(x2, mod, norm1_w, w_in, S)
    branches = [_attn_branch(qkv, B, S, d) for d in DILATIONS]
    dn = _gdn(dqkv, dz, gbc, conv_w, A_log, dt_bias, dn_norm_w, B, S)
    (o1, l1), (o2, l2), (o3, l3) = branches
    x1, h2, te, rk, gtc, cnt = _outproj(o1, o2, o3, l1, l2, l3, dn, x2, mod, w_out, norm2_w,
                                        w_router, b_router, S)
    tme = MOE_ROWS
    P = T * TOP_K + N_EXPERTS * tme
    counts = cnt[:, 0]
    padded = (counts + tme - 1) // tme * tme
    pend = jnp.cumsum(padded)
    pstart = (pend - padded).astype(I32)
    blk_start = jnp.arange(P // tme, dtype=I32) * tme
    blk_exp = jnp.minimum(jnp.sum((pend[None, :] <= blk_start[:, None]).astype(I32), axis=1),
                          N_EXPERTS - 1).astype(I32)
    blk_valid = jnp.clip(pstart[blk_exp] + counts[blk_exp] - blk_start, 0, tme).astype(I32)
    dest = _dest(pstart, te, rk)
    codes = _invperm(dest[:TOP_K].reshape(TOP_K * T), P)
    y4 = _experts(blk_exp, blk_valid, codes, h2, w1, b1, w2, b2, tme)
    return x1, gtc, y4


def kernel(x, c, w_ada, b_ada, norm1_w, w_in, conv_w, A_log, dt_bias, dn_norm_w, w_out, norm2_w,
           w_router, b_router, w1, b1, w2, b2, final_norm_w):
    B, S, D = x.shape
    depth = w_ada.shape[0]
    assert S % (ATTN_BLK * DILATIONS[-1]) == 0 and depth == 1
    x2 = x.reshape(B * S, D)
    mod = _ada(c, w_ada[0], b_ada[0])
    x1, gtc, y4 = _layer(x2, mod, B, S, norm1_w[0], w_in[0], conv_w[0], A_log[0], dt_bias[0],
                         dn_norm_w[0], w_out[0], norm2_w[0], w_router[0], b_router[0],
                         w1[0], b1[0], w2[0], b2[0])
    out = _combine(gtc, x1, mod, final_norm_w, y4, S)
    return out.reshape(B, S, D)
```

```python
import functools

import jax
import jax.numpy as jnp
from jax import lax
from jax.experimental import pallas as pl
from jax.experimental.pallas import tpu as pltpu
from jax.experimental.pallas import tpu_sc as plsc

F32 = jnp.float32
BF16 = jnp.bfloat16
I32 = jnp.int32
HI = lax.Precision.HIGHEST

LANES = 128
ATTN_HEADS = 8
ATTN_HD = 64
ATTN_W = ATTN_HEADS * ATTN_HD
ATTN_BLK = 128
DILATIONS = (1, 4, 16)
DN_HEADS = 4
DN_D = 128
DN_W = DN_HEADS * DN_D
DN_CONV = 3 * DN_W
CONV_K = 4
DN_CHUNK = 64
N_EXPERTS = 32
TOP_K = 4
MOE_ROWS = 512
ROW_UNROLL = 8
SWIGLU_LIMIT = 7.0
SWIGLU_ALPHA = 1.702
EPS = 1e-6
NEG = -1e30
MAIN_COLS = 3 * ATTN_W + DN_CONV + DN_W

VMEM_LIMIT = 56 * 1024 * 1024


def _cparams(sem):
    return pltpu.CompilerParams(dimension_semantics=sem, vmem_limit_bytes=VMEM_LIMIT)


def _nt(a, b, **kw):
    return lax.dot_general(a, b, (((1,), (1,)), ((), ())), preferred_element_type=F32, **kw)


def _sigmoid(x):
    return 1.0 / (1.0 + jnp.exp(-x))


def _store_slabs(ref, val, lead=()):
    n, d = val.shape
    ns = d // LANES
    for c in range(ns):
        ref[lead + (pl.ds(c, n, stride=ns), slice(None))] = val[:, c * LANES:(c + 1) * LANES]


def _load_slabs(ref, n, d, lead=()):
    ns = d // LANES
    return jnp.concatenate([ref[lead + (pl.ds(c, n, stride=ns), slice(None))] for c in range(ns)], axis=1)


def _ada_kernel(c_ref, w_ref, b_ref, o_ref):
    c = c_ref[...]
    cond = c * _sigmoid(c)
    o_ref[...] = jnp.dot(cond, w_ref[...], preferred_element_type=F32, precision=HI) + b_ref[...]


def _ada(c, w_ada, b_ada):
    B, D = c.shape
    N = w_ada.shape[1]
    cp = jnp.zeros((8, D), F32).at[:B].set(c)
    tn = 1024
    out = pl.pallas_call(
        _ada_kernel,
        grid=(N // tn,),
        in_specs=[pl.BlockSpec((8, D), lambda j: (0, 0)),
                  pl.BlockSpec((D, tn), lambda j: (0, j)),
                  pl.BlockSpec((1, tn), lambda j: (0, j))],
        out_specs=pl.BlockSpec((8, tn), lambda j: (0, j)),
        out_shape=jax.ShapeDtypeStruct((8, N), F32),
        compiler_params=_cparams(("arbitrary",)),
        name="ada",
    )(cp, w_ada, b_ada.reshape(1, N))
    return out[:B].reshape(B, 6, D)


def _inproj_kernel(x_ref, mod_ref, nw_ref, wm_ref, ws_ref, qkv_ref, dqkv_ref, dz_ref, gbc_ref):
    x = x_ref[...]
    shift = mod_ref[0, 0:1, :]
    scale = mod_ref[0, 1:2, :]
    ms = jnp.mean(x * x, axis=-1, keepdims=True)
    h = x * lax.rsqrt(ms + EPS) * nw_ref[...]
    hb = (h * (1.0 + scale) + shift).astype(BF16)
    for j in range(3):
        r = jnp.dot(hb, wm_ref[:, j * ATTN_W:(j + 1) * ATTN_W], preferred_element_type=F32)
        if j == 0:
            r = r * (ATTN_HD ** -0.5)
        qkv_ref[:, j * ATTN_W:(j + 1) * ATTN_W] = r.astype(BF16)
    for j in range(3):
        c0 = 3 * ATTN_W + j * DN_W
        r = jnp.dot(hb, wm_ref[:, c0:c0 + DN_W], preferred_element_type=F32)
        dqkv_ref[:, j * DN_W:(j + 1) * DN_W] = r.astype(BF16)
    c0 = 3 * ATTN_W + DN_CONV
    dz_ref[...] = jnp.dot(hb, wm_ref[:, c0:c0 + DN_W], preferred_element_type=F32).astype(BF16)
    gbc_ref[...] = jnp.dot(hb, ws_ref[...], preferred_element_type=F32)


def _inproj(x2, mod, norm_w, w_in, S):
    T, D = x2.shape
    tm = 512
    wm = w_in[:, :MAIN_COLS].astype(BF16)
    ws = jnp.zeros((D, LANES), F32).at[:, :2 * DN_HEADS].set(w_in[:, MAIN_COLS:]).astype(BF16)
    return pl.pallas_call(
        _inproj_kernel,
        grid=(T // tm,),
        in_specs=[pl.BlockSpec((tm, D), lambda i: (i, 0)),
                  pl.BlockSpec((1, 6, D), lambda i: (i * tm // S, 0, 0)),
                  pl.BlockSpec((1, D), lambda i: (0, 0)),
                  pl.BlockSpec((D, MAIN_COLS), lambda i: (0, 0)),
                  pl.BlockSpec((D, LANES), lambda i: (0, 0))],
        out_specs=[pl.BlockSpec((tm, 3 * ATTN_W), lambda i: (i, 0)),
                   pl.BlockSpec((tm, DN_CONV), lambda i: (i, 0)),
                   pl.BlockSpec((tm, DN_W), lambda i: (i, 0)),
                   pl.BlockSpec((tm, LANES), lambda i: (i, 0))],
        out_shape=[jax.ShapeDtypeStruct((T, 3 * ATTN_W), BF16),
                   jax.ShapeDtypeStruct((T, DN_CONV), BF16),
                   jax.ShapeDtypeStruct((T, DN_W), BF16),
                   jax.ShapeDtypeStruct((T, LANES), F32)],
        compiler_params=_cparams(("arbitrary",)),
        name="inproj",
    )(x2, mod, norm_w.reshape(1, D), wm, ws)


def _attn_kernel(q_ref, kc_ref, kp_ref, vc_ref, vp_ref, o_ref, lse_ref, kf, vf, *, qb):
    n = pl.program_id(2)
    kf[0:ATTN_BLK, :] = kp_ref[0]
    kf[ATTN_BLK:, :] = kc_ref[0]
    vf[0:ATTN_BLK, :] = vp_ref[0]
    vf[ATTN_BLK:, :] = vc_ref[0]
    row = lax.broadcasted_iota(I32, (ATTN_BLK, 2 * ATTN_BLK), 0)
    col = lax.broadcasted_iota(I32, (ATTN_BLK, 2 * ATTN_BLK), 1)
    band = jnp.logical_or(jnp.logical_and(col < ATTN_BLK, col >= row),
                          jnp.logical_and(col >= ATTN_BLK, col - ATTN_BLK <= row))
    lane = lax.broadcasted_iota(I32, (ATTN_BLK, LANES), 1)
    lo = lane < ATTN_HD

    def sub(j, carry):
        r0 = pl.multiple_of(j * ATTN_BLK, ATTN_BLK)
        first_col = jnp.where(jnp.logical_and(n == 0, j == 0), ATTN_BLK, 0)
        mask = jnp.logical_and(band, col >= first_col)
        lse_tile = jnp.zeros((ATTN_BLK, LANES), F32)
        for hp in range(ATTN_W // LANES):
            cs = slice(hp * LANES, (hp + 1) * LANES)
            q2 = q_ref[0, pl.ds(r0, ATTN_BLK), cs]
            k2 = kf[pl.ds(r0, 2 * ATTN_BLK), cs]
            v2 = vf[pl.ds(r0, 2 * ATTN_BLK), cs]
            outs = []
            for half in range(2):
                qm = jnp.where(lo if half == 0 else jnp.logical_not(lo), q2, jnp.zeros_like(q2))
                s = _nt(qm, k2)
                s = jnp.where(mask, s, NEG)
                m = jnp.max(s, axis=-1, keepdims=True)
                p = jnp.exp(s - m)
                den = jnp.sum(p, axis=-1, keepdims=True)
                acc = jnp.dot(p.astype(BF16), v2, preferred_element_type=F32)
                outs.append(acc / den)
                lse_tile = jnp.where(lane == 2 * hp + half, m + jnp.log(den), lse_tile)
            o_ref[0, pl.ds(r0, ATTN_BLK), cs] = jnp.where(lo, outs[0], outs[1]).astype(BF16)
        lse_ref[0, pl.ds(r0, ATTN_BLK), :] = lse_tile
        return carry

    lax.fori_loop(0, qb // ATTN_BLK, sub, 0)


def _attn_branch(qkv, B, S, d):
    L = S // d
    qb = min(512, L)
    nsub = qb // ATTN_BLK
    W3 = 3 * ATTN_W
    view = qkv.reshape(B, L, d * W3)
    cur = lambda c: pl.BlockSpec((1, qb, ATTN_W), lambda b, r, n: (b, n, 3 * r + c))
    prev = lambda c: pl.BlockSpec((1, ATTN_BLK, ATTN_W),
                                  lambda b, r, n: (b, jnp.maximum(n * nsub - 1, 0), 3 * r + c))
    o, lse = pl.pallas_call(
        functools.partial(_attn_kernel, qb=qb),
        grid=(B, d, L // qb),
        in_specs=[cur(0), cur(1), prev(1), cur(2), prev(2)],
        out_specs=[pl.BlockSpec((1, qb, ATTN_W), lambda b, r, n: (b, n, r)),
                   pl.BlockSpec((1, qb, LANES), lambda b, r, n: (b, n, r))],
        out_shape=[jax.ShapeDtypeStruct((B, L, d * ATTN_W), BF16),
                   jax.ShapeDtypeStruct((B, L, d * LANES), F32)],
        scratch_shapes=[pltpu.VMEM((qb + ATTN_BLK, ATTN_W), BF16),
                        pltpu.VMEM((qb + ATTN_BLK, ATTN_W), BF16)],
        compiler_params=_cparams(("arbitrary", "arbitrary", "arbitrary")),
        name=f"attn_d{d}",
    )(view, view, view, view, view)
    return o.reshape(B * S, ATTN_W), lse.reshape(B * S, LANES)


def _gdn_kernel(x_ref, z_ref, g_ref, cw_ref, prm_ref, nw_ref, o_ref, xext, yc, s0, s1, *, rb):
    i = pl.program_id(1)

    @pl.when(i == 0)
    def _():
        xext[0:8, :] = jnp.zeros((8, DN_CONV), F32)
        s0[...] = jnp.zeros_like(s0)
        s1[...] = jnp.zeros_like(s1)

    @pl.when(i > 0)
    def _():
        xext[0:8, :] = xext[rb:rb + 8, :]

    xext[8:, :] = x_ref[0].astype(F32)
    y = cw_ref[CONV_K - 1:CONV_K, :] * xext[8:8 + rb, :]
    for j in range(CONV_K - 1):
        off = 8 - (CONV_K - 1) + j
        y = y + cw_ref[j:j + 1, :] * xext[off:off + rb, :]
    yc[...] = y * _sigmoid(y)

    C = DN_CHUNK
    H = DN_HEADS
    CW = H * C
    dot = functools.partial(jnp.dot, preferred_element_type=F32)

    def iota(shape, d):
        return lax.broadcasted_iota(I32, shape, d)

    ltri_b = jnp.where(iota((C, C), 0) >= iota((C, C), 1), 1.0, 0.0).astype(BF16)
    lane = iota((C, LANES), 1)
    blane = lane < H
    glane = jnp.logical_and(lane >= H, lane < 2 * H)
    e512 = jnp.where(jnp.logical_or(iota((LANES, DN_W), 1) // DN_D == iota((LANES, DN_W), 0),
                                    iota((LANES, DN_W), 1) // DN_D == iota((LANES, DN_W), 0) - H),
                     1.0, 0.0).astype(BF16)
    e256 = jnp.where(iota((LANES, CW), 1) // C == iota((LANES, CW), 0) - H, 1.0, 0.0).astype(BF16)
    row4 = iota((C, CW), 0)
    col4 = iota((C, CW), 1) % C
    eye4 = jnp.where(row4 == col4, 1.0, 0.0).astype(F32)
    blk = [iota((C, CW), 1) // C == h for h in range(H)]
    bd_cc = iota((CW, CW), 0) // C == iota((CW, CW), 1) // C
    bd_pair = iota((CW, CW), 0) // DN_D == iota((CW, CW), 1) // DN_D
    rt_mask = iota((CW, DN_W), 0) // C == iota((CW, DN_W), 1) // DN_D
    neg_a = -jnp.exp(prm_ref[0:1, :])
    dtb = prm_ref[1:2, :]
    nw = nw_ref[...]

    def hilo(x):
        hi = x.astype(BF16)
        return hi, (x - hi.astype(F32)).astype(BF16)

    def heads(a, w):
        return [a[:, h * w:(h + 1) * w] for h in range(H)]

    def l2n(a, mult):
        return jnp.concatenate(
            [p * (lax.rsqrt(jnp.sum(p * p, axis=-1, keepdims=True) + EPS) * mult) for p in heads(a, DN_D)],
            axis=1)

    def stack4(a):
        return jnp.concatenate([a, a, a, a], axis=0)

    zb = jnp.zeros((), BF16)
    nchunk = rb // C
    chunks = []
    for c in range(nchunk):
        rs = slice(c * C, (c + 1) * C)
        G = g_ref[0, rs, :]
        xg = G + dtb
        gv = jnp.where(glane, neg_a * (jnp.maximum(xg, 0.0) + jnp.log1p(jnp.exp(-jnp.abs(xg)))), 0.0)
        be = jnp.where(blane, _sigmoid(G), 0.0)
        g_hi, g_lo = hilo(gv)
        gcum = dot(ltri_b, g_hi) + dot(ltri_b, g_lo)
        gtot = gcum[C - 1:C, :]
        eg = jnp.where(glane, jnp.exp(gcum), 0.0)
        ek = jnp.where(glane, jnp.exp(gtot - gcum), 0.0)
        ex = dot(jnp.concatenate([be, eg, ek], axis=0).astype(BF16), e512)
        bexp, egexp, ekexp = ex[0:C], ex[C:2 * C], ex[2 * C:3 * C]
        gexp = dot(g_hi, e256) + dot(g_lo, e256)
        d_hi, d_lo = hilo(jnp.where(row4 > col4, gexp, 0.0))
        diff = dot(ltri_b, d_hi) + dot(ltri_b, d_lo)
        decay = jnp.exp(jnp.where(row4 >= col4, diff, NEG))

        q4 = l2n(yc[rs, 0:DN_W], DN_D ** -0.5)
        k4 = l2n(yc[rs, DN_W:2 * DN_W], 1.0)
        v4 = yc[rs, 2 * DN_W:3 * DN_W]
        kb4 = k4 * bexp
        vb4 = v4 * bexp
        rt = jnp.where(rt_mask, stack4(k4.astype(BF16)), jnp.zeros((), BF16))
        ai = _nt(jnp.concatenate([kb4, q4], axis=0).astype(BF16), rt)
        a4 = jnp.where(row4 > col4, ai[0:C] * decay, 0.0)
        pb = a4.astype(BF16)
        chunks.append(dict(
            rs=rs, q4=q4, k4=k4, kb4=kb4, vb4=vb4, egexp=egexp, ekexp=ekexp,
            intra=ai[C:2 * C] * decay, t4=eye4 - a4, pb=pb,
            bd=jnp.where(bd_cc, stack4(pb), zb)))

    for _ in range(5):
        for ch in chunks:
            ch["pb"] = dot(ch["pb"], ch["bd"]).astype(BF16)
        for ch in chunks:
            ch["bd"] = jnp.where(bd_cc, stack4(ch["pb"]), zb)
        for ch in chunks:
            ch["t4"] = ch["t4"] + dot(ch["t4"].astype(BF16), ch["bd"])

    for ch in chunks:
        q4, k4, kb4, vb4, egexp, ekexp = (ch[n] for n in ("q4", "k4", "kb4", "vb4", "egexp", "ekexp"))
        t4b = ch["t4"].astype(BF16)
        lstk = jnp.concatenate([jnp.where(blk[h], t4b, zb) for h in range(H)], axis=0)
        kbg4 = kb4 * egexp
        rstk = jnp.concatenate(
            [jnp.concatenate([vb, kbg], axis=1) for vb, kbg in zip(heads(vb4, DN_D), heads(kbg4, DN_D))],
            axis=0).astype(BF16)
        uw = dot(lstk, rstk)
        u4 = jnp.concatenate([uw[h * C:(h + 1) * C, 0:DN_D] for h in range(H)], axis=1)
        w4 = jnp.concatenate([uw[h * C:(h + 1) * C, DN_D:2 * DN_D] for h in range(H)], axis=1)
        ib = ch["intra"].astype(BF16)
        ch.update(
            u4=u4, wq=jnp.concatenate([w4, q4 * egexp], axis=0).astype(BF16),
            kd4=(k4 * ekexp).astype(BF16), gl4=egexp[C - 1:C, :],
            lint=jnp.concatenate([jnp.where(blk[h], ib, zb) for h in range(H)], axis=0))

    for ch in chunks:
        rs, u4, wq, kd4, gl4, lint = (ch[n] for n in ("rs", "u4", "wq", "kd4", "gl4", "lint"))
        ra = dot(wq[:, 0:CW], s0[...].astype(BF16))
        rc = dot(wq[:, CW:2 * CW], s1[...].astype(BF16))
        vn = u4 - jnp.concatenate([ra[0:C], rc[0:C]], axis=1)
        vnb = vn.astype(BF16)
        oi = dot(lint, jnp.concatenate(heads(vnb, DN_D), axis=0))
        o = (jnp.concatenate([ra[C:2 * C], rc[C:2 * C]], axis=1)
             + jnp.concatenate([oi[h * C:(h + 1) * C] for h in range(H)], axis=1))
        tn = (((0,), (0,)), ((), ()))
        s0[...] = s0[...] * gl4[:, 0:CW] + jnp.where(
            bd_pair, lax.dot_general(kd4[:, 0:CW], vnb[:, 0:CW], tn, preferred_element_type=F32), 0.0)
        s1[...] = s1[...] * gl4[:, CW:2 * CW] + jnp.where(
            bd_pair, lax.dot_general(kd4[:, CW:2 * CW], vnb[:, CW:2 * CW], tn, preferred_element_type=F32), 0.0)

        z = z_ref[0, rs, :].astype(F32)
        on = jnp.concatenate(
            [p * lax.rsqrt(jnp.mean(p * p, axis=-1, keepdims=True) + EPS) * nw for p in heads(o, DN_D)], axis=1)
        o_ref[0, rs, :] = (on * (z * _sigmoid(z))).astype(BF16)


def _gdn(dqkv, dz, gbc, conv_w, A_log, dt_bias, dn_norm_w, B, S):
    rb = 512
    prm = jnp.zeros((2, LANES), F32)
    prm = prm.at[0, DN_HEADS:2 * DN_HEADS].set(A_log.astype(F32))
    prm = prm.at[1, DN_HEADS:2 * DN_HEADS].set(dt_bias.astype(F32))
    out = pl.pallas_call(
        functools.partial(_gdn_kernel, rb=rb),
        grid=(B, S // rb),
        in_specs=[pl.BlockSpec((1, rb, DN_CONV), lambda b, i: (b, i, 0)),
                  pl.BlockSpec((1, rb, DN_W), lambda b, i: (b, i, 0)),
                  pl.BlockSpec((1, rb, LANES), lambda b, i: (b, i, 0)),
                  pl.BlockSpec((CONV_K, DN_CONV), lambda b, i: (0, 0)),
                  pl.BlockSpec((2, LANES), lambda b, i: (0, 0)),
                  pl.BlockSpec((1, DN_D), lambda b, i: (0, 0))],
        out_specs=pl.BlockSpec((1, rb, DN_W), lambda b, i: (b, i, 0)),
        out_shape=jax.ShapeDtypeStruct((B, S, DN_W), BF16),
        scratch_shapes=[pltpu.VMEM((rb + 8, DN_CONV), F32),
                        pltpu.VMEM((rb, DN_CONV), F32),
                        pltpu.VMEM((2 * DN_D, 2 * DN_D), F32),
                        pltpu.VMEM((2 * DN_D, 2 * DN_D), F32)],
        compiler_params=_cparams(("arbitrary", "arbitrary")),
        name="gdn",
    )(dqkv.reshape(B, S, DN_CONV), dz.reshape(B, S, DN_W), gbc.reshape(B, S, LANES),
      conv_w, prm, dn_norm_w.reshape(1, DN_D))
    return out.reshape(B * S, DN_W)


def _out_kernel(o1_ref, o2_ref, o3_ref, l1_ref, l2_ref, l3_ref, dn_ref, x_ref, mod_ref, wo_ref,
                n2_ref, wr_ref, br_ref,
                x1_ref, h2_ref, te_ref, rk_ref, gtc_ref, cnt_ref, base, *, tm):
    i = pl.program_id(0)

    @pl.when(i == 0)
    def _():
        base[...] = jnp.zeros_like(base)

    l1, l2, l3 = l1_ref[...], l2_ref[...], l3_ref[...]
    mx = jnp.maximum(jnp.maximum(l1, l2), l3)
    e1, e2, e3 = jnp.exp(l1 - mx), jnp.exp(l2 - mx), jnp.exp(l3 - mx)
    zs = e1 + e2 + e3
    er = lax.broadcasted_iota(I32, (LANES, ATTN_W), 0)
    ec = lax.broadcasted_iota(I32, (LANES, ATTN_W), 1)
    expand = jnp.where(ec // ATTN_HD == er, 1.0, 0.0).astype(BF16)
    attn = jnp.zeros((tm, ATTN_W), F32)
    for e, o_ref in ((e1, o1_ref), (e2, o2_ref), (e3, o3_ref)):
        wgt = jnp.dot((e / zs).astype(BF16), expand, preferred_element_type=F32)
        attn = attn + wgt * o_ref[...].astype(F32)
    mix = (jnp.dot(attn.astype(BF16), wo_ref[0:ATTN_W, :], preferred_element_type=F32)
           + jnp.dot(dn_ref[...], wo_ref[ATTN_W:, :], preferred_element_type=F32))
    x1 = x_ref[...] + mod_ref[0, 2:3, :] * mix
    x1_ref[...] = x1
    ms = jnp.mean(x1 * x1, axis=-1, keepdims=True)
    h2 = x1 * lax.rsqrt(ms + EPS) * n2_ref[...]
    h2 = h2 * (1.0 + mod_ref[0, 4:5, :]) + mod_ref[0, 3:4, :]
    _store_slabs(h2_ref, h2)

    lg = _nt(wr_ref[...], h2, precision=HI) + br_ref[...]
    eidx = lax.broadcasted_iota(I32, (N_EXPERTS, tm), 0)
    vals, idxs, sels = [], [], []
    for _ in range(TOP_K):
        m = jnp.max(lg, axis=0, keepdims=True)
        idx = jnp.min(jnp.where(lg == m, eidx, N_EXPERTS), axis=0, keepdims=True)
        sel = eidx == idx
        vals.append(m)
        idxs.append(idx)
        sels.append(sel)
        lg = jnp.where(sel, -jnp.inf, lg)
    ex = [jnp.exp(v - vals[0]) for v in vals]
    den = ex[0] + ex[1] + ex[2] + ex[3]
    gates = [e / den for e in ex]

    msum = jnp.zeros((N_EXPERTS, tm), F32)
    for sel in sels:
        msum = msum + jnp.where(sel, 1.0, 0.0)
    tr = lax.broadcasted_iota(I32, (tm, tm), 0)
    tc = lax.broadcasted_iota(I32, (tm, tm), 1)
    upper = jnp.where(tr <= tc, 1.0, 0.0).astype(BF16)
    incl = jnp.dot(msum.astype(BF16), upper, preferred_element_type=F32)
    pos = base[:, 0:1] + (incl - msum)
    sub8 = lax.broadcasted_iota(I32, (8, tm), 0)
    te = jnp.zeros((8, tm), I32)
    rk = jnp.zeros((8, tm), I32)
    gt = jnp.zeros((8, tm), F32)
    for k in range(TOP_K):
        rank_k = jnp.sum(jnp.where(sels[k], pos, 0.0), axis=0, keepdims=True).astype(I32)
        te = jnp.where(sub8 == k, idxs[k], te)
        rk = jnp.where(sub8 == k, rank_k, rk)
        gt = jnp.where(sub8 == k, gates[k], gt)
    te_ref[...] = te
    rk_ref[...] = rk
    gtc_ref[...] = jnp.transpose(jnp.concatenate([gt, jnp.zeros((LANES - 8, tm), F32)], axis=0))
    base[...] = base[...] + jnp.sum(msum, axis=1, keepdims=True)
    cnt_ref[...] = base[...].astype(I32)


def _outproj(o1, o2, o3, l1, l2, l3, dn, x2, mod, w_out, norm2_w, w_router, b_router, S):
    T, D = x2.shape
    tm = 512
    row = lambda w: pl.BlockSpec((tm, w), lambda i: (i, 0))
    colb = pl.BlockSpec((8, tm), lambda i: (0, i))
    return pl.pallas_call(
        functools.partial(_out_kernel, tm=tm),
        grid=(T // tm,),
        in_specs=[row(ATTN_W), row(ATTN_W), row(ATTN_W), row(LANES), row(LANES), row(LANES),
                  row(DN_W), row(D),
                  pl.BlockSpec((1, 6, D), lambda i: (i * tm // S, 0, 0)),
                  pl.BlockSpec((D, D), lambda i: (0, 0)),
                  pl.BlockSpec((1, D), lambda i: (0, 0)),
                  pl.BlockSpec((N_EXPERTS, D), lambda i: (0, 0)),
                  pl.BlockSpec((N_EXPERTS, 1), lambda i: (0, 0))],
        out_specs=[row(D), pl.BlockSpec((tm * (D // LANES), LANES), lambda i: (i, 0)), colb, colb,
                   row(LANES), pl.BlockSpec((N_EXPERTS, LANES), lambda i: (0, 0))],
        out_shape=[jax.ShapeDtypeStruct((T, D), F32),
                   jax.ShapeDtypeStruct((T * (D // LANES), LANES), F32),
                   jax.ShapeDtypeStruct((8, T), I32),
                   jax.ShapeDtypeStruct((8, T), I32),
                   jax.ShapeDtypeStruct((T, LANES), F32),
                   jax.ShapeDtypeStruct((N_EXPERTS, LANES), I32)],
        scratch_shapes=[pltpu.VMEM((N_EXPERTS, LANES), F32)],
        compiler_params=_cparams(("arbitrary",)),
        name="outproj_router",
    )(o1, o2, o3, l1, l2, l3, dn, x2, mod, w_out.astype(BF16), norm2_w.reshape(1, D),
      jnp.transpose(w_router), b_router.reshape(N_EXPERTS, 1))


def _dest_kernel(ps_ref, te_ref, rk_ref, d_ref):
    te = te_ref[...]
    acc = jnp.zeros(te.shape, I32)
    for e in range(N_EXPERTS):
        acc = jnp.where(te == e, ps_ref[e], acc)
    d_ref[...] = acc + rk_ref[...]


def _dest(pstart, te, rk):
    T = te.shape[1]
    tb = 2048
    return pl.pallas_call(
        _dest_kernel,
        grid_spec=pltpu.PrefetchScalarGridSpec(
            num_scalar_prefetch=1,
            grid=(T // tb,),
            in_specs=[pl.BlockSpec((8, tb), lambda i, ps: (0, i)),
                      pl.BlockSpec((8, tb), lambda i, ps: (0, i))],
            out_specs=pl.BlockSpec((8, tb), lambda i, ps: (0, i))),
        out_shape=jax.ShapeDtypeStruct((8, T), I32),
        compiler_params=_cparams(("arbitrary",)),
        name="dest_rows",
    )(pstart, te, rk)


SC_CORES = 2
SC_SUBCORES = 16
SC_IDX_CHUNK = 128


def _invperm(dest_flat, P):
    N = dest_flat.shape[0]
    nw = SC_CORES * SC_SUBCORES
    nch = N // (nw * SC_IDX_CHUNK)
    mesh = plsc.VectorSubcoreMesh(core_axis_name="c", subcore_axis_name="s",
                                  num_cores=SC_CORES, num_subcores=SC_SUBCORES)

    @functools.partial(
        pl.kernel, mesh=mesh, out_type=jax.ShapeDtypeStruct((P,), I32),
        scratch_types=[pltpu.VMEM((nch, SC_IDX_CHUNK), I32), pltpu.VMEM((nch, SC_IDX_CHUNK), I32),
                       pltpu.SemaphoreType.DMA])
    def scatter_codes(idx_hbm, val_hbm, out_hbm, idx_v, val_v, sem):
        wid = lax.axis_index("s") * SC_CORES + lax.axis_index("c")
        pltpu.sync_copy(idx_hbm.at[wid], idx_v)
        pltpu.sync_copy(val_hbm.at[wid], val_v)

        @pl.loop(0, nch)
        def _(j):
            pltpu.async_copy(val_v.at[j], out_hbm.at[idx_v.at[j]], sem)

        @pl.loop(0, nch)
        def _(j):
            pltpu.make_async_copy(val_v.at[j], out_hbm.at[idx_v.at[j]], sem).wait()

    vals = jnp.arange(N, dtype=I32)
    return scatter_codes(dest_flat.reshape(nw, nch, SC_IDX_CHUNK), vals.reshape(nw, nch, SC_IDX_CHUNK))


def _row_copy(src, dst, sem):
    return pltpu.make_async_copy(src, dst, sem)


def _moe_kernel(be_ref, nv_ref, cc_ref, cn_ref, h2_ref, w1_ref, b1_ref, w2_ref, b2_ref, y4_ref,
                xbuf, ybuf, w1b, w2b, gsem, ssem, *, F, T, D, tme, nb):
    i = pl.program_id(0)
    s = i % 2
    ns = D // LANES

    def rows(first, n):
        return pl.ds(pl.multiple_of(first * ns, ns), n * ns)

    def gather_copy(tok, p, slot):
        return _row_copy(h2_ref.at[rows(tok, 1)], xbuf.at[slot, rows(p, 1)], gsem.at[slot])

    def token_of(code):
        return code & (T - 1) if T & (T - 1) == 0 else code % T

    def issue_rows(start_row, nv):
        ng = nv // ROW_UNROLL

        def grp(g, c):
            for j in range(ROW_UNROLL):
                start_row(g * ROW_UNROLL + j, j % 2)
            return c
        lax.fori_loop(0, ng, grp, 0)

        def one(p, c):
            start_row(p, 0)
            return c
        lax.fori_loop(ng * ROW_UNROLL, nv, one, 0)

    def start_gather(code_ref, nv, slot):
        issue_rows(lambda p, pri: gather_copy(token_of(code_ref[0, 0, p]), p, slot).start(priority=pri), nv)

    def wait_rows(make, nv):
        @pl.when(nv > 0)
        def _():
            make(nv).wait()

    def gathered(n, slot):
        return _row_copy(h2_ref.at[rows(0, n)], xbuf.at[slot, rows(0, n)], gsem.at[slot])

    def scattered(n, slot):
        return _row_copy(ybuf.at[slot, rows(0, n)], y4_ref.at[rows(0, n)], ssem.at[slot])

    @pl.when(i == 0)
    def _():
        start_gather(cc_ref, tme, 0)

    nv = nv_ref[i]
    gathered(tme, s).wait()

    @pl.when(i >= 2)
    def _():
        wait_rows(lambda n: scattered(n, s), nv_ref[i - 2])

    @pl.when(jnp.logical_or(i == 0, be_ref[i] != be_ref[jnp.maximum(i - 1, 0)]))
    def _():
        w1b[...] = w1_ref[0].astype(BF16)
        w2b[...] = w2_ref[0].astype(BF16)

    @pl.when(nv > 0)
    def _():
        x = _load_slabs(xbuf, tme, D, lead=(s,)).astype(BF16)
        for p in range(tme):
            gather_copy(token_of(cn_ref[0, 0, p]), p, 1 - s).start(priority=p % 2)
        hgu = jnp.dot(x, w1b[...], preferred_element_type=F32) + b1_ref[0]
        gate = jnp.minimum(hgu[:, :F], SWIGLU_LIMIT)
        up = jnp.clip(hgu[:, F:], -SWIGLU_LIMIT, SWIGLU_LIMIT)
        act = gate * _sigmoid(SWIGLU_ALPHA * gate) * (up + 1.0)
        y = jnp.dot(act.astype(BF16), w2b[...], preferred_element_type=F32) + b2_ref[0]
        _store_slabs(ybuf, y, lead=(s,))

        issue_rows(lambda p, pri: _row_copy(ybuf.at[s, rows(p, 1)], y4_ref.at[rows(cc_ref[0, 0, p], 1)],
                                            ssem.at[s]).start(priority=pri), nv)

    @pl.when(jnp.logical_and(nv == 0, i + 1 < nb))
    def _():
        start_gather(cn_ref, tme, 1 - s)

    @pl.when(i == nb - 1)
    def _():
        @pl.when(nv > 0)
        def _():
            gathered(tme, 1 - s).wait()
        wait_rows(lambda n: scattered(n, s), nv)
        if nb > 1:
            wait_rows(lambda n: scattered(n, 1 - s), nv_ref[i - 1])


def _experts(blk_exp, blk_valid, codes, h2s, w1, b1, w2, b2, tme):
    E, D, F2 = w1.shape
    F = F2 // 2
    ns = D // LANES
    T = h2s.shape[0] // ns
    nb = blk_exp.shape[0]
    codes3 = codes.reshape(nb, 1, tme)
    wspec = lambda shape: pl.BlockSpec(shape, lambda i, be, nv: (be[i], 0, 0))
    cspec = lambda off: pl.BlockSpec((1, 1, tme), lambda i, be, nv: (jnp.minimum(i + off, nb - 1), 0, 0),
                                     memory_space=pltpu.SMEM)
    return pl.pallas_call(
        functools.partial(_moe_kernel, F=F, T=T, D=D, tme=tme, nb=nb),
        grid_spec=pltpu.PrefetchScalarGridSpec(
            num_scalar_prefetch=2,
            grid=(nb,),
            in_specs=[cspec(0), cspec(1),
                      pl.BlockSpec(memory_space=pl.ANY),
                      wspec((1, D, F2)), wspec((1, 1, F2)), wspec((1, F, D)), wspec((1, 1, D))],
            out_specs=pl.BlockSpec(memory_space=pl.ANY),
            scratch_shapes=[pltpu.VMEM((2, tme * ns, LANES), F32), pltpu.VMEM((2, tme * ns, LANES), F32),
                            pltpu.VMEM((D, F2), BF16), pltpu.VMEM((F, D), BF16),
                            pltpu.SemaphoreType.DMA((2,)), pltpu.SemaphoreType.DMA((2,))]),
        out_shape=jax.ShapeDtypeStruct((TOP_K * T * ns, LANES), F32),
        compiler_params=_cparams(("arbitrary",)),
        name="experts",
    )(blk_exp, blk_valid, codes3, codes3, h2s, w1, b1.reshape(E, 1, F2), w2, b2.reshape(E, 1, D))


def _comb_kernel(g_ref, x1_ref, mod_ref, fw_ref, y0_ref, y1_ref, y2_ref, y3_ref, o_ref):
    g = g_ref[...]
    n, d = x1_ref.shape
    y = g[:, 0:1] * _load_slabs(y0_ref, n, d)
    for k, y_ref in ((1, y1_ref), (2, y2_ref), (3, y3_ref)):
        y = y + g[:, k:k + 1] * _load_slabs(y_ref, n, d)
    x2 = x1_ref[...] + mod_ref[0, 5:6, :] * y
    ms = jnp.mean(x2 * x2, axis=-1, keepdims=True)
    o_ref[...] = x2 * lax.rsqrt(ms + EPS) * fw_ref[...]


def _combine(gtc, x1, mod, final_w, y4, S):
    T, D = x1.shape
    tmc = 512
    nt = T // tmc
    yspec = lambda k: pl.BlockSpec((tmc * (D // LANES), LANES), lambda i: (k * nt + i, 0))
    return pl.pallas_call(
        _comb_kernel,
        grid=(nt,),
        in_specs=[pl.BlockSpec((tmc, LANES), lambda i: (i, 0)),
                  pl.BlockSpec((tmc, D), lambda i: (i, 0)),
                  pl.BlockSpec((1, 6, D), lambda i: (i * tmc // S, 0, 0)),
                  pl.BlockSpec((1, D), lambda i: (0, 0)),
                  yspec(0), yspec(1), yspec(2), yspec(3)],
        out_specs=pl.BlockSpec((tmc, D), lambda i: (i, 0)),
        out_shape=jax.ShapeDtypeStruct((T, D), F32),
        compiler_params=_cparams(("arbitrary",)),
        name="combine",
    )(gtc, x1, mod, final_w.reshape(1, D), y4, y4, y4, y4)


def _layer(x2, mod, B, S, norm1_w, w_in, conv_w, A_log, dt_bias, dn_norm_w, w_out,
           norm2_w, w_router, b_router, w1, b1, w2, b2):
    T, D = x2.shape
    qkv, dqkv, dz, gbc = _inproj(x2, mod, norm1_w, w_in, S)
    branches = [_attn_branch(qkv, B, S, d) for d in DILATIONS]
    dn = _gdn(dqkv, dz, gbc, conv_w, A_log, dt_bias, dn_norm_w, B, S)
    (o1, l1), (o2, l2), (o3, l3) = branches
    x1, h2, te, rk, gtc, cnt = _outproj(o1, o2, o3, l1, l2, l3, dn, x2, mod, w_out, norm2_w,
                                        w_router, b_router, S)
    tme = MOE_ROWS
    P = T * TOP_K + N_EXPERTS * tme
    counts = cnt[:, 0]
    padded = (counts + tme - 1) // tme * tme
    pend = jnp.cumsum(padded)
    pstart = (pend - padded).astype(I32)
    blk_start = jnp.arange(P // tme, dtype=I32) * tme
    blk_exp = jnp.minimum(jnp.sum((pend[None, :] <= blk_start[:, None]).astype(I32), axis=1),
                          N_EXPERTS - 1).astype(I32)
    blk_valid = jnp.clip(pstart[blk_exp] + counts[blk_exp] - blk_start, 0, tme).astype(I32)
    dest = _dest(pstart, te, rk)
    codes = _invperm(dest[:TOP_K].reshape(TOP_K * T), P)
    y4 = _experts(blk_exp, blk_valid, codes, h2, w1, b1, w2, b2, tme)
    return x1, gtc, y4


def kernel(x, c, w_ada, b_ada, norm1_w, w_in, conv_w, A_log, dt_bias, dn_norm_w, w_out, norm2_w,
           w_router, b_router, w1, b1, w2, b2, final_norm_w):
    B, S, D = x.shape
    depth = w_ada.shape[0]
    assert S % (ATTN_BLK * DILATIONS[-1]) == 0 and depth == 1
    x2 = x.reshape(B * S, D)
    mod = _ada(c, w_ada[0], b_ada[0])
    x1, gtc, y4 = _layer(x2, mod, B, S, norm1_w[0], w_in[0], conv_w[0], A_log[0], dt_bias[0],
                         dn_norm_w[0], w_out[0], norm2_w[0], w_router[0], b_router[0],
                         w1[0], b1[0], w2[0], b2[0])
    out = _combine(gtc, x1, mod, final_norm_w, y4, S)
    return out.reshape(B, S, D)
```

```python
import functools

import jax
import jax.numpy as jnp
from jax import lax
from jax.experimental import pallas as pl
from jax.experimental.pallas import tpu as pltpu
from jax.experimental.pallas import tpu_sc as plsc

F32 = jnp.float32
BF16 = jnp.bfloat16
I32 = jnp.int32
HI = lax.Precision.HIGHEST

LANES = 128
ATTN_HEADS = 8
ATTN_HD = 64
ATTN_W = ATTN_HEADS * ATTN_HD
ATTN_BLK = 128
DILATIONS = (1, 4, 16)
DN_HEADS = 4
DN_D = 128
DN_W = DN_HEADS * DN_D
DN_CONV = 3 * DN_W
CONV_K = 4
DN_CHUNK = 64
N_EXPERTS = 32
TOP_K = 4
MOE_ROWS = 512
ROW_UNROLL = 8
SWIGLU_LIMIT = 7.0
SWIGLU_ALPHA = 1.702
EPS = 1e-6
NEG = -1e30
MAIN_COLS = 3 * ATTN_W + DN_CONV + DN_W

VMEM_LIMIT = 56 * 1024 * 1024


def _cparams(sem):
    return pltpu.CompilerParams(dimension_semantics=sem, vmem_limit_bytes=VMEM_LIMIT)


def _nt(a, b, **kw):
    return lax.dot_general(a, b, (((1,), (1,)), ((), ())), preferred_element_type=F32, **kw)


def _sigmoid(x):
    return 1.0 / (1.0 + jnp.exp(-x))


def _store_slabs(ref, val, lead=()):
    n, d = val.shape
    ns = d // LANES
    for c in range(ns):
        ref[lead + (pl.ds(c, n, stride=ns), slice(None))] = val[:, c * LANES:(c + 1) * LANES]


def _load_slabs(ref, n, d, lead=()):
    ns = d // LANES
    return jnp.concatenate([ref[lead + (pl.ds(c, n, stride=ns), slice(None))] for c in range(ns)], axis=1)


def _ada_kernel(c_ref, w_ref, b_ref, o_ref):
    c = c_ref[...]
    cond = c * _sigmoid(c)
    o_ref[...] = jnp.dot(cond, w_ref[...], preferred_element_type=F32, precision=HI) + b_ref[...]


def _ada(c, w_ada, b_ada):
    B, D = c.shape
    N = w_ada.shape[1]
    cp = jnp.zeros((8, D), F32).at[:B].set(c)
    tn = 1024
    out = pl.pallas_call(
        _ada_kernel,
        grid=(N // tn,),
        in_specs=[pl.BlockSpec((8, D), lambda j: (0, 0)),
                  pl.BlockSpec((D, tn), lambda j: (0, j)),
                  pl.BlockSpec((1, tn), lambda j: (0, j))],
        out_specs=pl.BlockSpec((8, tn), lambda j: (0, j)),
        out_shape=jax.ShapeDtypeStruct((8, N), F32),
        compiler_params=_cparams(("arbitrary",)),
        name="ada",
    )(cp, w_ada, b_ada.reshape(1, N))
    return out[:B].reshape(B, 6, D)


def _inproj_kernel(x_ref, mod_ref, nw_ref, wm_ref, ws_ref, qkv_ref, qkv4_ref, qkv16_ref, dqkv_ref, dz_ref,
                   gbc_ref, scr):
    x = x_ref[...]
    tm = x.shape[0]
    shift = mod_ref[0, 0:1, :]
    scale = mod_ref[0, 1:2, :]
    ms = jnp.mean(x * x, axis=-1, keepdims=True)
    h = x * lax.rsqrt(ms + EPS) * nw_ref[...]
    hb = (h * (1.0 + scale) + shift).astype(BF16)
    nl = ATTN_W // LANES
    for j in range(3):
        cs = slice(j * ATTN_W, (j + 1) * ATTN_W)
        r = jnp.dot(hb, wm_ref[:, cs], preferred_element_type=F32)
        if j == 0:
            r = r * (ATTN_HD ** -0.5)
        qkv_ref[:, cs] = r.astype(BF16)
        for c in range(nl):
            scr[c] = r[:, c * LANES:(c + 1) * LANES]
        for d, ref in ((DILATIONS[1], qkv4_ref), (DILATIONS[2], qkv16_ref)):
            for res in range(d):
                part = jnp.concatenate([scr[c, pl.ds(res, tm // d, stride=d), :] for c in range(nl)], axis=1)
                ref[0, res, :, cs] = part.astype(BF16)
    for j in range(3):
        c0 = 3 * ATTN_W + j * DN_W
        r = jnp.dot(hb, wm_ref[:, c0:c0 + DN_W], preferred_element_type=F32)
        dqkv_ref[:, j * DN_W:(j + 1) * DN_W] = r.astype(BF16)
    c0 = 3 * ATTN_W + DN_CONV
    dz_ref[...] = jnp.dot(hb, wm_ref[:, c0:c0 + DN_W], preferred_element_type=F32).astype(BF16)
    gbc_ref[...] = jnp.dot(hb, ws_ref[...], preferred_element_type=F32)


def _inproj(x2, mod, norm_w, w_in, S):
    T, D = x2.shape
    tm = 512
    B, nt = T // S, S // tm
    wm = w_in[:, :MAIN_COLS].astype(BF16)
    ws = jnp.zeros((D, LANES), F32).at[:, :2 * DN_HEADS].set(w_in[:, MAIN_COLS:]).astype(BF16)
    return pl.pallas_call(
        _inproj_kernel,
        grid=(T // tm,),
        in_specs=[pl.BlockSpec((tm, D), lambda i: (i, 0)),
                  pl.BlockSpec((1, 6, D), lambda i: (i * tm // S, 0, 0)),
                  pl.BlockSpec((1, D), lambda i: (0, 0)),
                  pl.BlockSpec((D, MAIN_COLS), lambda i: (0, 0)),
                  pl.BlockSpec((D, LANES), lambda i: (0, 0))],
        out_specs=[pl.BlockSpec((tm, 3 * ATTN_W), lambda i: (i, 0))]
        + [pl.BlockSpec((1, d, tm // d, 3 * ATTN_W), lambda i: (i // nt, 0, i % nt, 0)) for d in DILATIONS[1:]]
        + [pl.BlockSpec((tm, DN_CONV), lambda i: (i, 0)),
           pl.BlockSpec((tm, DN_W), lambda i: (i, 0)),
           pl.BlockSpec((tm, LANES), lambda i: (i, 0))],
        out_shape=[jax.ShapeDtypeStruct((T, 3 * ATTN_W), BF16)]
        + [jax.ShapeDtypeStruct((B, d, S // d, 3 * ATTN_W), BF16) for d in DILATIONS[1:]]
        + [jax.ShapeDtypeStruct((T, DN_CONV), BF16),
           jax.ShapeDtypeStruct((T, DN_W), BF16),
           jax.ShapeDtypeStruct((T, LANES), F32)],
        scratch_shapes=[pltpu.VMEM((ATTN_W // LANES, tm, LANES), F32)],
        compiler_params=_cparams(("arbitrary",)),
        name="inproj",
    )(x2, mod, norm_w.reshape(1, D), wm, ws)


def _attn_kernel(q_ref, kc_ref, kp_ref, vc_ref, vp_ref, o_ref, lse_ref, kf, vf, *, qb):
    n = pl.program_id(2)
    kf[0:ATTN_BLK, :] = kp_ref[0, 0]
    kf[ATTN_BLK:, :] = kc_ref[0, 0]
    vf[0:ATTN_BLK, :] = vp_ref[0, 0]
    vf[ATTN_BLK:, :] = vc_ref[0, 0]
    row = lax.broadcasted_iota(I32, (ATTN_BLK, 2 * ATTN_BLK), 0)
    col = lax.broadcasted_iota(I32, (ATTN_BLK, 2 * ATTN_BLK), 1)
    band = jnp.logical_or(jnp.logical_and(col < ATTN_BLK, col >= row),
                          jnp.logical_and(col >= ATTN_BLK, col - ATTN_BLK <= row))
    lane = lax.broadcasted_iota(I32, (ATTN_BLK, LANES), 1)
    lo = lane < ATTN_HD

    def sub(j, carry):
        r0 = pl.multiple_of(j * ATTN_BLK, ATTN_BLK)
        first_col = jnp.where(jnp.logical_and(n == 0, j == 0), ATTN_BLK, 0)
        mask = jnp.logical_and(band, col >= first_col)
        lse_tile = jnp.zeros((ATTN_BLK, LANES), F32)
        for hp in range(ATTN_W // LANES):
            cs = slice(hp * LANES, (hp + 1) * LANES)
            q2 = q_ref[0, 0, pl.ds(r0, ATTN_BLK), cs]
            k2 = kf[pl.ds(r0, 2 * ATTN_BLK), cs]
            v2 = vf[pl.ds(r0, 2 * ATTN_BLK), cs]
            outs = []
            for half in range(2):
                qm = jnp.where(lo if half == 0 else jnp.logical_not(lo), q2, jnp.zeros_like(q2))
                s = _nt(qm, k2)
                s = jnp.where(mask, s, NEG)
                m = jnp.max(s, axis=-1, keepdims=True)
                p = jnp.exp(s - m)
                den = jnp.sum(p, axis=-1, keepdims=True)
                acc = jnp.dot(p.astype(BF16), v2, preferred_element_type=F32)
                outs.append(acc / den)
                lse_tile = jnp.where(lane == 2 * hp + half, m + jnp.log(den), lse_tile)
            o_ref[0, 0, pl.ds(r0, ATTN_BLK), cs] = jnp.where(lo, outs[0], outs[1]).astype(BF16)
        lse_ref[0, 0, pl.ds(r0, ATTN_BLK), :] = lse_tile
        return carry

    lax.fori_loop(0, qb // ATTN_BLK, sub, 0)


def _attn_branch(qkv, d):
    B, _, L, _ = qkv.shape
    qb = min(512, L)
    nsub = qb // ATTN_BLK
    cur = lambda c: pl.BlockSpec((1, 1, qb, ATTN_W), lambda b, r, n: (b, r, n, c))
    prev = lambda c: pl.BlockSpec((1, 1, ATTN_BLK, ATTN_W),
                                  lambda b, r, n: (b, r, jnp.maximum(n * nsub - 1, 0), c))
    return pl.pallas_call(
        functools.partial(_attn_kernel, qb=qb),
        grid=(B, d, L // qb),
        in_specs=[cur(0), cur(1), prev(1), cur(2), prev(2)],
        out_specs=[pl.BlockSpec((1, 1, qb, ATTN_W), lambda b, r, n: (b, r, n, 0)),
                   pl.BlockSpec((1, 1, qb, LANES), lambda b, r, n: (b, r, n, 0))],
        out_shape=[jax.ShapeDtypeStruct((B, d, L, ATTN_W), BF16),
                   jax.ShapeDtypeStruct((B, d, L, LANES), F32)],
        scratch_shapes=[pltpu.VMEM((qb + ATTN_BLK, ATTN_W), BF16),
                        pltpu.VMEM((qb + ATTN_BLK, ATTN_W), BF16)],
        compiler_params=_cparams(("arbitrary", "arbitrary", "arbitrary")),
        name=f"attn_d{d}",
    )(qkv, qkv, qkv, qkv, qkv)


def _gdn_kernel(x_ref, z_ref, g_ref, cw_ref, prm_ref, nw_ref, o_ref, xext, yc, s0, s1, *, rb):
    i = pl.program_id(1)

    @pl.when(i == 0)
    def _():
        xext[0:8, :] = jnp.zeros((8, DN_CONV), F32)
        s0[...] = jnp.zeros_like(s0)
        s1[...] = jnp.zeros_like(s1)

    @pl.when(i > 0)
    def _():
        xext[0:8, :] = xext[rb:rb + 8, :]

    xext[8:, :] = x_ref[0].astype(F32)
    y = cw_ref[CONV_K - 1:CONV_K, :] * xext[8:8 + rb, :]
    for j in range(CONV_K - 1):
        off = 8 - (CONV_K - 1) + j
        y = y + cw_ref[j:j + 1, :] * xext[off:off + rb, :]
    yc[...] = y * _sigmoid(y)

    C = DN_CHUNK
    H = DN_HEADS
    CW = H * C
    dot = functools.partial(jnp.dot, preferred_element_type=F32)

    def iota(shape, d):
        return lax.broadcasted_iota(I32, shape, d)

    ltri_b = jnp.where(iota((C, C), 0) >= iota((C, C), 1), 1.0, 0.0).astype(BF16)
    lane = iota((C, LANES), 1)
    blane = lane < H
    glane = jnp.logical_and(lane >= H, lane < 2 * H)
    e512 = jnp.where(jnp.logical_or(iota((LANES, DN_W), 1) // DN_D == iota((LANES, DN_W), 0),
                                    iota((LANES, DN_W), 1) // DN_D == iota((LANES, DN_W), 0) - H),
                     1.0, 0.0).astype(BF16)
    e256 = jnp.where(iota((LANES, CW), 1) // C == iota((LANES, CW), 0) - H, 1.0, 0.0).astype(BF16)
    row4 = iota((C, CW), 0)
    col4 = iota((C, CW), 1) % C
    eye4 = jnp.where(row4 == col4, 1.0, 0.0).astype(F32)
    blk = [iota((C, CW), 1) // C == h for h in range(H)]
    bd_cc = iota((CW, CW), 0) // C == iota((CW, CW), 1) // C
    bd_pair = iota((CW, CW), 0) // DN_D == iota((CW, CW), 1) // DN_D
    rt_mask = iota((CW, DN_W), 0) // C == iota((CW, DN_W), 1) // DN_D
    neg_a = -jnp.exp(prm_ref[0:1, :])
    dtb = prm_ref[1:2, :]
    nw = nw_ref[...]

    def hilo(x):
        hi = x.astype(BF16)
        return hi, (x - hi.astype(F32)).astype(BF16)

    def heads(a, w):
        return [a[:, h * w:(h + 1) * w] for h in range(H)]

    def l2n(a, mult):
        return jnp.concatenate(
            [p * (lax.rsqrt(jnp.sum(p * p, axis=-1, keepdims=True) + EPS) * mult) for p in heads(a, DN_D)],
            axis=1)

    def stack4(a):
        return jnp.concatenate([a, a, a, a], axis=0)

    zb = jnp.zeros((), BF16)
    nchunk = rb // C
    chunks = []
    for c in range(nchunk):
        rs = slice(c * C, (c + 1) * C)
        G = g_ref[0, rs, :]
        xg = G + dtb
        gv = jnp.where(glane, neg_a * (jnp.maximum(xg, 0.0) + jnp.log1p(jnp.exp(-jnp.abs(xg)))), 0.0)
        be = jnp.where(blane, _sigmoid(G), 0.0)
        g_hi, g_lo = hilo(gv)
        gcum = dot(ltri_b, g_hi) + dot(ltri_b, g_lo)
        gtot = gcum[C - 1:C, :]
        eg = jnp.where(glane, jnp.exp(gcum), 0.0)
        ek = jnp.where(glane, jnp.exp(gtot - gcum), 0.0)
        ex = dot(jnp.concatenate([be, eg, ek], axis=0).astype(BF16), e512)
        bexp, egexp, ekexp = ex[0:C], ex[C:2 * C], ex[2 * C:3 * C]
        gexp = dot(g_hi, e256) + dot(g_lo, e256)
        d_hi, d_lo = hilo(jnp.where(row4 > col4, gexp, 0.0))
        diff = dot(ltri_b, d_hi) + dot(ltri_b, d_lo)
        decay = jnp.exp(jnp.where(row4 >= col4, diff, NEG))

        q4 = l2n(yc[rs, 0:DN_W], DN_D ** -0.5)
        k4 = l2n(yc[rs, DN_W:2 * DN_W], 1.0)
        v4 = yc[rs, 2 * DN_W:3 * DN_W]
        kb4 = k4 * bexp
        vb4 = v4 * bexp
        rt = jnp.where(rt_mask, stack4(k4.astype(BF16)), jnp.zeros((), BF16))
        ai = _nt(jnp.concatenate([kb4, q4], axis=0).astype(BF16), rt)
        a4 = jnp.where(row4 > col4, ai[0:C] * decay, 0.0)
        pb = a4.astype(BF16)
        chunks.append(dict(
            rs=rs, q4=q4, k4=k4, kb4=kb4, vb4=vb4, egexp=egexp, ekexp=ekexp,
            intra=ai[C:2 * C] * decay, t4=eye4 - a4, pb=pb,
            bd=jnp.where(bd_cc, stack4(pb), zb)))

    for _ in range(5):
        for ch in chunks:
            ch["pb"] = dot(ch["pb"], ch["bd"]).astype(BF16)
        for ch in chunks:
            ch["bd"] = jnp.where(bd_cc, stack4(ch["pb"]), zb)
        for ch in chunks:
            ch["t4"] = ch["t4"] + dot(ch["t4"].astype(BF16), ch["bd"])

    for ch in chunks:
        q4, k4, kb4, vb4, egexp, ekexp = (ch[n] for n in ("q4", "k4", "kb4", "vb4", "egexp", "ekexp"))
        t4b = ch["t4"].astype(BF16)
        lstk = jnp.concatenate([jnp.where(blk[h], t4b, zb) for h in range(H)], axis=0)
        kbg4 = kb4 * egexp
        rstk = jnp.concatenate(
            [jnp.concatenate([vb, kbg], axis=1) for vb, kbg in zip(heads(vb4, DN_D), heads(kbg4, DN_D))],
            axis=0).astype(BF16)
        uw = dot(lstk, rstk)
        u4 = jnp.concatenate([uw[h * C:(h + 1) * C, 0:DN_D] for h in range(H)], axis=1)
        w4 = jnp.concatenate([uw[h * C:(h + 1) * C, DN_D:2 * DN_D] for h in range(H)], axis=1)
        ib = ch["intra"].astype(BF16)
        ch.update(
            u4=u4, wq=jnp.concatenate([w4, q4 * egexp], axis=0).astype(BF16),
            kd4=(k4 * ekexp).astype(BF16), gl4=egexp[C - 1:C, :],
            lint=jnp.concatenate([jnp.where(blk[h], ib, zb) for h in range(H)], axis=0))

    for ch in chunks:
        rs, u4, wq, kd4, gl4, lint = (ch[n] for n in ("rs", "u4", "wq", "kd4", "gl4", "lint"))
        ra = dot(wq[:, 0:CW], s0[...].astype(BF16))
        rc = dot(wq[:, CW:2 * CW], s1[...].astype(BF16))
        vn = u4 - jnp.concatenate([ra[0:C], rc[0:C]], axis=1)
        vnb = vn.astype(BF16)
        oi = dot(lint, jnp.concatenate(heads(vnb, DN_D), axis=0))
        o = (jnp.concatenate([ra[C:2 * C], rc[C:2 * C]], axis=1)
             + jnp.concatenate([oi[h * C:(h + 1) * C] for h in range(H)], axis=1))
        tn = (((0,), (0,)), ((), ()))
        s0[...] = s0[...] * gl4[:, 0:CW] + jnp.where(
            bd_pair, lax.dot_general(kd4[:, 0:CW], vnb[:, 0:CW], tn, preferred_element_type=F32), 0.0)
        s1[...] = s1[...] * gl4[:, CW:2 * CW] + jnp.where(
            bd_pair, lax.dot_general(kd4[:, CW:2 * CW], vnb[:, CW:2 * CW], tn, preferred_element_type=F32), 0.0)

        z = z_ref[0, rs, :].astype(F32)
        on = jnp.concatenate(
            [p * lax.rsqrt(jnp.mean(p * p, axis=-1, keepdims=True) + EPS) * nw for p in heads(o, DN_D)], axis=1)
        o_ref[0, rs, :] = (on * (z * _sigmoid(z))).astype(BF16)


def _gdn(dqkv, dz, gbc, conv_w, A_log, dt_bias, dn_norm_w, B, S):
    rb = 512
    prm = jnp.zeros((2, LANES), F32)
    prm = prm.at[0, DN_HEADS:2 * DN_HEADS].set(A_log.astype(F32))
    prm = prm.at[1, DN_HEADS:2 * DN_HEADS].set(dt_bias.astype(F32))
    out = pl.pallas_call(
        functools.partial(_gdn_kernel, rb=rb),
        grid=(B, S // rb),
        in_specs=[pl.BlockSpec((1, rb, DN_CONV), lambda b, i: (b, i, 0)),
                  pl.BlockSpec((1, rb, DN_W), lambda b, i: (b, i, 0)),
                  pl.BlockSpec((1, rb, LANES), lambda b, i: (b, i, 0)),
                  pl.BlockSpec((CONV_K, DN_CONV), lambda b, i: (0, 0)),
                  pl.BlockSpec((2, LANES), lambda b, i: (0, 0)),
                  pl.BlockSpec((1, DN_D), lambda b, i: (0, 0))],
        out_specs=pl.BlockSpec((1, rb, DN_W), lambda b, i: (b, i, 0)),
        out_shape=jax.ShapeDtypeStruct((B, S, DN_W), BF16),
        scratch_shapes=[pltpu.VMEM((rb + 8, DN_CONV), F32),
                        pltpu.VMEM((rb, DN_CONV), F32),
                        pltpu.VMEM((2 * DN_D, 2 * DN_D), F32),
                        pltpu.VMEM((2 * DN_D, 2 * DN_D), F32)],
        compiler_params=_cparams(("arbitrary", "arbitrary")),
        name="gdn",
    )(dqkv.reshape(B, S, DN_CONV), dz.reshape(B, S, DN_W), gbc.reshape(B, S, LANES),
      conv_w, prm, dn_norm_w.reshape(1, DN_D))
    return out.reshape(B * S, DN_W)


def _out_kernel(o1_ref, o2_ref, o3_ref, l1_ref, l2_ref, l3_ref, dn_ref, x_ref, mod_ref, wo_ref,
                n2_ref, wr_ref, br_ref,
                x1_ref, h2_ref, te_ref, rk_ref, gtc_ref, cnt_ref, base, scr, *, tm):
    i = pl.program_id(0)

    @pl.when(i == 0)
    def _():
        base[...] = jnp.zeros_like(base)

    def natural(ref, d):
        if d == 1:
            return ref[0, 0].astype(F32)
        nl = ref.shape[-1] // LANES
        for res in range(d):
            blk = ref[0, res].astype(F32)
            for c in range(nl):
                scr[c, pl.ds(res, tm // d, stride=d), :] = blk[:, c * LANES:(c + 1) * LANES]
        return jnp.concatenate([scr[c] for c in range(nl)], axis=1)

    l1, l2, l3 = (natural(r, d) for r, d in zip((l1_ref, l2_ref, l3_ref), DILATIONS))
    mx = jnp.maximum(jnp.maximum(l1, l2), l3)
    e1, e2, e3 = jnp.exp(l1 - mx), jnp.exp(l2 - mx), jnp.exp(l3 - mx)
    zs = e1 + e2 + e3
    er = lax.broadcasted_iota(I32, (LANES, ATTN_W), 0)
    ec = lax.broadcasted_iota(I32, (LANES, ATTN_W), 1)
    expand = jnp.where(ec // ATTN_HD == er, 1.0, 0.0).astype(BF16)
    attn = jnp.zeros((tm, ATTN_W), F32)
    for e, o_ref, d in zip((e1, e2, e3), (o1_ref, o2_ref, o3_ref), DILATIONS):
        wgt = jnp.dot((e / zs).astype(BF16), expand, preferred_element_type=F32)
        attn = attn + wgt * natural(o_ref, d)
    mix = (jnp.dot(attn.astype(BF16), wo_ref[0:ATTN_W, :], preferred_element_type=F32)
           + jnp.dot(dn_ref[...], wo_ref[ATTN_W:, :], preferred_element_type=F32))
    x1 = x_ref[...] + mod_ref[0, 2:3, :] * mix
    x1_ref[...] = x1
    ms = jnp.mean(x1 * x1, axis=-1, keepdims=True)
    h2 = x1 * lax.rsqrt(ms + EPS) * n2_ref[...]
    h2 = h2 * (1.0 + mod_ref[0, 4:5, :]) + mod_ref[0, 3:4, :]
    _store_slabs(h2_ref, h2)

    lg = _nt(wr_ref[...], h2, precision=HI) + br_ref[...]
    eidx = lax.broadcasted_iota(I32, (N_EXPERTS, tm), 0)
    vals, idxs, sels = [], [], []
    for _ in range(TOP_K):
        m = jnp.max(lg, axis=0, keepdims=True)
        idx = jnp.min(jnp.where(lg == m, eidx, N_EXPERTS), axis=0, keepdims=True)
        sel = eidx == idx
        vals.append(m)
        idxs.append(idx)
        sels.append(sel)
        lg = jnp.where(sel, -jnp.inf, lg)
    ex = [jnp.exp(v - vals[0]) for v in vals]
    den = ex[0] + ex[1] + ex[2] + ex[3]
    gates = [e / den for e in ex]

    msum = jnp.zeros((N_EXPERTS, tm), F32)
    for sel in sels:
        msum = msum + jnp.where(sel, 1.0, 0.0)
    tr = lax.broadcasted_iota(I32, (tm, tm), 0)
    tc = lax.broadcasted_iota(I32, (tm, tm), 1)
    upper = jnp.where(tr <= tc, 1.0, 0.0).astype(BF16)
    incl = jnp.dot(msum.astype(BF16), upper, preferred_element_type=F32)
    pos = base[:, 0:1] + (incl - msum)
    sub8 = lax.broadcasted_iota(I32, (8, tm), 0)
    te = jnp.zeros((8, tm), I32)
    rk = jnp.zeros((8, tm), I32)
    gt = jnp.zeros((8, tm), F32)
    for k in range(TOP_K):
        rank_k = jnp.sum(jnp.where(sels[k], pos, 0.0), axis=0, keepdims=True).astype(I32)
        te = jnp.where(sub8 == k, idxs[k], te)
        rk = jnp.where(sub8 == k, rank_k, rk)
        gt = jnp.where(sub8 == k, gates[k], gt)
    te_ref[...] = te
    rk_ref[...] = rk
    gtc_ref[...] = jnp.transpose(jnp.concatenate([gt, jnp.zeros((LANES - 8, tm), F32)], axis=0))
    base[...] = base[...] + jnp.sum(msum, axis=1, keepdims=True)
    cnt_ref[...] = base[...].astype(I32)


def _outproj(o1, o2, o3, l1, l2, l3, dn, x2, mod, w_out, norm2_w, w_router, b_router, S):
    T, D = x2.shape
    tm = 512
    nt = S // tm
    row = lambda w: pl.BlockSpec((tm, w), lambda i: (i, 0))
    res = lambda d, w: pl.BlockSpec((1, d, tm // d, w), lambda i: (i // nt, 0, i % nt, 0))
    colb = pl.BlockSpec((8, tm), lambda i: (0, i))
    return pl.pallas_call(
        functools.partial(_out_kernel, tm=tm),
        grid=(T // tm,),
        in_specs=[res(d, ATTN_W) for d in DILATIONS] + [res(d, LANES) for d in DILATIONS]
        + [row(DN_W), row(D),
                  pl.BlockSpec((1, 6, D), lambda i: (i * tm // S, 0, 0)),
                  pl.BlockSpec((D, D), lambda i: (0, 0)),
                  pl.BlockSpec((1, D), lambda i: (0, 0)),
                  pl.BlockSpec((N_EXPERTS, D), lambda i: (0, 0)),
                  pl.BlockSpec((N_EXPERTS, 1), lambda i: (0, 0))],
        out_specs=[row(D), pl.BlockSpec((tm * (D // LANES), LANES), lambda i: (i, 0)), colb, colb,
                   row(LANES), pl.BlockSpec((N_EXPERTS, LANES), lambda i: (0, 0))],
        out_shape=[jax.ShapeDtypeStruct((T, D), F32),
                   jax.ShapeDtypeStruct((T * (D // LANES), LANES), F32),
                   jax.ShapeDtypeStruct((8, T), I32),
                   jax.ShapeDtypeStruct((8, T), I32),
                   jax.ShapeDtypeStruct((T, LANES), F32),
                   jax.ShapeDtypeStruct((N_EXPERTS, LANES), I32)],
        scratch_shapes=[pltpu.VMEM((N_EXPERTS, LANES), F32),
                        pltpu.VMEM((ATTN_W // LANES, tm, LANES), F32)],
        compiler_params=_cparams(("arbitrary",)),
        name="outproj_router",
    )(o1, o2, o3, l1, l2, l3, dn, x2, mod, w_out.astype(BF16), norm2_w.reshape(1, D),
      jnp.transpose(w_router), b_router.reshape(N_EXPERTS, 1))


def _dest_kernel(ps_ref, te_ref, rk_ref, d_ref):
    te = te_ref[...]
    acc = jnp.zeros(te.shape, I32)
    for e in range(N_EXPERTS):
        acc = jnp.where(te == e, ps_ref[e], acc)
    d_ref[...] = acc + rk_ref[...]


def _dest(pstart, te, rk):
    T = te.shape[1]
    tb = 2048
    return pl.pallas_call(
        _dest_kernel,
        grid_spec=pltpu.PrefetchScalarGridSpec(
            num_scalar_prefetch=1,
            grid=(T // tb,),
            in_specs=[pl.BlockSpec((8, tb), lambda i, ps: (0, i)),
                      pl.BlockSpec((8, tb), lambda i, ps: (0, i))],
            out_specs=pl.BlockSpec((8, tb), lambda i, ps: (0, i))),
        out_shape=jax.ShapeDtypeStruct((8, T), I32),
        compiler_params=_cparams(("arbitrary",)),
        name="dest_rows",
    )(pstart, te, rk)


SC_CORES = 2
SC_SUBCORES = 16
SC_IDX_CHUNK = 128


def _invperm(dest_flat, P):
    N = dest_flat.shape[0]
    nw = SC_CORES * SC_SUBCORES
    nch = N // (nw * SC_IDX_CHUNK)
    mesh = plsc.VectorSubcoreMesh(core_axis_name="c", subcore_axis_name="s",
                                  num_cores=SC_CORES, num_subcores=SC_SUBCORES)

    @functools.partial(
        pl.kernel, mesh=mesh, out_type=jax.ShapeDtypeStruct((P,), I32),
        scratch_types=[pltpu.VMEM((nch, SC_IDX_CHUNK), I32), pltpu.VMEM((nch, SC_IDX_CHUNK), I32),
                       pltpu.SemaphoreType.DMA])
    def scatter_codes(idx_hbm, val_hbm, out_hbm, idx_v, val_v, sem):
        wid = lax.axis_index("s") * SC_CORES + lax.axis_index("c")
        pltpu.sync_copy(idx_hbm.at[wid], idx_v)
        pltpu.sync_copy(val_hbm.at[wid], val_v)

        @pl.loop(0, nch)
        def _(j):
            pltpu.async_copy(val_v.at[j], out_hbm.at[idx_v.at[j]], sem)

        @pl.loop(0, nch)
        def _(j):
            pltpu.make_async_copy(val_v.at[j], out_hbm.at[idx_v.at[j]], sem).wait()

    vals = jnp.arange(N, dtype=I32)
    return scatter_codes(dest_flat.reshape(nw, nch, SC_IDX_CHUNK), vals.reshape(nw, nch, SC_IDX_CHUNK))


def _row_copy(src, dst, sem):
    return pltpu.make_async_copy(src, dst, sem)


def _moe_kernel(be_ref, nv_ref, cc_ref, cn_ref, h2_ref, w1_ref, b1_ref, w2_ref, b2_ref, y4_ref,
                xbuf, ybuf, w1b, w2b, gsem, ssem, *, F, T, D, tme, nb):
    i = pl.program_id(0)
    s = i % 2
    ns = D // LANES

    def rows(first, n):
        return pl.ds(pl.multiple_of(first * ns, ns), n * ns)

    def gather_copy(tok, p, slot):
        return _row_copy(h2_ref.at[rows(tok, 1)], xbuf.at[slot, rows(p, 1)], gsem.at[slot])

    def token_of(code):
        return code & (T - 1) if T & (T - 1) == 0 else code % T

    def issue_rows(start_row, nv):
        ng = nv // ROW_UNROLL

        def grp(g, c):
            for j in range(ROW_UNROLL):
                start_row(g * ROW_UNROLL + j, j % 2)
            return c
        lax.fori_loop(0, ng, grp, 0)

        def one(p, c):
            start_row(p, 0)
            return c
        lax.fori_loop(ng * ROW_UNROLL, nv, one, 0)

    def start_gather(code_ref, nv, slot):
        issue_rows(lambda p, pri: gather_copy(token_of(code_ref[0, 0, p]), p, slot).start(priority=pri), nv)

    def wait_rows(make, nv):
        @pl.when(nv > 0)
        def _():
            make(nv).wait()

    def gathered(n, slot):
        return _row_copy(h2_ref.at[rows(0, n)], xbuf.at[slot, rows(0, n)], gsem.at[slot])

    def scattered(n, slot):
        return _row_copy(ybuf.at[slot, rows(0, n)], y4_ref.at[rows(0, n)], ssem.at[slot])

    @pl.when(i == 0)
    def _():
        start_gather(cc_ref, tme, 0)

    nv = nv_ref[i]
    gathered(tme, s).wait()

    @pl.when(i >= 2)
    def _():
        wait_rows(lambda n: scattered(n, s), nv_ref[i - 2])

    @pl.when(jnp.logical_or(i == 0, be_ref[i] != be_ref[jnp.maximum(i - 1, 0)]))
    def _():
        w1b[...] = w1_ref[0].astype(BF16)
        w2b[...] = w2_ref[0].astype(BF16)

    @pl.when(nv > 0)
    def _():
        x = _load_slabs(xbuf, tme, D, lead=(s,)).astype(BF16)
        for p in range(tme):
            gather_copy(token_of(cn_ref[0, 0, p]), p, 1 - s).start(priority=p % 2)
        hgu = jnp.dot(x, w1b[...], preferred_element_type=F32) + b1_ref[0]
        gate = jnp.minimum(hgu[:, :F], SWIGLU_LIMIT)
        up = jnp.clip(hgu[:, F:], -SWIGLU_LIMIT, SWIGLU_LIMIT)
        act = gate * _sigmoid(SWIGLU_ALPHA * gate) * (up + 1.0)
        y = jnp.dot(act.astype(BF16), w2b[...], preferred_element_type=F32) + b2_ref[0]
        _store_slabs(ybuf, y, lead=(s,))

        issue_rows(lambda p, pri: _row_copy(ybuf.at[s, rows(p, 1)], y4_ref.at[rows(cc_ref[0, 0, p], 1)],
                                            ssem.at[s]).start(priority=pri), nv)

    @pl.when(jnp.logical_and(nv == 0, i + 1 < nb))
    def _():
        start_gather(cn_ref, tme, 1 - s)

    @pl.when(i == nb - 1)
    def _():
        @pl.when(nv > 0)
        def _():
            gathered(tme, 1 - s).wait()
        wait_rows(lambda n: scattered(n, s), nv)
        if nb > 1:
            wait_rows(lambda n: scattered(n, 1 - s), nv_ref[i - 1])


def _experts(blk_exp, blk_valid, codes, h2s, w1, b1, w2, b2, tme):
    E, D, F2 = w1.shape
    F = F2 // 2
    ns = D // LANES
    T = h2s.shape[0] // ns
    nb = blk_exp.shape[0]
    codes3 = codes.reshape(nb, 1, tme)
    wspec = lambda shape: pl.BlockSpec(shape, lambda i, be, nv: (be[i], 0, 0))
    cspec = lambda off: pl.BlockSpec((1, 1, tme), lambda i, be, nv: (jnp.minimum(i + off, nb - 1), 0, 0),
                                     memory_space=pltpu.SMEM)
    return pl.pallas_call(
        functools.partial(_moe_kernel, F=F, T=T, D=D, tme=tme, nb=nb),
        grid_spec=pltpu.PrefetchScalarGridSpec(
            num_scalar_prefetch=2,
            grid=(nb,),
            in_specs=[cspec(0), cspec(1),
                      pl.BlockSpec(memory_space=pl.ANY),
                      wspec((1, D, F2)), wspec((1, 1, F2)), wspec((1, F, D)), wspec((1, 1, D))],
            out_specs=pl.BlockSpec(memory_space=pl.ANY),
            scratch_shapes=[pltpu.VMEM((2, tme * ns, LANES), F32), pltpu.VMEM((2, tme * ns, LANES), F32),
                            pltpu.VMEM((D, F2), BF16), pltpu.VMEM((F, D), BF16),
                            pltpu.SemaphoreType.DMA((2,)), pltpu.SemaphoreType.DMA((2,))]),
        out_shape=jax.ShapeDtypeStruct((TOP_K * T * ns, LANES), F32),
        compiler_params=_cparams(("arbitrary",)),
        name="experts",
    )(blk_exp, blk_valid, codes3, codes3, h2s, w1, b1.reshape(E, 1, F2), w2, b2.reshape(E, 1, D))


def _comb_kernel(g_ref, x1_ref, mod_ref, fw_ref, y0_ref, y1_ref, y2_ref, y3_ref, o_ref):
    g = g_ref[...]
    n, d = x1_ref.shape
    y = g[:, 0:1] * _load_slabs(y0_ref, n, d)
    for k, y_ref in ((1, y1_ref), (2, y2_ref), (3, y3_ref)):
        y = y + g[:, k:k + 1] * _load_slabs(y_ref, n, d)
    x2 = x1_ref[...] + mod_ref[0, 5:6, :] * y
    ms = jnp.mean(x2 * x2, axis=-1, keepdims=True)
    o_ref[...] = x2 * lax.rsqrt(ms + EPS) * fw_ref[...]


def _combine(gtc, x1, mod, final_w, y4, S):
    T, D = x1.shape
    tmc = 512
    nt = T // tmc
    yspec = lambda k: pl.BlockSpec((tmc * (D // LANES), LANES), lambda i: (k * nt + i, 0))
    return pl.pallas_call(
        _comb_kernel,
        grid=(nt,),
        in_specs=[pl.BlockSpec((tmc, LANES), lambda i: (i, 0)),
                  pl.BlockSpec((tmc, D), lambda i: (i, 0)),
                  pl.BlockSpec((1, 6, D), lambda i: (i * tmc // S, 0, 0)),
                  pl.BlockSpec((1, D), lambda i: (0, 0)),
                  yspec(0), yspec(1), yspec(2), yspec(3)],
        out_specs=pl.BlockSpec((tmc, D), lambda i: (i, 0)),
        out_shape=jax.ShapeDtypeStruct((T, D), F32),
        compiler_params=_cparams(("arbitrary",)),
        name="combine",
    )(gtc, x1, mod, final_w.reshape(1, D), y4, y4, y4, y4)


def _layer(x2, mod, B, S, norm1_w, w_in, conv_w, A_log, dt_bias, dn_norm_w, w_out,
           norm2_w, w_router, b_router, w1, b1, w2, b2):
    T, D = x2.shape
    qkv, qkv4, qkv16, dqkv, dz, gbc = _inproj(x2, mod, norm1_w, w_in, S)
    views = (qkv.reshape(B, 1, S, 3 * ATTN_W), qkv4, qkv16)
    branches = [_attn_branch(v, d) for v, d in zip(views, DILATIONS)]
    dn = _gdn(dqkv, dz, gbc, conv_w, A_log, dt_bias, dn_norm_w, B, S)
    (o1, l1), (o2, l2), (o3, l3) = branches
    x1, h2, te, rk, gtc, cnt = _outproj(o1, o2, o3, l1, l2, l3, dn, x2, mod, w_out, norm2_w,
                                        w_router, b_router, S)
    tme = MOE_ROWS
    P = T * TOP_K + N_EXPERTS * tme
    counts = cnt[:, 0]
    padded = (counts + tme - 1) // tme * tme
    pend = jnp.cumsum(padded)
    pstart = (pend - padded).astype(I32)
    blk_start = jnp.arange(P // tme, dtype=I32) * tme
    blk_exp = jnp.minimum(jnp.sum((pend[None, :] <= blk_start[:, None]).astype(I32), axis=1),
                          N_EXPERTS - 1).astype(I32)
    blk_valid = jnp.clip(pstart[blk_exp] + counts[blk_exp] - blk_start, 0, tme).astype(I32)
    dest = _dest(pstart, te, rk)
    codes = _invperm(dest[:TOP_K].reshape(TOP_K * T), P)
    y4 = _experts(blk_exp, blk_valid, codes, h2, w1, b1, w2, b2, tme)
    return x1, gtc, y4


def kernel(x, c, w_ada, b_ada, norm1_w, w_in, conv_w, A_log, dt_bias, dn_norm_w, w_out, norm2_w,
           w_router, b_router, w1, b1, w2, b2, final_norm_w):
    B, S, D = x.shape
    depth = w_ada.shape[0]
    assert S % (ATTN_BLK * DILATIONS[-1]) == 0 and depth == 1
    x2 = x.reshape(B * S, D)
    mod = _ada(c, w_ada[0], b_ada[0])
    x1, gtc, y4 = _layer(x2, mod, B, S, norm1_w[0], w_in[0], conv_w[0], A_log[0], dt_bias[0],
                         dn_norm_w[0], w_out[0], norm2_w[0], w_router[0], b_router[0],
                         w1[0], b1[0], w2[0], b2[0])
    out = _combine(gtc, x1, mod, final_norm_w, y4, S)
    return out.reshape(B, S, D)
```

```python
import functools

import jax
import jax.numpy as jnp
from jax import lax
from jax.experimental import pallas as pl
from jax.experimental.pallas import tpu as pltpu
from jax.experimental.pallas import tpu_sc as plsc

F32 = jnp.float32
BF16 = jnp.bfloat16
I32 = jnp.int32
HI = lax.Precision.HIGHEST

LANES = 128
ATTN_HEADS = 8
ATTN_HD = 64
ATTN_W = ATTN_HEADS * ATTN_HD
ATTN_BLK = 128
DILATIONS = (1, 4, 16)
DN_HEADS = 4
DN_D = 128
DN_W = DN_HEADS * DN_D
DN_CONV = 3 * DN_W
CONV_K = 4
DN_CHUNK = 64
N_EXPERTS = 32
TOP_K = 4
MOE_ROWS = 512
ROW_UNROLL = 8
SWIGLU_LIMIT = 7.0
SWIGLU_ALPHA = 1.702
EPS = 1e-6
NEG = -1e30
MAIN_COLS = 3 * ATTN_W + DN_CONV + DN_W

VMEM_LIMIT = 56 * 1024 * 1024


def _cparams(sem):
    return pltpu.CompilerParams(dimension_semantics=sem, vmem_limit_bytes=VMEM_LIMIT)


def _nt(a, b, **kw):
    return lax.dot_general(a, b, (((1,), (1,)), ((), ())), preferred_element_type=F32, **kw)


def _sigmoid(x):
    return 1.0 / (1.0 + jnp.exp(-x))


def _store_slabs(ref, val, lead=()):
    n, d = val.shape
    ns = d // LANES
    for c in range(ns):
        ref[lead + (pl.ds(c, n, stride=ns), slice(None))] = val[:, c * LANES:(c + 1) * LANES]


def _load_slabs(ref, n, d, lead=()):
    ns = d // LANES
    return jnp.concatenate([ref[lead + (pl.ds(c, n, stride=ns), slice(None))] for c in range(ns)], axis=1)


def _ada_kernel(c_ref, w_ref, b_ref, o_ref):
    c = c_ref[...]
    cond = c * _sigmoid(c)
    o_ref[...] = jnp.dot(cond, w_ref[...], preferred_element_type=F32, precision=HI) + b_ref[...]


def _ada(c, w_ada, b_ada):
    B, D = c.shape
    N = w_ada.shape[1]
    cp = jnp.zeros((8, D), F32).at[:B].set(c)
    tn = 1024
    out = pl.pallas_call(
        _ada_kernel,
        grid=(N // tn,),
        in_specs=[pl.BlockSpec((8, D), lambda j: (0, 0)),
                  pl.BlockSpec((D, tn), lambda j: (0, j)),
                  pl.BlockSpec((1, tn), lambda j: (0, j))],
        out_specs=pl.BlockSpec((8, tn), lambda j: (0, j)),
        out_shape=jax.ShapeDtypeStruct((8, N), F32),
        compiler_params=_cparams(("arbitrary",)),
        name="ada",
    )(cp, w_ada, b_ada.reshape(1, N))
    return out[:B].reshape(B, 6, D)


def _inproj_kernel(x_ref, mod_ref, nw_ref, wm_ref, ws_ref, qkv_ref, qkv4_ref, qkv16_ref, dqkv_ref, dz_ref,
                   gbc_ref, scr):
    x = x_ref[...]
    tm = x.shape[0]
    shift = mod_ref[0, 0:1, :]
    scale = mod_ref[0, 1:2, :]
    ms = jnp.mean(x * x, axis=-1, keepdims=True)
    h = x * lax.rsqrt(ms + EPS) * nw_ref[...]
    hb = (h * (1.0 + scale) + shift).astype(BF16)
    nl = ATTN_W // LANES
    for j in range(3):
        cs = slice(j * ATTN_W, (j + 1) * ATTN_W)
        r = jnp.dot(hb, wm_ref[:, cs], preferred_element_type=F32)
        if j == 0:
            r = r * (ATTN_HD ** -0.5)
        qkv_ref[:, cs] = r.astype(BF16)
        for c in range(nl):
            scr[c] = r[:, c * LANES:(c + 1) * LANES]
        for d, ref in ((DILATIONS[1], qkv4_ref), (DILATIONS[2], qkv16_ref)):
            for res in range(d):
                part = jnp.concatenate([scr[c, pl.ds(res, tm // d, stride=d), :] for c in range(nl)], axis=1)
                ref[0, res, :, cs] = part.astype(BF16)
    for j in range(3):
        c0 = 3 * ATTN_W + j * DN_W
        r = jnp.dot(hb, wm_ref[:, c0:c0 + DN_W], preferred_element_type=F32)
        dqkv_ref[:, j * DN_W:(j + 1) * DN_W] = r.astype(BF16)
    c0 = 3 * ATTN_W + DN_CONV
    dz_ref[...] = jnp.dot(hb, wm_ref[:, c0:c0 + DN_W], preferred_element_type=F32).astype(BF16)
    gbc_ref[...] = jnp.dot(hb, ws_ref[...], preferred_element_type=F32)


def _inproj(x2, mod, norm_w, w_in, S):
    T, D = x2.shape
    tm = 512
    B, nt = T // S, S // tm
    wm = w_in[:, :MAIN_COLS].astype(BF16)
    ws = jnp.zeros((D, LANES), F32).at[:, :2 * DN_HEADS].set(w_in[:, MAIN_COLS:]).astype(BF16)
    return pl.pallas_call(
        _inproj_kernel,
        grid=(T // tm,),
        in_specs=[pl.BlockSpec((tm, D), lambda i: (i, 0)),
                  pl.BlockSpec((1, 6, D), lambda i: (i * tm // S, 0, 0)),
                  pl.BlockSpec((1, D), lambda i: (0, 0)),
                  pl.BlockSpec((D, MAIN_COLS), lambda i: (0, 0)),
                  pl.BlockSpec((D, LANES), lambda i: (0, 0))],
        out_specs=[pl.BlockSpec((tm, 3 * ATTN_W), lambda i: (i, 0))]
        + [pl.BlockSpec((1, d, tm // d, 3 * ATTN_W), lambda i: (i // nt, 0, i % nt, 0)) for d in DILATIONS[1:]]
        + [pl.BlockSpec((tm, DN_CONV), lambda i: (i, 0)),
           pl.BlockSpec((tm, DN_W), lambda i: (i, 0)),
           pl.BlockSpec((tm, LANES), lambda i: (i, 0))],
        out_shape=[jax.ShapeDtypeStruct((T, 3 * ATTN_W), BF16)]
        + [jax.ShapeDtypeStruct((B, d, S // d, 3 * ATTN_W), BF16) for d in DILATIONS[1:]]
        + [jax.ShapeDtypeStruct((T, DN_CONV), BF16),
           jax.ShapeDtypeStruct((T, DN_W), BF16),
           jax.ShapeDtypeStruct((T, LANES), F32)],
        scratch_shapes=[pltpu.VMEM((ATTN_W // LANES, tm, LANES), F32)],
        compiler_params=_cparams(("arbitrary",)),
        name="inproj",
    )(x2, mod, norm_w.reshape(1, D), wm, ws)


def _attn_kernel(q_ref, kc_ref, kp_ref, vc_ref, vp_ref, o_ref, lse_ref, kf, vf, *, qb):
    n = pl.program_id(2)
    kf[0:ATTN_BLK, :] = kp_ref[0, 0]
    kf[ATTN_BLK:, :] = kc_ref[0, 0]
    vf[0:ATTN_BLK, :] = vp_ref[0, 0]
    vf[ATTN_BLK:, :] = vc_ref[0, 0]
    row = lax.broadcasted_iota(I32, (ATTN_BLK, 2 * ATTN_BLK), 0)
    col = lax.broadcasted_iota(I32, (ATTN_BLK, 2 * ATTN_BLK), 1)
    band = jnp.logical_or(jnp.logical_and(col < ATTN_BLK, col >= row),
                          jnp.logical_and(col >= ATTN_BLK, col - ATTN_BLK <= row))
    lane = lax.broadcasted_iota(I32, (ATTN_BLK, LANES), 1)
    lo = lane < ATTN_HD

    def sub(j, carry):
        r0 = pl.multiple_of(j * ATTN_BLK, ATTN_BLK)
        first_col = jnp.where(jnp.logical_and(n == 0, j == 0), ATTN_BLK, 0)
        mask = jnp.logical_and(band, col >= first_col)
        npair = ATTN_W // LANES
        cols = [slice(hp * LANES, (hp + 1) * LANES) for hp in range(npair)]
        heads = [(hp, half) for hp in range(npair) for half in range(2)]
        scores = []
        for hp, half in heads:
            q2 = q_ref[0, 0, pl.ds(r0, ATTN_BLK), cols[hp]]
            qm = jnp.where(lo if half == 0 else jnp.logical_not(lo), q2, jnp.zeros_like(q2))
            scores.append(jnp.where(mask, _nt(qm, kf[pl.ds(r0, 2 * ATTN_BLK), cols[hp]]), NEG))
        maxes = [jnp.max(s, axis=-1, keepdims=True) for s in scores]
        probs = [jnp.exp(s - m) for s, m in zip(scores, maxes)]
        dens = [jnp.sum(p, axis=-1, keepdims=True) for p in probs]
        accs = [jnp.dot(p.astype(BF16), vf[pl.ds(r0, 2 * ATTN_BLK), cols[hp]], preferred_element_type=F32)
                for p, (hp, _) in zip(probs, heads)]
        outs = [a / d for a, d in zip(accs, dens)]
        lse_tile = jnp.zeros((ATTN_BLK, LANES), F32)
        for h, (m, d) in enumerate(zip(maxes, dens)):
            lse_tile = jnp.where(lane == h, m + jnp.log(d), lse_tile)
        for hp in range(npair):
            o_ref[0, 0, pl.ds(r0, ATTN_BLK), cols[hp]] = jnp.where(lo, outs[2 * hp], outs[2 * hp + 1]).astype(BF16)
        lse_ref[0, 0, pl.ds(r0, ATTN_BLK), :] = lse_tile
        return carry

    lax.fori_loop(0, qb // ATTN_BLK, sub, 0)


def _attn_branch(qkv, d):
    B, _, L, _ = qkv.shape
    qb = min(512, L)
    nsub = qb // ATTN_BLK
    cur = lambda c: pl.BlockSpec((1, 1, qb, ATTN_W), lambda b, r, n: (b, r, n, c))
    prev = lambda c: pl.BlockSpec((1, 1, ATTN_BLK, ATTN_W),
                                  lambda b, r, n: (b, r, jnp.maximum(n * nsub - 1, 0), c))
    return pl.pallas_call(
        functools.partial(_attn_kernel, qb=qb),
        grid=(B, d, L // qb),
        in_specs=[cur(0), cur(1), prev(1), cur(2), prev(2)],
        out_specs=[pl.BlockSpec((1, 1, qb, ATTN_W), lambda b, r, n: (b, r, n, 0)),
                   pl.BlockSpec((1, 1, qb, LANES), lambda b, r, n: (b, r, n, 0))],
        out_shape=[jax.ShapeDtypeStruct((B, d, L, ATTN_W), BF16),
                   jax.ShapeDtypeStruct((B, d, L, LANES), F32)],
        scratch_shapes=[pltpu.VMEM((qb + ATTN_BLK, ATTN_W), BF16),
                        pltpu.VMEM((qb + ATTN_BLK, ATTN_W), BF16)],
        compiler_params=_cparams(("arbitrary", "arbitrary", "arbitrary")),
        name=f"attn_d{d}",
    )(qkv, qkv, qkv, qkv, qkv)


def _gdn_kernel(x_ref, z_ref, g_ref, cw_ref, prm_ref, nw_ref, o_ref, xext, yc, s0, s1, *, rb):
    i = pl.program_id(1)

    @pl.when(i == 0)
    def _():
        xext[0:8, :] = jnp.zeros((8, DN_CONV), F32)
        s0[...] = jnp.zeros_like(s0)
        s1[...] = jnp.zeros_like(s1)

    @pl.when(i > 0)
    def _():
        xext[0:8, :] = xext[rb:rb + 8, :]

    xext[8:, :] = x_ref[0].astype(F32)
    y = cw_ref[CONV_K - 1:CONV_K, :] * xext[8:8 + rb, :]
    for j in range(CONV_K - 1):
        off = 8 - (CONV_K - 1) + j
        y = y + cw_ref[j:j + 1, :] * xext[off:off + rb, :]
    yc[...] = y * _sigmoid(y)

    C = DN_CHUNK
    H = DN_HEADS
    CW = H * C
    dot = functools.partial(jnp.dot, preferred_element_type=F32)

    def iota(shape, d):
        return lax.broadcasted_iota(I32, shape, d)

    ltri_b = jnp.where(iota((C, C), 0) >= iota((C, C), 1), 1.0, 0.0).astype(BF16)
    lane = iota((C, LANES), 1)
    blane = lane < H
    glane = jnp.logical_and(lane >= H, lane < 2 * H)
    e512 = jnp.where(jnp.logical_or(iota((LANES, DN_W), 1) // DN_D == iota((LANES, DN_W), 0),
                                    iota((LANES, DN_W), 1) // DN_D == iota((LANES, DN_W), 0) - H),
                     1.0, 0.0).astype(BF16)
    e256 = jnp.where(iota((LANES, CW), 1) // C == iota((LANES, CW), 0) - H, 1.0, 0.0).astype(BF16)
    row4 = iota((C, CW), 0)
    col4 = iota((C, CW), 1) % C
    eye4 = jnp.where(row4 == col4, 1.0, 0.0).astype(F32)
    blk = [iota((C, CW), 1) // C == h for h in range(H)]
    bd_cc = iota((CW, CW), 0) // C == iota((CW, CW), 1) // C
    bd_pair = iota((CW, CW), 0) // DN_D == iota((CW, CW), 1) // DN_D
    rt_mask = iota((CW, DN_W), 0) // C == iota((CW, DN_W), 1) // DN_D
    neg_a = -jnp.exp(prm_ref[0:1, :])
    dtb = prm_ref[1:2, :]
    nw = nw_ref[...]

    def hilo(x):
        hi = x.astype(BF16)
        return hi, (x - hi.astype(F32)).astype(BF16)

    def heads(a, w):
        return [a[:, h * w:(h + 1) * w] for h in range(H)]

    def l2n(a, mult):
        return jnp.concatenate(
            [p * (lax.rsqrt(jnp.sum(p * p, axis=-1, keepdims=True) + EPS) * mult) for p in heads(a, DN_D)],
            axis=1)

    def stack4(a):
        return jnp.concatenate([a, a, a, a], axis=0)

    zb = jnp.zeros((), BF16)
    nchunk = rb // C
    chunks = []
    for c in range(nchunk):
        rs = slice(c * C, (c + 1) * C)
        G = g_ref[0, rs, :]
        xg = G + dtb
        gv = jnp.where(glane, neg_a * (jnp.maximum(xg, 0.0) + jnp.log1p(jnp.exp(-jnp.abs(xg)))), 0.0)
        be = jnp.where(blane, _sigmoid(G), 0.0)
        g_hi, g_lo = hilo(gv)
        gcum = dot(ltri_b, g_hi) + dot(ltri_b, g_lo)
        gtot = gcum[C - 1:C, :]
        eg = jnp.where(glane, jnp.exp(gcum), 0.0)
        ek = jnp.where(glane, jnp.exp(gtot - gcum), 0.0)
        ex = dot(jnp.concatenate([be, eg, ek], axis=0).astype(BF16), e512)
        bexp, egexp, ekexp = ex[0:C], ex[C:2 * C], ex[2 * C:3 * C]
        gexp = dot(g_hi, e256) + dot(g_lo, e256)
        d_hi, d_lo = hilo(jnp.where(row4 > col4, gexp, 0.0))
        diff = dot(ltri_b, d_hi) + dot(ltri_b, d_lo)
        decay = jnp.exp(jnp.where(row4 >= col4, diff, NEG))

        q4 = l2n(yc[rs, 0:DN_W], DN_D ** -0.5)
        k4 = l2n(yc[rs, DN_W:2 * DN_W], 1.0)
        v4 = yc[rs, 2 * DN_W:3 * DN_W]
        kb4 = k4 * bexp
        vb4 = v4 * bexp
        rt = jnp.where(rt_mask, stack4(k4.astype(BF16)), jnp.zeros((), BF16))
        ai = _nt(jnp.concatenate([kb4, q4], axis=0).astype(BF16), rt)
        a4 = jnp.where(row4 > col4, ai[0:C] * decay, 0.0)
        pb = a4.astype(BF16)
        chunks.append(dict(
            rs=rs, q4=q4, k4=k4, kb4=kb4, vb4=vb4, egexp=egexp, ekexp=ekexp,
            intra=ai[C:2 * C] * decay, t4=eye4 - a4, pb=pb,
            bd=jnp.where(bd_cc, stack4(pb), zb)))

    for _ in range(5):
        for ch in chunks:
            ch["pb"] = dot(ch["pb"], ch["bd"]).astype(BF16)
        for ch in chunks:
            ch["bd"] = jnp.where(bd_cc, stack4(ch["pb"]), zb)
        for ch in chunks:
            ch["t4"] = ch["t4"] + dot(ch["t4"].astype(BF16), ch["bd"])

    for ch in chunks:
        q4, k4, kb4, vb4, egexp, ekexp = (ch[n] for n in ("q4", "k4", "kb4", "vb4", "egexp", "ekexp"))
        t4b = ch["t4"].astype(BF16)
        lstk = jnp.concatenate([jnp.where(blk[h], t4b, zb) for h in range(H)], axis=0)
        kbg4 = kb4 * egexp
        rstk = jnp.concatenate(
            [jnp.concatenate([vb, kbg], axis=1) for vb, kbg in zip(heads(vb4, DN_D), heads(kbg4, DN_D))],
            axis=0).astype(BF16)
        uw = dot(lstk, rstk)
        u4 = jnp.concatenate([uw[h * C:(h + 1) * C, 0:DN_D] for h in range(H)], axis=1)
        w4 = jnp.concatenate([uw[h * C:(h + 1) * C, DN_D:2 * DN_D] for h in range(H)], axis=1)
        ib = ch["intra"].astype(BF16)
        ch.update(
            u4=u4, wq=jnp.concatenate([w4, q4 * egexp], axis=0).astype(BF16),
            kd4=(k4 * ekexp).astype(BF16), gl4=egexp[C - 1:C, :],
            lint=jnp.concatenate([jnp.where(blk[h], ib, zb) for h in range(H)], axis=0))

    for ch in chunks:
        rs, u4, wq, kd4, gl4, lint = (ch[n] for n in ("rs", "u4", "wq", "kd4", "gl4", "lint"))
        ra = dot(wq[:, 0:CW], s0[...].astype(BF16))
        rc = dot(wq[:, CW:2 * CW], s1[...].astype(BF16))
        vn = u4 - jnp.concatenate([ra[0:C], rc[0:C]], axis=1)
        vnb = vn.astype(BF16)
        oi = dot(lint, jnp.concatenate(heads(vnb, DN_D), axis=0))
        o = (jnp.concatenate([ra[C:2 * C], rc[C:2 * C]], axis=1)
             + jnp.concatenate([oi[h * C:(h + 1) * C] for h in range(H)], axis=1))
        tn = (((0,), (0,)), ((), ()))
        s0[...] = s0[...] * gl4[:, 0:CW] + jnp.where(
            bd_pair, lax.dot_general(kd4[:, 0:CW], vnb[:, 0:CW], tn, preferred_element_type=F32), 0.0)
        s1[...] = s1[...] * gl4[:, CW:2 * CW] + jnp.where(
            bd_pair, lax.dot_general(kd4[:, CW:2 * CW], vnb[:, CW:2 * CW], tn, preferred_element_type=F32), 0.0)

        z = z_ref[0, rs, :].astype(F32)
        on = jnp.concatenate(
            [p * lax.rsqrt(jnp.mean(p * p, axis=-1, keepdims=True) + EPS) * nw for p in heads(o, DN_D)], axis=1)
        o_ref[0, rs, :] = (on * (z * _sigmoid(z))).astype(BF16)


def _gdn(dqkv, dz, gbc, conv_w, A_log, dt_bias, dn_norm_w, B, S):
    rb = 512
    prm = jnp.zeros((2, LANES), F32)
    prm = prm.at[0, DN_HEADS:2 * DN_HEADS].set(A_log.astype(F32))
    prm = prm.at[1, DN_HEADS:2 * DN_HEADS].set(dt_bias.astype(F32))
    out = pl.pallas_call(
        functools.partial(_gdn_kernel, rb=rb),
        grid=(B, S // rb),
        in_specs=[pl.BlockSpec((1, rb, DN_CONV), lambda b, i: (b, i, 0)),
                  pl.BlockSpec((1, rb, DN_W), lambda b, i: (b, i, 0)),
                  pl.BlockSpec((1, rb, LANES), lambda b, i: (b, i, 0)),
                  pl.BlockSpec((CONV_K, DN_CONV), lambda b, i: (0, 0)),
                  pl.BlockSpec((2, LANES), lambda b, i: (0, 0)),
                  pl.BlockSpec((1, DN_D), lambda b, i: (0, 0))],
        out_specs=pl.BlockSpec((1, rb, DN_W), lambda b, i: (b, i, 0)),
        out_shape=jax.ShapeDtypeStruct((B, S, DN_W), BF16),
        scratch_shapes=[pltpu.VMEM((rb + 8, DN_CONV), F32),
                        pltpu.VMEM((rb, DN_CONV), F32),
                        pltpu.VMEM((2 * DN_D, 2 * DN_D), F32),
                        pltpu.VMEM((2 * DN_D, 2 * DN_D), F32)],
        compiler_params=_cparams(("arbitrary", "arbitrary")),
        name="gdn",
    )(dqkv.reshape(B, S, DN_CONV), dz.reshape(B, S, DN_W), gbc.reshape(B, S, LANES),
      conv_w, prm, dn_norm_w.reshape(1, DN_D))
    return out.reshape(B * S, DN_W)


def _out_kernel(o1_ref, o2_ref, o3_ref, l1_ref, l2_ref, l3_ref, dn_ref, x_ref, mod_ref, wo_ref,
                n2_ref, wr_ref, br_ref,
                x1_ref, h2_ref, te_ref, rk_ref, gtc_ref, cnt_ref, base, scr, *, tm):
    i = pl.program_id(0)

    @pl.when(i == 0)
    def _():
        base[...] = jnp.zeros_like(base)

    def natural(ref, d):
        if d == 1:
            return ref[0, 0].astype(F32)
        nl = ref.shape[-1] // LANES
        for res in range(d):
            blk = ref[0, res].astype(F32)
            for c in range(nl):
                scr[c, pl.ds(res, tm // d, stride=d), :] = blk[:, c * LANES:(c + 1) * LANES]
        return jnp.concatenate([scr[c] for c in range(nl)], axis=1)

    l1, l2, l3 = (natural(r, d) for r, d in zip((l1_ref, l2_ref, l3_ref), DILATIONS))
    mx = jnp.maximum(jnp.maximum(l1, l2), l3)
    e1, e2, e3 = jnp.exp(l1 - mx), jnp.exp(l2 - mx), jnp.exp(l3 - mx)
    zs = e1 + e2 + e3
    er = lax.broadcasted_iota(I32, (LANES, ATTN_W), 0)
    ec = lax.broadcasted_iota(I32, (LANES, ATTN_W), 1)
    expand = jnp.where(ec // ATTN_HD == er, 1.0, 0.0).astype(BF16)
    attn = jnp.zeros((tm, ATTN_W), F32)
    for e, o_ref, d in zip((e1, e2, e3), (o1_ref, o2_ref, o3_ref), DILATIONS):
        wgt = jnp.dot((e / zs).astype(BF16), expand, preferred_element_type=F32)
        attn = attn + wgt * natural(o_ref, d)
    mix = (jnp.dot(attn.astype(BF16), wo_ref[0:ATTN_W, :], preferred_element_type=F32)
           + jnp.dot(dn_ref[...], wo_ref[ATTN_W:, :], preferred_element_type=F32))
    x1 = x_ref[...] + mod_ref[0, 2:3, :] * mix
    x1_ref[...] = x1
    ms = jnp.mean(x1 * x1, axis=-1, keepdims=True)
    h2 = x1 * lax.rsqrt(ms + EPS) * n2_ref[...]
    h2 = h2 * (1.0 + mod_ref[0, 4:5, :]) + mod_ref[0, 3:4, :]
    _store_slabs(h2_ref, h2)

    lg = _nt(wr_ref[...], h2, precision=HI) + br_ref[...]
    eidx = lax.broadcasted_iota(I32, (N_EXPERTS, tm), 0)
    vals, idxs, sels = [], [], []
    for _ in range(TOP_K):
        m = jnp.max(lg, axis=0, keepdims=True)
        idx = jnp.min(jnp.where(lg == m, eidx, N_EXPERTS), axis=0, keepdims=True)
        sel = eidx == idx
        vals.append(m)
        idxs.append(idx)
        sels.append(sel)
        lg = jnp.where(sel, -jnp.inf, lg)
    ex = [jnp.exp(v - vals[0]) for v in vals]
    den = ex[0] + ex[1] + ex[2] + ex[3]
    gates = [e / den for e in ex]

    msum = jnp.zeros((N_EXPERTS, tm), F32)
    for sel in sels:
        msum = msum + jnp.where(sel, 1.0, 0.0)
    tr = lax.broadcasted_iota(I32, (tm, tm), 0)
    tc = lax.broadcasted_iota(I32, (tm, tm), 1)
    upper = jnp.where(tr <= tc, 1.0, 0.0).astype(BF16)
    incl = jnp.dot(msum.astype(BF16), upper, preferred_element_type=F32)
    pos = base[:, 0:1] + (incl - msum)
    sub8 = lax.broadcasted_iota(I32, (8, tm), 0)
    te = jnp.zeros((8, tm), I32)
    rk = jnp.zeros((8, tm), I32)
    gt = jnp.zeros((8, tm), F32)
    for k in range(TOP_K):
        rank_k = jnp.sum(jnp.where(sels[k], pos, 0.0), axis=0, keepdims=True).astype(I32)
        te = jnp.where(sub8 == k, idxs[k], te)
        rk = jnp.where(sub8 == k, rank_k, rk)
        gt = jnp.where(sub8 == k, gates[k], gt)
    te_ref[...] = te
    rk_ref[...] = rk
    gtc_ref[...] = jnp.transpose(jnp.concatenate([gt, jnp.zeros((LANES - 8, tm), F32)], axis=0))
    base[...] = base[...] + jnp.sum(msum, axis=1, keepdims=True)
    cnt_ref[...] = base[...].astype(I32)


def _outproj(o1, o2, o3, l1, l2, l3, dn, x2, mod, w_out, norm2_w, w_router, b_router, S):
    T, D = x2.shape
    tm = 512
    nt = S // tm
    row = lambda w: pl.BlockSpec((tm, w), lambda i: (i, 0))
    res = lambda d, w: pl.BlockSpec((1, d, tm // d, w), lambda i: (i // nt, 0, i % nt, 0))
    colb = pl.BlockSpec((8, tm), lambda i: (0, i))
    return pl.pallas_call(
        functools.partial(_out_kernel, tm=tm),
        grid=(T // tm,),
        in_specs=[res(d, ATTN_W) for d in DILATIONS] + [res(d, LANES) for d in DILATIONS]
        + [row(DN_W), row(D),
                  pl.BlockSpec((1, 6, D), lambda i: (i * tm // S, 0, 0)),
                  pl.BlockSpec((D, D), lambda i: (0, 0)),
                  pl.BlockSpec((1, D), lambda i: (0, 0)),
                  pl.BlockSpec((N_EXPERTS, D), lambda i: (0, 0)),
                  pl.BlockSpec((N_EXPERTS, 1), lambda i: (0, 0))],
        out_specs=[row(D), pl.BlockSpec((tm * (D // LANES), LANES), lambda i: (i, 0)), colb, colb,
                   row(LANES), pl.BlockSpec((N_EXPERTS, LANES), lambda i: (0, 0))],
        out_shape=[jax.ShapeDtypeStruct((T, D), F32),
                   jax.ShapeDtypeStruct((T * (D // LANES), LANES), F32),
                   jax.ShapeDtypeStruct((8, T), I32),
                   jax.ShapeDtypeStruct((8, T), I32),
                   jax.ShapeDtypeStruct((T, LANES), F32),
                   jax.ShapeDtypeStruct((N_EXPERTS, LANES), I32)],
        scratch_shapes=[pltpu.VMEM((N_EXPERTS, LANES), F32),
                        pltpu.VMEM((ATTN_W // LANES, tm, LANES), F32)],
        compiler_params=_cparams(("arbitrary",)),
        name="outproj_router",
    )(o1, o2, o3, l1, l2, l3, dn, x2, mod, w_out.astype(BF16), norm2_w.reshape(1, D),
      jnp.transpose(w_router), b_router.reshape(N_EXPERTS, 1))


def _dest_kernel(ps_ref, te_ref, rk_ref, d_ref):
    te = te_ref[...]
    acc = jnp.zeros(te.shape, I32)
    for e in range(N_EXPERTS):
        acc = jnp.where(te == e, ps_ref[e], acc)
    d_ref[...] = acc + rk_ref[...]


def _dest(pstart, te, rk):
    T = te.shape[1]
    tb = 2048
    return pl.pallas_call(
        _dest_kernel,
        grid_spec=pltpu.PrefetchScalarGridSpec(
            num_scalar_prefetch=1,
            grid=(T // tb,),
            in_specs=[pl.BlockSpec((8, tb), lambda i, ps: (0, i)),
                      pl.BlockSpec((8, tb), lambda i, ps: (0, i))],
            out_specs=pl.BlockSpec((8, tb), lambda i, ps: (0, i))),
        out_shape=jax.ShapeDtypeStruct((8, T), I32),
        compiler_params=_cparams(("arbitrary",)),
        name="dest_rows",
    )(pstart, te, rk)


SC_CORES = 2
SC_SUBCORES = 16
SC_IDX_CHUNK = 128


def _invperm(dest_flat, P):
    N = dest_flat.shape[0]
    nw = SC_CORES * SC_SUBCORES
    nch = N // (nw * SC_IDX_CHUNK)
    mesh = plsc.VectorSubcoreMesh(core_axis_name="c", subcore_axis_name="s",
                                  num_cores=SC_CORES, num_subcores=SC_SUBCORES)

    @functools.partial(
        pl.kernel, mesh=mesh, out_type=jax.ShapeDtypeStruct((P,), I32),
        scratch_types=[pltpu.VMEM((nch, SC_IDX_CHUNK), I32), pltpu.VMEM((nch, SC_IDX_CHUNK), I32),
                       pltpu.SemaphoreType.DMA])
    def scatter_codes(idx_hbm, val_hbm, out_hbm, idx_v, val_v, sem):
        wid = lax.axis_index("s") * SC_CORES + lax.axis_index("c")
        pltpu.sync_copy(idx_hbm.at[wid], idx_v)
        pltpu.sync_copy(val_hbm.at[wid], val_v)

        @pl.loop(0, nch)
        def _(j):
            pltpu.async_copy(val_v.at[j], out_hbm.at[idx_v.at[j]], sem)

        @pl.loop(0, nch)
        def _(j):
            pltpu.make_async_copy(val_v.at[j], out_hbm.at[idx_v.at[j]], sem).wait()

    vals = jnp.arange(N, dtype=I32)
    return scatter_codes(dest_flat.reshape(nw, nch, SC_IDX_CHUNK), vals.reshape(nw, nch, SC_IDX_CHUNK))


def _row_copy(src, dst, sem):
    return pltpu.make_async_copy(src, dst, sem)


def _moe_kernel(be_ref, nv_ref, cc_ref, cn_ref, h2_ref, w1_ref, b1_ref, w2_ref, b2_ref, y4_ref,
                xbuf, ybuf, w1b, w2b, gsem, ssem, *, F, T, D, tme, nb):
    i = pl.program_id(0)
    s = i % 2
    ns = D // LANES

    def rows(first, n):
        return pl.ds(pl.multiple_of(first * ns, ns), n * ns)

    def gather_copy(tok, p, slot):
        return _row_copy(h2_ref.at[rows(tok, 1)], xbuf.at[slot, rows(p, 1)], gsem.at[slot])

    def token_of(code):
        return code & (T - 1) if T & (T - 1) == 0 else code % T

    def issue_rows(start_row, nv):
        ng = nv // ROW_UNROLL

        def grp(g, c):
            for j in range(ROW_UNROLL):
                start_row(g * ROW_UNROLL + j, j % 2)
            return c
        lax.fori_loop(0, ng, grp, 0)

        def one(p, c):
            start_row(p, 0)
            return c
        lax.fori_loop(ng * ROW_UNROLL, nv, one, 0)

    def start_gather(code_ref, nv, slot):
        issue_rows(lambda p, pri: gather_copy(token_of(code_ref[0, 0, p]), p, slot).start(priority=pri), nv)

    def wait_rows(make, nv):
        @pl.when(nv > 0)
        def _():
            make(nv).wait()

    def gathered(n, slot):
        return _row_copy(h2_ref.at[rows(0, n)], xbuf.at[slot, rows(0, n)], gsem.at[slot])

    def scattered(n, slot):
        return _row_copy(ybuf.at[slot, rows(0, n)], y4_ref.at[rows(0, n)], ssem.at[slot])

    @pl.when(i == 0)
    def _():
        start_gather(cc_ref, tme, 0)

    nv = nv_ref[i]

    @pl.when(jnp.logical_or(i == 0, nv_ref[jnp.maximum(i - 1, 0)] > 0))
    def _():
        gathered(tme, s).wait()

    @pl.when(i >= 2)
    def _():
        wait_rows(lambda n: scattered(n, s), nv_ref[i - 2])

    @pl.when(jnp.logical_or(i == 0, be_ref[i] != be_ref[jnp.maximum(i - 1, 0)]))
    def _():
        w1b[...] = w1_ref[0].astype(BF16)
        w2b[...] = w2_ref[0].astype(BF16)

    @pl.when(nv > 0)
    def _():
        x = _load_slabs(xbuf, tme, D, lead=(s,)).astype(BF16)
        for p in range(tme):
            gather_copy(token_of(cn_ref[0, 0, p]), p, 1 - s).start(priority=p % 2)
        hgu = jnp.dot(x, w1b[...], preferred_element_type=F32) + b1_ref[0]
        gate = jnp.minimum(hgu[:, :F], SWIGLU_LIMIT)
        up = jnp.clip(hgu[:, F:], -SWIGLU_LIMIT, SWIGLU_LIMIT)
        act = gate * _sigmoid(SWIGLU_ALPHA * gate) * (up + 1.0)
        y = jnp.dot(act.astype(BF16), w2b[...], preferred_element_type=F32) + b2_ref[0]
        _store_slabs(ybuf, y, lead=(s,))

        for slot in range(2):
            @pl.when(s == slot)
            def _(slot=slot):
                issue_rows(lambda p, pri: _row_copy(ybuf.at[slot, rows(p, 1)],
                                                    y4_ref.at[rows(cc_ref[0, 0, p], 1)],
                                                    ssem.at[slot]).start(priority=pri), nv)

    @pl.when(i == nb - 1)
    def _():
        @pl.when(nv > 0)
        def _():
            gathered(tme, 1 - s).wait()
        wait_rows(lambda n: scattered(n, s), nv)
        if nb > 1:
            wait_rows(lambda n: scattered(n, 1 - s), nv_ref[i - 1])


def _experts(blk_exp, blk_valid, codes, h2s, w1, b1, w2, b2, tme):
    E, D, F2 = w1.shape
    F = F2 // 2
    ns = D // LANES
    T = h2s.shape[0] // ns
    nb = blk_exp.shape[0]
    codes3 = codes.reshape(nb, 1, tme)
    wspec = lambda shape: pl.BlockSpec(shape, lambda i, be, nv: (be[i], 0, 0))
    cspec = lambda off: pl.BlockSpec((1, 1, tme), lambda i, be, nv: (jnp.minimum(i + off, nb - 1), 0, 0),
                                     memory_space=pltpu.SMEM)
    return pl.pallas_call(
        functools.partial(_moe_kernel, F=F, T=T, D=D, tme=tme, nb=nb),
        grid_spec=pltpu.PrefetchScalarGridSpec(
            num_scalar_prefetch=2,
            grid=(nb,),
            in_specs=[cspec(0), cspec(1),
                      pl.BlockSpec(memory_space=pl.ANY),
                      wspec((1, D, F2)), wspec((1, 1, F2)), wspec((1, F, D)), wspec((1, 1, D))],
            out_specs=pl.BlockSpec(memory_space=pl.ANY),
            scratch_shapes=[pltpu.VMEM((2, tme * ns, LANES), F32), pltpu.VMEM((2, tme * ns, LANES), F32),
                            pltpu.VMEM((D, F2), BF16), pltpu.VMEM((F, D), BF16),
                            pltpu.SemaphoreType.DMA((2,)), pltpu.SemaphoreType.DMA((2,))]),
        out_shape=jax.ShapeDtypeStruct((TOP_K * T * ns, LANES), F32),
        compiler_params=_cparams(("arbitrary",)),
        name="experts",
    )(blk_exp, blk_valid, codes3, codes3, h2s, w1, b1.reshape(E, 1, F2), w2, b2.reshape(E, 1, D))


def _comb_kernel(g_ref, x1_ref, mod_ref, fw_ref, y0_ref, y1_ref, y2_ref, y3_ref, o_ref):
    g = g_ref[...]
    n, d = x1_ref.shape
    y = g[:, 0:1] * _load_slabs(y0_ref, n, d)
    for k, y_ref in ((1, y1_ref), (2, y2_ref), (3, y3_ref)):
        y = y + g[:, k:k + 1] * _load_slabs(y_ref, n, d)
    x2 = x1_ref[...] + mod_ref[0, 5:6, :] * y
    ms = jnp.mean(x2 * x2, axis=-1, keepdims=True)
    o_ref[...] = x2 * lax.rsqrt(ms + EPS) * fw_ref[...]


def _combine(gtc, x1, mod, final_w, y4, S):
    T, D = x1.shape
    tmc = 512
    nt = T // tmc
    yspec = lambda k: pl.BlockSpec((tmc * (D // LANES), LANES), lambda i: (k * nt + i, 0))
    return pl.pallas_call(
        _comb_kernel,
        grid=(nt,),
        in_specs=[pl.BlockSpec((tmc, LANES), lambda i: (i, 0)),
                  pl.BlockSpec((tmc, D), lambda i: (i, 0)),
                  pl.BlockSpec((1, 6, D), lambda i: (i * tmc // S, 0, 0)),
                  pl.BlockSpec((1, D), lambda i: (0, 0)),
                  yspec(0), yspec(1), yspec(2), yspec(3)],
        out_specs=pl.BlockSpec((tmc, D), lambda i: (i, 0)),
        out_shape=jax.ShapeDtypeStruct((T, D), F32),
        compiler_params=_cparams(("arbitrary",)),
        name="combine",
    )(gtc, x1, mod, final_w.reshape(1, D), y4, y4, y4, y4)


def _layer(x2, mod, B, S, norm1_w, w_in, conv_w, A_log, dt_bias, dn_norm_w, w_out,
           norm2_w, w_router, b_router, w1, b1, w2, b2):
    T, D = x2.shape
    qkv, qkv4, qkv16, dqkv, dz, gbc = _inproj(x2, mod, norm1_w, w_in, S)
    views = (qkv.reshape(B, 1, S, 3 * ATTN_W), qkv4, qkv16)
    branches = [_attn_branch(v, d) for v, d in zip(views, DILATIONS)]
    dn = _gdn(dqkv, dz, gbc, conv_w, A_log, dt_bias, dn_norm_w, B, S)
    (o1, l1), (o2, l2), (o3, l3) = branches
    x1, h2, te, rk, gtc, cnt = _outproj(o1, o2, o3, l1, l2, l3, dn, x2, mod, w_out, norm2_w,
                                        w_router, b_router, S)
    tme = MOE_ROWS
    P = T * TOP_K + N_EXPERTS * tme
    counts = cnt[:, 0]
    padded = (counts + tme - 1) // tme * tme
    pend = jnp.cumsum(padded)
    pstart = (pend - padded).astype(I32)
    blk_start = jnp.arange(P // tme, dtype=I32) * tme
    blk_exp = jnp.minimum(jnp.sum((pend[None, :] <= blk_start[:, None]).astype(I32), axis=1),
                          N_EXPERTS - 1).astype(I32)
    blk_valid = jnp.clip(pstart[blk_exp] + counts[blk_exp] - blk_start, 0, tme).astype(I32)
    dest = _dest(pstart, te, rk)
    codes = _invperm(dest[:TOP_K].reshape(TOP_K * T), P)
    y4 = _experts(blk_exp, blk_valid, codes, h2, w1, b1, w2, b2, tme)
    return x1, gtc, y4


def kernel(x, c, w_ada, b_ada, norm1_w, w_in, conv_w, A_log, dt_bias, dn_norm_w, w_out, norm2_w,
           w_router, b_router, w1, b1, w2, b2, final_norm_w):
    B, S, D = x.shape
    depth = w_ada.shape[0]
    assert S % (ATTN_BLK * DILATIONS[-1]) == 0 and depth == 1
    x2 = x.reshape(B * S, D)
    mod = _ada(c, w_ada[0], b_ada[0])
    x1, gtc, y4 = _layer(x2, mod, B, S, norm1_w[0], w_in[0], conv_w[0], A_log[0], dt_bias[0],
                         dn_norm_w[0], w_out[0], norm2_w[0], w_router[0], b_router[0],
                         w1[0], b1[0], w2[0], b2[0])
    out = _combine(gtc, x1, mod, final_norm_w, y4, S)
    return out.reshape(B, S, D)
```

```python
import functools

import jax
import jax.numpy as jnp
from jax import lax
from jax.experimental import pallas as pl
from jax.experimental.pallas import tpu as pltpu
from jax.experimental.pallas import tpu_sc as plsc

F32 = jnp.float32
BF16 = jnp.bfloat16
I32 = jnp.int32
HI = lax.Precision.HIGHEST

LANES = 128
ATTN_HEADS = 8
ATTN_HD = 64
ATTN_W = ATTN_HEADS * ATTN_HD
ATTN_BLK = 128
DILATIONS = (1, 4, 16)
DN_HEADS = 4
DN_D = 128
DN_W = DN_HEADS * DN_D
DN_CONV = 3 * DN_W
CONV_K = 4
DN_CHUNK = 64
N_EXPERTS = 32
TOP_K = 4
MOE_ROWS = 512
ROW_UNROLL = 8
SWIGLU_LIMIT = 7.0
SWIGLU_ALPHA = 1.702
EPS = 1e-6
NEG = -1e30
MAIN_COLS = 3 * ATTN_W + DN_CONV + DN_W

VMEM_LIMIT = 56 * 1024 * 1024


def _cparams(sem):
    return pltpu.CompilerParams(dimension_semantics=sem, vmem_limit_bytes=VMEM_LIMIT)


def _nt(a, b, **kw):
    return lax.dot_general(a, b, (((1,), (1,)), ((), ())), preferred_element_type=F32, **kw)


def _sigmoid(x):
    return 1.0 / (1.0 + jnp.exp(-x))


def _store_slabs(ref, val, lead=()):
    n, d = val.shape
    ns = d // LANES
    for c in range(ns):
        ref[lead + (pl.ds(c, n, stride=ns), slice(None))] = val[:, c * LANES:(c + 1) * LANES]


def _load_slabs(ref, n, d, lead=()):
    ns = d // LANES
    return jnp.concatenate([ref[lead + (pl.ds(c, n, stride=ns), slice(None))] for c in range(ns)], axis=1)


def _ada_kernel(c_ref, w_ref, b_ref, o_ref):
    c = c_ref[...]
    cond = c * _sigmoid(c)
    o_ref[...] = jnp.dot(cond, w_ref[...], preferred_element_type=F32, precision=HI) + b_ref[...]


def _ada(c, w_ada, b_ada):
    B, D = c.shape
    N = w_ada.shape[1]
    cp = jnp.zeros((8, D), F32).at[:B].set(c)
    tn = 1024
    out = pl.pallas_call(
        _ada_kernel,
        grid=(N // tn,),
        in_specs=[pl.BlockSpec((8, D), lambda j: (0, 0)),
                  pl.BlockSpec((D, tn), lambda j: (0, j)),
                  pl.BlockSpec((1, tn), lambda j: (0, j))],
        out_specs=pl.BlockSpec((8, tn), lambda j: (0, j)),
        out_shape=jax.ShapeDtypeStruct((8, N), F32),
        compiler_params=_cparams(("arbitrary",)),
        name="ada",
    )(cp, w_ada, b_ada.reshape(1, N))
    return out[:B].reshape(B, 6, D)


def _inproj_kernel(x_ref, mod_ref, nw_ref, wm_ref, ws_ref, qkv_ref, qkv4_ref, qkv16_ref, dqkv_ref, dz_ref,
                   gbc_ref, scr):
    x = x_ref[...]
    tm = x.shape[0]
    shift = mod_ref[0, 0:1, :]
    scale = mod_ref[0, 1:2, :]
    ms = jnp.mean(x * x, axis=-1, keepdims=True)
    h = x * lax.rsqrt(ms + EPS) * nw_ref[...]
    hb = (h * (1.0 + scale) + shift).astype(BF16)
    nl = ATTN_W // LANES
    for j in range(3):
        cs = slice(j * ATTN_W, (j + 1) * ATTN_W)
        r = jnp.dot(hb, wm_ref[:, cs], preferred_element_type=F32)
        if j == 0:
            r = r * (ATTN_HD ** -0.5)
        qkv_ref[:, cs] = r.astype(BF16)
        for c in range(nl):
            scr[c] = r[:, c * LANES:(c + 1) * LANES]
        for d, ref in ((DILATIONS[1], qkv4_ref), (DILATIONS[2], qkv16_ref)):
            for res in range(d):
                part = jnp.concatenate([scr[c, pl.ds(res, tm // d, stride=d), :] for c in range(nl)], axis=1)
                ref[0, res, :, cs] = part.astype(BF16)
    for j in range(3):
        c0 = 3 * ATTN_W + j * DN_W
        r = jnp.dot(hb, wm_ref[:, c0:c0 + DN_W], preferred_element_type=F32)
        dqkv_ref[:, j * DN_W:(j + 1) * DN_W] = r.astype(BF16)
    c0 = 3 * ATTN_W + DN_CONV
    dz_ref[...] = jnp.dot(hb, wm_ref[:, c0:c0 + DN_W], preferred_element_type=F32).astype(BF16)
    gbc_ref[...] = jnp.dot(hb, ws_ref[...], preferred_element_type=F32)


def _inproj(x2, mod, norm_w, w_in, S):
    T, D = x2.shape
    tm = 512
    B, nt = T // S, S // tm
    wm = w_in[:, :MAIN_COLS].astype(BF16)
    ws = jnp.zeros((D, LANES), F32).at[:, :2 * DN_HEADS].set(w_in[:, MAIN_COLS:]).astype(BF16)
    return pl.pallas_call(
        _inproj_kernel,
        grid=(T // tm,),
        in_specs=[pl.BlockSpec((tm, D), lambda i: (i, 0)),
                  pl.BlockSpec((1, 6, D), lambda i: (i * tm // S, 0, 0)),
                  pl.BlockSpec((1, D), lambda i: (0, 0)),
                  pl.BlockSpec((D, MAIN_COLS), lambda i: (0, 0)),
                  pl.BlockSpec((D, LANES), lambda i: (0, 0))],
        out_specs=[pl.BlockSpec((tm, 3 * ATTN_W), lambda i: (i, 0))]
        + [pl.BlockSpec((1, d, tm // d, 3 * ATTN_W), lambda i: (i // nt, 0, i % nt, 0)) for d in DILATIONS[1:]]
        + [pl.BlockSpec((tm, DN_CONV), lambda i: (i, 0)),
           pl.BlockSpec((tm, DN_W), lambda i: (i, 0)),
           pl.BlockSpec((tm, LANES), lambda i: (i, 0))],
        out_shape=[jax.ShapeDtypeStruct((T, 3 * ATTN_W), BF16)]
        + [jax.ShapeDtypeStruct((B, d, S // d, 3 * ATTN_W), BF16) for d in DILATIONS[1:]]
        + [jax.ShapeDtypeStruct((T, DN_CONV), BF16),
           jax.ShapeDtypeStruct((T, DN_W), BF16),
           jax.ShapeDtypeStruct((T, LANES), F32)],
        scratch_shapes=[pltpu.VMEM((ATTN_W // LANES, tm, LANES), F32)],
        compiler_params=_cparams(("arbitrary",)),
        name="inproj",
    )(x2, mod, norm_w.reshape(1, D), wm, ws)


def _attn_kernel(q_ref, kc_ref, kp_ref, vc_ref, vp_ref, o_ref, lse_ref, kf, vf, *, qb):
    n = pl.program_id(2)
    kf[0:ATTN_BLK, :] = kp_ref[0, 0]
    kf[ATTN_BLK:, :] = kc_ref[0, 0]
    vf[0:ATTN_BLK, :] = vp_ref[0, 0]
    vf[ATTN_BLK:, :] = vc_ref[0, 0]
    row = lax.broadcasted_iota(I32, (ATTN_BLK, 2 * ATTN_BLK), 0)
    col = lax.broadcasted_iota(I32, (ATTN_BLK, 2 * ATTN_BLK), 1)
    band = jnp.logical_or(jnp.logical_and(col < ATTN_BLK, col >= row),
                          jnp.logical_and(col >= ATTN_BLK, col - ATTN_BLK <= row))
    lane = lax.broadcasted_iota(I32, (ATTN_BLK, LANES), 1)
    lo = lane < ATTN_HD

    def sub(j, carry):
        r0 = pl.multiple_of(j * ATTN_BLK, ATTN_BLK)
        first_col = jnp.where(jnp.logical_and(n == 0, j == 0), ATTN_BLK, 0)
        mask = jnp.logical_and(band, col >= first_col)
        npair = ATTN_W // LANES
        cols = [slice(hp * LANES, (hp + 1) * LANES) for hp in range(npair)]
        heads = [(hp, half) for hp in range(npair) for half in range(2)]
        scores = []
        for hp, half in heads:
            q2 = q_ref[0, 0, pl.ds(r0, ATTN_BLK), cols[hp]]
            qm = jnp.where(lo if half == 0 else jnp.logical_not(lo), q2, jnp.zeros_like(q2))
            scores.append(jnp.where(mask, _nt(qm, kf[pl.ds(r0, 2 * ATTN_BLK), cols[hp]]), NEG))
        maxes = [jnp.max(s, axis=-1, keepdims=True) for s in scores]
        probs = [jnp.exp(s - m) for s, m in zip(scores, maxes)]
        dens = [jnp.sum(p, axis=-1, keepdims=True) for p in probs]
        accs = [jnp.dot(p.astype(BF16), vf[pl.ds(r0, 2 * ATTN_BLK), cols[hp]], preferred_element_type=F32)
                for p, (hp, _) in zip(probs, heads)]
        outs = [a / d for a, d in zip(accs, dens)]
        lse_tile = jnp.zeros((ATTN_BLK, LANES), F32)
        for h, (m, d) in enumerate(zip(maxes, dens)):
            lse_tile = jnp.where(lane == h, m + jnp.log(d), lse_tile)
        for hp in range(npair):
            o_ref[0, 0, pl.ds(r0, ATTN_BLK), cols[hp]] = jnp.where(lo, outs[2 * hp], outs[2 * hp + 1]).astype(BF16)
        lse_ref[0, 0, pl.ds(r0, ATTN_BLK), :] = lse_tile
        return carry

    lax.fori_loop(0, qb // ATTN_BLK, sub, 0)


def _attn_branch(qkv, d):
    B, _, L, _ = qkv.shape
    qb = min(512, L)
    nsub = qb // ATTN_BLK
    cur = lambda c: pl.BlockSpec((1, 1, qb, ATTN_W), lambda b, r, n: (b, r, n, c))
    prev = lambda c: pl.BlockSpec((1, 1, ATTN_BLK, ATTN_W),
                                  lambda b, r, n: (b, r, jnp.maximum(n * nsub - 1, 0), c))
    return pl.pallas_call(
        functools.partial(_attn_kernel, qb=qb),
        grid=(B, d, L // qb),
        in_specs=[cur(0), cur(1), prev(1), cur(2), prev(2)],
        out_specs=[pl.BlockSpec((1, 1, qb, ATTN_W), lambda b, r, n: (b, r, n, 0)),
                   pl.BlockSpec((1, 1, qb, LANES), lambda b, r, n: (b, r, n, 0))],
        out_shape=[jax.ShapeDtypeStruct((B, d, L, ATTN_W), BF16),
                   jax.ShapeDtypeStruct((B, d, L, LANES), F32)],
        scratch_shapes=[pltpu.VMEM((qb + ATTN_BLK, ATTN_W), BF16),
                        pltpu.VMEM((qb + ATTN_BLK, ATTN_W), BF16)],
        compiler_params=_cparams(("arbitrary", "arbitrary", "arbitrary")),
        name=f"attn_d{d}",
    )(qkv, qkv, qkv, qkv, qkv)


def _gdn_kernel(x_ref, z_ref, g_ref, cw_ref, prm_ref, nw_ref, o_ref, xext, yc, s0, s1, *, rb):
    i = pl.program_id(1)

    @pl.when(i == 0)
    def _():
        xext[0:8, :] = jnp.zeros((8, DN_CONV), F32)
        s0[...] = jnp.zeros_like(s0)
        s1[...] = jnp.zeros_like(s1)

    @pl.when(i > 0)
    def _():
        xext[0:8, :] = xext[rb:rb + 8, :]

    xext[8:, :] = x_ref[0].astype(F32)
    y = cw_ref[CONV_K - 1:CONV_K, :] * xext[8:8 + rb, :]
    for j in range(CONV_K - 1):
        off = 8 - (CONV_K - 1) + j
        y = y + cw_ref[j:j + 1, :] * xext[off:off + rb, :]
    yc[...] = y * _sigmoid(y)

    C = DN_CHUNK
    H = DN_HEADS
    CW = H * C
    dot = functools.partial(jnp.dot, preferred_element_type=F32)

    def iota(shape, d):
        return lax.broadcasted_iota(I32, shape, d)

    ltri_b = jnp.where(iota((C, C), 0) >= iota((C, C), 1), 1.0, 0.0).astype(BF16)
    lane = iota((C, LANES), 1)
    blane = lane < H
    glane = jnp.logical_and(lane >= H, lane < 2 * H)
    e512 = jnp.where(jnp.logical_or(iota((LANES, DN_W), 1) // DN_D == iota((LANES, DN_W), 0),
                                    iota((LANES, DN_W), 1) // DN_D == iota((LANES, DN_W), 0) - H),
                     1.0, 0.0).astype(BF16)
    e256 = jnp.where(iota((LANES, CW), 1) // C == iota((LANES, CW), 0) - H, 1.0, 0.0).astype(BF16)
    row4 = iota((C, CW), 0)
    col4 = iota((C, CW), 1) % C
    eye4 = jnp.where(row4 == col4, 1.0, 0.0).astype(F32)
    blk = [iota((C, CW), 1) // C == h for h in range(H)]
    bd_cc = iota((CW, CW), 0) // C == iota((CW, CW), 1) // C
    bd_pair = iota((CW, CW), 0) // DN_D == iota((CW, CW), 1) // DN_D
    rt_mask = iota((CW, DN_W), 0) // C == iota((CW, DN_W), 1) // DN_D
    neg_a = -jnp.exp(prm_ref[0:1, :])
    dtb = prm_ref[1:2, :]
    nw = nw_ref[...]

    def hilo(x):
        hi = x.astype(BF16)
        return hi, (x - hi.astype(F32)).astype(BF16)

    def heads(a, w):
        return [a[:, h * w:(h + 1) * w] for h in range(H)]

    def l2n(a, mult):
        return jnp.concatenate(
            [p * (lax.rsqrt(jnp.sum(p * p, axis=-1, keepdims=True) + EPS) * mult) for p in heads(a, DN_D)],
            axis=1)

    def stack4(a):
        return jnp.concatenate([a, a, a, a], axis=0)

    zb = jnp.zeros((), BF16)
    nchunk = rb // C
    chunks = []
    for c in range(nchunk):
        rs = slice(c * C, (c + 1) * C)
        G = g_ref[0, rs, :]
        xg = G + dtb
        gv = jnp.where(glane, neg_a * (jnp.maximum(xg, 0.0) + jnp.log1p(jnp.exp(-jnp.abs(xg)))), 0.0)
        be = jnp.where(blane, _sigmoid(G), 0.0)
        g_hi, g_lo = hilo(gv)
        gcum = dot(ltri_b, g_hi) + dot(ltri_b, g_lo)
        gtot = gcum[C - 1:C, :]
        eg = jnp.where(glane, jnp.exp(gcum), 0.0)
        ek = jnp.where(glane, jnp.exp(gtot - gcum), 0.0)
        ex = dot(jnp.concatenate([be, eg, ek], axis=0).astype(BF16), e512)
        bexp, egexp, ekexp = ex[0:C], ex[C:2 * C], ex[2 * C:3 * C]
        gexp = dot(g_hi, e256) + dot(g_lo, e256)
        d_hi, d_lo = hilo(jnp.where(row4 > col4, gexp, 0.0))
        diff = dot(ltri_b, d_hi) + dot(ltri_b, d_lo)
        decay = jnp.exp(jnp.where(row4 >= col4, diff, NEG))

        q4 = l2n(yc[rs, 0:DN_W], DN_D ** -0.5)
        k4 = l2n(yc[rs, DN_W:2 * DN_W], 1.0)
        v4 = yc[rs, 2 * DN_W:3 * DN_W]
        kb4 = k4 * bexp
        vb4 = v4 * bexp
        rt = jnp.where(rt_mask, stack4(k4.astype(BF16)), jnp.zeros((), BF16))
        ai = _nt(jnp.concatenate([kb4, q4], axis=0).astype(BF16), rt)
        a4 = jnp.where(row4 > col4, ai[0:C] * decay, 0.0)
        pb = a4.astype(BF16)
        chunks.append(dict(
            rs=rs, q4=q4, k4=k4, kb4=kb4, vb4=vb4, egexp=egexp, ekexp=ekexp,
            intra=ai[C:2 * C] * decay, t4=eye4 - a4, pb=pb,
            bd=jnp.where(bd_cc, stack4(pb), zb)))

    for _ in range(5):
        for ch in chunks:
            ch["pb"] = dot(ch["pb"], ch["bd"]).astype(BF16)
        for ch in chunks:
            ch["bd"] = jnp.where(bd_cc, stack4(ch["pb"]), zb)
        for ch in chunks:
            ch["t4"] = ch["t4"] + dot(ch["t4"].astype(BF16), ch["bd"])

    for ch in chunks:
        q4, k4, kb4, vb4, egexp, ekexp = (ch[n] for n in ("q4", "k4", "kb4", "vb4", "egexp", "ekexp"))
        t4b = ch["t4"].astype(BF16)
        lstk = jnp.concatenate([jnp.where(blk[h], t4b, zb) for h in range(H)], axis=0)
        kbg4 = kb4 * egexp
        rstk = jnp.concatenate(
            [jnp.concatenate([vb, kbg], axis=1) for vb, kbg in zip(heads(vb4, DN_D), heads(kbg4, DN_D))],
            axis=0).astype(BF16)
        uw = dot(lstk, rstk)
        u4 = jnp.concatenate([uw[h * C:(h + 1) * C, 0:DN_D] for h in range(H)], axis=1)
        w4 = jnp.concatenate([uw[h * C:(h + 1) * C, DN_D:2 * DN_D] for h in range(H)], axis=1)
        ib = ch["intra"].astype(BF16)
        ch.update(
            u4=u4, wq=jnp.concatenate([w4, q4 * egexp], axis=0).astype(BF16),
            kd4=(k4 * ekexp).astype(BF16), gl4=egexp[C - 1:C, :],
            lint=jnp.concatenate([jnp.where(blk[h], ib, zb) for h in range(H)], axis=0))

    for ch in chunks:
        rs, u4, wq, kd4, gl4, lint = (ch[n] for n in ("rs", "u4", "wq", "kd4", "gl4", "lint"))
        ra = dot(wq[:, 0:CW], s0[...].astype(BF16))
        rc = dot(wq[:, CW:2 * CW], s1[...].astype(BF16))
        vn = u4 - jnp.concatenate([ra[0:C], rc[0:C]], axis=1)
        vnb = vn.astype(BF16)
        oi = dot(lint, jnp.concatenate(heads(vnb, DN_D), axis=0))
        o = (jnp.concatenate([ra[C:2 * C], rc[C:2 * C]], axis=1)
             + jnp.concatenate([oi[h * C:(h + 1) * C] for h in range(H)], axis=1))
        tn = (((0,), (0,)), ((), ()))
        s0[...] = s0[...] * gl4[:, 0:CW] + jnp.where(
            bd_pair, lax.dot_general(kd4[:, 0:CW], vnb[:, 0:CW], tn, preferred_element_type=F32), 0.0)
        s1[...] = s1[...] * gl4[:, CW:2 * CW] + jnp.where(
            bd_pair, lax.dot_general(kd4[:, CW:2 * CW], vnb[:, CW:2 * CW], tn, preferred_element_type=F32), 0.0)

        z = z_ref[0, rs, :].astype(F32)
        on = jnp.concatenate(
            [p * lax.rsqrt(jnp.mean(p * p, axis=-1, keepdims=True) + EPS) * nw for p in heads(o, DN_D)], axis=1)
        o_ref[0, rs, :] = (on * (z * _sigmoid(z))).astype(BF16)


def _gdn(dqkv, dz, gbc, conv_w, A_log, dt_bias, dn_norm_w, B, S):
    rb = 512
    prm = jnp.zeros((2, LANES), F32)
    prm = prm.at[0, DN_HEADS:2 * DN_HEADS].set(A_log.astype(F32))
    prm = prm.at[1, DN_HEADS:2 * DN_HEADS].set(dt_bias.astype(F32))
    out = pl.pallas_call(
        functools.partial(_gdn_kernel, rb=rb),
        grid=(B, S // rb),
        in_specs=[pl.BlockSpec((1, rb, DN_CONV), lambda b, i: (b, i, 0)),
                  pl.BlockSpec((1, rb, DN_W), lambda b, i: (b, i, 0)),
                  pl.BlockSpec((1, rb, LANES), lambda b, i: (b, i, 0)),
                  pl.BlockSpec((CONV_K, DN_CONV), lambda b, i: (0, 0)),
                  pl.BlockSpec((2, LANES), lambda b, i: (0, 0)),
                  pl.BlockSpec((1, DN_D), lambda b, i: (0, 0))],
        out_specs=pl.BlockSpec((1, rb, DN_W), lambda b, i: (b, i, 0)),
        out_shape=jax.ShapeDtypeStruct((B, S, DN_W), BF16),
        scratch_shapes=[pltpu.VMEM((rb + 8, DN_CONV), F32),
                        pltpu.VMEM((rb, DN_CONV), F32),
                        pltpu.VMEM((2 * DN_D, 2 * DN_D), F32),
                        pltpu.VMEM((2 * DN_D, 2 * DN_D), F32)],
        compiler_params=_cparams(("arbitrary", "arbitrary")),
        name="gdn",
    )(dqkv.reshape(B, S, DN_CONV), dz.reshape(B, S, DN_W), gbc.reshape(B, S, LANES),
      conv_w, prm, dn_norm_w.reshape(1, DN_D))
    return out.reshape(B * S, DN_W)


def _out_kernel(o1_ref, o2_ref, o3_ref, l1_ref, l2_ref, l3_ref, dn_ref, x_ref, mod_ref, wo_ref,
                n2_ref, wr_ref, br_ref,
                x1_ref, h2_ref, te_ref, rk_ref, gtc_ref, cnt_ref, base, scr, *, tm):
    i = pl.program_id(0)

    @pl.when(i == 0)
    def _():
        base[...] = jnp.zeros_like(base)

    def natural(ref, d):
        if d == 1:
            return ref[0, 0].astype(F32)
        nl = ref.shape[-1] // LANES
        for res in range(d):
            blk = ref[0, res].astype(F32)
            for c in range(nl):
                scr[c, pl.ds(res, tm // d, stride=d), :] = blk[:, c * LANES:(c + 1) * LANES]
        return jnp.concatenate([scr[c] for c in range(nl)], axis=1)

    l1, l2, l3 = (natural(r, d) for r, d in zip((l1_ref, l2_ref, l3_ref), DILATIONS))
    mx = jnp.maximum(jnp.maximum(l1, l2), l3)
    e1, e2, e3 = jnp.exp(l1 - mx), jnp.exp(l2 - mx), jnp.exp(l3 - mx)
    zs = e1 + e2 + e3
    er = lax.broadcasted_iota(I32, (LANES, ATTN_W), 0)
    ec = lax.broadcasted_iota(I32, (LANES, ATTN_W), 1)
    expand = jnp.where(ec // ATTN_HD == er, 1.0, 0.0).astype(BF16)
    attn = jnp.zeros((tm, ATTN_W), F32)
    for e, o_ref, d in zip((e1, e2, e3), (o1_ref, o2_ref, o3_ref), DILATIONS):
        wgt = jnp.dot((e / zs).astype(BF16), expand, preferred_element_type=F32)
        attn = attn + wgt * natural(o_ref, d)
    mix = (jnp.dot(attn.astype(BF16), wo_ref[0:ATTN_W, :], preferred_element_type=F32)
           + jnp.dot(dn_ref[...], wo_ref[ATTN_W:, :], preferred_element_type=F32))
    x1 = x_ref[...] + mod_ref[0, 2:3, :] * mix
    x1_ref[...] = x1
    ms = jnp.mean(x1 * x1, axis=-1, keepdims=True)
    h2 = x1 * lax.rsqrt(ms + EPS) * n2_ref[...]
    h2 = h2 * (1.0 + mod_ref[0, 4:5, :]) + mod_ref[0, 3:4, :]
    _store_slabs(h2_ref, h2)

    lg = _nt(wr_ref[...], h2, precision=HI) + br_ref[...]
    eidx = lax.broadcasted_iota(I32, (N_EXPERTS, tm), 0)
    vals, idxs, sels = [], [], []
    for _ in range(TOP_K):
        m = jnp.max(lg, axis=0, keepdims=True)
        idx = jnp.min(jnp.where(lg == m, eidx, N_EXPERTS), axis=0, keepdims=True)
        sel = eidx == idx
        vals.append(m)
        idxs.append(idx)
        sels.append(sel)
        lg = jnp.where(sel, -jnp.inf, lg)
    ex = [jnp.exp(v - vals[0]) for v in vals]
    den = ex[0] + ex[1] + ex[2] + ex[3]
    gates = [e / den for e in ex]

    msum = jnp.zeros((N_EXPERTS, tm), F32)
    for sel in sels:
        msum = msum + jnp.where(sel, 1.0, 0.0)
    tr = lax.broadcasted_iota(I32, (tm, tm), 0)
    tc = lax.broadcasted_iota(I32, (tm, tm), 1)
    upper = jnp.where(tr <= tc, 1.0, 0.0).astype(BF16)
    incl = jnp.dot(msum.astype(BF16), upper, preferred_element_type=F32)
    pos = base[:, 0:1] + (incl - msum)
    sub8 = lax.broadcasted_iota(I32, (8, tm), 0)
    te = jnp.zeros((8, tm), I32)
    rk = jnp.zeros((8, tm), I32)
    gt = jnp.zeros((8, tm), F32)
    for k in range(TOP_K):
        rank_k = jnp.sum(jnp.where(sels[k], pos, 0.0), axis=0, keepdims=True).astype(I32)
        te = jnp.where(sub8 == k, idxs[k], te)
        rk = jnp.where(sub8 == k, rank_k, rk)
        gt = jnp.where(sub8 == k, gates[k], gt)
    te_ref[...] = te
    rk_ref[...] = rk
    gtc_ref[...] = jnp.transpose(jnp.concatenate([gt, jnp.zeros((LANES - 8, tm), F32)], axis=0))
    base[...] = base[...] + jnp.sum(msum, axis=1, keepdims=True)
    cnt_ref[...] = base[...].astype(I32)


def _outproj(o1, o2, o3, l1, l2, l3, dn, x2, mod, w_out, norm2_w, w_router, b_router, S):
    T, D = x2.shape
    tm = 512
    nt = S // tm
    row = lambda w: pl.BlockSpec((tm, w), lambda i: (i, 0))
    res = lambda d, w: pl.BlockSpec((1, d, tm // d, w), lambda i: (i // nt, 0, i % nt, 0))
    colb = pl.BlockSpec((8, tm), lambda i: (0, i))
    return pl.pallas_call(
        functools.partial(_out_kernel, tm=tm),
        grid=(T // tm,),
        in_specs=[res(d, ATTN_W) for d in DILATIONS] + [res(d, LANES) for d in DILATIONS]
        + [row(DN_W), row(D),
                  pl.BlockSpec((1, 6, D), lambda i: (i * tm // S, 0, 0)),
                  pl.BlockSpec((D, D), lambda i: (0, 0)),
                  pl.BlockSpec((1, D), lambda i: (0, 0)),
                  pl.BlockSpec((N_EXPERTS, D), lambda i: (0, 0)),
                  pl.BlockSpec((N_EXPERTS, 1), lambda i: (0, 0))],
        out_specs=[row(D), pl.BlockSpec((tm * (D // LANES), LANES), lambda i: (i, 0)), colb, colb,
                   row(LANES), pl.BlockSpec((N_EXPERTS, LANES), lambda i: (0, 0))],
        out_shape=[jax.ShapeDtypeStruct((T, D), F32),
                   jax.ShapeDtypeStruct((T * (D // LANES), LANES), F32),
                   jax.ShapeDtypeStruct((8, T), I32),
                   jax.ShapeDtypeStruct((8, T), I32),
                   jax.ShapeDtypeStruct((T, LANES), F32),
                   jax.ShapeDtypeStruct((N_EXPERTS, LANES), I32)],
        scratch_shapes=[pltpu.VMEM((N_EXPERTS, LANES), F32),
                        pltpu.VMEM((ATTN_W // LANES, tm, LANES), F32)],
        compiler_params=_cparams(("arbitrary",)),
        name="outproj_router",
    )(o1, o2, o3, l1, l2, l3, dn, x2, mod, w_out.astype(BF16), norm2_w.reshape(1, D),
      jnp.transpose(w_router), b_router.reshape(N_EXPERTS, 1))


def _dest_kernel(ps_ref, te_ref, rk_ref, d_ref):
    te = te_ref[...]
    acc = jnp.zeros(te.shape, I32)
    for e in range(N_EXPERTS):
        acc = jnp.where(te == e, ps_ref[e], acc)
    d_ref[...] = acc + rk_ref[...]


def _dest(pstart, te, rk):
    T = te.shape[1]
    tb = 2048
    return pl.pallas_call(
        _dest_kernel,
        grid_spec=pltpu.PrefetchScalarGridSpec(
            num_scalar_prefetch=1,
            grid=(T // tb,),
            in_specs=[pl.BlockSpec((8, tb), lambda i, ps: (0, i)),
                      pl.BlockSpec((8, tb), lambda i, ps: (0, i))],
            out_specs=pl.BlockSpec((8, tb), lambda i, ps: (0, i))),
        out_shape=jax.ShapeDtypeStruct((8, T), I32),
        compiler_params=_cparams(("arbitrary",)),
        name="dest_rows",
    )(pstart, te, rk)


SC_CORES = 2
SC_SUBCORES = 16
SC_IDX_CHUNK = 128


def _invperm(dest_flat, P):
    N = dest_flat.shape[0]
    nw = SC_CORES * SC_SUBCORES
    nch = N // (nw * SC_IDX_CHUNK)
    mesh = plsc.VectorSubcoreMesh(core_axis_name="c", subcore_axis_name="s",
                                  num_cores=SC_CORES, num_subcores=SC_SUBCORES)

    @functools.partial(
        pl.kernel, mesh=mesh, out_type=jax.ShapeDtypeStruct((P,), I32),
        scratch_types=[pltpu.VMEM((nch, SC_IDX_CHUNK), I32), pltpu.VMEM((nch, SC_IDX_CHUNK), I32),
                       pltpu.SemaphoreType.DMA])
    def scatter_codes(idx_hbm, val_hbm, out_hbm, idx_v, val_v, sem):
        wid = lax.axis_index("s") * SC_CORES + lax.axis_index("c")
        pltpu.sync_copy(idx_hbm.at[wid], idx_v)
        pltpu.sync_copy(val_hbm.at[wid], val_v)

        @pl.loop(0, nch)
        def _(j):
            pltpu.async_copy(val_v.at[j], out_hbm.at[idx_v.at[j]], sem)

        @pl.loop(0, nch)
        def _(j):
            pltpu.make_async_copy(val_v.at[j], out_hbm.at[idx_v.at[j]], sem).wait()

    vals = jnp.arange(N, dtype=I32)
    return scatter_codes(dest_flat.reshape(nw, nch, SC_IDX_CHUNK), vals.reshape(nw, nch, SC_IDX_CHUNK))


def _row_copy(src, dst, sem):
    return pltpu.make_async_copy(src, dst, sem)


def _moe_kernel(be_ref, nv_ref, cc_ref, cn_ref, h2_ref, w1_ref, b1_ref, w2_ref, b2_ref, y4_ref,
                xbuf, ybuf, w1b, w2b, gsem, ssem, *, F, T, D, tme, nb):
    i = pl.program_id(0)
    s = i % 2
    ns = D // LANES

    def rows(first, n):
        return pl.ds(pl.multiple_of(first * ns, ns), n * ns)

    def gather_copy(tok, p, slot):
        return _row_copy(h2_ref.at[rows(tok, 1)], xbuf.at[slot, rows(p, 1)], gsem.at[slot])

    def token_of(code):
        return code & (T - 1) if T & (T - 1) == 0 else code % T

    def issue_rows(start_row, nv):
        ng = nv // ROW_UNROLL

        def grp(g, c):
            for j in range(ROW_UNROLL):
                start_row(g * ROW_UNROLL + j, j % 2)
            return c
        lax.fori_loop(0, ng, grp, 0)

        def one(p, c):
            start_row(p, 0)
            return c
        lax.fori_loop(ng * ROW_UNROLL, nv, one, 0)

    def start_gather(code_ref, nv, slot):
        issue_rows(lambda p, pri: gather_copy(token_of(code_ref[0, 0, p]), p, slot).start(priority=pri), nv)

    def wait_rows(make, nv):
        @pl.when(nv > 0)
        def _():
            make(nv).wait()

    def gathered(n, slot):
        return _row_copy(h2_ref.at[rows(0, n)], xbuf.at[slot, rows(0, n)], gsem.at[slot])

    def scattered(n, slot):
        return _row_copy(ybuf.at[slot, rows(0, n)], y4_ref.at[rows(0, n)], ssem.at[slot])

    @pl.when(i == 0)
    def _():
        start_gather(cc_ref, tme, 0)

    nv = nv_ref[i]
    gathered(tme, s).wait()

    @pl.when(i >= 2)
    def _():
        wait_rows(lambda n: scattered(n, s), nv_ref[i - 2])

    @pl.when(jnp.logical_or(i == 0, be_ref[i] != be_ref[jnp.maximum(i - 1, 0)]))
    def _():
        w1b[...] = w1_ref[0].astype(BF16)
        w2b[...] = w2_ref[0].astype(BF16)

    @pl.when(nv > 0)
    def _():
        x = _load_slabs(xbuf, tme, D, lead=(s,)).astype(BF16)
        for p in range(tme):
            gather_copy(token_of(cn_ref[0, 0, p]), p, 1 - s).start(priority=p % 2)
        hgu = jnp.dot(x, w1b[...], preferred_element_type=F32) + b1_ref[0]
        gate = jnp.minimum(hgu[:, :F], SWIGLU_LIMIT)
        up = jnp.clip(hgu[:, F:], -SWIGLU_LIMIT, SWIGLU_LIMIT)
        act = gate * _sigmoid(SWIGLU_ALPHA * gate) * (up + 1.0)
        y = jnp.dot(act.astype(BF16), w2b[...], preferred_element_type=F32) + b2_ref[0]
        _store_slabs(ybuf, y, lead=(s,))

        issue_rows(lambda p, pri: _row_copy(ybuf.at[s, rows(p, 1)], y4_ref.at[rows(cc_ref[0, 0, p], 1)],
                                            ssem.at[s]).start(priority=pri), nv)

    @pl.when(jnp.logical_and(nv == 0, i + 1 < nb))
    def _():
        start_gather(cn_ref, tme, 1 - s)

    @pl.when(i == nb - 1)
    def _():
        @pl.when(nv > 0)
        def _():
            gathered(tme, 1 - s).wait()
        wait_rows(lambda n: scattered(n, s), nv)
        if nb > 1:
            wait_rows(lambda n: scattered(n, 1 - s), nv_ref[i - 1])


def _experts(blk_exp, blk_valid, codes, h2s, w1, b1, w2, b2, tme):
    E, D, F2 = w1.shape
    F = F2 // 2
    ns = D // LANES
    T = h2s.shape[0] // ns
    nb = blk_exp.shape[0]
    codes3 = codes.reshape(nb, 1, tme)
    wspec = lambda shape: pl.BlockSpec(shape, lambda i, be, nv: (be[i], 0, 0))
    cspec = lambda off: pl.BlockSpec((1, 1, tme), lambda i, be, nv: (jnp.minimum(i + off, nb - 1), 0, 0),
                                     memory_space=pltpu.SMEM)
    return pl.pallas_call(
        functools.partial(_moe_kernel, F=F, T=T, D=D, tme=tme, nb=nb),
        grid_spec=pltpu.PrefetchScalarGridSpec(
            num_scalar_prefetch=2,
            grid=(nb,),
            in_specs=[cspec(0), cspec(1),
                      pl.BlockSpec(memory_space=pl.ANY),
                      wspec((1, D, F2)), wspec((1, 1, F2)), wspec((1, F, D)), wspec((1, 1, D))],
            out_specs=pl.BlockSpec(memory_space=pl.ANY),
            scratch_shapes=[pltpu.VMEM((2, tme * ns, LANES), F32), pltpu.VMEM((2, tme * ns, LANES), F32),
                            pltpu.VMEM((D, F2), BF16), pltpu.VMEM((F, D), BF16),
                            pltpu.SemaphoreType.DMA((2,)), pltpu.SemaphoreType.DMA((2,))]),
        out_shape=jax.ShapeDtypeStruct((TOP_K * T * ns, LANES), F32),
        compiler_params=_cparams(("arbitrary",)),
        name="experts",
    )(blk_exp, blk_valid, codes3, codes3, h2s, w1, b1.reshape(E, 1, F2), w2, b2.reshape(E, 1, D))


def _comb_kernel(g_ref, x1_ref, mod_ref, fw_ref, y0_ref, y1_ref, y2_ref, y3_ref, o_ref):
    g = g_ref[...]
    n, d = x1_ref.shape
    y = g[:, 0:1] * _load_slabs(y0_ref, n, d)
    for k, y_ref in ((1, y1_ref), (2, y2_ref), (3, y3_ref)):
        y = y + g[:, k:k + 1] * _load_slabs(y_ref, n, d)
    x2 = x1_ref[...] + mod_ref[0, 5:6, :] * y
    ms = jnp.mean(x2 * x2, axis=-1, keepdims=True)
    o_ref[...] = x2 * lax.rsqrt(ms + EPS) * fw_ref[...]


def _combine(gtc, x1, mod, final_w, y4, S):
    T, D = x1.shape
    tmc = 512
    nt = T // tmc
    yspec = lambda k: pl.BlockSpec((tmc * (D // LANES), LANES), lambda i: (k * nt + i, 0))
    return pl.pallas_call(
        _comb_kernel,
        grid=(nt,),
        in_specs=[pl.BlockSpec((tmc, LANES), lambda i: (i, 0)),
                  pl.BlockSpec((tmc, D), lambda i: (i, 0)),
                  pl.BlockSpec((1, 6, D), lambda i: (i * tmc // S, 0, 0)),
                  pl.BlockSpec((1, D), lambda i: (0, 0)),
                  yspec(0), yspec(1), yspec(2), yspec(3)],
        out_specs=pl.BlockSpec((tmc, D), lambda i: (i, 0)),
        out_shape=jax.ShapeDtypeStruct((T, D), F32),
        compiler_params=_cparams(("arbitrary",)),
        name="combine",
    )(gtc, x1, mod, final_w.reshape(1, D), y4, y4, y4, y4)


def _layer(x2, mod, B, S, norm1_w, w_in, conv_w, A_log, dt_bias, dn_norm_w, w_out,
           norm2_w, w_router, b_router, w1, b1, w2, b2):
    T, D = x2.shape
    qkv, qkv4, qkv16, dqkv, dz, gbc = _inproj(x2, mod, norm1_w, w_in, S)
    views = (qkv.reshape(B, 1, S, 3 * ATTN_W), qkv4, qkv16)
    branches = [_attn_branch(v, d) for v, d in zip(views, DILATIONS)]
    dn = _gdn(dqkv, dz, gbc, conv_w, A_log, dt_bias, dn_norm_w, B, S)
    (o1, l1), (o2, l2), (o3, l3) = branches
    x1, h2, te, rk, gtc, cnt = _outproj(o1, o2, o3, l1, l2, l3, dn, x2, mod, w_out, norm2_w,
                                        w_router, b_router, S)
    tme = MOE_ROWS
    P = T * TOP_K + N_EXPERTS * tme
    counts = cnt[:, 0]
    padded = (counts + tme - 1) // tme * tme
    pend = jnp.cumsum(padded)
    pstart = (pend - padded).astype(I32)
    blk_start = jnp.arange(P // tme, dtype=I32) * tme
    blk_exp = jnp.minimum(jnp.sum((pend[None, :] <= blk_start[:, None]).astype(I32), axis=1),
                          N_EXPERTS - 1).astype(I32)
    blk_valid = jnp.clip(pstart[blk_exp] + counts[blk_exp] - blk_start, 0, tme).astype(I32)
    dest = _dest(pstart, te, rk)
    codes = _invperm(dest[:TOP_K].reshape(TOP_K * T), P)
    y4 = _experts(blk_exp, blk_valid, codes, h2, w1, b1, w2, b2, tme)
    return x1, gtc, y4


def kernel(x, c, w_ada, b_ada, norm1_w, w_in, conv_w, A_log, dt_bias, dn_norm_w, w_out, norm2_w,
           w_router, b_router, w1, b1, w2, b2, final_norm_w):
    B, S, D = x.shape
    depth = w_ada.shape[0]
    assert S % (ATTN_BLK * DILATIONS[-1]) == 0 and depth == 1
    x2 = x.reshape(B * S, D)
    mod = _ada(c, w_ada[0], b_ada[0])
    x1, gtc, y4 = _layer(x2, mod, B, S, norm1_w[0], w_in[0], conv_w[0], A_log[0], dt_bias[0],
                         dn_norm_w[0], w_out[0], norm2_w[0], w_router[0], b_router[0],
                         w1[0], b1[0], w2[0], b2[0])
    out = _combine(gtc, x1, mod, final_norm_w, y4, S)
    return out.reshape(B, S, D)
```

```python
import functools

import jax
import jax.numpy as jnp
from jax import lax
from jax.experimental import pallas as pl
from jax.experimental.pallas import tpu as pltpu
from jax.experimental.pallas import tpu_sc as plsc

F32 = jnp.float32
BF16 = jnp.bfloat16
I32 = jnp.int32
HI = lax.Precision.HIGHEST

LANES = 128
ATTN_HEADS = 8
ATTN_HD = 64
ATTN_W = ATTN_HEADS * ATTN_HD
ATTN_BLK = 128
DILATIONS = (1, 4, 16)
DN_HEADS = 4
DN_D = 128
DN_W = DN_HEADS * DN_D
DN_CONV = 3 * DN_W
CONV_K = 4
DN_CHUNK = 64
N_EXPERTS = 32
TOP_K = 4
MOE_ROWS = 512
ROW_UNROLL = 8
GATHER_ANCHOR_COLS = 256
SWIGLU_LIMIT = 7.0
SWIGLU_ALPHA = 1.702
EPS = 1e-6
NEG = -1e30
MAIN_COLS = 3 * ATTN_W + DN_CONV + DN_W

VMEM_LIMIT = 56 * 1024 * 1024


def _cparams(sem):
    return pltpu.CompilerParams(dimension_semantics=sem, vmem_limit_bytes=VMEM_LIMIT)


def _nt(a, b, **kw):
    return lax.dot_general(a, b, (((1,), (1,)), ((), ())), preferred_element_type=F32, **kw)


def _sigmoid(x):
    return 1.0 / (1.0 + jnp.exp(-x))


def _store_slabs(ref, val, lead=()):
    n, d = val.shape
    ns = d // LANES
    for c in range(ns):
        ref[lead + (pl.ds(c, n, stride=ns), slice(None))] = val[:, c * LANES:(c + 1) * LANES]


def _load_slabs(ref, n, d, lead=()):
    ns = d // LANES
    return jnp.concatenate([ref[lead + (pl.ds(c, n, stride=ns), slice(None))] for c in range(ns)], axis=1)


def _ada_kernel(c_ref, w_ref, b_ref, o_ref):
    c = c_ref[...]
    cond = c * _sigmoid(c)
    o_ref[...] = jnp.dot(cond, w_ref[...], preferred_element_type=F32, precision=HI) + b_ref[...]


def _ada(c, w_ada, b_ada):
    B, D = c.shape
    N = w_ada.shape[1]
    cp = jnp.zeros((8, D), F32).at[:B].set(c)
    tn = 1024
    out = pl.pallas_call(
        _ada_kernel,
        grid=(N // tn,),
        in_specs=[pl.BlockSpec((8, D), lambda j: (0, 0)),
                  pl.BlockSpec((D, tn), lambda j: (0, j)),
                  pl.BlockSpec((1, tn), lambda j: (0, j))],
        out_specs=pl.BlockSpec((8, tn), lambda j: (0, j)),
        out_shape=jax.ShapeDtypeStruct((8, N), F32),
        compiler_params=_cparams(("arbitrary",)),
        name="ada",
    )(cp, w_ada, b_ada.reshape(1, N))
    return out[:B].reshape(B, 6, D)


def _inproj_kernel(x_ref, mod_ref, nw_ref, wm_ref, ws_ref, qkv_ref, qkv4_ref, qkv16_ref, dqkv_ref, dz_ref,
                   gbc_ref, scr):
    x = x_ref[...]
    tm = x.shape[0]
    shift = mod_ref[0, 0:1, :]
    scale = mod_ref[0, 1:2, :]
    ms = jnp.mean(x * x, axis=-1, keepdims=True)
    h = x * lax.rsqrt(ms + EPS) * nw_ref[...]
    hb = (h * (1.0 + scale) + shift).astype(BF16)
    nl = ATTN_W // LANES
    for j in range(3):
        cs = slice(j * ATTN_W, (j + 1) * ATTN_W)
        r = jnp.dot(hb, wm_ref[:, cs], preferred_element_type=F32)
        if j == 0:
            r = r * (ATTN_HD ** -0.5)
        qkv_ref[:, cs] = r.astype(BF16)
        for c in range(nl):
            scr[c] = r[:, c * LANES:(c + 1) * LANES]
        for d, ref in ((DILATIONS[1], qkv4_ref), (DILATIONS[2], qkv16_ref)):
            for res in range(d):
                part = jnp.concatenate([scr[c, pl.ds(res, tm // d, stride=d), :] for c in range(nl)], axis=1)
                ref[0, res, :, cs] = part.astype(BF16)
    for j in range(3):
        c0 = 3 * ATTN_W + j * DN_W
        r = jnp.dot(hb, wm_ref[:, c0:c0 + DN_W], preferred_element_type=F32)
        dqkv_ref[:, j * DN_W:(j + 1) * DN_W] = r.astype(BF16)
    c0 = 3 * ATTN_W + DN_CONV
    dz_ref[...] = jnp.dot(hb, wm_ref[:, c0:c0 + DN_W], preferred_element_type=F32).astype(BF16)
    gbc_ref[...] = jnp.dot(hb, ws_ref[...], preferred_element_type=F32)


def _inproj(x2, mod, norm_w, w_in, S):
    T, D = x2.shape
    tm = 512
    B, nt = T // S, S // tm
    wm = w_in[:, :MAIN_COLS].astype(BF16)
    ws = jnp.zeros((D, LANES), F32).at[:, :2 * DN_HEADS].set(w_in[:, MAIN_COLS:]).astype(BF16)
    return pl.pallas_call(
        _inproj_kernel,
        grid=(T // tm,),
        in_specs=[pl.BlockSpec((tm, D), lambda i: (i, 0)),
                  pl.BlockSpec((1, 6, D), lambda i: (i * tm // S, 0, 0)),
                  pl.BlockSpec((1, D), lambda i: (0, 0)),
                  pl.BlockSpec((D, MAIN_COLS), lambda i: (0, 0)),
                  pl.BlockSpec((D, LANES), lambda i: (0, 0))],
        out_specs=[pl.BlockSpec((tm, 3 * ATTN_W), lambda i: (i, 0))]
        + [pl.BlockSpec((1, d, tm // d, 3 * ATTN_W), lambda i: (i // nt, 0, i % nt, 0)) for d in DILATIONS[1:]]
        + [pl.BlockSpec((tm, DN_CONV), lambda i: (i, 0)),
           pl.BlockSpec((tm, DN_W), lambda i: (i, 0)),
           pl.BlockSpec((tm, LANES), lambda i: (i, 0))],
        out_shape=[jax.ShapeDtypeStruct((T, 3 * ATTN_W), BF16)]
        + [jax.ShapeDtypeStruct((B, d, S // d, 3 * ATTN_W), BF16) for d in DILATIONS[1:]]
        + [jax.ShapeDtypeStruct((T, DN_CONV), BF16),
           jax.ShapeDtypeStruct((T, DN_W), BF16),
           jax.ShapeDtypeStruct((T, LANES), F32)],
        scratch_shapes=[pltpu.VMEM((ATTN_W // LANES, tm, LANES), F32)],
        compiler_params=_cparams(("arbitrary",)),
        name="inproj",
    )(x2, mod, norm_w.reshape(1, D), wm, ws)


def _attn_kernel(q_ref, kc_ref, kp_ref, vc_ref, vp_ref, o_ref, lse_ref, kf, vf, *, qb):
    n = pl.program_id(2)
    kf[0:ATTN_BLK, :] = kp_ref[0, 0]
    kf[ATTN_BLK:, :] = kc_ref[0, 0]
    vf[0:ATTN_BLK, :] = vp_ref[0, 0]
    vf[ATTN_BLK:, :] = vc_ref[0, 0]
    row = lax.broadcasted_iota(I32, (ATTN_BLK, 2 * ATTN_BLK), 0)
    col = lax.broadcasted_iota(I32, (ATTN_BLK, 2 * ATTN_BLK), 1)
    band = jnp.logical_or(jnp.logical_and(col < ATTN_BLK, col >= row),
                          jnp.logical_and(col >= ATTN_BLK, col - ATTN_BLK <= row))
    lane = lax.broadcasted_iota(I32, (ATTN_BLK, LANES), 1)
    lo = lane < ATTN_HD

    def sub(j, carry):
        r0 = pl.multiple_of(j * ATTN_BLK, ATTN_BLK)
        first_col = jnp.where(jnp.logical_and(n == 0, j == 0), ATTN_BLK, 0)
        mask = jnp.logical_and(band, col >= first_col)
        npair = ATTN_W // LANES
        cols = [slice(hp * LANES, (hp + 1) * LANES) for hp in range(npair)]
        heads = [(hp, half) for hp in range(npair) for half in range(2)]
        scores = []
        for hp, half in heads:
            q2 = q_ref[0, 0, pl.ds(r0, ATTN_BLK), cols[hp]]
            qm = jnp.where(lo if half == 0 else jnp.logical_not(lo), q2, jnp.zeros_like(q2))
            scores.append(jnp.where(mask, _nt(qm, kf[pl.ds(r0, 2 * ATTN_BLK), cols[hp]]), NEG))
        maxes = [jnp.max(s, axis=-1, keepdims=True) for s in scores]
        probs = [jnp.exp(s - m) for s, m in zip(scores, maxes)]
        dens = [jnp.sum(p, axis=-1, keepdims=True) for p in probs]
        accs = [jnp.dot(p.astype(BF16), vf[pl.ds(r0, 2 * ATTN_BLK), cols[hp]], preferred_element_type=F32)
                for p, (hp, _) in zip(probs, heads)]
        outs = [a / d for a, d in zip(accs, dens)]
        lse_tile = jnp.zeros((ATTN_BLK, LANES), F32)
        for h, (m, d) in enumerate(zip(maxes, dens)):
            lse_tile = jnp.where(lane == h, m + jnp.log(d), lse_tile)
        for hp in range(npair):
            o_ref[0, 0, pl.ds(r0, ATTN_BLK), cols[hp]] = jnp.where(lo, outs[2 * hp], outs[2 * hp + 1]).astype(BF16)
        lse_ref[0, 0, pl.ds(r0, ATTN_BLK), :] = lse_tile
        return carry

    lax.fori_loop(0, qb // ATTN_BLK, sub, 0)


def _attn_branch(qkv, d):
    B, _, L, _ = qkv.shape
    qb = min(512, L)
    nsub = qb // ATTN_BLK
    cur = lambda c: pl.BlockSpec((1, 1, qb, ATTN_W), lambda b, r, n: (b, r, n, c))
    prev = lambda c: pl.BlockSpec((1, 1, ATTN_BLK, ATTN_W),
                                  lambda b, r, n: (b, r, jnp.maximum(n * nsub - 1, 0), c))
    return pl.pallas_call(
        functools.partial(_attn_kernel, qb=qb),
        grid=(B, d, L // qb),
        in_specs=[cur(0), cur(1), prev(1), cur(2), prev(2)],
        out_specs=[pl.BlockSpec((1, 1, qb, ATTN_W), lambda b, r, n: (b, r, n, 0)),
                   pl.BlockSpec((1, 1, qb, LANES), lambda b, r, n: (b, r, n, 0))],
        out_shape=[jax.ShapeDtypeStruct((B, d, L, ATTN_W), BF16),
                   jax.ShapeDtypeStruct((B, d, L, LANES), F32)],
        scratch_shapes=[pltpu.VMEM((qb + ATTN_BLK, ATTN_W), BF16),
                        pltpu.VMEM((qb + ATTN_BLK, ATTN_W), BF16)],
        compiler_params=_cparams(("arbitrary", "arbitrary", "arbitrary")),
        name=f"attn_d{d}",
    )(qkv, qkv, qkv, qkv, qkv)


def _gdn_kernel(x_ref, z_ref, g_ref, cw_ref, prm_ref, nw_ref, o_ref, xext, yc, s0, s1, *, rb):
    i = pl.program_id(1)

    @pl.when(i == 0)
    def _():
        xext[0:8, :] = jnp.zeros((8, DN_CONV), F32)
        s0[...] = jnp.zeros_like(s0)
        s1[...] = jnp.zeros_like(s1)

    @pl.when(i > 0)
    def _():
        xext[0:8, :] = xext[rb:rb + 8, :]

    xext[8:, :] = x_ref[0].astype(F32)
    y = cw_ref[CONV_K - 1:CONV_K, :] * xext[8:8 + rb, :]
    for j in range(CONV_K - 1):
        off = 8 - (CONV_K - 1) + j
        y = y + cw_ref[j:j + 1, :] * xext[off:off + rb, :]
    yc[...] = y * _sigmoid(y)

    C = DN_CHUNK
    H = DN_HEADS
    CW = H * C
    dot = functools.partial(jnp.dot, preferred_element_type=F32)

    def iota(shape, d):
        return lax.broadcasted_iota(I32, shape, d)

    ltri_b = jnp.where(iota((C, C), 0) >= iota((C, C), 1), 1.0, 0.0).astype(BF16)
    lane = iota((C, LANES), 1)
    blane = lane < H
    glane = jnp.logical_and(lane >= H, lane < 2 * H)
    e512 = jnp.where(jnp.logical_or(iota((LANES, DN_W), 1) // DN_D == iota((LANES, DN_W), 0),
                                    iota((LANES, DN_W), 1) // DN_D == iota((LANES, DN_W), 0) - H),
                     1.0, 0.0).astype(BF16)
    e256 = jnp.where(iota((LANES, CW), 1) // C == iota((LANES, CW), 0) - H, 1.0, 0.0).astype(BF16)
    row4 = iota((C, CW), 0)
    col4 = iota((C, CW), 1) % C
    eye4 = jnp.where(row4 == col4, 1.0, 0.0).astype(F32)
    blk = [iota((C, CW), 1) // C == h for h in range(H)]
    bd_cc = iota((CW, CW), 0) // C == iota((CW, CW), 1) // C
    bd_pair = iota((CW, CW), 0) // DN_D == iota((CW, CW), 1) // DN_D
    rt_mask = iota((CW, DN_W), 0) // C == iota((CW, DN_W), 1) // DN_D
    neg_a = -jnp.exp(prm_ref[0:1, :])
    dtb = prm_ref[1:2, :]
    nw = nw_ref[...]

    def hilo(x):
        hi = x.astype(BF16)
        return hi, (x - hi.astype(F32)).astype(BF16)

    def heads(a, w):
        return [a[:, h * w:(h + 1) * w] for h in range(H)]

    def l2n(a, mult):
        return jnp.concatenate(
            [p * (lax.rsqrt(jnp.sum(p * p, axis=-1, keepdims=True) + EPS) * mult) for p in heads(a, DN_D)],
            axis=1)

    def stack4(a):
        return jnp.concatenate([a, a, a, a], axis=0)

    zb = jnp.zeros((), BF16)
    nchunk = rb // C
    chunks = []
    for c in range(nchunk):
        rs = slice(c * C, (c + 1) * C)
        G = g_ref[0, rs, :]
        xg = G + dtb
        gv = jnp.where(glane, neg_a * (jnp.maximum(xg, 0.0) + jnp.log1p(jnp.exp(-jnp.abs(xg)))), 0.0)
        be = jnp.where(blane, _sigmoid(G), 0.0)
        g_hi, g_lo = hilo(gv)
        gcum = dot(ltri_b, g_hi) + dot(ltri_b, g_lo)
        gtot = gcum[C - 1:C, :]
        eg = jnp.where(glane, jnp.exp(gcum), 0.0)
        ek = jnp.where(glane, jnp.exp(gtot - gcum), 0.0)
        ex = dot(jnp.concatenate([be, eg, ek], axis=0).astype(BF16), e512)
        bexp, egexp, ekexp = ex[0:C], ex[C:2 * C], ex[2 * C:3 * C]
        gexp = dot(g_hi, e256) + dot(g_lo, e256)
        d_hi, d_lo = hilo(jnp.where(row4 > col4, gexp, 0.0))
        diff = dot(ltri_b, d_hi) + dot(ltri_b, d_lo)
        decay = jnp.exp(jnp.where(row4 >= col4, diff, NEG))

        q4 = l2n(yc[rs, 0:DN_W], DN_D ** -0.5)
        k4 = l2n(yc[rs, DN_W:2 * DN_W], 1.0)
        v4 = yc[rs, 2 * DN_W:3 * DN_W]
        kb4 = k4 * bexp
        vb4 = v4 * bexp
        rt = jnp.where(rt_mask, stack4(k4.astype(BF16)), jnp.zeros((), BF16))
        ai = _nt(jnp.concatenate([kb4, q4], axis=0).astype(BF16), rt)
        a4 = jnp.where(row4 > col4, ai[0:C] * decay, 0.0)
        pb = a4.astype(BF16)
        chunks.append(dict(
            rs=rs, q4=q4, k4=k4, kb4=kb4, vb4=vb4, egexp=egexp, ekexp=ekexp,
            intra=ai[C:2 * C] * decay, t4=eye4 - a4, pb=pb,
            bd=jnp.where(bd_cc, stack4(pb), zb)))

    for _ in range(5):
        for ch in chunks:
            ch["pb"] = dot(ch["pb"], ch["bd"]).astype(BF16)
        for ch in chunks:
            ch["bd"] = jnp.where(bd_cc, stack4(ch["pb"]), zb)
        for ch in chunks:
            ch["t4"] = ch["t4"] + dot(ch["t4"].astype(BF16), ch["bd"])

    for ch in chunks:
        q4, k4, kb4, vb4, egexp, ekexp = (ch[n] for n in ("q4", "k4", "kb4", "vb4", "egexp", "ekexp"))
        t4b = ch["t4"].astype(BF16)
        lstk = jnp.concatenate([jnp.where(blk[h], t4b, zb) for h in range(H)], axis=0)
        kbg4 = kb4 * egexp
        rstk = jnp.concatenate(
            [jnp.concatenate([vb, kbg], axis=1) for vb, kbg in zip(heads(vb4, DN_D), heads(kbg4, DN_D))],
            axis=0).astype(BF16)
        uw = dot(lstk, rstk)
        u4 = jnp.concatenate([uw[h * C:(h + 1) * C, 0:DN_D] for h in range(H)], axis=1)
        w4 = jnp.concatenate([uw[h * C:(h + 1) * C, DN_D:2 * DN_D] for h in range(H)], axis=1)
        ib = ch["intra"].astype(BF16)
        ch.update(
            u4=u4, wq=jnp.concatenate([w4, q4 * egexp], axis=0).astype(BF16),
            kd4=(k4 * ekexp).astype(BF16), gl4=egexp[C - 1:C, :],
            lint=jnp.concatenate([jnp.where(blk[h], ib, zb) for h in range(H)], axis=0))

    for ch in chunks:
        rs, u4, wq, kd4, gl4, lint = (ch[n] for n in ("rs", "u4", "wq", "kd4", "gl4", "lint"))
        ra = dot(wq[:, 0:CW], s0[...].astype(BF16))
        rc = dot(wq[:, CW:2 * CW], s1[...].astype(BF16))
        vn = u4 - jnp.concatenate([ra[0:C], rc[0:C]], axis=1)
        vnb = vn.astype(BF16)
        oi = dot(lint, jnp.concatenate(heads(vnb, DN_D), axis=0))
        o = (jnp.concatenate([ra[C:2 * C], rc[C:2 * C]], axis=1)
             + jnp.concatenate([oi[h * C:(h + 1) * C] for h in range(H)], axis=1))
        tn = (((0,), (0,)), ((), ()))
        s0[...] = s0[...] * gl4[:, 0:CW] + jnp.where(
            bd_pair, lax.dot_general(kd4[:, 0:CW], vnb[:, 0:CW], tn, preferred_element_type=F32), 0.0)
        s1[...] = s1[...] * gl4[:, CW:2 * CW] + jnp.where(
            bd_pair, lax.dot_general(kd4[:, CW:2 * CW], vnb[:, CW:2 * CW], tn, preferred_element_type=F32), 0.0)

        z = z_ref[0, rs, :].astype(F32)
        on = jnp.concatenate(
            [p * lax.rsqrt(jnp.mean(p * p, axis=-1, keepdims=True) + EPS) * nw for p in heads(o, DN_D)], axis=1)
        o_ref[0, rs, :] = (on * (z * _sigmoid(z))).astype(BF16)


def _gdn(dqkv, dz, gbc, conv_w, A_log, dt_bias, dn_norm_w, B, S):
    rb = 512
    prm = jnp.zeros((2, LANES), F32)
    prm = prm.at[0, DN_HEADS:2 * DN_HEADS].set(A_log.astype(F32))
    prm = prm.at[1, DN_HEADS:2 * DN_HEADS].set(dt_bias.astype(F32))
    out = pl.pallas_call(
        functools.partial(_gdn_kernel, rb=rb),
        grid=(B, S // rb),
        in_specs=[pl.BlockSpec((1, rb, DN_CONV), lambda b, i: (b, i, 0)),
                  pl.BlockSpec((1, rb, DN_W), lambda b, i: (b, i, 0)),
                  pl.BlockSpec((1, rb, LANES), lambda b, i: (b, i, 0)),
                  pl.BlockSpec((CONV_K, DN_CONV), lambda b, i: (0, 0)),
                  pl.BlockSpec((2, LANES), lambda b, i: (0, 0)),
                  pl.BlockSpec((1, DN_D), lambda b, i: (0, 0))],
        out_specs=pl.BlockSpec((1, rb, DN_W), lambda b, i: (b, i, 0)),
        out_shape=jax.ShapeDtypeStruct((B, S, DN_W), BF16),
        scratch_shapes=[pltpu.VMEM((rb + 8, DN_CONV), F32),
                        pltpu.VMEM((rb, DN_CONV), F32),
                        pltpu.VMEM((2 * DN_D, 2 * DN_D), F32),
                        pltpu.VMEM((2 * DN_D, 2 * DN_D), F32)],
        compiler_params=_cparams(("arbitrary", "arbitrary")),
        name="gdn",
    )(dqkv.reshape(B, S, DN_CONV), dz.reshape(B, S, DN_W), gbc.reshape(B, S, LANES),
      conv_w, prm, dn_norm_w.reshape(1, DN_D))
    return out.reshape(B * S, DN_W)


def _out_kernel(o1_ref, o2_ref, o3_ref, l1_ref, l2_ref, l3_ref, dn_ref, x_ref, mod_ref, wo_ref,
                n2_ref, wr_ref, br_ref,
                x1_ref, h2_ref, te_ref, rk_ref, gtc_ref, cnt_ref, base, scr, *, tm):
    i = pl.program_id(0)

    @pl.when(i == 0)
    def _():
        base[...] = jnp.zeros_like(base)

    def natural(ref, d):
        if d == 1:
            return ref[0, 0].astype(F32)
        nl = ref.shape[-1] // LANES
        for res in range(d):
            blk = ref[0, res].astype(F32)
            for c in range(nl):
                scr[c, pl.ds(res, tm // d, stride=d), :] = blk[:, c * LANES:(c + 1) * LANES]
        return jnp.concatenate([scr[c] for c in range(nl)], axis=1)

    l1, l2, l3 = (natural(r, d) for r, d in zip((l1_ref, l2_ref, l3_ref), DILATIONS))
    mx = jnp.maximum(jnp.maximum(l1, l2), l3)
    e1, e2, e3 = jnp.exp(l1 - mx), jnp.exp(l2 - mx), jnp.exp(l3 - mx)
    zs = e1 + e2 + e3
    er = lax.broadcasted_iota(I32, (LANES, ATTN_W), 0)
    ec = lax.broadcasted_iota(I32, (LANES, ATTN_W), 1)
    expand = jnp.where(ec // ATTN_HD == er, 1.0, 0.0).astype(BF16)
    attn = jnp.zeros((tm, ATTN_W), F32)
    for e, o_ref, d in zip((e1, e2, e3), (o1_ref, o2_ref, o3_ref), DILATIONS):
        wgt = jnp.dot((e / zs).astype(BF16), expand, preferred_element_type=F32)
        attn = attn + wgt * natural(o_ref, d)
    mix = (jnp.dot(attn.astype(BF16), wo_ref[0:ATTN_W, :], preferred_element_type=F32)
           + jnp.dot(dn_ref[...], wo_ref[ATTN_W:, :], preferred_element_type=F32))
    x1 = x_ref[...] + mod_ref[0, 2:3, :] * mix
    x1_ref[...] = x1
    ms = jnp.mean(x1 * x1, axis=-1, keepdims=True)
    h2 = x1 * lax.rsqrt(ms + EPS) * n2_ref[...]
    h2 = h2 * (1.0 + mod_ref[0, 4:5, :]) + mod_ref[0, 3:4, :]
    _store_slabs(h2_ref, h2)

    lg = _nt(wr_ref[...], h2, precision=HI) + br_ref[...]
    eidx = lax.broadcasted_iota(I32, (N_EXPERTS, tm), 0)
    vals, idxs, sels = [], [], []
    for _ in range(TOP_K):
        m = jnp.max(lg, axis=0, keepdims=True)
        idx = jnp.min(jnp.where(lg == m, eidx, N_EXPERTS), axis=0, keepdims=True)
        sel = eidx == idx
        vals.append(m)
        idxs.append(idx)
        sels.append(sel)
        lg = jnp.where(sel, -jnp.inf, lg)
    ex = [jnp.exp(v - vals[0]) for v in vals]
    den = ex[0] + ex[1] + ex[2] + ex[3]
    gates = [e / den for e in ex]

    msum = jnp.zeros((N_EXPERTS, tm), F32)
    for sel in sels:
        msum = msum + jnp.where(sel, 1.0, 0.0)
    tr = lax.broadcasted_iota(I32, (tm, tm), 0)
    tc = lax.broadcasted_iota(I32, (tm, tm), 1)
    upper = jnp.where(tr <= tc, 1.0, 0.0).astype(BF16)
    incl = jnp.dot(msum.astype(BF16), upper, preferred_element_type=F32)
    pos = base[:, 0:1] + (incl - msum)
    sub8 = lax.broadcasted_iota(I32, (8, tm), 0)
    te = jnp.zeros((8, tm), I32)
    rk = jnp.zeros((8, tm), I32)
    gt = jnp.zeros((8, tm), F32)
    for k in range(TOP_K):
        rank_k = jnp.sum(jnp.where(sels[k], pos, 0.0), axis=0, keepdims=True).astype(I32)
        te = jnp.where(sub8 == k, idxs[k], te)
        rk = jnp.where(sub8 == k, rank_k, rk)
        gt = jnp.where(sub8 == k, gates[k], gt)
    te_ref[...] = te
    rk_ref[...] = rk
    gtc_ref[...] = jnp.transpose(jnp.concatenate([gt, jnp.zeros((LANES - 8, tm), F32)], axis=0))
    base[...] = base[...] + jnp.sum(msum, axis=1, keepdims=True)
    cnt_ref[...] = base[...].astype(I32)


def _outproj(o1, o2, o3, l1, l2, l3, dn, x2, mod, w_out, norm2_w, w_router, b_router, S):
    T, D = x2.shape
    tm = 512
    nt = S // tm
    row = lambda w: pl.BlockSpec((tm, w), lambda i: (i, 0))
    res = lambda d, w: pl.BlockSpec((1, d, tm // d, w), lambda i: (i // nt, 0, i % nt, 0))
    colb = pl.BlockSpec((8, tm), lambda i: (0, i))
    return pl.pallas_call(
        functools.partial(_out_kernel, tm=tm),
        grid=(T // tm,),
        in_specs=[res(d, ATTN_W) for d in DILATIONS] + [res(d, LANES) for d in DILATIONS]
        + [row(DN_W), row(D),
                  pl.BlockSpec((1, 6, D), lambda i: (i * tm // S, 0, 0)),
                  pl.BlockSpec((D, D), lambda i: (0, 0)),
                  pl.BlockSpec((1, D), lambda i: (0, 0)),
                  pl.BlockSpec((N_EXPERTS, D), lambda i: (0, 0)),
                  pl.BlockSpec((N_EXPERTS, 1), lambda i: (0, 0))],
        out_specs=[row(D), pl.BlockSpec((tm * (D // LANES), LANES), lambda i: (i, 0)), colb, colb,
                   row(LANES), pl.BlockSpec((N_EXPERTS, LANES), lambda i: (0, 0))],
        out_shape=[jax.ShapeDtypeStruct((T, D), F32),
                   jax.ShapeDtypeStruct((T * (D // LANES), LANES), F32),
                   jax.ShapeDtypeStruct((8, T), I32),
                   jax.ShapeDtypeStruct((8, T), I32),
                   jax.ShapeDtypeStruct((T, LANES), F32),
                   jax.ShapeDtypeStruct((N_EXPERTS, LANES), I32)],
        scratch_shapes=[pltpu.VMEM((N_EXPERTS, LANES), F32),
                        pltpu.VMEM((ATTN_W // LANES, tm, LANES), F32)],
        compiler_params=_cparams(("arbitrary",)),
        name="outproj_router",
    )(o1, o2, o3, l1, l2, l3, dn, x2, mod, w_out.astype(BF16), norm2_w.reshape(1, D),
      jnp.transpose(w_router), b_router.reshape(N_EXPERTS, 1))


def _dest_kernel(ps_ref, te_ref, rk_ref, d_ref):
    te = te_ref[...]
    acc = jnp.zeros(te.shape, I32)
    for e in range(N_EXPERTS):
        acc = jnp.where(te == e, ps_ref[e], acc)
    d_ref[...] = acc + rk_ref[...]


def _dest(pstart, te, rk):
    T = te.shape[1]
    tb = 2048
    return pl.pallas_call(
        _dest_kernel,
        grid_spec=pltpu.PrefetchScalarGridSpec(
            num_scalar_prefetch=1,
            grid=(T // tb,),
            in_specs=[pl.BlockSpec((8, tb), lambda i, ps: (0, i)),
                      pl.BlockSpec((8, tb), lambda i, ps: (0, i))],
            out_specs=pl.BlockSpec((8, tb), lambda i, ps: (0, i))),
        out_shape=jax.ShapeDtypeStruct((8, T), I32),
        compiler_params=_cparams(("arbitrary",)),
        name="dest_rows",
    )(pstart, te, rk)


SC_CORES = 2
SC_SUBCORES = 16
SC_IDX_CHUNK = 128


def _invperm(dest_flat, P):
    N = dest_flat.shape[0]
    nw = SC_CORES * SC_SUBCORES
    nch = N // (nw * SC_IDX_CHUNK)
    mesh = plsc.VectorSubcoreMesh(core_axis_name="c", subcore_axis_name="s",
                                  num_cores=SC_CORES, num_subcores=SC_SUBCORES)

    @functools.partial(
        pl.kernel, mesh=mesh, out_type=jax.ShapeDtypeStruct((P,), I32),
        scratch_types=[pltpu.VMEM((nch, SC_IDX_CHUNK), I32), pltpu.VMEM((nch, SC_IDX_CHUNK), I32),
                       pltpu.SemaphoreType.DMA])
    def scatter_codes(idx_hbm, val_hbm, out_hbm, idx_v, val_v, sem):
        wid = lax.axis_index("s") * SC_CORES + lax.axis_index("c")
        pltpu.sync_copy(idx_hbm.at[wid], idx_v)
        pltpu.sync_copy(val_hbm.at[wid], val_v)

        @pl.loop(0, nch)
        def _(j):
            pltpu.async_copy(val_v.at[j], out_hbm.at[idx_v.at[j]], sem)

        @pl.loop(0, nch)
        def _(j):
            pltpu.make_async_copy(val_v.at[j], out_hbm.at[idx_v.at[j]], sem).wait()

    vals = jnp.arange(N, dtype=I32)
    return scatter_codes(dest_flat.reshape(nw, nch, SC_IDX_CHUNK), vals.reshape(nw, nch, SC_IDX_CHUNK))


def _row_copy(src, dst, sem):
    return pltpu.make_async_copy(src, dst, sem)


def _moe_kernel(be_ref, nv_ref, cp_ref, cc_ref, cn_ref, h2_ref, w1_ref, b1_ref, w2_ref, b2_ref, y4_ref,
                xbuf, ybuf, w1b, w2b, gsem, ssem, *, F, T, D, tme, nb):
    i = pl.program_id(0)
    s = i % 2
    ns = D // LANES
    spare = TOP_K * T

    def rows(first, n):
        return pl.ds(pl.multiple_of(first * ns, ns), n * ns)

    def gather_copy(tok, p, slot):
        return _row_copy(h2_ref.at[rows(tok, 1)], xbuf.at[slot, rows(p, 1)], gsem.at[slot])

    def token_of(code):
        return code & (T - 1) if T & (T - 1) == 0 else code % T

    def issue_rows(start_row, nv):
        ng = nv // ROW_UNROLL

        def grp(g, c):
            for j in range(ROW_UNROLL):
                start_row(g * ROW_UNROLL + j, j % 2)
            return c
        lax.fori_loop(0, ng, grp, 0)

        def one(p, c):
            start_row(p, 0)
            return c
        lax.fori_loop(ng * ROW_UNROLL, nv, one, 0)

    def start_gather(code_ref, nv, slot):
        issue_rows(lambda p, pri: gather_copy(token_of(code_ref[0, 0, p]), p, slot).start(priority=pri), nv)

    def wait_rows(make, nv):
        @pl.when(nv > 0)
        def _():
            make(nv).wait()

    def gathered(n, slot):
        return _row_copy(h2_ref.at[rows(0, n)], xbuf.at[slot, rows(0, n)], gsem.at[slot])

    def scattered(n, slot):
        return _row_copy(ybuf.at[slot, rows(0, n)], y4_ref.at[rows(0, n)], ssem.at[slot])

    @pl.when(i == 0)
    def _():
        ybuf[...] = jnp.zeros_like(ybuf)
        for slot in range(2):
            fill = _row_copy(ybuf.at[slot, rows(0, tme)], y4_ref.at[rows(spare + slot * tme, tme)],
                             ssem.at[slot])
            fill.start()
            fill.wait()
        start_gather(cc_ref, tme, 0)

    nv = nv_ref[i]
    nvp = jnp.where(i > 0, nv_ref[jnp.maximum(i - 1, 0)], 0)
    gathered(tme, s).wait()

    def scatter_copy(p, dst_row, slot):
        return _row_copy(ybuf.at[slot, rows(p, 1)], y4_ref.at[rows(dst_row, 1)], ssem.at[slot])

    @pl.when(jnp.logical_or(i == 0, be_ref[i] != be_ref[jnp.maximum(i - 1, 0)]))
    def _():
        w1b[...] = w1_ref[0].astype(BF16)
        w2b[...] = w2_ref[0].astype(BF16)

    @pl.when(nv > 0)
    def _():
        x = _load_slabs(xbuf, tme, D, lead=(s,)).astype(BF16)
        hgu = jnp.dot(x, w1b[...], preferred_element_type=F32) + b1_ref[0]
        def scatter_prev(lo, hi):
            for p in range(lo, hi):
                dst = jnp.where(p < nvp, cp_ref[0, 0, p], spare + (1 - s) * tme + p)
                scatter_copy(p, dst, 1 - s).start(priority=p % 2)

        ngroup = (2 * F) // GATHER_ANCHOR_COLS
        per = tme // ngroup
        half = tme // 2
        for g in range(ngroup):
            for p in range(g * per, (g + 1) * per):
                gather_copy(token_of(cn_ref[0, 0, p]), p, 1 - s).start(priority=p % 2)
            scatter_prev(g * half // ngroup, (g + 1) * half // ngroup)
            c0 = g * GATHER_ANCHOR_COLS
            xbuf[1 - s, tme * ns:tme * ns + 8, :] = hgu[0:8, c0:c0 + LANES]
            ybuf[1 - s, tme * ns:tme * ns + 8, :] = hgu[8:16, c0:c0 + LANES]
        gate = jnp.minimum(hgu[:, :F], SWIGLU_LIMIT)
        up = jnp.clip(hgu[:, F:], -SWIGLU_LIMIT, SWIGLU_LIMIT)
        act = gate * _sigmoid(SWIGLU_ALPHA * gate) * (up + 1.0)
        y = jnp.dot(act.astype(BF16), w2b[...], preferred_element_type=F32) + b2_ref[0]
        ngroup2 = D // GATHER_ANCHOR_COLS
        for g in range(ngroup2):
            scatter_prev(half + g * half // ngroup2, half + (g + 1) * half // ngroup2)
            c0 = g * GATHER_ANCHOR_COLS
            ybuf[1 - s, tme * ns:tme * ns + 8, :] = y[0:8, c0:c0 + LANES]

        @pl.when(i > 0)
        def _():
            scattered(tme, s).wait()
        _store_slabs(ybuf, y, lead=(s,))

    @pl.when(nv == 0)
    def _():
        @pl.when(i < nb)
        def _():
            start_gather(cn_ref, tme, 1 - s)

        @pl.when(nvp > 0)
        def _():
            issue_rows(lambda p, pri: scatter_copy(p, cp_ref[0, 0, p], 1 - s).start(priority=pri), nvp)
            wait_rows(lambda n: scattered(n, 1 - s), nvp)
            scattered(tme, s).wait()


def _experts(blk_exp, blk_valid, codes, h2s, w1, b1, w2, b2, tme):
    E, D, F2 = w1.shape
    F = F2 // 2
    ns = D // LANES
    T = h2s.shape[0] // ns
    nb = blk_exp.shape[0]
    codes3 = codes.reshape(nb, 1, tme)
    blk_exp = jnp.concatenate([blk_exp, blk_exp[-1:]])
    blk_valid = jnp.concatenate([blk_valid, jnp.zeros((1,), I32)])
    wspec = lambda shape: pl.BlockSpec(shape, lambda i, be, nv: (be[i], 0, 0))
    cspec = lambda off: pl.BlockSpec((1, 1, tme), lambda i, be, nv: (jnp.clip(i + off, 0, nb - 1), 0, 0),
                                     memory_space=pltpu.SMEM)
    return pl.pallas_call(
        functools.partial(_moe_kernel, F=F, T=T, D=D, tme=tme, nb=nb),
        grid_spec=pltpu.PrefetchScalarGridSpec(
            num_scalar_prefetch=2,
            grid=(nb + 1,),
            in_specs=[cspec(-1), cspec(0), cspec(1),
                      pl.BlockSpec(memory_space=pl.ANY),
                      wspec((1, D, F2)), wspec((1, 1, F2)), wspec((1, F, D)), wspec((1, 1, D))],
            out_specs=pl.BlockSpec(memory_space=pl.ANY),
            scratch_shapes=[pltpu.VMEM((2, tme * ns + 8, LANES), F32),
                            pltpu.VMEM((2, tme * ns + 8, LANES), F32),
                            pltpu.VMEM((D, F2), BF16), pltpu.VMEM((F, D), BF16),
                            pltpu.SemaphoreType.DMA((2,)), pltpu.SemaphoreType.DMA((2,))]),
        out_shape=jax.ShapeDtypeStruct(((TOP_K * T + 2 * tme) * ns, LANES), F32),
        compiler_params=_cparams(("arbitrary",)),
        name="experts",
    )(blk_exp, blk_valid, codes3, codes3, codes3, h2s, w1, b1.reshape(E, 1, F2), w2, b2.reshape(E, 1, D))


def _comb_kernel(g_ref, x1_ref, mod_ref, fw_ref, y0_ref, y1_ref, y2_ref, y3_ref, o_ref):
    g = g_ref[...]
    n, d = x1_ref.shape
    y = g[:, 0:1] * _load_slabs(y0_ref, n, d)
    for k, y_ref in ((1, y1_ref), (2, y2_ref), (3, y3_ref)):
        y = y + g[:, k:k + 1] * _load_slabs(y_ref, n, d)
    x2 = x1_ref[...] + mod_ref[0, 5:6, :] * y
    ms = jnp.mean(x2 * x2, axis=-1, keepdims=True)
    o_ref[...] = x2 * lax.rsqrt(ms + EPS) * fw_ref[...]


def _combine(gtc, x1, mod, final_w, y4, S):
    T, D = x1.shape
    tmc = 512
    nt = T // tmc
    yspec = lambda k: pl.BlockSpec((tmc * (D // LANES), LANES), lambda i: (k * nt + i, 0))
    return pl.pallas_call(
        _comb_kernel,
        grid=(nt,),
        in_specs=[pl.BlockSpec((tmc, LANES), lambda i: (i, 0)),
                  pl.BlockSpec((tmc, D), lambda i: (i, 0)),
                  pl.BlockSpec((1, 6, D), lambda i: (i * tmc // S, 0, 0)),
                  pl.BlockSpec((1, D), lambda i: (0, 0)),
                  yspec(0), yspec(1), yspec(2), yspec(3)],
        out_specs=pl.BlockSpec((tmc, D), lambda i: (i, 0)),
        out_shape=jax.ShapeDtypeStruct((T, D), F32),
        compiler_params=_cparams(("arbitrary",)),
        name="combine",
    )(gtc, x1, mod, final_w.reshape(1, D), y4, y4, y4, y4)


def _layer(x2, mod, B, S, norm1_w, w_in, conv_w, A_log, dt_bias, dn_norm_w, w_out,
           norm2_w, w_router, b_router, w1, b1, w2, b2):
    T, D = x2.shape
    qkv, qkv4, qkv16, dqkv, dz, gbc = _inproj(x2, mod, norm1_w, w_in, S)
    views = (qkv.reshape(B, 1, S, 3 * ATTN_W), qkv4, qkv16)
    branches = [_attn_branch(v, d) for v, d in zip(views, DILATIONS)]
    dn = _gdn(dqkv, dz, gbc, conv_w, A_log, dt_bias, dn_norm_w, B, S)
    (o1, l1), (o2, l2), (o3, l3) = branches
    x1, h2, te, rk, gtc, cnt = _outproj(o1, o2, o3, l1, l2, l3, dn, x2, mod, w_out, norm2_w,
                                        w_router, b_router, S)
    tme = MOE_ROWS
    P = T * TOP_K + N_EXPERTS * tme
    counts = cnt[:, 0]
    padded = (counts + tme - 1) // tme * tme
    pend = jnp.cumsum(padded)
    pstart = (pend - padded).astype(I32)
    blk_start = jnp.arange(P // tme, dtype=I32) * tme
    blk_exp = jnp.minimum(jnp.sum((pend[None, :] <= blk_start[:, None]).astype(I32), axis=1),
                          N_EXPERTS - 1).astype(I32)
    blk_valid = jnp.clip(pstart[blk_exp] + counts[blk_exp] - blk_start, 0, tme).astype(I32)
    dest = _dest(pstart, te, rk)
    codes = _invperm(dest[:TOP_K].reshape(TOP_K * T), P)
    y4 = _experts(blk_exp, blk_valid, codes, h2, w1, b1, w2, b2, tme)
    return x1, gtc, y4


def kernel(x, c, w_ada, b_ada, norm1_w, w_in, conv_w, A_log, dt_bias, dn_norm_w, w_out, norm2_w,
           w_router, b_router, w1, b1, w2, b2, final_norm_w):
    B, S, D = x.shape
    depth = w_ada.shape[0]
    assert S % (ATTN_BLK * DILATIONS[-1]) == 0 and depth == 1
    x2 = x.reshape(B * S, D)
    mod = _ada(c, w_ada[0], b_ada[0])
    x1, gtc, y4 = _layer(x2, mod, B, S, norm1_w[0], w_in[0], conv_w[0], A_log[0], dt_bias[0],
                         dn_norm_w[0], w_out[0], norm2_w[0], w_router[0], b_router[0],
                         w1[0], b1[0], w2[0], b2[0])
    out = _combine(gtc, x1, mod, final_norm_w, y4, S)
    return out.reshape(B, S, D)
```

```python
import functools

import jax
import jax.numpy as jnp
from jax import lax
from jax.experimental import pallas as pl
from jax.experimental.pallas import tpu as pltpu
from jax.experimental.pallas import tpu_sc as plsc

F32 = jnp.float32
BF16 = jnp.bfloat16
I32 = jnp.int32
U32 = jnp.uint32
HI = lax.Precision.HIGHEST

LANES = 128
ATTN_HEADS = 8
ATTN_HD = 64
ATTN_W = ATTN_HEADS * ATTN_HD
ATTN_BLK = 128
DILATIONS = (1, 4, 16)
DN_HEADS = 4
DN_D = 128
DN_W = DN_HEADS * DN_D
DN_CONV = 3 * DN_W
CONV_K = 4
DN_CHUNK = 64
N_EXPERTS = 32
TOP_K = 4
MOE_ROWS = 512
ROW_UNROLL = 8
SWIGLU_LIMIT = 7.0
SWIGLU_ALPHA = 1.702
EPS = 1e-6
NEG = -1e30
MAIN_COLS = 3 * ATTN_W + DN_CONV + DN_W

VMEM_LIMIT = 56 * 1024 * 1024


def _cparams(sem):
    return pltpu.CompilerParams(dimension_semantics=sem, vmem_limit_bytes=VMEM_LIMIT)


def _nt(a, b, **kw):
    return lax.dot_general(a, b, (((1,), (1,)), ((), ())), preferred_element_type=F32, **kw)


def _sigmoid(x):
    return 1.0 / (1.0 + jnp.exp(-x))


def _store_packed(ref, val, lead=()):
    n, d = val.shape
    h = d // 2
    lo = lax.bitcast_convert_type(val[:, :h].astype(BF16).astype(F32), U32)
    hi = lax.bitcast_convert_type(val[:, h:].astype(BF16).astype(F32), U32)
    word = (lo >> 16) | hi
    for c in range(h // LANES):
        ref[lead + (slice(None), c, slice(None))] = word[:, c * LANES:(c + 1) * LANES]


def _load_packed(ref, lead=()):
    nc = ref.shape[-2]
    word = jnp.concatenate([ref[lead + (slice(None), c, slice(None))] for c in range(nc)], axis=1)
    lo = lax.bitcast_convert_type(word << 16, F32)
    hi = lax.bitcast_convert_type(word & jnp.uint32(0xFFFF0000), F32)
    return jnp.concatenate([lo, hi], axis=1)


def _ada_kernel(c_ref, w_ref, b_ref, o_ref):
    c = c_ref[...]
    cond = c * _sigmoid(c)
    o_ref[...] = jnp.dot(cond, w_ref[...], preferred_element_type=F32, precision=HI) + b_ref[...]


def _ada(c, w_ada, b_ada):
    B, D = c.shape
    N = w_ada.shape[1]
    cp = jnp.zeros((8, D), F32).at[:B].set(c)
    tn = 1024
    out = pl.pallas_call(
        _ada_kernel,
        grid=(N // tn,),
        in_specs=[pl.BlockSpec((8, D), lambda j: (0, 0)),
                  pl.BlockSpec((D, tn), lambda j: (0, j)),
                  pl.BlockSpec((1, tn), lambda j: (0, j))],
        out_specs=pl.BlockSpec((8, tn), lambda j: (0, j)),
        out_shape=jax.ShapeDtypeStruct((8, N), F32),
        compiler_params=_cparams(("arbitrary",)),
        name="ada",
    )(cp, w_ada, b_ada.reshape(1, N))
    return out[:B].reshape(B, 6, D)


def _inproj_kernel(x_ref, mod_ref, nw_ref, wm_ref, ws_ref, qkv_ref, qkv4_ref, qkv16_ref, dqkv_ref, dz_ref,
                   gbc_ref, scr):
    x = x_ref[...]
    tm = x.shape[0]
    shift = mod_ref[0, 0:1, :]
    scale = mod_ref[0, 1:2, :]
    ms = jnp.mean(x * x, axis=-1, keepdims=True)
    h = x * lax.rsqrt(ms + EPS) * nw_ref[...]
    hb = (h * (1.0 + scale) + shift).astype(BF16)
    nl = ATTN_W // LANES
    for j in range(3):
        cs = slice(j * ATTN_W, (j + 1) * ATTN_W)
        r = jnp.dot(hb, wm_ref[:, cs], preferred_element_type=F32)
        if j == 0:
            r = r * (ATTN_HD ** -0.5)
        qkv_ref[:, cs] = r.astype(BF16)
        for c in range(nl):
            scr[c] = r[:, c * LANES:(c + 1) * LANES]
        for d, ref in ((DILATIONS[1], qkv4_ref), (DILATIONS[2], qkv16_ref)):
            for res in range(d):
                part = jnp.concatenate([scr[c, pl.ds(res, tm // d, stride=d), :] for c in range(nl)], axis=1)
                ref[0, res, :, cs] = part.astype(BF16)
    for j in range(3):
        c0 = 3 * ATTN_W + j * DN_W
        r = jnp.dot(hb, wm_ref[:, c0:c0 + DN_W], preferred_element_type=F32)
        dqkv_ref[:, j * DN_W:(j + 1) * DN_W] = r.astype(BF16)
    c0 = 3 * ATTN_W + DN_CONV
    dz_ref[...] = jnp.dot(hb, wm_ref[:, c0:c0 + DN_W], preferred_element_type=F32).astype(BF16)
    gbc_ref[...] = jnp.dot(hb, ws_ref[...], preferred_element_type=F32)


def _inproj(x2, mod, norm_w, w_in, S):
    T, D = x2.shape
    tm = 512
    B, nt = T // S, S // tm
    wm = w_in[:, :MAIN_COLS].astype(BF16)
    ws = jnp.zeros((D, LANES), F32).at[:, :2 * DN_HEADS].set(w_in[:, MAIN_COLS:]).astype(BF16)
    return pl.pallas_call(
        _inproj_kernel,
        grid=(T // tm,),
        in_specs=[pl.BlockSpec((tm, D), lambda i: (i, 0)),
                  pl.BlockSpec((1, 6, D), lambda i: (i * tm // S, 0, 0)),
                  pl.BlockSpec((1, D), lambda i: (0, 0)),
                  pl.BlockSpec((D, MAIN_COLS), lambda i: (0, 0)),
                  pl.BlockSpec((D, LANES), lambda i: (0, 0))],
        out_specs=[pl.BlockSpec((tm, 3 * ATTN_W), lambda i: (i, 0))]
        + [pl.BlockSpec((1, d, tm // d, 3 * ATTN_W), lambda i: (i // nt, 0, i % nt, 0)) for d in DILATIONS[1:]]
        + [pl.BlockSpec((tm, DN_CONV), lambda i: (i, 0)),
           pl.BlockSpec((tm, DN_W), lambda i: (i, 0)),
           pl.BlockSpec((tm, LANES), lambda i: (i, 0))],
        out_shape=[jax.ShapeDtypeStruct((T, 3 * ATTN_W), BF16)]
        + [jax.ShapeDtypeStruct((B, d, S // d, 3 * ATTN_W), BF16) for d in DILATIONS[1:]]
        + [jax.ShapeDtypeStruct((T, DN_CONV), BF16),
           jax.ShapeDtypeStruct((T, DN_W), BF16),
           jax.ShapeDtypeStruct((T, LANES), F32)],
        scratch_shapes=[pltpu.VMEM((ATTN_W // LANES, tm, LANES), F32)],
        compiler_params=_cparams(("arbitrary",)),
        name="inproj",
    )(x2, mod, norm_w.reshape(1, D), wm, ws)


def _attn_kernel(q_ref, kc_ref, kp_ref, vc_ref, vp_ref, o_ref, lse_ref, kf, vf, *, qb):
    n = pl.program_id(2)
    kf[0:ATTN_BLK, :] = kp_ref[0, 0]
    kf[ATTN_BLK:, :] = kc_ref[0, 0]
    vf[0:ATTN_BLK, :] = vp_ref[0, 0]
    vf[ATTN_BLK:, :] = vc_ref[0, 0]
    row = lax.broadcasted_iota(I32, (ATTN_BLK, 2 * ATTN_BLK), 0)
    col = lax.broadcasted_iota(I32, (ATTN_BLK, 2 * ATTN_BLK), 1)
    band = jnp.logical_or(jnp.logical_and(col < ATTN_BLK, col >= row),
                          jnp.logical_and(col >= ATTN_BLK, col - ATTN_BLK <= row))
    lane = lax.broadcasted_iota(I32, (ATTN_BLK, LANES), 1)
    lo = lane < ATTN_HD

    def sub(j, carry):
        r0 = pl.multiple_of(j * ATTN_BLK, ATTN_BLK)
        first_col = jnp.where(jnp.logical_and(n == 0, j == 0), ATTN_BLK, 0)
        mask = jnp.logical_and(band, col >= first_col)
        npair = ATTN_W // LANES
        cols = [slice(hp * LANES, (hp + 1) * LANES) for hp in range(npair)]
        heads = [(hp, half) for hp in range(npair) for half in range(2)]
        scores = []
        for hp, half in heads:
            q2 = q_ref[0, 0, pl.ds(r0, ATTN_BLK), cols[hp]]
            qm = jnp.where(lo if half == 0 else jnp.logical_not(lo), q2, jnp.zeros_like(q2))
            scores.append(jnp.where(mask, _nt(qm, kf[pl.ds(r0, 2 * ATTN_BLK), cols[hp]]), NEG))
        maxes = [jnp.max(s, axis=-1, keepdims=True) for s in scores]
        probs = [jnp.exp(s - m) for s, m in zip(scores, maxes)]
        dens = [jnp.sum(p, axis=-1, keepdims=True) for p in probs]
        accs = [jnp.dot(p.astype(BF16), vf[pl.ds(r0, 2 * ATTN_BLK), cols[hp]], preferred_element_type=F32)
                for p, (hp, _) in zip(probs, heads)]
        outs = [a / d for a, d in zip(accs, dens)]
        lse_tile = jnp.zeros((ATTN_BLK, LANES), F32)
        for h, (m, d) in enumerate(zip(maxes, dens)):
            lse_tile = jnp.where(lane == h, m + jnp.log(d), lse_tile)
        for hp in range(npair):
            o_ref[0, 0, pl.ds(r0, ATTN_BLK), cols[hp]] = jnp.where(lo, outs[2 * hp], outs[2 * hp + 1]).astype(BF16)
        lse_ref[0, 0, pl.ds(r0, ATTN_BLK), :] = lse_tile
        return carry

    lax.fori_loop(0, qb // ATTN_BLK, sub, 0)


def _attn_branch(qkv, d):
    B, _, L, _ = qkv.shape
    qb = min(512, L)
    nsub = qb // ATTN_BLK
    cur = lambda c: pl.BlockSpec((1, 1, qb, ATTN_W), lambda b, r, n: (b, r, n, c))
    prev = lambda c: pl.BlockSpec((1, 1, ATTN_BLK, ATTN_W),
                                  lambda b, r, n: (b, r, jnp.maximum(n * nsub - 1, 0), c))
    return pl.pallas_call(
        functools.partial(_attn_kernel, qb=qb),
        grid=(B, d, L // qb),
        in_specs=[cur(0), cur(1), prev(1), cur(2), prev(2)],
        out_specs=[pl.BlockSpec((1, 1, qb, ATTN_W), lambda b, r, n: (b, r, n, 0)),
                   pl.BlockSpec((1, 1, qb, LANES), lambda b, r, n: (b, r, n, 0))],
        out_shape=[jax.ShapeDtypeStruct((B, d, L, ATTN_W), BF16),
                   jax.ShapeDtypeStruct((B, d, L, LANES), F32)],
        scratch_shapes=[pltpu.VMEM((qb + ATTN_BLK, ATTN_W), BF16),
                        pltpu.VMEM((qb + ATTN_BLK, ATTN_W), BF16)],
        compiler_params=_cparams(("arbitrary", "arbitrary", "arbitrary")),
        name=f"attn_d{d}",
    )(qkv, qkv, qkv, qkv, qkv)


def _gdn_kernel(x_ref, z_ref, g_ref, cw_ref, prm_ref, nw_ref, o_ref, xext, yc, s0, s1, *, rb):
    i = pl.program_id(1)

    @pl.when(i == 0)
    def _():
        xext[0:8, :] = jnp.zeros((8, DN_CONV), F32)
        s0[...] = jnp.zeros_like(s0)
        s1[...] = jnp.zeros_like(s1)

    @pl.when(i > 0)
    def _():
        xext[0:8, :] = xext[rb:rb + 8, :]

    xext[8:, :] = x_ref[0].astype(F32)
    y = cw_ref[CONV_K - 1:CONV_K, :] * xext[8:8 + rb, :]
    for j in range(CONV_K - 1):
        off = 8 - (CONV_K - 1) + j
        y = y + cw_ref[j:j + 1, :] * xext[off:off + rb, :]
    yc[...] = y * _sigmoid(y)

    C = DN_CHUNK
    H = DN_HEADS
    CW = H * C
    dot = functools.partial(jnp.dot, preferred_element_type=F32)

    def iota(shape, d):
        return lax.broadcasted_iota(I32, shape, d)

    ltri_b = jnp.where(iota((C, C), 0) >= iota((C, C), 1), 1.0, 0.0).astype(BF16)
    lane = iota((C, LANES), 1)
    blane = lane < H
    glane = jnp.logical_and(lane >= H, lane < 2 * H)
    e512 = jnp.where(jnp.logical_or(iota((LANES, DN_W), 1) // DN_D == iota((LANES, DN_W), 0),
                                    iota((LANES, DN_W), 1) // DN_D == iota((LANES, DN_W), 0) - H),
                     1.0, 0.0).astype(BF16)
    e256 = jnp.where(iota((LANES, CW), 1) // C == iota((LANES, CW), 0) - H, 1.0, 0.0).astype(BF16)
    row4 = iota((C, CW), 0)
    col4 = iota((C, CW), 1) % C
    eye4 = jnp.where(row4 == col4, 1.0, 0.0).astype(F32)
    blk = [iota((C, CW), 1) // C == h for h in range(H)]
    bd_cc = iota((CW, CW), 0) // C == iota((CW, CW), 1) // C
    bd_pair = iota((CW, CW), 0) // DN_D == iota((CW, CW), 1) // DN_D
    rt_mask = iota((CW, DN_W), 0) // C == iota((CW, DN_W), 1) // DN_D
    neg_a = -jnp.exp(prm_ref[0:1, :])
    dtb = prm_ref[1:2, :]
    nw = nw_ref[...]

    def hilo(x):
        hi = x.astype(BF16)
        return hi, (x - hi.astype(F32)).astype(BF16)

    def heads(a, w):
        return [a[:, h * w:(h + 1) * w] for h in range(H)]

    def l2n(a, mult):
        return jnp.concatenate(
            [p * (lax.rsqrt(jnp.sum(p * p, axis=-1, keepdims=True) + EPS) * mult) for p in heads(a, DN_D)],
            axis=1)

    def stack4(a):
        return jnp.concatenate([a, a, a, a], axis=0)

    zb = jnp.zeros((), BF16)
    nchunk = rb // C
    chunks = []
    for c in range(nchunk):
        rs = slice(c * C, (c + 1) * C)
        G = g_ref[0, rs, :]
        xg = G + dtb
        gv = jnp.where(glane, neg_a * (jnp.maximum(xg, 0.0) + jnp.log1p(jnp.exp(-jnp.abs(xg)))), 0.0)
        be = jnp.where(blane, _sigmoid(G), 0.0)
        g_hi, g_lo = hilo(gv)
        gcum = dot(ltri_b, g_hi) + dot(ltri_b, g_lo)
        gtot = gcum[C - 1:C, :]
        eg = jnp.where(glane, jnp.exp(gcum), 0.0)
        ek = jnp.where(glane, jnp.exp(gtot - gcum), 0.0)
        ex = dot(jnp.concatenate([be, eg, ek], axis=0).astype(BF16), e512)
        bexp, egexp, ekexp = ex[0:C], ex[C:2 * C], ex[2 * C:3 * C]
        gexp = dot(g_hi, e256) + dot(g_lo, e256)
        d_hi, d_lo = hilo(jnp.where(row4 > col4, gexp, 0.0))
        diff = dot(ltri_b, d_hi) + dot(ltri_b, d_lo)
        decay = jnp.exp(jnp.where(row4 >= col4, diff, NEG))

        q4 = l2n(yc[rs, 0:DN_W], DN_D ** -0.5)
        k4 = l2n(yc[rs, DN_W:2 * DN_W], 1.0)
        v4 = yc[rs, 2 * DN_W:3 * DN_W]
        kb4 = k4 * bexp
        vb4 = v4 * bexp
        rt = jnp.where(rt_mask, stack4(k4.astype(BF16)), jnp.zeros((), BF16))
        ai = _nt(jnp.concatenate([kb4, q4], axis=0).astype(BF16), rt)
        a4 = jnp.where(row4 > col4, ai[0:C] * decay, 0.0)
        pb = a4.astype(BF16)
        chunks.append(dict(
            rs=rs, q4=q4, k4=k4, kb4=kb4, vb4=vb4, egexp=egexp, ekexp=ekexp,
            intra=ai[C:2 * C] * decay, t4=eye4 - a4, pb=pb,
            bd=jnp.where(bd_cc, stack4(pb), zb)))

    for _ in range(5):
        for ch in chunks:
            ch["pb"] = dot(ch["pb"], ch["bd"]).astype(BF16)
        for ch in chunks:
            ch["bd"] = jnp.where(bd_cc, stack4(ch["pb"]), zb)
        for ch in chunks:
            ch["t4"] = ch["t4"] + dot(ch["t4"].astype(BF16), ch["bd"])

    for ch in chunks:
        q4, k4, kb4, vb4, egexp, ekexp = (ch[n] for n in ("q4", "k4", "kb4", "vb4", "egexp", "ekexp"))
        t4b = ch["t4"].astype(BF16)
        lstk = jnp.concatenate([jnp.where(blk[h], t4b, zb) for h in range(H)], axis=0)
        kbg4 = kb4 * egexp
        rstk = jnp.concatenate(
            [jnp.concatenate([vb, kbg], axis=1) for vb, kbg in zip(heads(vb4, DN_D), heads(kbg4, DN_D))],
            axis=0).astype(BF16)
        uw = dot(lstk, rstk)
        u4 = jnp.concatenate([uw[h * C:(h + 1) * C, 0:DN_D] for h in range(H)], axis=1)
        w4 = jnp.concatenate([uw[h * C:(h + 1) * C, DN_D:2 * DN_D] for h in range(H)], axis=1)
        ib = ch["intra"].astype(BF16)
        ch.update(
            u4=u4, wq=jnp.concatenate([w4, q4 * egexp], axis=0).astype(BF16),
            kd4=(k4 * ekexp).astype(BF16), gl4=egexp[C - 1:C, :],
            lint=jnp.concatenate([jnp.where(blk[h], ib, zb) for h in range(H)], axis=0))

    for ch in chunks:
        rs, u4, wq, kd4, gl4, lint = (ch[n] for n in ("rs", "u4", "wq", "kd4", "gl4", "lint"))
        ra = dot(wq[:, 0:CW], s0[...].astype(BF16))
        rc = dot(wq[:, CW:2 * CW], s1[...].astype(BF16))
        vn = u4 - jnp.concatenate([ra[0:C], rc[0:C]], axis=1)
        vnb = vn.astype(BF16)
        oi = dot(lint, jnp.concatenate(heads(vnb, DN_D), axis=0))
        o = (jnp.concatenate([ra[C:2 * C], rc[C:2 * C]], axis=1)
             + jnp.concatenate([oi[h * C:(h + 1) * C] for h in range(H)], axis=1))
        tn = (((0,), (0,)), ((), ()))
        s0[...] = s0[...] * gl4[:, 0:CW] + jnp.where(
            bd_pair, lax.dot_general(kd4[:, 0:CW], vnb[:, 0:CW], tn, preferred_element_type=F32), 0.0)
        s1[...] = s1[...] * gl4[:, CW:2 * CW] + jnp.where(
            bd_pair, lax.dot_general(kd4[:, CW:2 * CW], vnb[:, CW:2 * CW], tn, preferred_element_type=F32), 0.0)

        z = z_ref[0, rs, :].astype(F32)
        on = jnp.concatenate(
            [p * lax.rsqrt(jnp.mean(p * p, axis=-1, keepdims=True) + EPS) * nw for p in heads(o, DN_D)], axis=1)
        o_ref[0, rs, :] = (on * (z * _sigmoid(z))).astype(BF16)


def _gdn(dqkv, dz, gbc, conv_w, A_log, dt_bias, dn_norm_w, B, S):
    rb = 512
    prm = jnp.zeros((2, LANES), F32)
    prm = prm.at[0, DN_HEADS:2 * DN_HEADS].set(A_log.astype(F32))
    prm = prm.at[1, DN_HEADS:2 * DN_HEADS].set(dt_bias.astype(F32))
    out = pl.pallas_call(
        functools.partial(_gdn_kernel, rb=rb),
        grid=(B, S // rb),
        in_specs=[pl.BlockSpec((1, rb, DN_CONV), lambda b, i: (b, i, 0)),
                  pl.BlockSpec((1, rb, DN_W), lambda b, i: (b, i, 0)),
                  pl.BlockSpec((1, rb, LANES), lambda b, i: (b, i, 0)),
                  pl.BlockSpec((CONV_K, DN_CONV), lambda b, i: (0, 0)),
                  pl.BlockSpec((2, LANES), lambda b, i: (0, 0)),
                  pl.BlockSpec((1, DN_D), lambda b, i: (0, 0))],
        out_specs=pl.BlockSpec((1, rb, DN_W), lambda b, i: (b, i, 0)),
        out_shape=jax.ShapeDtypeStruct((B, S, DN_W), BF16),
        scratch_shapes=[pltpu.VMEM((rb + 8, DN_CONV), F32),
                        pltpu.VMEM((rb, DN_CONV), F32),
                        pltpu.VMEM((2 * DN_D, 2 * DN_D), F32),
                        pltpu.VMEM((2 * DN_D, 2 * DN_D), F32)],
        compiler_params=_cparams(("arbitrary", "arbitrary")),
        name="gdn",
    )(dqkv.reshape(B, S, DN_CONV), dz.reshape(B, S, DN_W), gbc.reshape(B, S, LANES),
      conv_w, prm, dn_norm_w.reshape(1, DN_D))
    return out.reshape(B * S, DN_W)


def _out_kernel(o1_ref, o2_ref, o3_ref, l1_ref, l2_ref, l3_ref, dn_ref, x_ref, mod_ref, wo_ref,
                n2_ref, wr_ref, br_ref,
                x1_ref, h2_ref, te_ref, rk_ref, gtc_ref, cnt_ref, base, scr, *, tm):
    i = pl.program_id(0)

    @pl.when(i == 0)
    def _():
        base[...] = jnp.zeros_like(base)

    def natural(ref, d):
        if d == 1:
            return ref[0, 0].astype(F32)
        nl = ref.shape[-1] // LANES
        for res in range(d):
            blk = ref[0, res].astype(F32)
            for c in range(nl):
                scr[c, pl.ds(res, tm // d, stride=d), :] = blk[:, c * LANES:(c + 1) * LANES]
        return jnp.concatenate([scr[c] for c in range(nl)], axis=1)

    l1, l2, l3 = (natural(r, d) for r, d in zip((l1_ref, l2_ref, l3_ref), DILATIONS))
    mx = jnp.maximum(jnp.maximum(l1, l2), l3)
    e1, e2, e3 = jnp.exp(l1 - mx), jnp.exp(l2 - mx), jnp.exp(l3 - mx)
    zs = e1 + e2 + e3
    er = lax.broadcasted_iota(I32, (LANES, ATTN_W), 0)
    ec = lax.broadcasted_iota(I32, (LANES, ATTN_W), 1)
    expand = jnp.where(ec // ATTN_HD == er, 1.0, 0.0).astype(BF16)
    attn = jnp.zeros((tm, ATTN_W), F32)
    for e, o_ref, d in zip((e1, e2, e3), (o1_ref, o2_ref, o3_ref), DILATIONS):
        wgt = jnp.dot((e / zs).astype(BF16), expand, preferred_element_type=F32)
        attn = attn + wgt * natural(o_ref, d)
    mix = (jnp.dot(attn.astype(BF16), wo_ref[0:ATTN_W, :], preferred_element_type=F32)
           + jnp.dot(dn_ref[...], wo_ref[ATTN_W:, :], preferred_element_type=F32))
    x1 = x_ref[...] + mod_ref[0, 2:3, :] * mix
    x1_ref[...] = x1
    ms = jnp.mean(x1 * x1, axis=-1, keepdims=True)
    h2 = x1 * lax.rsqrt(ms + EPS) * n2_ref[...]
    h2 = h2 * (1.0 + mod_ref[0, 4:5, :]) + mod_ref[0, 3:4, :]
    _store_packed(h2_ref, h2)

    lg = _nt(wr_ref[...], h2, precision=HI) + br_ref[...]
    eidx = lax.broadcasted_iota(I32, (N_EXPERTS, tm), 0)
    vals, idxs, sels = [], [], []
    for _ in range(TOP_K):
        m = jnp.max(lg, axis=0, keepdims=True)
        idx = jnp.min(jnp.where(lg == m, eidx, N_EXPERTS), axis=0, keepdims=True)
        sel = eidx == idx
        vals.append(m)
        idxs.append(idx)
        sels.append(sel)
        lg = jnp.where(sel, -jnp.inf, lg)
    ex = [jnp.exp(v - vals[0]) for v in vals]
    den = ex[0] + ex[1] + ex[2] + ex[3]
    gates = [e / den for e in ex]

    msum = jnp.zeros((N_EXPERTS, tm), F32)
    for sel in sels:
        msum = msum + jnp.where(sel, 1.0, 0.0)
    tr = lax.broadcasted_iota(I32, (tm, tm), 0)
    tc = lax.broadcasted_iota(I32, (tm, tm), 1)
    upper = jnp.where(tr <= tc, 1.0, 0.0).astype(BF16)
    incl = jnp.dot(msum.astype(BF16), upper, preferred_element_type=F32)
    pos = base[:, 0:1] + (incl - msum)
    sub8 = lax.broadcasted_iota(I32, (8, tm), 0)
    te = jnp.zeros((8, tm), I32)
    rk = jnp.zeros((8, tm), I32)
    gt = jnp.zeros((8, tm), F32)
    for k in range(TOP_K):
        rank_k = jnp.sum(jnp.where(sels[k], pos, 0.0), axis=0, keepdims=True).astype(I32)
        te = jnp.where(sub8 == k, idxs[k], te)
        rk = jnp.where(sub8 == k, rank_k, rk)
        gt = jnp.where(sub8 == k, gates[k], gt)
    te_ref[...] = te
    rk_ref[...] = rk
    gtc_ref[...] = jnp.transpose(jnp.concatenate([gt, jnp.zeros((LANES - 8, tm), F32)], axis=0))
    base[...] = base[...] + jnp.sum(msum, axis=1, keepdims=True)
    cnt_ref[...] = base[...].astype(I32)


def _outproj(o1, o2, o3, l1, l2, l3, dn, x2, mod, w_out, norm2_w, w_router, b_router, S):
    T, D = x2.shape
    tm = 512
    nt = S // tm
    row = lambda w: pl.BlockSpec((tm, w), lambda i: (i, 0))
    res = lambda d, w: pl.BlockSpec((1, d, tm // d, w), lambda i: (i // nt, 0, i % nt, 0))
    colb = pl.BlockSpec((8, tm), lambda i: (0, i))
    return pl.pallas_call(
        functools.partial(_out_kernel, tm=tm),
        grid=(T // tm,),
        in_specs=[res(d, ATTN_W) for d in DILATIONS] + [res(d, LANES) for d in DILATIONS]
        + [row(DN_W), row(D),
                  pl.BlockSpec((1, 6, D), lambda i: (i * tm // S, 0, 0)),
                  pl.BlockSpec((D, D), lambda i: (0, 0)),
                  pl.BlockSpec((1, D), lambda i: (0, 0)),
                  pl.BlockSpec((N_EXPERTS, D), lambda i: (0, 0)),
                  pl.BlockSpec((N_EXPERTS, 1), lambda i: (0, 0))],
        out_specs=[row(D), pl.BlockSpec((tm, D // (2 * LANES), LANES), lambda i: (i, 0, 0)), colb, colb,
                   row(LANES), pl.BlockSpec((N_EXPERTS, LANES), lambda i: (0, 0))],
        out_shape=[jax.ShapeDtypeStruct((T, D), F32),
                   jax.ShapeDtypeStruct((T, D // (2 * LANES), LANES), U32),
                   jax.ShapeDtypeStruct((8, T), I32),
                   jax.ShapeDtypeStruct((8, T), I32),
                   jax.ShapeDtypeStruct((T, LANES), F32),
                   jax.ShapeDtypeStruct((N_EXPERTS, LANES), I32)],
        scratch_shapes=[pltpu.VMEM((N_EXPERTS, LANES), F32),
                        pltpu.VMEM((ATTN_W // LANES, tm, LANES), F32)],
        compiler_params=_cparams(("arbitrary",)),
        name="outproj_router",
    )(o1, o2, o3, l1, l2, l3, dn, x2, mod, w_out.astype(BF16), norm2_w.reshape(1, D),
      jnp.transpose(w_router), b_router.reshape(N_EXPERTS, 1))


def _dest_kernel(ps_ref, te_ref, rk_ref, d_ref):
    te = te_ref[...]
    acc = jnp.zeros(te.shape, I32)
    for e in range(N_EXPERTS):
        acc = jnp.where(te == e, ps_ref[e], acc)
    d_ref[...] = acc + rk_ref[...]


def _dest(pstart, te, rk):
    T = te.shape[1]
    tb = 2048
    return pl.pallas_call(
        _dest_kernel,
        grid_spec=pltpu.PrefetchScalarGridSpec(
            num_scalar_prefetch=1,
            grid=(T // tb,),
            in_specs=[pl.BlockSpec((8, tb), lambda i, ps: (0, i)),
                      pl.BlockSpec((8, tb), lambda i, ps: (0, i))],
            out_specs=pl.BlockSpec((8, tb), lambda i, ps: (0, i))),
        out_shape=jax.ShapeDtypeStruct((8, T), I32),
        compiler_params=_cparams(("arbitrary",)),
        name="dest_rows",
    )(pstart, te, rk)


SC_CORES = 2
SC_SUBCORES = 16
SC_IDX_CHUNK = 128


def _invperm(dest_flat, P):
    N = dest_flat.shape[0]
    nw = SC_CORES * SC_SUBCORES
    nch = N // (nw * SC_IDX_CHUNK)
    mesh = plsc.VectorSubcoreMesh(core_axis_name="c", subcore_axis_name="s",
                                  num_cores=SC_CORES, num_subcores=SC_SUBCORES)

    @functools.partial(
        pl.kernel, mesh=mesh, out_type=jax.ShapeDtypeStruct((P,), I32),
        scratch_types=[pltpu.VMEM((nch, SC_IDX_CHUNK), I32), pltpu.VMEM((nch, SC_IDX_CHUNK), I32),
                       pltpu.SemaphoreType.DMA])
    def scatter_codes(idx_hbm, val_hbm, out_hbm, idx_v, val_v, sem):
        wid = lax.axis_index("s") * SC_CORES + lax.axis_index("c")
        pltpu.sync_copy(idx_hbm.at[wid], idx_v)
        pltpu.sync_copy(val_hbm.at[wid], val_v)

        @pl.loop(0, nch)
        def _(j):
            pltpu.async_copy(val_v.at[j], out_hbm.at[idx_v.at[j]], sem)

        @pl.loop(0, nch)
        def _(j):
            pltpu.make_async_copy(val_v.at[j], out_hbm.at[idx_v.at[j]], sem).wait()

    vals = jnp.arange(N, dtype=I32)
    return scatter_codes(dest_flat.reshape(nw, nch, SC_IDX_CHUNK), vals.reshape(nw, nch, SC_IDX_CHUNK))


def _row_copy(src, dst, sem):
    return pltpu.make_async_copy(src, dst, sem)


def _moe_kernel(be_ref, nv_ref, cc_ref, cn_ref, h2_ref, w1_ref, b1_ref, w2_ref, b2_ref, y4_ref,
                xbuf, ybuf, w1b, w2b, gsem, ssem, *, F, T, D, tme, nb):
    i = pl.program_id(0)
    s = i % 2

    def rows(first, n):
        return pl.ds(first, n)

    def gather_copy(tok, p, slot):
        return _row_copy(h2_ref.at[rows(tok, 1)], xbuf.at[slot, rows(p, 1)], gsem.at[slot])

    def token_of(code):
        return code & (T - 1) if T & (T - 1) == 0 else code % T

    def issue_rows(start_row, nv):
        ng = nv // ROW_UNROLL

        def grp(g, c):
            for j in range(ROW_UNROLL):
                start_row(g * ROW_UNROLL + j, j % 2)
            return c
        lax.fori_loop(0, ng, grp, 0)

        def one(p, c):
            start_row(p, 0)
            return c
        lax.fori_loop(ng * ROW_UNROLL, nv, one, 0)

    def start_gather(code_ref, nv, slot):
        issue_rows(lambda p, pri: gather_copy(token_of(code_ref[0, 0, p]), p, slot).start(priority=pri), nv)

    def wait_rows(make, nv):
        @pl.when(nv > 0)
        def _():
            make(nv).wait()

    def gathered(n, slot):
        return _row_copy(h2_ref.at[rows(0, n)], xbuf.at[slot, rows(0, n)], gsem.at[slot])

    def scattered(n, slot):
        return _row_copy(ybuf.at[slot, rows(0, n)], y4_ref.at[rows(0, n)], ssem.at[slot])

    @pl.when(i == 0)
    def _():
        start_gather(cc_ref, tme, 0)

    nv = nv_ref[i]
    gathered(tme, s).wait()

    @pl.when(i >= 2)
    def _():
        wait_rows(lambda n: scattered(n, s), nv_ref[i - 2])

    @pl.when(jnp.logical_or(i == 0, be_ref[i] != be_ref[jnp.maximum(i - 1, 0)]))
    def _():
        w1b[...] = w1_ref[0].astype(BF16)
        w2b[...] = w2_ref[0].astype(BF16)

    @pl.when(nv > 0)
    def _():
        x = _load_packed(xbuf, lead=(s,)).astype(BF16)
        for p in range(tme):
            gather_copy(token_of(cn_ref[0, 0, p]), p, 1 - s).start(priority=p % 2)
        hgu = jnp.dot(x, w1b[...], preferred_element_type=F32) + b1_ref[0]
        gate = jnp.minimum(hgu[:, :F], SWIGLU_LIMIT)
        up = jnp.clip(hgu[:, F:], -SWIGLU_LIMIT, SWIGLU_LIMIT)
        act = gate * _sigmoid(SWIGLU_ALPHA * gate) * (up + 1.0)
        y = jnp.dot(act.astype(BF16), w2b[...], preferred_element_type=F32) + b2_ref[0]
        _store_packed(ybuf, y, lead=(s,))

        issue_rows(lambda p, pri: _row_copy(ybuf.at[s, rows(p, 1)], y4_ref.at[rows(cc_ref[0, 0, p], 1)],
                                            ssem.at[s]).start(priority=pri), nv)

    @pl.when(jnp.logical_and(nv == 0, i + 1 < nb))
    def _():
        start_gather(cn_ref, tme, 1 - s)

    @pl.when(i == nb - 1)
    def _():
        @pl.when(nv > 0)
        def _():
            gathered(tme, 1 - s).wait()
        wait_rows(lambda n: scattered(n, s), nv)
        if nb > 1:
            wait_rows(lambda n: scattered(n, 1 - s), nv_ref[i - 1])


def _experts(blk_exp, blk_valid, codes, h2s, w1, b1, w2, b2, tme):
    E, D, F2 = w1.shape
    F = F2 // 2
    T, nc, _ = h2s.shape
    nb = blk_exp.shape[0]
    codes3 = codes.reshape(nb, 1, tme)
    wspec = lambda shape: pl.BlockSpec(shape, lambda i, be, nv: (be[i], 0, 0))
    cspec = lambda off: pl.BlockSpec((1, 1, tme), lambda i, be, nv: (jnp.minimum(i + off, nb - 1), 0, 0),
                                     memory_space=pltpu.SMEM)
    return pl.pallas_call(
        functools.partial(_moe_kernel, F=F, T=T, D=D, tme=tme, nb=nb),
        grid_spec=pltpu.PrefetchScalarGridSpec(
            num_scalar_prefetch=2,
            grid=(nb,),
            in_specs=[cspec(0), cspec(1),
                      pl.BlockSpec(memory_space=pl.ANY),
                      wspec((1, D, F2)), wspec((1, 1, F2)), wspec((1, F, D)), wspec((1, 1, D))],
            out_specs=pl.BlockSpec(memory_space=pl.ANY),
            scratch_shapes=[pltpu.VMEM((2, tme, nc, LANES), U32), pltpu.VMEM((2, tme, nc, LANES), U32),
                            pltpu.VMEM((D, F2), BF16), pltpu.VMEM((F, D), BF16),
                            pltpu.SemaphoreType.DMA((2,)), pltpu.SemaphoreType.DMA((2,))]),
        out_shape=jax.ShapeDtypeStruct((TOP_K * T, nc, LANES), U32),
        compiler_params=_cparams(("arbitrary",)),
        name="experts",
    )(blk_exp, blk_valid, codes3, codes3, h2s, w1, b1.reshape(E, 1, F2), w2, b2.reshape(E, 1, D))


def _comb_kernel(g_ref, x1_ref, mod_ref, fw_ref, y0_ref, y1_ref, y2_ref, y3_ref, o_ref):
    g = g_ref[...]
    y = g[:, 0:1] * _load_packed(y0_ref)
    for k, y_ref in ((1, y1_ref), (2, y2_ref), (3, y3_ref)):
        y = y + g[:, k:k + 1] * _load_packed(y_ref)
    x2 = x1_ref[...] + mod_ref[0, 5:6, :] * y
    ms = jnp.mean(x2 * x2, axis=-1, keepdims=True)
    o_ref[...] = x2 * lax.rsqrt(ms + EPS) * fw_ref[...]


def _combine(gtc, x1, mod, final_w, y4, S):
    T, D = x1.shape
    tmc = 512
    nt = T // tmc
    yspec = lambda k: pl.BlockSpec((tmc,) + y4.shape[1:], lambda i: (k * nt + i, 0, 0))
    return pl.pallas_call(
        _comb_kernel,
        grid=(nt,),
        in_specs=[pl.BlockSpec((tmc, LANES), lambda i: (i, 0)),
                  pl.BlockSpec((tmc, D), lambda i: (i, 0)),
                  pl.BlockSpec((1, 6, D), lambda i: (i * tmc // S, 0, 0)),
                  pl.BlockSpec((1, D), lambda i: (0, 0)),
                  yspec(0), yspec(1), yspec(2), yspec(3)],
        out_specs=pl.BlockSpec((tmc, D), lambda i: (i, 0)),
        out_shape=jax.ShapeDtypeStruct((T, D), F32),
        compiler_params=_cparams(("arbitrary",)),
        name="combine",
    )(gtc, x1, mod, final_w.reshape(1, D), y4, y4, y4, y4)


def _layer(x2, mod, B, S, norm1_w, w_in, conv_w, A_log, dt_bias, dn_norm_w, w_out,
           norm2_w, w_router, b_router, w1, b1, w2, b2):
    T, D = x2.shape
    qkv, qkv4, qkv16, dqkv, dz, gbc = _inproj(x2, mod, norm1_w, w_in, S)
    views = (qkv.reshape(B, 1, S, 3 * ATTN_W), qkv4, qkv16)
    branches = [_attn_branch(v, d) for v, d in zip(views, DILATIONS)]
    dn = _gdn(dqkv, dz, gbc, conv_w, A_log, dt_bias, dn_norm_w, B, S)
    (o1, l1), (o2, l2), (o3, l3) = branches
    x1, h2, te, rk, gtc, cnt = _outproj(o1, o2, o3, l1, l2, l3, dn, x2, mod, w_out, norm2_w,
                                        w_router, b_router, S)
    tme = MOE_ROWS
    P = T * TOP_K + N_EXPERTS * tme
    counts = cnt[:, 0]
    padded = (counts + tme - 1) // tme * tme
    pend = jnp.cumsum(padded)
    pstart = (pend - padded).astype(I32)
    blk_start = jnp.arange(P // tme, dtype=I32) * tme
    blk_exp = jnp.minimum(jnp.sum((pend[None, :] <= blk_start[:, None]).astype(I32), axis=1),
                          N_EXPERTS - 1).astype(I32)
    blk_valid = jnp.clip(pstart[blk_exp] + counts[blk_exp] - blk_start, 0, tme).astype(I32)
    dest = _dest(pstart, te, rk)
    codes = _invperm(dest[:TOP_K].reshape(TOP_K * T), P)
    y4 = _experts(blk_exp, blk_valid, codes, h2, w1, b1, w2, b2, tme)
    return x1, gtc, y4


def kernel(x, c, w_ada, b_ada, norm1_w, w_in, conv_w, A_log, dt_bias, dn_norm_w, w_out, norm2_w,
           w_router, b_router, w1, b1, w2, b2, final_norm_w):
    B, S, D = x.shape
    depth = w_ada.shape[0]
    assert S % (ATTN_BLK * DILATIONS[-1]) == 0 and depth == 1
    x2 = x.reshape(B * S, D)
    mod = _ada(c, w_ada[0], b_ada[0])
    x1, gtc, y4 = _layer(x2, mod, B, S, norm1_w[0], w_in[0], conv_w[0], A_log[0], dt_bias[0],
                         dn_norm_w[0], w_out[0], norm2_w[0], w_router[0], b_router[0],
                         w1[0], b1[0], w2[0], b2[0])
    out = _combine(gtc, x1, mod, final_norm_w, y4, S)
    return out.reshape(B, S, D)
```

```python
import functools

import jax
import jax.numpy as jnp
from jax import lax
from jax.experimental import pallas as pl
from jax.experimental.pallas import tpu as pltpu
from jax.experimental.pallas import tpu_sc as plsc

F32 = jnp.float32
BF16 = jnp.bfloat16
I32 = jnp.int32
HI = lax.Precision.HIGHEST

LANES = 128
ATTN_HEADS = 8
ATTN_HD = 64
ATTN_W = ATTN_HEADS * ATTN_HD
ATTN_BLK = 128
DILATIONS = (1, 4, 16)
DN_HEADS = 4
DN_D = 128
DN_W = DN_HEADS * DN_D
DN_CONV = 3 * DN_W
CONV_K = 4
DN_CHUNK = 64
N_EXPERTS = 32
TOP_K = 4
MOE_ROWS = 512
ROW_UNROLL = 8
SWIGLU_LIMIT = 7.0
SWIGLU_ALPHA = 1.702
EPS = 1e-6
NEG = -1e30
MAIN_COLS = 3 * ATTN_W + DN_CONV + DN_W

VMEM_LIMIT = 56 * 1024 * 1024


def _cparams(sem):
    return pltpu.CompilerParams(dimension_semantics=sem, vmem_limit_bytes=VMEM_LIMIT)


def _nt(a, b, **kw):
    return lax.dot_general(a, b, (((1,), (1,)), ((), ())), preferred_element_type=F32, **kw)


def _sigmoid(x):
    return 1.0 / (1.0 + jnp.exp(-x))


def _store_slabs(ref, val, lead=()):
    n, d = val.shape
    ns = d // LANES
    for c in range(ns):
        ref[lead + (pl.ds(c, n, stride=ns), slice(None))] = val[:, c * LANES:(c + 1) * LANES]


def _load_slabs(ref, n, d, lead=()):
    ns = d // LANES
    return jnp.concatenate([ref[lead + (pl.ds(c, n, stride=ns), slice(None))] for c in range(ns)], axis=1)


def _ada_kernel(c_ref, w_ref, b_ref, o_ref):
    c = c_ref[...]
    cond = c * _sigmoid(c)
    o_ref[...] = jnp.dot(cond, w_ref[...], preferred_element_type=F32, precision=HI) + b_ref[...]


def _ada(c, w_ada, b_ada):
    B, D = c.shape
    N = w_ada.shape[1]
    cp = jnp.zeros((8, D), F32).at[:B].set(c)
    tn = 1024
    out = pl.pallas_call(
        _ada_kernel,
        grid=(N // tn,),
        in_specs=[pl.BlockSpec((8, D), lambda j: (0, 0)),
                  pl.BlockSpec((D, tn), lambda j: (0, j)),
                  pl.BlockSpec((1, tn), lambda j: (0, j))],
        out_specs=pl.BlockSpec((8, tn), lambda j: (0, j)),
        out_shape=jax.ShapeDtypeStruct((8, N), F32),
        compiler_params=_cparams(("arbitrary",)),
        name="ada",
    )(cp, w_ada, b_ada.reshape(1, N))
    return out[:B].reshape(B, 6, D)


def _inproj_kernel(x_ref, mod_ref, nw_ref, wm_ref, ws_ref, qkv_ref, qkv4_ref, qkv16_ref, dqkv_ref, dz_ref,
                   gbc_ref, scr):
    x = x_ref[...]
    tm = x.shape[0]
    shift = mod_ref[0, 0:1, :]
    scale = mod_ref[0, 1:2, :]
    ms = jnp.mean(x * x, axis=-1, keepdims=True)
    h = x * lax.rsqrt(ms + EPS) * nw_ref[...]
    hb = (h * (1.0 + scale) + shift).astype(BF16)
    nl = ATTN_W // LANES
    for j in range(3):
        cs = slice(j * ATTN_W, (j + 1) * ATTN_W)
        r = jnp.dot(hb, wm_ref[:, cs], preferred_element_type=F32)
        if j == 0:
            r = r * (ATTN_HD ** -0.5)
        qkv_ref[:, cs] = r.astype(BF16)
        for c in range(nl):
            scr[c] = r[:, c * LANES:(c + 1) * LANES]
        for d, ref in ((DILATIONS[1], qkv4_ref), (DILATIONS[2], qkv16_ref)):
            for res in range(d):
                part = jnp.concatenate([scr[c, pl.ds(res, tm // d, stride=d), :] for c in range(nl)], axis=1)
                ref[0, res, :, cs] = part.astype(BF16)
    for j in range(3):
        c0 = 3 * ATTN_W + j * DN_W
        r = jnp.dot(hb, wm_ref[:, c0:c0 + DN_W], preferred_element_type=F32)
        dqkv_ref[:, j * DN_W:(j + 1) * DN_W] = r.astype(BF16)
    c0 = 3 * ATTN_W + DN_CONV
    dz_ref[...] = jnp.dot(hb, wm_ref[:, c0:c0 + DN_W], preferred_element_type=F32).astype(BF16)
    gbc_ref[...] = jnp.dot(hb, ws_ref[...], preferred_element_type=F32)


def _inproj(x2, mod, norm_w, w_in, S):
    T, D = x2.shape
    tm = 512
    B, nt = T // S, S // tm
    wm = w_in[:, :MAIN_COLS].astype(BF16)
    ws = jnp.zeros((D, LANES), F32).at[:, :2 * DN_HEADS].set(w_in[:, MAIN_COLS:]).astype(BF16)
    return pl.pallas_call(
        _inproj_kernel,
        grid=(T // tm,),
        in_specs=[pl.BlockSpec((tm, D), lambda i: (i, 0)),
                  pl.BlockSpec((1, 6, D), lambda i: (i * tm // S, 0, 0)),
                  pl.BlockSpec((1, D), lambda i: (0, 0)),
                  pl.BlockSpec((D, MAIN_COLS), lambda i: (0, 0)),
                  pl.BlockSpec((D, LANES), lambda i: (0, 0))],
        out_specs=[pl.BlockSpec((tm, 3 * ATTN_W), lambda i: (i, 0))]
        + [pl.BlockSpec((1, d, tm // d, 3 * ATTN_W), lambda i: (i // nt, 0, i % nt, 0)) for d in DILATIONS[1:]]
        + [pl.BlockSpec((tm, DN_CONV), lambda i: (i, 0)),
           pl.BlockSpec((tm, DN_W), lambda i: (i, 0)),
           pl.BlockSpec((tm, LANES), lambda i: (i, 0))],
        out_shape=[jax.ShapeDtypeStruct((T, 3 * ATTN_W), BF16)]
        + [jax.ShapeDtypeStruct((B, d, S // d, 3 * ATTN_W), BF16) for d in DILATIONS[1:]]
        + [jax.ShapeDtypeStruct((T, DN_CONV), BF16),
           jax.ShapeDtypeStruct((T, DN_W), BF16),
           jax.ShapeDtypeStruct((T, LANES), F32)],
        scratch_shapes=[pltpu.VMEM((ATTN_W // LANES, tm, LANES), F32)],
        compiler_params=_cparams(("arbitrary",)),
        name="inproj",
    )(x2, mod, norm_w.reshape(1, D), wm, ws)


def _attn_kernel(q_ref, kc_ref, kp_ref, vc_ref, vp_ref, o_ref, lse_ref, kf, vf, *, qb):
    n = pl.program_id(2)
    kf[0:ATTN_BLK, :] = kp_ref[0, 0]
    kf[ATTN_BLK:, :] = kc_ref[0, 0]
    vf[0:ATTN_BLK, :] = vp_ref[0, 0]
    vf[ATTN_BLK:, :] = vc_ref[0, 0]
    row = lax.broadcasted_iota(I32, (ATTN_BLK, 2 * ATTN_BLK), 0)
    col = lax.broadcasted_iota(I32, (ATTN_BLK, 2 * ATTN_BLK), 1)
    band = jnp.logical_or(jnp.logical_and(col < ATTN_BLK, col >= row),
                          jnp.logical_and(col >= ATTN_BLK, col - ATTN_BLK <= row))
    lane = lax.broadcasted_iota(I32, (ATTN_BLK, LANES), 1)
    lo = lane < ATTN_HD

    def sub(j, carry):
        r0 = pl.multiple_of(j * ATTN_BLK, ATTN_BLK)
        first_col = jnp.where(jnp.logical_and(n == 0, j == 0), ATTN_BLK, 0)
        mask = jnp.logical_and(band, col >= first_col)
        npair = ATTN_W // LANES
        cols = [slice(hp * LANES, (hp + 1) * LANES) for hp in range(npair)]
        heads = [(hp, half) for hp in range(npair) for half in range(2)]
        scores = []
        for hp, half in heads:
            q2 = q_ref[0, 0, pl.ds(r0, ATTN_BLK), cols[hp]]
            qm = jnp.where(lo if half == 0 else jnp.logical_not(lo), q2, jnp.zeros_like(q2))
            scores.append(jnp.where(mask, _nt(qm, kf[pl.ds(r0, 2 * ATTN_BLK), cols[hp]]), NEG))
        maxes = [jnp.max(s, axis=-1, keepdims=True) for s in scores]
        probs = [jnp.exp(s - m) for s, m in zip(scores, maxes)]
        dens = [jnp.sum(p, axis=-1, keepdims=True) for p in probs]
        accs = [jnp.dot(p.astype(BF16), vf[pl.ds(r0, 2 * ATTN_BLK), cols[hp]], preferred_element_type=F32)
                for p, (hp, _) in zip(probs, heads)]
        outs = [a / d for a, d in zip(accs, dens)]
        lse_tile = jnp.zeros((ATTN_BLK, LANES), F32)
        for h, (m, d) in enumerate(zip(maxes, dens)):
            lse_tile = jnp.where(lane == h, m + jnp.log(d), lse_tile)
        for hp in range(npair):
            o_ref[0, 0, pl.ds(r0, ATTN_BLK), cols[hp]] = jnp.where(lo, outs[2 * hp], outs[2 * hp + 1]).astype(BF16)
        lse_ref[0, 0, pl.ds(r0, ATTN_BLK), :] = lse_tile
        return carry

    lax.fori_loop(0, qb // ATTN_BLK, sub, 0)


def _attn_branch(qkv, d):
    B, _, L, _ = qkv.shape
    qb = min(512, L)
    nsub = qb // ATTN_BLK
    cur = lambda c: pl.BlockSpec((1, 1, qb, ATTN_W), lambda b, r, n: (b, r, n, c))
    prev = lambda c: pl.BlockSpec((1, 1, ATTN_BLK, ATTN_W),
                                  lambda b, r, n: (b, r, jnp.maximum(n * nsub - 1, 0), c))
    return pl.pallas_call(
        functools.partial(_attn_kernel, qb=qb),
        grid=(B, d, L // qb),
        in_specs=[cur(0), cur(1), prev(1), cur(2), prev(2)],
        out_specs=[pl.BlockSpec((1, 1, qb, ATTN_W), lambda b, r, n: (b, r, n, 0)),
                   pl.BlockSpec((1, 1, qb, LANES), lambda b, r, n: (b, r, n, 0))],
        out_shape=[jax.ShapeDtypeStruct((B, d, L, ATTN_W), BF16),
                   jax.ShapeDtypeStruct((B, d, L, LANES), F32)],
        scratch_shapes=[pltpu.VMEM((qb + ATTN_BLK, ATTN_W), BF16),
                        pltpu.VMEM((qb + ATTN_BLK, ATTN_W), BF16)],
        compiler_params=_cparams(("arbitrary", "arbitrary", "arbitrary")),
        name=f"attn_d{d}",
    )(qkv, qkv, qkv, qkv, qkv)


def _gdn_kernel(x_ref, z_ref, g_ref, cw_ref, prm_ref, nw_ref, o_ref, xext, yc, s0, s1, *, rb):
    i = pl.program_id(1)

    @pl.when(i == 0)
    def _():
        xext[0:8, :] = jnp.zeros((8, DN_CONV), F32)
        s0[...] = jnp.zeros_like(s0)
        s1[...] = jnp.zeros_like(s1)

    @pl.when(i > 0)
    def _():
        xext[0:8, :] = xext[rb:rb + 8, :]

    xext[8:, :] = x_ref[0].astype(F32)
    y = cw_ref[CONV_K - 1:CONV_K, :] * xext[8:8 + rb, :]
    for j in range(CONV_K - 1):
        off = 8 - (CONV_K - 1) + j
        y = y + cw_ref[j:j + 1, :] * xext[off:off + rb, :]
    yc[...] = y * _sigmoid(y)

    C = DN_CHUNK
    H = DN_HEADS
    CW = H * C
    dot = functools.partial(jnp.dot, preferred_element_type=F32)

    def iota(shape, d):
        return lax.broadcasted_iota(I32, shape, d)

    ltri_b = jnp.where(iota((C, C), 0) >= iota((C, C), 1), 1.0, 0.0).astype(BF16)
    lane = iota((C, LANES), 1)
    blane = lane < H
    glane = jnp.logical_and(lane >= H, lane < 2 * H)
    e512 = jnp.where(jnp.logical_or(iota((LANES, DN_W), 1) // DN_D == iota((LANES, DN_W), 0),
                                    iota((LANES, DN_W), 1) // DN_D == iota((LANES, DN_W), 0) - H),
                     1.0, 0.0).astype(BF16)
    e256 = jnp.where(iota((LANES, CW), 1) // C == iota((LANES, CW), 0) - H, 1.0, 0.0).astype(BF16)
    row4 = iota((C, CW), 0)
    col4 = iota((C, CW), 1) % C
    eye4 = jnp.where(row4 == col4, 1.0, 0.0).astype(F32)
    blk = [iota((C, CW), 1) // C == h for h in range(H)]
    bd_cc = iota((CW, CW), 0) // C == iota((CW, CW), 1) // C
    bd_pair = iota((CW, CW), 0) // DN_D == iota((CW, CW), 1) // DN_D
    rt_mask = iota((CW, DN_W), 0) // C == iota((CW, DN_W), 1) // DN_D
    neg_a = -jnp.exp(prm_ref[0:1, :])
    dtb = prm_ref[1:2, :]
    nw = nw_ref[...]

    def hilo(x):
        hi = x.astype(BF16)
        return hi, (x - hi.astype(F32)).astype(BF16)

    def heads(a, w):
        return [a[:, h * w:(h + 1) * w] for h in range(H)]

    def l2n(a, mult):
        return jnp.concatenate(
            [p * (lax.rsqrt(jnp.sum(p * p, axis=-1, keepdims=True) + EPS) * mult) for p in heads(a, DN_D)],
            axis=1)

    def stack4(a):
        return jnp.concatenate([a, a, a, a], axis=0)

    zb = jnp.zeros((), BF16)
    nchunk = rb // C
    chunks = []
    for c in range(nchunk):
        rs = slice(c * C, (c + 1) * C)
        G = g_ref[0, rs, :]
        xg = G + dtb
        gv = jnp.where(glane, neg_a * (jnp.maximum(xg, 0.0) + jnp.log1p(jnp.exp(-jnp.abs(xg)))), 0.0)
        be = jnp.where(blane, _sigmoid(G), 0.0)
        g_hi, g_lo = hilo(gv)
        gcum = dot(ltri_b, g_hi) + dot(ltri_b, g_lo)
        gtot = gcum[C - 1:C, :]
        eg = jnp.where(glane, jnp.exp(gcum), 0.0)
        ek = jnp.where(glane, jnp.exp(gtot - gcum), 0.0)
        ex = dot(jnp.concatenate([be, eg, ek], axis=0).astype(BF16), e512)
        bexp, egexp, ekexp = ex[0:C], ex[C:2 * C], ex[2 * C:3 * C]
        gexp = dot(g_hi, e256) + dot(g_lo, e256)
        d_hi, d_lo = hilo(jnp.where(row4 > col4, gexp, 0.0))
        diff = dot(ltri_b, d_hi) + dot(ltri_b, d_lo)
        decay = jnp.exp(jnp.where(row4 >= col4, diff, NEG))

        q4 = l2n(yc[rs, 0:DN_W], DN_D ** -0.5)
        k4 = l2n(yc[rs, DN_W:2 * DN_W], 1.0)
        v4 = yc[rs, 2 * DN_W:3 * DN_W]
        kb4 = k4 * bexp
        vb4 = v4 * bexp
        rt = jnp.where(rt_mask, stack4(k4.astype(BF16)), jnp.zeros((), BF16))
        ai = _nt(jnp.concatenate([kb4, q4], axis=0).astype(BF16), rt)
        a4 = jnp.where(row4 > col4, ai[0:C] * decay, 0.0)
        pb = a4.astype(BF16)
        chunks.append(dict(
            rs=rs, q4=q4, k4=k4, kb4=kb4, vb4=vb4, egexp=egexp, ekexp=ekexp,
            intra=ai[C:2 * C] * decay, t4=eye4 - a4, pb=pb,
            bd=jnp.where(bd_cc, stack4(pb), zb)))

    for _ in range(5):
        for ch in chunks:
            ch["pb"] = dot(ch["pb"], ch["bd"]).astype(BF16)
        for ch in chunks:
            ch["bd"] = jnp.where(bd_cc, stack4(ch["pb"]), zb)
        for ch in chunks:
            ch["t4"] = ch["t4"] + dot(ch["t4"].astype(BF16), ch["bd"])

    for ch in chunks:
        q4, k4, kb4, vb4, egexp, ekexp = (ch[n] for n in ("q4", "k4", "kb4", "vb4", "egexp", "ekexp"))
        t4b = ch["t4"].astype(BF16)
        lstk = jnp.concatenate([jnp.where(blk[h], t4b, zb) for h in range(H)], axis=0)
        kbg4 = kb4 * egexp
        rstk = jnp.concatenate(
            [jnp.concatenate([vb, kbg], axis=1) for vb, kbg in zip(heads(vb4, DN_D), heads(kbg4, DN_D))],
            axis=0).astype(BF16)
        uw = dot(lstk, rstk)
        u4 = jnp.concatenate([uw[h * C:(h + 1) * C, 0:DN_D] for h in range(H)], axis=1)
        w4 = jnp.concatenate([uw[h * C:(h + 1) * C, DN_D:2 * DN_D] for h in range(H)], axis=1)
        ib = ch["intra"].astype(BF16)
        ch.update(
            u4=u4, wq=jnp.concatenate([w4, q4 * egexp], axis=0).astype(BF16),
            kd4=(k4 * ekexp).astype(BF16), gl4=egexp[C - 1:C, :],
            lint=jnp.concatenate([jnp.where(blk[h], ib, zb) for h in range(H)], axis=0))

    for ch in chunks:
        rs, u4, wq, kd4, gl4, lint = (ch[n] for n in ("rs", "u4", "wq", "kd4", "gl4", "lint"))
        ra = dot(wq[:, 0:CW], s0[...].astype(BF16))
        rc = dot(wq[:, CW:2 * CW], s1[...].astype(BF16))
        vn = u4 - jnp.concatenate([ra[0:C], rc[0:C]], axis=1)
        vnb = vn.astype(BF16)
        oi = dot(lint, jnp.concatenate(heads(vnb, DN_D), axis=0))
        o = (jnp.concatenate([ra[C:2 * C], rc[C:2 * C]], axis=1)
             + jnp.concatenate([oi[h * C:(h + 1) * C] for h in range(H)], axis=1))
        tn = (((0,), (0,)), ((), ()))
        s0[...] = s0[...] * gl4[:, 0:CW] + jnp.where(
            bd_pair, lax.dot_general(kd4[:, 0:CW], vnb[:, 0:CW], tn, preferred_element_type=F32), 0.0)
        s1[...] = s1[...] * gl4[:, CW:2 * CW] + jnp.where(
            bd_pair, lax.dot_general(kd4[:, CW:2 * CW], vnb[:, CW:2 * CW], tn, preferred_element_type=F32), 0.0)

        z = z_ref[0, rs, :].astype(F32)
        on = jnp.concatenate(
            [p * lax.rsqrt(jnp.mean(p * p, axis=-1, keepdims=True) + EPS) * nw for p in heads(o, DN_D)], axis=1)
        o_ref[0, rs, :] = (on * (z * _sigmoid(z))).astype(BF16)


def _gdn(dqkv, dz, gbc, conv_w, A_log, dt_bias, dn_norm_w, B, S):
    rb = 512
    prm = jnp.zeros((2, LANES), F32)
    prm = prm.at[0, DN_HEADS:2 * DN_HEADS].set(A_log.astype(F32))
    prm = prm.at[1, DN_HEADS:2 * DN_HEADS].set(dt_bias.astype(F32))
    out = pl.pallas_call(
        functools.partial(_gdn_kernel, rb=rb),
        grid=(B, S // rb),
        in_specs=[pl.BlockSpec((1, rb, DN_CONV), lambda b, i: (b, i, 0)),
                  pl.BlockSpec((1, rb, DN_W), lambda b, i: (b, i, 0)),
                  pl.BlockSpec((1, rb, LANES), lambda b, i: (b, i, 0)),
                  pl.BlockSpec((CONV_K, DN_CONV), lambda b, i: (0, 0)),
                  pl.BlockSpec((2, LANES), lambda b, i: (0, 0)),
                  pl.BlockSpec((1, DN_D), lambda b, i: (0, 0))],
        out_specs=pl.BlockSpec((1, rb, DN_W), lambda b, i: (b, i, 0)),
        out_shape=jax.ShapeDtypeStruct((B, S, DN_W), BF16),
        scratch_shapes=[pltpu.VMEM((rb + 8, DN_CONV), F32),
                        pltpu.VMEM((rb, DN_CONV), F32),
                        pltpu.VMEM((2 * DN_D, 2 * DN_D), F32),
                        pltpu.VMEM((2 * DN_D, 2 * DN_D), F32)],
        compiler_params=_cparams(("arbitrary", "arbitrary")),
        name="gdn",
    )(dqkv.reshape(B, S, DN_CONV), dz.reshape(B, S, DN_W), gbc.reshape(B, S, LANES),
      conv_w, prm, dn_norm_w.reshape(1, DN_D))
    return out.reshape(B * S, DN_W)


def _out_kernel(o1_ref, o2_ref, o3_ref, l1_ref, l2_ref, l3_ref, dn_ref, x_ref, mod_ref, wo_ref,
                n2_ref, wr_ref, br_ref,
                x1_ref, h2_ref, te_ref, rk_ref, gtc_ref, cnt_ref, base, scr, *, tm):
    i = pl.program_id(0)

    @pl.when(i == 0)
    def _():
        base[...] = jnp.zeros_like(base)

    def natural(ref, d):
        if d == 1:
            return ref[0, 0].astype(F32)
        nl = ref.shape[-1] // LANES
        for res in range(d):
            blk = ref[0, res].astype(F32)
            for c in range(nl):
                scr[c, pl.ds(res, tm // d, stride=d), :] = blk[:, c * LANES:(c + 1) * LANES]
        return jnp.concatenate([scr[c] for c in range(nl)], axis=1)

    l1, l2, l3 = (natural(r, d) for r, d in zip((l1_ref, l2_ref, l3_ref), DILATIONS))
    mx = jnp.maximum(jnp.maximum(l1, l2), l3)
    e1, e2, e3 = jnp.exp(l1 - mx), jnp.exp(l2 - mx), jnp.exp(l3 - mx)
    zs = e1 + e2 + e3
    er = lax.broadcasted_iota(I32, (LANES, ATTN_W), 0)
    ec = lax.broadcasted_iota(I32, (LANES, ATTN_W), 1)
    expand = jnp.where(ec // ATTN_HD == er, 1.0, 0.0).astype(BF16)
    attn = jnp.zeros((tm, ATTN_W), F32)
    for e, o_ref, d in zip((e1, e2, e3), (o1_ref, o2_ref, o3_ref), DILATIONS):
        wgt = jnp.dot((e / zs).astype(BF16), expand, preferred_element_type=F32)
        attn = attn + wgt * natural(o_ref, d)
    mix = (jnp.dot(attn.astype(BF16), wo_ref[0:ATTN_W, :], preferred_element_type=F32)
           + jnp.dot(dn_ref[...], wo_ref[ATTN_W:, :], preferred_element_type=F32))
    x1 = x_ref[...] + mod_ref[0, 2:3, :] * mix
    x1_ref[...] = x1
    ms = jnp.mean(x1 * x1, axis=-1, keepdims=True)
    h2 = x1 * lax.rsqrt(ms + EPS) * n2_ref[...]
    h2 = h2 * (1.0 + mod_ref[0, 4:5, :]) + mod_ref[0, 3:4, :]
    _store_slabs(h2_ref, h2)

    lg = _nt(wr_ref[...], h2, precision=HI) + br_ref[...]
    eidx = lax.broadcasted_iota(I32, (N_EXPERTS, tm), 0)
    vals, idxs, sels = [], [], []
    for _ in range(TOP_K):
        m = jnp.max(lg, axis=0, keepdims=True)
        idx = jnp.min(jnp.where(lg == m, eidx, N_EXPERTS), axis=0, keepdims=True)
        sel = eidx == idx
        vals.append(m)
        idxs.append(idx)
        sels.append(sel)
        lg = jnp.where(sel, -jnp.inf, lg)
    ex = [jnp.exp(v - vals[0]) for v in vals]
    den = ex[0] + ex[1] + ex[2] + ex[3]
    gates = [e / den for e in ex]

    msum = jnp.zeros((N_EXPERTS, tm), F32)
    for sel in sels:
        msum = msum + jnp.where(sel, 1.0, 0.0)
    tr = lax.broadcasted_iota(I32, (tm, tm), 0)
    tc = lax.broadcasted_iota(I32, (tm, tm), 1)
    upper = jnp.where(tr <= tc, 1.0, 0.0).astype(BF16)
    incl = jnp.dot(msum.astype(BF16), upper, preferred_element_type=F32)
    pos = base[:, 0:1] + (incl - msum)
    sub8 = lax.broadcasted_iota(I32, (8, tm), 0)
    te = jnp.zeros((8, tm), I32)
    rk = jnp.zeros((8, tm), I32)
    gt = jnp.zeros((8, tm), F32)
    for k in range(TOP_K):
        rank_k = jnp.sum(jnp.where(sels[k], pos, 0.0), axis=0, keepdims=True).astype(I32)
        te = jnp.where(sub8 == k, idxs[k], te)
        rk = jnp.where(sub8 == k, rank_k, rk)
        gt = jnp.where(sub8 == k, gates[k], gt)
    te_ref[...] = te
    rk_ref[...] = rk
    gtc_ref[...] = jnp.transpose(jnp.concatenate([gt, jnp.zeros((LANES - 8, tm), F32)], axis=0))
    base[...] = base[...] + jnp.sum(msum, axis=1, keepdims=True)
    cnt_ref[...] = base[...].astype(I32)


def _outproj(o1, o2, o3, l1, l2, l3, dn, x2, mod, w_out, norm2_w, w_router, b_router, S):
    T, D = x2.shape
    tm = 512
    nt = S // tm
    row = lambda w: pl.BlockSpec((tm, w), lambda i: (i, 0))
    res = lambda d, w: pl.BlockSpec((1, d, tm // d, w), lambda i: (i // nt, 0, i % nt, 0))
    colb = pl.BlockSpec((8, tm), lambda i: (0, i))
    return pl.pallas_call(
        functools.partial(_out_kernel, tm=tm),
        grid=(T // tm,),
        in_specs=[res(d, ATTN_W) for d in DILATIONS] + [res(d, LANES) for d in DILATIONS]
        + [row(DN_W), row(D),
                  pl.BlockSpec((1, 6, D), lambda i: (i * tm // S, 0, 0)),
                  pl.BlockSpec((D, D), lambda i: (0, 0)),
                  pl.BlockSpec((1, D), lambda i: (0, 0)),
                  pl.BlockSpec((N_EXPERTS, D), lambda i: (0, 0)),
                  pl.BlockSpec((N_EXPERTS, 1), lambda i: (0, 0))],
        out_specs=[row(D), pl.BlockSpec((tm * (D // LANES), LANES), lambda i: (i, 0)), colb, colb,
                   row(LANES), pl.BlockSpec((N_EXPERTS, LANES), lambda i: (0, 0))],
        out_shape=[jax.ShapeDtypeStruct((T, D), F32),
                   jax.ShapeDtypeStruct((T * (D // LANES), LANES), F32),
                   jax.ShapeDtypeStruct((8, T), I32),
                   jax.ShapeDtypeStruct((8, T), I32),
                   jax.ShapeDtypeStruct((T, LANES), F32),
                   jax.ShapeDtypeStruct((N_EXPERTS, LANES), I32)],
        scratch_shapes=[pltpu.VMEM((N_EXPERTS, LANES), F32),
                        pltpu.VMEM((ATTN_W // LANES, tm, LANES), F32)],
        compiler_params=_cparams(("arbitrary",)),
        name="outproj_router",
    )(o1, o2, o3, l1, l2, l3, dn, x2, mod, w_out.astype(BF16), norm2_w.reshape(1, D),
      jnp.transpose(w_router), b_router.reshape(N_EXPERTS, 1))


def _dest_kernel(ps_ref, te_ref, rk_ref, d_ref):
    te = te_ref[...]
    acc = jnp.zeros(te.shape, I32)
    for e in range(N_EXPERTS):
        acc = jnp.where(te == e, ps_ref[e], acc)
    d_ref[...] = acc + rk_ref[...]


def _dest(pstart, te, rk):
    T = te.shape[1]
    tb = 2048
    return pl.pallas_call(
        _dest_kernel,
        grid_spec=pltpu.PrefetchScalarGridSpec(
            num_scalar_prefetch=1,
            grid=(T // tb,),
            in_specs=[pl.BlockSpec((8, tb), lambda i, ps: (0, i)),
                      pl.BlockSpec((8, tb), lambda i, ps: (0, i))],
            out_specs=pl.BlockSpec((8, tb), lambda i, ps: (0, i))),
        out_shape=jax.ShapeDtypeStruct((8, T), I32),
        compiler_params=_cparams(("arbitrary",)),
        name="dest_rows",
    )(pstart, te, rk)


SC_CORES = 2
SC_SUBCORES = 16
SC_IDX_CHUNK = 128


def _invperm(dest_flat, P):
    N = dest_flat.shape[0]
    nch = N // (SC_SUBCORES * SC_IDX_CHUNK)
    half = P // SC_CORES
    per_out = half // SC_SUBCORES
    assert N % (SC_SUBCORES * SC_IDX_CHUNK) == 0 and P % (SC_CORES * SC_SUBCORES * 8) == 0
    mesh = plsc.VectorSubcoreMesh(core_axis_name="c", subcore_axis_name="s",
                                  num_cores=SC_CORES, num_subcores=SC_SUBCORES)

    @functools.partial(
        pl.kernel, mesh=mesh, out_type=jax.ShapeDtypeStruct((P,), I32),
        scratch_types=[pltpu.VMEM((nch, SC_IDX_CHUNK), I32), pltpu.VMEM((nch, SC_IDX_CHUNK), I32),
                       pltpu.VMEM_SHARED((P,), I32), pltpu.VMEM((per_out,), I32), pltpu.SemaphoreType.DMA])
    def scatter_codes(idx_hbm, val_hbm, out_hbm, idx_v, val_v, table, stage, sem):
        core = lax.axis_index("c")
        sub = lax.axis_index("s")
        pltpu.sync_copy(idx_hbm.at[sub], idx_v)
        pltpu.sync_copy(val_hbm.at[sub], val_v)

        @pl.loop(0, nch)
        def _(j):
            pltpu.async_copy(val_v.at[j], table.at[idx_v.at[j]], sem)

        @pl.loop(0, nch)
        def _(j):
            pltpu.make_async_copy(val_v.at[j], table.at[idx_v.at[j]], sem).wait()

        plsc.subcore_barrier()
        off = pl.multiple_of(core * half + sub * per_out, 8)
        pltpu.sync_copy(table.at[pl.ds(off, per_out)], stage)
        pltpu.sync_copy(stage, out_hbm.at[pl.ds(off, per_out)])

    vals = jnp.arange(N, dtype=I32)
    shape = (SC_SUBCORES, nch, SC_IDX_CHUNK)
    return scatter_codes(dest_flat.reshape(shape), vals.reshape(shape))


def _row_copy(src, dst, sem):
    return pltpu.make_async_copy(src, dst, sem)


def _moe_kernel(be_ref, nv_ref, cc_ref, cn_ref, h2_ref, w1_ref, b1_ref, w2_ref, b2_ref, y4_ref,
                xbuf, ybuf, w1b, w2b, gsem, ssem, *, F, T, D, tme, nb):
    i = pl.program_id(0)
    s = i % 2
    ns = D // LANES

    def rows(first, n):
        return pl.ds(pl.multiple_of(first * ns, ns), n * ns)

    def gather_copy(tok, p, slot):
        return _row_copy(h2_ref.at[rows(tok, 1)], xbuf.at[slot, rows(p, 1)], gsem.at[slot])

    def token_of(code):
        return code & (T - 1) if T & (T - 1) == 0 else code % T

    def issue_rows(start_row, nv):
        ng = nv // ROW_UNROLL

        def grp(g, c):
            for j in range(ROW_UNROLL):
                start_row(g * ROW_UNROLL + j, j % 2)
            return c
        lax.fori_loop(0, ng, grp, 0)

        def one(p, c):
            start_row(p, 0)
            return c
        lax.fori_loop(ng * ROW_UNROLL, nv, one, 0)

    def start_gather(code_ref, nv, slot):
        issue_rows(lambda p, pri: gather_copy(token_of(code_ref[0, 0, p]), p, slot).start(priority=pri), nv)

    def wait_rows(make, nv):
        @pl.when(nv > 0)
        def _():
            make(nv).wait()

    def gathered(n, slot):
        return _row_copy(h2_ref.at[rows(0, n)], xbuf.at[slot, rows(0, n)], gsem.at[slot])

    def scattered(n, slot):
        return _row_copy(ybuf.at[slot, rows(0, n)], y4_ref.at[rows(0, n)], ssem.at[slot])

    @pl.when(i == 0)
    def _():
        start_gather(cc_ref, tme, 0)

    nv = nv_ref[i]
    gathered(tme, s).wait()

    @pl.when(i >= 2)
    def _():
        wait_rows(lambda n: scattered(n, s), nv_ref[i - 2])

    @pl.when(jnp.logical_or(i == 0, be_ref[i] != be_ref[jnp.maximum(i - 1, 0)]))
    def _():
        w1b[...] = w1_ref[0].astype(BF16)
        w2b[...] = w2_ref[0].astype(BF16)

    @pl.when(nv > 0)
    def _():
        x = _load_slabs(xbuf, tme, D, lead=(s,)).astype(BF16)
        for p in range(tme):
            gather_copy(token_of(cn_ref[0, 0, p]), p, 1 - s).start(priority=p % 2)
        hgu = jnp.dot(x, w1b[...], preferred_element_type=F32) + b1_ref[0]
        gate = jnp.minimum(hgu[:, :F], SWIGLU_LIMIT)
        up = jnp.clip(hgu[:, F:], -SWIGLU_LIMIT, SWIGLU_LIMIT)
        act = gate * _sigmoid(SWIGLU_ALPHA * gate) * (up + 1.0)
        y = jnp.dot(act.astype(BF16), w2b[...], preferred_element_type=F32) + b2_ref[0]
        _store_slabs(ybuf, y, lead=(s,))

        issue_rows(lambda p, pri: _row_copy(ybuf.at[s, rows(p, 1)], y4_ref.at[rows(cc_ref[0, 0, p], 1)],
                                            ssem.at[s]).start(priority=pri), nv)

    @pl.when(jnp.logical_and(nv == 0, i + 1 < nb))
    def _():
        start_gather(cn_ref, tme, 1 - s)

    @pl.when(i == nb - 1)
    def _():
        @pl.when(nv > 0)
        def _():
            gathered(tme, 1 - s).wait()
        wait_rows(lambda n: scattered(n, s), nv)
        if nb > 1:
            wait_rows(lambda n: scattered(n, 1 - s), nv_ref[i - 1])


def _experts(blk_exp, blk_valid, codes, h2s, w1, b1, w2, b2, tme):
    E, D, F2 = w1.shape
    F = F2 // 2
    ns = D // LANES
    T = h2s.shape[0] // ns
    nb = blk_exp.shape[0]
    codes3 = codes.reshape(nb, 1, tme)
    wspec = lambda shape: pl.BlockSpec(shape, lambda i, be, nv: (be[i], 0, 0))
    cspec = lambda off: pl.BlockSpec((1, 1, tme), lambda i, be, nv: (jnp.minimum(i + off, nb - 1), 0, 0),
                                     memory_space=pltpu.SMEM)
    return pl.pallas_call(
        functools.partial(_moe_kernel, F=F, T=T, D=D, tme=tme, nb=nb),
        grid_spec=pltpu.PrefetchScalarGridSpec(
            num_scalar_prefetch=2,
            grid=(nb,),
            in_specs=[cspec(0), cspec(1),
                      pl.BlockSpec(memory_space=pl.ANY),
                      wspec((1, D, F2)), wspec((1, 1, F2)), wspec((1, F, D)), wspec((1, 1, D))],
            out_specs=pl.BlockSpec(memory_space=pl.ANY),
            scratch_shapes=[pltpu.VMEM((2, tme * ns, LANES), F32), pltpu.VMEM((2, tme * ns, LANES), F32),
                            pltpu.VMEM((D, F2), BF16), pltpu.VMEM((F, D), BF16),
                            pltpu.SemaphoreType.DMA((2,)), pltpu.SemaphoreType.DMA((2,))]),
        out_shape=jax.ShapeDtypeStruct((TOP_K * T * ns, LANES), F32),
        compiler_params=_cparams(("arbitrary",)),
        name="experts",
    )(blk_exp, blk_valid, codes3, codes3, h2s, w1, b1.reshape(E, 1, F2), w2, b2.reshape(E, 1, D))


def _comb_kernel(g_ref, x1_ref, mod_ref, fw_ref, y0_ref, y1_ref, y2_ref, y3_ref, o_ref):
    g = g_ref[...]
    n, d = x1_ref.shape
    y = g[:, 0:1] * _load_slabs(y0_ref, n, d)
    for k, y_ref in ((1, y1_ref), (2, y2_ref), (3, y3_ref)):
        y = y + g[:, k:k + 1] * _load_slabs(y_ref, n, d)
    x2 = x1_ref[...] + mod_ref[0, 5:6, :] * y
    ms = jnp.mean(x2 * x2, axis=-1, keepdims=True)
    o_ref[...] = x2 * lax.rsqrt(ms + EPS) * fw_ref[...]


def _combine(gtc, x1, mod, final_w, y4, S):
    T, D = x1.shape
    tmc = 512
    nt = T // tmc
    yspec = lambda k: pl.BlockSpec((tmc * (D // LANES), LANES), lambda i: (k * nt + i, 0))
    return pl.pallas_call(
        _comb_kernel,
        grid=(nt,),
        in_specs=[pl.BlockSpec((tmc, LANES), lambda i: (i, 0)),
                  pl.BlockSpec((tmc, D), lambda i: (i, 0)),
                  pl.BlockSpec((1, 6, D), lambda i: (i * tmc // S, 0, 0)),
                  pl.BlockSpec((1, D), lambda i: (0, 0)),
                  yspec(0), yspec(1), yspec(2), yspec(3)],
        out_specs=pl.BlockSpec((tmc, D), lambda i: (i, 0)),
        out_shape=jax.ShapeDtypeStruct((T, D), F32),
        compiler_params=_cparams(("arbitrary",)),
        name="combine",
    )(gtc, x1, mod, final_w.reshape(1, D), y4, y4, y4, y4)


def _layer(x2, mod, B, S, norm1_w, w_in, conv_w, A_log, dt_bias, dn_norm_w, w_out,
           norm2_w, w_router, b_router, w1, b1, w2, b2):
    T, D = x2.shape
    qkv, qkv4, qkv16, dqkv, dz, gbc = _inproj(x2, mod, norm1_w, w_in, S)
    views = (qkv.reshape(B, 1, S, 3 * ATTN_W), qkv4, qkv16)
    branches = [_attn_branch(v, d) for v, d in zip(views, DILATIONS)]
    dn = _gdn(dqkv, dz, gbc, conv_w, A_log, dt_bias, dn_norm_w, B, S)
    (o1, l1), (o2, l2), (o3, l3) = branches
    x1, h2, te, rk, gtc, cnt = _outproj(o1, o2, o3, l1, l2, l3, dn, x2, mod, w_out, norm2_w,
                                        w_router, b_router, S)
    tme = MOE_ROWS
    P = T * TOP_K + N_EXPERTS * tme
    counts = cnt[:, 0]
    padded = (counts + tme - 1) // tme * tme
    pend = jnp.cumsum(padded)
    pstart = (pend - padded).astype(I32)
    blk_start = jnp.arange(P // tme, dtype=I32) * tme
    blk_exp = jnp.minimum(jnp.sum((pend[None, :] <= blk_start[:, None]).astype(I32), axis=1),
                          N_EXPERTS - 1).astype(I32)
    blk_valid = jnp.clip(pstart[blk_exp] + counts[blk_exp] - blk_start, 0, tme).astype(I32)
    dest = _dest(pstart, te, rk)
    codes = _invperm(dest[:TOP_K].reshape(TOP_K * T), P)
    y4 = _experts(blk_exp, blk_valid, codes, h2, w1, b1, w2, b2, tme)
    return x1, gtc, y4


def kernel(x, c, w_ada, b_ada, norm1_w, w_in, conv_w, A_log, dt_bias, dn_norm_w, w_out, norm2_w,
           w_router, b_router, w1, b1, w2, b2, final_norm_w):
    B, S, D = x.shape
    depth = w_ada.shape[0]
    assert S % (ATTN_BLK * DILATIONS[-1]) == 0 and depth == 1
    x2 = x.reshape(B * S, D)
    mod = _ada(c, w_ada[0], b_ada[0])
    x1, gtc, y4 = _layer(x2, mod, B, S, norm1_w[0], w_in[0], conv_w[0], A_log[0], dt_bias[0],
                         dn_norm_w[0], w_out[0], norm2_w[0], w_router[0], b_router[0],
                         w1[0], b1[0], w2[0], b2[0])
    out = _combine(gtc, x1, mod, final_norm_w, y4, S)
    return out.reshape(B, S, D)
```

```python
import functools

import jax
import jax.numpy as jnp
from jax import lax
from jax.experimental import pallas as pl
from jax.experimental.pallas import tpu as pltpu
from jax.experimental.pallas import tpu_sc as plsc

F32 = jnp.float32
BF16 = jnp.bfloat16
I32 = jnp.int32
HI = lax.Precision.HIGHEST

LANES = 128
ATTN_HEADS = 8
ATTN_HD = 64
ATTN_W = ATTN_HEADS * ATTN_HD
ATTN_BLK = 128
DILATIONS = (1, 4, 16)
DN_HEADS = 4
DN_D = 128
DN_W = DN_HEADS * DN_D
DN_CONV = 3 * DN_W
CONV_K = 4
DN_CHUNK = 64
N_EXPERTS = 32
TOP_K = 4
MOE_ROWS = 512
ROW_UNROLL = 8
GATHER_ANCHOR_COLS = 256
SWIGLU_LIMIT = 7.0
SWIGLU_ALPHA = 1.702
EPS = 1e-6
NEG = -1e30
MAIN_COLS = 3 * ATTN_W + DN_CONV + DN_W

VMEM_LIMIT = 56 * 1024 * 1024


def _cparams(sem):
    return pltpu.CompilerParams(dimension_semantics=sem, vmem_limit_bytes=VMEM_LIMIT)


def _nt(a, b, **kw):
    return lax.dot_general(a, b, (((1,), (1,)), ((), ())), preferred_element_type=F32, **kw)


def _sigmoid(x):
    return 1.0 / (1.0 + jnp.exp(-x))


def _store_slabs(ref, val, lead=()):
    n, d = val.shape
    ns = d // LANES
    for c in range(ns):
        ref[lead + (pl.ds(c, n, stride=ns), slice(None))] = val[:, c * LANES:(c + 1) * LANES]


def _load_slabs(ref, n, d, lead=()):
    ns = d // LANES
    return jnp.concatenate([ref[lead + (pl.ds(c, n, stride=ns), slice(None))] for c in range(ns)], axis=1)


def _ada_kernel(c_ref, w_ref, b_ref, o_ref):
    c = c_ref[...]
    cond = c * _sigmoid(c)
    o_ref[...] = jnp.dot(cond, w_ref[...], preferred_element_type=F32, precision=HI) + b_ref[...]


def _ada(c, w_ada, b_ada):
    B, D = c.shape
    N = w_ada.shape[1]
    cp = jnp.zeros((8, D), F32).at[:B].set(c)
    tn = 1024
    out = pl.pallas_call(
        _ada_kernel,
        grid=(N // tn,),
        in_specs=[pl.BlockSpec((8, D), lambda j: (0, 0)),
                  pl.BlockSpec((D, tn), lambda j: (0, j)),
                  pl.BlockSpec((1, tn), lambda j: (0, j))],
        out_specs=pl.BlockSpec((8, tn), lambda j: (0, j)),
        out_shape=jax.ShapeDtypeStruct((8, N), F32),
        compiler_params=_cparams(("arbitrary",)),
        name="ada",
    )(cp, w_ada, b_ada.reshape(1, N))
    return out[:B].reshape(B, 6, D)


def _inproj_kernel(x_ref, mod_ref, nw_ref, wm_ref, ws_ref, qkv_ref, qkv4_ref, qkv16_ref, dqkv_ref, dz_ref,
                   gbc_ref, scr, scr2):
    x = x_ref[...]
    tm = x.shape[0]
    shift = mod_ref[0, 0:1, :]
    scale = mod_ref[0, 1:2, :]
    ms = jnp.mean(x * x, axis=-1, keepdims=True)
    h = x * lax.rsqrt(ms + EPS) * nw_ref[...]
    hb = (h * (1.0 + scale) + shift).astype(BF16)
    nl = ATTN_W // LANES
    for j in range(3):
        cs = slice(j * ATTN_W, (j + 1) * ATTN_W)
        r = jnp.dot(hb, wm_ref[:, cs], preferred_element_type=F32)
        if j == 0:
            r = r * (ATTN_HD ** -0.5)
        qkv_ref[:, cs] = r.astype(BF16)
        d1, d2 = DILATIONS[1], DILATIONS[2]
        step = d2 // d1
        for c in range(nl):
            scr[c] = r[:, c * LANES:(c + 1) * LANES]
        for res in range(d1):
            for c in range(nl):
                scr2[c, res * (tm // d1):(res + 1) * (tm // d1), :] = scr[c, pl.ds(res, tm // d1, stride=d1), :]
            part = jnp.concatenate([scr2[c, res * (tm // d1):(res + 1) * (tm // d1), :] for c in range(nl)],
                                   axis=1)
            qkv4_ref[0, res, :, cs] = part.astype(BF16)
        for res in range(d2):
            r1, q = res % d1, res // d1
            part = jnp.concatenate(
                [scr2[c, pl.ds(r1 * (tm // d1) + q, tm // d2, stride=step), :] for c in range(nl)], axis=1)
            qkv16_ref[0, res, :, cs] = part.astype(BF16)
    for j in range(3):
        c0 = 3 * ATTN_W + j * DN_W
        r = jnp.dot(hb, wm_ref[:, c0:c0 + DN_W], preferred_element_type=F32)
        dqkv_ref[:, j * DN_W:(j + 1) * DN_W] = r.astype(BF16)
    c0 = 3 * ATTN_W + DN_CONV
    dz_ref[...] = jnp.dot(hb, wm_ref[:, c0:c0 + DN_W], preferred_element_type=F32).astype(BF16)
    gbc_ref[...] = jnp.dot(hb, ws_ref[...], preferred_element_type=F32)


def _inproj(x2, mod, norm_w, w_in, S):
    T, D = x2.shape
    tm = 512
    B, nt = T // S, S // tm
    wm = w_in[:, :MAIN_COLS].astype(BF16)
    ws = jnp.zeros((D, LANES), F32).at[:, :2 * DN_HEADS].set(w_in[:, MAIN_COLS:]).astype(BF16)
    return pl.pallas_call(
        _inproj_kernel,
        grid=(T // tm,),
        in_specs=[pl.BlockSpec((tm, D), lambda i: (i, 0)),
                  pl.BlockSpec((1, 6, D), lambda i: (i * tm // S, 0, 0)),
                  pl.BlockSpec((1, D), lambda i: (0, 0)),
                  pl.BlockSpec((D, MAIN_COLS), lambda i: (0, 0)),
                  pl.BlockSpec((D, LANES), lambda i: (0, 0))],
        out_specs=[pl.BlockSpec((tm, 3 * ATTN_W), lambda i: (i, 0))]
        + [pl.BlockSpec((1, d, tm // d, 3 * ATTN_W), lambda i: (i // nt, 0, i % nt, 0)) for d in DILATIONS[1:]]
        + [pl.BlockSpec((tm, DN_CONV), lambda i: (i, 0)),
           pl.BlockSpec((tm, DN_W), lambda i: (i, 0)),
           pl.BlockSpec((tm, LANES), lambda i: (i, 0))],
        out_shape=[jax.ShapeDtypeStruct((T, 3 * ATTN_W), BF16)]
        + [jax.ShapeDtypeStruct((B, d, S // d, 3 * ATTN_W), BF16) for d in DILATIONS[1:]]
        + [jax.ShapeDtypeStruct((T, DN_CONV), BF16),
           jax.ShapeDtypeStruct((T, DN_W), BF16),
           jax.ShapeDtypeStruct((T, LANES), F32)],
        scratch_shapes=[pltpu.VMEM((ATTN_W // LANES, tm, LANES), F32),
                        pltpu.VMEM((ATTN_W // LANES, tm, LANES), F32)],
        compiler_params=_cparams(("arbitrary",)),
        name="inproj",
    )(x2, mod, norm_w.reshape(1, D), wm, ws)


def _attn_kernel(q_ref, kc_ref, kp_ref, vc_ref, vp_ref, o_ref, lse_ref, kf, vf, *, qb):
    n = pl.program_id(2)
    kf[0:ATTN_BLK, :] = kp_ref[0, 0]
    kf[ATTN_BLK:, :] = kc_ref[0, 0]
    vf[0:ATTN_BLK, :] = vp_ref[0, 0]
    vf[ATTN_BLK:, :] = vc_ref[0, 0]
    row = lax.broadcasted_iota(I32, (ATTN_BLK, 2 * ATTN_BLK), 0)
    col = lax.broadcasted_iota(I32, (ATTN_BLK, 2 * ATTN_BLK), 1)
    band = jnp.logical_or(jnp.logical_and(col < ATTN_BLK, col >= row),
                          jnp.logical_and(col >= ATTN_BLK, col - ATTN_BLK <= row))
    lane = lax.broadcasted_iota(I32, (ATTN_BLK, LANES), 1)
    lo = lane < ATTN_HD

    def sub(j, carry):
        r0 = pl.multiple_of(j * ATTN_BLK, ATTN_BLK)
        first_col = jnp.where(jnp.logical_and(n == 0, j == 0), ATTN_BLK, 0)
        mask = jnp.logical_and(band, col >= first_col)
        npair = ATTN_W // LANES
        cols = [slice(hp * LANES, (hp + 1) * LANES) for hp in range(npair)]
        heads = [(hp, half) for hp in range(npair) for half in range(2)]
        scores = []
        for hp, half in heads:
            q2 = q_ref[0, 0, pl.ds(r0, ATTN_BLK), cols[hp]]
            qm = jnp.where(lo if half == 0 else jnp.logical_not(lo), q2, jnp.zeros_like(q2))
            scores.append(jnp.where(mask, _nt(qm, kf[pl.ds(r0, 2 * ATTN_BLK), cols[hp]]), NEG))
        maxes = [jnp.max(s, axis=-1, keepdims=True) for s in scores]
        probs = [jnp.exp(s - m) for s, m in zip(scores, maxes)]
        dens = [jnp.sum(p, axis=-1, keepdims=True) for p in probs]
        accs = [jnp.dot(p.astype(BF16), vf[pl.ds(r0, 2 * ATTN_BLK), cols[hp]], preferred_element_type=F32)
                for p, (hp, _) in zip(probs, heads)]
        outs = [a / d for a, d in zip(accs, dens)]
        lse_tile = jnp.zeros((ATTN_BLK, LANES), F32)
        for h, (m, d) in enumerate(zip(maxes, dens)):
            lse_tile = jnp.where(lane == h, m + jnp.log(d), lse_tile)
        for hp in range(npair):
            o_ref[0, 0, pl.ds(r0, ATTN_BLK), cols[hp]] = jnp.where(lo, outs[2 * hp], outs[2 * hp + 1]).astype(BF16)
        lse_ref[0, 0, pl.ds(r0, ATTN_BLK), :] = lse_tile
        return carry

    lax.fori_loop(0, qb // ATTN_BLK, sub, 0)


def _attn_branch(qkv, d):
    B, _, L, _ = qkv.shape
    qb = min(512, L)
    nsub = qb // ATTN_BLK
    cur = lambda c: pl.BlockSpec((1, 1, qb, ATTN_W), lambda b, r, n: (b, r, n, c))
    prev = lambda c: pl.BlockSpec((1, 1, ATTN_BLK, ATTN_W),
                                  lambda b, r, n: (b, r, jnp.maximum(n * nsub - 1, 0), c))
    return pl.pallas_call(
        functools.partial(_attn_kernel, qb=qb),
        grid=(B, d, L // qb),
        in_specs=[cur(0), cur(1), prev(1), cur(2), prev(2)],
        out_specs=[pl.BlockSpec((1, 1, qb, ATTN_W), lambda b, r, n: (b, r, n, 0)),
                   pl.BlockSpec((1, 1, qb, LANES), lambda b, r, n: (b, r, n, 0))],
        out_shape=[jax.ShapeDtypeStruct((B, d, L, ATTN_W), BF16),
                   jax.ShapeDtypeStruct((B, d, L, LANES), F32)],
        scratch_shapes=[pltpu.VMEM((qb + ATTN_BLK, ATTN_W), BF16),
                        pltpu.VMEM((qb + ATTN_BLK, ATTN_W), BF16)],
        compiler_params=_cparams(("arbitrary", "arbitrary", "arbitrary")),
        name=f"attn_d{d}",
    )(qkv, qkv, qkv, qkv, qkv)


def _gdn_kernel(x_ref, z_ref, g_ref, cw_ref, prm_ref, nw_ref, o_ref, xext, yc, s0, s1, *, rb):
    i = pl.program_id(1)

    @pl.when(i == 0)
    def _():
        xext[0:8, :] = jnp.zeros((8, DN_CONV), F32)
        s0[...] = jnp.zeros_like(s0)
        s1[...] = jnp.zeros_like(s1)

    @pl.when(i > 0)
    def _():
        xext[0:8, :] = xext[rb:rb + 8, :]

    xext[8:, :] = x_ref[0].astype(F32)
    y = cw_ref[CONV_K - 1:CONV_K, :] * xext[8:8 + rb, :]
    for j in range(CONV_K - 1):
        off = 8 - (CONV_K - 1) + j
        y = y + cw_ref[j:j + 1, :] * xext[off:off + rb, :]
    yc[...] = y * _sigmoid(y)

    C = DN_CHUNK
    H = DN_HEADS
    CW = H * C
    dot = functools.partial(jnp.dot, preferred_element_type=F32)

    def iota(shape, d):
        return lax.broadcasted_iota(I32, shape, d)

    ltri_b = jnp.where(iota((C, C), 0) >= iota((C, C), 1), 1.0, 0.0).astype(BF16)
    lane = iota((C, LANES), 1)
    blane = lane < H
    glane = jnp.logical_and(lane >= H, lane < 2 * H)
    e512 = jnp.where(jnp.logical_or(iota((LANES, DN_W), 1) // DN_D == iota((LANES, DN_W), 0),
                                    iota((LANES, DN_W), 1) // DN_D == iota((LANES, DN_W), 0) - H),
                     1.0, 0.0).astype(BF16)
    e256 = jnp.where(iota((LANES, CW), 1) // C == iota((LANES, CW), 0) - H, 1.0, 0.0).astype(BF16)
    row4 = iota((C, CW), 0)
    col4 = iota((C, CW), 1) % C
    eye4 = jnp.where(row4 == col4, 1.0, 0.0).astype(F32)
    blk = [iota((C, CW), 1) // C == h for h in range(H)]
    bd_cc = iota((CW, CW), 0) // C == iota((CW, CW), 1) // C
    bd_pair = iota((CW, CW), 0) // DN_D == iota((CW, CW), 1) // DN_D
    rt_mask = iota((CW, DN_W), 0) // C == iota((CW, DN_W), 1) // DN_D
    neg_a = -jnp.exp(prm_ref[0:1, :])
    dtb = prm_ref[1:2, :]
    nw = nw_ref[...]

    def hilo(x):
        hi = x.astype(BF16)
        return hi, (x - hi.astype(F32)).astype(BF16)

    def heads(a, w):
        return [a[:, h * w:(h + 1) * w] for h in range(H)]

    def l2n(a, mult):
        return jnp.concatenate(
            [p * (lax.rsqrt(jnp.sum(p * p, axis=-1, keepdims=True) + EPS) * mult) for p in heads(a, DN_D)],
            axis=1)

    def stack4(a):
        return jnp.concatenate([a, a, a, a], axis=0)

    zb = jnp.zeros((), BF16)
    nchunk = rb // C
    chunks = []
    for c in range(nchunk):
        rs = slice(c * C, (c + 1) * C)
        G = g_ref[0, rs, :]
        xg = G + dtb
        gv = jnp.where(glane, neg_a * (jnp.maximum(xg, 0.0) + jnp.log1p(jnp.exp(-jnp.abs(xg)))), 0.0)
        be = jnp.where(blane, _sigmoid(G), 0.0)
        g_hi, g_lo = hilo(gv)
        gcum = dot(ltri_b, g_hi) + dot(ltri_b, g_lo)
        gtot = gcum[C - 1:C, :]
        eg = jnp.where(glane, jnp.exp(gcum), 0.0)
        ek = jnp.where(glane, jnp.exp(gtot - gcum), 0.0)
        ex = dot(jnp.concatenate([be, eg, ek], axis=0).astype(BF16), e512)
        bexp, egexp, ekexp = ex[0:C], ex[C:2 * C], ex[2 * C:3 * C]
        gexp = dot(g_hi, e256) + dot(g_lo, e256)
        d_hi, d_lo = hilo(jnp.where(row4 > col4, gexp, 0.0))
        diff = dot(ltri_b, d_hi) + dot(ltri_b, d_lo)
        decay = jnp.exp(jnp.where(row4 >= col4, diff, NEG))

        q4 = l2n(yc[rs, 0:DN_W], DN_D ** -0.5)
        k4 = l2n(yc[rs, DN_W:2 * DN_W], 1.0)
        v4 = yc[rs, 2 * DN_W:3 * DN_W]
        kb4 = k4 * bexp
        vb4 = v4 * bexp
        rt = jnp.where(rt_mask, stack4(k4.astype(BF16)), jnp.zeros((), BF16))
        ai = _nt(jnp.concatenate([kb4, q4], axis=0).astype(BF16), rt)
        a4 = jnp.where(row4 > col4, ai[0:C] * decay, 0.0)
        pb = a4.astype(BF16)
        chunks.append(dict(
            rs=rs, q4=q4, k4=k4, kb4=kb4, vb4=vb4, egexp=egexp, ekexp=ekexp,
            intra=ai[C:2 * C] * decay, t4=eye4 - a4, pb=pb,
            bd=jnp.where(bd_cc, stack4(pb), zb)))

    for _ in range(5):
        for ch in chunks:
            ch["pb"] = dot(ch["pb"], ch["bd"]).astype(BF16)
        for ch in chunks:
            ch["bd"] = jnp.where(bd_cc, stack4(ch["pb"]), zb)
        for ch in chunks:
            ch["t4"] = ch["t4"] + dot(ch["t4"].astype(BF16), ch["bd"])

    for ch in chunks:
        q4, k4, kb4, vb4, egexp, ekexp = (ch[n] for n in ("q4", "k4", "kb4", "vb4", "egexp", "ekexp"))
        t4b = ch["t4"].astype(BF16)
        lstk = jnp.concatenate([jnp.where(blk[h], t4b, zb) for h in range(H)], axis=0)
        kbg4 = kb4 * egexp
        rstk = jnp.concatenate(
            [jnp.concatenate([vb, kbg], axis=1) for vb, kbg in zip(heads(vb4, DN_D), heads(kbg4, DN_D))],
            axis=0).astype(BF16)
        uw = dot(lstk, rstk)
        u4 = jnp.concatenate([uw[h * C:(h + 1) * C, 0:DN_D] for h in range(H)], axis=1)
        w4 = jnp.concatenate([uw[h * C:(h + 1) * C, DN_D:2 * DN_D] for h in range(H)], axis=1)
        ib = ch["intra"].astype(BF16)
        ch.update(
            u4=u4, wq=jnp.concatenate([w4, q4 * egexp], axis=0).astype(BF16),
            kd4=(k4 * ekexp).astype(BF16), gl4=egexp[C - 1:C, :],
            lint=jnp.concatenate([jnp.where(blk[h], ib, zb) for h in range(H)], axis=0))

    for ch in chunks:
        rs, u4, wq, kd4, gl4, lint = (ch[n] for n in ("rs", "u4", "wq", "kd4", "gl4", "lint"))
        ra = dot(wq[:, 0:CW], s0[...].astype(BF16))
        rc = dot(wq[:, CW:2 * CW], s1[...].astype(BF16))
        vn = u4 - jnp.concatenate([ra[0:C], rc[0:C]], axis=1)
        vnb = vn.astype(BF16)
        oi = dot(lint, jnp.concatenate(heads(vnb, DN_D), axis=0))
        o = (jnp.concatenate([ra[C:2 * C], rc[C:2 * C]], axis=1)
             + jnp.concatenate([oi[h * C:(h + 1) * C] for h in range(H)], axis=1))
        tn = (((0,), (0,)), ((), ()))
        s0[...] = s0[...] * gl4[:, 0:CW] + jnp.where(
            bd_pair, lax.dot_general(kd4[:, 0:CW], vnb[:, 0:CW], tn, preferred_element_type=F32), 0.0)
        s1[...] = s1[...] * gl4[:, CW:2 * CW] + jnp.where(
            bd_pair, lax.dot_general(kd4[:, CW:2 * CW], vnb[:, CW:2 * CW], tn, preferred_element_type=F32), 0.0)

        z = z_ref[0, rs, :].astype(F32)
        on = jnp.concatenate(
            [p * lax.rsqrt(jnp.mean(p * p, axis=-1, keepdims=True) + EPS) * nw for p in heads(o, DN_D)], axis=1)
        o_ref[0, rs, :] = (on * (z * _sigmoid(z))).astype(BF16)


def _gdn(dqkv, dz, gbc, conv_w, A_log, dt_bias, dn_norm_w, B, S):
    rb = 512
    prm = jnp.zeros((2, LANES), F32)
    prm = prm.at[0, DN_HEADS:2 * DN_HEADS].set(A_log.astype(F32))
    prm = prm.at[1, DN_HEADS:2 * DN_HEADS].set(dt_bias.astype(F32))
    out = pl.pallas_call(
        functools.partial(_gdn_kernel, rb=rb),
        grid=(B, S // rb),
        in_specs=[pl.BlockSpec((1, rb, DN_CONV), lambda b, i: (b, i, 0)),
                  pl.BlockSpec((1, rb, DN_W), lambda b, i: (b, i, 0)),
                  pl.BlockSpec((1, rb, LANES), lambda b, i: (b, i, 0)),
                  pl.BlockSpec((CONV_K, DN_CONV), lambda b, i: (0, 0)),
                  pl.BlockSpec((2, LANES), lambda b, i: (0, 0)),
                  pl.BlockSpec((1, DN_D), lambda b, i: (0, 0))],
        out_specs=pl.BlockSpec((1, rb, DN_W), lambda b, i: (b, i, 0)),
        out_shape=jax.ShapeDtypeStruct((B, S, DN_W), BF16),
        scratch_shapes=[pltpu.VMEM((rb + 8, DN_CONV), F32),
                        pltpu.VMEM((rb, DN_CONV), F32),
                        pltpu.VMEM((2 * DN_D, 2 * DN_D), F32),
                        pltpu.VMEM((2 * DN_D, 2 * DN_D), F32)],
        compiler_params=_cparams(("arbitrary", "arbitrary")),
        name="gdn",
    )(dqkv.reshape(B, S, DN_CONV), dz.reshape(B, S, DN_W), gbc.reshape(B, S, LANES),
      conv_w, prm, dn_norm_w.reshape(1, DN_D))
    return out.reshape(B * S, DN_W)


def _out_kernel(o1_ref, o2_ref, o3_ref, l1_ref, l2_ref, l3_ref, dn_ref, x_ref, mod_ref, wo_ref,
                n2_ref, wr_ref, br_ref,
                x1_ref, h2_ref, te_ref, rk_ref, gtc_ref, cnt_ref, base, scr, *, tm):
    i = pl.program_id(0)

    @pl.when(i == 0)
    def _():
        base[...] = jnp.zeros_like(base)

    def natural(ref, d):
        if d == 1:
            return ref[0, 0].astype(F32)
        nl = ref.shape[-1] // LANES
        for res in range(d):
            blk = ref[0, res].astype(F32)
            for c in range(nl):
                scr[c, pl.ds(res, tm // d, stride=d), :] = blk[:, c * LANES:(c + 1) * LANES]
        return jnp.concatenate([scr[c] for c in range(nl)], axis=1)

    l1, l2, l3 = (natural(r, d) for r, d in zip((l1_ref, l2_ref, l3_ref), DILATIONS))
    mx = jnp.maximum(jnp.maximum(l1, l2), l3)
    e1, e2, e3 = jnp.exp(l1 - mx), jnp.exp(l2 - mx), jnp.exp(l3 - mx)
    zs = e1 + e2 + e3
    er = lax.broadcasted_iota(I32, (LANES, ATTN_W), 0)
    ec = lax.broadcasted_iota(I32, (LANES, ATTN_W), 1)
    expand = jnp.where(ec // ATTN_HD == er, 1.0, 0.0).astype(BF16)
    attn = jnp.zeros((tm, ATTN_W), F32)
    for e, o_ref, d in zip((e1, e2, e3), (o1_ref, o2_ref, o3_ref), DILATIONS):
        wgt = jnp.dot((e / zs).astype(BF16), expand, preferred_element_type=F32)
        attn = attn + wgt * natural(o_ref, d)
    mix = (jnp.dot(attn.astype(BF16), wo_ref[0:ATTN_W, :], preferred_element_type=F32)
           + jnp.dot(dn_ref[...], wo_ref[ATTN_W:, :], preferred_element_type=F32))
    x1 = x_ref[...] + mod_ref[0, 2:3, :] * mix
    x1_ref[...] = x1
    ms = jnp.mean(x1 * x1, axis=-1, keepdims=True)
    h2 = x1 * lax.rsqrt(ms + EPS) * n2_ref[...]
    h2 = h2 * (1.0 + mod_ref[0, 4:5, :]) + mod_ref[0, 3:4, :]
    _store_slabs(h2_ref, h2)

    lg = _nt(wr_ref[...], h2, precision=HI) + br_ref[...]
    eidx = lax.broadcasted_iota(I32, (N_EXPERTS, tm), 0)
    vals, idxs, sels = [], [], []
    for _ in range(TOP_K):
        m = jnp.max(lg, axis=0, keepdims=True)
        idx = jnp.min(jnp.where(lg == m, eidx, N_EXPERTS), axis=0, keepdims=True)
        sel = eidx == idx
        vals.append(m)
        idxs.append(idx)
        sels.append(sel)
        lg = jnp.where(sel, -jnp.inf, lg)
    ex = [jnp.exp(v - vals[0]) for v in vals]
    den = ex[0] + ex[1] + ex[2] + ex[3]
    gates = [e / den for e in ex]

    msum = jnp.zeros((N_EXPERTS, tm), F32)
    for sel in sels:
        msum = msum + jnp.where(sel, 1.0, 0.0)
    tr = lax.broadcasted_iota(I32, (tm, tm), 0)
    tc = lax.broadcasted_iota(I32, (tm, tm), 1)
    upper = jnp.where(tr <= tc, 1.0, 0.0).astype(BF16)
    incl = jnp.dot(msum.astype(BF16), upper, preferred_element_type=F32)
    pos = base[:, 0:1] + (incl - msum)
    sub8 = lax.broadcasted_iota(I32, (8, tm), 0)
    te = jnp.zeros((8, tm), I32)
    rk = jnp.zeros((8, tm), I32)
    gt = jnp.zeros((8, tm), F32)
    for k in range(TOP_K):
        rank_k = jnp.sum(jnp.where(sels[k], pos, 0.0), axis=0, keepdims=True).astype(I32)
        te = jnp.where(sub8 == k, idxs[k], te)
        rk = jnp.where(sub8 == k, rank_k, rk)
        gt = jnp.where(sub8 == k, gates[k], gt)
    te_ref[...] = te
    rk_ref[...] = rk
    gtc_ref[...] = jnp.transpose(jnp.concatenate([gt, jnp.zeros((LANES - 8, tm), F32)], axis=0))
    base[...] = base[...] + jnp.sum(msum, axis=1, keepdims=True)
    cnt_ref[...] = base[...].astype(I32)


def _outproj(o1, o2, o3, l1, l2, l3, dn, x2, mod, w_out, norm2_w, w_router, b_router, S):
    T, D = x2.shape
    tm = 512
    nt = S // tm
    row = lambda w: pl.BlockSpec((tm, w), lambda i: (i, 0))
    res = lambda d, w: pl.BlockSpec((1, d, tm // d, w), lambda i: (i // nt, 0, i % nt, 0))
    colb = pl.BlockSpec((8, tm), lambda i: (0, i))
    return pl.pallas_call(
        functools.partial(_out_kernel, tm=tm),
        grid=(T // tm,),
        in_specs=[res(d, ATTN_W) for d in DILATIONS] + [res(d, LANES) for d in DILATIONS]
        + [row(DN_W), row(D),
                  pl.BlockSpec((1, 6, D), lambda i: (i * tm // S, 0, 0)),
                  pl.BlockSpec((D, D), lambda i: (0, 0)),
                  pl.BlockSpec((1, D), lambda i: (0, 0)),
                  pl.BlockSpec((N_EXPERTS, D), lambda i: (0, 0)),
                  pl.BlockSpec((N_EXPERTS, 1), lambda i: (0, 0))],
        out_specs=[row(D), pl.BlockSpec((tm * (D // LANES), LANES), lambda i: (i, 0)), colb, colb,
                   row(LANES), pl.BlockSpec((N_EXPERTS, LANES), lambda i: (0, 0))],
        out_shape=[jax.ShapeDtypeStruct((T, D), F32),
                   jax.ShapeDtypeStruct((T * (D // LANES), LANES), F32),
                   jax.ShapeDtypeStruct((8, T), I32),
                   jax.ShapeDtypeStruct((8, T), I32),
                   jax.ShapeDtypeStruct((T, LANES), F32),
                   jax.ShapeDtypeStruct((N_EXPERTS, LANES), I32)],
        scratch_shapes=[pltpu.VMEM((N_EXPERTS, LANES), F32),
                        pltpu.VMEM((ATTN_W // LANES, tm, LANES), F32)],
        compiler_params=_cparams(("arbitrary",)),
        name="outproj_router",
    )(o1, o2, o3, l1, l2, l3, dn, x2, mod, w_out.astype(BF16), norm2_w.reshape(1, D),
      jnp.transpose(w_router), b_router.reshape(N_EXPERTS, 1))


def _dest_kernel(ps_ref, te_ref, rk_ref, d_ref):
    te = te_ref[...]
    acc = jnp.zeros(te.shape, I32)
    for e in range(N_EXPERTS):
        acc = jnp.where(te == e, ps_ref[e], acc)
    d_ref[...] = acc + rk_ref[...]


def _dest(pstart, te, rk):
    T = te.shape[1]
    tb = 2048
    return pl.pallas_call(
        _dest_kernel,
        grid_spec=pltpu.PrefetchScalarGridSpec(
            num_scalar_prefetch=1,
            grid=(T // tb,),
            in_specs=[pl.BlockSpec((8, tb), lambda i, ps: (0, i)),
                      pl.BlockSpec((8, tb), lambda i, ps: (0, i))],
            out_specs=pl.BlockSpec((8, tb), lambda i, ps: (0, i))),
        out_shape=jax.ShapeDtypeStruct((8, T), I32),
        compiler_params=_cparams(("arbitrary",)),
        name="dest_rows",
    )(pstart, te, rk)


SC_CORES = 2
SC_SUBCORES = 16
SC_IDX_CHUNK = 128


def _invperm(dest_flat, P):
    N = dest_flat.shape[0]
    nch = N // (SC_SUBCORES * SC_IDX_CHUNK)
    half = P // SC_CORES
    per_out = half // SC_SUBCORES
    assert N % (SC_SUBCORES * SC_IDX_CHUNK) == 0 and P % (SC_CORES * SC_SUBCORES * 8) == 0
    mesh = plsc.VectorSubcoreMesh(core_axis_name="c", subcore_axis_name="s",
                                  num_cores=SC_CORES, num_subcores=SC_SUBCORES)

    @functools.partial(
        pl.kernel, mesh=mesh, out_type=jax.ShapeDtypeStruct((P,), I32),
        scratch_types=[pltpu.VMEM((nch, SC_IDX_CHUNK), I32), pltpu.VMEM((nch, SC_IDX_CHUNK), I32),
                       pltpu.VMEM_SHARED((P,), I32), pltpu.VMEM((per_out,), I32), pltpu.SemaphoreType.DMA])
    def scatter_codes(idx_hbm, val_hbm, out_hbm, idx_v, val_v, table, stage, sem):
        core = lax.axis_index("c")
        sub = lax.axis_index("s")
        pltpu.sync_copy(idx_hbm.at[sub], idx_v)
        pltpu.sync_copy(val_hbm.at[sub], val_v)

        @pl.loop(0, nch)
        def _(j):
            pltpu.async_copy(val_v.at[j], table.at[idx_v.at[j]], sem)

        @pl.loop(0, nch)
        def _(j):
            pltpu.make_async_copy(val_v.at[j], table.at[idx_v.at[j]], sem).wait()

        plsc.subcore_barrier()
        off = pl.multiple_of(core * half + sub * per_out, 8)
        pltpu.sync_copy(table.at[pl.ds(off, per_out)], stage)
        pltpu.sync_copy(stage, out_hbm.at[pl.ds(off, per_out)])

    vals = jnp.arange(N, dtype=I32)
    shape = (SC_SUBCORES, nch, SC_IDX_CHUNK)
    return scatter_codes(dest_flat.reshape(shape), vals.reshape(shape))


def _row_copy(src, dst, sem):
    return pltpu.make_async_copy(src, dst, sem)


def _moe_kernel(be_ref, nv_ref, cc_ref, cn_ref, h2_ref, w1_ref, b1_ref, w2_ref, b2_ref, y4_ref,
                xbuf, ybuf, w1b, w2b, gsem, ssem, *, F, T, D, tme, nb):
    i = pl.program_id(0)
    s = i % 2
    ns = D // LANES

    def rows(first, n):
        return pl.ds(pl.multiple_of(first * ns, ns), n * ns)

    def gather_copy(tok, p, slot):
        return _row_copy(h2_ref.at[rows(tok, 1)], xbuf.at[slot, rows(p, 1)], gsem.at[slot])

    def token_of(code):
        return code & (T - 1) if T & (T - 1) == 0 else code % T

    def issue_rows(start_row, nv):
        ng = nv // ROW_UNROLL

        def grp(g, c):
            for j in range(ROW_UNROLL):
                start_row(g * ROW_UNROLL + j, j % 2)
            return c
        lax.fori_loop(0, ng, grp, 0)

        def one(p, c):
            start_row(p, 0)
            return c
        lax.fori_loop(ng * ROW_UNROLL, nv, one, 0)

    def start_gather(code_ref, nv, slot):
        issue_rows(lambda p, pri: gather_copy(token_of(code_ref[0, 0, p]), p, slot).start(priority=pri), nv)

    def wait_rows(make, nv):
        @pl.when(nv > 0)
        def _():
            make(nv).wait()

    def gathered(n, slot):
        return _row_copy(h2_ref.at[rows(0, n)], xbuf.at[slot, rows(0, n)], gsem.at[slot])

    def scattered(n, slot):
        return _row_copy(ybuf.at[slot, rows(0, n)], y4_ref.at[rows(0, n)], ssem.at[slot])

    @pl.when(i == 0)
    def _():
        start_gather(cc_ref, tme, 0)

    nv = nv_ref[i]
    gathered(tme, s).wait()

    @pl.when(i >= 2)
    def _():
        wait_rows(lambda n: scattered(n, s), nv_ref[i - 2])

    @pl.when(jnp.logical_or(i == 0, be_ref[i] != be_ref[jnp.maximum(i - 1, 0)]))
    def _():
        w1b[...] = w1_ref[0].astype(BF16)
        w2b[...] = w2_ref[0].astype(BF16)

    @pl.when(nv > 0)
    def _():
        x = _load_slabs(xbuf, tme, D, lead=(s,)).astype(BF16)
        hgu = jnp.dot(x, w1b[...], preferred_element_type=F32) + b1_ref[0]
        ngroup = (2 * F) // GATHER_ANCHOR_COLS
        per = tme // ngroup
        for g in range(ngroup):
            for p in range(g * per, (g + 1) * per):
                gather_copy(token_of(cn_ref[0, 0, p]), p, 1 - s).start(priority=p % 2)
            c0 = g * GATHER_ANCHOR_COLS
            xbuf[1 - s, tme * ns:tme * ns + 8, :] = hgu[0:8, c0:c0 + LANES]
        gate = jnp.minimum(hgu[:, :F], SWIGLU_LIMIT)
        up = jnp.clip(hgu[:, F:], -SWIGLU_LIMIT, SWIGLU_LIMIT)
        act = gate * _sigmoid(SWIGLU_ALPHA * gate) * (up + 1.0)
        y = jnp.dot(act.astype(BF16), w2b[...], preferred_element_type=F32) + b2_ref[0]
        _store_slabs(ybuf, y, lead=(s,))

        issue_rows(lambda p, pri: _row_copy(ybuf.at[s, rows(p, 1)], y4_ref.at[rows(cc_ref[0, 0, p], 1)],
                                            ssem.at[s]).start(priority=pri), nv)

    @pl.when(jnp.logical_and(nv == 0, i + 1 < nb))
    def _():
        start_gather(cn_ref, tme, 1 - s)

    @pl.when(i == nb - 1)
    def _():
        @pl.when(nv > 0)
        def _():
            gathered(tme, 1 - s).wait()
        wait_rows(lambda n: scattered(n, s), nv)
        if nb > 1:
            wait_rows(lambda n: scattered(n, 1 - s), nv_ref[i - 1])


def _experts(blk_exp, blk_valid, codes, h2s, w1, b1, w2, b2, tme):
    E, D, F2 = w1.shape
    F = F2 // 2
    ns = D // LANES
    T = h2s.shape[0] // ns
    nb = blk_exp.shape[0]
    codes3 = codes.reshape(nb, 1, tme)
    wspec = lambda shape: pl.BlockSpec(shape, lambda i, be, nv: (be[i], 0, 0))
    cspec = lambda off: pl.BlockSpec((1, 1, tme), lambda i, be, nv: (jnp.minimum(i + off, nb - 1), 0, 0),
                                     memory_space=pltpu.SMEM)
    return pl.pallas_call(
        functools.partial(_moe_kernel, F=F, T=T, D=D, tme=tme, nb=nb),
        grid_spec=pltpu.PrefetchScalarGridSpec(
            num_scalar_prefetch=2,
            grid=(nb,),
            in_specs=[cspec(0), cspec(1),
                      pl.BlockSpec(memory_space=pl.ANY),
                      wspec((1, D, F2)), wspec((1, 1, F2)), wspec((1, F, D)), wspec((1, 1, D))],
            out_specs=pl.BlockSpec(memory_space=pl.ANY),
            scratch_shapes=[pltpu.VMEM((2, tme * ns + 8, LANES), F32), pltpu.VMEM((2, tme * ns, LANES), F32),
                            pltpu.VMEM((D, F2), BF16), pltpu.VMEM((F, D), BF16),
                            pltpu.SemaphoreType.DMA((2,)), pltpu.SemaphoreType.DMA((2,))]),
        out_shape=jax.ShapeDtypeStruct((TOP_K * T * ns, LANES), F32),
        compiler_params=_cparams(("arbitrary",)),
        name="experts",
    )(blk_exp, blk_valid, codes3, codes3, h2s, w1, b1.reshape(E, 1, F2), w2, b2.reshape(E, 1, D))


def _comb_kernel(g_ref, x1_ref, mod_ref, fw_ref, y0_ref, y1_ref, y2_ref, y3_ref, o_ref):
    g = g_ref[...]
    n, d = x1_ref.shape
    y = g[:, 0:1] * _load_slabs(y0_ref, n, d)
    for k, y_ref in ((1, y1_ref), (2, y2_ref), (3, y3_ref)):
        y = y + g[:, k:k + 1] * _load_slabs(y_ref, n, d)
    x2 = x1_ref[...] + mod_ref[0, 5:6, :] * y
    ms = jnp.mean(x2 * x2, axis=-1, keepdims=True)
    o_ref[...] = x2 * lax.rsqrt(ms + EPS) * fw_ref[...]


def _combine(gtc, x1, mod, final_w, y4, S):
    T, D = x1.shape
    tmc = 512
    nt = T // tmc
    yspec = lambda k: pl.BlockSpec((tmc * (D // LANES), LANES), lambda i: (k * nt + i, 0))
    return pl.pallas_call(
        _comb_kernel,
        grid=(nt,),
        in_specs=[pl.BlockSpec((tmc, LANES), lambda i: (i, 0)),
                  pl.BlockSpec((tmc, D), lambda i: (i, 0)),
                  pl.BlockSpec((1, 6, D), lambda i: (i * tmc // S, 0, 0)),
                  pl.BlockSpec((1, D), lambda i: (0, 0)),
                  yspec(0), yspec(1), yspec(2), yspec(3)],
        out_specs=pl.BlockSpec((tmc, D), lambda i: (i, 0)),
        out_shape=jax.ShapeDtypeStruct((T, D), F32),
        compiler_params=_cparams(("arbitrary",)),
        name="combine",
    )(gtc, x1, mod, final_w.reshape(1, D), y4, y4, y4, y4)


def _layer(x2, mod, B, S, norm1_w, w_in, conv_w, A_log, dt_bias, dn_norm_w, w_out,
           norm2_w, w_router, b_router, w1, b1, w2, b2):
    T, D = x2.shape
    qkv, qkv4, qkv16, dqkv, dz, gbc = _inproj(x2, mod, norm1_w, w_in, S)
    views = (qkv.reshape(B, 1, S, 3 * ATTN_W), qkv4, qkv16)
    branches = [_attn_branch(v, d) for v, d in zip(views, DILATIONS)]
    dn = _gdn(dqkv, dz, gbc, conv_w, A_log, dt_bias, dn_norm_w, B, S)
    (o1, l1), (o2, l2), (o3, l3) = branches
    x1, h2, te, rk, gtc, cnt = _outproj(o1, o2, o3, l1, l2, l3, dn, x2, mod, w_out, norm2_w,
                                        w_router, b_router, S)
    tme = MOE_ROWS
    P = T * TOP_K + N_EXPERTS * tme
    counts = cnt[:, 0]
    padded = (counts + tme - 1) // tme * tme
    pend = jnp.cumsum(padded)
    pstart = (pend - padded).astype(I32)
    blk_start = jnp.arange(P // tme, dtype=I32) * tme
    blk_exp = jnp.minimum(jnp.sum((pend[None, :] <= blk_start[:, None]).astype(I32), axis=1),
                          N_EXPERTS - 1).astype(I32)
    blk_valid = jnp.clip(pstart[blk_exp] + counts[blk_exp] - blk_start, 0, tme).astype(I32)
    dest = _dest(pstart, te, rk)
    codes = _invperm(dest[:TOP_K].reshape(TOP_K * T), P)
    y4 = _experts(blk_exp, blk_valid, codes, h2, w1, b1, w2, b2, tme)
    return x1, gtc, y4


def kernel(x, c, w_ada, b_ada, norm1_w, w_in, conv_w, A_log, dt_bias, dn_norm_w, w_out, norm2_w,
           w_router, b_router, w1, b1, w2, b2, final_norm_w):
    B, S, D = x.shape
    depth = w_ada.shape[0]
    assert S % (ATTN_BLK * DILATIONS[-1]) == 0 and depth == 1
    x2 = x.reshape(B * S, D)
    mod = _ada(c, w_ada[0], b_ada[0])
    x1, gtc, y4 = _layer(x2, mod, B, S, norm1_w[0], w_in[0], conv_w[0], A_log[0], dt_bias[0],
                         dn_norm_w[0], w_out[0], norm2_w[0], w_router[0], b_router[0],
                         w1[0], b1[0], w2[0], b2[0])
    out = _combine(gtc, x1, mod, final_norm_w, y4, S)
    return out.reshape(B, S, D)
```

```python
import functools

import jax
import jax.numpy as jnp
from jax import lax
from jax.experimental import pallas as pl
from jax.experimental.pallas import tpu as pltpu
from jax.experimental.pallas import tpu_sc as plsc

F32 = jnp.float32
BF16 = jnp.bfloat16
I32 = jnp.int32
HI = lax.Precision.HIGHEST

LANES = 128
ATTN_HEADS = 8
ATTN_HD = 64
ATTN_W = ATTN_HEADS * ATTN_HD
ATTN_BLK = 128
DILATIONS = (1, 4, 16)
DN_HEADS = 4
DN_D = 128
DN_W = DN_HEADS * DN_D
DN_CONV = 3 * DN_W
CONV_K = 4
DN_CHUNK = 64
N_EXPERTS = 32
TOP_K = 4
MOE_ROWS = 512
ROW_UNROLL = 8
SWIGLU_LIMIT = 7.0
SWIGLU_ALPHA = 1.702
EPS = 1e-6
NEG = -1e30
MAIN_COLS = 3 * ATTN_W + DN_CONV + DN_W

VMEM_LIMIT = 56 * 1024 * 1024


def _cparams(sem):
    return pltpu.CompilerParams(dimension_semantics=sem, vmem_limit_bytes=VMEM_LIMIT)


def _nt(a, b, **kw):
    return lax.dot_general(a, b, (((1,), (1,)), ((), ())), preferred_element_type=F32, **kw)


def _sigmoid(x):
    return 1.0 / (1.0 + jnp.exp(-x))


def _store_slabs(ref, val, lead=()):
    n, d = val.shape
    ns = d // LANES
    for c in range(ns):
        ref[lead + (pl.ds(c, n, stride=ns), slice(None))] = val[:, c * LANES:(c + 1) * LANES]


def _load_slabs(ref, n, d, lead=()):
    ns = d // LANES
    return jnp.concatenate([ref[lead + (pl.ds(c, n, stride=ns), slice(None))] for c in range(ns)], axis=1)


def _ada_kernel(c_ref, w_ref, b_ref, o_ref):
    c = c_ref[...]
    cond = c * _sigmoid(c)
    o_ref[...] = jnp.dot(cond, w_ref[...], preferred_element_type=F32, precision=HI) + b_ref[...]


def _ada(c, w_ada, b_ada):
    B, D = c.shape
    N = w_ada.shape[1]
    cp = jnp.zeros((8, D), F32).at[:B].set(c)
    tn = 1024
    out = pl.pallas_call(
        _ada_kernel,
        grid=(N // tn,),
        in_specs=[pl.BlockSpec((8, D), lambda j: (0, 0)),
                  pl.BlockSpec((D, tn), lambda j: (0, j)),
                  pl.BlockSpec((1, tn), lambda j: (0, j))],
        out_specs=pl.BlockSpec((8, tn), lambda j: (0, j)),
        out_shape=jax.ShapeDtypeStruct((8, N), F32),
        compiler_params=_cparams(("arbitrary",)),
        name="ada",
    )(cp, w_ada, b_ada.reshape(1, N))
    return out[:B].reshape(B, 6, D)


def _inproj_kernel(x_ref, mod_ref, nw_ref, wm_ref, ws_ref, qkv_ref, qkv4_ref, qkv16_ref, dqkv_ref, dz_ref,
                   gbc_ref, scr, scr2):
    x = x_ref[...]
    tm = x.shape[0]
    shift = mod_ref[0, 0:1, :]
    scale = mod_ref[0, 1:2, :]
    ms = jnp.mean(x * x, axis=-1, keepdims=True)
    h = x * lax.rsqrt(ms + EPS) * nw_ref[...]
    hb = (h * (1.0 + scale) + shift).astype(BF16)
    nl = ATTN_W // LANES
    for j in range(3):
        cs = slice(j * ATTN_W, (j + 1) * ATTN_W)
        r = jnp.dot(hb, wm_ref[:, cs], preferred_element_type=F32)
        if j == 0:
            r = r * (ATTN_HD ** -0.5)
        qkv_ref[:, cs] = r.astype(BF16)
        d1, d2 = DILATIONS[1], DILATIONS[2]
        step = d2 // d1
        for c in range(nl):
            scr[c] = r[:, c * LANES:(c + 1) * LANES]
        for res in range(d1):
            for c in range(nl):
                scr2[c, res * (tm // d1):(res + 1) * (tm // d1), :] = scr[c, pl.ds(res, tm // d1, stride=d1), :]
            part = jnp.concatenate([scr2[c, res * (tm // d1):(res + 1) * (tm // d1), :] for c in range(nl)],
                                   axis=1)
            qkv4_ref[0, res, :, cs] = part.astype(BF16)
        for res in range(d2):
            r1, q = res % d1, res // d1
            part = jnp.concatenate(
                [scr2[c, pl.ds(r1 * (tm // d1) + q, tm // d2, stride=step), :] for c in range(nl)], axis=1)
            qkv16_ref[0, res, :, cs] = part.astype(BF16)
    for j in range(3):
        c0 = 3 * ATTN_W + j * DN_W
        r = jnp.dot(hb, wm_ref[:, c0:c0 + DN_W], preferred_element_type=F32)
        dqkv_ref[:, j * DN_W:(j + 1) * DN_W] = r.astype(BF16)
    c0 = 3 * ATTN_W + DN_CONV
    dz_ref[...] = jnp.dot(hb, wm_ref[:, c0:c0 + DN_W], preferred_element_type=F32).astype(BF16)
    gbc_ref[...] = jnp.dot(hb, ws_ref[...], preferred_element_type=F32)


def _inproj(x2, mod, norm_w, w_in, S):
    T, D = x2.shape
    tm = 512
    B, nt = T // S, S // tm
    wm = w_in[:, :MAIN_COLS].astype(BF16)
    ws = jnp.zeros((D, LANES), F32).at[:, :2 * DN_HEADS].set(w_in[:, MAIN_COLS:]).astype(BF16)
    return pl.pallas_call(
        _inproj_kernel,
        grid=(T // tm,),
        in_specs=[pl.BlockSpec((tm, D), lambda i: (i, 0)),
                  pl.BlockSpec((1, 6, D), lambda i: (i * tm // S, 0, 0)),
                  pl.BlockSpec((1, D), lambda i: (0, 0)),
                  pl.BlockSpec((D, MAIN_COLS), lambda i: (0, 0)),
                  pl.BlockSpec((D, LANES), lambda i: (0, 0))],
        out_specs=[pl.BlockSpec((tm, 3 * ATTN_W), lambda i: (i, 0))]
        + [pl.BlockSpec((1, d, tm // d, 3 * ATTN_W), lambda i: (i // nt, 0, i % nt, 0)) for d in DILATIONS[1:]]
        + [pl.BlockSpec((tm, DN_CONV), lambda i: (i, 0)),
           pl.BlockSpec((tm, DN_W), lambda i: (i, 0)),
           pl.BlockSpec((tm, LANES), lambda i: (i, 0))],
        out_shape=[jax.ShapeDtypeStruct((T, 3 * ATTN_W), BF16)]
        + [jax.ShapeDtypeStruct((B, d, S // d, 3 * ATTN_W), BF16) for d in DILATIONS[1:]]
        + [jax.ShapeDtypeStruct((T, DN_CONV), BF16),
           jax.ShapeDtypeStruct((T, DN_W), BF16),
           jax.ShapeDtypeStruct((T, LANES), F32)],
        scratch_shapes=[pltpu.VMEM((ATTN_W // LANES, tm, LANES), F32),
                        pltpu.VMEM((ATTN_W // LANES, tm, LANES), F32)],
        compiler_params=_cparams(("arbitrary",)),
        name="inproj",
    )(x2, mod, norm_w.reshape(1, D), wm, ws)


def _attn_kernel(q_ref, kc_ref, kp_ref, vc_ref, vp_ref, o_ref, lse_ref, kf, vf, *, qb):
    n = pl.program_id(2)
    kf[0:ATTN_BLK, :] = kp_ref[0, 0]
    kf[ATTN_BLK:, :] = kc_ref[0, 0]
    vf[0:ATTN_BLK, :] = vp_ref[0, 0]
    vf[ATTN_BLK:, :] = vc_ref[0, 0]
    row = lax.broadcasted_iota(I32, (ATTN_BLK, 2 * ATTN_BLK), 0)
    col = lax.broadcasted_iota(I32, (ATTN_BLK, 2 * ATTN_BLK), 1)
    band = jnp.logical_or(jnp.logical_and(col < ATTN_BLK, col >= row),
                          jnp.logical_and(col >= ATTN_BLK, col - ATTN_BLK <= row))
    lane = lax.broadcasted_iota(I32, (ATTN_BLK, LANES), 1)
    lo = lane < ATTN_HD

    def sub(j, carry):
        r0 = pl.multiple_of(j * ATTN_BLK, ATTN_BLK)
        first_col = jnp.where(jnp.logical_and(n == 0, j == 0), ATTN_BLK, 0)
        mask = jnp.logical_and(band, col >= first_col)
        npair = ATTN_W // LANES
        cols = [slice(hp * LANES, (hp + 1) * LANES) for hp in range(npair)]
        heads = [(hp, half) for hp in range(npair) for half in range(2)]
        scores = []
        for hp, half in heads:
            q2 = q_ref[0, 0, pl.ds(r0, ATTN_BLK), cols[hp]]
            qm = jnp.where(lo if half == 0 else jnp.logical_not(lo), q2, jnp.zeros_like(q2))
            scores.append(jnp.where(mask, _nt(qm, kf[pl.ds(r0, 2 * ATTN_BLK), cols[hp]]), NEG))
        maxes = [jnp.max(s, axis=-1, keepdims=True) for s in scores]
        probs = [jnp.exp(s - m) for s, m in zip(scores, maxes)]
        dens = [jnp.sum(p, axis=-1, keepdims=True) for p in probs]
        accs = [jnp.dot(p.astype(BF16), vf[pl.ds(r0, 2 * ATTN_BLK), cols[hp]], preferred_element_type=F32)
                for p, (hp, _) in zip(probs, heads)]
        outs = [a / d for a, d in zip(accs, dens)]
        lse_tile = jnp.zeros((ATTN_BLK, LANES), F32)
        for h, (m, d) in enumerate(zip(maxes, dens)):
            lse_tile = jnp.where(lane == h, m + jnp.log(d), lse_tile)
        for hp in range(npair):
            o_ref[0, 0, pl.ds(r0, ATTN_BLK), cols[hp]] = jnp.where(lo, outs[2 * hp], outs[2 * hp + 1]).astype(BF16)
        lse_ref[0, 0, pl.ds(r0, ATTN_BLK), :] = lse_tile
        return carry

    lax.fori_loop(0, qb // ATTN_BLK, sub, 0)


def _attn_branch(qkv, d):
    B, _, L, _ = qkv.shape
    qb = min(512, L)
    nsub = qb // ATTN_BLK
    cur = lambda c: pl.BlockSpec((1, 1, qb, ATTN_W), lambda b, r, n: (b, r, n, c))
    prev = lambda c: pl.BlockSpec((1, 1, ATTN_BLK, ATTN_W),
                                  lambda b, r, n: (b, r, jnp.maximum(n * nsub - 1, 0), c))
    return pl.pallas_call(
        functools.partial(_attn_kernel, qb=qb),
        grid=(B, d, L // qb),
        in_specs=[cur(0), cur(1), prev(1), cur(2), prev(2)],
        out_specs=[pl.BlockSpec((1, 1, qb, ATTN_W), lambda b, r, n: (b, r, n, 0)),
                   pl.BlockSpec((1, 1, qb, LANES), lambda b, r, n: (b, r, n, 0))],
        out_shape=[jax.ShapeDtypeStruct((B, d, L, ATTN_W), BF16),
                   jax.ShapeDtypeStruct((B, d, L, LANES), F32)],
        scratch_shapes=[pltpu.VMEM((qb + ATTN_BLK, ATTN_W), BF16),
                        pltpu.VMEM((qb + ATTN_BLK, ATTN_W), BF16)],
        compiler_params=_cparams(("arbitrary", "arbitrary", "arbitrary")),
        name=f"attn_d{d}",
    )(qkv, qkv, qkv, qkv, qkv)


def _gdn_kernel(x_ref, z_ref, g_ref, cw_ref, prm_ref, nw_ref, o_ref, xext, yc, s0, s1, *, rb):
    i = pl.program_id(1)

    @pl.when(i == 0)
    def _():
        xext[0:8, :] = jnp.zeros((8, DN_CONV), F32)
        s0[...] = jnp.zeros_like(s0)
        s1[...] = jnp.zeros_like(s1)

    @pl.when(i > 0)
    def _():
        xext[0:8, :] = xext[rb:rb + 8, :]

    xext[8:, :] = x_ref[0].astype(F32)
    y = cw_ref[CONV_K - 1:CONV_K, :] * xext[8:8 + rb, :]
    for j in range(CONV_K - 1):
        off = 8 - (CONV_K - 1) + j
        y = y + cw_ref[j:j + 1, :] * xext[off:off + rb, :]
    yc[...] = y * _sigmoid(y)

    C = DN_CHUNK
    H = DN_HEADS
    CW = H * C
    dot = functools.partial(jnp.dot, preferred_element_type=F32)

    def iota(shape, d):
        return lax.broadcasted_iota(I32, shape, d)

    ltri_b = jnp.where(iota((C, C), 0) >= iota((C, C), 1), 1.0, 0.0).astype(BF16)
    lane = iota((C, LANES), 1)
    blane = lane < H
    glane = jnp.logical_and(lane >= H, lane < 2 * H)
    e512 = jnp.where(jnp.logical_or(iota((LANES, DN_W), 1) // DN_D == iota((LANES, DN_W), 0),
                                    iota((LANES, DN_W), 1) // DN_D == iota((LANES, DN_W), 0) - H),
                     1.0, 0.0).astype(BF16)
    e256 = jnp.where(iota((LANES, CW), 1) // C == iota((LANES, CW), 0) - H, 1.0, 0.0).astype(BF16)
    row4 = iota((C, CW), 0)
    col4 = iota((C, CW), 1) % C
    eye4 = jnp.where(row4 == col4, 1.0, 0.0).astype(F32)
    blk = [iota((C, CW), 1) // C == h for h in range(H)]
    bd_cc = iota((CW, CW), 0) // C == iota((CW, CW), 1) // C
    bd_pair = iota((CW, CW), 0) // DN_D == iota((CW, CW), 1) // DN_D
    rt_mask = iota((CW, DN_W), 0) // C == iota((CW, DN_W), 1) // DN_D
    neg_a = -jnp.exp(prm_ref[0:1, :])
    dtb = prm_ref[1:2, :]
    nw = nw_ref[...]

    def hilo(x):
        hi = x.astype(BF16)
        return hi, (x - hi.astype(F32)).astype(BF16)

    def heads(a, w):
        return [a[:, h * w:(h + 1) * w] for h in range(H)]

    def l2n(a, mult):
        return jnp.concatenate(
            [p * (lax.rsqrt(jnp.sum(p * p, axis=-1, keepdims=True) + EPS) * mult) for p in heads(a, DN_D)],
            axis=1)

    def stack4(a):
        return jnp.concatenate([a, a, a, a], axis=0)

    zb = jnp.zeros((), BF16)
    nchunk = rb // C
    chunks = []
    for c in range(nchunk):
        rs = slice(c * C, (c + 1) * C)
        G = g_ref[0, rs, :]
        xg = G + dtb
        gv = jnp.where(glane, neg_a * (jnp.maximum(xg, 0.0) + jnp.log1p(jnp.exp(-jnp.abs(xg)))), 0.0)
        be = jnp.where(blane, _sigmoid(G), 0.0)
        g_hi, g_lo = hilo(gv)
        gcum = dot(ltri_b, g_hi) + dot(ltri_b, g_lo)
        gtot = gcum[C - 1:C, :]
        eg = jnp.where(glane, jnp.exp(gcum), 0.0)
        ek = jnp.where(glane, jnp.exp(gtot - gcum), 0.0)
        ex = dot(jnp.concatenate([be, eg, ek], axis=0).astype(BF16), e512)
        bexp, egexp, ekexp = ex[0:C], ex[C:2 * C], ex[2 * C:3 * C]
        gexp = dot(g_hi, e256) + dot(g_lo, e256)
        d_hi, d_lo = hilo(jnp.where(row4 > col4, gexp, 0.0))
        diff = dot(ltri_b, d_hi) + dot(ltri_b, d_lo)
        decay = jnp.exp(jnp.where(row4 >= col4, diff, NEG))

        q4 = l2n(yc[rs, 0:DN_W], DN_D ** -0.5)
        k4 = l2n(yc[rs, DN_W:2 * DN_W], 1.0)
        v4 = yc[rs, 2 * DN_W:3 * DN_W]
        kb4 = k4 * bexp
        vb4 = v4 * bexp
        rt = jnp.where(rt_mask, stack4(k4.astype(BF16)), jnp.zeros((), BF16))
        ai = _nt(jnp.concatenate([kb4, q4], axis=0).astype(BF16), rt)
        a4 = jnp.where(row4 > col4, ai[0:C] * decay, 0.0)
        pb = a4.astype(BF16)
        chunks.append(dict(
            rs=rs, q4=q4, k4=k4, kb4=kb4, vb4=vb4, egexp=egexp, ekexp=ekexp,
            intra=ai[C:2 * C] * decay, t4=eye4 - a4, pb=pb,
            bd=jnp.where(bd_cc, stack4(pb), zb)))

    for _ in range(5):
        for ch in chunks:
            ch["pb"] = dot(ch["pb"], ch["bd"]).astype(BF16)
        for ch in chunks:
            ch["bd"] = jnp.where(bd_cc, stack4(ch["pb"]), zb)
        for ch in chunks:
            ch["t4"] = ch["t4"] + dot(ch["t4"].astype(BF16), ch["bd"])

    for ch in chunks:
        q4, k4, kb4, vb4, egexp, ekexp = (ch[n] for n in ("q4", "k4", "kb4", "vb4", "egexp", "ekexp"))
        t4b = ch["t4"].astype(BF16)
        lstk = jnp.concatenate([jnp.where(blk[h], t4b, zb) for h in range(H)], axis=0)
        kbg4 = kb4 * egexp
        rstk = jnp.concatenate(
            [jnp.concatenate([vb, kbg], axis=1) for vb, kbg in zip(heads(vb4, DN_D), heads(kbg4, DN_D))],
            axis=0).astype(BF16)
        uw = dot(lstk, rstk)
        u4 = jnp.concatenate([uw[h * C:(h + 1) * C, 0:DN_D] for h in range(H)], axis=1)
        w4 = jnp.concatenate([uw[h * C:(h + 1) * C, DN_D:2 * DN_D] for h in range(H)], axis=1)
        ib = ch["intra"].astype(BF16)
        ch.update(
            u4=u4, wq=jnp.concatenate([w4, q4 * egexp], axis=0).astype(BF16),
            kd4=(k4 * ekexp).astype(BF16), gl4=egexp[C - 1:C, :],
            lint=jnp.concatenate([jnp.where(blk[h], ib, zb) for h in range(H)], axis=0))

    for ch in chunks:
        rs, u4, wq, kd4, gl4, lint = (ch[n] for n in ("rs", "u4", "wq", "kd4", "gl4", "lint"))
        ra = dot(wq[:, 0:CW], s0[...].astype(BF16))
        rc = dot(wq[:, CW:2 * CW], s1[...].astype(BF16))
        vn = u4 - jnp.concatenate([ra[0:C], rc[0:C]], axis=1)
        vnb = vn.astype(BF16)
        oi = dot(lint, jnp.concatenate(heads(vnb, DN_D), axis=0))
        o = (jnp.concatenate([ra[C:2 * C], rc[C:2 * C]], axis=1)
             + jnp.concatenate([oi[h * C:(h + 1) * C] for h in range(H)], axis=1))
        tn = (((0,), (0,)), ((), ()))
        s0[...] = s0[...] * gl4[:, 0:CW] + jnp.where(
            bd_pair, lax.dot_general(kd4[:, 0:CW], vnb[:, 0:CW], tn, preferred_element_type=F32), 0.0)
        s1[...] = s1[...] * gl4[:, CW:2 * CW] + jnp.where(
            bd_pair, lax.dot_general(kd4[:, CW:2 * CW], vnb[:, CW:2 * CW], tn, preferred_element_type=F32), 0.0)

        z = z_ref[0, rs, :].astype(F32)
        on = jnp.concatenate(
            [p * lax.rsqrt(jnp.mean(p * p, axis=-1, keepdims=True) + EPS) * nw for p in heads(o, DN_D)], axis=1)
        o_ref[0, rs, :] = (on * (z * _sigmoid(z))).astype(BF16)


def _gdn(dqkv, dz, gbc, conv_w, A_log, dt_bias, dn_norm_w, B, S):
    rb = 1024
    prm = jnp.zeros((2, LANES), F32)
    prm = prm.at[0, DN_HEADS:2 * DN_HEADS].set(A_log.astype(F32))
    prm = prm.at[1, DN_HEADS:2 * DN_HEADS].set(dt_bias.astype(F32))
    out = pl.pallas_call(
        functools.partial(_gdn_kernel, rb=rb),
        grid=(B, S // rb),
        in_specs=[pl.BlockSpec((1, rb, DN_CONV), lambda b, i: (b, i, 0)),
                  pl.BlockSpec((1, rb, DN_W), lambda b, i: (b, i, 0)),
                  pl.BlockSpec((1, rb, LANES), lambda b, i: (b, i, 0)),
                  pl.BlockSpec((CONV_K, DN_CONV), lambda b, i: (0, 0)),
                  pl.BlockSpec((2, LANES), lambda b, i: (0, 0)),
                  pl.BlockSpec((1, DN_D), lambda b, i: (0, 0))],
        out_specs=pl.BlockSpec((1, rb, DN_W), lambda b, i: (b, i, 0)),
        out_shape=jax.ShapeDtypeStruct((B, S, DN_W), BF16),
        scratch_shapes=[pltpu.VMEM((rb + 8, DN_CONV), F32),
                        pltpu.VMEM((rb, DN_CONV), F32),
                        pltpu.VMEM((2 * DN_D, 2 * DN_D), F32),
                        pltpu.VMEM((2 * DN_D, 2 * DN_D), F32)],
        compiler_params=_cparams(("arbitrary", "arbitrary")),
        name="gdn",
    )(dqkv.reshape(B, S, DN_CONV), dz.reshape(B, S, DN_W), gbc.reshape(B, S, LANES),
      conv_w, prm, dn_norm_w.reshape(1, DN_D))
    return out.reshape(B * S, DN_W)


def _out_kernel(o1_ref, o2_ref, o3_ref, l1_ref, l2_ref, l3_ref, dn_ref, x_ref, mod_ref, wo_ref,
                n2_ref, wr_ref, br_ref,
                x1_ref, h2_ref, te_ref, rk_ref, gtc_ref, cnt_ref, base, scr, *, tm):
    i = pl.program_id(0)

    @pl.when(i == 0)
    def _():
        base[...] = jnp.zeros_like(base)

    def natural(ref, d):
        if d == 1:
            return ref[0, 0].astype(F32)
        nl = ref.shape[-1] // LANES
        for res in range(d):
            blk = ref[0, res].astype(F32)
            for c in range(nl):
                scr[c, pl.ds(res, tm // d, stride=d), :] = blk[:, c * LANES:(c + 1) * LANES]
        return jnp.concatenate([scr[c] for c in range(nl)], axis=1)

    l1, l2, l3 = (natural(r, d) for r, d in zip((l1_ref, l2_ref, l3_ref), DILATIONS))
    mx = jnp.maximum(jnp.maximum(l1, l2), l3)
    e1, e2, e3 = jnp.exp(l1 - mx), jnp.exp(l2 - mx), jnp.exp(l3 - mx)
    zs = e1 + e2 + e3
    er = lax.broadcasted_iota(I32, (LANES, ATTN_W), 0)
    ec = lax.broadcasted_iota(I32, (LANES, ATTN_W), 1)
    expand = jnp.where(ec // ATTN_HD == er, 1.0, 0.0).astype(BF16)
    attn = jnp.zeros((tm, ATTN_W), F32)
    for e, o_ref, d in zip((e1, e2, e3), (o1_ref, o2_ref, o3_ref), DILATIONS):
        wgt = jnp.dot((e / zs).astype(BF16), expand, preferred_element_type=F32)
        attn = attn + wgt * natural(o_ref, d)
    mix = (jnp.dot(attn.astype(BF16), wo_ref[0:ATTN_W, :], preferred_element_type=F32)
           + jnp.dot(dn_ref[...], wo_ref[ATTN_W:, :], preferred_element_type=F32))
    x1 = x_ref[...] + mod_ref[0, 2:3, :] * mix
    x1_ref[...] = x1
    ms = jnp.mean(x1 * x1, axis=-1, keepdims=True)
    h2 = x1 * lax.rsqrt(ms + EPS) * n2_ref[...]
    h2 = h2 * (1.0 + mod_ref[0, 4:5, :]) + mod_ref[0, 3:4, :]
    _store_slabs(h2_ref, h2)

    lg = _nt(wr_ref[...], h2, precision=HI) + br_ref[...]
    eidx = lax.broadcasted_iota(I32, (N_EXPERTS, tm), 0)
    vals, idxs, sels = [], [], []
    for _ in range(TOP_K):
        m = jnp.max(lg, axis=0, keepdims=True)
        idx = jnp.min(jnp.where(lg == m, eidx, N_EXPERTS), axis=0, keepdims=True)
        sel = eidx == idx
        vals.append(m)
        idxs.append(idx)
        sels.append(sel)
        lg = jnp.where(sel, -jnp.inf, lg)
    ex = [jnp.exp(v - vals[0]) for v in vals]
    den = ex[0] + ex[1] + ex[2] + ex[3]
    gates = [e / den for e in ex]

    msum = jnp.zeros((N_EXPERTS, tm), F32)
    for sel in sels:
        msum = msum + jnp.where(sel, 1.0, 0.0)
    tr = lax.broadcasted_iota(I32, (tm, tm), 0)
    tc = lax.broadcasted_iota(I32, (tm, tm), 1)
    upper = jnp.where(tr <= tc, 1.0, 0.0).astype(BF16)
    incl = jnp.dot(msum.astype(BF16), upper, preferred_element_type=F32)
    pos = base[:, 0:1] + (incl - msum)
    sub8 = lax.broadcasted_iota(I32, (8, tm), 0)
    te = jnp.zeros((8, tm), I32)
    rk = jnp.zeros((8, tm), I32)
    gt = jnp.zeros((8, tm), F32)
    for k in range(TOP_K):
        rank_k = jnp.sum(jnp.where(sels[k], pos, 0.0), axis=0, keepdims=True).astype(I32)
        te = jnp.where(sub8 == k, idxs[k], te)
        rk = jnp.where(sub8 == k, rank_k, rk)
        gt = jnp.where(sub8 == k, gates[k], gt)
    te_ref[...] = te
    rk_ref[...] = rk
    gtc_ref[...] = jnp.transpose(jnp.concatenate([gt, jnp.zeros((LANES - 8, tm), F32)], axis=0))
    base[...] = base[...] + jnp.sum(msum, axis=1, keepdims=True)
    cnt_ref[...] = base[...].astype(I32)


def _outproj(o1, o2, o3, l1, l2, l3, dn, x2, mod, w_out, norm2_w, w_router, b_router, S):
    T, D = x2.shape
    tm = 512
    nt = S // tm
    row = lambda w: pl.BlockSpec((tm, w), lambda i: (i, 0))
    res = lambda d, w: pl.BlockSpec((1, d, tm // d, w), lambda i: (i // nt, 0, i % nt, 0))
    colb = pl.BlockSpec((8, tm), lambda i: (0, i))
    return pl.pallas_call(
        functools.partial(_out_kernel, tm=tm),
        grid=(T // tm,),
        in_specs=[res(d, ATTN_W) for d in DILATIONS] + [res(d, LANES) for d in DILATIONS]
        + [row(DN_W), row(D),
                  pl.BlockSpec((1, 6, D), lambda i: (i * tm // S, 0, 0)),
                  pl.BlockSpec((D, D), lambda i: (0, 0)),
                  pl.BlockSpec((1, D), lambda i: (0, 0)),
                  pl.BlockSpec((N_EXPERTS, D), lambda i: (0, 0)),
                  pl.BlockSpec((N_EXPERTS, 1), lambda i: (0, 0))],
        out_specs=[row(D), pl.BlockSpec((tm * (D // LANES), LANES), lambda i: (i, 0)), colb, colb,
                   row(LANES), pl.BlockSpec((N_EXPERTS, LANES), lambda i: (0, 0))],
        out_shape=[jax.ShapeDtypeStruct((T, D), F32),
                   jax.ShapeDtypeStruct((T * (D // LANES), LANES), F32),
                   jax.ShapeDtypeStruct((8, T), I32),
                   jax.ShapeDtypeStruct((8, T), I32),
                   jax.ShapeDtypeStruct((T, LANES), F32),
                   jax.ShapeDtypeStruct((N_EXPERTS, LANES), I32)],
        scratch_shapes=[pltpu.VMEM((N_EXPERTS, LANES), F32),
                        pltpu.VMEM((ATTN_W // LANES, tm, LANES), F32)],
        compiler_params=_cparams(("arbitrary",)),
        name="outproj_router",
    )(o1, o2, o3, l1, l2, l3, dn, x2, mod, w_out.astype(BF16), norm2_w.reshape(1, D),
      jnp.transpose(w_router), b_router.reshape(N_EXPERTS, 1))


def _dest_kernel(ps_ref, te_ref, rk_ref, d_ref):
    te = te_ref[...]
    acc = jnp.zeros(te.shape, I32)
    for e in range(N_EXPERTS):
        acc = jnp.where(te == e, ps_ref[e], acc)
    d_ref[...] = acc + rk_ref[...]


def _dest(pstart, te, rk):
    T = te.shape[1]
    tb = 2048
    return pl.pallas_call(
        _dest_kernel,
        grid_spec=pltpu.PrefetchScalarGridSpec(
            num_scalar_prefetch=1,
            grid=(T // tb,),
            in_specs=[pl.BlockSpec((8, tb), lambda i, ps: (0, i)),
                      pl.BlockSpec((8, tb), lambda i, ps: (0, i))],
            out_specs=pl.BlockSpec((8, tb), lambda i, ps: (0, i))),
        out_shape=jax.ShapeDtypeStruct((8, T), I32),
        compiler_params=_cparams(("arbitrary",)),
        name="dest_rows",
    )(pstart, te, rk)


SC_CORES = 2
SC_SUBCORES = 16
SC_IDX_CHUNK = 128


def _invperm(dest_flat, P):
    N = dest_flat.shape[0]
    nch = N // (SC_SUBCORES * SC_IDX_CHUNK)
    half = P // SC_CORES
    per_out = half // SC_SUBCORES
    assert N % (SC_SUBCORES * SC_IDX_CHUNK) == 0 and P % (SC_CORES * SC_SUBCORES * 8) == 0
    mesh = plsc.VectorSubcoreMesh(core_axis_name="c", subcore_axis_name="s",
                                  num_cores=SC_CORES, num_subcores=SC_SUBCORES)

    @functools.partial(
        pl.kernel, mesh=mesh, out_type=jax.ShapeDtypeStruct((P,), I32),
        scratch_types=[pltpu.VMEM((nch, SC_IDX_CHUNK), I32), pltpu.VMEM((nch, SC_IDX_CHUNK), I32),
                       pltpu.VMEM_SHARED((P,), I32), pltpu.VMEM((per_out,), I32), pltpu.SemaphoreType.DMA])
    def scatter_codes(idx_hbm, val_hbm, out_hbm, idx_v, val_v, table, stage, sem):
        core = lax.axis_index("c")
        sub = lax.axis_index("s")
        pltpu.sync_copy(idx_hbm.at[sub], idx_v)
        pltpu.sync_copy(val_hbm.at[sub], val_v)

        @pl.loop(0, nch)
        def _(j):
            pltpu.async_copy(val_v.at[j], table.at[idx_v.at[j]], sem)

        @pl.loop(0, nch)
        def _(j):
            pltpu.make_async_copy(val_v.at[j], table.at[idx_v.at[j]], sem).wait()

        plsc.subcore_barrier()
        off = pl.multiple_of(core * half + sub * per_out, 8)
        pltpu.sync_copy(table.at[pl.ds(off, per_out)], stage)
        pltpu.sync_copy(stage, out_hbm.at[pl.ds(off, per_out)])

    vals = jnp.arange(N, dtype=I32)
    shape = (SC_SUBCORES, nch, SC_IDX_CHUNK)
    return scatter_codes(dest_flat.reshape(shape), vals.reshape(shape))


def _row_copy(src, dst, sem):
    return pltpu.make_async_copy(src, dst, sem)


def _moe_kernel(be_ref, nv_ref, cc_ref, cn_ref, h2_ref, w1_ref, b1_ref, w2_ref, b2_ref, y4_ref,
                xbuf, ybuf, w1b, w2b, gsem, ssem, *, F, T, D, tme, nb):
    i = pl.program_id(0)
    s = i % 2
    ns = D // LANES

    def rows(first, n):
        return pl.ds(pl.multiple_of(first * ns, ns), n * ns)

    def gather_copy(tok, p, slot):
        return _row_copy(h2_ref.at[rows(tok, 1)], xbuf.at[slot, rows(p, 1)], gsem.at[slot])

    def token_of(code):
        return code & (T - 1) if T & (T - 1) == 0 else code % T

    def issue_rows(start_row, nv):
        ng = nv // ROW_UNROLL

        def grp(g, c):
            for j in range(ROW_UNROLL):
                start_row(g * ROW_UNROLL + j, j % 2)
            return c
        lax.fori_loop(0, ng, grp, 0)

        def one(p, c):
            start_row(p, 0)
            return c
        lax.fori_loop(ng * ROW_UNROLL, nv, one, 0)

    def start_gather(code_ref, nv, slot):
        issue_rows(lambda p, pri: gather_copy(token_of(code_ref[0, 0, p]), p, slot).start(priority=pri), nv)

    def wait_rows(make, nv):
        @pl.when(nv > 0)
        def _():
            make(nv).wait()

    def gathered(n, slot):
        return _row_copy(h2_ref.at[rows(0, n)], xbuf.at[slot, rows(0, n)], gsem.at[slot])

    def scattered(n, slot):
        return _row_copy(ybuf.at[slot, rows(0, n)], y4_ref.at[rows(0, n)], ssem.at[slot])

    @pl.when(i == 0)
    def _():
        start_gather(cc_ref, tme, 0)

    nv = nv_ref[i]
    gathered(tme, s).wait()

    @pl.when(i >= 2)
    def _():
        wait_rows(lambda n: scattered(n, s), nv_ref[i - 2])

    @pl.when(jnp.logical_or(i == 0, be_ref[i] != be_ref[jnp.maximum(i - 1, 0)]))
    def _():
        w1b[...] = w1_ref[0].astype(BF16)
        w2b[...] = w2_ref[0].astype(BF16)

    @pl.when(nv > 0)
    def _():
        x = _load_slabs(xbuf, tme, D, lead=(s,)).astype(BF16)
        for p in range(tme):
            gather_copy(token_of(cn_ref[0, 0, p]), p, 1 - s).start(priority=p % 2)
        hgu = jnp.dot(x, w1b[...], preferred_element_type=F32) + b1_ref[0]
        gate = jnp.minimum(hgu[:, :F], SWIGLU_LIMIT)
        up = jnp.clip(hgu[:, F:], -SWIGLU_LIMIT, SWIGLU_LIMIT)
        act = gate * _sigmoid(SWIGLU_ALPHA * gate) * (up + 1.0)
        y = jnp.dot(act.astype(BF16), w2b[...], preferred_element_type=F32) + b2_ref[0]
        _store_slabs(ybuf, y, lead=(s,))

        issue_rows(lambda p, pri: _row_copy(ybuf.at[s, rows(p, 1)], y4_ref.at[rows(cc_ref[0, 0, p], 1)],
                                            ssem.at[s]).start(priority=pri), nv)

    @pl.when(jnp.logical_and(nv == 0, i + 1 < nb))
    def _():
        start_gather(cn_ref, tme, 1 - s)

    @pl.when(i == nb - 1)
    def _():
        @pl.when(nv > 0)
        def _():
            gathered(tme, 1 - s).wait()
        wait_rows(lambda n: scattered(n, s), nv)
        if nb > 1:
            wait_rows(lambda n: scattered(n, 1 - s), nv_ref[i - 1])


def _experts(blk_exp, blk_valid, codes, h2s, w1, b1, w2, b2, tme):
    E, D, F2 = w1.shape
    F = F2 // 2
    ns = D // LANES
    T = h2s.shape[0] // ns
    nb = blk_exp.shape[0]
    codes3 = codes.reshape(nb, 1, tme)
    wspec = lambda shape: pl.BlockSpec(shape, lambda i, be, nv: (be[i], 0, 0))
    cspec = lambda off: pl.BlockSpec((1, 1, tme), lambda i, be, nv: (jnp.minimum(i + off, nb - 1), 0, 0),
                                     memory_space=pltpu.SMEM)
    return pl.pallas_call(
        functools.partial(_moe_kernel, F=F, T=T, D=D, tme=tme, nb=nb),
        grid_spec=pltpu.PrefetchScalarGridSpec(
            num_scalar_prefetch=2,
            grid=(nb,),
            in_specs=[cspec(0), cspec(1),
                      pl.BlockSpec(memory_space=pl.ANY),
                      wspec((1, D, F2)), wspec((1, 1, F2)), wspec((1, F, D)), wspec((1, 1, D))],
            out_specs=pl.BlockSpec(memory_space=pl.ANY),
            scratch_shapes=[pltpu.VMEM((2, tme * ns, LANES), F32), pltpu.VMEM((2, tme * ns, LANES), F32),
                            pltpu.VMEM((D, F2), BF16), pltpu.VMEM((F, D), BF16),
                            pltpu.SemaphoreType.DMA((2,)), pltpu.SemaphoreType.DMA((2,))]),
        out_shape=jax.ShapeDtypeStruct((TOP_K * T * ns, LANES), F32),
        compiler_params=_cparams(("arbitrary",)),
        name="experts",
    )(blk_exp, blk_valid, codes3, codes3, h2s, w1, b1.reshape(E, 1, F2), w2, b2.reshape(E, 1, D))


def _comb_kernel(g_ref, x1_ref, mod_ref, fw_ref, y0_ref, y1_ref, y2_ref, y3_ref, o_ref):
    g = g_ref[...]
    n, d = x1_ref.shape
    y = g[:, 0:1] * _load_slabs(y0_ref, n, d)
    for k, y_ref in ((1, y1_ref), (2, y2_ref), (3, y3_ref)):
        y = y + g[:, k:k + 1] * _load_slabs(y_ref, n, d)
    x2 = x1_ref[...] + mod_ref[0, 5:6, :] * y
    ms = jnp.mean(x2 * x2, axis=-1, keepdims=True)
    o_ref[...] = x2 * lax.rsqrt(ms + EPS) * fw_ref[...]


def _combine(gtc, x1, mod, final_w, y4, S):
    T, D = x1.shape
    tmc = 512
    nt = T // tmc
    yspec = lambda k: pl.BlockSpec((tmc * (D // LANES), LANES), lambda i: (k * nt + i, 0))
    return pl.pallas_call(
        _comb_kernel,
        grid=(nt,),
        in_specs=[pl.BlockSpec((tmc, LANES), lambda i: (i, 0)),
                  pl.BlockSpec((tmc, D), lambda i: (i, 0)),
                  pl.BlockSpec((1, 6, D), lambda i: (i * tmc // S, 0, 0)),
                  pl.BlockSpec((1, D), lambda i: (0, 0)),
                  yspec(0), yspec(1), yspec(2), yspec(3)],
        out_specs=pl.BlockSpec((tmc, D), lambda i: (i, 0)),
        out_shape=jax.ShapeDtypeStruct((T, D), F32),
        compiler_params=_cparams(("arbitrary",)),
        name="combine",
    )(gtc, x1, mod, final_w.reshape(1, D), y4, y4, y4, y4)


def _layer(x2, mod, B, S, norm1_w, w_in, conv_w, A_log, dt_bias, dn_norm_w, w_out,
           norm2_w, w_router, b_router, w1, b1, w2, b2):
    T, D = x2.shape
    qkv, qkv4, qkv16, dqkv, dz, gbc = _inproj(x2, mod, norm1_w, w_in, S)
    views = (qkv.reshape(B, 1, S, 3 * ATTN_W), qkv4, qkv16)
    branches = [_attn_branch(v, d) for v, d in zip(views, DILATIONS)]
    dn = _gdn(dqkv, dz, gbc, conv_w, A_log, dt_bias, dn_norm_w, B, S)
    (o1, l1), (o2, l2), (o3, l3) = branches
    x1, h2, te, rk, gtc, cnt = _outproj(o1, o2, o3, l1, l2, l3, dn, x2, mod, w_out, norm2_w,
                                        w_router, b_router, S)
    tme = MOE_ROWS
    P = T * TOP_K + N_EXPERTS * tme
    counts = cnt[:, 0]
    padded = (counts + tme - 1) // tme * tme
    pend = jnp.cumsum(padded)
    pstart = (pend - padded).astype(I32)
    blk_start = jnp.arange(P // tme, dtype=I32) * tme
    blk_exp = jnp.minimum(jnp.sum((pend[None, :] <= blk_start[:, None]).astype(I32), axis=1),
                          N_EXPERTS - 1).astype(I32)
    blk_valid = jnp.clip(pstart[blk_exp] + counts[blk_exp] - blk_start, 0, tme).astype(I32)
    dest = _dest(pstart, te, rk)
    codes = _invperm(dest[:TOP_K].reshape(TOP_K * T), P)
    y4 = _experts(blk_exp, blk_valid, codes, h2, w1, b1, w2, b2, tme)
    return x1, gtc, y4


def kernel(x, c, w_ada, b_ada, norm1_w, w_in, conv_w, A_log, dt_bias, dn_norm_w, w_out, norm2_w,
           w_router, b_router, w1, b1, w2, b2, final_norm_w):
    B, S, D = x.shape
    depth = w_ada.shape[0]
    assert S % (ATTN_BLK * DILATIONS[-1]) == 0 and depth == 1
    x2 = x.reshape(B * S, D)
    mod = _ada(c, w_ada[0], b_ada[0])
    x1, gtc, y4 = _layer(x2, mod, B, S, norm1_w[0], w_in[0], conv_w[0], A_log[0], dt_bias[0],
                         dn_norm_w[0], w_out[0], norm2_w[0], w_router[0], b_router[0],
                         w1[0], b1[0], w2[0], b2[0])
    out = _combine(gtc, x1, mod, final_norm_w, y4, S)
    return out.reshape(B, S, D)
```

```python
import functools

import jax
import jax.numpy as jnp
from jax import lax
from jax.experimental import pallas as pl
from jax.experimental.pallas import tpu as pltpu
from jax.experimental.pallas import tpu_sc as plsc

F32 = jnp.float32
BF16 = jnp.bfloat16
I32 = jnp.int32
HI = lax.Precision.HIGHEST

LANES = 128
ATTN_HEADS = 8
ATTN_HD = 64
ATTN_W = ATTN_HEADS * ATTN_HD
ATTN_BLK = 128
DILATIONS = (1, 4, 16)
DN_HEADS = 4
DN_D = 128
DN_W = DN_HEADS * DN_D
DN_CONV = 3 * DN_W
CONV_K = 4
DN_CHUNK = 64
N_EXPERTS = 32
TOP_K = 4
MOE_ROWS = 512
ROW_UNROLL = 8
SWIGLU_LIMIT = 7.0
SWIGLU_ALPHA = 1.702
EPS = 1e-6
NEG = -1e30
MAIN_COLS = 3 * ATTN_W + DN_CONV + DN_W

VMEM_LIMIT = 56 * 1024 * 1024


def _cparams(sem):
    return pltpu.CompilerParams(dimension_semantics=sem, vmem_limit_bytes=VMEM_LIMIT)


def _nt(a, b, **kw):
    return lax.dot_general(a, b, (((1,), (1,)), ((), ())), preferred_element_type=F32, **kw)


def _sigmoid(x):
    return 1.0 / (1.0 + jnp.exp(-x))


def _store_slabs(ref, val, lead=()):
    n, d = val.shape
    ns = d // LANES
    for c in range(ns):
        ref[lead + (pl.ds(c, n, stride=ns), slice(None))] = val[:, c * LANES:(c + 1) * LANES]


def _load_slabs(ref, n, d, lead=()):
    ns = d // LANES
    return jnp.concatenate([ref[lead + (pl.ds(c, n, stride=ns), slice(None))] for c in range(ns)], axis=1)


def _ada_kernel(c_ref, w_ref, b_ref, o_ref):
    c = c_ref[...]
    cond = c * _sigmoid(c)
    o_ref[...] = jnp.dot(cond, w_ref[...], preferred_element_type=F32, precision=HI) + b_ref[...]


def _ada(c, w_ada, b_ada):
    B, D = c.shape
    N = w_ada.shape[1]
    cp = jnp.zeros((8, D), F32).at[:B].set(c)
    tn = 1024
    out = pl.pallas_call(
        _ada_kernel,
        grid=(N // tn,),
        in_specs=[pl.BlockSpec((8, D), lambda j: (0, 0)),
                  pl.BlockSpec((D, tn), lambda j: (0, j)),
                  pl.BlockSpec((1, tn), lambda j: (0, j))],
        out_specs=pl.BlockSpec((8, tn), lambda j: (0, j)),
        out_shape=jax.ShapeDtypeStruct((8, N), F32),
        compiler_params=_cparams(("arbitrary",)),
        name="ada",
    )(cp, w_ada, b_ada.reshape(1, N))
    return out[:B].reshape(B, 6, D)


def _inproj_kernel(x_ref, mod_ref, nw_ref, wm_ref, ws_ref, qkv_ref, qkv4_ref, qkv16_ref, dqkv_ref, dz_ref,
                   gbc_ref, scr, scr2):
    x = x_ref[...]
    tm = x.shape[0]
    shift = mod_ref[0, 0:1, :]
    scale = mod_ref[0, 1:2, :]
    ms = jnp.mean(x * x, axis=-1, keepdims=True)
    h = x * lax.rsqrt(ms + EPS) * nw_ref[...]
    hb = (h * (1.0 + scale) + shift).astype(BF16)
    nl = ATTN_W // LANES
    for j in range(3):
        cs = slice(j * ATTN_W, (j + 1) * ATTN_W)
        r = jnp.dot(hb, wm_ref[:, cs], preferred_element_type=F32)
        if j == 0:
            r = r * (ATTN_HD ** -0.5)
        qkv_ref[:, cs] = r.astype(BF16)
        d1, d2 = DILATIONS[1], DILATIONS[2]
        step = d2 // d1
        for c in range(nl):
            scr[c] = r[:, c * LANES:(c + 1) * LANES]
        for res in range(d1):
            for c in range(nl):
                scr2[c, res * (tm // d1):(res + 1) * (tm // d1), :] = scr[c, pl.ds(res, tm // d1, stride=d1), :]
            part = jnp.concatenate([scr2[c, res * (tm // d1):(res + 1) * (tm // d1), :] for c in range(nl)],
                                   axis=1)
            qkv4_ref[0, res, :, cs] = part.astype(BF16)
        for res in range(d2):
            r1, q = res % d1, res // d1
            part = jnp.concatenate(
                [scr2[c, pl.ds(r1 * (tm // d1) + q, tm // d2, stride=step), :] for c in range(nl)], axis=1)
            qkv16_ref[0, res, :, cs] = part.astype(BF16)
    for j in range(3):
        c0 = 3 * ATTN_W + j * DN_W
        r = jnp.dot(hb, wm_ref[:, c0:c0 + DN_W], preferred_element_type=F32)
        dqkv_ref[:, j * DN_W:(j + 1) * DN_W] = r.astype(BF16)
    c0 = 3 * ATTN_W + DN_CONV
    dz_ref[...] = jnp.dot(hb, wm_ref[:, c0:c0 + DN_W], preferred_element_type=F32).astype(BF16)
    gbc_ref[...] = jnp.dot(hb, ws_ref[...], preferred_element_type=F32)


def _inproj(x2, mod, norm_w, w_in, S):
    T, D = x2.shape
    tm = 512
    B, nt = T // S, S // tm
    wm = w_in[:, :MAIN_COLS].astype(BF16)
    ws = jnp.zeros((D, LANES), F32).at[:, :2 * DN_HEADS].set(w_in[:, MAIN_COLS:]).astype(BF16)
    return pl.pallas_call(
        _inproj_kernel,
        grid=(T // tm,),
        in_specs=[pl.BlockSpec((tm, D), lambda i: (i, 0)),
                  pl.BlockSpec((1, 6, D), lambda i: (i * tm // S, 0, 0)),
                  pl.BlockSpec((1, D), lambda i: (0, 0)),
                  pl.BlockSpec((D, MAIN_COLS), lambda i: (0, 0)),
                  pl.BlockSpec((D, LANES), lambda i: (0, 0))],
        out_specs=[pl.BlockSpec((tm, 3 * ATTN_W), lambda i: (i, 0))]
        + [pl.BlockSpec((1, d, tm // d, 3 * ATTN_W), lambda i: (i // nt, 0, i % nt, 0)) for d in DILATIONS[1:]]
        + [pl.BlockSpec((tm, DN_CONV), lambda i: (i, 0)),
           pl.BlockSpec((tm, DN_W), lambda i: (i, 0)),
           pl.BlockSpec((tm, LANES), lambda i: (i, 0))],
        out_shape=[jax.ShapeDtypeStruct((T, 3 * ATTN_W), BF16)]
        + [jax.ShapeDtypeStruct((B, d, S // d, 3 * ATTN_W), BF16) for d in DILATIONS[1:]]
        + [jax.ShapeDtypeStruct((T, DN_CONV), BF16),
           jax.ShapeDtypeStruct((T, DN_W), BF16),
           jax.ShapeDtypeStruct((T, LANES), F32)],
        scratch_shapes=[pltpu.VMEM((ATTN_W // LANES, tm, LANES), F32),
                        pltpu.VMEM((ATTN_W // LANES, tm, LANES), F32)],
        compiler_params=_cparams(("arbitrary",)),
        name="inproj",
    )(x2, mod, norm_w.reshape(1, D), wm, ws)


def _attn_kernel(q_ref, kc_ref, kp_ref, vc_ref, vp_ref, o_ref, lse_ref, kf, vf, *, qb):
    n = pl.program_id(2)
    kf[0:ATTN_BLK, :] = kp_ref[0, 0]
    kf[ATTN_BLK:, :] = kc_ref[0, 0]
    vf[0:ATTN_BLK, :] = vp_ref[0, 0]
    vf[ATTN_BLK:, :] = vc_ref[0, 0]
    row = lax.broadcasted_iota(I32, (ATTN_BLK, 2 * ATTN_BLK), 0)
    col = lax.broadcasted_iota(I32, (ATTN_BLK, 2 * ATTN_BLK), 1)
    band = jnp.logical_or(jnp.logical_and(col < ATTN_BLK, col >= row),
                          jnp.logical_and(col >= ATTN_BLK, col - ATTN_BLK <= row))
    lane = lax.broadcasted_iota(I32, (ATTN_BLK, LANES), 1)
    lo = lane < ATTN_HD

    def sub(j, carry):
        r0 = pl.multiple_of(j * ATTN_BLK, ATTN_BLK)
        first_col = jnp.where(jnp.logical_and(n == 0, j == 0), ATTN_BLK, 0)
        mask = jnp.logical_and(band, col >= first_col)
        npair = ATTN_W // LANES
        cols = [slice(hp * LANES, (hp + 1) * LANES) for hp in range(npair)]
        heads = [(hp, half) for hp in range(npair) for half in range(2)]
        scores = []
        for hp, half in heads:
            q2 = q_ref[0, 0, pl.ds(r0, ATTN_BLK), cols[hp]]
            qm = jnp.where(lo if half == 0 else jnp.logical_not(lo), q2, jnp.zeros_like(q2))
            scores.append(jnp.where(mask, _nt(qm, kf[pl.ds(r0, 2 * ATTN_BLK), cols[hp]]), NEG))
        maxes = [jnp.max(s, axis=-1, keepdims=True) for s in scores]
        probs = [jnp.exp(s - m) for s, m in zip(scores, maxes)]
        dens = [jnp.sum(p, axis=-1, keepdims=True) for p in probs]
        accs = [jnp.dot(p.astype(BF16), vf[pl.ds(r0, 2 * ATTN_BLK), cols[hp]], preferred_element_type=F32)
                for p, (hp, _) in zip(probs, heads)]
        outs = [a / d for a, d in zip(accs, dens)]
        lse_tile = jnp.zeros((ATTN_BLK, LANES), F32)
        for h, (m, d) in enumerate(zip(maxes, dens)):
            lse_tile = jnp.where(lane == h, m + jnp.log(d), lse_tile)
        for hp in range(npair):
            o_ref[0, 0, pl.ds(r0, ATTN_BLK), cols[hp]] = jnp.where(lo, outs[2 * hp], outs[2 * hp + 1]).astype(BF16)
        lse_ref[0, 0, pl.ds(r0, ATTN_BLK), :] = lse_tile
        return carry

    lax.fori_loop(0, qb // ATTN_BLK, sub, 0, unroll=True)


def _attn_branch(qkv, d):
    B, _, L, _ = qkv.shape
    qb = min(1024, L)
    nsub = qb // ATTN_BLK
    cur = lambda c: pl.BlockSpec((1, 1, qb, ATTN_W), lambda b, r, n: (b, r, n, c))
    prev = lambda c: pl.BlockSpec((1, 1, ATTN_BLK, ATTN_W),
                                  lambda b, r, n: (b, r, jnp.maximum(n * nsub - 1, 0), c))
    return pl.pallas_call(
        functools.partial(_attn_kernel, qb=qb),
        grid=(B, d, L // qb),
        in_specs=[cur(0), cur(1), prev(1), cur(2), prev(2)],
        out_specs=[pl.BlockSpec((1, 1, qb, ATTN_W), lambda b, r, n: (b, r, n, 0)),
                   pl.BlockSpec((1, 1, qb, LANES), lambda b, r, n: (b, r, n, 0))],
        out_shape=[jax.ShapeDtypeStruct((B, d, L, ATTN_W), BF16),
                   jax.ShapeDtypeStruct((B, d, L, LANES), F32)],
        scratch_shapes=[pltpu.VMEM((qb + ATTN_BLK, ATTN_W), BF16),
                        pltpu.VMEM((qb + ATTN_BLK, ATTN_W), BF16)],
        compiler_params=_cparams(("arbitrary", "arbitrary", "arbitrary")),
        name=f"attn_d{d}",
    )(qkv, qkv, qkv, qkv, qkv)


def _gdn_kernel(x_ref, z_ref, g_ref, cw_ref, prm_ref, nw_ref, o_ref, xext, yc, s0, s1, *, rb):
    i = pl.program_id(1)

    @pl.when(i == 0)
    def _():
        xext[0:8, :] = jnp.zeros((8, DN_CONV), F32)
        s0[...] = jnp.zeros_like(s0)
        s1[...] = jnp.zeros_like(s1)

    @pl.when(i > 0)
    def _():
        xext[0:8, :] = xext[rb:rb + 8, :]

    xext[8:, :] = x_ref[0].astype(F32)
    y = cw_ref[CONV_K - 1:CONV_K, :] * xext[8:8 + rb, :]
    for j in range(CONV_K - 1):
        off = 8 - (CONV_K - 1) + j
        y = y + cw_ref[j:j + 1, :] * xext[off:off + rb, :]
    yc[...] = y * _sigmoid(y)

    C = DN_CHUNK
    H = DN_HEADS
    CW = H * C
    dot = functools.partial(jnp.dot, preferred_element_type=F32)

    def iota(shape, d):
        return lax.broadcasted_iota(I32, shape, d)

    ltri_b = jnp.where(iota((C, C), 0) >= iota((C, C), 1), 1.0, 0.0).astype(BF16)
    lane = iota((C, LANES), 1)
    blane = lane < H
    glane = jnp.logical_and(lane >= H, lane < 2 * H)
    e512 = jnp.where(jnp.logical_or(iota((LANES, DN_W), 1) // DN_D == iota((LANES, DN_W), 0),
                                    iota((LANES, DN_W), 1) // DN_D == iota((LANES, DN_W), 0) - H),
                     1.0, 0.0).astype(BF16)
    e256 = jnp.where(iota((LANES, CW), 1) // C == iota((LANES, CW), 0) - H, 1.0, 0.0).astype(BF16)
    row4 = iota((C, CW), 0)
    col4 = iota((C, CW), 1) % C
    eye4 = jnp.where(row4 == col4, 1.0, 0.0).astype(F32)
    blk = [iota((C, CW), 1) // C == h for h in range(H)]
    bd_cc = iota((CW, CW), 0) // C == iota((CW, CW), 1) // C
    bd_pair = iota((CW, CW), 0) // DN_D == iota((CW, CW), 1) // DN_D
    rt_mask = iota((CW, DN_W), 0) // C == iota((CW, DN_W), 1) // DN_D
    neg_a = -jnp.exp(prm_ref[0:1, :])
    dtb = prm_ref[1:2, :]
    nw = nw_ref[...]

    def hilo(x):
        hi = x.astype(BF16)
        return hi, (x - hi.astype(F32)).astype(BF16)

    def heads(a, w):
        return [a[:, h * w:(h + 1) * w] for h in range(H)]

    def l2n(a, mult):
        return jnp.concatenate(
            [p * (lax.rsqrt(jnp.sum(p * p, axis=-1, keepdims=True) + EPS) * mult) for p in heads(a, DN_D)],
            axis=1)

    def stack4(a):
        return jnp.concatenate([a, a, a, a], axis=0)

    zb = jnp.zeros((), BF16)
    nchunk = rb // C
    chunks = []
    for c in range(nchunk):
        rs = slice(c * C, (c + 1) * C)
        G = g_ref[0, rs, :]
        xg = G + dtb
        gv = jnp.where(glane, neg_a * (jnp.maximum(xg, 0.0) + jnp.log1p(jnp.exp(-jnp.abs(xg)))), 0.0)
        be = jnp.where(blane, _sigmoid(G), 0.0)
        g_hi, g_lo = hilo(gv)
        gcum = dot(ltri_b, g_hi) + dot(ltri_b, g_lo)
        gtot = gcum[C - 1:C, :]
        eg = jnp.where(glane, jnp.exp(gcum), 0.0)
        ek = jnp.where(glane, jnp.exp(gtot - gcum), 0.0)
        ex = dot(jnp.concatenate([be, eg, ek], axis=0).astype(BF16), e512)
        bexp, egexp, ekexp = ex[0:C], ex[C:2 * C], ex[2 * C:3 * C]
        gexp = dot(g_hi, e256) + dot(g_lo, e256)
        d_hi, d_lo = hilo(jnp.where(row4 > col4, gexp, 0.0))
        diff = dot(ltri_b, d_hi) + dot(ltri_b, d_lo)
        decay = jnp.exp(jnp.where(row4 >= col4, diff, NEG))

        q4 = l2n(yc[rs, 0:DN_W], DN_D ** -0.5)
        k4 = l2n(yc[rs, DN_W:2 * DN_W], 1.0)
        v4 = yc[rs, 2 * DN_W:3 * DN_W]
        kb4 = k4 * bexp
        vb4 = v4 * bexp
        rt = jnp.where(rt_mask, stack4(k4.astype(BF16)), jnp.zeros((), BF16))
        ai = _nt(jnp.concatenate([kb4, q4], axis=0).astype(BF16), rt)
        a4 = jnp.where(row4 > col4, ai[0:C] * decay, 0.0)
        pb = a4.astype(BF16)
        chunks.append(dict(
            rs=rs, q4=q4, k4=k4, kb4=kb4, vb4=vb4, egexp=egexp, ekexp=ekexp,
            intra=ai[C:2 * C] * decay, t4=eye4 - a4, pb=pb,
            bd=jnp.where(bd_cc, stack4(pb), zb)))

    for _ in range(5):
        for ch in chunks:
            ch["pb"] = dot(ch["pb"], ch["bd"]).astype(BF16)
        for ch in chunks:
            ch["bd"] = jnp.where(bd_cc, stack4(ch["pb"]), zb)
        for ch in chunks:
            ch["t4"] = ch["t4"] + dot(ch["t4"].astype(BF16), ch["bd"])

    for ch in chunks:
        q4, k4, kb4, vb4, egexp, ekexp = (ch[n] for n in ("q4", "k4", "kb4", "vb4", "egexp", "ekexp"))
        t4b = ch["t4"].astype(BF16)
        lstk = jnp.concatenate([jnp.where(blk[h], t4b, zb) for h in range(H)], axis=0)
        kbg4 = kb4 * egexp
        rstk = jnp.concatenate(
            [jnp.concatenate([vb, kbg], axis=1) for vb, kbg in zip(heads(vb4, DN_D), heads(kbg4, DN_D))],
            axis=0).astype(BF16)
        uw = dot(lstk, rstk)
        u4 = jnp.concatenate([uw[h * C:(h + 1) * C, 0:DN_D] for h in range(H)], axis=1)
        w4 = jnp.concatenate([uw[h * C:(h + 1) * C, DN_D:2 * DN_D] for h in range(H)], axis=1)
        ib = ch["intra"].astype(BF16)
        ch.update(
            u4=u4, wq=jnp.concatenate([w4, q4 * egexp], axis=0).astype(BF16),
            kd4=(k4 * ekexp).astype(BF16), gl4=egexp[C - 1:C, :],
            lint=jnp.concatenate([jnp.where(blk[h], ib, zb) for h in range(H)], axis=0))

    for ch in chunks:
        rs, u4, wq, kd4, gl4, lint = (ch[n] for n in ("rs", "u4", "wq", "kd4", "gl4", "lint"))
        ra = dot(wq[:, 0:CW], s0[...].astype(BF16))
        rc = dot(wq[:, CW:2 * CW], s1[...].astype(BF16))
        vn = u4 - jnp.concatenate([ra[0:C], rc[0:C]], axis=1)
        vnb = vn.astype(BF16)
        oi = dot(lint, jnp.concatenate(heads(vnb, DN_D), axis=0))
        o = (jnp.concatenate([ra[C:2 * C], rc[C:2 * C]], axis=1)
             + jnp.concatenate([oi[h * C:(h + 1) * C] for h in range(H)], axis=1))
        tn = (((0,), (0,)), ((), ()))
        s0[...] = s0[...] * gl4[:, 0:CW] + jnp.where(
            bd_pair, lax.dot_general(kd4[:, 0:CW], vnb[:, 0:CW], tn, preferred_element_type=F32), 0.0)
        s1[...] = s1[...] * gl4[:, CW:2 * CW] + jnp.where(
            bd_pair, lax.dot_general(kd4[:, CW:2 * CW], vnb[:, CW:2 * CW], tn, preferred_element_type=F32), 0.0)

        z = z_ref[0, rs, :].astype(F32)
        on = jnp.concatenate(
            [p * lax.rsqrt(jnp.mean(p * p, axis=-1, keepdims=True) + EPS) * nw for p in heads(o, DN_D)], axis=1)
        o_ref[0, rs, :] = (on * (z * _sigmoid(z))).astype(BF16)


def _gdn(dqkv, dz, gbc, conv_w, A_log, dt_bias, dn_norm_w, B, S):
    rb = 1024
    prm = jnp.zeros((2, LANES), F32)
    prm = prm.at[0, DN_HEADS:2 * DN_HEADS].set(A_log.astype(F32))
    prm = prm.at[1, DN_HEADS:2 * DN_HEADS].set(dt_bias.astype(F32))
    out = pl.pallas_call(
        functools.partial(_gdn_kernel, rb=rb),
        grid=(B, S // rb),
        in_specs=[pl.BlockSpec((1, rb, DN_CONV), lambda b, i: (b, i, 0)),
                  pl.BlockSpec((1, rb, DN_W), lambda b, i: (b, i, 0)),
                  pl.BlockSpec((1, rb, LANES), lambda b, i: (b, i, 0)),
                  pl.BlockSpec((CONV_K, DN_CONV), lambda b, i: (0, 0)),
                  pl.BlockSpec((2, LANES), lambda b, i: (0, 0)),
                  pl.BlockSpec((1, DN_D), lambda b, i: (0, 0))],
        out_specs=pl.BlockSpec((1, rb, DN_W), lambda b, i: (b, i, 0)),
        out_shape=jax.ShapeDtypeStruct((B, S, DN_W), BF16),
        scratch_shapes=[pltpu.VMEM((rb + 8, DN_CONV), F32),
                        pltpu.VMEM((rb, DN_CONV), F32),
                        pltpu.VMEM((2 * DN_D, 2 * DN_D), F32),
                        pltpu.VMEM((2 * DN_D, 2 * DN_D), F32)],
        compiler_params=_cparams(("arbitrary", "arbitrary")),
        name="gdn",
    )(dqkv.reshape(B, S, DN_CONV), dz.reshape(B, S, DN_W), gbc.reshape(B, S, LANES),
      conv_w, prm, dn_norm_w.reshape(1, DN_D))
    return out.reshape(B * S, DN_W)


def _out_kernel(o1_ref, o2_ref, o3_ref, l1_ref, l2_ref, l3_ref, dn_ref, x_ref, mod_ref, wo_ref,
                n2_ref, wr_ref, br_ref,
                x1_ref, h2_ref, te_ref, rk_ref, gtc_ref, cnt_ref, base, scr, *, tm):
    i = pl.program_id(0)

    @pl.when(i == 0)
    def _():
        base[...] = jnp.zeros_like(base)

    def natural(ref, d):
        if d == 1:
            return ref[0, 0].astype(F32)
        nl = ref.shape[-1] // LANES
        for res in range(d):
            blk = ref[0, res].astype(F32)
            for c in range(nl):
                scr[c, pl.ds(res, tm // d, stride=d), :] = blk[:, c * LANES:(c + 1) * LANES]
        return jnp.concatenate([scr[c] for c in range(nl)], axis=1)

    l1, l2, l3 = (natural(r, d) for r, d in zip((l1_ref, l2_ref, l3_ref), DILATIONS))
    mx = jnp.maximum(jnp.maximum(l1, l2), l3)
    e1, e2, e3 = jnp.exp(l1 - mx), jnp.exp(l2 - mx), jnp.exp(l3 - mx)
    zs = e1 + e2 + e3
    er = lax.broadcasted_iota(I32, (LANES, ATTN_W), 0)
    ec = lax.broadcasted_iota(I32, (LANES, ATTN_W), 1)
    expand = jnp.where(ec // ATTN_HD == er, 1.0, 0.0).astype(BF16)
    attn = jnp.zeros((tm, ATTN_W), F32)
    for e, o_ref, d in zip((e1, e2, e3), (o1_ref, o2_ref, o3_ref), DILATIONS):
        wgt = jnp.dot((e / zs).astype(BF16), expand, preferred_element_type=F32)
        attn = attn + wgt * natural(o_ref, d)
    mix = (jnp.dot(attn.astype(BF16), wo_ref[0:ATTN_W, :], preferred_element_type=F32)
           + jnp.dot(dn_ref[...], wo_ref[ATTN_W:, :], preferred_element_type=F32))
    x1 = x_ref[...] + mod_ref[0, 2:3, :] * mix
    x1_ref[...] = x1
    ms = jnp.mean(x1 * x1, axis=-1, keepdims=True)
    h2 = x1 * lax.rsqrt(ms + EPS) * n2_ref[...]
    h2 = h2 * (1.0 + mod_ref[0, 4:5, :]) + mod_ref[0, 3:4, :]
    _store_slabs(h2_ref, h2)

    lg = _nt(wr_ref[...], h2, precision=HI) + br_ref[...]
    eidx = lax.broadcasted_iota(I32, (N_EXPERTS, tm), 0)
    vals, idxs, sels = [], [], []
    for _ in range(TOP_K):
        m = jnp.max(lg, axis=0, keepdims=True)
        idx = jnp.min(jnp.where(lg == m, eidx, N_EXPERTS), axis=0, keepdims=True)
        sel = eidx == idx
        vals.append(m)
        idxs.append(idx)
        sels.append(sel)
        lg = jnp.where(sel, -jnp.inf, lg)
    ex = [jnp.exp(v - vals[0]) for v in vals]
    den = ex[0] + ex[1] + ex[2] + ex[3]
    gates = [e / den for e in ex]

    msum = jnp.zeros((N_EXPERTS, tm), F32)
    for sel in sels:
        msum = msum + jnp.where(sel, 1.0, 0.0)
    tr = lax.broadcasted_iota(I32, (tm, tm), 0)
    tc = lax.broadcasted_iota(I32, (tm, tm), 1)
    upper = jnp.where(tr <= tc, 1.0, 0.0).astype(BF16)
    incl = jnp.dot(msum.astype(BF16), upper, preferred_element_type=F32)
    pos = base[:, 0:1] + (incl - msum)
    sub8 = lax.broadcasted_iota(I32, (8, tm), 0)
    te = jnp.zeros((8, tm), I32)
    rk = jnp.zeros((8, tm), I32)
    gt = jnp.zeros((8, tm), F32)
    for k in range(TOP_K):
        rank_k = jnp.sum(jnp.where(sels[k], pos, 0.0), axis=0, keepdims=True).astype(I32)
        te = jnp.where(sub8 == k, idxs[k], te)
        rk = jnp.where(sub8 == k, rank_k, rk)
        gt = jnp.where(sub8 == k, gates[k], gt)
    te_ref[...] = te
    rk_ref[...] = rk
    gtc_ref[...] = jnp.transpose(jnp.concatenate([gt, jnp.zeros((LANES - 8, tm), F32)], axis=0))
    base[...] = base[...] + jnp.sum(msum, axis=1, keepdims=True)
    cnt_ref[...] = base[...].astype(I32)


def _outproj(o1, o2, o3, l1, l2, l3, dn, x2, mod, w_out, norm2_w, w_router, b_router, S):
    T, D = x2.shape
    tm = 512
    nt = S // tm
    row = lambda w: pl.BlockSpec((tm, w), lambda i: (i, 0))
    res = lambda d, w: pl.BlockSpec((1, d, tm // d, w), lambda i: (i // nt, 0, i % nt, 0))
    colb = pl.BlockSpec((8, tm), lambda i: (0, i))
    return pl.pallas_call(
        functools.partial(_out_kernel, tm=tm),
        grid=(T // tm,),
        in_specs=[res(d, ATTN_W) for d in DILATIONS] + [res(d, LANES) for d in DILATIONS]
        + [row(DN_W), row(D),
                  pl.BlockSpec((1, 6, D), lambda i: (i * tm // S, 0, 0)),
                  pl.BlockSpec((D, D), lambda i: (0, 0)),
                  pl.BlockSpec((1, D), lambda i: (0, 0)),
                  pl.BlockSpec((N_EXPERTS, D), lambda i: (0, 0)),
                  pl.BlockSpec((N_EXPERTS, 1), lambda i: (0, 0))],
        out_specs=[row(D), pl.BlockSpec((tm * (D // LANES), LANES), lambda i: (i, 0)), colb, colb,
                   row(LANES), pl.BlockSpec((N_EXPERTS, LANES), lambda i: (0, 0))],
        out_shape=[jax.ShapeDtypeStruct((T, D), F32),
                   jax.ShapeDtypeStruct((T * (D // LANES), LANES), F32),
                   jax.ShapeDtypeStruct((8, T), I32),
                   jax.ShapeDtypeStruct((8, T), I32),
                   jax.ShapeDtypeStruct((T, LANES), F32),
                   jax.ShapeDtypeStruct((N_EXPERTS, LANES), I32)],
        scratch_shapes=[pltpu.VMEM((N_EXPERTS, LANES), F32),
                        pltpu.VMEM((ATTN_W // LANES, tm, LANES), F32)],
        compiler_params=_cparams(("arbitrary",)),
        name="outproj_router",
    )(o1, o2, o3, l1, l2, l3, dn, x2, mod, w_out.astype(BF16), norm2_w.reshape(1, D),
      jnp.transpose(w_router), b_router.reshape(N_EXPERTS, 1))


def _dest_kernel(ps_ref, te_ref, rk_ref, d_ref):
    te = te_ref[...]
    acc = jnp.zeros(te.shape, I32)
    for e in range(N_EXPERTS):
        acc = jnp.where(te == e, ps_ref[e], acc)
    d_ref[...] = acc + rk_ref[...]


def _dest(pstart, te, rk):
    T = te.shape[1]
    tb = 2048
    return pl.pallas_call(
        _dest_kernel,
        grid_spec=pltpu.PrefetchScalarGridSpec(
            num_scalar_prefetch=1,
            grid=(T // tb,),
            in_specs=[pl.BlockSpec((8, tb), lambda i, ps: (0, i)),
                      pl.BlockSpec((8, tb), lambda i, ps: (0, i))],
            out_specs=pl.BlockSpec((8, tb), lambda i, ps: (0, i))),
        out_shape=jax.ShapeDtypeStruct((8, T), I32),
        compiler_params=_cparams(("arbitrary",)),
        name="dest_rows",
    )(pstart, te, rk)


SC_CORES = 2
SC_SUBCORES = 16
SC_IDX_CHUNK = 128


def _invperm(dest_flat, P):
    N = dest_flat.shape[0]
    nch = N // (SC_SUBCORES * SC_IDX_CHUNK)
    half = P // SC_CORES
    per_out = half // SC_SUBCORES
    assert N % (SC_SUBCORES * SC_IDX_CHUNK) == 0 and P % (SC_CORES * SC_SUBCORES * 8) == 0
    mesh = plsc.VectorSubcoreMesh(core_axis_name="c", subcore_axis_name="s",
                                  num_cores=SC_CORES, num_subcores=SC_SUBCORES)

    @functools.partial(
        pl.kernel, mesh=mesh, out_type=jax.ShapeDtypeStruct((P,), I32),
        scratch_types=[pltpu.VMEM((nch, SC_IDX_CHUNK), I32), pltpu.VMEM((nch, SC_IDX_CHUNK), I32),
                       pltpu.VMEM_SHARED((P,), I32), pltpu.VMEM((per_out,), I32), pltpu.SemaphoreType.DMA])
    def scatter_codes(idx_hbm, val_hbm, out_hbm, idx_v, val_v, table, stage, sem):
        core = lax.axis_index("c")
        sub = lax.axis_index("s")
        pltpu.sync_copy(idx_hbm.at[sub], idx_v)
        pltpu.sync_copy(val_hbm.at[sub], val_v)

        @pl.loop(0, nch)
        def _(j):
            pltpu.async_copy(val_v.at[j], table.at[idx_v.at[j]], sem)

        @pl.loop(0, nch)
        def _(j):
            pltpu.make_async_copy(val_v.at[j], table.at[idx_v.at[j]], sem).wait()

        plsc.subcore_barrier()
        off = pl.multiple_of(core * half + sub * per_out, 8)
        pltpu.sync_copy(table.at[pl.ds(off, per_out)], stage)
        pltpu.sync_copy(stage, out_hbm.at[pl.ds(off, per_out)])

    vals = jnp.arange(N, dtype=I32)
    shape = (SC_SUBCORES, nch, SC_IDX_CHUNK)
    return scatter_codes(dest_flat.reshape(shape), vals.reshape(shape))


def _row_copy(src, dst, sem):
    return pltpu.make_async_copy(src, dst, sem)


def _moe_kernel(be_ref, nv_ref, cc_ref, cn_ref, h2_ref, w1_ref, b1_ref, w2_ref, b2_ref, y4_ref,
                xbuf, ybuf, w1b, w2b, gsem, ssem, *, F, T, D, tme, nb):
    i = pl.program_id(0)
    s = i % 2
    ns = D // LANES

    def rows(first, n):
        return pl.ds(pl.multiple_of(first * ns, ns), n * ns)

    def gather_copy(tok, p, slot):
        return _row_copy(h2_ref.at[rows(tok, 1)], xbuf.at[slot, rows(p, 1)], gsem.at[slot])

    def token_of(code):
        return code & (T - 1) if T & (T - 1) == 0 else code % T

    def issue_rows(start_row, nv):
        ng = nv // ROW_UNROLL

        def grp(g, c):
            for j in range(ROW_UNROLL):
                start_row(g * ROW_UNROLL + j, j % 2)
            return c
        lax.fori_loop(0, ng, grp, 0)

        def one(p, c):
            start_row(p, 0)
            return c
        lax.fori_loop(ng * ROW_UNROLL, nv, one, 0)

    def start_gather(code_ref, nv, slot):
        issue_rows(lambda p, pri: gather_copy(token_of(code_ref[0, 0, p]), p, slot).start(priority=pri), nv)

    def wait_rows(make, nv):
        @pl.when(nv > 0)
        def _():
            make(nv).wait()

    def gathered(n, slot):
        return _row_copy(h2_ref.at[rows(0, n)], xbuf.at[slot, rows(0, n)], gsem.at[slot])

    def scattered(n, slot):
        return _row_copy(ybuf.at[slot, rows(0, n)], y4_ref.at[rows(0, n)], ssem.at[slot])

    @pl.when(i == 0)
    def _():
        start_gather(cc_ref, tme, 0)

    nv = nv_ref[i]
    gathered(tme, s).wait()

    @pl.when(i >= 2)
    def _():
        wait_rows(lambda n: scattered(n, s), nv_ref[i - 2])

    @pl.when(jnp.logical_or(i == 0, be_ref[i] != be_ref[jnp.maximum(i - 1, 0)]))
    def _():
        w1b[...] = w1_ref[0].astype(BF16)
        w2b[...] = w2_ref[0].astype(BF16)

    @pl.when(nv > 0)
    def _():
        x = _load_slabs(xbuf, tme, D, lead=(s,)).astype(BF16)
        for p in range(tme):
            gather_copy(token_of(cn_ref[0, 0, p]), p, 1 - s).start(priority=p % 2)
        hgu = jnp.dot(x, w1b[...], preferred_element_type=F32) + b1_ref[0]
        gate = jnp.minimum(hgu[:, :F], SWIGLU_LIMIT)
        up = jnp.clip(hgu[:, F:], -SWIGLU_LIMIT, SWIGLU_LIMIT)
        act = gate * _sigmoid(SWIGLU_ALPHA * gate) * (up + 1.0)
        y = jnp.dot(act.astype(BF16), w2b[...], preferred_element_type=F32) + b2_ref[0]
        _store_slabs(ybuf, y, lead=(s,))

        issue_rows(lambda p, pri: _row_copy(ybuf.at[s, rows(p, 1)], y4_ref.at[rows(cc_ref[0, 0, p], 1)],
                                            ssem.at[s]).start(priority=pri), nv)

    @pl.when(jnp.logical_and(nv == 0, i + 1 < nb))
    def _():
        start_gather(cn_ref, tme, 1 - s)

    @pl.when(i == nb - 1)
    def _():
        @pl.when(nv > 0)
        def _():
            gathered(tme, 1 - s).wait()
        wait_rows(lambda n: scattered(n, s), nv)
        if nb > 1:
            wait_rows(lambda n: scattered(n, 1 - s), nv_ref[i - 1])


def _experts(blk_exp, blk_valid, codes, h2s, w1, b1, w2, b2, tme):
    E, D, F2 = w1.shape
    F = F2 // 2
    ns = D // LANES
    T = h2s.shape[0] // ns
    nb = blk_exp.shape[0]
    codes3 = codes.reshape(nb, 1, tme)
    wspec = lambda shape: pl.BlockSpec(shape, lambda i, be, nv: (be[i], 0, 0))
    cspec = lambda off: pl.BlockSpec((1, 1, tme), lambda i, be, nv: (jnp.minimum(i + off, nb - 1), 0, 0),
                                     memory_space=pltpu.SMEM)
    return pl.pallas_call(
        functools.partial(_moe_kernel, F=F, T=T, D=D, tme=tme, nb=nb),
        grid_spec=pltpu.PrefetchScalarGridSpec(
            num_scalar_prefetch=2,
            grid=(nb,),
            in_specs=[cspec(0), cspec(1),
                      pl.BlockSpec(memory_space=pl.ANY),
                      wspec((1, D, F2)), wspec((1, 1, F2)), wspec((1, F, D)), wspec((1, 1, D))],
            out_specs=pl.BlockSpec(memory_space=pl.ANY),
            scratch_shapes=[pltpu.VMEM((2, tme * ns, LANES), F32), pltpu.VMEM((2, tme * ns, LANES), F32),
                            pltpu.VMEM((D, F2), BF16), pltpu.VMEM((F, D), BF16),
                            pltpu.SemaphoreType.DMA((2,)), pltpu.SemaphoreType.DMA((2,))]),
        out_shape=jax.ShapeDtypeStruct((TOP_K * T * ns, LANES), F32),
        compiler_params=_cparams(("arbitrary",)),
        name="experts",
    )(blk_exp, blk_valid, codes3, codes3, h2s, w1, b1.reshape(E, 1, F2), w2, b2.reshape(E, 1, D))


def _comb_kernel(g_ref, x1_ref, mod_ref, fw_ref, y0_ref, y1_ref, y2_ref, y3_ref, o_ref):
    g = g_ref[...]
    n, d = x1_ref.shape
    y = g[:, 0:1] * _load_slabs(y0_ref, n, d)
    for k, y_ref in ((1, y1_ref), (2, y2_ref), (3, y3_ref)):
        y = y + g[:, k:k + 1] * _load_slabs(y_ref, n, d)
    x2 = x1_ref[...] + mod_ref[0, 5:6, :] * y
    ms = jnp.mean(x2 * x2, axis=-1, keepdims=True)
    o_ref[...] = x2 * lax.rsqrt(ms + EPS) * fw_ref[...]


def _combine(gtc, x1, mod, final_w, y4, S):
    T, D = x1.shape
    tmc = 512
    nt = T // tmc
    yspec = lambda k: pl.BlockSpec((tmc * (D // LANES), LANES), lambda i: (k * nt + i, 0))
    return pl.pallas_call(
        _comb_kernel,
        grid=(nt,),
        in_specs=[pl.BlockSpec((tmc, LANES), lambda i: (i, 0)),
                  pl.BlockSpec((tmc, D), lambda i: (i, 0)),
                  pl.BlockSpec((1, 6, D), lambda i: (i * tmc // S, 0, 0)),
                  pl.BlockSpec((1, D), lambda i: (0, 0)),
                  yspec(0), yspec(1), yspec(2), yspec(3)],
        out_specs=pl.BlockSpec((tmc, D), lambda i: (i, 0)),
        out_shape=jax.ShapeDtypeStruct((T, D), F32),
        compiler_params=_cparams(("arbitrary",)),
        name="combine",
    )(gtc, x1, mod, final_w.reshape(1, D), y4, y4, y4, y4)


def _layer(x2, mod, B, S, norm1_w, w_in, conv_w, A_log, dt_bias, dn_norm_w, w_out,
           norm2_w, w_router, b_router, w1, b1, w2, b2):
    T, D = x2.shape
    qkv, qkv4, qkv16, dqkv, dz, gbc = _inproj(x2, mod, norm1_w, w_in, S)
    views = (qkv.reshape(B, 1, S, 3 * ATTN_W), qkv4, qkv16)
    branches = [_attn_branch(v, d) for v, d in zip(views, DILATIONS)]
    dn = _gdn(dqkv, dz, gbc, conv_w, A_log, dt_bias, dn_norm_w, B, S)
    (o1, l1), (o2, l2), (o3, l3) = branches
    x1, h2, te, rk, gtc, cnt = _outproj(o1, o2, o3, l1, l2, l3, dn, x2, mod, w_out, norm2_w,
                                        w_router, b_router, S)
    tme = MOE_ROWS
    P = T * TOP_K + N_EXPERTS * tme
    counts = cnt[:, 0]
    padded = (counts + tme - 1) // tme * tme
    pend = jnp.cumsum(padded)
    pstart = (pend - padded).astype(I32)
    blk_start = jnp.arange(P // tme, dtype=I32) * tme
    blk_exp = jnp.minimum(jnp.sum((pend[None, :] <= blk_start[:, None]).astype(I32), axis=1),
                          N_EXPERTS - 1).astype(I32)
    blk_valid = jnp.clip(pstart[blk_exp] + counts[blk_exp] - blk_start, 0, tme).astype(I32)
    dest = _dest(pstart, te, rk)
    codes = _invperm(dest[:TOP_K].reshape(TOP_K * T), P)
    y4 = _experts(blk_exp, blk_valid, codes, h2, w1, b1, w2, b2, tme)
    return x1, gtc, y4


def kernel(x, c, w_ada, b_ada, norm1_w, w_in, conv_w, A_log, dt_bias, dn_norm_w, w_out, norm2_w,
           w_router, b_router, w1, b1, w2, b2, final_norm_w):
    B, S, D = x.shape
    depth = w_ada.shape[0]
    assert S % (ATTN_BLK * DILATIONS[-1]) == 0 and depth == 1
    x2 = x.reshape(B * S, D)
    mod = _ada(c, w_ada[0], b_ada[0])
    x1, gtc, y4 = _layer(x2, mod, B, S, norm1_w[0], w_in[0], conv_w[0], A_log[0], dt_bias[0],
                         dn_norm_w[0], w_out[0], norm2_w[0], w_router[0], b_router[0],
                         w1[0], b1[0], w2[0], b2[0])
    out = _combine(gtc, x1, mod, final_norm_w, y4, S)
    return out.reshape(B, S, D)
```

```python
import functools

import jax
import jax.numpy as jnp
from jax import lax
from jax.experimental import pallas as pl
from jax.experimental.pallas import tpu as pltpu
from jax.experimental.pallas import tpu_sc as plsc

F32 = jnp.float32
BF16 = jnp.bfloat16
I32 = jnp.int32
HI = lax.Precision.HIGHEST

LANES = 128
ATTN_HEADS = 8
ATTN_HD = 64
ATTN_W = ATTN_HEADS * ATTN_HD
ATTN_BLK = 128
DILATIONS = (1, 4, 16)
DN_HEADS = 4
DN_D = 128
DN_W = DN_HEADS * DN_D
DN_CONV = 3 * DN_W
CONV_K = 4
DN_CHUNK = 64
N_EXPERTS = 32
TOP_K = 4
MOE_ROWS = 512
ROW_UNROLL = 8
SWIGLU_LIMIT = 7.0
SWIGLU_ALPHA = 1.702
EPS = 1e-6
NEG = -1e30
MAIN_COLS = 3 * ATTN_W + DN_CONV + DN_W

VMEM_LIMIT = 56 * 1024 * 1024


def _cparams(sem):
    return pltpu.CompilerParams(dimension_semantics=sem, vmem_limit_bytes=VMEM_LIMIT)


def _nt(a, b, **kw):
    return lax.dot_general(a, b, (((1,), (1,)), ((), ())), preferred_element_type=F32, **kw)


def _sigmoid(x):
    return 1.0 / (1.0 + jnp.exp(-x))


def _store_slabs(ref, val, lead=()):
    n, d = val.shape
    ns = d // LANES
    for c in range(ns):
        ref[lead + (pl.ds(c, n, stride=ns), slice(None))] = val[:, c * LANES:(c + 1) * LANES]


def _load_slabs(ref, n, d, lead=()):
    ns = d // LANES
    return jnp.concatenate([ref[lead + (pl.ds(c, n, stride=ns), slice(None))] for c in range(ns)], axis=1)


def _ada_kernel(c_ref, w_ref, b_ref, o_ref):
    c = c_ref[...]
    cond = c * _sigmoid(c)
    o_ref[...] = jnp.dot(cond, w_ref[...], preferred_element_type=F32, precision=HI) + b_ref[...]


def _ada(c, w_ada, b_ada):
    B, D = c.shape
    N = w_ada.shape[1]
    cp = jnp.zeros((8, D), F32).at[:B].set(c)
    tn = 1024
    out = pl.pallas_call(
        _ada_kernel,
        grid=(N // tn,),
        in_specs=[pl.BlockSpec((8, D), lambda j: (0, 0)),
                  pl.BlockSpec((D, tn), lambda j: (0, j)),
                  pl.BlockSpec((1, tn), lambda j: (0, j))],
        out_specs=pl.BlockSpec((8, tn), lambda j: (0, j)),
        out_shape=jax.ShapeDtypeStruct((8, N), F32),
        compiler_params=_cparams(("arbitrary",)),
        name="ada",
    )(cp, w_ada, b_ada.reshape(1, N))
    return out[:B].reshape(B, 6, D)


def _inproj_kernel(x_ref, mod_ref, nw_ref, wm_ref, ws_ref, qkv_ref, qkv4_ref, qkv16_ref, dqkv_ref, dz_ref,
                   gbc_ref, scr, scr2):
    x = x_ref[...]
    tm = x.shape[0]
    shift = mod_ref[0, 0:1, :]
    scale = mod_ref[0, 1:2, :]
    ms = jnp.mean(x * x, axis=-1, keepdims=True)
    h = x * lax.rsqrt(ms + EPS) * nw_ref[...]
    hb = (h * (1.0 + scale) + shift).astype(BF16)
    nl = ATTN_W // LANES
    proj = [jnp.dot(hb, wm_ref[:, c0:c0 + ATTN_W], preferred_element_type=F32)
            for c0 in range(0, MAIN_COLS, ATTN_W)]
    for j in range(3):
        cs = slice(j * ATTN_W, (j + 1) * ATTN_W)
        r = proj[j]
        if j == 0:
            r = r * (ATTN_HD ** -0.5)
        qkv_ref[:, cs] = r.astype(BF16)
        d1, d2 = DILATIONS[1], DILATIONS[2]
        step = d2 // d1
        for c in range(nl):
            scr[c] = r[:, c * LANES:(c + 1) * LANES]
        for res in range(d1):
            for c in range(nl):
                scr2[c, res * (tm // d1):(res + 1) * (tm // d1), :] = scr[c, pl.ds(res, tm // d1, stride=d1), :]
            part = jnp.concatenate([scr2[c, res * (tm // d1):(res + 1) * (tm // d1), :] for c in range(nl)],
                                   axis=1)
            qkv4_ref[0, res, :, cs] = part.astype(BF16)
        for res in range(d2):
            r1, q = res % d1, res // d1
            part = jnp.concatenate(
                [scr2[c, pl.ds(r1 * (tm // d1) + q, tm // d2, stride=step), :] for c in range(nl)], axis=1)
            qkv16_ref[0, res, :, cs] = part.astype(BF16)
    for j in range(3):
        dqkv_ref[:, j * DN_W:(j + 1) * DN_W] = proj[3 + j].astype(BF16)
    dz_ref[...] = proj[6].astype(BF16)
    gbc_ref[...] = jnp.dot(hb, ws_ref[...], preferred_element_type=F32)


def _inproj(x2, mod, norm_w, w_in, S):
    T, D = x2.shape
    tm = 1024
    B, nt = T // S, S // tm
    wm = w_in[:, :MAIN_COLS].astype(BF16)
    ws = jnp.zeros((D, LANES), F32).at[:, :2 * DN_HEADS].set(w_in[:, MAIN_COLS:]).astype(BF16)
    return pl.pallas_call(
        _inproj_kernel,
        grid=(T // tm,),
        in_specs=[pl.BlockSpec((tm, D), lambda i: (i, 0)),
                  pl.BlockSpec((1, 6, D), lambda i: (i * tm // S, 0, 0)),
                  pl.BlockSpec((1, D), lambda i: (0, 0)),
                  pl.BlockSpec((D, MAIN_COLS), lambda i: (0, 0)),
                  pl.BlockSpec((D, LANES), lambda i: (0, 0))],
        out_specs=[pl.BlockSpec((tm, 3 * ATTN_W), lambda i: (i, 0))]
        + [pl.BlockSpec((1, d, tm // d, 3 * ATTN_W), lambda i: (i // nt, 0, i % nt, 0)) for d in DILATIONS[1:]]
        + [pl.BlockSpec((tm, DN_CONV), lambda i: (i, 0)),
           pl.BlockSpec((tm, DN_W), lambda i: (i, 0)),
           pl.BlockSpec((tm, LANES), lambda i: (i, 0))],
        out_shape=[jax.ShapeDtypeStruct((T, 3 * ATTN_W), BF16)]
        + [jax.ShapeDtypeStruct((B, d, S // d, 3 * ATTN_W), BF16) for d in DILATIONS[1:]]
        + [jax.ShapeDtypeStruct((T, DN_CONV), BF16),
           jax.ShapeDtypeStruct((T, DN_W), BF16),
           jax.ShapeDtypeStruct((T, LANES), F32)],
        scratch_shapes=[pltpu.VMEM((ATTN_W // LANES, tm, LANES), F32),
                        pltpu.VMEM((ATTN_W // LANES, tm, LANES), F32)],
        compiler_params=_cparams(("arbitrary",)),
        name="inproj",
    )(x2, mod, norm_w.reshape(1, D), wm, ws)


def _attn_kernel(q_ref, kc_ref, kp_ref, vc_ref, vp_ref, o_ref, lse_ref, kf, vf, *, qb):
    n = pl.program_id(2)
    kf[0:ATTN_BLK, :] = kp_ref[0, 0]
    kf[ATTN_BLK:, :] = kc_ref[0, 0]
    vf[0:ATTN_BLK, :] = vp_ref[0, 0]
    vf[ATTN_BLK:, :] = vc_ref[0, 0]
    row = lax.broadcasted_iota(I32, (ATTN_BLK, 2 * ATTN_BLK), 0)
    col = lax.broadcasted_iota(I32, (ATTN_BLK, 2 * ATTN_BLK), 1)
    band = jnp.logical_or(jnp.logical_and(col < ATTN_BLK, col >= row),
                          jnp.logical_and(col >= ATTN_BLK, col - ATTN_BLK <= row))
    lane = lax.broadcasted_iota(I32, (ATTN_BLK, LANES), 1)
    lo = lane < ATTN_HD

    def sub(j, carry):
        r0 = pl.multiple_of(j * ATTN_BLK, ATTN_BLK)
        first_col = jnp.where(jnp.logical_and(n == 0, j == 0), ATTN_BLK, 0)
        mask = jnp.logical_and(band, col >= first_col)
        npair = ATTN_W // LANES
        cols = [slice(hp * LANES, (hp + 1) * LANES) for hp in range(npair)]
        heads = [(hp, half) for hp in range(npair) for half in range(2)]
        scores = []
        for hp, half in heads:
            q2 = q_ref[0, 0, pl.ds(r0, ATTN_BLK), cols[hp]]
            qm = jnp.where(lo if half == 0 else jnp.logical_not(lo), q2, jnp.zeros_like(q2))
            scores.append(jnp.where(mask, _nt(qm, kf[pl.ds(r0, 2 * ATTN_BLK), cols[hp]]), NEG))
        maxes = [jnp.max(s, axis=-1, keepdims=True) for s in scores]
        probs = [jnp.exp(s - m) for s, m in zip(scores, maxes)]
        dens = [jnp.sum(p, axis=-1, keepdims=True) for p in probs]
        accs = [jnp.dot(p.astype(BF16), vf[pl.ds(r0, 2 * ATTN_BLK), cols[hp]], preferred_element_type=F32)
                for p, (hp, _) in zip(probs, heads)]
        outs = [a / d for a, d in zip(accs, dens)]
        lse_tile = jnp.zeros((ATTN_BLK, LANES), F32)
        for h, (m, d) in enumerate(zip(maxes, dens)):
            lse_tile = jnp.where(lane == h, m + jnp.log(d), lse_tile)
        for hp in range(npair):
            o_ref[0, 0, pl.ds(r0, ATTN_BLK), cols[hp]] = jnp.where(lo, outs[2 * hp], outs[2 * hp + 1]).astype(BF16)
        lse_ref[0, 0, pl.ds(r0, ATTN_BLK), :] = lse_tile
        return carry

    lax.fori_loop(0, qb // ATTN_BLK, sub, 0, unroll=True)


def _attn_branch(qkv, d):
    B, _, L, _ = qkv.shape
    qb = min(1024, L)
    nsub = qb // ATTN_BLK
    cur = lambda c: pl.BlockSpec((1, 1, qb, ATTN_W), lambda b, r, n: (b, r, n, c))
    prev = lambda c: pl.BlockSpec((1, 1, ATTN_BLK, ATTN_W),
                                  lambda b, r, n: (b, r, jnp.maximum(n * nsub - 1, 0), c))
    return pl.pallas_call(
        functools.partial(_attn_kernel, qb=qb),
        grid=(B, d, L // qb),
        in_specs=[cur(0), cur(1), prev(1), cur(2), prev(2)],
        out_specs=[pl.BlockSpec((1, 1, qb, ATTN_W), lambda b, r, n: (b, r, n, 0)),
                   pl.BlockSpec((1, 1, qb, LANES), lambda b, r, n: (b, r, n, 0))],
        out_shape=[jax.ShapeDtypeStruct((B, d, L, ATTN_W), BF16),
                   jax.ShapeDtypeStruct((B, d, L, LANES), F32)],
        scratch_shapes=[pltpu.VMEM((qb + ATTN_BLK, ATTN_W), BF16),
                        pltpu.VMEM((qb + ATTN_BLK, ATTN_W), BF16)],
        compiler_params=_cparams(("arbitrary", "arbitrary", "arbitrary")),
        name=f"attn_d{d}",
    )(qkv, qkv, qkv, qkv, qkv)


def _gdn_kernel(x_ref, z_ref, g_ref, cw_ref, prm_ref, nw_ref, o_ref, xext, yc, s0, s1, *, rb):
    i = pl.program_id(1)

    @pl.when(i == 0)
    def _():
        xext[0:8, :] = jnp.zeros((8, DN_CONV), F32)
        s0[...] = jnp.zeros_like(s0)
        s1[...] = jnp.zeros_like(s1)

    @pl.when(i > 0)
    def _():
        xext[0:8, :] = xext[rb:rb + 8, :]

    xext[8:, :] = x_ref[0].astype(F32)
    y = cw_ref[CONV_K - 1:CONV_K, :] * xext[8:8 + rb, :]
    for j in range(CONV_K - 1):
        off = 8 - (CONV_K - 1) + j
        y = y + cw_ref[j:j + 1, :] * xext[off:off + rb, :]
    yc[...] = y * _sigmoid(y)

    C = DN_CHUNK
    H = DN_HEADS
    CW = H * C
    dot = functools.partial(jnp.dot, preferred_element_type=F32)

    def iota(shape, d):
        return lax.broadcasted_iota(I32, shape, d)

    ltri_b = jnp.where(iota((C, C), 0) >= iota((C, C), 1), 1.0, 0.0).astype(BF16)
    lane = iota((C, LANES), 1)
    blane = lane < H
    glane = jnp.logical_and(lane >= H, lane < 2 * H)
    e512 = jnp.where(jnp.logical_or(iota((LANES, DN_W), 1) // DN_D == iota((LANES, DN_W), 0),
                                    iota((LANES, DN_W), 1) // DN_D == iota((LANES, DN_W), 0) - H),
                     1.0, 0.0).astype(BF16)
    e256 = jnp.where(iota((LANES, CW), 1) // C == iota((LANES, CW), 0) - H, 1.0, 0.0).astype(BF16)
    row4 = iota((C, CW), 0)
    col4 = iota((C, CW), 1) % C
    eye4 = jnp.where(row4 == col4, 1.0, 0.0).astype(F32)
    blk = [iota((C, CW), 1) // C == h for h in range(H)]
    bd_cc = iota((CW, CW), 0) // C == iota((CW, CW), 1) // C
    bd_pair = iota((CW, CW), 0) // DN_D == iota((CW, CW), 1) // DN_D
    rt_mask = iota((CW, DN_W), 0) // C == iota((CW, DN_W), 1) // DN_D
    neg_a = -jnp.exp(prm_ref[0:1, :])
    dtb = prm_ref[1:2, :]
    nw = nw_ref[...]

    def hilo(x):
        hi = x.astype(BF16)
        return hi, (x - hi.astype(F32)).astype(BF16)

    def heads(a, w):
        return [a[:, h * w:(h + 1) * w] for h in range(H)]

    def l2n(a, mult):
        return jnp.concatenate(
            [p * (lax.rsqrt(jnp.sum(p * p, axis=-1, keepdims=True) + EPS) * mult) for p in heads(a, DN_D)],
            axis=1)

    def stack4(a):
        return jnp.concatenate([a, a, a, a], axis=0)

    zb = jnp.zeros((), BF16)
    nchunk = rb // C
    chunks = []
    for c in range(nchunk):
        rs = slice(c * C, (c + 1) * C)
        G = g_ref[0, rs, :]
        xg = G + dtb
        gv = jnp.where(glane, neg_a * (jnp.maximum(xg, 0.0) + jnp.log1p(jnp.exp(-jnp.abs(xg)))), 0.0)
        be = jnp.where(blane, _sigmoid(G), 0.0)
        g_hi, g_lo = hilo(gv)
        gcum = dot(ltri_b, g_hi) + dot(ltri_b, g_lo)
        gtot = gcum[C - 1:C, :]
        eg = jnp.where(glane, jnp.exp(gcum), 0.0)
        ek = jnp.where(glane, jnp.exp(gtot - gcum), 0.0)
        ex = dot(jnp.concatenate([be, eg, ek], axis=0).astype(BF16), e512)
        bexp, egexp, ekexp = ex[0:C], ex[C:2 * C], ex[2 * C:3 * C]
        gexp = dot(g_hi, e256) + dot(g_lo, e256)
        d_hi, d_lo = hilo(jnp.where(row4 > col4, gexp, 0.0))
        diff = dot(ltri_b, d_hi) + dot(ltri_b, d_lo)
        decay = jnp.exp(jnp.where(row4 >= col4, diff, NEG))

        q4 = l2n(yc[rs, 0:DN_W], DN_D ** -0.5)
        k4 = l2n(yc[rs, DN_W:2 * DN_W], 1.0)
        v4 = yc[rs, 2 * DN_W:3 * DN_W]
        kb4 = k4 * bexp
        vb4 = v4 * bexp
        rt = jnp.where(rt_mask, stack4(k4.astype(BF16)), jnp.zeros((), BF16))
        ai = _nt(jnp.concatenate([kb4, q4], axis=0).astype(BF16), rt)
        a4 = jnp.where(row4 > col4, ai[0:C] * decay, 0.0)
        pb = a4.astype(BF16)
        chunks.append(dict(
            rs=rs, q4=q4, k4=k4, kb4=kb4, vb4=vb4, egexp=egexp, ekexp=ekexp,
            intra=ai[C:2 * C] * decay, t4=eye4 - a4, pb=pb,
            bd=jnp.where(bd_cc, stack4(pb), zb)))

    for _ in range(5):
        for ch in chunks:
            ch["pb"] = dot(ch["pb"], ch["bd"]).astype(BF16)
        for ch in chunks:
            ch["bd"] = jnp.where(bd_cc, stack4(ch["pb"]), zb)
        for ch in chunks:
            ch["t4"] = ch["t4"] + dot(ch["t4"].astype(BF16), ch["bd"])

    for ch in chunks:
        q4, k4, kb4, vb4, egexp, ekexp = (ch[n] for n in ("q4", "k4", "kb4", "vb4", "egexp", "ekexp"))
        t4b = ch["t4"].astype(BF16)
        lstk = jnp.concatenate([jnp.where(blk[h], t4b, zb) for h in range(H)], axis=0)
        kbg4 = kb4 * egexp
        rstk = jnp.concatenate(
            [jnp.concatenate([vb, kbg], axis=1) for vb, kbg in zip(heads(vb4, DN_D), heads(kbg4, DN_D))],
            axis=0).astype(BF16)
        uw = dot(lstk, rstk)
        u4 = jnp.concatenate([uw[h * C:(h + 1) * C, 0:DN_D] for h in range(H)], axis=1)
        w4 = jnp.concatenate([uw[h * C:(h + 1) * C, DN_D:2 * DN_D] for h in range(H)], axis=1)
        ib = ch["intra"].astype(BF16)
        ch.update(
            u4=u4, wq=jnp.concatenate([w4, q4 * egexp], axis=0).astype(BF16),
            kd4=(k4 * ekexp).astype(BF16), gl4=egexp[C - 1:C, :],
            lint=jnp.concatenate([jnp.where(blk[h], ib, zb) for h in range(H)], axis=0))

    for ch in chunks:
        rs, u4, wq, kd4, gl4, lint = (ch[n] for n in ("rs", "u4", "wq", "kd4", "gl4", "lint"))
        ra = dot(wq[:, 0:CW], s0[...].astype(BF16))
        rc = dot(wq[:, CW:2 * CW], s1[...].astype(BF16))
        vn = u4 - jnp.concatenate([ra[0:C], rc[0:C]], axis=1)
        vnb = vn.astype(BF16)
        oi = dot(lint, jnp.concatenate(heads(vnb, DN_D), axis=0))
        o = (jnp.concatenate([ra[C:2 * C], rc[C:2 * C]], axis=1)
             + jnp.concatenate([oi[h * C:(h + 1) * C] for h in range(H)], axis=1))
        tn = (((0,), (0,)), ((), ()))
        s0[...] = s0[...] * gl4[:, 0:CW] + jnp.where(
            bd_pair, lax.dot_general(kd4[:, 0:CW], vnb[:, 0:CW], tn, preferred_element_type=F32), 0.0)
        s1[...] = s1[...] * gl4[:, CW:2 * CW] + jnp.where(
            bd_pair, lax.dot_general(kd4[:, CW:2 * CW], vnb[:, CW:2 * CW], tn, preferred_element_type=F32), 0.0)

        z = z_ref[0, rs, :].astype(F32)
        on = jnp.concatenate(
            [p * lax.rsqrt(jnp.mean(p * p, axis=-1, keepdims=True) + EPS) * nw for p in heads(o, DN_D)], axis=1)
        o_ref[0, rs, :] = (on * (z * _sigmoid(z))).astype(BF16)


def _gdn(dqkv, dz, gbc, conv_w, A_log, dt_bias, dn_norm_w, B, S):
    rb = 1024
    prm = jnp.zeros((2, LANES), F32)
    prm = prm.at[0, DN_HEADS:2 * DN_HEADS].set(A_log.astype(F32))
    prm = prm.at[1, DN_HEADS:2 * DN_HEADS].set(dt_bias.astype(F32))
    out = pl.pallas_call(
        functools.partial(_gdn_kernel, rb=rb),
        grid=(B, S // rb),
        in_specs=[pl.BlockSpec((1, rb, DN_CONV), lambda b, i: (b, i, 0)),
                  pl.BlockSpec((1, rb, DN_W), lambda b, i: (b, i, 0)),
                  pl.BlockSpec((1, rb, LANES), lambda b, i: (b, i, 0)),
                  pl.BlockSpec((CONV_K, DN_CONV), lambda b, i: (0, 0)),
                  pl.BlockSpec((2, LANES), lambda b, i: (0, 0)),
                  pl.BlockSpec((1, DN_D), lambda b, i: (0, 0))],
        out_specs=pl.BlockSpec((1, rb, DN_W), lambda b, i: (b, i, 0)),
        out_shape=jax.ShapeDtypeStruct((B, S, DN_W), BF16),
        scratch_shapes=[pltpu.VMEM((rb + 8, DN_CONV), F32),
                        pltpu.VMEM((rb, DN_CONV), F32),
                        pltpu.VMEM((2 * DN_D, 2 * DN_D), F32),
                        pltpu.VMEM((2 * DN_D, 2 * DN_D), F32)],
        compiler_params=_cparams(("arbitrary", "arbitrary")),
        name="gdn",
    )(dqkv.reshape(B, S, DN_CONV), dz.reshape(B, S, DN_W), gbc.reshape(B, S, LANES),
      conv_w, prm, dn_norm_w.reshape(1, DN_D))
    return out.reshape(B * S, DN_W)


def _out_kernel(o1_ref, o2_ref, o3_ref, l1_ref, l2_ref, l3_ref, dn_ref, x_ref, mod_ref, wo_ref,
                n2_ref, wr_ref, br_ref,
                x1_ref, h2_ref, te_ref, rk_ref, gtc_ref, cnt_ref, base, scr, *, tm):
    i = pl.program_id(0)

    @pl.when(i == 0)
    def _():
        base[...] = jnp.zeros_like(base)

    def natural(ref, d):
        if d == 1:
            return ref[0, 0].astype(F32)
        nl = ref.shape[-1] // LANES
        for res in range(d):
            blk = ref[0, res].astype(F32)
            for c in range(nl):
                scr[c, pl.ds(res, tm // d, stride=d), :] = blk[:, c * LANES:(c + 1) * LANES]
        return jnp.concatenate([scr[c] for c in range(nl)], axis=1)

    l1, l2, l3 = (natural(r, d) for r, d in zip((l1_ref, l2_ref, l3_ref), DILATIONS))
    mx = jnp.maximum(jnp.maximum(l1, l2), l3)
    e1, e2, e3 = jnp.exp(l1 - mx), jnp.exp(l2 - mx), jnp.exp(l3 - mx)
    zs = e1 + e2 + e3
    er = lax.broadcasted_iota(I32, (LANES, ATTN_W), 0)
    ec = lax.broadcasted_iota(I32, (LANES, ATTN_W), 1)
    expand = jnp.where(ec // ATTN_HD == er, 1.0, 0.0).astype(BF16)
    attn = jnp.zeros((tm, ATTN_W), F32)
    for e, o_ref, d in zip((e1, e2, e3), (o1_ref, o2_ref, o3_ref), DILATIONS):
        wgt = jnp.dot((e / zs).astype(BF16), expand, preferred_element_type=F32)
        attn = attn + wgt * natural(o_ref, d)
    mix = (jnp.dot(attn.astype(BF16), wo_ref[0:ATTN_W, :], preferred_element_type=F32)
           + jnp.dot(dn_ref[...], wo_ref[ATTN_W:, :], preferred_element_type=F32))
    x1 = x_ref[...] + mod_ref[0, 2:3, :] * mix
    x1_ref[...] = x1
    ms = jnp.mean(x1 * x1, axis=-1, keepdims=True)
    h2 = x1 * lax.rsqrt(ms + EPS) * n2_ref[...]
    h2 = h2 * (1.0 + mod_ref[0, 4:5, :]) + mod_ref[0, 3:4, :]
    _store_slabs(h2_ref, h2)

    lg = _nt(wr_ref[...], h2, precision=HI) + br_ref[...]
    eidx = lax.broadcasted_iota(I32, (N_EXPERTS, tm), 0)
    vals, idxs, sels = [], [], []
    for _ in range(TOP_K):
        m = jnp.max(lg, axis=0, keepdims=True)
        idx = jnp.min(jnp.where(lg == m, eidx, N_EXPERTS), axis=0, keepdims=True)
        sel = eidx == idx
        vals.append(m)
        idxs.append(idx)
        sels.append(sel)
        lg = jnp.where(sel, -jnp.inf, lg)
    ex = [jnp.exp(v - vals[0]) for v in vals]
    den = ex[0] + ex[1] + ex[2] + ex[3]
    gates = [e / den for e in ex]

    msum = jnp.zeros((N_EXPERTS, tm), F32)
    for sel in sels:
        msum = msum + jnp.where(sel, 1.0, 0.0)
    tr = lax.broadcasted_iota(I32, (tm, tm), 0)
    tc = lax.broadcasted_iota(I32, (tm, tm), 1)
    upper = jnp.where(tr <= tc, 1.0, 0.0).astype(BF16)
    incl = jnp.dot(msum.astype(BF16), upper, preferred_element_type=F32)
    pos = base[:, 0:1] + (incl - msum)
    sub8 = lax.broadcasted_iota(I32, (8, tm), 0)
    te = jnp.zeros((8, tm), I32)
    rk = jnp.zeros((8, tm), I32)
    gt = jnp.zeros((8, tm), F32)
    for k in range(TOP_K):
        rank_k = jnp.sum(jnp.where(sels[k], pos, 0.0), axis=0, keepdims=True).astype(I32)
        te = jnp.where(sub8 == k, idxs[k], te)
        rk = jnp.where(sub8 == k, rank_k, rk)
        gt = jnp.where(sub8 == k, gates[k], gt)
    te_ref[...] = te
    rk_ref[...] = rk
    gtc_ref[...] = jnp.transpose(jnp.concatenate([gt, jnp.zeros((LANES - 8, tm), F32)], axis=0))
    base[...] = base[...] + jnp.sum(msum, axis=1, keepdims=True)
    cnt_ref[...] = base[...].astype(I32)


def _outproj(o1, o2, o3, l1, l2, l3, dn, x2, mod, w_out, norm2_w, w_router, b_router, S):
    T, D = x2.shape
    tm = 1024
    nt = S // tm
    row = lambda w: pl.BlockSpec((tm, w), lambda i: (i, 0))
    res = lambda d, w: pl.BlockSpec((1, d, tm // d, w), lambda i: (i // nt, 0, i % nt, 0))
    colb = pl.BlockSpec((8, tm), lambda i: (0, i))
    return pl.pallas_call(
        functools.partial(_out_kernel, tm=tm),
        grid=(T // tm,),
        in_specs=[res(d, ATTN_W) for d in DILATIONS] + [res(d, LANES) for d in DILATIONS]
        + [row(DN_W), row(D),
                  pl.BlockSpec((1, 6, D), lambda i: (i * tm // S, 0, 0)),
                  pl.BlockSpec((D, D), lambda i: (0, 0)),
                  pl.BlockSpec((1, D), lambda i: (0, 0)),
                  pl.BlockSpec((N_EXPERTS, D), lambda i: (0, 0)),
                  pl.BlockSpec((N_EXPERTS, 1), lambda i: (0, 0))],
        out_specs=[row(D), pl.BlockSpec((tm * (D // LANES), LANES), lambda i: (i, 0)), colb, colb,
                   row(LANES), pl.BlockSpec((N_EXPERTS, LANES), lambda i: (0, 0))],
        out_shape=[jax.ShapeDtypeStruct((T, D), F32),
                   jax.ShapeDtypeStruct((T * (D // LANES), LANES), F32),
                   jax.ShapeDtypeStruct((8, T), I32),
                   jax.ShapeDtypeStruct((8, T), I32),
                   jax.ShapeDtypeStruct((T, LANES), F32),
                   jax.ShapeDtypeStruct((N_EXPERTS, LANES), I32)],
        scratch_shapes=[pltpu.VMEM((N_EXPERTS, LANES), F32),
                        pltpu.VMEM((ATTN_W // LANES, tm, LANES), F32)],
        compiler_params=_cparams(("arbitrary",)),
        name="outproj_router",
    )(o1, o2, o3, l1, l2, l3, dn, x2, mod, w_out.astype(BF16), norm2_w.reshape(1, D),
      jnp.transpose(w_router), b_router.reshape(N_EXPERTS, 1))


def _dest_kernel(ps_ref, te_ref, rk_ref, d_ref):
    te = te_ref[...]
    acc = jnp.zeros(te.shape, I32)
    for e in range(N_EXPERTS):
        acc = jnp.where(te == e, ps_ref[e], acc)
    d_ref[...] = acc + rk_ref[...]


def _dest(pstart, te, rk):
    T = te.shape[1]
    tb = 2048
    return pl.pallas_call(
        _dest_kernel,
        grid_spec=pltpu.PrefetchScalarGridSpec(
            num_scalar_prefetch=1,
            grid=(T // tb,),
            in_specs=[pl.BlockSpec((8, tb), lambda i, ps: (0, i)),
                      pl.BlockSpec((8, tb), lambda i, ps: (0, i))],
            out_specs=pl.BlockSpec((8, tb), lambda i, ps: (0, i))),
        out_shape=jax.ShapeDtypeStruct((8, T), I32),
        compiler_params=_cparams(("arbitrary",)),
        name="dest_rows",
    )(pstart, te, rk)


SC_CORES = 2
SC_SUBCORES = 16
SC_IDX_CHUNK = 128


def _invperm(dest_flat, P):
    N = dest_flat.shape[0]
    nch = N // (SC_SUBCORES * SC_IDX_CHUNK)
    half = P // SC_CORES
    per_out = half // SC_SUBCORES
    assert N % (SC_SUBCORES * SC_IDX_CHUNK) == 0 and P % (SC_CORES * SC_SUBCORES * 8) == 0
    mesh = plsc.VectorSubcoreMesh(core_axis_name="c", subcore_axis_name="s",
                                  num_cores=SC_CORES, num_subcores=SC_SUBCORES)

    @functools.partial(
        pl.kernel, mesh=mesh, out_type=jax.ShapeDtypeStruct((P,), I32),
        scratch_types=[pltpu.VMEM((nch, SC_IDX_CHUNK), I32), pltpu.VMEM((nch, SC_IDX_CHUNK), I32),
                       pltpu.VMEM_SHARED((P,), I32), pltpu.VMEM((per_out,), I32), pltpu.SemaphoreType.DMA])
    def scatter_codes(idx_hbm, val_hbm, out_hbm, idx_v, val_v, table, stage, sem):
        core = lax.axis_index("c")
        sub = lax.axis_index("s")
        pltpu.sync_copy(idx_hbm.at[sub], idx_v)
        pltpu.sync_copy(val_hbm.at[sub], val_v)

        @pl.loop(0, nch)
        def _(j):
            pltpu.async_copy(val_v.at[j], table.at[idx_v.at[j]], sem)

        @pl.loop(0, nch)
        def _(j):
            pltpu.make_async_copy(val_v.at[j], table.at[idx_v.at[j]], sem).wait()

        plsc.subcore_barrier()
        off = pl.multiple_of(core * half + sub * per_out, 8)
        pltpu.sync_copy(table.at[pl.ds(off, per_out)], stage)
        pltpu.sync_copy(stage, out_hbm.at[pl.ds(off, per_out)])

    vals = jnp.arange(N, dtype=I32)
    shape = (SC_SUBCORES, nch, SC_IDX_CHUNK)
    return scatter_codes(dest_flat.reshape(shape), vals.reshape(shape))


def _row_copy(src, dst, sem):
    return pltpu.make_async_copy(src, dst, sem)


def _moe_kernel(be_ref, nv_ref, cc_ref, cn_ref, h2_ref, w1_ref, b1_ref, w2_ref, b2_ref, y4_ref,
                xbuf, ybuf, w1b, w2b, gsem, ssem, *, F, T, D, tme, nb):
    i = pl.program_id(0)
    s = i % 2
    ns = D // LANES

    def rows(first, n):
        return pl.ds(pl.multiple_of(first * ns, ns), n * ns)

    def gather_copy(tok, p, slot):
        return _row_copy(h2_ref.at[rows(tok, 1)], xbuf.at[slot, rows(p, 1)], gsem.at[slot])

    def token_of(code):
        return code & (T - 1) if T & (T - 1) == 0 else code % T

    def issue_rows(start_row, nv):
        ng = nv // ROW_UNROLL

        def grp(g, c):
            for j in range(ROW_UNROLL):
                start_row(g * ROW_UNROLL + j, j % 2)
            return c
        lax.fori_loop(0, ng, grp, 0)

        def one(p, c):
            start_row(p, 0)
            return c
        lax.fori_loop(ng * ROW_UNROLL, nv, one, 0)

    def start_gather(code_ref, nv, slot):
        issue_rows(lambda p, pri: gather_copy(token_of(code_ref[0, 0, p]), p, slot).start(priority=pri), nv)

    def wait_rows(make, nv):
        @pl.when(nv > 0)
        def _():
            make(nv).wait()

    def gathered(n, slot):
        return _row_copy(h2_ref.at[rows(0, n)], xbuf.at[slot, rows(0, n)], gsem.at[slot])

    def scattered(n, slot):
        return _row_copy(ybuf.at[slot, rows(0, n)], y4_ref.at[rows(0, n)], ssem.at[slot])

    @pl.when(i == 0)
    def _():
        start_gather(cc_ref, tme, 0)

    nv = nv_ref[i]
    gathered(tme, s).wait()

    @pl.when(i >= 2)
    def _():
        wait_rows(lambda n: scattered(n, s), nv_ref[i - 2])

    @pl.when(jnp.logical_or(i == 0, be_ref[i] != be_ref[jnp.maximum(i - 1, 0)]))
    def _():
        w1b[...] = w1_ref[0].astype(BF16)
        w2b[...] = w2_ref[0].astype(BF16)

    @pl.when(nv > 0)
    def _():
        x = _load_slabs(xbuf, tme, D, lead=(s,)).astype(BF16)
        for p in range(tme):
            gather_copy(token_of(cn_ref[0, 0, p]), p, 1 - s).start(priority=p % 2)
        hgu = jnp.dot(x, w1b[...], preferred_element_type=F32) + b1_ref[0]
        gate = jnp.minimum(hgu[:, :F], SWIGLU_LIMIT)
        up = jnp.clip(hgu[:, F:], -SWIGLU_LIMIT, SWIGLU_LIMIT)
        act = gate * _sigmoid(SWIGLU_ALPHA * gate) * (up + 1.0)
        y = jnp.dot(act.astype(BF16), w2b[...], preferred_element_type=F32) + b2_ref[0]
        _store_slabs(ybuf, y, lead=(s,))

        issue_rows(lambda p, pri: _row_copy(ybuf.at[s, rows(p, 1)], y4_ref.at[rows(cc_ref[0, 0, p], 1)],
                                            ssem.at[s]).start(priority=pri), nv)

    @pl.when(jnp.logical_and(nv == 0, i + 1 < nb))
    def _():
        start_gather(cn_ref, tme, 1 - s)

    @pl.when(i == nb - 1)
    def _():
        @pl.when(nv > 0)
        def _():
            gathered(tme, 1 - s).wait()
        wait_rows(lambda n: scattered(n, s), nv)
        if nb > 1:
            wait_rows(lambda n: scattered(n, 1 - s), nv_ref[i - 1])


def _experts(blk_exp, blk_valid, codes, h2s, w1, b1, w2, b2, tme):
    E, D, F2 = w1.shape
    F = F2 // 2
    ns = D // LANES
    T = h2s.shape[0] // ns
    nb = blk_exp.shape[0]
    codes3 = codes.reshape(nb, 1, tme)
    wspec = lambda shape: pl.BlockSpec(shape, lambda i, be, nv: (be[i], 0, 0))
    cspec = lambda off: pl.BlockSpec((1, 1, tme), lambda i, be, nv: (jnp.minimum(i + off, nb - 1), 0, 0),
                                     memory_space=pltpu.SMEM)
    return pl.pallas_call(
        functools.partial(_moe_kernel, F=F, T=T, D=D, tme=tme, nb=nb),
        grid_spec=pltpu.PrefetchScalarGridSpec(
            num_scalar_prefetch=2,
            grid=(nb,),
            in_specs=[cspec(0), cspec(1),
                      pl.BlockSpec(memory_space=pl.ANY),
                      wspec((1, D, F2)), wspec((1, 1, F2)), wspec((1, F, D)), wspec((1, 1, D))],
            out_specs=pl.BlockSpec(memory_space=pl.ANY),
            scratch_shapes=[pltpu.VMEM((2, tme * ns, LANES), F32), pltpu.VMEM((2, tme * ns, LANES), F32),
                            pltpu.VMEM((D, F2), BF16), pltpu.VMEM((F, D), BF16),
                            pltpu.SemaphoreType.DMA((2,)), pltpu.SemaphoreType.DMA((2,))]),
        out_shape=jax.ShapeDtypeStruct((TOP_K * T * ns, LANES), F32),
        compiler_params=_cparams(("arbitrary",)),
        name="experts",
    )(blk_exp, blk_valid, codes3, codes3, h2s, w1, b1.reshape(E, 1, F2), w2, b2.reshape(E, 1, D))


def _comb_kernel(g_ref, x1_ref, mod_ref, fw_ref, y0_ref, y1_ref, y2_ref, y3_ref, o_ref):
    g = g_ref[...]
    n, d = x1_ref.shape
    y = g[:, 0:1] * _load_slabs(y0_ref, n, d)
    for k, y_ref in ((1, y1_ref), (2, y2_ref), (3, y3_ref)):
        y = y + g[:, k:k + 1] * _load_slabs(y_ref, n, d)
    x2 = x1_ref[...] + mod_ref[0, 5:6, :] * y
    ms = jnp.mean(x2 * x2, axis=-1, keepdims=True)
    o_ref[...] = x2 * lax.rsqrt(ms + EPS) * fw_ref[...]


def _combine(gtc, x1, mod, final_w, y4, S):
    T, D = x1.shape
    tmc = 512
    nt = T // tmc
    yspec = lambda k: pl.BlockSpec((tmc * (D // LANES), LANES), lambda i: (k * nt + i, 0))
    return pl.pallas_call(
        _comb_kernel,
        grid=(nt,),
        in_specs=[pl.BlockSpec((tmc, LANES), lambda i: (i, 0)),
                  pl.BlockSpec((tmc, D), lambda i: (i, 0)),
                  pl.BlockSpec((1, 6, D), lambda i: (i * tmc // S, 0, 0)),
                  pl.BlockSpec((1, D), lambda i: (0, 0)),
                  yspec(0), yspec(1), yspec(2), yspec(3)],
        out_specs=pl.BlockSpec((tmc, D), lambda i: (i, 0)),
        out_shape=jax.ShapeDtypeStruct((T, D), F32),
        compiler_params=_cparams(("arbitrary",)),
        name="combine",
    )(gtc, x1, mod, final_w.reshape(1, D), y4, y4, y4, y4)


def _layer(x2, mod, B, S, norm1_w, w_in, conv_w, A_log, dt_bias, dn_norm_w, w_out,
           norm2_w, w_router, b_router, w1, b1, w2, b2):
    T, D = x2.shape
    qkv, qkv4, qkv16, dqkv, dz, gbc = _inproj(x2, mod, norm1_w, w_in, S)
    views = (qkv.reshape(B, 1, S, 3 * ATTN_W), qkv4, qkv16)
    branches = [_attn_branch(v, d) for v, d in zip(views, DILATIONS)]
    dn = _gdn(dqkv, dz, gbc, conv_w, A_log, dt_bias, dn_norm_w, B, S)
    (o1, l1), (o2, l2), (o3, l3) = branches
    x1, h2, te, rk, gtc, cnt = _outproj(o1, o2, o3, l1, l2, l3, dn, x2, mod, w_out, norm2_w,
                                        w_router, b_router, S)
    tme = MOE_ROWS
    P = T * TOP_K + N_EXPERTS * tme
    counts = cnt[:, 0]
    padded = (counts + tme - 1) // tme * tme
    pend = jnp.cumsum(padded)
    pstart = (pend - padded).astype(I32)
    blk_start = jnp.arange(P // tme, dtype=I32) * tme
    blk_exp = jnp.minimum(jnp.sum((pend[None, :] <= blk_start[:, None]).astype(I32), axis=1),
                          N_EXPERTS - 1).astype(I32)
    blk_valid = jnp.clip(pstart[blk_exp] + counts[blk_exp] - blk_start, 0, tme).astype(I32)
    dest = _dest(pstart, te, rk)
    codes = _invperm(dest[:TOP_K].reshape(TOP_K * T), P)
    y4 = _experts(blk_exp, blk_valid, codes, h2, w1, b1, w2, b2, tme)
    return x1, gtc, y4


def kernel(x, c, w_ada, b_ada, norm1_w, w_in, conv_w, A_log, dt_bias, dn_norm_w, w_out, norm2_w,
           w_router, b_router, w1, b1, w2, b2, final_norm_w):
    B, S, D = x.shape
    depth = w_ada.shape[0]
    assert S % (ATTN_BLK * DILATIONS[-1]) == 0 and depth == 1
    x2 = x.reshape(B * S, D)
    mod = _ada(c, w_ada[0], b_ada[0])
    x1, gtc, y4 = _layer(x2, mod, B, S, norm1_w[0], w_in[0], conv_w[0], A_log[0], dt_bias[0],
                         dn_norm_w[0], w_out[0], norm2_w[0], w_router[0], b_router[0],
                         w1[0], b1[0], w2[0], b2[0])
    out = _combine(gtc, x1, mod, final_norm_w, y4, S)
    return out.reshape(B, S, D)
```

```python
import functools

import jax
import jax.numpy as jnp
from jax import lax
from jax.experimental import pallas as pl
from jax.experimental.pallas import tpu as pltpu
from jax.experimental.pallas import tpu_sc as plsc

F32 = jnp.float32
BF16 = jnp.bfloat16
I32 = jnp.int32
HI = lax.Precision.HIGHEST

LANES = 128
ATTN_HEADS = 8
ATTN_HD = 64
ATTN_W = ATTN_HEADS * ATTN_HD
ATTN_BLK = 128
DILATIONS = (1, 4, 16)
DN_HEADS = 4
DN_D = 128
DN_W = DN_HEADS * DN_D
DN_CONV = 3 * DN_W
CONV_K = 4
DN_CHUNK = 64
N_EXPERTS = 32
TOP_K = 4
MOE_ROWS = 512
ROW_UNROLL = 8
SWIGLU_LIMIT = 7.0
SWIGLU_ALPHA = 1.702
EPS = 1e-6
NEG = -1e30
MAIN_COLS = 3 * ATTN_W + DN_CONV + DN_W

VMEM_LIMIT = 56 * 1024 * 1024


def _cparams(sem):
    return pltpu.CompilerParams(dimension_semantics=sem, vmem_limit_bytes=VMEM_LIMIT)


def _nt(a, b, **kw):
    return lax.dot_general(a, b, (((1,), (1,)), ((), ())), preferred_element_type=F32, **kw)


def _sigmoid(x):
    return 1.0 / (1.0 + jnp.exp(-x))


def _store_slabs(ref, val, lead=()):
    n, d = val.shape
    ns = d // LANES
    for c in range(ns):
        ref[lead + (pl.ds(c, n, stride=ns), slice(None))] = val[:, c * LANES:(c + 1) * LANES]


def _load_slabs(ref, n, d, lead=()):
    ns = d // LANES
    return jnp.concatenate([ref[lead + (pl.ds(c, n, stride=ns), slice(None))] for c in range(ns)], axis=1)


def _ada_kernel(c_ref, w_ref, b_ref, o_ref):
    c = c_ref[...]
    cond = c * _sigmoid(c)
    o_ref[...] = jnp.dot(cond, w_ref[...], preferred_element_type=F32, precision=HI) + b_ref[...]


def _ada(c, w_ada, b_ada):
    B, D = c.shape
    N = w_ada.shape[1]
    cp = jnp.zeros((8, D), F32).at[:B].set(c)
    tn = 1024
    out = pl.pallas_call(
        _ada_kernel,
        grid=(N // tn,),
        in_specs=[pl.BlockSpec((8, D), lambda j: (0, 0)),
                  pl.BlockSpec((D, tn), lambda j: (0, j)),
                  pl.BlockSpec((1, tn), lambda j: (0, j))],
        out_specs=pl.BlockSpec((8, tn), lambda j: (0, j)),
        out_shape=jax.ShapeDtypeStruct((8, N), F32),
        compiler_params=_cparams(("arbitrary",)),
        name="ada",
    )(cp, w_ada, b_ada.reshape(1, N))
    return out[:B].reshape(B, 6, D)


def _inproj_kernel(x_ref, mod_ref, nw_ref, wm_ref, ws_ref, qkv_ref, qkv4_ref, qkv16_ref, dqkv_ref, dz_ref,
                   gbc_ref, scr, scr2):
    x = x_ref[...]
    tm = x.shape[0]
    shift = mod_ref[0, 0:1, :]
    scale = mod_ref[0, 1:2, :]
    ms = jnp.mean(x * x, axis=-1, keepdims=True)
    h = x * lax.rsqrt(ms + EPS) * nw_ref[...]
    hb = (h * (1.0 + scale) + shift).astype(BF16)
    nl = ATTN_W // LANES
    proj = [jnp.dot(hb, wm_ref[:, c0:c0 + ATTN_W], preferred_element_type=F32)
            for c0 in range(0, MAIN_COLS, ATTN_W)]
    for j in range(3):
        cs = slice(j * ATTN_W, (j + 1) * ATTN_W)
        r = proj[j]
        if j == 0:
            r = r * (ATTN_HD ** -0.5)
        qkv_ref[:, cs] = r.astype(BF16)
        d1, d2 = DILATIONS[1], DILATIONS[2]
        step = d2 // d1
        for c in range(nl):
            scr[c] = r[:, c * LANES:(c + 1) * LANES]
        for res in range(d1):
            for c in range(nl):
                scr2[c, res * (tm // d1):(res + 1) * (tm // d1), :] = scr[c, pl.ds(res, tm // d1, stride=d1), :]
            part = jnp.concatenate([scr2[c, res * (tm // d1):(res + 1) * (tm // d1), :] for c in range(nl)],
                                   axis=1)
            qkv4_ref[0, res, :, cs] = part.astype(BF16)
        for res in range(d2):
            r1, q = res % d1, res // d1
            part = jnp.concatenate(
                [scr2[c, pl.ds(r1 * (tm // d1) + q, tm // d2, stride=step), :] for c in range(nl)], axis=1)
            qkv16_ref[0, res, :, cs] = part.astype(BF16)
    for j in range(3):
        dqkv_ref[:, j * DN_W:(j + 1) * DN_W] = proj[3 + j].astype(BF16)
    dz_ref[...] = proj[6].astype(BF16)
    gbc_ref[...] = jnp.dot(hb, ws_ref[...], preferred_element_type=F32)


def _inproj(x2, mod, norm_w, w_in, S):
    T, D = x2.shape
    tm = 1024
    B, nt = T // S, S // tm
    wm = w_in[:, :MAIN_COLS].astype(BF16)
    ws = jnp.zeros((D, LANES), F32).at[:, :2 * DN_HEADS].set(w_in[:, MAIN_COLS:]).astype(BF16)
    return pl.pallas_call(
        _inproj_kernel,
        grid=(T // tm,),
        in_specs=[pl.BlockSpec((tm, D), lambda i: (i, 0)),
                  pl.BlockSpec((1, 6, D), lambda i: (i * tm // S, 0, 0)),
                  pl.BlockSpec((1, D), lambda i: (0, 0)),
                  pl.BlockSpec((D, MAIN_COLS), lambda i: (0, 0)),
                  pl.BlockSpec((D, LANES), lambda i: (0, 0))],
        out_specs=[pl.BlockSpec((tm, 3 * ATTN_W), lambda i: (i, 0))]
        + [pl.BlockSpec((1, d, tm // d, 3 * ATTN_W), lambda i: (i // nt, 0, i % nt, 0)) for d in DILATIONS[1:]]
        + [pl.BlockSpec((tm, DN_CONV), lambda i: (i, 0)),
           pl.BlockSpec((tm, DN_W), lambda i: (i, 0)),
           pl.BlockSpec((tm, LANES), lambda i: (i, 0))],
        out_shape=[jax.ShapeDtypeStruct((T, 3 * ATTN_W), BF16)]
        + [jax.ShapeDtypeStruct((B, d, S // d, 3 * ATTN_W), BF16) for d in DILATIONS[1:]]
        + [jax.ShapeDtypeStruct((T, DN_CONV), BF16),
           jax.ShapeDtypeStruct((T, DN_W), BF16),
           jax.ShapeDtypeStruct((T, LANES), F32)],
        scratch_shapes=[pltpu.VMEM((ATTN_W // LANES, tm, LANES), F32),
                        pltpu.VMEM((ATTN_W // LANES, tm, LANES), F32)],
        compiler_params=_cparams(("arbitrary",)),
        name="inproj",
    )(x2, mod, norm_w.reshape(1, D), wm, ws)


def _attn_kernel(q_ref, kc_ref, kp_ref, vc_ref, vp_ref, o_ref, lse_ref, kf, vf, *, qb):
    n = pl.program_id(2)
    kf[0:ATTN_BLK, :] = kp_ref[0, 0]
    kf[ATTN_BLK:, :] = kc_ref[0, 0]
    vf[0:ATTN_BLK, :] = vp_ref[0, 0]
    vf[ATTN_BLK:, :] = vc_ref[0, 0]
    row = lax.broadcasted_iota(I32, (ATTN_BLK, 2 * ATTN_BLK), 0)
    col = lax.broadcasted_iota(I32, (ATTN_BLK, 2 * ATTN_BLK), 1)
    band = jnp.logical_or(jnp.logical_and(col < ATTN_BLK, col >= row),
                          jnp.logical_and(col >= ATTN_BLK, col - ATTN_BLK <= row))
    lane = lax.broadcasted_iota(I32, (ATTN_BLK, LANES), 1)
    lo = lane < ATTN_HD

    def sub(j, carry):
        r0 = pl.multiple_of(j * ATTN_BLK, ATTN_BLK)
        first_col = jnp.where(jnp.logical_and(n == 0, j == 0), ATTN_BLK, 0)
        mask = jnp.logical_and(band, col >= first_col)
        npair = ATTN_W // LANES
        cols = [slice(hp * LANES, (hp + 1) * LANES) for hp in range(npair)]
        heads = [(hp, half) for hp in range(npair) for half in range(2)]
        scores = []
        for hp, half in heads:
            q2 = q_ref[0, 0, pl.ds(r0, ATTN_BLK), cols[hp]]
            qm = jnp.where(lo if half == 0 else jnp.logical_not(lo), q2, jnp.zeros_like(q2))
            scores.append(jnp.where(mask, _nt(qm, kf[pl.ds(r0, 2 * ATTN_BLK), cols[hp]]), NEG))
        maxes = [jnp.max(s, axis=-1, keepdims=True) for s in scores]
        probs = [jnp.exp(s - m) for s, m in zip(scores, maxes)]
        dens = [jnp.sum(p, axis=-1, keepdims=True) for p in probs]
        accs = [jnp.dot(p.astype(BF16), vf[pl.ds(r0, 2 * ATTN_BLK), cols[hp]], preferred_element_type=F32)
                for p, (hp, _) in zip(probs, heads)]
        outs = [a / d for a, d in zip(accs, dens)]
        lse_tile = jnp.zeros((ATTN_BLK, LANES), F32)
        for h, (m, d) in enumerate(zip(maxes, dens)):
            lse_tile = jnp.where(lane == h, m + jnp.log(d), lse_tile)
        for hp in range(npair):
            o_ref[0, 0, pl.ds(r0, ATTN_BLK), cols[hp]] = jnp.where(lo, outs[2 * hp], outs[2 * hp + 1]).astype(BF16)
        lse_ref[0, 0, pl.ds(r0, ATTN_BLK), :] = lse_tile
        return carry

    lax.fori_loop(0, qb // ATTN_BLK, sub, 0, unroll=True)


def _attn_branch(qkv, d):
    B, _, L, _ = qkv.shape
    qb = min(1024, L)
    nsub = qb // ATTN_BLK
    cur = lambda c: pl.BlockSpec((1, 1, qb, ATTN_W), lambda b, r, n: (b, r, n, c))
    prev = lambda c: pl.BlockSpec((1, 1, ATTN_BLK, ATTN_W),
                                  lambda b, r, n: (b, r, jnp.maximum(n * nsub - 1, 0), c))
    return pl.pallas_call(
        functools.partial(_attn_kernel, qb=qb),
        grid=(B, d, L // qb),
        in_specs=[cur(0), cur(1), prev(1), cur(2), prev(2)],
        out_specs=[pl.BlockSpec((1, 1, qb, ATTN_W), lambda b, r, n: (b, r, n, 0)),
                   pl.BlockSpec((1, 1, qb, LANES), lambda b, r, n: (b, r, n, 0))],
        out_shape=[jax.ShapeDtypeStruct((B, d, L, ATTN_W), BF16),
                   jax.ShapeDtypeStruct((B, d, L, LANES), F32)],
        scratch_shapes=[pltpu.VMEM((qb + ATTN_BLK, ATTN_W), BF16),
                        pltpu.VMEM((qb + ATTN_BLK, ATTN_W), BF16)],
        compiler_params=_cparams(("arbitrary", "arbitrary", "arbitrary")),
        name=f"attn_d{d}",
    )(qkv, qkv, qkv, qkv, qkv)


def _gdn_kernel(x_ref, z_ref, g_ref, cw_ref, prm_ref, nw_ref, o_ref, xext, yc, s0, s1, *, rb, nseq):
    i = pl.program_id(1)

    @pl.when(i == 0)
    def _():
        xext[:, 0:8, :] = jnp.zeros((nseq, 8, DN_CONV), F32)
        s0[...] = jnp.zeros_like(s0)
        s1[...] = jnp.zeros_like(s1)

    @pl.when(i > 0)
    def _():
        xext[:, 0:8, :] = xext[:, rb:rb + 8, :]

    for q in range(nseq):
        xext[q, 8:, :] = x_ref[q].astype(F32)
        y = cw_ref[CONV_K - 1:CONV_K, :] * xext[q, 8:8 + rb, :]
        for j in range(CONV_K - 1):
            off = 8 - (CONV_K - 1) + j
            y = y + cw_ref[j:j + 1, :] * xext[q, off:off + rb, :]
        yc[q] = y * _sigmoid(y)

    C = DN_CHUNK
    H = DN_HEADS
    CW = H * C
    dot = functools.partial(jnp.dot, preferred_element_type=F32)

    def iota(shape, d):
        return lax.broadcasted_iota(I32, shape, d)

    ltri_b = jnp.where(iota((C, C), 0) >= iota((C, C), 1), 1.0, 0.0).astype(BF16)
    lane = iota((C, LANES), 1)
    blane = lane < H
    glane = jnp.logical_and(lane >= H, lane < 2 * H)
    e512 = jnp.where(jnp.logical_or(iota((LANES, DN_W), 1) // DN_D == iota((LANES, DN_W), 0),
                                    iota((LANES, DN_W), 1) // DN_D == iota((LANES, DN_W), 0) - H),
                     1.0, 0.0).astype(BF16)
    e256 = jnp.where(iota((LANES, CW), 1) // C == iota((LANES, CW), 0) - H, 1.0, 0.0).astype(BF16)
    row4 = iota((C, CW), 0)
    col4 = iota((C, CW), 1) % C
    eye4 = jnp.where(row4 == col4, 1.0, 0.0).astype(F32)
    blk = [iota((C, CW), 1) // C == h for h in range(H)]
    bd_cc = iota((CW, CW), 0) // C == iota((CW, CW), 1) // C
    bd_pair = iota((CW, CW), 0) // DN_D == iota((CW, CW), 1) // DN_D
    rt_mask = iota((CW, DN_W), 0) // C == iota((CW, DN_W), 1) // DN_D
    neg_a = -jnp.exp(prm_ref[0:1, :])
    dtb = prm_ref[1:2, :]
    nw = nw_ref[...]

    def hilo(x):
        hi = x.astype(BF16)
        return hi, (x - hi.astype(F32)).astype(BF16)

    def heads(a, w):
        return [a[:, h * w:(h + 1) * w] for h in range(H)]

    def l2n(a, mult):
        return jnp.concatenate(
            [p * (lax.rsqrt(jnp.sum(p * p, axis=-1, keepdims=True) + EPS) * mult) for p in heads(a, DN_D)],
            axis=1)

    def stack4(a):
        return jnp.concatenate([a, a, a, a], axis=0)

    zb = jnp.zeros((), BF16)
    nchunk = rb // C
    chunks = []
    for c, sq in ((c, sq) for c in range(nchunk) for sq in range(nseq)):
        rs = slice(c * C, (c + 1) * C)
        G = g_ref[sq, rs, :]
        xg = G + dtb
        gv = jnp.where(glane, neg_a * (jnp.maximum(xg, 0.0) + jnp.log1p(jnp.exp(-jnp.abs(xg)))), 0.0)
        be = jnp.where(blane, _sigmoid(G), 0.0)
        g_hi, g_lo = hilo(gv)
        gcum = dot(ltri_b, g_hi) + dot(ltri_b, g_lo)
        gtot = gcum[C - 1:C, :]
        eg = jnp.where(glane, jnp.exp(gcum), 0.0)
        ek = jnp.where(glane, jnp.exp(gtot - gcum), 0.0)
        ex = dot(jnp.concatenate([be, eg, ek], axis=0).astype(BF16), e512)
        bexp, egexp, ekexp = ex[0:C], ex[C:2 * C], ex[2 * C:3 * C]
        gexp = dot(g_hi, e256) + dot(g_lo, e256)
        d_hi, d_lo = hilo(jnp.where(row4 > col4, gexp, 0.0))
        diff = dot(ltri_b, d_hi) + dot(ltri_b, d_lo)
        decay = jnp.exp(jnp.where(row4 >= col4, diff, NEG))

        q4 = l2n(yc[sq, rs, 0:DN_W], DN_D ** -0.5)
        k4 = l2n(yc[sq, rs, DN_W:2 * DN_W], 1.0)
        v4 = yc[sq, rs, 2 * DN_W:3 * DN_W]
        kb4 = k4 * bexp
        vb4 = v4 * bexp
        rt = jnp.where(rt_mask, stack4(k4.astype(BF16)), jnp.zeros((), BF16))
        ai = _nt(jnp.concatenate([kb4, q4], axis=0).astype(BF16), rt)
        a4 = jnp.where(row4 > col4, ai[0:C] * decay, 0.0)
        pb = a4.astype(BF16)
        chunks.append(dict(
            sq=sq, rs=rs, q4=q4, k4=k4, kb4=kb4, vb4=vb4, egexp=egexp, ekexp=ekexp,
            intra=ai[C:2 * C] * decay, t4=eye4 - a4, pb=pb,
            bd=jnp.where(bd_cc, stack4(pb), zb)))

    for _ in range(5):
        for ch in chunks:
            ch["pb"] = dot(ch["pb"], ch["bd"]).astype(BF16)
        for ch in chunks:
            ch["bd"] = jnp.where(bd_cc, stack4(ch["pb"]), zb)
        for ch in chunks:
            ch["t4"] = ch["t4"] + dot(ch["t4"].astype(BF16), ch["bd"])

    for ch in chunks:
        q4, k4, kb4, vb4, egexp, ekexp = (ch[n] for n in ("q4", "k4", "kb4", "vb4", "egexp", "ekexp"))
        t4b = ch["t4"].astype(BF16)
        lstk = jnp.concatenate([jnp.where(blk[h], t4b, zb) for h in range(H)], axis=0)
        kbg4 = kb4 * egexp
        rstk = jnp.concatenate(
            [jnp.concatenate([vb, kbg], axis=1) for vb, kbg in zip(heads(vb4, DN_D), heads(kbg4, DN_D))],
            axis=0).astype(BF16)
        uw = dot(lstk, rstk)
        u4 = jnp.concatenate([uw[h * C:(h + 1) * C, 0:DN_D] for h in range(H)], axis=1)
        w4 = jnp.concatenate([uw[h * C:(h + 1) * C, DN_D:2 * DN_D] for h in range(H)], axis=1)
        ib = ch["intra"].astype(BF16)
        ch.update(
            u4=u4, wq=jnp.concatenate([w4, q4 * egexp], axis=0).astype(BF16),
            kd4=(k4 * ekexp).astype(BF16), gl4=egexp[C - 1:C, :],
            lint=jnp.concatenate([jnp.where(blk[h], ib, zb) for h in range(H)], axis=0))

    for ch in chunks:
        sq, rs, u4, wq, kd4, gl4, lint = (ch[n] for n in ("sq", "rs", "u4", "wq", "kd4", "gl4", "lint"))
        ra = dot(wq[:, 0:CW], s0[sq].astype(BF16))
        rc = dot(wq[:, CW:2 * CW], s1[sq].astype(BF16))
        vn = u4 - jnp.concatenate([ra[0:C], rc[0:C]], axis=1)
        vnb = vn.astype(BF16)
        oi = dot(lint, jnp.concatenate(heads(vnb, DN_D), axis=0))
        o = (jnp.concatenate([ra[C:2 * C], rc[C:2 * C]], axis=1)
             + jnp.concatenate([oi[h * C:(h + 1) * C] for h in range(H)], axis=1))
        tn = (((0,), (0,)), ((), ()))
        s0[sq] = s0[sq] * gl4[:, 0:CW] + jnp.where(
            bd_pair, lax.dot_general(kd4[:, 0:CW], vnb[:, 0:CW], tn, preferred_element_type=F32), 0.0)
        s1[sq] = s1[sq] * gl4[:, CW:2 * CW] + jnp.where(
            bd_pair, lax.dot_general(kd4[:, CW:2 * CW], vnb[:, CW:2 * CW], tn, preferred_element_type=F32), 0.0)

        z = z_ref[sq, rs, :].astype(F32)
        on = jnp.concatenate(
            [p * lax.rsqrt(jnp.mean(p * p, axis=-1, keepdims=True) + EPS) * nw for p in heads(o, DN_D)], axis=1)
        o_ref[sq, rs, :] = (on * (z * _sigmoid(z))).astype(BF16)


def _gdn(dqkv, dz, gbc, conv_w, A_log, dt_bias, dn_norm_w, B, S):
    nseq = 2 if B % 2 == 0 else 1
    rb = 1024 // nseq
    prm = jnp.zeros((2, LANES), F32)
    prm = prm.at[0, DN_HEADS:2 * DN_HEADS].set(A_log.astype(F32))
    prm = prm.at[1, DN_HEADS:2 * DN_HEADS].set(dt_bias.astype(F32))
    out = pl.pallas_call(
        functools.partial(_gdn_kernel, rb=rb, nseq=nseq),
        grid=(B // nseq, S // rb),
        in_specs=[pl.BlockSpec((nseq, rb, DN_CONV), lambda b, i: (b, i, 0)),
                  pl.BlockSpec((nseq, rb, DN_W), lambda b, i: (b, i, 0)),
                  pl.BlockSpec((nseq, rb, LANES), lambda b, i: (b, i, 0)),
                  pl.BlockSpec((CONV_K, DN_CONV), lambda b, i: (0, 0)),
                  pl.BlockSpec((2, LANES), lambda b, i: (0, 0)),
                  pl.BlockSpec((1, DN_D), lambda b, i: (0, 0))],
        out_specs=pl.BlockSpec((nseq, rb, DN_W), lambda b, i: (b, i, 0)),
        out_shape=jax.ShapeDtypeStruct((B, S, DN_W), BF16),
        scratch_shapes=[pltpu.VMEM((nseq, rb + 8, DN_CONV), F32),
                        pltpu.VMEM((nseq, rb, DN_CONV), F32),
                        pltpu.VMEM((nseq, 2 * DN_D, 2 * DN_D), F32),
                        pltpu.VMEM((nseq, 2 * DN_D, 2 * DN_D), F32)],
        compiler_params=_cparams(("arbitrary", "arbitrary")),
        name="gdn",
    )(dqkv.reshape(B, S, DN_CONV), dz.reshape(B, S, DN_W), gbc.reshape(B, S, LANES),
      conv_w, prm, dn_norm_w.reshape(1, DN_D))
    return out.reshape(B * S, DN_W)


def _out_kernel(o1_ref, o2_ref, o3_ref, l1_ref, l2_ref, l3_ref, dn_ref, x_ref, mod_ref, wo_ref,
                n2_ref, wr_ref, br_ref,
                x1_ref, h2_ref, te_ref, rk_ref, gtc_ref, cnt_ref, base, scr, *, tm):
    i = pl.program_id(0)

    @pl.when(i == 0)
    def _():
        base[...] = jnp.zeros_like(base)

    def natural(ref, d):
        if d == 1:
            return ref[0, 0].astype(F32)
        nl = ref.shape[-1] // LANES
        for res in range(d):
            blk = ref[0, res].astype(F32)
            for c in range(nl):
                scr[c, pl.ds(res, tm // d, stride=d), :] = blk[:, c * LANES:(c + 1) * LANES]
        return jnp.concatenate([scr[c] for c in range(nl)], axis=1)

    l1, l2, l3 = (natural(r, d) for r, d in zip((l1_ref, l2_ref, l3_ref), DILATIONS))
    mx = jnp.maximum(jnp.maximum(l1, l2), l3)
    e1, e2, e3 = jnp.exp(l1 - mx), jnp.exp(l2 - mx), jnp.exp(l3 - mx)
    zs = e1 + e2 + e3
    er = lax.broadcasted_iota(I32, (LANES, ATTN_W), 0)
    ec = lax.broadcasted_iota(I32, (LANES, ATTN_W), 1)
    expand = jnp.where(ec // ATTN_HD == er, 1.0, 0.0).astype(BF16)
    attn = jnp.zeros((tm, ATTN_W), F32)
    for e, o_ref, d in zip((e1, e2, e3), (o1_ref, o2_ref, o3_ref), DILATIONS):
        wgt = jnp.dot((e / zs).astype(BF16), expand, preferred_element_type=F32)
        attn = attn + wgt * natural(o_ref, d)
    mix = (jnp.dot(attn.astype(BF16), wo_ref[0:ATTN_W, :], preferred_element_type=F32)
           + jnp.dot(dn_ref[...], wo_ref[ATTN_W:, :], preferred_element_type=F32))
    x1 = x_ref[...] + mod_ref[0, 2:3, :] * mix
    x1_ref[...] = x1
    ms = jnp.mean(x1 * x1, axis=-1, keepdims=True)
    h2 = x1 * lax.rsqrt(ms + EPS) * n2_ref[...]
    h2 = h2 * (1.0 + mod_ref[0, 4:5, :]) + mod_ref[0, 3:4, :]
    _store_slabs(h2_ref, h2)

    lg = _nt(wr_ref[...], h2, precision=HI) + br_ref[...]
    eidx = lax.broadcasted_iota(I32, (N_EXPERTS, tm), 0)
    vals, idxs, sels = [], [], []
    for _ in range(TOP_K):
        m = jnp.max(lg, axis=0, keepdims=True)
        idx = jnp.min(jnp.where(lg == m, eidx, N_EXPERTS), axis=0, keepdims=True)
        sel = eidx == idx
        vals.append(m)
        idxs.append(idx)
        sels.append(sel)
        lg = jnp.where(sel, -jnp.inf, lg)
    ex = [jnp.exp(v - vals[0]) for v in vals]
    den = ex[0] + ex[1] + ex[2] + ex[3]
    gates = [e / den for e in ex]

    msum = jnp.zeros((N_EXPERTS, tm), F32)
    for sel in sels:
        msum = msum + jnp.where(sel, 1.0, 0.0)
    tr = lax.broadcasted_iota(I32, (tm, tm), 0)
    tc = lax.broadcasted_iota(I32, (tm, tm), 1)
    upper = jnp.where(tr <= tc, 1.0, 0.0).astype(BF16)
    incl = jnp.dot(msum.astype(BF16), upper, preferred_element_type=F32)
    pos = base[:, 0:1] + (incl - msum)
    sub8 = lax.broadcasted_iota(I32, (8, tm), 0)
    te = jnp.zeros((8, tm), I32)
    rk = jnp.zeros((8, tm), I32)
    gt = jnp.zeros((8, tm), F32)
    for k in range(TOP_K):
        rank_k = jnp.sum(jnp.where(sels[k], pos, 0.0), axis=0, keepdims=True).astype(I32)
        te = jnp.where(sub8 == k, idxs[k], te)
        rk = jnp.where(sub8 == k, rank_k, rk)
        gt = jnp.where(sub8 == k, gates[k], gt)
    te_ref[...] = te
    rk_ref[...] = rk
    gtc_ref[...] = jnp.transpose(jnp.concatenate([gt, jnp.zeros((LANES - 8, tm), F32)], axis=0))
    base[...] = base[...] + jnp.sum(msum, axis=1, keepdims=True)
    cnt_ref[...] = base[...].astype(I32)


def _outproj(o1, o2, o3, l1, l2, l3, dn, x2, mod, w_out, norm2_w, w_router, b_router, S):
    T, D = x2.shape
    tm = 1024
    nt = S // tm
    row = lambda w: pl.BlockSpec((tm, w), lambda i: (i, 0))
    res = lambda d, w: pl.BlockSpec((1, d, tm // d, w), lambda i: (i // nt, 0, i % nt, 0))
    colb = pl.BlockSpec((8, tm), lambda i: (0, i))
    return pl.pallas_call(
        functools.partial(_out_kernel, tm=tm),
        grid=(T // tm,),
        in_specs=[res(d, ATTN_W) for d in DILATIONS] + [res(d, LANES) for d in DILATIONS]
        + [row(DN_W), row(D),
                  pl.BlockSpec((1, 6, D), lambda i: (i * tm // S, 0, 0)),
                  pl.BlockSpec((D, D), lambda i: (0, 0)),
                  pl.BlockSpec((1, D), lambda i: (0, 0)),
                  pl.BlockSpec((N_EXPERTS, D), lambda i: (0, 0)),
                  pl.BlockSpec((N_EXPERTS, 1), lambda i: (0, 0))],
        out_specs=[row(D), pl.BlockSpec((tm * (D // LANES), LANES), lambda i: (i, 0)), colb, colb,
                   row(LANES), pl.BlockSpec((N_EXPERTS, LANES), lambda i: (0, 0))],
        out_shape=[jax.ShapeDtypeStruct((T, D), F32),
                   jax.ShapeDtypeStruct((T * (D // LANES), LANES), F32),
                   jax.ShapeDtypeStruct((8, T), I32),
                   jax.ShapeDtypeStruct((8, T), I32),
                   jax.ShapeDtypeStruct((T, LANES), F32),
                   jax.ShapeDtypeStruct((N_EXPERTS, LANES), I32)],
        scratch_shapes=[pltpu.VMEM((N_EXPERTS, LANES), F32),
                        pltpu.VMEM((ATTN_W // LANES, tm, LANES), F32)],
        compiler_params=_cparams(("arbitrary",)),
        name="outproj_router",
    )(o1, o2, o3, l1, l2, l3, dn, x2, mod, w_out.astype(BF16), norm2_w.reshape(1, D),
      jnp.transpose(w_router), b_router.reshape(N_EXPERTS, 1))


def _dest_kernel(ps_ref, te_ref, rk_ref, d_ref):
    te = te_ref[...]
    acc = jnp.zeros(te.shape, I32)
    for e in range(N_EXPERTS):
        acc = jnp.where(te == e, ps_ref[e], acc)
    d_ref[...] = acc + rk_ref[...]


def _dest(pstart, te, rk):
    T = te.shape[1]
    tb = 2048
    return pl.pallas_call(
        _dest_kernel,
        grid_spec=pltpu.PrefetchScalarGridSpec(
            num_scalar_prefetch=1,
            grid=(T // tb,),
            in_specs=[pl.BlockSpec((8, tb), lambda i, ps: (0, i)),
                      pl.BlockSpec((8, tb), lambda i, ps: (0, i))],
            out_specs=pl.BlockSpec((8, tb), lambda i, ps: (0, i))),
        out_shape=jax.ShapeDtypeStruct((8, T), I32),
        compiler_params=_cparams(("arbitrary",)),
        name="dest_rows",
    )(pstart, te, rk)


SC_CORES = 2
SC_SUBCORES = 16
SC_IDX_CHUNK = 128


def _invperm(dest_flat, P):
    N = dest_flat.shape[0]
    nch = N // (SC_SUBCORES * SC_IDX_CHUNK)
    half = P // SC_CORES
    per_out = half // SC_SUBCORES
    assert N % (SC_SUBCORES * SC_IDX_CHUNK) == 0 and P % (SC_CORES * SC_SUBCORES * 8) == 0
    mesh = plsc.VectorSubcoreMesh(core_axis_name="c", subcore_axis_name="s",
                                  num_cores=SC_CORES, num_subcores=SC_SUBCORES)

    @functools.partial(
        pl.kernel, mesh=mesh, out_type=jax.ShapeDtypeStruct((P,), I32),
        scratch_types=[pltpu.VMEM((nch, SC_IDX_CHUNK), I32), pltpu.VMEM((nch, SC_IDX_CHUNK), I32),
                       pltpu.VMEM_SHARED((P,), I32), pltpu.VMEM((per_out,), I32), pltpu.SemaphoreType.DMA])
    def scatter_codes(idx_hbm, val_hbm, out_hbm, idx_v, val_v, table, stage, sem):
        core = lax.axis_index("c")
        sub = lax.axis_index("s")
        pltpu.sync_copy(idx_hbm.at[sub], idx_v)
        pltpu.sync_copy(val_hbm.at[sub], val_v)

        @pl.loop(0, nch)
        def _(j):
            pltpu.async_copy(val_v.at[j], table.at[idx_v.at[j]], sem)

        @pl.loop(0, nch)
        def _(j):
            pltpu.make_async_copy(val_v.at[j], table.at[idx_v.at[j]], sem).wait()

        plsc.subcore_barrier()
        off = pl.multiple_of(core * half + sub * per_out, 8)
        pltpu.sync_copy(table.at[pl.ds(off, per_out)], stage)
        pltpu.sync_copy(stage, out_hbm.at[pl.ds(off, per_out)])

    vals = jnp.arange(N, dtype=I32)
    shape = (SC_SUBCORES, nch, SC_IDX_CHUNK)
    return scatter_codes(dest_flat.reshape(shape), vals.reshape(shape))


def _row_copy(src, dst, sem):
    return pltpu.make_async_copy(src, dst, sem)


def _moe_kernel(be_ref, nv_ref, cc_ref, cn_ref, h2_ref, w1_ref, b1_ref, w2_ref, b2_ref, y4_ref,
                xbuf, ybuf, w1b, w2b, gsem, ssem, *, F, T, D, tme, nb):
    i = pl.program_id(0)
    s = i % 2
    ns = D // LANES

    def rows(first, n):
        return pl.ds(pl.multiple_of(first * ns, ns), n * ns)

    def gather_copy(tok, p, slot):
        return _row_copy(h2_ref.at[rows(tok, 1)], xbuf.at[slot, rows(p, 1)], gsem.at[slot])

    def token_of(code):
        return code & (T - 1) if T & (T - 1) == 0 else code % T

    def issue_rows(start_row, nv):
        ng = nv // ROW_UNROLL

        def grp(g, c):
            for j in range(ROW_UNROLL):
                start_row(g * ROW_UNROLL + j, j % 2)
            return c
        lax.fori_loop(0, ng, grp, 0)

        def one(p, c):
            start_row(p, 0)
            return c
        lax.fori_loop(ng * ROW_UNROLL, nv, one, 0)

    def start_gather(code_ref, nv, slot):
        issue_rows(lambda p, pri: gather_copy(token_of(code_ref[0, 0, p]), p, slot).start(priority=pri), nv)

    def wait_rows(make, nv):
        @pl.when(nv > 0)
        def _():
            make(nv).wait()

    def gathered(n, slot):
        return _row_copy(h2_ref.at[rows(0, n)], xbuf.at[slot, rows(0, n)], gsem.at[slot])

    def scattered(n, slot):
        return _row_copy(ybuf.at[slot, rows(0, n)], y4_ref.at[rows(0, n)], ssem.at[slot])

    @pl.when(i == 0)
    def _():
        start_gather(cc_ref, tme, 0)

    nv = nv_ref[i]
    gathered(tme, s).wait()

    @pl.when(i >= 2)
    def _():
        wait_rows(lambda n: scattered(n, s), nv_ref[i - 2])

    @pl.when(jnp.logical_or(i == 0, be_ref[i] != be_ref[jnp.maximum(i - 1, 0)]))
    def _():
        w1b[...] = w1_ref[0].astype(BF16)
        w2b[...] = w2_ref[0].astype(BF16)

    @pl.when(nv > 0)
    def _():
        x = _load_slabs(xbuf, tme, D, lead=(s,)).astype(BF16)
        for p in range(tme):
            gather_copy(token_of(cn_ref[0, 0, p]), p, 1 - s).start(priority=p % 2)
        hgu = jnp.dot(x, w1b[...], preferred_element_type=F32) + b1_ref[0]
        gate = jnp.minimum(hgu[:, :F], SWIGLU_LIMIT)
        up = jnp.clip(hgu[:, F:], -SWIGLU_LIMIT, SWIGLU_LIMIT)
        act = gate * _sigmoid(SWIGLU_ALPHA * gate) * (up + 1.0)
        y = jnp.dot(act.astype(BF16), w2b[...], preferred_element_type=F32) + b2_ref[0]
        _store_slabs(ybuf, y, lead=(s,))

        issue_rows(lambda p, pri: _row_copy(ybuf.at[s, rows(p, 1)], y4_ref.at[rows(cc_ref[0, 0, p], 1)],
                                            ssem.at[s]).start(priority=pri), nv)

    @pl.when(jnp.logical_and(nv == 0, i + 1 < nb))
    def _():
        start_gather(cn_ref, tme, 1 - s)

    @pl.when(i == nb - 1)
    def _():
        @pl.when(nv > 0)
        def _():
            gathered(tme, 1 - s).wait()
        wait_rows(lambda n: scattered(n, s), nv)
        if nb > 1:
            wait_rows(lambda n: scattered(n, 1 - s), nv_ref[i - 1])


def _experts(blk_exp, blk_valid, codes, h2s, w1, b1, w2, b2, tme):
    E, D, F2 = w1.shape
    F = F2 // 2
    ns = D // LANES
    T = h2s.shape[0] // ns
    nb = blk_exp.shape[0]
    codes3 = codes.reshape(nb, 1, tme)
    wspec = lambda shape: pl.BlockSpec(shape, lambda i, be, nv: (be[i], 0, 0))
    cspec = lambda off: pl.BlockSpec((1, 1, tme), lambda i, be, nv: (jnp.minimum(i + off, nb - 1), 0, 0),
                                     memory_space=pltpu.SMEM)
    return pl.pallas_call(
        functools.partial(_moe_kernel, F=F, T=T, D=D, tme=tme, nb=nb),
        grid_spec=pltpu.PrefetchScalarGridSpec(
            num_scalar_prefetch=2,
            grid=(nb,),
            in_specs=[cspec(0), cspec(1),
                      pl.BlockSpec(memory_space=pl.ANY),
                      wspec((1, D, F2)), wspec((1, 1, F2)), wspec((1, F, D)), wspec((1, 1, D))],
            out_specs=pl.BlockSpec(memory_space=pl.ANY),
            scratch_shapes=[pltpu.VMEM((2, tme * ns, LANES), F32), pltpu.VMEM((2, tme * ns, LANES), F32),
                            pltpu.VMEM((D, F2), BF16), pltpu.VMEM((F, D), BF16),
                            pltpu.SemaphoreType.DMA((2,)), pltpu.SemaphoreType.DMA((2,))]),
        out_shape=jax.ShapeDtypeStruct((TOP_K * T * ns, LANES), F32),
        compiler_params=_cparams(("arbitrary",)),
        name="experts",
    )(blk_exp, blk_valid, codes3, codes3, h2s, w1, b1.reshape(E, 1, F2), w2, b2.reshape(E, 1, D))


def _comb_kernel(g_ref, x1_ref, mod_ref, fw_ref, y0_ref, y1_ref, y2_ref, y3_ref, o_ref):
    g = g_ref[...]
    n, d = x1_ref.shape
    y = g[:, 0:1] * _load_slabs(y0_ref, n, d)
    for k, y_ref in ((1, y1_ref), (2, y2_ref), (3, y3_ref)):
        y = y + g[:, k:k + 1] * _load_slabs(y_ref, n, d)
    x2 = x1_ref[...] + mod_ref[0, 5:6, :] * y
    ms = jnp.mean(x2 * x2, axis=-1, keepdims=True)
    o_ref[...] = x2 * lax.rsqrt(ms + EPS) * fw_ref[...]


def _combine(gtc, x1, mod, final_w, y4, S):
    T, D = x1.shape
    tmc = 512
    nt = T // tmc
    yspec = lambda k: pl.BlockSpec((tmc * (D // LANES), LANES), lambda i: (k * nt + i, 0))
    return pl.pallas_call(
        _comb_kernel,
        grid=(nt,),
        in_specs=[pl.BlockSpec((tmc, LANES), lambda i: (i, 0)),
                  pl.BlockSpec((tmc, D), lambda i: (i, 0)),
                  pl.BlockSpec((1, 6, D), lambda i: (i * tmc // S, 0, 0)),
                  pl.BlockSpec((1, D), lambda i: (0, 0)),
                  yspec(0), yspec(1), yspec(2), yspec(3)],
        out_specs=pl.BlockSpec((tmc, D), lambda i: (i, 0)),
        out_shape=jax.ShapeDtypeStruct((T, D), F32),
        compiler_params=_cparams(("arbitrary",)),
        name="combine",
    )(gtc, x1, mod, final_w.reshape(1, D), y4, y4, y4, y4)


def _layer(x2, mod, B, S, norm1_w, w_in, conv_w, A_log, dt_bias, dn_norm_w, w_out,
           norm2_w, w_router, b_router, w1, b1, w2, b2):
    T, D = x2.shape
    qkv, qkv4, qkv16, dqkv, dz, gbc = _inproj(x2, mod, norm1_w, w_in, S)
    views = (qkv.reshape(B, 1, S, 3 * ATTN_W), qkv4, qkv16)
    branches = [_attn_branch(v, d) for v, d in zip(views, DILATIONS)]
    dn = _gdn(dqkv, dz, gbc, conv_w, A_log, dt_bias, dn_norm_w, B, S)
    (o1, l1), (o2, l2), (o3, l3) = branches
    x1, h2, te, rk, gtc, cnt = _outproj(o1, o2, o3, l1, l2, l3, dn, x2, mod, w_out, norm2_w,
                                        w_router, b_router, S)
    tme = MOE_ROWS
    P = T * TOP_K + N_EXPERTS * tme
    counts = cnt[:, 0]
    padded = (counts + tme - 1) // tme * tme
    pend = jnp.cumsum(padded)
    pstart = (pend - padded).astype(I32)
    blk_start = jnp.arange(P // tme, dtype=I32) * tme
    blk_exp = jnp.minimum(jnp.sum((pend[None, :] <= blk_start[:, None]).astype(I32), axis=1),
                          N_EXPERTS - 1).astype(I32)
    blk_valid = jnp.clip(pstart[blk_exp] + counts[blk_exp] - blk_start, 0, tme).astype(I32)
    dest = _dest(pstart, te, rk)
    codes = _invperm(dest[:TOP_K].reshape(TOP_K * T), P)
    y4 = _experts(blk_exp, blk_valid, codes, h2, w1, b1, w2, b2, tme)
    return x1, gtc, y4


def kernel(x, c, w_ada, b_ada, norm1_w, w_in, conv_w, A_log, dt_bias, dn_norm_w, w_out, norm2_w,
           w_router, b_router, w1, b1, w2, b2, final_norm_w):
    B, S, D = x.shape
    depth = w_ada.shape[0]
    assert S % (ATTN_BLK * DILATIONS[-1]) == 0 and depth == 1
    x2 = x.reshape(B * S, D)
    mod = _ada(c, w_ada[0], b_ada[0])
    x1, gtc, y4 = _layer(x2, mod, B, S, norm1_w[0], w_in[0], conv_w[0], A_log[0], dt_bias[0],
                         dn_norm_w[0], w_out[0], norm2_w[0], w_router[0], b_router[0],
                         w1[0], b1[0], w2[0], b2[0])
    out = _combine(gtc, x1, mod, final_norm_w, y4, S)
    return out.reshape(B, S, D)
```

```python
import functools

import jax
import jax.numpy as jnp
from jax import lax
from jax.experimental import pallas as pl
from jax.experimental.pallas import tpu as pltpu
from jax.experimental.pallas import tpu_sc as plsc

F32 = jnp.float32
BF16 = jnp.bfloat16
I32 = jnp.int32
HI = lax.Precision.HIGHEST

LANES = 128
ATTN_HEADS = 8
ATTN_HD = 64
ATTN_W = ATTN_HEADS * ATTN_HD
ATTN_BLK = 128
DILATIONS = (1, 4, 16)
DN_HEADS = 4
DN_D = 128
DN_W = DN_HEADS * DN_D
DN_CONV = 3 * DN_W
CONV_K = 4
DN_CHUNK = 64
N_EXPERTS = 32
TOP_K = 4
MOE_ROWS = 512
ROW_UNROLL = 8
SWIGLU_LIMIT = 7.0
SWIGLU_ALPHA = 1.702
EPS = 1e-6
NEG = -1e30
MAIN_COLS = 3 * ATTN_W + DN_CONV + DN_W

VMEM_LIMIT = 56 * 1024 * 1024


def _cparams(sem):
    return pltpu.CompilerParams(dimension_semantics=sem, vmem_limit_bytes=VMEM_LIMIT)


def _nt(a, b, **kw):
    return lax.dot_general(a, b, (((1,), (1,)), ((), ())), preferred_element_type=F32, **kw)


def _sigmoid(x):
    return 1.0 / (1.0 + jnp.exp(-x))


def _store_slabs(ref, val, lead=()):
    n, d = val.shape
    ns = d // LANES
    for c in range(ns):
        ref[lead + (pl.ds(c, n, stride=ns), slice(None))] = val[:, c * LANES:(c + 1) * LANES]


def _load_slabs(ref, n, d, lead=()):
    ns = d // LANES
    return jnp.concatenate([ref[lead + (pl.ds(c, n, stride=ns), slice(None))] for c in range(ns)], axis=1)


def _ada_kernel(c_ref, w_ref, b_ref, o_ref):
    c = c_ref[...]
    cond = c * _sigmoid(c)
    o_ref[...] = jnp.dot(cond, w_ref[...], preferred_element_type=F32, precision=HI) + b_ref[...]


def _ada(c, w_ada, b_ada):
    B, D = c.shape
    N = w_ada.shape[1]
    cp = jnp.zeros((8, D), F32).at[:B].set(c)
    tn = 1024
    out = pl.pallas_call(
        _ada_kernel,
        grid=(N // tn,),
        in_specs=[pl.BlockSpec((8, D), lambda j: (0, 0)),
                  pl.BlockSpec((D, tn), lambda j: (0, j)),
                  pl.BlockSpec((1, tn), lambda j: (0, j))],
        out_specs=pl.BlockSpec((8, tn), lambda j: (0, j)),
        out_shape=jax.ShapeDtypeStruct((8, N), F32),
        compiler_params=_cparams(("arbitrary",)),
        name="ada",
    )(cp, w_ada, b_ada.reshape(1, N))
    return out[:B].reshape(B, 6, D)


def _inproj_kernel(x_ref, mod_ref, nw_ref, wm_ref, ws_ref, qkv_ref, qkv4_ref, qkv16_ref, dqkv_ref, dz_ref,
                   gbc_ref, scr, scr2):
    x = x_ref[...]
    tm = x.shape[0]
    shift = mod_ref[0, 0:1, :]
    scale = mod_ref[0, 1:2, :]
    ms = jnp.mean(x * x, axis=-1, keepdims=True)
    h = x * lax.rsqrt(ms + EPS) * nw_ref[...]
    hb = (h * (1.0 + scale) + shift).astype(BF16)
    nl = ATTN_W // LANES
    proj = [jnp.dot(hb, wm_ref[:, c0:c0 + ATTN_W], preferred_element_type=F32)
            for c0 in range(0, MAIN_COLS, ATTN_W)]
    for j in range(3):
        cs = slice(j * ATTN_W, (j + 1) * ATTN_W)
        r = proj[j]
        if j == 0:
            r = r * (ATTN_HD ** -0.5)
        qkv_ref[:, cs] = r.astype(BF16)
        d1, d2 = DILATIONS[1], DILATIONS[2]
        step = d2 // d1
        for c in range(nl):
            scr[c] = r[:, c * LANES:(c + 1) * LANES]
        for res in range(d1):
            for c in range(nl):
                scr2[c, res * (tm // d1):(res + 1) * (tm // d1), :] = scr[c, pl.ds(res, tm // d1, stride=d1), :]
            part = jnp.concatenate([scr2[c, res * (tm // d1):(res + 1) * (tm // d1), :] for c in range(nl)],
                                   axis=1)
            qkv4_ref[0, res, :, cs] = part.astype(BF16)
        for res in range(d2):
            r1, q = res % d1, res // d1
            part = jnp.concatenate(
                [scr2[c, pl.ds(r1 * (tm // d1) + q, tm // d2, stride=step), :] for c in range(nl)], axis=1)
            qkv16_ref[0, res, :, cs] = part.astype(BF16)
    for j in range(3):
        dqkv_ref[:, j * DN_W:(j + 1) * DN_W] = proj[3 + j].astype(BF16)
    dz_ref[...] = proj[6].astype(BF16)
    gbc_ref[...] = jnp.dot(hb, ws_ref[...], preferred_element_type=F32)


def _inproj(x2, mod, norm_w, w_in, S):
    T, D = x2.shape
    tm = 1024
    B, nt = T // S, S // tm
    wm = w_in[:, :MAIN_COLS].astype(BF16)
    ws = jnp.zeros((D, LANES), F32).at[:, :2 * DN_HEADS].set(w_in[:, MAIN_COLS:]).astype(BF16)
    return pl.pallas_call(
        _inproj_kernel,
        grid=(T // tm,),
        in_specs=[pl.BlockSpec((tm, D), lambda i: (i, 0)),
                  pl.BlockSpec((1, 6, D), lambda i: (i * tm // S, 0, 0)),
                  pl.BlockSpec((1, D), lambda i: (0, 0)),
                  pl.BlockSpec((D, MAIN_COLS), lambda i: (0, 0)),
                  pl.BlockSpec((D, LANES), lambda i: (0, 0))],
        out_specs=[pl.BlockSpec((tm, 3 * ATTN_W), lambda i: (i, 0))]
        + [pl.BlockSpec((1, d, tm // d, 3 * ATTN_W), lambda i: (i // nt, 0, i % nt, 0)) for d in DILATIONS[1:]]
        + [pl.BlockSpec((tm, DN_CONV), lambda i: (i, 0)),
           pl.BlockSpec((tm, DN_W), lambda i: (i, 0)),
           pl.BlockSpec((tm, LANES), lambda i: (i, 0))],
        out_shape=[jax.ShapeDtypeStruct((T, 3 * ATTN_W), BF16)]
        + [jax.ShapeDtypeStruct((B, d, S // d, 3 * ATTN_W), BF16) for d in DILATIONS[1:]]
        + [jax.ShapeDtypeStruct((T, DN_CONV), BF16),
           jax.ShapeDtypeStruct((T, DN_W), BF16),
           jax.ShapeDtypeStruct((T, LANES), F32)],
        scratch_shapes=[pltpu.VMEM((ATTN_W // LANES, tm, LANES), F32),
                        pltpu.VMEM((ATTN_W // LANES, tm, LANES), F32)],
        compiler_params=_cparams(("arbitrary",)),
        name="inproj",
    )(x2, mod, norm_w.reshape(1, D), wm, ws)


def _attn_kernel(q_ref, kc_ref, kp_ref, vc_ref, vp_ref, o_ref, lse_ref, kf, vf, *, qb):
    n = pl.program_id(2)
    kf[0:ATTN_BLK, :] = kp_ref[0, 0]
    kf[ATTN_BLK:, :] = kc_ref[0, 0]
    vf[0:ATTN_BLK, :] = vp_ref[0, 0]
    vf[ATTN_BLK:, :] = vc_ref[0, 0]
    row = lax.broadcasted_iota(I32, (ATTN_BLK, 2 * ATTN_BLK), 0)
    col = lax.broadcasted_iota(I32, (ATTN_BLK, 2 * ATTN_BLK), 1)
    band = jnp.logical_or(jnp.logical_and(col < ATTN_BLK, col >= row),
                          jnp.logical_and(col >= ATTN_BLK, col - ATTN_BLK <= row))
    lane = lax.broadcasted_iota(I32, (ATTN_BLK, LANES), 1)
    lo = lane < ATTN_HD

    def sub(j, carry):
        r0 = pl.multiple_of(j * ATTN_BLK, ATTN_BLK)
        first_col = jnp.where(jnp.logical_and(n == 0, j == 0), ATTN_BLK, 0)
        mask = jnp.logical_and(band, col >= first_col)
        npair = ATTN_W // LANES
        cols = [slice(hp * LANES, (hp + 1) * LANES) for hp in range(npair)]
        heads = [(hp, half) for hp in range(npair) for half in range(2)]
        scores = []
        for hp, half in heads:
            q2 = q_ref[0, 0, pl.ds(r0, ATTN_BLK), cols[hp]]
            qm = jnp.where(lo if half == 0 else jnp.logical_not(lo), q2, jnp.zeros_like(q2))
            scores.append(jnp.where(mask, _nt(qm, kf[pl.ds(r0, 2 * ATTN_BLK), cols[hp]]), NEG))
        maxes = [jnp.max(s, axis=-1, keepdims=True) for s in scores]
        probs = [jnp.exp(s - m) for s, m in zip(scores, maxes)]
        dens = [jnp.sum(p, axis=-1, keepdims=True) for p in probs]
        accs = [jnp.dot(p.astype(BF16), vf[pl.ds(r0, 2 * ATTN_BLK), cols[hp]], preferred_element_type=F32)
                for p, (hp, _) in zip(probs, heads)]
        outs = [a / d for a, d in zip(accs, dens)]
        lse_tile = jnp.zeros((ATTN_BLK, LANES), F32)
        for h, (m, d) in enumerate(zip(maxes, dens)):
            lse_tile = jnp.where(lane == h, m + jnp.log(d), lse_tile)
        for hp in range(npair):
            o_ref[0, 0, pl.ds(r0, ATTN_BLK), cols[hp]] = jnp.where(lo, outs[2 * hp], outs[2 * hp + 1]).astype(BF16)
        lse_ref[0, 0, pl.ds(r0, ATTN_BLK), :] = lse_tile
        return carry

    lax.fori_loop(0, qb // ATTN_BLK, sub, 0, unroll=True)


def _attn_branch(qkv, d):
    B, _, L, _ = qkv.shape
    qb = min(1024, L)
    nsub = qb // ATTN_BLK
    cur = lambda c: pl.BlockSpec((1, 1, qb, ATTN_W), lambda b, r, n: (b, r, n, c))
    prev = lambda c: pl.BlockSpec((1, 1, ATTN_BLK, ATTN_W),
                                  lambda b, r, n: (b, r, jnp.maximum(n * nsub - 1, 0), c))
    return pl.pallas_call(
        functools.partial(_attn_kernel, qb=qb),
        grid=(B, d, L // qb),
        in_specs=[cur(0), cur(1), prev(1), cur(2), prev(2)],
        out_specs=[pl.BlockSpec((1, 1, qb, ATTN_W), lambda b, r, n: (b, r, n, 0)),
                   pl.BlockSpec((1, 1, qb, LANES), lambda b, r, n: (b, r, n, 0))],
        out_shape=[jax.ShapeDtypeStruct((B, d, L, ATTN_W), BF16),
                   jax.ShapeDtypeStruct((B, d, L, LANES), F32)],
        scratch_shapes=[pltpu.VMEM((qb + ATTN_BLK, ATTN_W), BF16),
                        pltpu.VMEM((qb + ATTN_BLK, ATTN_W), BF16)],
        compiler_params=_cparams(("arbitrary", "arbitrary", "arbitrary")),
        name=f"attn_d{d}",
    )(qkv, qkv, qkv, qkv, qkv)


def _gdn_kernel(x_ref, z_ref, g_ref, cw_ref, prm_ref, nw_ref, o_ref, xext, yc, s0, s1, *, rb, nseq):
    i = pl.program_id(1)

    @pl.when(i == 0)
    def _():
        xext[:, 0:8, :] = jnp.zeros((nseq, 8, DN_CONV), F32)
        s0[...] = jnp.zeros_like(s0)
        s1[...] = jnp.zeros_like(s1)

    @pl.when(i > 0)
    def _():
        xext[:, 0:8, :] = xext[:, rb:rb + 8, :]

    for q in range(nseq):
        xext[q, 8:, :] = x_ref[q].astype(F32)
        y = cw_ref[CONV_K - 1:CONV_K, :] * xext[q, 8:8 + rb, :]
        for j in range(CONV_K - 1):
            off = 8 - (CONV_K - 1) + j
            y = y + cw_ref[j:j + 1, :] * xext[q, off:off + rb, :]
        yc[q] = y * _sigmoid(y)

    C = DN_CHUNK
    H = DN_HEADS
    CW = H * C
    dot = functools.partial(jnp.dot, preferred_element_type=F32)

    def iota(shape, d):
        return lax.broadcasted_iota(I32, shape, d)

    ltri_b = jnp.where(iota((C, C), 0) >= iota((C, C), 1), 1.0, 0.0).astype(BF16)
    lane = iota((C, LANES), 1)
    blane = lane < H
    glane = jnp.logical_and(lane >= H, lane < 2 * H)
    e512 = jnp.where(jnp.logical_or(iota((LANES, DN_W), 1) // DN_D == iota((LANES, DN_W), 0),
                                    iota((LANES, DN_W), 1) // DN_D == iota((LANES, DN_W), 0) - H),
                     1.0, 0.0).astype(BF16)
    e256 = jnp.where(iota((LANES, CW), 1) // C == iota((LANES, CW), 0) - H, 1.0, 0.0).astype(BF16)
    row4 = iota((C, CW), 0)
    col4 = iota((C, CW), 1) % C
    eye4 = jnp.where(row4 == col4, 1.0, 0.0).astype(F32)
    blk = [iota((C, CW), 1) // C == h for h in range(H)]
    bd_cc = iota((CW, CW), 0) // C == iota((CW, CW), 1) // C
    bd_pair = iota((CW, CW), 0) // DN_D == iota((CW, CW), 1) // DN_D
    rt_mask = iota((CW, DN_W), 0) // C == iota((CW, DN_W), 1) // DN_D
    neg_a = -jnp.exp(prm_ref[0:1, :])
    dtb = prm_ref[1:2, :]
    nw = nw_ref[...]

    def hilo(x):
        hi = x.astype(BF16)
        return hi, (x - hi.astype(F32)).astype(BF16)

    def heads(a, w):
        return [a[:, h * w:(h + 1) * w] for h in range(H)]

    def l2n(a, mult):
        return jnp.concatenate(
            [p * (lax.rsqrt(jnp.sum(p * p, axis=-1, keepdims=True) + EPS) * mult) for p in heads(a, DN_D)],
            axis=1)

    def stack4(a):
        return jnp.concatenate([a, a, a, a], axis=0)

    zb = jnp.zeros((), BF16)
    nchunk = rb // C
    chunks = []
    for c, sq in ((c, sq) for c in range(nchunk) for sq in range(nseq)):
        rs = slice(c * C, (c + 1) * C)
        G = g_ref[sq, rs, :]
        xg = G + dtb
        gv = jnp.where(glane, neg_a * (jnp.maximum(xg, 0.0) + jnp.log1p(jnp.exp(-jnp.abs(xg)))), 0.0)
        be = jnp.where(blane, _sigmoid(G), 0.0)
        g_hi, g_lo = hilo(gv)
        gcum = dot(ltri_b, g_hi) + dot(ltri_b, g_lo)
        gtot = gcum[C - 1:C, :]
        eg = jnp.where(glane, jnp.exp(gcum), 0.0)
        ek = jnp.where(glane, jnp.exp(gtot - gcum), 0.0)
        ex = dot(jnp.concatenate([be, eg, ek], axis=0).astype(BF16), e512)
        bexp, egexp, ekexp = ex[0:C], ex[C:2 * C], ex[2 * C:3 * C]
        gexp = dot(g_hi, e256) + dot(g_lo, e256)
        d_hi, d_lo = hilo(jnp.where(row4 > col4, gexp, 0.0))
        diff = dot(ltri_b, d_hi) + dot(ltri_b, d_lo)
        decay = jnp.exp(jnp.where(row4 >= col4, diff, NEG))

        q4 = l2n(yc[sq, rs, 0:DN_W], DN_D ** -0.5)
        k4 = l2n(yc[sq, rs, DN_W:2 * DN_W], 1.0)
        v4 = yc[sq, rs, 2 * DN_W:3 * DN_W]
        kb4 = k4 * bexp
        vb4 = v4 * bexp
        rt = jnp.where(rt_mask, stack4(k4.astype(BF16)), jnp.zeros((), BF16))
        ai = _nt(jnp.concatenate([kb4, q4], axis=0).astype(BF16), rt)
        a4 = jnp.where(row4 > col4, ai[0:C] * decay, 0.0)
        pb = a4.astype(BF16)
        chunks.append(dict(
            sq=sq, rs=rs, q4=q4, k4=k4, kb4=kb4, vb4=vb4, egexp=egexp, ekexp=ekexp,
            intra=ai[C:2 * C] * decay, t4=eye4 - a4, pb=pb,
            bd=jnp.where(bd_cc, stack4(pb), zb)))

    for _ in range(5):
        for ch in chunks:
            ch["pb"] = dot(ch["pb"], ch["bd"]).astype(BF16)
        for ch in chunks:
            ch["bd"] = jnp.where(bd_cc, stack4(ch["pb"]), zb)
        for ch in chunks:
            ch["t4"] = ch["t4"] + dot(ch["t4"].astype(BF16), ch["bd"])

    for ch in chunks:
        q4, k4, kb4, vb4, egexp, ekexp = (ch[n] for n in ("q4", "k4", "kb4", "vb4", "egexp", "ekexp"))
        t4b = ch["t4"].astype(BF16)
        lstk = jnp.concatenate([jnp.where(blk[h], t4b, zb) for h in range(H)], axis=0)
        kbg4 = kb4 * egexp
        rstk = jnp.concatenate(
            [jnp.concatenate([vb, kbg], axis=1) for vb, kbg in zip(heads(vb4, DN_D), heads(kbg4, DN_D))],
            axis=0).astype(BF16)
        uw = dot(lstk, rstk)
        u4 = jnp.concatenate([uw[h * C:(h + 1) * C, 0:DN_D] for h in range(H)], axis=1)
        w4 = jnp.concatenate([uw[h * C:(h + 1) * C, DN_D:2 * DN_D] for h in range(H)], axis=1)
        ib = ch["intra"].astype(BF16)
        ch.update(
            u4=u4, wq=jnp.concatenate([w4, q4 * egexp], axis=0).astype(BF16),
            kd4=(k4 * ekexp).astype(BF16), gl4=egexp[C - 1:C, :],
            lint=jnp.concatenate([jnp.where(blk[h], ib, zb) for h in range(H)], axis=0))

    for ch in chunks:
        sq, rs, u4, wq, kd4, gl4, lint = (ch[n] for n in ("sq", "rs", "u4", "wq", "kd4", "gl4", "lint"))
        ra = dot(wq[:, 0:CW], s0[sq].astype(BF16))
        rc = dot(wq[:, CW:2 * CW], s1[sq].astype(BF16))
        vn = u4 - jnp.concatenate([ra[0:C], rc[0:C]], axis=1)
        vnb = vn.astype(BF16)
        oi = dot(lint, jnp.concatenate(heads(vnb, DN_D), axis=0))
        o = (jnp.concatenate([ra[C:2 * C], rc[C:2 * C]], axis=1)
             + jnp.concatenate([oi[h * C:(h + 1) * C] for h in range(H)], axis=1))
        tn = (((0,), (0,)), ((), ()))
        s0[sq] = s0[sq] * gl4[:, 0:CW] + jnp.where(
            bd_pair, lax.dot_general(kd4[:, 0:CW], vnb[:, 0:CW], tn, preferred_element_type=F32), 0.0)
        s1[sq] = s1[sq] * gl4[:, CW:2 * CW] + jnp.where(
            bd_pair, lax.dot_general(kd4[:, CW:2 * CW], vnb[:, CW:2 * CW], tn, preferred_element_type=F32), 0.0)

        z = z_ref[sq, rs, :].astype(F32)
        on = jnp.concatenate(
            [p * lax.rsqrt(jnp.mean(p * p, axis=-1, keepdims=True) + EPS) * nw for p in heads(o, DN_D)], axis=1)
        o_ref[sq, rs, :] = (on * (z * _sigmoid(z))).astype(BF16)


def _gdn(dqkv, dz, gbc, conv_w, A_log, dt_bias, dn_norm_w, B, S):
    nseq = 2 if B % 2 == 0 else 1
    rb = 1024 // nseq
    prm = jnp.zeros((2, LANES), F32)
    prm = prm.at[0, DN_HEADS:2 * DN_HEADS].set(A_log.astype(F32))
    prm = prm.at[1, DN_HEADS:2 * DN_HEADS].set(dt_bias.astype(F32))
    out = pl.pallas_call(
        functools.partial(_gdn_kernel, rb=rb, nseq=nseq),
        grid=(B // nseq, S // rb),
        in_specs=[pl.BlockSpec((nseq, rb, DN_CONV), lambda b, i: (b, i, 0)),
                  pl.BlockSpec((nseq, rb, DN_W), lambda b, i: (b, i, 0)),
                  pl.BlockSpec((nseq, rb, LANES), lambda b, i: (b, i, 0)),
                  pl.BlockSpec((CONV_K, DN_CONV), lambda b, i: (0, 0)),
                  pl.BlockSpec((2, LANES), lambda b, i: (0, 0)),
                  pl.BlockSpec((1, DN_D), lambda b, i: (0, 0))],
        out_specs=pl.BlockSpec((nseq, rb, DN_W), lambda b, i: (b, i, 0)),
        out_shape=jax.ShapeDtypeStruct((B, S, DN_W), BF16),
        scratch_shapes=[pltpu.VMEM((nseq, rb + 8, DN_CONV), F32),
                        pltpu.VMEM((nseq, rb, DN_CONV), F32),
                        pltpu.VMEM((nseq, 2 * DN_D, 2 * DN_D), F32),
                        pltpu.VMEM((nseq, 2 * DN_D, 2 * DN_D), F32)],
        compiler_params=_cparams(("arbitrary", "arbitrary")),
        name="gdn",
    )(dqkv.reshape(B, S, DN_CONV), dz.reshape(B, S, DN_W), gbc.reshape(B, S, LANES),
      conv_w, prm, dn_norm_w.reshape(1, DN_D))
    return out.reshape(B * S, DN_W)


def _out_kernel(o1_ref, o2_ref, o3_ref, l1_ref, l2_ref, l3_ref, dn_ref, x_ref, mod_ref, wo_ref,
                n2_ref, wr_ref, br_ref,
                x1_ref, h2_ref, te_ref, rk_ref, gtc_ref, cnt_ref, base, scr, *, tm):
    i = pl.program_id(0)

    @pl.when(i == 0)
    def _():
        base[...] = jnp.zeros_like(base)

    def natural(ref, d):
        if d == 1:
            return ref[0, 0].astype(F32)
        nl = ref.shape[-1] // LANES
        for res in range(d):
            blk = ref[0, res].astype(F32)
            for c in range(nl):
                scr[c, pl.ds(res, tm // d, stride=d), :] = blk[:, c * LANES:(c + 1) * LANES]
        return jnp.concatenate([scr[c] for c in range(nl)], axis=1)

    l1, l2, l3 = (natural(r, d) for r, d in zip((l1_ref, l2_ref, l3_ref), DILATIONS))
    mx = jnp.maximum(jnp.maximum(l1, l2), l3)
    e1, e2, e3 = jnp.exp(l1 - mx), jnp.exp(l2 - mx), jnp.exp(l3 - mx)
    zs = e1 + e2 + e3
    er = lax.broadcasted_iota(I32, (LANES, ATTN_W), 0)
    ec = lax.broadcasted_iota(I32, (LANES, ATTN_W), 1)
    expand = jnp.where(ec // ATTN_HD == er, 1.0, 0.0).astype(BF16)
    attn = jnp.zeros((tm, ATTN_W), F32)
    for e, o_ref, d in zip((e1, e2, e3), (o1_ref, o2_ref, o3_ref), DILATIONS):
        wgt = jnp.dot((e / zs).astype(BF16), expand, preferred_element_type=F32)
        attn = attn + wgt * natural(o_ref, d)
    mix = (jnp.dot(attn.astype(BF16), wo_ref[0:ATTN_W, :], preferred_element_type=F32)
           + jnp.dot(dn_ref[...], wo_ref[ATTN_W:, :], preferred_element_type=F32))
    x1 = x_ref[...] + mod_ref[0, 2:3, :] * mix
    x1_ref[...] = x1
    ms = jnp.mean(x1 * x1, axis=-1, keepdims=True)
    h2 = x1 * lax.rsqrt(ms + EPS) * n2_ref[...]
    h2 = h2 * (1.0 + mod_ref[0, 4:5, :]) + mod_ref[0, 3:4, :]
    _store_slabs(h2_ref, h2)

    lg = _nt(wr_ref[...], h2, precision=HI) + br_ref[...]
    eidx = lax.broadcasted_iota(I32, (N_EXPERTS, tm), 0)
    vals, idxs, sels = [], [], []
    for _ in range(TOP_K):
        m = jnp.max(lg, axis=0, keepdims=True)
        idx = jnp.min(jnp.where(lg == m, eidx, N_EXPERTS), axis=0, keepdims=True)
        sel = eidx == idx
        vals.append(m)
        idxs.append(idx)
        sels.append(sel)
        lg = jnp.where(sel, -jnp.inf, lg)
    ex = [jnp.exp(v - vals[0]) for v in vals]
    den = ex[0] + ex[1] + ex[2] + ex[3]
    gates = [e / den for e in ex]

    msum = jnp.zeros((N_EXPERTS, tm), F32)
    for sel in sels:
        msum = msum + jnp.where(sel, 1.0, 0.0)
    tr = lax.broadcasted_iota(I32, (tm, tm), 0)
    tc = lax.broadcasted_iota(I32, (tm, tm), 1)
    upper = jnp.where(tr <= tc, 1.0, 0.0).astype(BF16)
    incl = jnp.dot(msum.astype(BF16), upper, preferred_element_type=F32)
    pos = base[:, 0:1] + (incl - msum)
    sub8 = lax.broadcasted_iota(I32, (8, tm), 0)
    te = jnp.zeros((8, tm), I32)
    rk = jnp.zeros((8, tm), I32)
    gt = jnp.zeros((8, tm), F32)
    for k in range(TOP_K):
        rank_k = jnp.sum(jnp.where(sels[k], pos, 0.0), axis=0, keepdims=True).astype(I32)
        te = jnp.where(sub8 == k, idxs[k], te)
        rk = jnp.where(sub8 == k, rank_k, rk)
        gt = jnp.where(sub8 == k, gates[k], gt)
    te_ref[...] = te
    rk_ref[...] = rk
    gtc_ref[...] = jnp.transpose(jnp.concatenate([gt, jnp.zeros((LANES - 8, tm), F32)], axis=0))
    base[...] = base[...] + jnp.sum(msum, axis=1, keepdims=True)
    cnt_ref[...] = base[...].astype(I32)


def _outproj(o1, o2, o3, l1, l2, l3, dn, x2, mod, w_out, norm2_w, w_router, b_router, S):
    T, D = x2.shape
    tm = 1024
    nt = S // tm
    row = lambda w: pl.BlockSpec((tm, w), lambda i: (i, 0))
    res = lambda d, w: pl.BlockSpec((1, d, tm // d, w), lambda i: (i // nt, 0, i % nt, 0))
    colb = pl.BlockSpec((8, tm), lambda i: (0, i))
    return pl.pallas_call(
        functools.partial(_out_kernel, tm=tm),
        grid=(T // tm,),
        in_specs=[res(d, ATTN_W) for d in DILATIONS] + [res(d, LANES) for d in DILATIONS]
        + [row(DN_W), row(D),
                  pl.BlockSpec((1, 6, D), lambda i: (i * tm // S, 0, 0)),
                  pl.BlockSpec((D, D), lambda i: (0, 0)),
                  pl.BlockSpec((1, D), lambda i: (0, 0)),
                  pl.BlockSpec((N_EXPERTS, D), lambda i: (0, 0)),
                  pl.BlockSpec((N_EXPERTS, 1), lambda i: (0, 0))],
        out_specs=[row(D), pl.BlockSpec((tm * (D // LANES), LANES), lambda i: (i, 0)), colb, colb,
                   row(LANES), pl.BlockSpec((N_EXPERTS, LANES), lambda i: (0, 0))],
        out_shape=[jax.ShapeDtypeStruct((T, D), F32),
                   jax.ShapeDtypeStruct((T * (D // LANES), LANES), F32),
                   jax.ShapeDtypeStruct((8, T), I32),
                   jax.ShapeDtypeStruct((8, T), I32),
                   jax.ShapeDtypeStruct((T, LANES), F32),
                   jax.ShapeDtypeStruct((N_EXPERTS, LANES), I32)],
        scratch_shapes=[pltpu.VMEM((N_EXPERTS, LANES), F32),
                        pltpu.VMEM((ATTN_W // LANES, tm, LANES), F32)],
        compiler_params=_cparams(("arbitrary",)),
        name="outproj_router",
    )(o1, o2, o3, l1, l2, l3, dn, x2, mod, w_out.astype(BF16), norm2_w.reshape(1, D),
      jnp.transpose(w_router), b_router.reshape(N_EXPERTS, 1))


def _dest_kernel(ps_ref, te_ref, rk_ref, d_ref):
    te = te_ref[...]
    acc = jnp.zeros(te.shape, I32)
    for e in range(N_EXPERTS):
        acc = jnp.where(te == e, ps_ref[e], acc)
    d_ref[...] = acc + rk_ref[...]


def _dest(pstart, te, rk):
    T = te.shape[1]
    tb = 2048
    return pl.pallas_call(
        _dest_kernel,
        grid_spec=pltpu.PrefetchScalarGridSpec(
            num_scalar_prefetch=1,
            grid=(T // tb,),
            in_specs=[pl.BlockSpec((8, tb), lambda i, ps: (0, i)),
                      pl.BlockSpec((8, tb), lambda i, ps: (0, i))],
            out_specs=pl.BlockSpec((8, tb), lambda i, ps: (0, i))),
        out_shape=jax.ShapeDtypeStruct((8, T), I32),
        compiler_params=_cparams(("arbitrary",)),
        name="dest_rows",
    )(pstart, te, rk)


SC_CORES = 2
SC_SUBCORES = 16
SC_IDX_CHUNK = 128


def _invperm(dest_flat, P):
    N = dest_flat.shape[0]
    nch = N // (SC_SUBCORES * SC_IDX_CHUNK)
    half = P // SC_CORES
    per_out = half // SC_SUBCORES
    assert N % (SC_SUBCORES * SC_IDX_CHUNK) == 0 and P % (SC_CORES * SC_SUBCORES * 8) == 0
    mesh = plsc.VectorSubcoreMesh(core_axis_name="c", subcore_axis_name="s",
                                  num_cores=SC_CORES, num_subcores=SC_SUBCORES)

    @functools.partial(
        pl.kernel, mesh=mesh, out_type=jax.ShapeDtypeStruct((P,), I32),
        scratch_types=[pltpu.VMEM((nch, SC_IDX_CHUNK), I32), pltpu.VMEM((nch, SC_IDX_CHUNK), I32),
                       pltpu.VMEM_SHARED((P,), I32), pltpu.VMEM((per_out,), I32), pltpu.SemaphoreType.DMA])
    def scatter_codes(idx_hbm, val_hbm, out_hbm, idx_v, val_v, table, stage, sem):
        core = lax.axis_index("c")
        sub = lax.axis_index("s")
        pltpu.sync_copy(idx_hbm.at[sub], idx_v)
        pltpu.sync_copy(val_hbm.at[sub], val_v)

        @pl.loop(0, nch)
        def _(j):
            pltpu.async_copy(val_v.at[j], table.at[idx_v.at[j]], sem)

        @pl.loop(0, nch)
        def _(j):
            pltpu.make_async_copy(val_v.at[j], table.at[idx_v.at[j]], sem).wait()

        plsc.subcore_barrier()
        off = pl.multiple_of(core * half + sub * per_out, 8)
        pltpu.sync_copy(table.at[pl.ds(off, per_out)], stage)
        pltpu.sync_copy(stage, out_hbm.at[pl.ds(off, per_out)])

    vals = jnp.arange(N, dtype=I32)
    shape = (SC_SUBCORES, nch, SC_IDX_CHUNK)
    return scatter_codes(dest_flat.reshape(shape), vals.reshape(shape))


def _row_copy(src, dst, sem):
    return pltpu.make_async_copy(src, dst, sem)


def _moe_kernel(be_ref, nv_ref, cc_ref, cn_ref, h2_ref, w1_ref, b1_ref, w2_ref, b2_ref, y4_ref,
                xbuf, ybuf, w1b, w2b, gsem, ssem, *, F, T, D, tme, nb):
    i = pl.program_id(0)
    s = i % 2
    ns = D // LANES

    def rows(first, n):
        return pl.ds(pl.multiple_of(first * ns, ns), n * ns)

    def gather_copy(tok, p, slot):
        return _row_copy(h2_ref.at[rows(tok, 1)], xbuf.at[slot, rows(p, 1)], gsem.at[slot])

    def token_of(code):
        return code & (T - 1) if T & (T - 1) == 0 else code % T

    def issue_rows(start_row, nv):
        ng = nv // ROW_UNROLL

        def grp(g, c):
            for j in range(ROW_UNROLL):
                start_row(g * ROW_UNROLL + j, j % 2)
            return c
        lax.fori_loop(0, ng, grp, 0)

        def one(p, c):
            start_row(p, 0)
            return c
        lax.fori_loop(ng * ROW_UNROLL, nv, one, 0)

    def start_gather(code_ref, nv, slot):
        issue_rows(lambda p, pri: gather_copy(token_of(code_ref[0, 0, p]), p, slot).start(priority=pri), nv)

    def wait_rows(make, nv):
        @pl.when(nv > 0)
        def _():
            make(nv).wait()

    def gathered(n, slot):
        return _row_copy(h2_ref.at[rows(0, n)], xbuf.at[slot, rows(0, n)], gsem.at[slot])

    def scattered(n, slot):
        return _row_copy(ybuf.at[slot, rows(0, n)], y4_ref.at[rows(0, n)], ssem.at[slot])

    @pl.when(i == 0)
    def _():
        start_gather(cc_ref, tme, 0)

    nv = nv_ref[i]
    gathered(tme, s).wait()

    @pl.when(i >= 2)
    def _():
        wait_rows(lambda n: scattered(n, s), nv_ref[i - 2])

    @pl.when(jnp.logical_or(i == 0, be_ref[i] != be_ref[jnp.maximum(i - 1, 0)]))
    def _():
        w1b[...] = w1_ref[0].astype(BF16)
        w2b[...] = w2_ref[0].astype(BF16)

    def compute(slot):
        x = _load_slabs(xbuf, tme, D, lead=(slot,)).astype(BF16)
        for p in range(tme):
            gather_copy(token_of(cn_ref[0, 0, p]), p, 1 - slot).start(priority=p % 2)
        hgu = jnp.dot(x, w1b[...], preferred_element_type=F32) + b1_ref[0]
        gate = jnp.minimum(hgu[:, :F], SWIGLU_LIMIT)
        up = jnp.clip(hgu[:, F:], -SWIGLU_LIMIT, SWIGLU_LIMIT)
        act = gate * _sigmoid(SWIGLU_ALPHA * gate) * (up + 1.0)
        y = jnp.dot(act.astype(BF16), w2b[...], preferred_element_type=F32) + b2_ref[0]
        _store_slabs(ybuf, y, lead=(slot,))

        issue_rows(lambda p, pri: _row_copy(ybuf.at[slot, rows(p, 1)], y4_ref.at[rows(cc_ref[0, 0, p], 1)],
                                            ssem.at[slot]).start(priority=pri), nv)

    for slot in range(2):
        pl.when(jnp.logical_and(nv > 0, s == slot))(functools.partial(compute, slot))

    @pl.when(jnp.logical_and(nv == 0, i + 1 < nb))
    def _():
        start_gather(cn_ref, tme, 1 - s)

    @pl.when(i == nb - 1)
    def _():
        @pl.when(nv > 0)
        def _():
            gathered(tme, 1 - s).wait()
        wait_rows(lambda n: scattered(n, s), nv)
        if nb > 1:
            wait_rows(lambda n: scattered(n, 1 - s), nv_ref[i - 1])


def _experts(blk_exp, blk_valid, codes, h2s, w1, b1, w2, b2, tme):
    E, D, F2 = w1.shape
    F = F2 // 2
    ns = D // LANES
    T = h2s.shape[0] // ns
    nb = blk_exp.shape[0]
    codes3 = codes.reshape(nb, 1, tme)
    wspec = lambda shape: pl.BlockSpec(shape, lambda i, be, nv: (be[i], 0, 0))
    cspec = lambda off: pl.BlockSpec((1, 1, tme), lambda i, be, nv: (jnp.minimum(i + off, nb - 1), 0, 0),
                                     memory_space=pltpu.SMEM)
    return pl.pallas_call(
        functools.partial(_moe_kernel, F=F, T=T, D=D, tme=tme, nb=nb),
        grid_spec=pltpu.PrefetchScalarGridSpec(
            num_scalar_prefetch=2,
            grid=(nb,),
            in_specs=[cspec(0), cspec(1),
                      pl.BlockSpec(memory_space=pl.ANY),
                      wspec((1, D, F2)), wspec((1, 1, F2)), wspec((1, F, D)), wspec((1, 1, D))],
            out_specs=pl.BlockSpec(memory_space=pl.ANY),
            scratch_shapes=[pltpu.VMEM((2, tme * ns, LANES), F32), pltpu.VMEM((2, tme * ns, LANES), F32),
                            pltpu.VMEM((D, F2), BF16), pltpu.VMEM((F, D), BF16),
                            pltpu.SemaphoreType.DMA((2,)), pltpu.SemaphoreType.DMA((2,))]),
        out_shape=jax.ShapeDtypeStruct((TOP_K * T * ns, LANES), F32),
        compiler_params=_cparams(("arbitrary",)),
        name="experts",
    )(blk_exp, blk_valid, codes3, codes3, h2s, w1, b1.reshape(E, 1, F2), w2, b2.reshape(E, 1, D))


def _comb_kernel(g_ref, x1_ref, mod_ref, fw_ref, y0_ref, y1_ref, y2_ref, y3_ref, o_ref):
    g = g_ref[...]
    n, d = x1_ref.shape
    y = g[:, 0:1] * _load_slabs(y0_ref, n, d)
    for k, y_ref in ((1, y1_ref), (2, y2_ref), (3, y3_ref)):
        y = y + g[:, k:k + 1] * _load_slabs(y_ref, n, d)
    x2 = x1_ref[...] + mod_ref[0, 5:6, :] * y
    ms = jnp.mean(x2 * x2, axis=-1, keepdims=True)
    o_ref[...] = x2 * lax.rsqrt(ms + EPS) * fw_ref[...]


def _combine(gtc, x1, mod, final_w, y4, S):
    T, D = x1.shape
    tmc = 512
    nt = T // tmc
    yspec = lambda k: pl.BlockSpec((tmc * (D // LANES), LANES), lambda i: (k * nt + i, 0))
    return pl.pallas_call(
        _comb_kernel,
        grid=(nt,),
        in_specs=[pl.BlockSpec((tmc, LANES), lambda i: (i, 0)),
                  pl.BlockSpec((tmc, D), lambda i: (i, 0)),
                  pl.BlockSpec((1, 6, D), lambda i: (i * tmc // S, 0, 0)),
                  pl.BlockSpec((1, D), lambda i: (0, 0)),
                  yspec(0), yspec(1), yspec(2), yspec(3)],
        out_specs=pl.BlockSpec((tmc, D), lambda i: (i, 0)),
        out_shape=jax.ShapeDtypeStruct((T, D), F32),
        compiler_params=_cparams(("arbitrary",)),
        name="combine",
    )(gtc, x1, mod, final_w.reshape(1, D), y4, y4, y4, y4)


def _layer(x2, mod, B, S, norm1_w, w_in, conv_w, A_log, dt_bias, dn_norm_w, w_out,
           norm2_w, w_router, b_router, w1, b1, w2, b2):
    T, D = x2.shape
    qkv, qkv4, qkv16, dqkv, dz, gbc = _inproj(x2, mod, norm1_w, w_in, S)
    views = (qkv.reshape(B, 1, S, 3 * ATTN_W), qkv4, qkv16)
    branches = [_attn_branch(v, d) for v, d in zip(views, DILATIONS)]
    dn = _gdn(dqkv, dz, gbc, conv_w, A_log, dt_bias, dn_norm_w, B, S)
    (o1, l1), (o2, l2), (o3, l3) = branches
    x1, h2, te, rk, gtc, cnt = _outproj(o1, o2, o3, l1, l2, l3, dn, x2, mod, w_out, norm2_w,
                                        w_router, b_router, S)
    tme = MOE_ROWS
    P = T * TOP_K + N_EXPERTS * tme
    counts = cnt[:, 0]
    padded = (counts + tme - 1) // tme * tme
    pend = jnp.cumsum(padded)
    pstart = (pend - padded).astype(I32)
    blk_start = jnp.arange(P // tme, dtype=I32) * tme
    blk_exp = jnp.minimum(jnp.sum((pend[None, :] <= blk_start[:, None]).astype(I32), axis=1),
                          N_EXPERTS - 1).astype(I32)
    blk_valid = jnp.clip(pstart[blk_exp] + counts[blk_exp] - blk_start, 0, tme).astype(I32)
    dest = _dest(pstart, te, rk)
    codes = _invperm(dest[:TOP_K].reshape(TOP_K * T), P)
    y4 = _experts(blk_exp, blk_valid, codes, h2, w1, b1, w2, b2, tme)
    return x1, gtc, y4


def kernel(x, c, w_ada, b_ada, norm1_w, w_in, conv_w, A_log, dt_bias, dn_norm_w, w_out, norm2_w,
           w_router, b_router, w1, b1, w2, b2, final_norm_w):
    B, S, D = x.shape
    depth = w_ada.shape[0]
    assert S % (ATTN_BLK * DILATIONS[-1]) == 0 and depth == 1
    x2 = x.reshape(B * S, D)
    mod = _ada(c, w_ada[0], b_ada[0])
    x1, gtc, y4 = _layer(x2, mod, B, S, norm1_w[0], w_in[0], conv_w[0], A_log[0], dt_bias[0],
                         dn_norm_w[0], w_out[0], norm2_w[0], w_router[0], b_router[0],
                         w1[0], b1[0], w2[0], b2[0])
    out = _combine(gtc, x1, mod, final_norm_w, y4, S)
    return out.reshape(B, S, D)
```

```python
import functools

import jax
import jax.numpy as jnp
from jax import lax
from jax.experimental import pallas as pl
from jax.experimental.pallas import tpu as pltpu
from jax.experimental.pallas import tpu_sc as plsc

F32 = jnp.float32
BF16 = jnp.bfloat16
I32 = jnp.int32
HI = lax.Precision.HIGHEST

LANES = 128
ATTN_HEADS = 8
ATTN_HD = 64
ATTN_W = ATTN_HEADS * ATTN_HD
ATTN_BLK = 128
DILATIONS = (1, 4, 16)
DN_HEADS = 4
DN_D = 128
DN_W = DN_HEADS * DN_D
DN_CONV = 3 * DN_W
CONV_K = 4
DN_CHUNK = 64
N_EXPERTS = 32
TOP_K = 4
MOE_ROWS = 512
ROW_UNROLL = 8
SWIGLU_LIMIT = 7.0
SWIGLU_ALPHA = 1.702
EPS = 1e-6
NEG = -1e30
MAIN_COLS = 3 * ATTN_W + DN_CONV + DN_W

ADA_COLS = 1024
PROJ_ROWS = 1024
ATTN_ROWS = 1024
GDN_ROWS = 1024
DEST_COLS = 2048
COMBINE_ROWS = 512

V7X_VMEM_BYTES = 64 * 1024 * 1024
VMEM_LIMIT = V7X_VMEM_BYTES - 8 * 1024 * 1024


def _cparams(sem):
    return pltpu.CompilerParams(dimension_semantics=sem, vmem_limit_bytes=VMEM_LIMIT)


def _nt(a, b, **kw):
    return lax.dot_general(a, b, (((1,), (1,)), ((), ())), preferred_element_type=F32, **kw)


def _sigmoid(x):
    return 1.0 / (1.0 + jnp.exp(-x))


def _store_slabs(ref, val, lead=()):
    n, d = val.shape
    ns = d // LANES
    for c in range(ns):
        ref[lead + (pl.ds(c, n, stride=ns), slice(None))] = val[:, c * LANES:(c + 1) * LANES]


def _load_slabs(ref, n, d, lead=()):
    ns = d // LANES
    return jnp.concatenate([ref[lead + (pl.ds(c, n, stride=ns), slice(None))] for c in range(ns)], axis=1)


def _ada_kernel(c_ref, w_ref, b_ref, o_ref):
    c = c_ref[...]
    cond = c * _sigmoid(c)
    o_ref[...] = jnp.dot(cond, w_ref[...], preferred_element_type=F32, precision=HI) + b_ref[...]


def _ada(c, w_ada, b_ada):
    B, D = c.shape
    N = w_ada.shape[1]
    cp = jnp.zeros((8, D), F32).at[:B].set(c)
    tn = ADA_COLS
    out = pl.pallas_call(
        _ada_kernel,
        grid=(N // tn,),
        in_specs=[pl.BlockSpec((8, D), lambda j: (0, 0)),
                  pl.BlockSpec((D, tn), lambda j: (0, j)),
                  pl.BlockSpec((1, tn), lambda j: (0, j))],
        out_specs=pl.BlockSpec((8, tn), lambda j: (0, j)),
        out_shape=jax.ShapeDtypeStruct((8, N), F32),
        compiler_params=_cparams(("arbitrary",)),
        name="ada",
    )(cp, w_ada, b_ada.reshape(1, N))
    return out[:B].reshape(B, 6, D)


def _inproj_kernel(x_ref, mod_ref, nw_ref, wm_ref, ws_ref, qkv_ref, qkv4_ref, qkv16_ref, dqkv_ref, dz_ref,
                   gbc_ref, scr, scr2):
    x = x_ref[...]
    tm = x.shape[0]
    shift = mod_ref[0, 0:1, :]
    scale = mod_ref[0, 1:2, :]
    ms = jnp.mean(x * x, axis=-1, keepdims=True)
    h = x * lax.rsqrt(ms + EPS) * nw_ref[...]
    hb = (h * (1.0 + scale) + shift).astype(BF16)
    nl = ATTN_W // LANES
    proj = [jnp.dot(hb, wm_ref[:, c0:c0 + ATTN_W], preferred_element_type=F32)
            for c0 in range(0, MAIN_COLS, ATTN_W)]
    for j in range(3):
        cs = slice(j * ATTN_W, (j + 1) * ATTN_W)
        r = proj[j]
        if j == 0:
            r = r * (ATTN_HD ** -0.5)
        qkv_ref[:, cs] = r.astype(BF16)
        d1, d2 = DILATIONS[1], DILATIONS[2]
        step = d2 // d1
        for c in range(nl):
            scr[c] = r[:, c * LANES:(c + 1) * LANES]
        for res in range(d1):
            for c in range(nl):
                scr2[c, res * (tm // d1):(res + 1) * (tm // d1), :] = scr[c, pl.ds(res, tm // d1, stride=d1), :]
            part = jnp.concatenate([scr2[c, res * (tm // d1):(res + 1) * (tm // d1), :] for c in range(nl)],
                                   axis=1)
            qkv4_ref[0, res, :, cs] = part.astype(BF16)
        for res in range(d2):
            r1, q = res % d1, res // d1
            part = jnp.concatenate(
                [scr2[c, pl.ds(r1 * (tm // d1) + q, tm // d2, stride=step), :] for c in range(nl)], axis=1)
            qkv16_ref[0, res, :, cs] = part.astype(BF16)
    for j in range(3):
        dqkv_ref[:, j * DN_W:(j + 1) * DN_W] = proj[3 + j].astype(BF16)
    dz_ref[...] = proj[6].astype(BF16)
    gbc_ref[...] = jnp.dot(hb, ws_ref[...], preferred_element_type=F32)


def _inproj(x2, mod, norm_w, w_in, S):
    T, D = x2.shape
    tm = PROJ_ROWS
    B, nt = T // S, S // tm
    wm = w_in[:, :MAIN_COLS].astype(BF16)
    ws = jnp.zeros((D, LANES), F32).at[:, :2 * DN_HEADS].set(w_in[:, MAIN_COLS:]).astype(BF16)
    return pl.pallas_call(
        _inproj_kernel,
        grid=(T // tm,),
        in_specs=[pl.BlockSpec((tm, D), lambda i: (i, 0)),
                  pl.BlockSpec((1, 6, D), lambda i: (i * tm // S, 0, 0)),
                  pl.BlockSpec((1, D), lambda i: (0, 0)),
                  pl.BlockSpec((D, MAIN_COLS), lambda i: (0, 0)),
                  pl.BlockSpec((D, LANES), lambda i: (0, 0))],
        out_specs=[pl.BlockSpec((tm, 3 * ATTN_W), lambda i: (i, 0))]
        + [pl.BlockSpec((1, d, tm // d, 3 * ATTN_W), lambda i: (i // nt, 0, i % nt, 0)) for d in DILATIONS[1:]]
        + [pl.BlockSpec((tm, DN_CONV), lambda i: (i, 0)),
           pl.BlockSpec((tm, DN_W), lambda i: (i, 0)),
           pl.BlockSpec((tm, LANES), lambda i: (i, 0))],
        out_shape=[jax.ShapeDtypeStruct((T, 3 * ATTN_W), BF16)]
        + [jax.ShapeDtypeStruct((B, d, S // d, 3 * ATTN_W), BF16) for d in DILATIONS[1:]]
        + [jax.ShapeDtypeStruct((T, DN_CONV), BF16),
           jax.ShapeDtypeStruct((T, DN_W), BF16),
           jax.ShapeDtypeStruct((T, LANES), F32)],
        scratch_shapes=[pltpu.VMEM((ATTN_W // LANES, tm, LANES), F32),
                        pltpu.VMEM((ATTN_W // LANES, tm, LANES), F32)],
        compiler_params=_cparams(("arbitrary",)),
        name="inproj",
    )(x2, mod, norm_w.reshape(1, D), wm, ws)


def _attn_kernel(q_ref, kc_ref, kp_ref, vc_ref, vp_ref, o_ref, lse_ref, kf, vf, *, qb):
    n = pl.program_id(2)
    kf[0:ATTN_BLK, :] = kp_ref[0, 0]
    kf[ATTN_BLK:, :] = kc_ref[0, 0]
    vf[0:ATTN_BLK, :] = vp_ref[0, 0]
    vf[ATTN_BLK:, :] = vc_ref[0, 0]
    row = lax.broadcasted_iota(I32, (ATTN_BLK, 2 * ATTN_BLK), 0)
    col = lax.broadcasted_iota(I32, (ATTN_BLK, 2 * ATTN_BLK), 1)
    band = jnp.logical_or(jnp.logical_and(col < ATTN_BLK, col >= row),
                          jnp.logical_and(col >= ATTN_BLK, col - ATTN_BLK <= row))
    lane = lax.broadcasted_iota(I32, (ATTN_BLK, LANES), 1)
    lo = lane < ATTN_HD

    def sub(j, carry):
        r0 = pl.multiple_of(j * ATTN_BLK, ATTN_BLK)
        first_col = jnp.where(jnp.logical_and(n == 0, j == 0), ATTN_BLK, 0)
        mask = jnp.logical_and(band, col >= first_col)
        npair = ATTN_W // LANES
        cols = [slice(hp * LANES, (hp + 1) * LANES) for hp in range(npair)]
        heads = [(hp, half) for hp in range(npair) for half in range(2)]
        scores = []
        for hp, half in heads:
            q2 = q_ref[0, 0, pl.ds(r0, ATTN_BLK), cols[hp]]
            qm = jnp.where(lo if half == 0 else jnp.logical_not(lo), q2, jnp.zeros_like(q2))
            scores.append(jnp.where(mask, _nt(qm, kf[pl.ds(r0, 2 * ATTN_BLK), cols[hp]]), NEG))
        maxes = [jnp.max(s, axis=-1, keepdims=True) for s in scores]
        probs = [jnp.exp(s - m) for s, m in zip(scores, maxes)]
        dens = [jnp.sum(p, axis=-1, keepdims=True) for p in probs]
        accs = [jnp.dot(p.astype(BF16), vf[pl.ds(r0, 2 * ATTN_BLK), cols[hp]], preferred_element_type=F32)
                for p, (hp, _) in zip(probs, heads)]
        outs = [a / d for a, d in zip(accs, dens)]
        lse_tile = jnp.zeros((ATTN_BLK, LANES), F32)
        for h, (m, d) in enumerate(zip(maxes, dens)):
            lse_tile = jnp.where(lane == h, m + jnp.log(d), lse_tile)
        for hp in range(npair):
            o_ref[0, 0, pl.ds(r0, ATTN_BLK), cols[hp]] = jnp.where(lo, outs[2 * hp], outs[2 * hp + 1]).astype(BF16)
        lse_ref[0, 0, pl.ds(r0, ATTN_BLK), :] = lse_tile
        return carry

    lax.fori_loop(0, qb // ATTN_BLK, sub, 0, unroll=True)


def _attn_branch(qkv, d):
    B, _, L, _ = qkv.shape
    qb = min(ATTN_ROWS, L)
    nsub = qb // ATTN_BLK
    cur = lambda c: pl.BlockSpec((1, 1, qb, ATTN_W), lambda b, r, n: (b, r, n, c))
    prev = lambda c: pl.BlockSpec((1, 1, ATTN_BLK, ATTN_W),
                                  lambda b, r, n: (b, r, jnp.maximum(n * nsub - 1, 0), c))
    return pl.pallas_call(
        functools.partial(_attn_kernel, qb=qb),
        grid=(B, d, L // qb),
        in_specs=[cur(0), cur(1), prev(1), cur(2), prev(2)],
        out_specs=[pl.BlockSpec((1, 1, qb, ATTN_W), lambda b, r, n: (b, r, n, 0)),
                   pl.BlockSpec((1, 1, qb, LANES), lambda b, r, n: (b, r, n, 0))],
        out_shape=[jax.ShapeDtypeStruct((B, d, L, ATTN_W), BF16),
                   jax.ShapeDtypeStruct((B, d, L, LANES), F32)],
        scratch_shapes=[pltpu.VMEM((qb + ATTN_BLK, ATTN_W), BF16),
                        pltpu.VMEM((qb + ATTN_BLK, ATTN_W), BF16)],
        compiler_params=_cparams(("arbitrary", "arbitrary", "arbitrary")),
        name=f"attn_d{d}",
    )(qkv, qkv, qkv, qkv, qkv)


def _gdn_kernel(x_ref, z_ref, g_ref, cw_ref, prm_ref, nw_ref, o_ref, xext, yc, s0, s1, *, rb, nseq):
    i = pl.program_id(1)

    @pl.when(i == 0)
    def _():
        xext[:, 0:8, :] = jnp.zeros((nseq, 8, DN_CONV), F32)
        s0[...] = jnp.zeros_like(s0)
        s1[...] = jnp.zeros_like(s1)

    @pl.when(i > 0)
    def _():
        xext[:, 0:8, :] = xext[:, rb:rb + 8, :]

    for q in range(nseq):
        xext[q, 8:, :] = x_ref[q].astype(F32)
        y = cw_ref[CONV_K - 1:CONV_K, :] * xext[q, 8:8 + rb, :]
        for j in range(CONV_K - 1):
            off = 8 - (CONV_K - 1) + j
            y = y + cw_ref[j:j + 1, :] * xext[q, off:off + rb, :]
        yc[q] = y * _sigmoid(y)

    C = DN_CHUNK
    H = DN_HEADS
    CW = H * C
    dot = functools.partial(jnp.dot, preferred_element_type=F32)

    def iota(shape, d):
        return lax.broadcasted_iota(I32, shape, d)

    ltri_b = jnp.where(iota((C, C), 0) >= iota((C, C), 1), 1.0, 0.0).astype(BF16)
    lane = iota((C, LANES), 1)
    blane = lane < H
    glane = jnp.logical_and(lane >= H, lane < 2 * H)
    e512 = jnp.where(jnp.logical_or(iota((LANES, DN_W), 1) // DN_D == iota((LANES, DN_W), 0),
                                    iota((LANES, DN_W), 1) // DN_D == iota((LANES, DN_W), 0) - H),
                     1.0, 0.0).astype(BF16)
    e256 = jnp.where(iota((LANES, CW), 1) // C == iota((LANES, CW), 0) - H, 1.0, 0.0).astype(BF16)
    row4 = iota((C, CW), 0)
    col4 = iota((C, CW), 1) % C
    eye4 = jnp.where(row4 == col4, 1.0, 0.0).astype(F32)
    blk = [iota((C, CW), 1) // C == h for h in range(H)]
    bd_cc = iota((CW, CW), 0) // C == iota((CW, CW), 1) // C
    bd_pair = iota((CW, CW), 0) // DN_D == iota((CW, CW), 1) // DN_D
    rt_mask = iota((CW, DN_W), 0) // C == iota((CW, DN_W), 1) // DN_D
    neg_a = -jnp.exp(prm_ref[0:1, :])
    dtb = prm_ref[1:2, :]
    nw = nw_ref[...]

    def hilo(x):
        hi = x.astype(BF16)
        return hi, (x - hi.astype(F32)).astype(BF16)

    def heads(a, w):
        return [a[:, h * w:(h + 1) * w] for h in range(H)]

    def l2n(a, mult):
        return jnp.concatenate(
            [p * (lax.rsqrt(jnp.sum(p * p, axis=-1, keepdims=True) + EPS) * mult) for p in heads(a, DN_D)],
            axis=1)

    def stack4(a):
        return jnp.concatenate([a, a, a, a], axis=0)

    zb = jnp.zeros((), BF16)
    nchunk = rb // C
    chunks = []
    for c, sq in ((c, sq) for c in range(nchunk) for sq in range(nseq)):
        rs = slice(c * C, (c + 1) * C)
        G = g_ref[sq, rs, :]
        xg = G + dtb
        gv = jnp.where(glane, neg_a * (jnp.maximum(xg, 0.0) + jnp.log1p(jnp.exp(-jnp.abs(xg)))), 0.0)
        be = jnp.where(blane, _sigmoid(G), 0.0)
        g_hi, g_lo = hilo(gv)
        gcum = dot(ltri_b, g_hi) + dot(ltri_b, g_lo)
        gtot = gcum[C - 1:C, :]
        eg = jnp.where(glane, jnp.exp(gcum), 0.0)
        ek = jnp.where(glane, jnp.exp(gtot - gcum), 0.0)
        ex = dot(jnp.concatenate([be, eg, ek], axis=0).astype(BF16), e512)
        bexp, egexp, ekexp = ex[0:C], ex[C:2 * C], ex[2 * C:3 * C]
        gexp = dot(g_hi, e256) + dot(g_lo, e256)
        d_hi, d_lo = hilo(jnp.where(row4 > col4, gexp, 0.0))
        diff = dot(ltri_b, d_hi) + dot(ltri_b, d_lo)
        decay = jnp.exp(jnp.where(row4 >= col4, diff, NEG))

        q4 = l2n(yc[sq, rs, 0:DN_W], DN_D ** -0.5)
        k4 = l2n(yc[sq, rs, DN_W:2 * DN_W], 1.0)
        v4 = yc[sq, rs, 2 * DN_W:3 * DN_W]
        kb4 = k4 * bexp
        vb4 = v4 * bexp
        rt = jnp.where(rt_mask, stack4(k4.astype(BF16)), jnp.zeros((), BF16))
        ai = _nt(jnp.concatenate([kb4, q4], axis=0).astype(BF16), rt)
        a4 = jnp.where(row4 > col4, ai[0:C] * decay, 0.0)
        pb = a4.astype(BF16)
        chunks.append(dict(
            sq=sq, rs=rs, q4=q4, k4=k4, kb4=kb4, vb4=vb4, egexp=egexp, ekexp=ekexp,
            intra=ai[C:2 * C] * decay, t4=eye4 - a4, pb=pb,
            bd=jnp.where(bd_cc, stack4(pb), zb)))

    for _ in range(5):
        for ch in chunks:
            ch["pb"] = dot(ch["pb"], ch["bd"]).astype(BF16)
        for ch in chunks:
            ch["bd"] = jnp.where(bd_cc, stack4(ch["pb"]), zb)
        for ch in chunks:
            ch["t4"] = ch["t4"] + dot(ch["t4"].astype(BF16), ch["bd"])

    for ch in chunks:
        q4, k4, kb4, vb4, egexp, ekexp = (ch[n] for n in ("q4", "k4", "kb4", "vb4", "egexp", "ekexp"))
        t4b = ch["t4"].astype(BF16)
        lstk = jnp.concatenate([jnp.where(blk[h], t4b, zb) for h in range(H)], axis=0)
        kbg4 = kb4 * egexp
        rstk = jnp.concatenate(
            [jnp.concatenate([vb, kbg], axis=1) for vb, kbg in zip(heads(vb4, DN_D), heads(kbg4, DN_D))],
            axis=0).astype(BF16)
        uw = dot(lstk, rstk)
        u4 = jnp.concatenate([uw[h * C:(h + 1) * C, 0:DN_D] for h in range(H)], axis=1)
        w4 = jnp.concatenate([uw[h * C:(h + 1) * C, DN_D:2 * DN_D] for h in range(H)], axis=1)
        ib = ch["intra"].astype(BF16)
        ch.update(
            u4=u4, wq=jnp.concatenate([w4, q4 * egexp], axis=0).astype(BF16),
            kd4=(k4 * ekexp).astype(BF16), gl4=egexp[C - 1:C, :],
            lint=jnp.concatenate([jnp.where(blk[h], ib, zb) for h in range(H)], axis=0))

    for ch in chunks:
        sq, rs, u4, wq, kd4, gl4, lint = (ch[n] for n in ("sq", "rs", "u4", "wq", "kd4", "gl4", "lint"))
        ra = dot(wq[:, 0:CW], s0[sq].astype(BF16))
        rc = dot(wq[:, CW:2 * CW], s1[sq].astype(BF16))
        vn = u4 - jnp.concatenate([ra[0:C], rc[0:C]], axis=1)
        vnb = vn.astype(BF16)
        oi = dot(lint, jnp.concatenate(heads(vnb, DN_D), axis=0))
        o = (jnp.concatenate([ra[C:2 * C], rc[C:2 * C]], axis=1)
             + jnp.concatenate([oi[h * C:(h + 1) * C] for h in range(H)], axis=1))
        tn = (((0,), (0,)), ((), ()))
        s0[sq] = s0[sq] * gl4[:, 0:CW] + jnp.where(
            bd_pair, lax.dot_general(kd4[:, 0:CW], vnb[:, 0:CW], tn, preferred_element_type=F32), 0.0)
        s1[sq] = s1[sq] * gl4[:, CW:2 * CW] + jnp.where(
            bd_pair, lax.dot_general(kd4[:, CW:2 * CW], vnb[:, CW:2 * CW], tn, preferred_element_type=F32), 0.0)

        z = z_ref[sq, rs, :].astype(F32)
        on = jnp.concatenate(
            [p * lax.rsqrt(jnp.mean(p * p, axis=-1, keepdims=True) + EPS) * nw for p in heads(o, DN_D)], axis=1)
        o_ref[sq, rs, :] = (on * (z * _sigmoid(z))).astype(BF16)


def _gdn(dqkv, dz, gbc, conv_w, A_log, dt_bias, dn_norm_w, B, S):
    nseq = 2 if B % 2 == 0 else 1
    rb = GDN_ROWS // nseq
    prm = jnp.zeros((2, LANES), F32)
    prm = prm.at[0, DN_HEADS:2 * DN_HEADS].set(A_log.astype(F32))
    prm = prm.at[1, DN_HEADS:2 * DN_HEADS].set(dt_bias.astype(F32))
    out = pl.pallas_call(
        functools.partial(_gdn_kernel, rb=rb, nseq=nseq),
        grid=(B // nseq, S // rb),
        in_specs=[pl.BlockSpec((nseq, rb, DN_CONV), lambda b, i: (b, i, 0)),
                  pl.BlockSpec((nseq, rb, DN_W), lambda b, i: (b, i, 0)),
                  pl.BlockSpec((nseq, rb, LANES), lambda b, i: (b, i, 0)),
                  pl.BlockSpec((CONV_K, DN_CONV), lambda b, i: (0, 0)),
                  pl.BlockSpec((2, LANES), lambda b, i: (0, 0)),
                  pl.BlockSpec((1, DN_D), lambda b, i: (0, 0))],
        out_specs=pl.BlockSpec((nseq, rb, DN_W), lambda b, i: (b, i, 0)),
        out_shape=jax.ShapeDtypeStruct((B, S, DN_W), BF16),
        scratch_shapes=[pltpu.VMEM((nseq, rb + 8, DN_CONV), F32),
                        pltpu.VMEM((nseq, rb, DN_CONV), F32),
                        pltpu.VMEM((nseq, 2 * DN_D, 2 * DN_D), F32),
                        pltpu.VMEM((nseq, 2 * DN_D, 2 * DN_D), F32)],
        compiler_params=_cparams(("arbitrary", "arbitrary")),
        name="gdn",
    )(dqkv.reshape(B, S, DN_CONV), dz.reshape(B, S, DN_W), gbc.reshape(B, S, LANES),
      conv_w, prm, dn_norm_w.reshape(1, DN_D))
    return out.reshape(B * S, DN_W)


def _out_kernel(o1_ref, o2_ref, o3_ref, l1_ref, l2_ref, l3_ref, dn_ref, x_ref, mod_ref, wo_ref,
                n2_ref, wr_ref, br_ref,
                x1_ref, h2_ref, te_ref, rk_ref, gtc_ref, cnt_ref, base, scr, *, tm):
    i = pl.program_id(0)

    @pl.when(i == 0)
    def _():
        base[...] = jnp.zeros_like(base)

    def natural(ref, d):
        if d == 1:
            return ref[0, 0].astype(F32)
        nl = ref.shape[-1] // LANES
        for res in range(d):
            blk = ref[0, res].astype(F32)
            for c in range(nl):
                scr[c, pl.ds(res, tm // d, stride=d), :] = blk[:, c * LANES:(c + 1) * LANES]
        return jnp.concatenate([scr[c] for c in range(nl)], axis=1)

    l1, l2, l3 = (natural(r, d) for r, d in zip((l1_ref, l2_ref, l3_ref), DILATIONS))
    mx = jnp.maximum(jnp.maximum(l1, l2), l3)
    e1, e2, e3 = jnp.exp(l1 - mx), jnp.exp(l2 - mx), jnp.exp(l3 - mx)
    zs = e1 + e2 + e3
    er = lax.broadcasted_iota(I32, (LANES, ATTN_W), 0)
    ec = lax.broadcasted_iota(I32, (LANES, ATTN_W), 1)
    expand = jnp.where(ec // ATTN_HD == er, 1.0, 0.0).astype(BF16)
    attn = jnp.zeros((tm, ATTN_W), F32)
    for e, o_ref, d in zip((e1, e2, e3), (o1_ref, o2_ref, o3_ref), DILATIONS):
        wgt = jnp.dot((e / zs).astype(BF16), expand, preferred_element_type=F32)
        attn = attn + wgt * natural(o_ref, d)
    mix = (jnp.dot(attn.astype(BF16), wo_ref[0:ATTN_W, :], preferred_element_type=F32)
           + jnp.dot(dn_ref[...], wo_ref[ATTN_W:, :], preferred_element_type=F32))
    x1 = x_ref[...] + mod_ref[0, 2:3, :] * mix
    x1_ref[...] = x1
    ms = jnp.mean(x1 * x1, axis=-1, keepdims=True)
    h2 = x1 * lax.rsqrt(ms + EPS) * n2_ref[...]
    h2 = h2 * (1.0 + mod_ref[0, 4:5, :]) + mod_ref[0, 3:4, :]
    _store_slabs(h2_ref, h2)

    lg = _nt(wr_ref[...], h2, precision=HI) + br_ref[...]
    eidx = lax.broadcasted_iota(I32, (N_EXPERTS, tm), 0)
    vals, idxs, sels = [], [], []
    for _ in range(TOP_K):
        m = jnp.max(lg, axis=0, keepdims=True)
        idx = jnp.min(jnp.where(lg == m, eidx, N_EXPERTS), axis=0, keepdims=True)
        sel = eidx == idx
        vals.append(m)
        idxs.append(idx)
        sels.append(sel)
        lg = jnp.where(sel, -jnp.inf, lg)
    ex = [jnp.exp(v - vals[0]) for v in vals]
    den = ex[0] + ex[1] + ex[2] + ex[3]
    gates = [e / den for e in ex]

    msum = jnp.zeros((N_EXPERTS, tm), F32)
    for sel in sels:
        msum = msum + jnp.where(sel, 1.0, 0.0)
    tr = lax.broadcasted_iota(I32, (tm, tm), 0)
    tc = lax.broadcasted_iota(I32, (tm, tm), 1)
    upper = jnp.where(tr <= tc, 1.0, 0.0).astype(BF16)
    incl = jnp.dot(msum.astype(BF16), upper, preferred_element_type=F32)
    pos = base[:, 0:1] + (incl - msum)
    sub8 = lax.broadcasted_iota(I32, (8, tm), 0)
    te = jnp.zeros((8, tm), I32)
    rk = jnp.zeros((8, tm), I32)
    gt = jnp.zeros((8, tm), F32)
    for k in range(TOP_K):
        rank_k = jnp.sum(jnp.where(sels[k], pos, 0.0), axis=0, keepdims=True).astype(I32)
        te = jnp.where(sub8 == k, idxs[k], te)
        rk = jnp.where(sub8 == k, rank_k, rk)
        gt = jnp.where(sub8 == k, gates[k], gt)
    te_ref[...] = te
    rk_ref[...] = rk
    gtc_ref[...] = jnp.transpose(jnp.concatenate([gt, jnp.zeros((LANES - 8, tm), F32)], axis=0))
    base[...] = base[...] + jnp.sum(msum, axis=1, keepdims=True)
    cnt_ref[...] = base[...].astype(I32)


def _outproj(o1, o2, o3, l1, l2, l3, dn, x2, mod, w_out, norm2_w, w_router, b_router, S):
    T, D = x2.shape
    tm = PROJ_ROWS
    nt = S // tm
    row = lambda w: pl.BlockSpec((tm, w), lambda i: (i, 0))
    res = lambda d, w: pl.BlockSpec((1, d, tm // d, w), lambda i: (i // nt, 0, i % nt, 0))
    colb = pl.BlockSpec((8, tm), lambda i: (0, i))
    return pl.pallas_call(
        functools.partial(_out_kernel, tm=tm),
        grid=(T // tm,),
        in_specs=[res(d, ATTN_W) for d in DILATIONS] + [res(d, LANES) for d in DILATIONS]
        + [row(DN_W), row(D),
                  pl.BlockSpec((1, 6, D), lambda i: (i * tm // S, 0, 0)),
                  pl.BlockSpec((D, D), lambda i: (0, 0)),
                  pl.BlockSpec((1, D), lambda i: (0, 0)),
                  pl.BlockSpec((N_EXPERTS, D), lambda i: (0, 0)),
                  pl.BlockSpec((N_EXPERTS, 1), lambda i: (0, 0))],
        out_specs=[row(D), pl.BlockSpec((tm * (D // LANES), LANES), lambda i: (i, 0)), colb, colb,
                   row(LANES), pl.BlockSpec((N_EXPERTS, LANES), lambda i: (0, 0))],
        out_shape=[jax.ShapeDtypeStruct((T, D), F32),
                   jax.ShapeDtypeStruct((T * (D // LANES), LANES), F32),
                   jax.ShapeDtypeStruct((8, T), I32),
                   jax.ShapeDtypeStruct((8, T), I32),
                   jax.ShapeDtypeStruct((T, LANES), F32),
                   jax.ShapeDtypeStruct((N_EXPERTS, LANES), I32)],
        scratch_shapes=[pltpu.VMEM((N_EXPERTS, LANES), F32),
                        pltpu.VMEM((ATTN_W // LANES, tm, LANES), F32)],
        compiler_params=_cparams(("arbitrary",)),
        name="outproj_router",
    )(o1, o2, o3, l1, l2, l3, dn, x2, mod, w_out.astype(BF16), norm2_w.reshape(1, D),
      jnp.transpose(w_router), b_router.reshape(N_EXPERTS, 1))


def _dest_kernel(ps_ref, te_ref, rk_ref, d_ref):
    te = te_ref[...]
    acc = jnp.zeros(te.shape, I32)
    for e in range(N_EXPERTS):
        acc = jnp.where(te == e, ps_ref[e], acc)
    d_ref[...] = acc + rk_ref[...]


def _dest(pstart, te, rk):
    T = te.shape[1]
    tb = DEST_COLS
    return pl.pallas_call(
        _dest_kernel,
        grid_spec=pltpu.PrefetchScalarGridSpec(
            num_scalar_prefetch=1,
            grid=(T // tb,),
            in_specs=[pl.BlockSpec((8, tb), lambda i, ps: (0, i)),
                      pl.BlockSpec((8, tb), lambda i, ps: (0, i))],
            out_specs=pl.BlockSpec((8, tb), lambda i, ps: (0, i))),
        out_shape=jax.ShapeDtypeStruct((8, T), I32),
        compiler_params=_cparams(("arbitrary",)),
        name="dest_rows",
    )(pstart, te, rk)


SC_CORES = 2
SC_SUBCORES = 16
SC_IDX_CHUNK = 128


def _invperm(dest_flat, P):
    N = dest_flat.shape[0]
    nch = N // (SC_SUBCORES * SC_IDX_CHUNK)
    half = P // SC_CORES
    per_out = half // SC_SUBCORES
    assert N % (SC_SUBCORES * SC_IDX_CHUNK) == 0 and P % (SC_CORES * SC_SUBCORES * 8) == 0
    mesh = plsc.VectorSubcoreMesh(core_axis_name="c", subcore_axis_name="s",
                                  num_cores=SC_CORES, num_subcores=SC_SUBCORES)

    @functools.partial(
        pl.kernel, mesh=mesh, out_type=jax.ShapeDtypeStruct((P,), I32),
        scratch_types=[pltpu.VMEM((nch, SC_IDX_CHUNK), I32), pltpu.VMEM((nch, SC_IDX_CHUNK), I32),
                       pltpu.VMEM_SHARED((P,), I32), pltpu.VMEM((per_out,), I32), pltpu.SemaphoreType.DMA])
    def scatter_codes(idx_hbm, val_hbm, out_hbm, idx_v, val_v, table, stage, sem):
        core = lax.axis_index("c")
        sub = lax.axis_index("s")
        pltpu.sync_copy(idx_hbm.at[sub], idx_v)
        pltpu.sync_copy(val_hbm.at[sub], val_v)

        @pl.loop(0, nch)
        def _(j):
            pltpu.async_copy(val_v.at[j], table.at[idx_v.at[j]], sem)

        @pl.loop(0, nch)
        def _(j):
            pltpu.make_async_copy(val_v.at[j], table.at[idx_v.at[j]], sem).wait()

        plsc.subcore_barrier()
        off = pl.multiple_of(core * half + sub * per_out, 8)
        pltpu.sync_copy(table.at[pl.ds(off, per_out)], stage)
        pltpu.sync_copy(stage, out_hbm.at[pl.ds(off, per_out)])

    vals = jnp.arange(N, dtype=I32)
    shape = (SC_SUBCORES, nch, SC_IDX_CHUNK)
    return scatter_codes(dest_flat.reshape(shape), vals.reshape(shape))


def _row_copy(src, dst, sem):
    return pltpu.make_async_copy(src, dst, sem)


def _moe_kernel(be_ref, nv_ref, cc_ref, cn_ref, h2_ref, w1_ref, b1_ref, w2_ref, b2_ref, y4_ref,
                xbuf, ybuf, w1b, w2b, gsem, ssem, *, F, T, D, tme, nb):
    i = pl.program_id(0)
    s = i % 2
    ns = D // LANES

    def rows(first, n):
        return pl.ds(pl.multiple_of(first * ns, ns), n * ns)

    def gather_copy(tok, p, slot):
        return _row_copy(h2_ref.at[rows(tok, 1)], xbuf.at[slot, rows(p, 1)], gsem.at[slot])

    def token_of(code):
        return code & (T - 1) if T & (T - 1) == 0 else code % T

    def issue_rows(start_row, nv):
        ng = nv // ROW_UNROLL

        def grp(g, c):
            for j in range(ROW_UNROLL):
                start_row(g * ROW_UNROLL + j, j % 2)
            return c
        lax.fori_loop(0, ng, grp, 0)

        def one(p, c):
            start_row(p, 0)
            return c
        lax.fori_loop(ng * ROW_UNROLL, nv, one, 0)

    def start_gather(code_ref, nv, slot):
        issue_rows(lambda p, pri: gather_copy(token_of(code_ref[0, 0, p]), p, slot).start(priority=pri), nv)

    def wait_rows(make, nv):
        @pl.when(nv > 0)
        def _():
            make(nv).wait()

    def gathered(n, slot):
        return _row_copy(h2_ref.at[rows(0, n)], xbuf.at[slot, rows(0, n)], gsem.at[slot])

    def scattered(n, slot):
        return _row_copy(ybuf.at[slot, rows(0, n)], y4_ref.at[rows(0, n)], ssem.at[slot])

    @pl.when(i == 0)
    def _():
        start_gather(cc_ref, tme, 0)

    nv = nv_ref[i]
    gathered(tme, s).wait()

    @pl.when(i >= 2)
    def _():
        wait_rows(lambda n: scattered(n, s), nv_ref[i - 2])

    @pl.when(jnp.logical_or(i == 0, be_ref[i] != be_ref[jnp.maximum(i - 1, 0)]))
    def _():
        w1b[...] = w1_ref[0].astype(BF16)
        w2b[...] = w2_ref[0].astype(BF16)

    def compute(slot):
        x = _load_slabs(xbuf, tme, D, lead=(slot,)).astype(BF16)
        for p in range(tme):
            gather_copy(token_of(cn_ref[0, 0, p]), p, 1 - slot).start(priority=p % 2)
        hgu = jnp.dot(x, w1b[...], preferred_element_type=F32) + b1_ref[0]
        gate = jnp.minimum(hgu[:, :F], SWIGLU_LIMIT)
        up = jnp.clip(hgu[:, F:], -SWIGLU_LIMIT, SWIGLU_LIMIT)
        act = gate * _sigmoid(SWIGLU_ALPHA * gate) * (up + 1.0)
        y = jnp.dot(act.astype(BF16), w2b[...], preferred_element_type=F32) + b2_ref[0]
        _store_slabs(ybuf, y, lead=(slot,))

        issue_rows(lambda p, pri: _row_copy(ybuf.at[slot, rows(p, 1)], y4_ref.at[rows(cc_ref[0, 0, p], 1)],
                                            ssem.at[slot]).start(priority=pri), nv)

    for slot in range(2):
        pl.when(jnp.logical_and(nv > 0, s == slot))(functools.partial(compute, slot))

    @pl.when(jnp.logical_and(nv == 0, i + 1 < nb))
    def _():
        start_gather(cn_ref, tme, 1 - s)

    @pl.when(i == nb - 1)
    def _():
        @pl.when(nv > 0)
        def _():
            gathered(tme, 1 - s).wait()
        wait_rows(lambda n: scattered(n, s), nv)
        if nb > 1:
            wait_rows(lambda n: scattered(n, 1 - s), nv_ref[i - 1])


def _experts(blk_exp, blk_valid, codes, h2s, w1, b1, w2, b2, tme):
    E, D, F2 = w1.shape
    F = F2 // 2
    ns = D // LANES
    T = h2s.shape[0] // ns
    nb = blk_exp.shape[0]
    codes3 = codes.reshape(nb, 1, tme)
    wspec = lambda shape: pl.BlockSpec(shape, lambda i, be, nv: (be[i], 0, 0))
    cspec = lambda off: pl.BlockSpec((1, 1, tme), lambda i, be, nv: (jnp.minimum(i + off, nb - 1), 0, 0),
                                     memory_space=pltpu.SMEM)
    return pl.pallas_call(
        functools.partial(_moe_kernel, F=F, T=T, D=D, tme=tme, nb=nb),
        grid_spec=pltpu.PrefetchScalarGridSpec(
            num_scalar_prefetch=2,
            grid=(nb,),
            in_specs=[cspec(0), cspec(1),
                      pl.BlockSpec(memory_space=pl.ANY),
                      wspec((1, D, F2)), wspec((1, 1, F2)), wspec((1, F, D)), wspec((1, 1, D))],
            out_specs=pl.BlockSpec(memory_space=pl.ANY),
            scratch_shapes=[pltpu.VMEM((2, tme * ns, LANES), F32), pltpu.VMEM((2, tme * ns, LANES), F32),
                            pltpu.VMEM((D, F2), BF16), pltpu.VMEM((F, D), BF16),
                            pltpu.SemaphoreType.DMA((2,)), pltpu.SemaphoreType.DMA((2,))]),
        out_shape=jax.ShapeDtypeStruct((TOP_K * T * ns, LANES), F32),
        compiler_params=_cparams(("arbitrary",)),
        name="experts",
    )(blk_exp, blk_valid, codes3, codes3, h2s, w1, b1.reshape(E, 1, F2), w2, b2.reshape(E, 1, D))


def _comb_kernel(g_ref, x1_ref, mod_ref, fw_ref, y0_ref, y1_ref, y2_ref, y3_ref, o_ref):
    g = g_ref[...]
    n, d = x1_ref.shape
    y = g[:, 0:1] * _load_slabs(y0_ref, n, d)
    for k, y_ref in ((1, y1_ref), (2, y2_ref), (3, y3_ref)):
        y = y + g[:, k:k + 1] * _load_slabs(y_ref, n, d)
    x2 = x1_ref[...] + mod_ref[0, 5:6, :] * y
    ms = jnp.mean(x2 * x2, axis=-1, keepdims=True)
    o_ref[...] = x2 * lax.rsqrt(ms + EPS) * fw_ref[...]


def _combine(gtc, x1, mod, final_w, y4, S):
    T, D = x1.shape
    tmc = COMBINE_ROWS
    nt = T // tmc
    yspec = lambda k: pl.BlockSpec((tmc * (D // LANES), LANES), lambda i: (k * nt + i, 0))
    return pl.pallas_call(
        _comb_kernel,
        grid=(nt,),
        in_specs=[pl.BlockSpec((tmc, LANES), lambda i: (i, 0)),
                  pl.BlockSpec((tmc, D), lambda i: (i, 0)),
                  pl.BlockSpec((1, 6, D), lambda i: (i * tmc // S, 0, 0)),
                  pl.BlockSpec((1, D), lambda i: (0, 0)),
                  yspec(0), yspec(1), yspec(2), yspec(3)],
        out_specs=pl.BlockSpec((tmc, D), lambda i: (i, 0)),
        out_shape=jax.ShapeDtypeStruct((T, D), F32),
        compiler_params=_cparams(("arbitrary",)),
        name="combine",
    )(gtc, x1, mod, final_w.reshape(1, D), y4, y4, y4, y4)


def _layer(x2, mod, B, S, norm1_w, w_in, conv_w, A_log, dt_bias, dn_norm_w, w_out,
           norm2_w, w_router, b_router, w1, b1, w2, b2):
    T, D = x2.shape
    qkv, qkv4, qkv16, dqkv, dz, gbc = _inproj(x2, mod, norm1_w, w_in, S)
    views = (qkv.reshape(B, 1, S, 3 * ATTN_W), qkv4, qkv16)
    branches = [_attn_branch(v, d) for v, d in zip(views, DILATIONS)]
    dn = _gdn(dqkv, dz, gbc, conv_w, A_log, dt_bias, dn_norm_w, B, S)
    (o1, l1), (o2, l2), (o3, l3) = branches
    x1, h2, te, rk, gtc, cnt = _outproj(o1, o2, o3, l1, l2, l3, dn, x2, mod, w_out, norm2_w,
                                        w_router, b_router, S)
    tme = MOE_ROWS
    P = T * TOP_K + N_EXPERTS * tme
    counts = cnt[:, 0]
    padded = (counts + tme - 1) // tme * tme
    pend = jnp.cumsum(padded)
    pstart = (pend - padded).astype(I32)
    blk_start = jnp.arange(P // tme, dtype=I32) * tme
    blk_exp = jnp.minimum(jnp.sum((pend[None, :] <= blk_start[:, None]).astype(I32), axis=1),
                          N_EXPERTS - 1).astype(I32)
    blk_valid = jnp.clip(pstart[blk_exp] + counts[blk_exp] - blk_start, 0, tme).astype(I32)
    dest = _dest(pstart, te, rk)
    codes = _invperm(dest[:TOP_K].reshape(TOP_K * T), P)
    y4 = _experts(blk_exp, blk_valid, codes, h2, w1, b1, w2, b2, tme)
    return x1, gtc, y4


def kernel(x, c, w_ada, b_ada, norm1_w, w_in, conv_w, A_log, dt_bias, dn_norm_w, w_out, norm2_w,
           w_router, b_router, w1, b1, w2, b2, final_norm_w):
    B, S, D = x.shape
    depth = w_ada.shape[0]
    assert depth == 1, "one layer: the final norm is fused into the combine step"
    assert S % (ATTN_BLK * DILATIONS[-1]) == 0 and S % max(PROJ_ROWS, GDN_ROWS, COMBINE_ROWS) == 0
    assert (B * S) % DEST_COLS == 0 and w_ada.shape[2] % ADA_COLS == 0
    x2 = x.reshape(B * S, D)
    mod = _ada(c, w_ada[0], b_ada[0])
    x1, gtc, y4 = _layer(x2, mod, B, S, norm1_w[0], w_in[0], conv_w[0], A_log[0], dt_bias[0],
                         dn_norm_w[0], w_out[0], norm2_w[0], w_router[0], b_router[0],
                         w1[0], b1[0], w2[0], b2[0])
    out = _combine(gtc, x1, mod, final_norm_w, y4, S)
    return out.reshape(B, S, D)
```

```python
import functools

import jax
import jax.numpy as jnp
from jax import lax
from jax.experimental import pallas as pl
from jax.experimental.pallas import tpu as pltpu
from jax.experimental.pallas import tpu_sc as plsc

F32 = jnp.float32
BF16 = jnp.bfloat16
I32 = jnp.int32
HI = lax.Precision.HIGHEST

LANES = 128
ATTN_HEADS = 8
ATTN_HD = 64
ATTN_W = ATTN_HEADS * ATTN_HD
ATTN_BLK = 128
DILATIONS = (1, 4, 16)
DN_HEADS = 4
DN_D = 128
DN_W = DN_HEADS * DN_D
DN_CONV = 3 * DN_W
CONV_K = 4
DN_CHUNK = 64
N_EXPERTS = 32
TOP_K = 4
MOE_ROWS = 512
ROW_UNROLL = 8
SWIGLU_LIMIT = 7.0
SWIGLU_ALPHA = 1.702
EPS = 1e-6
NEG = -1e30
LOG2E = 1.4426950408889634
MAIN_COLS = 3 * ATTN_W + DN_CONV + DN_W

ADA_COLS = 1024
PROJ_ROWS = 1024
ATTN_ROWS = 1024
GDN_ROWS = 1024
DEST_COLS = 2048
COMBINE_ROWS = 512

V7X_VMEM_BYTES = 64 * 1024 * 1024
VMEM_LIMIT = V7X_VMEM_BYTES - 8 * 1024 * 1024


def _cparams(sem):
    return pltpu.CompilerParams(dimension_semantics=sem, vmem_limit_bytes=VMEM_LIMIT)


def _nt(a, b, **kw):
    return lax.dot_general(a, b, (((1,), (1,)), ((), ())), preferred_element_type=F32, **kw)


def _sigmoid(x):
    return 1.0 / (1.0 + jnp.exp(-x))


def _store_slabs(ref, val, lead=()):
    n, d = val.shape
    ns = d // LANES
    for c in range(ns):
        ref[lead + (pl.ds(c, n, stride=ns), slice(None))] = val[:, c * LANES:(c + 1) * LANES]


def _load_slabs(ref, n, d, lead=()):
    ns = d // LANES
    return jnp.concatenate([ref[lead + (pl.ds(c, n, stride=ns), slice(None))] for c in range(ns)], axis=1)


def _ada_kernel(c_ref, w_ref, b_ref, o_ref):
    c = c_ref[...]
    cond = c * _sigmoid(c)
    o_ref[...] = jnp.dot(cond, w_ref[...], preferred_element_type=F32, precision=HI) + b_ref[...]


def _ada(c, w_ada, b_ada):
    B, D = c.shape
    N = w_ada.shape[1]
    cp = jnp.zeros((8, D), F32).at[:B].set(c)
    tn = ADA_COLS
    out = pl.pallas_call(
        _ada_kernel,
        grid=(N // tn,),
        in_specs=[pl.BlockSpec((8, D), lambda j: (0, 0)),
                  pl.BlockSpec((D, tn), lambda j: (0, j)),
                  pl.BlockSpec((1, tn), lambda j: (0, j))],
        out_specs=pl.BlockSpec((8, tn), lambda j: (0, j)),
        out_shape=jax.ShapeDtypeStruct((8, N), F32),
        compiler_params=_cparams(("arbitrary",)),
        name="ada",
    )(cp, w_ada, b_ada.reshape(1, N))
    return out[:B].reshape(B, 6, D)


def _inproj_kernel(x_ref, mod_ref, nw_ref, wm_ref, ws_ref, qkv_ref, qkv4_ref, qkv16_ref, dqkv_ref, dz_ref,
                   gbc_ref, scr, scr2):
    x = x_ref[...]
    tm = x.shape[0]
    shift = mod_ref[0, 0:1, :]
    scale = mod_ref[0, 1:2, :]
    ms = jnp.mean(x * x, axis=-1, keepdims=True)
    h = x * lax.rsqrt(ms + EPS) * nw_ref[...]
    hb = (h * (1.0 + scale) + shift).astype(BF16)
    nl = ATTN_W // LANES
    proj = [jnp.dot(hb, wm_ref[:, c0:c0 + ATTN_W], preferred_element_type=F32)
            for c0 in range(0, MAIN_COLS, ATTN_W)]
    for j in range(3):
        cs = slice(j * ATTN_W, (j + 1) * ATTN_W)
        r = proj[j]
        if j == 0:
            r = r * (ATTN_HD ** -0.5 * LOG2E)
        qkv_ref[:, cs] = r.astype(BF16)
        d1, d2 = DILATIONS[1], DILATIONS[2]
        step = d2 // d1
        for c in range(nl):
            scr[c] = r[:, c * LANES:(c + 1) * LANES]
        for res in range(d1):
            for c in range(nl):
                scr2[c, res * (tm // d1):(res + 1) * (tm // d1), :] = scr[c, pl.ds(res, tm // d1, stride=d1), :]
            part = jnp.concatenate([scr2[c, res * (tm // d1):(res + 1) * (tm // d1), :] for c in range(nl)],
                                   axis=1)
            qkv4_ref[0, res, :, cs] = part.astype(BF16)
        for res in range(d2):
            r1, q = res % d1, res // d1
            part = jnp.concatenate(
                [scr2[c, pl.ds(r1 * (tm // d1) + q, tm // d2, stride=step), :] for c in range(nl)], axis=1)
            qkv16_ref[0, res, :, cs] = part.astype(BF16)
    for j in range(3):
        dqkv_ref[:, j * DN_W:(j + 1) * DN_W] = proj[3 + j].astype(BF16)
    dz_ref[...] = proj[6].astype(BF16)
    gbc_ref[...] = jnp.dot(hb, ws_ref[...], preferred_element_type=F32)


def _inproj(x2, mod, norm_w, w_in, S):
    T, D = x2.shape
    tm = PROJ_ROWS
    B, nt = T // S, S // tm
    wm = w_in[:, :MAIN_COLS].astype(BF16)
    ws = jnp.zeros((D, LANES), F32).at[:, :2 * DN_HEADS].set(w_in[:, MAIN_COLS:]).astype(BF16)
    return pl.pallas_call(
        _inproj_kernel,
        grid=(T // tm,),
        in_specs=[pl.BlockSpec((tm, D), lambda i: (i, 0)),
                  pl.BlockSpec((1, 6, D), lambda i: (i * tm // S, 0, 0)),
                  pl.BlockSpec((1, D), lambda i: (0, 0)),
                  pl.BlockSpec((D, MAIN_COLS), lambda i: (0, 0)),
                  pl.BlockSpec((D, LANES), lambda i: (0, 0))],
        out_specs=[pl.BlockSpec((tm, 3 * ATTN_W), lambda i: (i, 0))]
        + [pl.BlockSpec((1, d, tm // d, 3 * ATTN_W), lambda i: (i // nt, 0, i % nt, 0)) for d in DILATIONS[1:]]
        + [pl.BlockSpec((tm, DN_CONV), lambda i: (i, 0)),
           pl.BlockSpec((tm, DN_W), lambda i: (i, 0)),
           pl.BlockSpec((tm, LANES), lambda i: (i, 0))],
        out_shape=[jax.ShapeDtypeStruct((T, 3 * ATTN_W), BF16)]
        + [jax.ShapeDtypeStruct((B, d, S // d, 3 * ATTN_W), BF16) for d in DILATIONS[1:]]
        + [jax.ShapeDtypeStruct((T, DN_CONV), BF16),
           jax.ShapeDtypeStruct((T, DN_W), BF16),
           jax.ShapeDtypeStruct((T, LANES), F32)],
        scratch_shapes=[pltpu.VMEM((ATTN_W // LANES, tm, LANES), F32),
                        pltpu.VMEM((ATTN_W // LANES, tm, LANES), F32)],
        compiler_params=_cparams(("arbitrary",)),
        name="inproj",
    )(x2, mod, norm_w.reshape(1, D), wm, ws)


def _attn_kernel(q_ref, kc_ref, kp_ref, vc_ref, vp_ref, o_ref, lse_ref, kf, vf, *, qb):
    n = pl.program_id(2)
    kf[0:ATTN_BLK, :] = kp_ref[0, 0]
    kf[ATTN_BLK:, :] = kc_ref[0, 0]
    vf[0:ATTN_BLK, :] = vp_ref[0, 0]
    vf[ATTN_BLK:, :] = vc_ref[0, 0]
    row = lax.broadcasted_iota(I32, (ATTN_BLK, 2 * ATTN_BLK), 0)
    col = lax.broadcasted_iota(I32, (ATTN_BLK, 2 * ATTN_BLK), 1)
    band = jnp.logical_or(jnp.logical_and(col < ATTN_BLK, col >= row),
                          jnp.logical_and(col >= ATTN_BLK, col - ATTN_BLK <= row))
    lane = lax.broadcasted_iota(I32, (ATTN_BLK, LANES), 1)
    lo = lane < ATTN_HD

    def sub(j, carry):
        r0 = pl.multiple_of(j * ATTN_BLK, ATTN_BLK)
        first_col = jnp.where(jnp.logical_and(n == 0, j == 0), ATTN_BLK, 0)
        mask = jnp.logical_and(band, col >= first_col)
        npair = ATTN_W // LANES
        cols = [slice(hp * LANES, (hp + 1) * LANES) for hp in range(npair)]
        heads = [(hp, half) for hp in range(npair) for half in range(2)]
        scores = []
        for hp, half in heads:
            q2 = q_ref[0, 0, pl.ds(r0, ATTN_BLK), cols[hp]]
            qm = jnp.where(lo if half == 0 else jnp.logical_not(lo), q2, jnp.zeros_like(q2))
            scores.append(jnp.where(mask, _nt(qm, kf[pl.ds(r0, 2 * ATTN_BLK), cols[hp]]), NEG))
        maxes = [jnp.max(s, axis=-1, keepdims=True) for s in scores]
        probs = [jnp.exp2(s - m) for s, m in zip(scores, maxes)]
        dens = [jnp.sum(p, axis=-1, keepdims=True) for p in probs]
        accs = [jnp.dot(p.astype(BF16), vf[pl.ds(r0, 2 * ATTN_BLK), cols[hp]], preferred_element_type=F32)
                for p, (hp, _) in zip(probs, heads)]
        outs = [a / d for a, d in zip(accs, dens)]
        lse_tile = jnp.zeros((ATTN_BLK, LANES), F32)
        for h, (m, d) in enumerate(zip(maxes, dens)):
            lse_tile = jnp.where(lane == h, m + jnp.log2(d), lse_tile)
        for hp in range(npair):
            o_ref[0, 0, pl.ds(r0, ATTN_BLK), cols[hp]] = jnp.where(lo, outs[2 * hp], outs[2 * hp + 1]).astype(BF16)
        lse_ref[0, 0, pl.ds(r0, ATTN_BLK), :] = lse_tile
        return carry

    lax.fori_loop(0, qb // ATTN_BLK, sub, 0, unroll=True)


def _attn_branch(qkv, d):
    B, _, L, _ = qkv.shape
    qb = min(ATTN_ROWS, L)
    nsub = qb // ATTN_BLK
    cur = lambda c: pl.BlockSpec((1, 1, qb, ATTN_W), lambda b, r, n: (b, r, n, c))
    prev = lambda c: pl.BlockSpec((1, 1, ATTN_BLK, ATTN_W),
                                  lambda b, r, n: (b, r, jnp.maximum(n * nsub - 1, 0), c))
    return pl.pallas_call(
        functools.partial(_attn_kernel, qb=qb),
        grid=(B, d, L // qb),
        in_specs=[cur(0), cur(1), prev(1), cur(2), prev(2)],
        out_specs=[pl.BlockSpec((1, 1, qb, ATTN_W), lambda b, r, n: (b, r, n, 0)),
                   pl.BlockSpec((1, 1, qb, LANES), lambda b, r, n: (b, r, n, 0))],
        out_shape=[jax.ShapeDtypeStruct((B, d, L, ATTN_W), BF16),
                   jax.ShapeDtypeStruct((B, d, L, LANES), F32)],
        scratch_shapes=[pltpu.VMEM((qb + ATTN_BLK, ATTN_W), BF16),
                        pltpu.VMEM((qb + ATTN_BLK, ATTN_W), BF16)],
        compiler_params=_cparams(("arbitrary", "arbitrary", "arbitrary")),
        name=f"attn_d{d}",
    )(qkv, qkv, qkv, qkv, qkv)


def _gdn_kernel(x_ref, z_ref, g_ref, cw_ref, prm_ref, nw_ref, o_ref, xext, yc, s0, s1, *, rb, nseq):
    i = pl.program_id(1)

    @pl.when(i == 0)
    def _():
        xext[:, 0:8, :] = jnp.zeros((nseq, 8, DN_CONV), F32)
        s0[...] = jnp.zeros_like(s0)
        s1[...] = jnp.zeros_like(s1)

    @pl.when(i > 0)
    def _():
        xext[:, 0:8, :] = xext[:, rb:rb + 8, :]

    for q in range(nseq):
        xext[q, 8:, :] = x_ref[q].astype(F32)
        y = cw_ref[CONV_K - 1:CONV_K, :] * xext[q, 8:8 + rb, :]
        for j in range(CONV_K - 1):
            off = 8 - (CONV_K - 1) + j
            y = y + cw_ref[j:j + 1, :] * xext[q, off:off + rb, :]
        yc[q] = y * _sigmoid(y)

    C = DN_CHUNK
    H = DN_HEADS
    CW = H * C
    dot = functools.partial(jnp.dot, preferred_element_type=F32)

    def iota(shape, d):
        return lax.broadcasted_iota(I32, shape, d)

    ltri_b = jnp.where(iota((C, C), 0) >= iota((C, C), 1), 1.0, 0.0).astype(BF16)
    lane = iota((C, LANES), 1)
    blane = lane < H
    glane = jnp.logical_and(lane >= H, lane < 2 * H)
    e512 = jnp.where(jnp.logical_or(iota((LANES, DN_W), 1) // DN_D == iota((LANES, DN_W), 0),
                                    iota((LANES, DN_W), 1) // DN_D == iota((LANES, DN_W), 0) - H),
                     1.0, 0.0).astype(BF16)
    e256 = jnp.where(iota((LANES, CW), 1) // C == iota((LANES, CW), 0) - H, 1.0, 0.0).astype(BF16)
    row4 = iota((C, CW), 0)
    col4 = iota((C, CW), 1) % C
    eye4 = jnp.where(row4 == col4, 1.0, 0.0).astype(F32)
    blk = [iota((C, CW), 1) // C == h for h in range(H)]
    bd_cc = iota((CW, CW), 0) // C == iota((CW, CW), 1) // C
    bd_pair = iota((CW, CW), 0) // DN_D == iota((CW, CW), 1) // DN_D
    rt_mask = iota((CW, DN_W), 0) // C == iota((CW, DN_W), 1) // DN_D
    neg_a = -jnp.exp(prm_ref[0:1, :])
    dtb = prm_ref[1:2, :]
    nw = nw_ref[...]

    def hilo(x):
        hi = x.astype(BF16)
        return hi, (x - hi.astype(F32)).astype(BF16)

    def heads(a, w):
        return [a[:, h * w:(h + 1) * w] for h in range(H)]

    def l2n(a, mult):
        return jnp.concatenate(
            [p * (lax.rsqrt(jnp.sum(p * p, axis=-1, keepdims=True) + EPS) * mult) for p in heads(a, DN_D)],
            axis=1)

    def stack4(a):
        return jnp.concatenate([a, a, a, a], axis=0)

    zb = jnp.zeros((), BF16)
    nchunk = rb // C
    chunks = []
    for c, sq in ((c, sq) for c in range(nchunk) for sq in range(nseq)):
        rs = slice(c * C, (c + 1) * C)
        G = g_ref[sq, rs, :]
        xg = G + dtb
        gv = jnp.where(glane, neg_a * (jnp.maximum(xg, 0.0) + jnp.log1p(jnp.exp(-jnp.abs(xg)))), 0.0)
        be = jnp.where(blane, _sigmoid(G), 0.0)
        g_hi, g_lo = hilo(gv)
        gcum = dot(ltri_b, g_hi) + dot(ltri_b, g_lo)
        gtot = gcum[C - 1:C, :]
        eg = jnp.where(glane, jnp.exp(gcum), 0.0)
        ek = jnp.where(glane, jnp.exp(gtot - gcum), 0.0)
        ex = dot(jnp.concatenate([be, eg, ek], axis=0).astype(BF16), e512)
        bexp, egexp, ekexp = ex[0:C], ex[C:2 * C], ex[2 * C:3 * C]
        gexp = dot(g_hi, e256) + dot(g_lo, e256)
        d_hi, d_lo = hilo(jnp.where(row4 > col4, gexp, 0.0))
        diff = dot(ltri_b, d_hi) + dot(ltri_b, d_lo)
        decay = jnp.exp(jnp.where(row4 >= col4, diff, NEG))

        q4 = l2n(yc[sq, rs, 0:DN_W], DN_D ** -0.5)
        k4 = l2n(yc[sq, rs, DN_W:2 * DN_W], 1.0)
        v4 = yc[sq, rs, 2 * DN_W:3 * DN_W]
        kb4 = k4 * bexp
        vb4 = v4 * bexp
        rt = jnp.where(rt_mask, stack4(k4.astype(BF16)), jnp.zeros((), BF16))
        ai = _nt(jnp.concatenate([kb4, q4], axis=0).astype(BF16), rt)
        a4 = jnp.where(row4 > col4, ai[0:C] * decay, 0.0)
        pb = a4.astype(BF16)
        chunks.append(dict(
            sq=sq, rs=rs, q4=q4, k4=k4, kb4=kb4, vb4=vb4, egexp=egexp, ekexp=ekexp,
            intra=ai[C:2 * C] * decay, t4=eye4 - a4, pb=pb,
            bd=jnp.where(bd_cc, stack4(pb), zb)))

    for _ in range(5):
        for ch in chunks:
            ch["pb"] = dot(ch["pb"], ch["bd"]).astype(BF16)
        for ch in chunks:
            ch["bd"] = jnp.where(bd_cc, stack4(ch["pb"]), zb)
        for ch in chunks:
            ch["t4"] = ch["t4"] + dot(ch["t4"].astype(BF16), ch["bd"])

    for ch in chunks:
        q4, k4, kb4, vb4, egexp, ekexp = (ch[n] for n in ("q4", "k4", "kb4", "vb4", "egexp", "ekexp"))
        t4b = ch["t4"].astype(BF16)
        lstk = jnp.concatenate([jnp.where(blk[h], t4b, zb) for h in range(H)], axis=0)
        kbg4 = kb4 * egexp
        rstk = jnp.concatenate(
            [jnp.concatenate([vb, kbg], axis=1) for vb, kbg in zip(heads(vb4, DN_D), heads(kbg4, DN_D))],
            axis=0).astype(BF16)
        uw = dot(lstk, rstk)
        u4 = jnp.concatenate([uw[h * C:(h + 1) * C, 0:DN_D] for h in range(H)], axis=1)
        w4 = jnp.concatenate([uw[h * C:(h + 1) * C, DN_D:2 * DN_D] for h in range(H)], axis=1)
        ib = ch["intra"].astype(BF16)
        ch.update(
            u4=u4, wq=jnp.concatenate([w4, q4 * egexp], axis=0).astype(BF16),
            kd4=(k4 * ekexp).astype(BF16), gl4=egexp[C - 1:C, :],
            lint=jnp.concatenate([jnp.where(blk[h], ib, zb) for h in range(H)], axis=0))

    for ch in chunks:
        sq, rs, u4, wq, kd4, gl4, lint = (ch[n] for n in ("sq", "rs", "u4", "wq", "kd4", "gl4", "lint"))
        ra = dot(wq[:, 0:CW], s0[sq].astype(BF16))
        rc = dot(wq[:, CW:2 * CW], s1[sq].astype(BF16))
        vn = u4 - jnp.concatenate([ra[0:C], rc[0:C]], axis=1)
        vnb = vn.astype(BF16)
        oi = dot(lint, jnp.concatenate(heads(vnb, DN_D), axis=0))
        o = (jnp.concatenate([ra[C:2 * C], rc[C:2 * C]], axis=1)
             + jnp.concatenate([oi[h * C:(h + 1) * C] for h in range(H)], axis=1))
        tn = (((0,), (0,)), ((), ()))
        s0[sq] = s0[sq] * gl4[:, 0:CW] + jnp.where(
            bd_pair, lax.dot_general(kd4[:, 0:CW], vnb[:, 0:CW], tn, preferred_element_type=F32), 0.0)
        s1[sq] = s1[sq] * gl4[:, CW:2 * CW] + jnp.where(
            bd_pair, lax.dot_general(kd4[:, CW:2 * CW], vnb[:, CW:2 * CW], tn, preferred_element_type=F32), 0.0)

        z = z_ref[sq, rs, :].astype(F32)
        on = jnp.concatenate(
            [p * lax.rsqrt(jnp.mean(p * p, axis=-1, keepdims=True) + EPS) * nw for p in heads(o, DN_D)], axis=1)
        o_ref[sq, rs, :] = (on * (z * _sigmoid(z))).astype(BF16)


def _gdn(dqkv, dz, gbc, conv_w, A_log, dt_bias, dn_norm_w, B, S):
    nseq = 2 if B % 2 == 0 else 1
    rb = GDN_ROWS // nseq
    prm = jnp.zeros((2, LANES), F32)
    prm = prm.at[0, DN_HEADS:2 * DN_HEADS].set(A_log.astype(F32))
    prm = prm.at[1, DN_HEADS:2 * DN_HEADS].set(dt_bias.astype(F32))
    out = pl.pallas_call(
        functools.partial(_gdn_kernel, rb=rb, nseq=nseq),
        grid=(B // nseq, S // rb),
        in_specs=[pl.BlockSpec((nseq, rb, DN_CONV), lambda b, i: (b, i, 0)),
                  pl.BlockSpec((nseq, rb, DN_W), lambda b, i: (b, i, 0)),
                  pl.BlockSpec((nseq, rb, LANES), lambda b, i: (b, i, 0)),
                  pl.BlockSpec((CONV_K, DN_CONV), lambda b, i: (0, 0)),
                  pl.BlockSpec((2, LANES), lambda b, i: (0, 0)),
                  pl.BlockSpec((1, DN_D), lambda b, i: (0, 0))],
        out_specs=pl.BlockSpec((nseq, rb, DN_W), lambda b, i: (b, i, 0)),
        out_shape=jax.ShapeDtypeStruct((B, S, DN_W), BF16),
        scratch_shapes=[pltpu.VMEM((nseq, rb + 8, DN_CONV), F32),
                        pltpu.VMEM((nseq, rb, DN_CONV), F32),
                        pltpu.VMEM((nseq, 2 * DN_D, 2 * DN_D), F32),
                        pltpu.VMEM((nseq, 2 * DN_D, 2 * DN_D), F32)],
        compiler_params=_cparams(("arbitrary", "arbitrary")),
        name="gdn",
    )(dqkv.reshape(B, S, DN_CONV), dz.reshape(B, S, DN_W), gbc.reshape(B, S, LANES),
      conv_w, prm, dn_norm_w.reshape(1, DN_D))
    return out.reshape(B * S, DN_W)


def _out_kernel(o1_ref, o2_ref, o3_ref, l1_ref, l2_ref, l3_ref, dn_ref, x_ref, mod_ref, wo_ref,
                n2_ref, wr_ref, br_ref,
                x1_ref, h2_ref, te_ref, rk_ref, gtc_ref, cnt_ref, base, scr, *, tm):
    i = pl.program_id(0)

    @pl.when(i == 0)
    def _():
        base[...] = jnp.zeros_like(base)

    def natural(ref, d):
        if d == 1:
            return ref[0, 0].astype(F32)
        nl = ref.shape[-1] // LANES
        for res in range(d):
            blk = ref[0, res].astype(F32)
            for c in range(nl):
                scr[c, pl.ds(res, tm // d, stride=d), :] = blk[:, c * LANES:(c + 1) * LANES]
        return jnp.concatenate([scr[c] for c in range(nl)], axis=1)

    l1, l2, l3 = (natural(r, d) for r, d in zip((l1_ref, l2_ref, l3_ref), DILATIONS))
    mx = jnp.maximum(jnp.maximum(l1, l2), l3)
    e1, e2, e3 = jnp.exp2(l1 - mx), jnp.exp2(l2 - mx), jnp.exp2(l3 - mx)
    zs = e1 + e2 + e3
    er = lax.broadcasted_iota(I32, (LANES, ATTN_W), 0)
    ec = lax.broadcasted_iota(I32, (LANES, ATTN_W), 1)
    expand = jnp.where(ec // ATTN_HD == er, 1.0, 0.0).astype(BF16)
    attn = jnp.zeros((tm, ATTN_W), F32)
    for e, o_ref, d in zip((e1, e2, e3), (o1_ref, o2_ref, o3_ref), DILATIONS):
        wgt = jnp.dot((e / zs).astype(BF16), expand, preferred_element_type=F32)
        attn = attn + wgt * natural(o_ref, d)
    mix = (jnp.dot(attn.astype(BF16), wo_ref[0:ATTN_W, :], preferred_element_type=F32)
           + jnp.dot(dn_ref[...], wo_ref[ATTN_W:, :], preferred_element_type=F32))
    x1 = x_ref[...] + mod_ref[0, 2:3, :] * mix
    x1_ref[...] = x1
    ms = jnp.mean(x1 * x1, axis=-1, keepdims=True)
    h2 = x1 * lax.rsqrt(ms + EPS) * n2_ref[...]
    h2 = h2 * (1.0 + mod_ref[0, 4:5, :]) + mod_ref[0, 3:4, :]
    _store_slabs(h2_ref, h2)

    def split(a):
        hi = a.astype(BF16)
        return hi, (a - hi.astype(F32)).astype(BF16)

    h_hi, h_lo = split(h2)
    w_hi, w_lo = split(wr_ref[...])
    lg = _nt(w_hi, h_hi) + (_nt(w_hi, h_lo) + _nt(w_lo, h_hi)) + br_ref[...]
    eidx = lax.broadcasted_iota(I32, (N_EXPERTS, tm), 0)
    vals, idxs, sels = [], [], []
    for _ in range(TOP_K):
        m = jnp.max(lg, axis=0, keepdims=True)
        idx = jnp.min(jnp.where(lg == m, eidx, N_EXPERTS), axis=0, keepdims=True)
        sel = eidx == idx
        vals.append(m)
        idxs.append(idx)
        sels.append(sel)
        lg = jnp.where(sel, -jnp.inf, lg)
    ex = [jnp.exp(v - vals[0]) for v in vals]
    den = ex[0] + ex[1] + ex[2] + ex[3]
    gates = [e / den for e in ex]

    msum = jnp.zeros((N_EXPERTS, tm), F32)
    for sel in sels:
        msum = msum + jnp.where(sel, 1.0, 0.0)
    tr = lax.broadcasted_iota(I32, (tm, tm), 0)
    tc = lax.broadcasted_iota(I32, (tm, tm), 1)
    upper = jnp.where(tr <= tc, 1.0, 0.0).astype(BF16)
    incl = jnp.dot(msum.astype(BF16), upper, preferred_element_type=F32)
    pos = base[:, 0:1] + (incl - msum)
    sub8 = lax.broadcasted_iota(I32, (8, tm), 0)
    te = jnp.zeros((8, tm), I32)
    rk = jnp.zeros((8, tm), I32)
    gt = jnp.zeros((8, tm), F32)
    for k in range(TOP_K):
        rank_k = jnp.sum(jnp.where(sels[k], pos, 0.0), axis=0, keepdims=True).astype(I32)
        te = jnp.where(sub8 == k, idxs[k], te)
        rk = jnp.where(sub8 == k, rank_k, rk)
        gt = jnp.where(sub8 == k, gates[k], gt)
    te_ref[...] = te
    rk_ref[...] = rk
    gtc_ref[...] = jnp.transpose(jnp.concatenate([gt, jnp.zeros((LANES - 8, tm), F32)], axis=0))
    base[...] = base[...] + jnp.sum(msum, axis=1, keepdims=True)
    cnt_ref[...] = base[...].astype(I32)


def _outproj(o1, o2, o3, l1, l2, l3, dn, x2, mod, w_out, norm2_w, w_router, b_router, S):
    T, D = x2.shape
    tm = PROJ_ROWS
    nt = S // tm
    row = lambda w: pl.BlockSpec((tm, w), lambda i: (i, 0))
    res = lambda d, w: pl.BlockSpec((1, d, tm // d, w), lambda i: (i // nt, 0, i % nt, 0))
    colb = pl.BlockSpec((8, tm), lambda i: (0, i))
    return pl.pallas_call(
        functools.partial(_out_kernel, tm=tm),
        grid=(T // tm,),
        in_specs=[res(d, ATTN_W) for d in DILATIONS] + [res(d, LANES) for d in DILATIONS]
        + [row(DN_W), row(D),
                  pl.BlockSpec((1, 6, D), lambda i: (i * tm // S, 0, 0)),
                  pl.BlockSpec((D, D), lambda i: (0, 0)),
                  pl.BlockSpec((1, D), lambda i: (0, 0)),
                  pl.BlockSpec((N_EXPERTS, D), lambda i: (0, 0)),
                  pl.BlockSpec((N_EXPERTS, 1), lambda i: (0, 0))],
        out_specs=[row(D), pl.BlockSpec((tm * (D // LANES), LANES), lambda i: (i, 0)), colb, colb,
                   row(LANES), pl.BlockSpec((N_EXPERTS, LANES), lambda i: (0, 0))],
        out_shape=[jax.ShapeDtypeStruct((T, D), F32),
                   jax.ShapeDtypeStruct((T * (D // LANES), LANES), F32),
                   jax.ShapeDtypeStruct((8, T), I32),
                   jax.ShapeDtypeStruct((8, T), I32),
                   jax.ShapeDtypeStruct((T, LANES), F32),
                   jax.ShapeDtypeStruct((N_EXPERTS, LANES), I32)],
        scratch_shapes=[pltpu.VMEM((N_EXPERTS, LANES), F32),
                        pltpu.VMEM((ATTN_W // LANES, tm, LANES), F32)],
        compiler_params=_cparams(("arbitrary",)),
        name="outproj_router",
    )(o1, o2, o3, l1, l2, l3, dn, x2, mod, w_out.astype(BF16), norm2_w.reshape(1, D),
      jnp.transpose(w_router), b_router.reshape(N_EXPERTS, 1))


def _dest_kernel(ps_ref, te_ref, rk_ref, d_ref):
    te = te_ref[...]
    acc = jnp.zeros(te.shape, I32)
    for e in range(N_EXPERTS):
        acc = jnp.where(te == e, ps_ref[e], acc)
    d_ref[...] = acc + rk_ref[...]


def _dest(pstart, te, rk):
    T = te.shape[1]
    tb = DEST_COLS
    return pl.pallas_call(
        _dest_kernel,
        grid_spec=pltpu.PrefetchScalarGridSpec(
            num_scalar_prefetch=1,
            grid=(T // tb,),
            in_specs=[pl.BlockSpec((8, tb), lambda i, ps: (0, i)),
                      pl.BlockSpec((8, tb), lambda i, ps: (0, i))],
            out_specs=pl.BlockSpec((8, tb), lambda i, ps: (0, i))),
        out_shape=jax.ShapeDtypeStruct((8, T), I32),
        compiler_params=_cparams(("arbitrary",)),
        name="dest_rows",
    )(pstart, te, rk)


SC_CORES = 2
SC_SUBCORES = 16
SC_IDX_CHUNK = 128


def _invperm(dest_flat, P):
    N = dest_flat.shape[0]
    nch = N // (SC_SUBCORES * SC_IDX_CHUNK)
    half = P // SC_CORES
    per_out = half // SC_SUBCORES
    assert N % (SC_SUBCORES * SC_IDX_CHUNK) == 0 and P % (SC_CORES * SC_SUBCORES * 8) == 0
    mesh = plsc.VectorSubcoreMesh(core_axis_name="c", subcore_axis_name="s",
                                  num_cores=SC_CORES, num_subcores=SC_SUBCORES)

    @functools.partial(
        pl.kernel, mesh=mesh, out_type=jax.ShapeDtypeStruct((P,), I32),
        scratch_types=[pltpu.VMEM((nch, SC_IDX_CHUNK), I32), pltpu.VMEM((nch, SC_IDX_CHUNK), I32),
                       pltpu.VMEM_SHARED((P,), I32), pltpu.VMEM((per_out,), I32), pltpu.SemaphoreType.DMA])
    def scatter_codes(idx_hbm, val_hbm, out_hbm, idx_v, val_v, table, stage, sem):
        core = lax.axis_index("c")
        sub = lax.axis_index("s")
        pltpu.sync_copy(idx_hbm.at[sub], idx_v)
        pltpu.sync_copy(val_hbm.at[sub], val_v)

        @pl.loop(0, nch)
        def _(j):
            pltpu.async_copy(val_v.at[j], table.at[idx_v.at[j]], sem)

        @pl.loop(0, nch)
        def _(j):
            pltpu.make_async_copy(val_v.at[j], table.at[idx_v.at[j]], sem).wait()

        plsc.subcore_barrier()
        off = pl.multiple_of(core * half + sub * per_out, 8)
        pltpu.sync_copy(table.at[pl.ds(off, per_out)], stage)
        pltpu.sync_copy(stage, out_hbm.at[pl.ds(off, per_out)])

    vals = jnp.arange(N, dtype=I32)
    shape = (SC_SUBCORES, nch, SC_IDX_CHUNK)
    return scatter_codes(dest_flat.reshape(shape), vals.reshape(shape))


def _row_copy(src, dst, sem):
    return pltpu.make_async_copy(src, dst, sem)


def _moe_kernel(be_ref, nv_ref, cc_ref, cn_ref, h2_ref, w1_ref, b1_ref, w2_ref, b2_ref, y4_ref,
                xbuf, ybuf, w1b, w2b, gsem, ssem, *, F, T, D, tme, nb):
    i = pl.program_id(0)
    s = i % 2
    ns = D // LANES

    def rows(first, n):
        return pl.ds(pl.multiple_of(first * ns, ns), n * ns)

    def gather_copy(tok, p, slot):
        return _row_copy(h2_ref.at[rows(tok, 1)], xbuf.at[slot, rows(p, 1)], gsem.at[slot])

    def token_of(code):
        return code & (T - 1) if T & (T - 1) == 0 else code % T

    def issue_rows(start_row, nv):
        ng = nv // ROW_UNROLL

        def grp(g, c):
            for j in range(ROW_UNROLL):
                start_row(g * ROW_UNROLL + j, j % 2)
            return c
        lax.fori_loop(0, ng, grp, 0)

        def one(p, c):
            start_row(p, 0)
            return c
        lax.fori_loop(ng * ROW_UNROLL, nv, one, 0)

    def start_gather(code_ref, nv, slot):
        issue_rows(lambda p, pri: gather_copy(token_of(code_ref[0, 0, p]), p, slot).start(priority=pri), nv)

    def wait_rows(make, nv):
        @pl.when(nv > 0)
        def _():
            make(nv).wait()

    def gathered(n, slot):
        return _row_copy(h2_ref.at[rows(0, n)], xbuf.at[slot, rows(0, n)], gsem.at[slot])

    def scattered(n, slot):
        return _row_copy(ybuf.at[slot, rows(0, n)], y4_ref.at[rows(0, n)], ssem.at[slot])

    @pl.when(i == 0)
    def _():
        start_gather(cc_ref, tme, 0)

    nv = nv_ref[i]
    gathered(tme, s).wait()

    @pl.when(i >= 2)
    def _():
        wait_rows(lambda n: scattered(n, s), nv_ref[i - 2])

    @pl.when(jnp.logical_or(i == 0, be_ref[i] != be_ref[jnp.maximum(i - 1, 0)]))
    def _():
        w1b[...] = w1_ref[0].astype(BF16)
        w2b[...] = w2_ref[0].astype(BF16)

    def compute(slot):
        x = _load_slabs(xbuf, tme, D, lead=(slot,)).astype(BF16)
        for p in range(tme):
            gather_copy(token_of(cn_ref[0, 0, p]), p, 1 - slot).start(priority=p % 2)
        hgu = jnp.dot(x, w1b[...], preferred_element_type=F32) + b1_ref[0]
        gate = jnp.minimum(hgu[:, :F], SWIGLU_LIMIT)
        up = jnp.clip(hgu[:, F:], -SWIGLU_LIMIT, SWIGLU_LIMIT)
        act = gate * _sigmoid(SWIGLU_ALPHA * gate) * (up + 1.0)
        y = jnp.dot(act.astype(BF16), w2b[...], preferred_element_type=F32) + b2_ref[0]
        _store_slabs(ybuf, y, lead=(slot,))

        issue_rows(lambda p, pri: _row_copy(ybuf.at[slot, rows(p, 1)], y4_ref.at[rows(cc_ref[0, 0, p], 1)],
                                            ssem.at[slot]).start(priority=pri), nv)

    for slot in range(2):
        pl.when(jnp.logical_and(nv > 0, s == slot))(functools.partial(compute, slot))

    @pl.when(jnp.logical_and(nv == 0, i + 1 < nb))
    def _():
        start_gather(cn_ref, tme, 1 - s)

    @pl.when(i == nb - 1)
    def _():
        @pl.when(nv > 0)
        def _():
            gathered(tme, 1 - s).wait()
        wait_rows(lambda n: scattered(n, s), nv)
        if nb > 1:
            wait_rows(lambda n: scattered(n, 1 - s), nv_ref[i - 1])


def _experts(blk_exp, blk_valid, codes, h2s, w1, b1, w2, b2, tme):
    E, D, F2 = w1.shape
    F = F2 // 2
    ns = D // LANES
    T = h2s.shape[0] // ns
    nb = blk_exp.shape[0]
    codes3 = codes.reshape(nb, 1, tme)
    wspec = lambda shape: pl.BlockSpec(shape, lambda i, be, nv: (be[i], 0, 0))
    cspec = lambda off: pl.BlockSpec((1, 1, tme), lambda i, be, nv: (jnp.minimum(i + off, nb - 1), 0, 0),
                                     memory_space=pltpu.SMEM)
    return pl.pallas_call(
        functools.partial(_moe_kernel, F=F, T=T, D=D, tme=tme, nb=nb),
        grid_spec=pltpu.PrefetchScalarGridSpec(
            num_scalar_prefetch=2,
            grid=(nb,),
            in_specs=[cspec(0), cspec(1),
                      pl.BlockSpec(memory_space=pl.ANY),
                      wspec((1, D, F2)), wspec((1, 1, F2)), wspec((1, F, D)), wspec((1, 1, D))],
            out_specs=pl.BlockSpec(memory_space=pl.ANY),
            scratch_shapes=[pltpu.VMEM((2, tme * ns, LANES), F32), pltpu.VMEM((2, tme * ns, LANES), F32),
                            pltpu.VMEM((D, F2), BF16), pltpu.VMEM((F, D), BF16),
                            pltpu.SemaphoreType.DMA((2,)), pltpu.SemaphoreType.DMA((2,))]),
        out_shape=jax.ShapeDtypeStruct((TOP_K * T * ns, LANES), F32),
        compiler_params=_cparams(("arbitrary",)),
        name="experts",
    )(blk_exp, blk_valid, codes3, codes3, h2s, w1, b1.reshape(E, 1, F2), w2, b2.reshape(E, 1, D))


def _comb_kernel(g_ref, x1_ref, mod_ref, fw_ref, y0_ref, y1_ref, y2_ref, y3_ref, o_ref):
    g = g_ref[...]
    n, d = x1_ref.shape
    y = g[:, 0:1] * _load_slabs(y0_ref, n, d)
    for k, y_ref in ((1, y1_ref), (2, y2_ref), (3, y3_ref)):
        y = y + g[:, k:k + 1] * _load_slabs(y_ref, n, d)
    x2 = x1_ref[...] + mod_ref[0, 5:6, :] * y
    ms = jnp.mean(x2 * x2, axis=-1, keepdims=True)
    o_ref[...] = x2 * lax.rsqrt(ms + EPS) * fw_ref[...]


def _combine(gtc, x1, mod, final_w, y4, S):
    T, D = x1.shape
    tmc = COMBINE_ROWS
    nt = T // tmc
    yspec = lambda k: pl.BlockSpec((tmc * (D // LANES), LANES), lambda i: (k * nt + i, 0))
    return pl.pallas_call(
        _comb_kernel,
        grid=(nt,),
        in_specs=[pl.BlockSpec((tmc, LANES), lambda i: (i, 0)),
                  pl.BlockSpec((tmc, D), lambda i: (i, 0)),
                  pl.BlockSpec((1, 6, D), lambda i: (i * tmc // S, 0, 0)),
                  pl.BlockSpec((1, D), lambda i: (0, 0)),
                  yspec(0), yspec(1), yspec(2), yspec(3)],
        out_specs=pl.BlockSpec((tmc, D), lambda i: (i, 0)),
        out_shape=jax.ShapeDtypeStruct((T, D), F32),
        compiler_params=_cparams(("arbitrary",)),
        name="combine",
    )(gtc, x1, mod, final_w.reshape(1, D), y4, y4, y4, y4)


def _layer(x2, mod, B, S, norm1_w, w_in, conv_w, A_log, dt_bias, dn_norm_w, w_out,
           norm2_w, w_router, b_router, w1, b1, w2, b2):
    T, D = x2.shape
    qkv, qkv4, qkv16, dqkv, dz, gbc = _inproj(x2, mod, norm1_w, w_in, S)
    views = (qkv.reshape(B, 1, S, 3 * ATTN_W), qkv4, qkv16)
    branches = [_attn_branch(v, d) for v, d in zip(views, DILATIONS)]
    dn = _gdn(dqkv, dz, gbc, conv_w, A_log, dt_bias, dn_norm_w, B, S)
    (o1, l1), (o2, l2), (o3, l3) = branches
    x1, h2, te, rk, gtc, cnt = _outproj(o1, o2, o3, l1, l2, l3, dn, x2, mod, w_out, norm2_w,
                                        w_router, b_router, S)
    tme = MOE_ROWS
    P = T * TOP_K + N_EXPERTS * tme
    counts = cnt[:, 0]
    padded = (counts + tme - 1) // tme * tme
    pend = jnp.cumsum(padded)
    pstart = (pend - padded).astype(I32)
    blk_start = jnp.arange(P // tme, dtype=I32) * tme
    blk_exp = jnp.minimum(jnp.sum((pend[None, :] <= blk_start[:, None]).astype(I32), axis=1),
                          N_EXPERTS - 1).astype(I32)
    blk_valid = jnp.clip(pstart[blk_exp] + counts[blk_exp] - blk_start, 0, tme).astype(I32)
    dest = _dest(pstart, te, rk)
    codes = _invperm(dest[:TOP_K].reshape(TOP_K * T), P)
    y4 = _experts(blk_exp, blk_valid, codes, h2, w1, b1, w2, b2, tme)
    return x1, gtc, y4


def kernel(x, c, w_ada, b_ada, norm1_w, w_in, conv_w, A_log, dt_bias, dn_norm_w, w_out, norm2_w,
           w_router, b_router, w1, b1, w2, b2, final_norm_w):
    B, S, D = x.shape
    depth = w_ada.shape[0]
    assert depth == 1, "one layer: the final norm is fused into the combine step"
    assert S % (ATTN_BLK * DILATIONS[-1]) == 0 and S % max(PROJ_ROWS, GDN_ROWS, COMBINE_ROWS) == 0
    assert (B * S) % DEST_COLS == 0 and w_ada.shape[2] % ADA_COLS == 0
    x2 = x.reshape(B * S, D)
    mod = _ada(c, w_ada[0], b_ada[0])
    x1, gtc, y4 = _layer(x2, mod, B, S, norm1_w[0], w_in[0], conv_w[0], A_log[0], dt_bias[0],
                         dn_norm_w[0], w_out[0], norm2_w[0], w_router[0], b_router[0],
                         w1[0], b1[0], w2[0], b2[0])
    out = _combine(gtc, x1, mod, final_norm_w, y4, S)
    return out.reshape(B, S, D)
```

```python
import functools

import jax
import jax.numpy as jnp
from jax import lax
from jax.experimental import pallas as pl
from jax.experimental.pallas import tpu as pltpu
from jax.experimental.pallas import tpu_sc as plsc

F32 = jnp.float32
BF16 = jnp.bfloat16
I32 = jnp.int32
HI = lax.Precision.HIGHEST

LANES = 128
ATTN_HEADS = 8
ATTN_HD = 64
ATTN_W = ATTN_HEADS * ATTN_HD
ATTN_BLK = 128
DILATIONS = (1, 4, 16)
DN_HEADS = 4
DN_D = 128
DN_W = DN_HEADS * DN_D
DN_CONV = 3 * DN_W
CONV_K = 4
DN_CHUNK = 64
N_EXPERTS = 32
TOP_K = 4
MOE_ROWS = 512
ROW_UNROLL = 8
SWIGLU_LIMIT = 7.0
SWIGLU_ALPHA = 1.702
EPS = 1e-6
NEG = -1e30
LOG2E = 1.4426950408889634
MAIN_COLS = 3 * ATTN_W + DN_CONV + DN_W

ADA_COLS = 1024
PROJ_ROWS = 1024
ATTN_ROWS = 1024
GDN_ROWS = 1024
DEST_COLS = 2048
COMBINE_ROWS = 1024

V7X_VMEM_BYTES = 64 * 1024 * 1024
VMEM_LIMIT = V7X_VMEM_BYTES - 8 * 1024 * 1024


def _cparams(sem):
    return pltpu.CompilerParams(dimension_semantics=sem, vmem_limit_bytes=VMEM_LIMIT)


def _nt(a, b, **kw):
    return lax.dot_general(a, b, (((1,), (1,)), ((), ())), preferred_element_type=F32, **kw)


def _sigmoid(x):
    return 1.0 / (1.0 + jnp.exp(-x))


def _store_slabs(ref, val, lead=()):
    n, d = val.shape
    ns = d // LANES
    for c in range(ns):
        ref[lead + (pl.ds(c, n, stride=ns), slice(None))] = val[:, c * LANES:(c + 1) * LANES]


def _load_slabs(ref, n, d, lead=()):
    ns = d // LANES
    return jnp.concatenate([ref[lead + (pl.ds(c, n, stride=ns), slice(None))] for c in range(ns)], axis=1)


def _ada_kernel(c_ref, w_ref, b_ref, o_ref):
    c = c_ref[...]
    cond = c * _sigmoid(c)
    o_ref[...] = jnp.dot(cond, w_ref[...], preferred_element_type=F32, precision=HI) + b_ref[...]


def _ada(c, w_ada, b_ada):
    B, D = c.shape
    N = w_ada.shape[1]
    cp = jnp.zeros((8, D), F32).at[:B].set(c)
    tn = ADA_COLS
    out = pl.pallas_call(
        _ada_kernel,
        grid=(N // tn,),
        in_specs=[pl.BlockSpec((8, D), lambda j: (0, 0)),
                  pl.BlockSpec((D, tn), lambda j: (0, j)),
                  pl.BlockSpec((1, tn), lambda j: (0, j))],
        out_specs=pl.BlockSpec((8, tn), lambda j: (0, j)),
        out_shape=jax.ShapeDtypeStruct((8, N), F32),
        compiler_params=_cparams(("arbitrary",)),
        name="ada",
    )(cp, w_ada, b_ada.reshape(1, N))
    return out[:B].reshape(B, 6, D)


def _inproj_kernel(x_ref, mod_ref, nw_ref, wm_ref, ws_ref, qkv_ref, qkv4_ref, qkv16_ref, dqkv_ref, dz_ref,
                   gbc_ref, scr, scr2):
    x = x_ref[...]
    tm = x.shape[0]
    shift = mod_ref[0, 0:1, :]
    scale = mod_ref[0, 1:2, :]
    ms = jnp.mean(x * x, axis=-1, keepdims=True)
    h = x * lax.rsqrt(ms + EPS) * nw_ref[...]
    hb = (h * (1.0 + scale) + shift).astype(BF16)
    nl = ATTN_W // LANES
    proj = [jnp.dot(hb, wm_ref[:, c0:c0 + ATTN_W], preferred_element_type=F32)
            for c0 in range(0, MAIN_COLS, ATTN_W)]
    for j in range(3):
        cs = slice(j * ATTN_W, (j + 1) * ATTN_W)
        r = proj[j]
        if j == 0:
            r = r * (ATTN_HD ** -0.5 * LOG2E)
        qkv_ref[:, cs] = r.astype(BF16)
        d1, d2 = DILATIONS[1], DILATIONS[2]
        step = d2 // d1
        for c in range(nl):
            scr[c] = r[:, c * LANES:(c + 1) * LANES]
        for res in range(d1):
            for c in range(nl):
                scr2[c, res * (tm // d1):(res + 1) * (tm // d1), :] = scr[c, pl.ds(res, tm // d1, stride=d1), :]
            part = jnp.concatenate([scr2[c, res * (tm // d1):(res + 1) * (tm // d1), :] for c in range(nl)],
                                   axis=1)
            qkv4_ref[0, res, :, cs] = part.astype(BF16)
        for res in range(d2):
            r1, q = res % d1, res // d1
            part = jnp.concatenate(
                [scr2[c, pl.ds(r1 * (tm // d1) + q, tm // d2, stride=step), :] for c in range(nl)], axis=1)
            qkv16_ref[0, res, :, cs] = part.astype(BF16)
    for j in range(3):
        dqkv_ref[:, j * DN_W:(j + 1) * DN_W] = proj[3 + j].astype(BF16)
    dz_ref[...] = proj[6].astype(BF16)
    gbc_ref[...] = jnp.dot(hb, ws_ref[...], preferred_element_type=F32)


def _inproj(x2, mod, norm_w, w_in, S):
    T, D = x2.shape
    tm = PROJ_ROWS
    B, nt = T // S, S // tm
    wm = w_in[:, :MAIN_COLS].astype(BF16)
    ws = jnp.zeros((D, LANES), F32).at[:, :2 * DN_HEADS].set(w_in[:, MAIN_COLS:]).astype(BF16)
    return pl.pallas_call(
        _inproj_kernel,
        grid=(T // tm,),
        in_specs=[pl.BlockSpec((tm, D), lambda i: (i, 0)),
                  pl.BlockSpec((1, 6, D), lambda i: (i * tm // S, 0, 0)),
                  pl.BlockSpec((1, D), lambda i: (0, 0)),
                  pl.BlockSpec((D, MAIN_COLS), lambda i: (0, 0)),
                  pl.BlockSpec((D, LANES), lambda i: (0, 0))],
        out_specs=[pl.BlockSpec((tm, 3 * ATTN_W), lambda i: (i, 0))]
        + [pl.BlockSpec((1, d, tm // d, 3 * ATTN_W), lambda i: (i // nt, 0, i % nt, 0)) for d in DILATIONS[1:]]
        + [pl.BlockSpec((tm, DN_CONV), lambda i: (i, 0)),
           pl.BlockSpec((tm, DN_W), lambda i: (i, 0)),
           pl.BlockSpec((tm, LANES), lambda i: (i, 0))],
        out_shape=[jax.ShapeDtypeStruct((T, 3 * ATTN_W), BF16)]
        + [jax.ShapeDtypeStruct((B, d, S // d, 3 * ATTN_W), BF16) for d in DILATIONS[1:]]
        + [jax.ShapeDtypeStruct((T, DN_CONV), BF16),
           jax.ShapeDtypeStruct((T, DN_W), BF16),
           jax.ShapeDtypeStruct((T, LANES), F32)],
        scratch_shapes=[pltpu.VMEM((ATTN_W // LANES, tm, LANES), F32),
                        pltpu.VMEM((ATTN_W // LANES, tm, LANES), F32)],
        compiler_params=_cparams(("arbitrary",)),
        name="inproj",
    )(x2, mod, norm_w.reshape(1, D), wm, ws)


def _attn_kernel(q_ref, kc_ref, kp_ref, vc_ref, vp_ref, o_ref, lse_ref, kf, vf, *, qb):
    n = pl.program_id(2)
    kf[0:ATTN_BLK, :] = kp_ref[0, 0]
    kf[ATTN_BLK:, :] = kc_ref[0, 0]
    vf[0:ATTN_BLK, :] = vp_ref[0, 0]
    vf[ATTN_BLK:, :] = vc_ref[0, 0]
    row = lax.broadcasted_iota(I32, (ATTN_BLK, 2 * ATTN_BLK), 0)
    col = lax.broadcasted_iota(I32, (ATTN_BLK, 2 * ATTN_BLK), 1)
    band = jnp.logical_or(jnp.logical_and(col < ATTN_BLK, col >= row),
                          jnp.logical_and(col >= ATTN_BLK, col - ATTN_BLK <= row))
    lane = lax.broadcasted_iota(I32, (ATTN_BLK, LANES), 1)
    lo = lane < ATTN_HD

    def sub(j, carry):
        r0 = pl.multiple_of(j * ATTN_BLK, ATTN_BLK)
        first_col = jnp.where(jnp.logical_and(n == 0, j == 0), ATTN_BLK, 0)
        mask = jnp.logical_and(band, col >= first_col)
        npair = ATTN_W // LANES
        cols = [slice(hp * LANES, (hp + 1) * LANES) for hp in range(npair)]
        heads = [(hp, half) for hp in range(npair) for half in range(2)]
        scores = []
        for hp, half in heads:
            q2 = q_ref[0, 0, pl.ds(r0, ATTN_BLK), cols[hp]]
            qm = jnp.where(lo if half == 0 else jnp.logical_not(lo), q2, jnp.zeros_like(q2))
            scores.append(jnp.where(mask, _nt(qm, kf[pl.ds(r0, 2 * ATTN_BLK), cols[hp]]), NEG))
        maxes = [jnp.max(s, axis=-1, keepdims=True) for s in scores]
        probs = [jnp.exp2(s - m) for s, m in zip(scores, maxes)]
        dens = [jnp.sum(p, axis=-1, keepdims=True) for p in probs]
        accs = [jnp.dot(p.astype(BF16), vf[pl.ds(r0, 2 * ATTN_BLK), cols[hp]], preferred_element_type=F32)
                for p, (hp, _) in zip(probs, heads)]
        outs = [a / d for a, d in zip(accs, dens)]
        lse_tile = jnp.zeros((ATTN_BLK, LANES), F32)
        for h, (m, d) in enumerate(zip(maxes, dens)):
            lse_tile = jnp.where(lane == h, m + jnp.log2(d), lse_tile)
        for hp in range(npair):
            o_ref[0, 0, pl.ds(r0, ATTN_BLK), cols[hp]] = jnp.where(lo, outs[2 * hp], outs[2 * hp + 1]).astype(BF16)
        lse_ref[0, 0, pl.ds(r0, ATTN_BLK), :] = lse_tile
        return carry

    lax.fori_loop(0, qb // ATTN_BLK, sub, 0, unroll=True)


def _attn_branch(qkv, d):
    B, _, L, _ = qkv.shape
    qb = min(ATTN_ROWS, L)
    nsub = qb // ATTN_BLK
    cur = lambda c: pl.BlockSpec((1, 1, qb, ATTN_W), lambda b, r, n: (b, r, n, c))
    prev = lambda c: pl.BlockSpec((1, 1, ATTN_BLK, ATTN_W),
                                  lambda b, r, n: (b, r, jnp.maximum(n * nsub - 1, 0), c))
    return pl.pallas_call(
        functools.partial(_attn_kernel, qb=qb),
        grid=(B, d, L // qb),
        in_specs=[cur(0), cur(1), prev(1), cur(2), prev(2)],
        out_specs=[pl.BlockSpec((1, 1, qb, ATTN_W), lambda b, r, n: (b, r, n, 0)),
                   pl.BlockSpec((1, 1, qb, LANES), lambda b, r, n: (b, r, n, 0))],
        out_shape=[jax.ShapeDtypeStruct((B, d, L, ATTN_W), BF16),
                   jax.ShapeDtypeStruct((B, d, L, LANES), F32)],
        scratch_shapes=[pltpu.VMEM((qb + ATTN_BLK, ATTN_W), BF16),
                        pltpu.VMEM((qb + ATTN_BLK, ATTN_W), BF16)],
        compiler_params=_cparams(("arbitrary", "arbitrary", "arbitrary")),
        name=f"attn_d{d}",
    )(qkv, qkv, qkv, qkv, qkv)


def _gdn_kernel(x_ref, z_ref, g_ref, cw_ref, prm_ref, nw_ref, o_ref, xext, yc, s0, s1, *, rb, nseq):
    i = pl.program_id(1)

    @pl.when(i == 0)
    def _():
        xext[:, 0:8, :] = jnp.zeros((nseq, 8, DN_CONV), F32)
        s0[...] = jnp.zeros_like(s0)
        s1[...] = jnp.zeros_like(s1)

    @pl.when(i > 0)
    def _():
        xext[:, 0:8, :] = xext[:, rb:rb + 8, :]

    for q in range(nseq):
        xext[q, 8:, :] = x_ref[q].astype(F32)
        y = cw_ref[CONV_K - 1:CONV_K, :] * xext[q, 8:8 + rb, :]
        for j in range(CONV_K - 1):
            off = 8 - (CONV_K - 1) + j
            y = y + cw_ref[j:j + 1, :] * xext[q, off:off + rb, :]
        yc[q] = y * _sigmoid(y)

    C = DN_CHUNK
    H = DN_HEADS
    CW = H * C
    dot = functools.partial(jnp.dot, preferred_element_type=F32)

    def iota(shape, d):
        return lax.broadcasted_iota(I32, shape, d)

    ltri_b = jnp.where(iota((C, C), 0) >= iota((C, C), 1), 1.0, 0.0).astype(BF16)
    lane = iota((C, LANES), 1)
    blane = lane < H
    glane = jnp.logical_and(lane >= H, lane < 2 * H)
    e512 = jnp.where(jnp.logical_or(iota((LANES, DN_W), 1) // DN_D == iota((LANES, DN_W), 0),
                                    iota((LANES, DN_W), 1) // DN_D == iota((LANES, DN_W), 0) - H),
                     1.0, 0.0).astype(BF16)
    e256 = jnp.where(iota((LANES, CW), 1) // C == iota((LANES, CW), 0) - H, 1.0, 0.0).astype(BF16)
    row4 = iota((C, CW), 0)
    col4 = iota((C, CW), 1) % C
    eye4 = jnp.where(row4 == col4, 1.0, 0.0).astype(F32)
    blk = [iota((C, CW), 1) // C == h for h in range(H)]
    bd_cc = iota((CW, CW), 0) // C == iota((CW, CW), 1) // C
    bd_pair = iota((CW, CW), 0) // DN_D == iota((CW, CW), 1) // DN_D
    rt_mask = iota((CW, DN_W), 0) // C == iota((CW, DN_W), 1) // DN_D
    neg_a = -jnp.exp(prm_ref[0:1, :])
    dtb = prm_ref[1:2, :]
    nw = nw_ref[...]

    def hilo(x):
        hi = x.astype(BF16)
        return hi, (x - hi.astype(F32)).astype(BF16)

    def heads(a, w):
        return [a[:, h * w:(h + 1) * w] for h in range(H)]

    def l2n(a, mult):
        return jnp.concatenate(
            [p * (lax.rsqrt(jnp.sum(p * p, axis=-1, keepdims=True) + EPS) * mult) for p in heads(a, DN_D)],
            axis=1)

    def stack4(a):
        return jnp.concatenate([a, a, a, a], axis=0)

    zb = jnp.zeros((), BF16)
    nchunk = rb // C
    chunks = []
    for c, sq in ((c, sq) for c in range(nchunk) for sq in range(nseq)):
        rs = slice(c * C, (c + 1) * C)
        G = g_ref[sq, rs, :]
        xg = G + dtb
        gv = jnp.where(glane, neg_a * (jnp.maximum(xg, 0.0) + jnp.log1p(jnp.exp(-jnp.abs(xg)))), 0.0)
        be = jnp.where(blane, _sigmoid(G), 0.0)
        g_hi, g_lo = hilo(gv)
        gcum = dot(ltri_b, g_hi) + dot(ltri_b, g_lo)
        gtot = gcum[C - 1:C, :]
        eg = jnp.where(glane, jnp.exp(gcum), 0.0)
        ek = jnp.where(glane, jnp.exp(gtot - gcum), 0.0)
        ex = dot(jnp.concatenate([be, eg, ek], axis=0).astype(BF16), e512)
        bexp, egexp, ekexp = ex[0:C], ex[C:2 * C], ex[2 * C:3 * C]
        gexp = dot(g_hi, e256) + dot(g_lo, e256)
        d_hi, d_lo = hilo(jnp.where(row4 > col4, gexp, 0.0))
        diff = dot(ltri_b, d_hi) + dot(ltri_b, d_lo)
        decay = jnp.exp(jnp.where(row4 >= col4, diff, NEG))

        q4 = l2n(yc[sq, rs, 0:DN_W], DN_D ** -0.5)
        k4 = l2n(yc[sq, rs, DN_W:2 * DN_W], 1.0)
        v4 = yc[sq, rs, 2 * DN_W:3 * DN_W]
        kb4 = k4 * bexp
        vb4 = v4 * bexp
        rt = jnp.where(rt_mask, stack4(k4.astype(BF16)), jnp.zeros((), BF16))
        ai = _nt(jnp.concatenate([kb4, q4], axis=0).astype(BF16), rt)
        a4 = jnp.where(row4 > col4, ai[0:C] * decay, 0.0)
        pb = a4.astype(BF16)
        chunks.append(dict(
            sq=sq, rs=rs, q4=q4, k4=k4, kb4=kb4, vb4=vb4, egexp=egexp, ekexp=ekexp,
            intra=ai[C:2 * C] * decay, t4=eye4 - a4, pb=pb,
            bd=jnp.where(bd_cc, stack4(pb), zb)))

    for _ in range(5):
        for ch in chunks:
            ch["pb"] = dot(ch["pb"], ch["bd"]).astype(BF16)
        for ch in chunks:
            ch["bd"] = jnp.where(bd_cc, stack4(ch["pb"]), zb)
        for ch in chunks:
            ch["t4"] = ch["t4"] + dot(ch["t4"].astype(BF16), ch["bd"])

    for ch in chunks:
        q4, k4, kb4, vb4, egexp, ekexp = (ch[n] for n in ("q4", "k4", "kb4", "vb4", "egexp", "ekexp"))
        t4b = ch["t4"].astype(BF16)
        lstk = jnp.concatenate([jnp.where(blk[h], t4b, zb) for h in range(H)], axis=0)
        kbg4 = kb4 * egexp
        rstk = jnp.concatenate(
            [jnp.concatenate([vb, kbg], axis=1) for vb, kbg in zip(heads(vb4, DN_D), heads(kbg4, DN_D))],
            axis=0).astype(BF16)
        uw = dot(lstk, rstk)
        u4 = jnp.concatenate([uw[h * C:(h + 1) * C, 0:DN_D] for h in range(H)], axis=1)
        w4 = jnp.concatenate([uw[h * C:(h + 1) * C, DN_D:2 * DN_D] for h in range(H)], axis=1)
        ib = ch["intra"].astype(BF16)
        ch.update(
            u4=u4, wq=jnp.concatenate([w4, q4 * egexp], axis=0).astype(BF16),
            kd4=(k4 * ekexp).astype(BF16), gl4=egexp[C - 1:C, :],
            lint=jnp.concatenate([jnp.where(blk[h], ib, zb) for h in range(H)], axis=0))

    for ch in chunks:
        sq, rs, u4, wq, kd4, gl4, lint = (ch[n] for n in ("sq", "rs", "u4", "wq", "kd4", "gl4", "lint"))
        ra = dot(wq[:, 0:CW], s0[sq].astype(BF16))
        rc = dot(wq[:, CW:2 * CW], s1[sq].astype(BF16))
        vn = u4 - jnp.concatenate([ra[0:C], rc[0:C]], axis=1)
        vnb = vn.astype(BF16)
        oi = dot(lint, jnp.concatenate(heads(vnb, DN_D), axis=0))
        o = (jnp.concatenate([ra[C:2 * C], rc[C:2 * C]], axis=1)
             + jnp.concatenate([oi[h * C:(h + 1) * C] for h in range(H)], axis=1))
        tn = (((0,), (0,)), ((), ()))
        s0[sq] = s0[sq] * gl4[:, 0:CW] + jnp.where(
            bd_pair, lax.dot_general(kd4[:, 0:CW], vnb[:, 0:CW], tn, preferred_element_type=F32), 0.0)
        s1[sq] = s1[sq] * gl4[:, CW:2 * CW] + jnp.where(
            bd_pair, lax.dot_general(kd4[:, CW:2 * CW], vnb[:, CW:2 * CW], tn, preferred_element_type=F32), 0.0)

        z = z_ref[sq, rs, :].astype(F32)
        on = jnp.concatenate(
            [p * lax.rsqrt(jnp.mean(p * p, axis=-1, keepdims=True) + EPS) * nw for p in heads(o, DN_D)], axis=1)
        o_ref[sq, rs, :] = (on * (z * _sigmoid(z))).astype(BF16)


def _gdn(dqkv, dz, gbc, conv_w, A_log, dt_bias, dn_norm_w, B, S):
    nseq = 2 if B % 2 == 0 else 1
    rb = GDN_ROWS // nseq
    prm = jnp.zeros((2, LANES), F32)
    prm = prm.at[0, DN_HEADS:2 * DN_HEADS].set(A_log.astype(F32))
    prm = prm.at[1, DN_HEADS:2 * DN_HEADS].set(dt_bias.astype(F32))
    out = pl.pallas_call(
        functools.partial(_gdn_kernel, rb=rb, nseq=nseq),
        grid=(B // nseq, S // rb),
        in_specs=[pl.BlockSpec((nseq, rb, DN_CONV), lambda b, i: (b, i, 0)),
                  pl.BlockSpec((nseq, rb, DN_W), lambda b, i: (b, i, 0)),
                  pl.BlockSpec((nseq, rb, LANES), lambda b, i: (b, i, 0)),
                  pl.BlockSpec((CONV_K, DN_CONV), lambda b, i: (0, 0)),
                  pl.BlockSpec((2, LANES), lambda b, i: (0, 0)),
                  pl.BlockSpec((1, DN_D), lambda b, i: (0, 0))],
        out_specs=pl.BlockSpec((nseq, rb, DN_W), lambda b, i: (b, i, 0)),
        out_shape=jax.ShapeDtypeStruct((B, S, DN_W), BF16),
        scratch_shapes=[pltpu.VMEM((nseq, rb + 8, DN_CONV), F32),
                        pltpu.VMEM((nseq, rb, DN_CONV), F32),
                        pltpu.VMEM((nseq, 2 * DN_D, 2 * DN_D), F32),
                        pltpu.VMEM((nseq, 2 * DN_D, 2 * DN_D), F32)],
        compiler_params=_cparams(("arbitrary", "arbitrary")),
        name="gdn",
    )(dqkv.reshape(B, S, DN_CONV), dz.reshape(B, S, DN_W), gbc.reshape(B, S, LANES),
      conv_w, prm, dn_norm_w.reshape(1, DN_D))
    return out.reshape(B * S, DN_W)


def _out_kernel(o1_ref, o2_ref, o3_ref, l1_ref, l2_ref, l3_ref, dn_ref, x_ref, mod_ref, wo_ref,
                n2_ref, wr_ref, br_ref,
                x1_ref, h2_ref, te_ref, rk_ref, gtc_ref, cnt_ref, base, scr, *, tm):
    i = pl.program_id(0)

    @pl.when(i == 0)
    def _():
        base[...] = jnp.zeros_like(base)

    def natural(ref, d):
        if d == 1:
            return ref[0, 0].astype(F32)
        nl = ref.shape[-1] // LANES
        for res in range(d):
            blk = ref[0, res].astype(F32)
            for c in range(nl):
                scr[c, pl.ds(res, tm // d, stride=d), :] = blk[:, c * LANES:(c + 1) * LANES]
        return jnp.concatenate([scr[c] for c in range(nl)], axis=1)

    l1, l2, l3 = (natural(r, d) for r, d in zip((l1_ref, l2_ref, l3_ref), DILATIONS))
    mx = jnp.maximum(jnp.maximum(l1, l2), l3)
    e1, e2, e3 = jnp.exp2(l1 - mx), jnp.exp2(l2 - mx), jnp.exp2(l3 - mx)
    zs = e1 + e2 + e3
    er = lax.broadcasted_iota(I32, (LANES, ATTN_W), 0)
    ec = lax.broadcasted_iota(I32, (LANES, ATTN_W), 1)
    expand = jnp.where(ec // ATTN_HD == er, 1.0, 0.0).astype(BF16)
    attn = jnp.zeros((tm, ATTN_W), F32)
    for e, o_ref, d in zip((e1, e2, e3), (o1_ref, o2_ref, o3_ref), DILATIONS):
        wgt = jnp.dot((e / zs).astype(BF16), expand, preferred_element_type=F32)
        attn = attn + wgt * natural(o_ref, d)
    mix = (jnp.dot(attn.astype(BF16), wo_ref[0:ATTN_W, :], preferred_element_type=F32)
           + jnp.dot(dn_ref[...], wo_ref[ATTN_W:, :], preferred_element_type=F32))
    x1 = x_ref[...] + mod_ref[0, 2:3, :] * mix
    x1_ref[...] = x1
    ms = jnp.mean(x1 * x1, axis=-1, keepdims=True)
    h2 = x1 * lax.rsqrt(ms + EPS) * n2_ref[...]
    h2 = h2 * (1.0 + mod_ref[0, 4:5, :]) + mod_ref[0, 3:4, :]
    _store_slabs(h2_ref, h2)

    def split(a):
        hi = a.astype(BF16)
        return hi, (a - hi.astype(F32)).astype(BF16)

    h_hi, h_lo = split(h2)
    w_hi, w_lo = split(wr_ref[...])
    lg = _nt(w_hi, h_hi) + (_nt(w_hi, h_lo) + _nt(w_lo, h_hi)) + br_ref[...]
    eidx = lax.broadcasted_iota(I32, (N_EXPERTS, tm), 0)
    vals, idxs, sels = [], [], []
    for _ in range(TOP_K):
        m = jnp.max(lg, axis=0, keepdims=True)
        idx = jnp.min(jnp.where(lg == m, eidx, N_EXPERTS), axis=0, keepdims=True)
        sel = eidx == idx
        vals.append(m)
        idxs.append(idx)
        sels.append(sel)
        lg = jnp.where(sel, -jnp.inf, lg)
    ex = [jnp.exp(v - vals[0]) for v in vals]
    den = ex[0] + ex[1] + ex[2] + ex[3]
    gates = [e / den for e in ex]

    msum = jnp.zeros((N_EXPERTS, tm), F32)
    for sel in sels:
        msum = msum + jnp.where(sel, 1.0, 0.0)
    tr = lax.broadcasted_iota(I32, (tm, tm), 0)
    tc = lax.broadcasted_iota(I32, (tm, tm), 1)
    upper = jnp.where(tr <= tc, 1.0, 0.0).astype(BF16)
    incl = jnp.dot(msum.astype(BF16), upper, preferred_element_type=F32)
    pos = base[:, 0:1] + (incl - msum)
    sub8 = lax.broadcasted_iota(I32, (8, tm), 0)
    te = jnp.zeros((8, tm), I32)
    rk = jnp.zeros((8, tm), I32)
    gt = jnp.zeros((8, tm), F32)
    for k in range(TOP_K):
        rank_k = jnp.sum(jnp.where(sels[k], pos, 0.0), axis=0, keepdims=True).astype(I32)
        te = jnp.where(sub8 == k, idxs[k], te)
        rk = jnp.where(sub8 == k, rank_k, rk)
        gt = jnp.where(sub8 == k, gates[k], gt)
    te_ref[...] = te
    rk_ref[...] = rk
    gtc_ref[...] = jnp.transpose(jnp.concatenate([gt, jnp.zeros((LANES - 8, tm), F32)], axis=0))
    base[...] = base[...] + jnp.sum(msum, axis=1, keepdims=True)
    cnt_ref[...] = base[...].astype(I32)


def _outproj(o1, o2, o3, l1, l2, l3, dn, x2, mod, w_out, norm2_w, w_router, b_router, S):
    T, D = x2.shape
    tm = PROJ_ROWS
    nt = S // tm
    row = lambda w: pl.BlockSpec((tm, w), lambda i: (i, 0))
    res = lambda d, w: pl.BlockSpec((1, d, tm // d, w), lambda i: (i // nt, 0, i % nt, 0))
    colb = pl.BlockSpec((8, tm), lambda i: (0, i))
    return pl.pallas_call(
        functools.partial(_out_kernel, tm=tm),
        grid=(T // tm,),
        in_specs=[res(d, ATTN_W) for d in DILATIONS] + [res(d, LANES) for d in DILATIONS]
        + [row(DN_W), row(D),
                  pl.BlockSpec((1, 6, D), lambda i: (i * tm // S, 0, 0)),
                  pl.BlockSpec((D, D), lambda i: (0, 0)),
                  pl.BlockSpec((1, D), lambda i: (0, 0)),
                  pl.BlockSpec((N_EXPERTS, D), lambda i: (0, 0)),
                  pl.BlockSpec((N_EXPERTS, 1), lambda i: (0, 0))],
        out_specs=[row(D), pl.BlockSpec((tm * (D // LANES), LANES), lambda i: (i, 0)), colb, colb,
                   row(LANES), pl.BlockSpec((N_EXPERTS, LANES), lambda i: (0, 0))],
        out_shape=[jax.ShapeDtypeStruct((T, D), F32),
                   jax.ShapeDtypeStruct((T * (D // LANES), LANES), F32),
                   jax.ShapeDtypeStruct((8, T), I32),
                   jax.ShapeDtypeStruct((8, T), I32),
                   jax.ShapeDtypeStruct((T, LANES), F32),
                   jax.ShapeDtypeStruct((N_EXPERTS, LANES), I32)],
        scratch_shapes=[pltpu.VMEM((N_EXPERTS, LANES), F32),
                        pltpu.VMEM((ATTN_W // LANES, tm, LANES), F32)],
        compiler_params=_cparams(("arbitrary",)),
        name="outproj_router",
    )(o1, o2, o3, l1, l2, l3, dn, x2, mod, w_out.astype(BF16), norm2_w.reshape(1, D),
      jnp.transpose(w_router), b_router.reshape(N_EXPERTS, 1))


def _dest_kernel(ps_ref, te_ref, rk_ref, d_ref):
    te = te_ref[...]
    acc = jnp.zeros(te.shape, I32)
    for e in range(N_EXPERTS):
        acc = jnp.where(te == e, ps_ref[e], acc)
    d_ref[...] = acc + rk_ref[...]


def _dest(pstart, te, rk):
    T = te.shape[1]
    tb = DEST_COLS
    return pl.pallas_call(
        _dest_kernel,
        grid_spec=pltpu.PrefetchScalarGridSpec(
            num_scalar_prefetch=1,
            grid=(T // tb,),
            in_specs=[pl.BlockSpec((8, tb), lambda i, ps: (0, i)),
                      pl.BlockSpec((8, tb), lambda i, ps: (0, i))],
            out_specs=pl.BlockSpec((8, tb), lambda i, ps: (0, i))),
        out_shape=jax.ShapeDtypeStruct((8, T), I32),
        compiler_params=_cparams(("arbitrary",)),
        name="dest_rows",
    )(pstart, te, rk)


SC_CORES = 2
SC_SUBCORES = 16
SC_IDX_CHUNK = 128


def _invperm(dest_flat, P):
    N = dest_flat.shape[0]
    nch = N // (SC_SUBCORES * SC_IDX_CHUNK)
    half = P // SC_CORES
    per_out = half // SC_SUBCORES
    assert N % (SC_SUBCORES * SC_IDX_CHUNK) == 0 and P % (SC_CORES * SC_SUBCORES * 8) == 0
    mesh = plsc.VectorSubcoreMesh(core_axis_name="c", subcore_axis_name="s",
                                  num_cores=SC_CORES, num_subcores=SC_SUBCORES)

    @functools.partial(
        pl.kernel, mesh=mesh, out_type=jax.ShapeDtypeStruct((P,), I32),
        scratch_types=[pltpu.VMEM((nch, SC_IDX_CHUNK), I32), pltpu.VMEM((nch, SC_IDX_CHUNK), I32),
                       pltpu.VMEM_SHARED((P,), I32), pltpu.VMEM((per_out,), I32), pltpu.SemaphoreType.DMA])
    def scatter_codes(idx_hbm, val_hbm, out_hbm, idx_v, val_v, table, stage, sem):
        core = lax.axis_index("c")
        sub = lax.axis_index("s")
        pltpu.sync_copy(idx_hbm.at[sub], idx_v)
        pltpu.sync_copy(val_hbm.at[sub], val_v)

        @pl.loop(0, nch)
        def _(j):
            pltpu.async_copy(val_v.at[j], table.at[idx_v.at[j]], sem)

        @pl.loop(0, nch)
        def _(j):
            pltpu.make_async_copy(val_v.at[j], table.at[idx_v.at[j]], sem).wait()

        plsc.subcore_barrier()
        off = pl.multiple_of(core * half + sub * per_out, 8)
        pltpu.sync_copy(table.at[pl.ds(off, per_out)], stage)
        pltpu.sync_copy(stage, out_hbm.at[pl.ds(off, per_out)])

    vals = jnp.arange(N, dtype=I32)
    shape = (SC_SUBCORES, nch, SC_IDX_CHUNK)
    return scatter_codes(dest_flat.reshape(shape), vals.reshape(shape))


def _row_copy(src, dst, sem):
    return pltpu.make_async_copy(src, dst, sem)


def _moe_kernel(be_ref, nv_ref, cc_ref, cn_ref, h2_ref, w1_ref, b1_ref, w2_ref, b2_ref, y4_ref,
                xbuf, ybuf, w1b, w2b, gsem, ssem, *, F, T, D, tme, nb):
    i = pl.program_id(0)
    s = i % 2
    ns = D // LANES

    def rows(first, n):
        return pl.ds(pl.multiple_of(first * ns, ns), n * ns)

    def gather_copy(tok, p, slot):
        return _row_copy(h2_ref.at[rows(tok, 1)], xbuf.at[slot, rows(p, 1)], gsem.at[slot])

    def token_of(code):
        return code & (T - 1) if T & (T - 1) == 0 else code % T

    def issue_rows(start_row, nv):
        ng = nv // ROW_UNROLL

        def grp(g, c):
            for j in range(ROW_UNROLL):
                start_row(g * ROW_UNROLL + j, j % 2)
            return c
        lax.fori_loop(0, ng, grp, 0)

        def one(p, c):
            start_row(p, 0)
            return c
        lax.fori_loop(ng * ROW_UNROLL, nv, one, 0)

    def start_gather(code_ref, nv, slot):
        issue_rows(lambda p, pri: gather_copy(token_of(code_ref[0, 0, p]), p, slot).start(priority=pri), nv)

    def wait_rows(make, nv):
        @pl.when(nv > 0)
        def _():
            make(nv).wait()

    def gathered(n, slot):
        return _row_copy(h2_ref.at[rows(0, n)], xbuf.at[slot, rows(0, n)], gsem.at[slot])

    def scattered(n, slot):
        return _row_copy(ybuf.at[slot, rows(0, n)], y4_ref.at[rows(0, n)], ssem.at[slot])

    @pl.when(i == 0)
    def _():
        start_gather(cc_ref, tme, 0)

    nv = nv_ref[i]
    gathered(tme, s).wait()

    @pl.when(i >= 2)
    def _():
        wait_rows(lambda n: scattered(n, s), nv_ref[i - 2])

    @pl.when(jnp.logical_or(i == 0, be_ref[i] != be_ref[jnp.maximum(i - 1, 0)]))
    def _():
        w1b[...] = w1_ref[0].astype(BF16)
        w2b[...] = w2_ref[0].astype(BF16)

    def compute(slot):
        x = _load_slabs(xbuf, tme, D, lead=(slot,)).astype(BF16)
        for p in range(tme):
            gather_copy(token_of(cn_ref[0, 0, p]), p, 1 - slot).start(priority=p % 2)
        hgu = jnp.dot(x, w1b[...], preferred_element_type=F32) + b1_ref[0]
        gate = jnp.minimum(hgu[:, :F], SWIGLU_LIMIT)
        up = jnp.clip(hgu[:, F:], -SWIGLU_LIMIT, SWIGLU_LIMIT)
        act = gate * _sigmoid(SWIGLU_ALPHA * gate) * (up + 1.0)
        y = jnp.dot(act.astype(BF16), w2b[...], preferred_element_type=F32) + b2_ref[0]
        _store_slabs(ybuf, y, lead=(slot,))

        issue_rows(lambda p, pri: _row_copy(ybuf.at[slot, rows(p, 1)], y4_ref.at[rows(cc_ref[0, 0, p], 1)],
                                            ssem.at[slot]).start(priority=pri), nv)

    for slot in range(2):
        pl.when(jnp.logical_and(nv > 0, s == slot))(functools.partial(compute, slot))

    @pl.when(jnp.logical_and(nv == 0, i + 1 < nb))
    def _():
        start_gather(cn_ref, tme, 1 - s)

    @pl.when(i == nb - 1)
    def _():
        @pl.when(nv > 0)
        def _():
            gathered(tme, 1 - s).wait()
        wait_rows(lambda n: scattered(n, s), nv)
        if nb > 1:
            wait_rows(lambda n: scattered(n, 1 - s), nv_ref[i - 1])


def _experts(blk_exp, blk_valid, codes, h2s, w1, b1, w2, b2, tme):
    E, D, F2 = w1.shape
    F = F2 // 2
    ns = D // LANES
    T = h2s.shape[0] // ns
    nb = blk_exp.shape[0]
    codes3 = codes.reshape(nb, 1, tme)
    wspec = lambda shape: pl.BlockSpec(shape, lambda i, be, nv: (be[i], 0, 0))
    cspec = lambda off: pl.BlockSpec((1, 1, tme), lambda i, be, nv: (jnp.minimum(i + off, nb - 1), 0, 0),
                                     memory_space=pltpu.SMEM)
    return pl.pallas_call(
        functools.partial(_moe_kernel, F=F, T=T, D=D, tme=tme, nb=nb),
        grid_spec=pltpu.PrefetchScalarGridSpec(
            num_scalar_prefetch=2,
            grid=(nb,),
            in_specs=[cspec(0), cspec(1),
                      pl.BlockSpec(memory_space=pl.ANY),
                      wspec((1, D, F2)), wspec((1, 1, F2)), wspec((1, F, D)), wspec((1, 1, D))],
            out_specs=pl.BlockSpec(memory_space=pl.ANY),
            scratch_shapes=[pltpu.VMEM((2, tme * ns, LANES), F32), pltpu.VMEM((2, tme * ns, LANES), F32),
                            pltpu.VMEM((D, F2), BF16), pltpu.VMEM((F, D), BF16),
                            pltpu.SemaphoreType.DMA((2,)), pltpu.SemaphoreType.DMA((2,))]),
        out_shape=jax.ShapeDtypeStruct((TOP_K * T * ns, LANES), F32),
        compiler_params=_cparams(("arbitrary",)),
        name="experts",
    )(blk_exp, blk_valid, codes3, codes3, h2s, w1, b1.reshape(E, 1, F2), w2, b2.reshape(E, 1, D))


def _comb_kernel(g_ref, x1_ref, mod_ref, fw_ref, y0_ref, y1_ref, y2_ref, y3_ref, o_ref):
    g = g_ref[...]
    n, d = x1_ref.shape
    y = g[:, 0:1] * _load_slabs(y0_ref, n, d)
    for k, y_ref in ((1, y1_ref), (2, y2_ref), (3, y3_ref)):
        y = y + g[:, k:k + 1] * _load_slabs(y_ref, n, d)
    x2 = x1_ref[...] + mod_ref[0, 5:6, :] * y
    ms = jnp.mean(x2 * x2, axis=-1, keepdims=True)
    o_ref[...] = x2 * lax.rsqrt(ms + EPS) * fw_ref[...]


def _combine(gtc, x1, mod, final_w, y4, S):
    T, D = x1.shape
    tmc = COMBINE_ROWS
    nt = T // tmc
    yspec = lambda k: pl.BlockSpec((tmc * (D // LANES), LANES), lambda i: (k * nt + i, 0))
    return pl.pallas_call(
        _comb_kernel,
        grid=(nt,),
        in_specs=[pl.BlockSpec((tmc, LANES), lambda i: (i, 0)),
                  pl.BlockSpec((tmc, D), lambda i: (i, 0)),
                  pl.BlockSpec((1, 6, D), lambda i: (i * tmc // S, 0, 0)),
                  pl.BlockSpec((1, D), lambda i: (0, 0)),
                  yspec(0), yspec(1), yspec(2), yspec(3)],
        out_specs=pl.BlockSpec((tmc, D), lambda i: (i, 0)),
        out_shape=jax.ShapeDtypeStruct((T, D), F32),
        compiler_params=_cparams(("arbitrary",)),
        name="combine",
    )(gtc, x1, mod, final_w.reshape(1, D), y4, y4, y4, y4)


def _layer(x2, mod, B, S, norm1_w, w_in, conv_w, A_log, dt_bias, dn_norm_w, w_out,
           norm2_w, w_router, b_router, w1, b1, w2, b2):
    T, D = x2.shape
    qkv, qkv4, qkv16, dqkv, dz, gbc = _inproj(x2, mod, norm1_w, w_in, S)
    views = (qkv.reshape(B, 1, S, 3 * ATTN_W), qkv4, qkv16)
    branches = [_attn_branch(v, d) for v, d in zip(views, DILATIONS)]
    dn = _gdn(dqkv, dz, gbc, conv_w, A_log, dt_bias, dn_norm_w, B, S)
    (o1, l1), (o2, l2), (o3, l3) = branches
    x1, h2, te, rk, gtc, cnt = _outproj(o1, o2, o3, l1, l2, l3, dn, x2, mod, w_out, norm2_w,
                                        w_router, b_router, S)
    tme = MOE_ROWS
    P = T * TOP_K + N_EXPERTS * tme
    counts = cnt[:, 0]
    padded = (counts + tme - 1) // tme * tme
    pend = jnp.cumsum(padded)
    pstart = (pend - padded).astype(I32)
    blk_start = jnp.arange(P // tme, dtype=I32) * tme
    blk_exp = jnp.minimum(jnp.sum((pend[None, :] <= blk_start[:, None]).astype(I32), axis=1),
                          N_EXPERTS - 1).astype(I32)
    blk_valid = jnp.clip(pstart[blk_exp] + counts[blk_exp] - blk_start, 0, tme).astype(I32)
    dest = _dest(pstart, te, rk)
    codes = _invperm(dest[:TOP_K].reshape(TOP_K * T), P)
    y4 = _experts(blk_exp, blk_valid, codes, h2, w1, b1, w2, b2, tme)
    return x1, gtc, y4


def kernel(x, c, w_ada, b_ada, norm1_w, w_in, conv_w, A_log, dt_bias, dn_norm_w, w_out, norm2_w,
           w_router, b_router, w1, b1, w2, b2, final_norm_w):
    B, S, D = x.shape
    depth = w_ada.shape[0]
    assert depth == 1, "one layer: the final norm is fused into the combine step"
    assert S % (ATTN_BLK * DILATIONS[-1]) == 0 and S % max(PROJ_ROWS, GDN_ROWS, COMBINE_ROWS) == 0
    assert (B * S) % DEST_COLS == 0 and w_ada.shape[2] % ADA_COLS == 0
    x2 = x.reshape(B * S, D)
    mod = _ada(c, w_ada[0], b_ada[0])
    x1, gtc, y4 = _layer(x2, mod, B, S, norm1_w[0], w_in[0], conv_w[0], A_log[0], dt_bias[0],
                         dn_norm_w[0], w_out[0], norm2_w[0], w_router[0], b_router[0],
                         w1[0], b1[0], w2[0], b2[0])
    out = _combine(gtc, x1, mod, final_norm_w, y4, S)
    return out.reshape(B, S, D)
```

```python
import functools

import jax
import jax.numpy as jnp
from jax import lax
from jax.experimental import pallas as pl
from jax.experimental.pallas import tpu as pltpu
from jax.experimental.pallas import tpu_sc as plsc

F32 = jnp.float32
BF16 = jnp.bfloat16
I32 = jnp.int32
HI = lax.Precision.HIGHEST

LANES = 128
ATTN_HEADS = 8
ATTN_HD = 64
ATTN_W = ATTN_HEADS * ATTN_HD
ATTN_BLK = 128
DILATIONS = (1, 4, 16)
DN_HEADS = 4
DN_D = 128
DN_W = DN_HEADS * DN_D
DN_CONV = 3 * DN_W
CONV_K = 4
DN_CHUNK = 64
N_EXPERTS = 32
TOP_K = 4
MOE_ROWS = 512
ROW_UNROLL = 8
SWIGLU_LIMIT = 7.0
SWIGLU_ALPHA = 1.702
EPS = 1e-6
NEG = -1e30
LOG2E = 1.4426950408889634
MAIN_COLS = 3 * ATTN_W + DN_CONV + DN_W

ADA_COLS = 1024
PROJ_ROWS = 1024
ATTN_ROWS = 1024
GDN_ROWS = 1024
DEST_COLS = 2048
COMBINE_ROWS = 1024

V7X_VMEM_BYTES = 64 * 1024 * 1024
VMEM_LIMIT = V7X_VMEM_BYTES - 8 * 1024 * 1024


def _cparams(sem):
    return pltpu.CompilerParams(dimension_semantics=sem, vmem_limit_bytes=VMEM_LIMIT)


def _nt(a, b, **kw):
    return lax.dot_general(a, b, (((1,), (1,)), ((), ())), preferred_element_type=F32, **kw)


def _sigmoid(x):
    return 1.0 / (1.0 + jnp.exp(-x))


def _store_slabs(ref, val, lead=()):
    n, d = val.shape
    ns = d // LANES
    for c in range(ns):
        ref[lead + (pl.ds(c, n, stride=ns), slice(None))] = val[:, c * LANES:(c + 1) * LANES]


def _load_slabs(ref, n, d, lead=()):
    ns = d // LANES
    return jnp.concatenate([ref[lead + (pl.ds(c, n, stride=ns), slice(None))] for c in range(ns)], axis=1)


def _ada_kernel(c_ref, w_ref, b_ref, o_ref):
    c = c_ref[...]
    cond = c * _sigmoid(c)
    o_ref[...] = jnp.dot(cond, w_ref[...], preferred_element_type=F32, precision=HI) + b_ref[...]


def _ada(c, w_ada, b_ada):
    B, D = c.shape
    N = w_ada.shape[1]
    cp = jnp.zeros((8, D), F32).at[:B].set(c)
    tn = ADA_COLS
    out = pl.pallas_call(
        _ada_kernel,
        grid=(N // tn,),
        in_specs=[pl.BlockSpec((8, D), lambda j: (0, 0)),
                  pl.BlockSpec((D, tn), lambda j: (0, j)),
                  pl.BlockSpec((1, tn), lambda j: (0, j))],
        out_specs=pl.BlockSpec((8, tn), lambda j: (0, j)),
        out_shape=jax.ShapeDtypeStruct((8, N), F32),
        compiler_params=_cparams(("arbitrary",)),
        name="ada",
    )(cp, w_ada, b_ada.reshape(1, N))
    return out[:B].reshape(B, 6, D)


def _inproj_kernel(x_ref, mod_ref, nw_ref, wm_ref, ws_ref, qkv_ref, qkv4_ref, qkv16_ref, dqkv_ref, dz_ref,
                   gbc_ref, scr, scr2):
    x = x_ref[...]
    tm = x.shape[0]
    shift = mod_ref[0, 0:1, :]
    scale = mod_ref[0, 1:2, :]
    ms = jnp.mean(x * x, axis=-1, keepdims=True)
    h = x * lax.rsqrt(ms + EPS) * nw_ref[...]
    hb = (h * (1.0 + scale) + shift).astype(BF16)
    nl = ATTN_W // LANES
    proj = [jnp.dot(hb, wm_ref[:, c0:c0 + ATTN_W], preferred_element_type=F32)
            for c0 in range(0, MAIN_COLS, ATTN_W)]
    for j in range(3):
        cs = slice(j * ATTN_W, (j + 1) * ATTN_W)
        r = proj[j]
        if j == 0:
            r = r * (ATTN_HD ** -0.5 * LOG2E)
        qkv_ref[:, cs] = r.astype(BF16)
        d1, d2 = DILATIONS[1], DILATIONS[2]
        step = d2 // d1
        for c in range(nl):
            scr[c] = r[:, c * LANES:(c + 1) * LANES]
        for res in range(d1):
            for c in range(nl):
                scr2[c, res * (tm // d1):(res + 1) * (tm // d1), :] = scr[c, pl.ds(res, tm // d1, stride=d1), :]
            part = jnp.concatenate([scr2[c, res * (tm // d1):(res + 1) * (tm // d1), :] for c in range(nl)],
                                   axis=1)
            qkv4_ref[0, res, :, cs] = part.astype(BF16)
        for res in range(d2):
            r1, q = res % d1, res // d1
            part = jnp.concatenate(
                [scr2[c, pl.ds(r1 * (tm // d1) + q, tm // d2, stride=step), :] for c in range(nl)], axis=1)
            qkv16_ref[0, res, :, cs] = part.astype(BF16)
    for j in range(3):
        dqkv_ref[:, j * DN_W:(j + 1) * DN_W] = proj[3 + j].astype(BF16)
    dz_ref[...] = proj[6].astype(BF16)
    gbc_ref[...] = jnp.dot(hb, ws_ref[...], preferred_element_type=F32)


def _inproj(x2, mod, norm_w, w_in, S):
    T, D = x2.shape
    tm = PROJ_ROWS
    B, nt = T // S, S // tm
    wm = w_in[:, :MAIN_COLS].astype(BF16)
    ws = jnp.zeros((D, LANES), F32).at[:, :2 * DN_HEADS].set(w_in[:, MAIN_COLS:]).astype(BF16)
    return pl.pallas_call(
        _inproj_kernel,
        grid=(T // tm,),
        in_specs=[pl.BlockSpec((tm, D), lambda i: (i, 0)),
                  pl.BlockSpec((1, 6, D), lambda i: (i * tm // S, 0, 0)),
                  pl.BlockSpec((1, D), lambda i: (0, 0)),
                  pl.BlockSpec((D, MAIN_COLS), lambda i: (0, 0)),
                  pl.BlockSpec((D, LANES), lambda i: (0, 0))],
        out_specs=[pl.BlockSpec((tm, 3 * ATTN_W), lambda i: (i, 0))]
        + [pl.BlockSpec((1, d, tm // d, 3 * ATTN_W), lambda i: (i // nt, 0, i % nt, 0)) for d in DILATIONS[1:]]
        + [pl.BlockSpec((tm, DN_CONV), lambda i: (i, 0)),
           pl.BlockSpec((tm, DN_W), lambda i: (i, 0)),
           pl.BlockSpec((tm, LANES), lambda i: (i, 0))],
        out_shape=[jax.ShapeDtypeStruct((T, 3 * ATTN_W), BF16)]
        + [jax.ShapeDtypeStruct((B, d, S // d, 3 * ATTN_W), BF16) for d in DILATIONS[1:]]
        + [jax.ShapeDtypeStruct((T, DN_CONV), BF16),
           jax.ShapeDtypeStruct((T, DN_W), BF16),
           jax.ShapeDtypeStruct((T, LANES), F32)],
        scratch_shapes=[pltpu.VMEM((ATTN_W // LANES, tm, LANES), F32),
                        pltpu.VMEM((ATTN_W // LANES, tm, LANES), F32)],
        compiler_params=_cparams(("arbitrary",)),
        name="inproj",
    )(x2, mod, norm_w.reshape(1, D), wm, ws)


def _attn_kernel(q_ref, kc_ref, kp_ref, vc_ref, vp_ref, o_ref, lse_ref, kf, vf, *, qb):
    n = pl.program_id(2)
    kf[0:ATTN_BLK, :] = kp_ref[0, 0]
    kf[ATTN_BLK:, :] = kc_ref[0, 0]
    vf[0:ATTN_BLK, :] = vp_ref[0, 0]
    vf[ATTN_BLK:, :] = vc_ref[0, 0]
    row = lax.broadcasted_iota(I32, (ATTN_BLK, 2 * ATTN_BLK), 0)
    col = lax.broadcasted_iota(I32, (ATTN_BLK, 2 * ATTN_BLK), 1)
    band = jnp.logical_or(jnp.logical_and(col < ATTN_BLK, col >= row),
                          jnp.logical_and(col >= ATTN_BLK, col - ATTN_BLK <= row))
    lane = lax.broadcasted_iota(I32, (ATTN_BLK, LANES), 1)
    lo = lane < ATTN_HD

    def sub(j, carry):
        r0 = pl.multiple_of(j * ATTN_BLK, ATTN_BLK)
        first_col = jnp.where(jnp.logical_and(n == 0, j == 0), ATTN_BLK, 0)
        mask = jnp.logical_and(band, col >= first_col)
        npair = ATTN_W // LANES
        cols = [slice(hp * LANES, (hp + 1) * LANES) for hp in range(npair)]
        heads = [(hp, half) for hp in range(npair) for half in range(2)]
        scores = []
        for hp, half in heads:
            q2 = q_ref[0, 0, pl.ds(r0, ATTN_BLK), cols[hp]]
            qm = jnp.where(lo if half == 0 else jnp.logical_not(lo), q2, jnp.zeros_like(q2))
            scores.append(jnp.where(mask, _nt(qm, kf[pl.ds(r0, 2 * ATTN_BLK), cols[hp]]), NEG))
        maxes = [jnp.max(s, axis=-1, keepdims=True) for s in scores]
        probs = [jnp.exp2(s - m) for s, m in zip(scores, maxes)]
        dens = [jnp.sum(p, axis=-1, keepdims=True) for p in probs]
        accs = [jnp.dot(p.astype(BF16), vf[pl.ds(r0, 2 * ATTN_BLK), cols[hp]], preferred_element_type=F32)
                for p, (hp, _) in zip(probs, heads)]
        outs = [a / d for a, d in zip(accs, dens)]
        lse_tile = jnp.zeros((ATTN_BLK, LANES), F32)
        for h, (m, d) in enumerate(zip(maxes, dens)):
            lse_tile = jnp.where(lane == h, m + jnp.log2(d), lse_tile)
        for hp in range(npair):
            o_ref[0, 0, pl.ds(r0, ATTN_BLK), cols[hp]] = jnp.where(lo, outs[2 * hp], outs[2 * hp + 1]).astype(BF16)
        lse_ref[0, 0, pl.ds(r0, ATTN_BLK), :] = lse_tile
        return carry

    lax.fori_loop(0, qb // ATTN_BLK, sub, 0, unroll=True)


def _attn_branch(qkv, d):
    B, _, L, _ = qkv.shape
    qb = min(ATTN_ROWS, L)
    nsub = qb // ATTN_BLK
    cur = lambda c: pl.BlockSpec((1, 1, qb, ATTN_W), lambda b, r, n: (b, r, n, c))
    prev = lambda c: pl.BlockSpec((1, 1, ATTN_BLK, ATTN_W),
                                  lambda b, r, n: (b, r, jnp.maximum(n * nsub - 1, 0), c))
    return pl.pallas_call(
        functools.partial(_attn_kernel, qb=qb),
        grid=(B, d, L // qb),
        in_specs=[cur(0), cur(1), prev(1), cur(2), prev(2)],
        out_specs=[pl.BlockSpec((1, 1, qb, ATTN_W), lambda b, r, n: (b, r, n, 0)),
                   pl.BlockSpec((1, 1, qb, LANES), lambda b, r, n: (b, r, n, 0))],
        out_shape=[jax.ShapeDtypeStruct((B, d, L, ATTN_W), BF16),
                   jax.ShapeDtypeStruct((B, d, L, LANES), F32)],
        scratch_shapes=[pltpu.VMEM((qb + ATTN_BLK, ATTN_W), BF16),
                        pltpu.VMEM((qb + ATTN_BLK, ATTN_W), BF16)],
        compiler_params=_cparams(("arbitrary", "arbitrary", "arbitrary")),
        name=f"attn_d{d}",
    )(qkv, qkv, qkv, qkv, qkv)


def _gdn_kernel(x_ref, z_ref, g_ref, cw_ref, prm_ref, nw_ref, o_ref, xext, yc, s0, s1, *, rb, nseq):
    i = pl.program_id(1)

    @pl.when(i == 0)
    def _():
        xext[:, 0:8, :] = jnp.zeros((nseq, 8, DN_CONV), F32)
        s0[...] = jnp.zeros_like(s0)
        s1[...] = jnp.zeros_like(s1)

    @pl.when(i > 0)
    def _():
        xext[:, 0:8, :] = xext[:, rb:rb + 8, :]

    for q in range(nseq):
        xext[q, 8:, :] = x_ref[q].astype(F32)
        y = cw_ref[CONV_K - 1:CONV_K, :] * xext[q, 8:8 + rb, :]
        for j in range(CONV_K - 1):
            off = 8 - (CONV_K - 1) + j
            y = y + cw_ref[j:j + 1, :] * xext[q, off:off + rb, :]
        yc[q] = y * _sigmoid(y)

    C = DN_CHUNK
    H = DN_HEADS
    CW = H * C
    dot = functools.partial(jnp.dot, preferred_element_type=F32)

    def iota(shape, d):
        return lax.broadcasted_iota(I32, shape, d)

    ltri_b = jnp.where(iota((C, C), 0) >= iota((C, C), 1), 1.0, 0.0).astype(BF16)
    lane = iota((C, LANES), 1)
    blane = lane < H
    glane = jnp.logical_and(lane >= H, lane < 2 * H)
    e512 = jnp.where(jnp.logical_or(iota((LANES, DN_W), 1) // DN_D == iota((LANES, DN_W), 0),
                                    iota((LANES, DN_W), 1) // DN_D == iota((LANES, DN_W), 0) - H),
                     1.0, 0.0).astype(BF16)
    e256 = jnp.where(iota((LANES, CW), 1) // C == iota((LANES, CW), 0) - H, 1.0, 0.0).astype(BF16)
    row4 = iota((C, CW), 0)
    col4 = iota((C, CW), 1) % C
    eye4 = jnp.where(row4 == col4, 1.0, 0.0).astype(F32)
    blk = [iota((C, CW), 1) // C == h for h in range(H)]
    bd_cc = iota((CW, CW), 0) // C == iota((CW, CW), 1) // C
    bd_pair = iota((CW, CW), 0) // DN_D == iota((CW, CW), 1) // DN_D
    rt_mask = iota((CW, DN_W), 0) // C == iota((CW, DN_W), 1) // DN_D
    neg_a = -jnp.exp(prm_ref[0:1, :])
    dtb = prm_ref[1:2, :]
    nw = nw_ref[...]

    def hilo(x):
        hi = x.astype(BF16)
        return hi, (x - hi.astype(F32)).astype(BF16)

    def heads(a, w):
        return [a[:, h * w:(h + 1) * w] for h in range(H)]

    def l2n(a, mult):
        return jnp.concatenate(
            [p * (lax.rsqrt(jnp.sum(p * p, axis=-1, keepdims=True) + EPS) * mult) for p in heads(a, DN_D)],
            axis=1)

    def stack4(a):
        return jnp.concatenate([a, a, a, a], axis=0)

    zb = jnp.zeros((), BF16)
    nchunk = rb // C
    chunks = []
    for c, sq in ((c, sq) for c in range(nchunk) for sq in range(nseq)):
        rs = slice(c * C, (c + 1) * C)
        G = g_ref[sq, rs, :]
        xg = G + dtb
        gv = jnp.where(glane, neg_a * (jnp.maximum(xg, 0.0) + jnp.log1p(jnp.exp(-jnp.abs(xg)))), 0.0)
        be = jnp.where(blane, _sigmoid(G), 0.0)
        g_hi, g_lo = hilo(gv)
        gcum = dot(ltri_b, g_hi) + dot(ltri_b, g_lo)
        gtot = gcum[C - 1:C, :]
        eg = jnp.where(glane, jnp.exp(gcum), 0.0)
        ek = jnp.where(glane, jnp.exp(gtot - gcum), 0.0)
        ex = dot(jnp.concatenate([be, eg, ek], axis=0).astype(BF16), e512)
        bexp, egexp, ekexp = ex[0:C], ex[C:2 * C], ex[2 * C:3 * C]
        gexp = dot(g_hi, e256) + dot(g_lo, e256)
        d_hi, d_lo = hilo(jnp.where(row4 > col4, gexp, 0.0))
        diff = dot(ltri_b, d_hi) + dot(ltri_b, d_lo)
        decay = jnp.exp(jnp.where(row4 >= col4, diff, NEG))

        q4 = l2n(yc[sq, rs, 0:DN_W], DN_D ** -0.5)
        k4 = l2n(yc[sq, rs, DN_W:2 * DN_W], 1.0)
        v4 = yc[sq, rs, 2 * DN_W:3 * DN_W]
        kb4 = k4 * bexp
        vb4 = v4 * bexp
        rt = jnp.where(rt_mask, stack4(k4.astype(BF16)), jnp.zeros((), BF16))
        ai = _nt(jnp.concatenate([kb4, q4], axis=0).astype(BF16), rt)
        a4 = jnp.where(row4 > col4, ai[0:C] * decay, 0.0)
        pb = a4.astype(BF16)
        chunks.append(dict(
            sq=sq, rs=rs, q4=q4, k4=k4, kb4=kb4, vb4=vb4, egexp=egexp, ekexp=ekexp,
            intra=ai[C:2 * C] * decay, t4=eye4 - a4, pb=pb,
            bd=jnp.where(bd_cc, stack4(pb), zb)))

    for _ in range(5):
        for ch in chunks:
            ch["pb"] = dot(ch["pb"], ch["bd"]).astype(BF16)
        for ch in chunks:
            ch["bd"] = jnp.where(bd_cc, stack4(ch["pb"]), zb)
        for ch in chunks:
            ch["t4"] = ch["t4"] + dot(ch["t4"].astype(BF16), ch["bd"])

    for ch in chunks:
        q4, k4, kb4, vb4, egexp, ekexp = (ch[n] for n in ("q4", "k4", "kb4", "vb4", "egexp", "ekexp"))
        t4b = ch["t4"].astype(BF16)
        lstk = jnp.concatenate([jnp.where(blk[h], t4b, zb) for h in range(H)], axis=0)
        kbg4 = kb4 * egexp
        rstk = jnp.concatenate(
            [jnp.concatenate([vb, kbg], axis=1) for vb, kbg in zip(heads(vb4, DN_D), heads(kbg4, DN_D))],
            axis=0).astype(BF16)
        uw = dot(lstk, rstk)
        u4 = jnp.concatenate([uw[h * C:(h + 1) * C, 0:DN_D] for h in range(H)], axis=1)
        w4 = jnp.concatenate([uw[h * C:(h + 1) * C, DN_D:2 * DN_D] for h in range(H)], axis=1)
        ib = ch["intra"].astype(BF16)
        ch.update(
            u4=u4, wq=jnp.concatenate([w4, q4 * egexp], axis=0).astype(BF16),
            kd4=(k4 * ekexp).astype(BF16), gl4=egexp[C - 1:C, :],
            lint=jnp.concatenate([jnp.where(blk[h], ib, zb) for h in range(H)], axis=0))

    for ch in chunks:
        sq, rs, u4, wq, kd4, gl4, lint = (ch[n] for n in ("sq", "rs", "u4", "wq", "kd4", "gl4", "lint"))
        ra = dot(wq[:, 0:CW], s0[sq].astype(BF16))
        rc = dot(wq[:, CW:2 * CW], s1[sq].astype(BF16))
        vn = u4 - jnp.concatenate([ra[0:C], rc[0:C]], axis=1)
        vnb = vn.astype(BF16)
        oi = dot(lint, jnp.concatenate(heads(vnb, DN_D), axis=0))
        o = (jnp.concatenate([ra[C:2 * C], rc[C:2 * C]], axis=1)
             + jnp.concatenate([oi[h * C:(h + 1) * C] for h in range(H)], axis=1))
        tn = (((0,), (0,)), ((), ()))
        s0[sq] = s0[sq] * gl4[:, 0:CW] + jnp.where(
            bd_pair, lax.dot_general(kd4[:, 0:CW], vnb[:, 0:CW], tn, preferred_element_type=F32), 0.0)
        s1[sq] = s1[sq] * gl4[:, CW:2 * CW] + jnp.where(
            bd_pair, lax.dot_general(kd4[:, CW:2 * CW], vnb[:, CW:2 * CW], tn, preferred_element_type=F32), 0.0)

        z = z_ref[sq, rs, :].astype(F32)
        on = jnp.concatenate(
            [p * lax.rsqrt(jnp.mean(p * p, axis=-1, keepdims=True) + EPS) * nw for p in heads(o, DN_D)], axis=1)
        o_ref[sq, rs, :] = (on * (z * _sigmoid(z))).astype(BF16)


def _gdn(dqkv, dz, gbc, conv_w, A_log, dt_bias, dn_norm_w, B, S):
    nseq = 2 if B % 2 == 0 else 1
    rb = GDN_ROWS // nseq
    prm = jnp.zeros((2, LANES), F32)
    prm = prm.at[0, DN_HEADS:2 * DN_HEADS].set(A_log.astype(F32))
    prm = prm.at[1, DN_HEADS:2 * DN_HEADS].set(dt_bias.astype(F32))
    out = pl.pallas_call(
        functools.partial(_gdn_kernel, rb=rb, nseq=nseq),
        grid=(B // nseq, S // rb),
        in_specs=[pl.BlockSpec((nseq, rb, DN_CONV), lambda b, i: (b, i, 0)),
                  pl.BlockSpec((nseq, rb, DN_W), lambda b, i: (b, i, 0)),
                  pl.BlockSpec((nseq, rb, LANES), lambda b, i: (b, i, 0)),
                  pl.BlockSpec((CONV_K, DN_CONV), lambda b, i: (0, 0)),
                  pl.BlockSpec((2, LANES), lambda b, i: (0, 0)),
                  pl.BlockSpec((1, DN_D), lambda b, i: (0, 0))],
        out_specs=pl.BlockSpec((nseq, rb, DN_W), lambda b, i: (b, i, 0)),
        out_shape=jax.ShapeDtypeStruct((B, S, DN_W), BF16),
        scratch_shapes=[pltpu.VMEM((nseq, rb + 8, DN_CONV), F32),
                        pltpu.VMEM((nseq, rb, DN_CONV), F32),
                        pltpu.VMEM((nseq, 2 * DN_D, 2 * DN_D), F32),
                        pltpu.VMEM((nseq, 2 * DN_D, 2 * DN_D), F32)],
        compiler_params=_cparams(("arbitrary", "arbitrary")),
        name="gdn",
    )(dqkv.reshape(B, S, DN_CONV), dz.reshape(B, S, DN_W), gbc.reshape(B, S, LANES),
      conv_w, prm, dn_norm_w.reshape(1, DN_D))
    return out.reshape(B * S, DN_W)


def _out_kernel(o1_ref, o2_ref, o3_ref, l1_ref, l2_ref, l3_ref, dn_ref, x_ref, mod_ref, wo_ref,
                n2_ref, wr_ref, br_ref,
                x1_ref, h2_ref, te_ref, rk_ref, gtc_ref, cnt_ref, base, scr, *, tm):
    i = pl.program_id(0)

    @pl.when(i == 0)
    def _():
        base[...] = jnp.zeros_like(base)

    def natural(ref, d):
        if d == 1:
            return ref[0, 0].astype(F32)
        nl = ref.shape[-1] // LANES
        for res in range(d):
            blk = ref[0, res].astype(F32)
            for c in range(nl):
                scr[c, pl.ds(res, tm // d, stride=d), :] = blk[:, c * LANES:(c + 1) * LANES]
        return jnp.concatenate([scr[c] for c in range(nl)], axis=1)

    l1, l2, l3 = (natural(r, d) for r, d in zip((l1_ref, l2_ref, l3_ref), DILATIONS))
    mx = jnp.maximum(jnp.maximum(l1, l2), l3)
    e1, e2, e3 = jnp.exp2(l1 - mx), jnp.exp2(l2 - mx), jnp.exp2(l3 - mx)
    zs = e1 + e2 + e3
    er = lax.broadcasted_iota(I32, (LANES, ATTN_W), 0)
    ec = lax.broadcasted_iota(I32, (LANES, ATTN_W), 1)
    expand = jnp.where(ec // ATTN_HD == er, 1.0, 0.0).astype(BF16)
    attn = jnp.zeros((tm, ATTN_W), F32)
    for e, o_ref, d in zip((e1, e2, e3), (o1_ref, o2_ref, o3_ref), DILATIONS):
        wgt = jnp.dot((e / zs).astype(BF16), expand, preferred_element_type=F32)
        attn = attn + wgt * natural(o_ref, d)
    mix = (jnp.dot(attn.astype(BF16), wo_ref[0:ATTN_W, :], preferred_element_type=F32)
           + jnp.dot(dn_ref[...], wo_ref[ATTN_W:, :], preferred_element_type=F32))
    x1 = x_ref[...] + mod_ref[0, 2:3, :] * mix
    x1_ref[...] = x1
    ms = jnp.mean(x1 * x1, axis=-1, keepdims=True)
    h2 = x1 * lax.rsqrt(ms + EPS) * n2_ref[...]
    h2 = h2 * (1.0 + mod_ref[0, 4:5, :]) + mod_ref[0, 3:4, :]
    _store_slabs(h2_ref, h2)

    def split(a):
        hi = a.astype(BF16)
        return hi, (a - hi.astype(F32)).astype(BF16)

    h_hi, h_lo = split(h2)
    w_hi, w_lo = split(wr_ref[...])
    lg = _nt(w_hi, h_hi) + (_nt(w_hi, h_lo) + _nt(w_lo, h_hi)) + br_ref[...]
    eidx = lax.broadcasted_iota(I32, (N_EXPERTS, tm), 0)
    vals, idxs, sels = [], [], []
    for _ in range(TOP_K):
        m = jnp.max(lg, axis=0, keepdims=True)
        idx = jnp.min(jnp.where(lg == m, eidx, N_EXPERTS), axis=0, keepdims=True)
        sel = eidx == idx
        vals.append(m)
        idxs.append(idx)
        sels.append(sel)
        lg = jnp.where(sel, -jnp.inf, lg)
    ex = [jnp.exp(v - vals[0]) for v in vals]
    den = ex[0] + ex[1] + ex[2] + ex[3]
    gates = [e / den for e in ex]

    msum = jnp.zeros((N_EXPERTS, tm), F32)
    for sel in sels:
        msum = msum + jnp.where(sel, 1.0, 0.0)
    tr = lax.broadcasted_iota(I32, (tm, tm), 0)
    tc = lax.broadcasted_iota(I32, (tm, tm), 1)
    upper = jnp.where(tr <= tc, 1.0, 0.0).astype(BF16)
    incl = jnp.dot(msum.astype(BF16), upper, preferred_element_type=F32)
    pos = base[:, 0:1] + (incl - msum)
    sub8 = lax.broadcasted_iota(I32, (8, tm), 0)
    te = jnp.zeros((8, tm), I32)
    rk = jnp.zeros((8, tm), I32)
    gt = jnp.zeros((8, tm), F32)
    for k in range(TOP_K):
        rank_k = jnp.sum(jnp.where(sels[k], pos, 0.0), axis=0, keepdims=True).astype(I32)
        te = jnp.where(sub8 == k, idxs[k], te)
        rk = jnp.where(sub8 == k, rank_k, rk)
        gt = jnp.where(sub8 == k, gates[k], gt)
    te_ref[...] = te
    rk_ref[...] = rk
    gtc_ref[...] = jnp.transpose(jnp.concatenate([gt, jnp.zeros((LANES - 8, tm), F32)], axis=0))
    base[...] = base[...] + jnp.sum(msum, axis=1, keepdims=True)
    cnt_ref[...] = base[...].astype(I32)


def _outproj(o1, o2, o3, l1, l2, l3, dn, x2, mod, w_out, norm2_w, w_router, b_router, S):
    T, D = x2.shape
    tm = PROJ_ROWS
    nt = S // tm
    row = lambda w: pl.BlockSpec((tm, w), lambda i: (i, 0))
    res = lambda d, w: pl.BlockSpec((1, d, tm // d, w), lambda i: (i // nt, 0, i % nt, 0))
    colb = pl.BlockSpec((8, tm), lambda i: (0, i))
    return pl.pallas_call(
        functools.partial(_out_kernel, tm=tm),
        grid=(T // tm,),
        in_specs=[res(d, ATTN_W) for d in DILATIONS] + [res(d, LANES) for d in DILATIONS]
        + [row(DN_W), row(D),
                  pl.BlockSpec((1, 6, D), lambda i: (i * tm // S, 0, 0)),
                  pl.BlockSpec((D, D), lambda i: (0, 0)),
                  pl.BlockSpec((1, D), lambda i: (0, 0)),
                  pl.BlockSpec((N_EXPERTS, D), lambda i: (0, 0)),
                  pl.BlockSpec((N_EXPERTS, 1), lambda i: (0, 0))],
        out_specs=[row(D), pl.BlockSpec((tm * (D // LANES), LANES), lambda i: (i, 0)), colb, colb,
                   row(LANES), pl.BlockSpec((N_EXPERTS, LANES), lambda i: (0, 0))],
        out_shape=[jax.ShapeDtypeStruct((T, D), F32),
                   jax.ShapeDtypeStruct((T * (D // LANES), LANES), F32),
                   jax.ShapeDtypeStruct((8, T), I32),
                   jax.ShapeDtypeStruct((8, T), I32),
                   jax.ShapeDtypeStruct((T, LANES), F32),
                   jax.ShapeDtypeStruct((N_EXPERTS, LANES), I32)],
        scratch_shapes=[pltpu.VMEM((N_EXPERTS, LANES), F32),
                        pltpu.VMEM((ATTN_W // LANES, tm, LANES), F32)],
        compiler_params=_cparams(("arbitrary",)),
        name="outproj_router",
    )(o1, o2, o3, l1, l2, l3, dn, x2, mod, w_out.astype(BF16), norm2_w.reshape(1, D),
      jnp.transpose(w_router), b_router.reshape(N_EXPERTS, 1))


def _dest_kernel(ps_ref, te_ref, rk_ref, d_ref):
    te = te_ref[...]
    acc = jnp.zeros(te.shape, I32)
    for e in range(N_EXPERTS):
        acc = jnp.where(te == e, ps_ref[e], acc)
    d_ref[...] = acc + rk_ref[...]


def _dest(pstart, te, rk):
    T = te.shape[1]
    tb = DEST_COLS
    return pl.pallas_call(
        _dest_kernel,
        grid_spec=pltpu.PrefetchScalarGridSpec(
            num_scalar_prefetch=1,
            grid=(T // tb,),
            in_specs=[pl.BlockSpec((8, tb), lambda i, ps: (0, i)),
                      pl.BlockSpec((8, tb), lambda i, ps: (0, i))],
            out_specs=pl.BlockSpec((8, tb), lambda i, ps: (0, i))),
        out_shape=jax.ShapeDtypeStruct((8, T), I32),
        compiler_params=_cparams(("arbitrary",)),
        name="dest_rows",
    )(pstart, te, rk)


SC_CORES = 2
SC_SUBCORES = 16
SC_IDX_CHUNK = 128


def _invperm(dest_flat, P):
    N = dest_flat.shape[0]
    nch = N // (SC_SUBCORES * SC_IDX_CHUNK)
    half = P // SC_CORES
    per_out = half // SC_SUBCORES
    assert N % (SC_SUBCORES * SC_IDX_CHUNK) == 0 and P % (SC_CORES * SC_SUBCORES * 8) == 0
    mesh = plsc.VectorSubcoreMesh(core_axis_name="c", subcore_axis_name="s",
                                  num_cores=SC_CORES, num_subcores=SC_SUBCORES)

    @functools.partial(
        pl.kernel, mesh=mesh, out_type=jax.ShapeDtypeStruct((P,), I32),
        scratch_types=[pltpu.VMEM((nch, SC_IDX_CHUNK), I32), pltpu.VMEM((nch, SC_IDX_CHUNK), I32),
                       pltpu.VMEM_SHARED((P,), I32), pltpu.VMEM((per_out,), I32), pltpu.SemaphoreType.DMA])
    def scatter_codes(idx_hbm, val_hbm, out_hbm, idx_v, val_v, table, stage, sem):
        core = lax.axis_index("c")
        sub = lax.axis_index("s")
        pltpu.sync_copy(idx_hbm.at[sub], idx_v)
        pltpu.sync_copy(val_hbm.at[sub], val_v)

        @pl.loop(0, nch)
        def _(j):
            pltpu.async_copy(val_v.at[j], table.at[idx_v.at[j]], sem)

        @pl.loop(0, nch)
        def _(j):
            pltpu.make_async_copy(val_v.at[j], table.at[idx_v.at[j]], sem).wait()

        plsc.subcore_barrier()
        off = pl.multiple_of(core * half + sub * per_out, 8)
        pltpu.sync_copy(table.at[pl.ds(off, per_out)], stage)
        pltpu.sync_copy(stage, out_hbm.at[pl.ds(off, per_out)])

    vals = jnp.arange(N, dtype=I32)
    shape = (SC_SUBCORES, nch, SC_IDX_CHUNK)
    return scatter_codes(dest_flat.reshape(shape), vals.reshape(shape))


def _row_copy(src, dst, sem):
    return pltpu.make_async_copy(src, dst, sem)


def _moe_kernel(be_ref, nv_ref, cc_ref, cn_ref, h2_ref, w1_ref, b1_ref, w2_ref, b2_ref, y4_ref,
                xbuf, ybuf, w1b, w2b, gsem, ssem, *, F, T, D, tme, nb):
    i = pl.program_id(0)
    s = i % 2
    ns = D // LANES

    def rows(first, n):
        return pl.ds(pl.multiple_of(first * ns, ns), n * ns)

    def gather_copy(tok, p, slot):
        return _row_copy(h2_ref.at[rows(tok, 1)], xbuf.at[slot, rows(p, 1)], gsem.at[slot])

    def token_of(code):
        return code & (T - 1) if T & (T - 1) == 0 else code % T

    def issue_rows(start_row, nv):
        ng = nv // ROW_UNROLL

        def grp(g, c):
            for j in range(ROW_UNROLL):
                start_row(g * ROW_UNROLL + j, j % 2)
            return c
        lax.fori_loop(0, ng, grp, 0)

        def one(p, c):
            start_row(p, 0)
            return c
        lax.fori_loop(ng * ROW_UNROLL, nv, one, 0)

    def start_gather(code_ref, nv, slot):
        issue_rows(lambda p, pri: gather_copy(token_of(code_ref[0, 0, p]), p, slot).start(priority=pri), nv)

    def wait_rows(make, nv):
        @pl.when(nv > 0)
        def _():
            make(nv).wait()

    def gathered(n, slot):
        return _row_copy(h2_ref.at[rows(0, n)], xbuf.at[slot, rows(0, n)], gsem.at[slot])

    def scattered(n, slot):
        return _row_copy(ybuf.at[slot, rows(0, n)], y4_ref.at[rows(0, n)], ssem.at[slot])

    @pl.when(i == 0)
    def _():
        start_gather(cc_ref, tme, 0)

    nv = nv_ref[i]
    gathered(tme, s).wait()

    @pl.when(i >= 2)
    def _():
        wait_rows(lambda n: scattered(n, s), nv_ref[i - 2])

    @pl.when(jnp.logical_or(i == 0, be_ref[i] != be_ref[jnp.maximum(i - 1, 0)]))
    def _():
        w1b[...] = w1_ref[0].astype(BF16)
        w2b[...] = w2_ref[0].astype(BF16)

    def compute(slot):
        x = _load_slabs(xbuf, tme, D, lead=(slot,)).astype(BF16)
        for p in range(tme):
            gather_copy(token_of(cn_ref[0, 0, p]), p, 1 - slot).start(priority=p % 2)
        hgu = jnp.dot(x, w1b[...], preferred_element_type=F32) + b1_ref[0]
        gate = jnp.minimum(hgu[:, :F], SWIGLU_LIMIT)
        up = jnp.clip(hgu[:, F:], -SWIGLU_LIMIT, SWIGLU_LIMIT)
        act = gate * _sigmoid(SWIGLU_ALPHA * gate) * (up + 1.0)
        y = jnp.dot(act.astype(BF16), w2b[...], preferred_element_type=F32) + b2_ref[0]
        _store_slabs(ybuf, y, lead=(slot,))

        issue_rows(lambda p, pri: _row_copy(ybuf.at[slot, rows(p, 1)], y4_ref.at[rows(cc_ref[0, 0, p], 1)],
                                            ssem.at[slot]).start(priority=pri), nv)

    for slot in range(2):
        pl.when(jnp.logical_and(nv > 0, s == slot))(functools.partial(compute, slot))

    @pl.when(jnp.logical_and(nv == 0, i + 1 < nb))
    def _():
        gathered(tme, 1 - s).start()

    @pl.when(i == nb - 1)
    def _():
        @pl.when(nv > 0)
        def _():
            gathered(tme, 1 - s).wait()
        wait_rows(lambda n: scattered(n, s), nv)
        if nb > 1:
            wait_rows(lambda n: scattered(n, 1 - s), nv_ref[i - 1])


def _experts(blk_exp, blk_valid, codes, h2s, w1, b1, w2, b2, tme):
    E, D, F2 = w1.shape
    F = F2 // 2
    ns = D // LANES
    T = h2s.shape[0] // ns
    nb = blk_exp.shape[0]
    codes3 = codes.reshape(nb, 1, tme)
    wspec = lambda shape: pl.BlockSpec(shape, lambda i, be, nv: (be[i], 0, 0))
    cspec = lambda off: pl.BlockSpec((1, 1, tme), lambda i, be, nv: (jnp.minimum(i + off, nb - 1), 0, 0),
                                     memory_space=pltpu.SMEM)
    return pl.pallas_call(
        functools.partial(_moe_kernel, F=F, T=T, D=D, tme=tme, nb=nb),
        grid_spec=pltpu.PrefetchScalarGridSpec(
            num_scalar_prefetch=2,
            grid=(nb,),
            in_specs=[cspec(0), cspec(1),
                      pl.BlockSpec(memory_space=pl.ANY),
                      wspec((1, D, F2)), wspec((1, 1, F2)), wspec((1, F, D)), wspec((1, 1, D))],
            out_specs=pl.BlockSpec(memory_space=pl.ANY),
            scratch_shapes=[pltpu.VMEM((2, tme * ns, LANES), F32), pltpu.VMEM((2, tme * ns, LANES), F32),
                            pltpu.VMEM((D, F2), BF16), pltpu.VMEM((F, D), BF16),
                            pltpu.SemaphoreType.DMA((2,)), pltpu.SemaphoreType.DMA((2,))]),
        out_shape=jax.ShapeDtypeStruct((TOP_K * T * ns, LANES), F32),
        compiler_params=_cparams(("arbitrary",)),
        name="experts",
    )(blk_exp, blk_valid, codes3, codes3, h2s, w1, b1.reshape(E, 1, F2), w2, b2.reshape(E, 1, D))


def _comb_kernel(g_ref, x1_ref, mod_ref, fw_ref, y0_ref, y1_ref, y2_ref, y3_ref, o_ref):
    g = g_ref[...]
    n, d = x1_ref.shape
    y = g[:, 0:1] * _load_slabs(y0_ref, n, d)
    for k, y_ref in ((1, y1_ref), (2, y2_ref), (3, y3_ref)):
        y = y + g[:, k:k + 1] * _load_slabs(y_ref, n, d)
    x2 = x1_ref[...] + mod_ref[0, 5:6, :] * y
    ms = jnp.mean(x2 * x2, axis=-1, keepdims=True)
    o_ref[...] = x2 * lax.rsqrt(ms + EPS) * fw_ref[...]


def _combine(gtc, x1, mod, final_w, y4, S):
    T, D = x1.shape
    tmc = COMBINE_ROWS
    nt = T // tmc
    yspec = lambda k: pl.BlockSpec((tmc * (D // LANES), LANES), lambda i: (k * nt + i, 0))
    return pl.pallas_call(
        _comb_kernel,
        grid=(nt,),
        in_specs=[pl.BlockSpec((tmc, LANES), lambda i: (i, 0)),
                  pl.BlockSpec((tmc, D), lambda i: (i, 0)),
                  pl.BlockSpec((1, 6, D), lambda i: (i * tmc // S, 0, 0)),
                  pl.BlockSpec((1, D), lambda i: (0, 0)),
                  yspec(0), yspec(1), yspec(2), yspec(3)],
        out_specs=pl.BlockSpec((tmc, D), lambda i: (i, 0)),
        out_shape=jax.ShapeDtypeStruct((T, D), F32),
        compiler_params=_cparams(("arbitrary",)),
        name="combine",
    )(gtc, x1, mod, final_w.reshape(1, D), y4, y4, y4, y4)


def _layer(x2, mod, B, S, norm1_w, w_in, conv_w, A_log, dt_bias, dn_norm_w, w_out,
           norm2_w, w_router, b_router, w1, b1, w2, b2):
    T, D = x2.shape
    qkv, qkv4, qkv16, dqkv, dz, gbc = _inproj(x2, mod, norm1_w, w_in, S)
    views = (qkv.reshape(B, 1, S, 3 * ATTN_W), qkv4, qkv16)
    branches = [_attn_branch(v, d) for v, d in zip(views, DILATIONS)]
    dn = _gdn(dqkv, dz, gbc, conv_w, A_log, dt_bias, dn_norm_w, B, S)
    (o1, l1), (o2, l2), (o3, l3) = branches
    x1, h2, te, rk, gtc, cnt = _outproj(o1, o2, o3, l1, l2, l3, dn, x2, mod, w_out, norm2_w,
                                        w_router, b_router, S)
    tme = MOE_ROWS
    P = T * TOP_K + N_EXPERTS * tme
    counts = cnt[:, 0]
    padded = (counts + tme - 1) // tme * tme
    pend = jnp.cumsum(padded)
    pstart = (pend - padded).astype(I32)
    blk_start = jnp.arange(P // tme, dtype=I32) * tme
    blk_exp = jnp.minimum(jnp.sum((pend[None, :] <= blk_start[:, None]).astype(I32), axis=1),
                          N_EXPERTS - 1).astype(I32)
    blk_valid = jnp.clip(pstart[blk_exp] + counts[blk_exp] - blk_start, 0, tme).astype(I32)
    dest = _dest(pstart, te, rk)
    codes = _invperm(dest[:TOP_K].reshape(TOP_K * T), P)
    y4 = _experts(blk_exp, blk_valid, codes, h2, w1, b1, w2, b2, tme)
    return x1, gtc, y4


def kernel(x, c, w_ada, b_ada, norm1_w, w_in, conv_w, A_log, dt_bias, dn_norm_w, w_out, norm2_w,
           w_router, b_router, w1, b1, w2, b2, final_norm_w):
    B, S, D = x.shape
    depth = w_ada.shape[0]
    assert depth == 1, "one layer: the final norm is fused into the combine step"
    assert S % (ATTN_BLK * DILATIONS[-1]) == 0 and S % max(PROJ_ROWS, GDN_ROWS, COMBINE_ROWS) == 0
    assert (B * S) % DEST_COLS == 0 and w_ada.shape[2] % ADA_COLS == 0
    x2 = x.reshape(B * S, D)
    mod = _ada(c, w_ada[0], b_ada[0])
    x1, gtc, y4 = _layer(x2, mod, B, S, norm1_w[0], w_in[0], conv_w[0], A_log[0], dt_bias[0],
                         dn_norm_w[0], w_out[0], norm2_w[0], w_router[0], b_router[0],
                         w1[0], b1[0], w2[0], b2[0])
    out = _combine(gtc, x1, mod, final_norm_w, y4, S)
    return out.reshape(B, S, D)
```

```python
import functools

import jax
import jax.numpy as jnp
from jax import lax
from jax.experimental import pallas as pl
from jax.experimental.pallas import tpu as pltpu
from jax.experimental.pallas import tpu_sc as plsc

F32 = jnp.float32
BF16 = jnp.bfloat16
I32 = jnp.int32
HI = lax.Precision.HIGHEST

LANES = 128
ATTN_HEADS = 8
ATTN_HD = 64
ATTN_W = ATTN_HEADS * ATTN_HD
ATTN_BLK = 128
DILATIONS = (1, 4, 16)
DN_HEADS = 4
DN_D = 128
DN_W = DN_HEADS * DN_D
DN_CONV = 3 * DN_W
CONV_K = 4
DN_CHUNK = 64
N_EXPERTS = 32
TOP_K = 4
MOE_ROWS = 512
ROW_UNROLL = 16
SWIGLU_LIMIT = 7.0
SWIGLU_ALPHA = 1.702
EPS = 1e-6
NEG = -1e30
LOG2E = 1.4426950408889634
MAIN_COLS = 3 * ATTN_W + DN_CONV + DN_W

ADA_COLS = 1024
PROJ_ROWS = 1024
ATTN_ROWS = 1024
GDN_ROWS = 1024
DEST_COLS = 2048
COMBINE_ROWS = 1024

V7X_VMEM_BYTES = 64 * 1024 * 1024
VMEM_LIMIT = V7X_VMEM_BYTES - 8 * 1024 * 1024


def _cparams(sem):
    return pltpu.CompilerParams(dimension_semantics=sem, vmem_limit_bytes=VMEM_LIMIT)


def _nt(a, b, **kw):
    return lax.dot_general(a, b, (((1,), (1,)), ((), ())), preferred_element_type=F32, **kw)


def _sigmoid(x):
    return 1.0 / (1.0 + jnp.exp(-x))


def _store_slabs(ref, val, lead=()):
    n, d = val.shape
    ns = d // LANES
    for c in range(ns):
        ref[lead + (pl.ds(c, n, stride=ns), slice(None))] = val[:, c * LANES:(c + 1) * LANES]


def _load_slabs(ref, n, d, lead=()):
    ns = d // LANES
    return jnp.concatenate([ref[lead + (pl.ds(c, n, stride=ns), slice(None))] for c in range(ns)], axis=1)


def _ada_kernel(c_ref, w_ref, b_ref, o_ref):
    c = c_ref[...]
    cond = c * _sigmoid(c)
    o_ref[...] = jnp.dot(cond, w_ref[...], preferred_element_type=F32, precision=HI) + b_ref[...]


def _ada(c, w_ada, b_ada):
    B, D = c.shape
    N = w_ada.shape[1]
    cp = jnp.zeros((8, D), F32).at[:B].set(c)
    tn = ADA_COLS
    out = pl.pallas_call(
        _ada_kernel,
        grid=(N // tn,),
        in_specs=[pl.BlockSpec((8, D), lambda j: (0, 0)),
                  pl.BlockSpec((D, tn), lambda j: (0, j)),
                  pl.BlockSpec((1, tn), lambda j: (0, j))],
        out_specs=pl.BlockSpec((8, tn), lambda j: (0, j)),
        out_shape=jax.ShapeDtypeStruct((8, N), F32),
        compiler_params=_cparams(("arbitrary",)),
        name="ada",
    )(cp, w_ada, b_ada.reshape(1, N))
    return out[:B].reshape(B, 6, D)


def _inproj_kernel(x_ref, mod_ref, nw_ref, wm_ref, ws_ref, qkv_ref, qkv4_ref, qkv16_ref, dqkv_ref, dz_ref,
                   gbc_ref, scr, scr2):
    x = x_ref[...]
    tm = x.shape[0]
    shift = mod_ref[0, 0:1, :]
    scale = mod_ref[0, 1:2, :]
    ms = jnp.mean(x * x, axis=-1, keepdims=True)
    h = x * lax.rsqrt(ms + EPS) * nw_ref[...]
    hb = (h * (1.0 + scale) + shift).astype(BF16)
    nl = ATTN_W // LANES
    proj = [jnp.dot(hb, wm_ref[:, c0:c0 + ATTN_W], preferred_element_type=F32)
            for c0 in range(0, MAIN_COLS, ATTN_W)]
    for j in range(3):
        cs = slice(j * ATTN_W, (j + 1) * ATTN_W)
        r = proj[j]
        if j == 0:
            r = r * (ATTN_HD ** -0.5 * LOG2E)
        qkv_ref[:, cs] = r.astype(BF16)
        d1, d2 = DILATIONS[1], DILATIONS[2]
        step = d2 // d1
        for c in range(nl):
            scr[c] = r[:, c * LANES:(c + 1) * LANES]
        for res in range(d1):
            for c in range(nl):
                scr2[c, res * (tm // d1):(res + 1) * (tm // d1), :] = scr[c, pl.ds(res, tm // d1, stride=d1), :]
            part = jnp.concatenate([scr2[c, res * (tm // d1):(res + 1) * (tm // d1), :] for c in range(nl)],
                                   axis=1)
            qkv4_ref[0, res, :, cs] = part.astype(BF16)
        for res in range(d2):
            r1, q = res % d1, res // d1
            part = jnp.concatenate(
                [scr2[c, pl.ds(r1 * (tm // d1) + q, tm // d2, stride=step), :] for c in range(nl)], axis=1)
            qkv16_ref[0, res, :, cs] = part.astype(BF16)
    for j in range(3):
        dqkv_ref[:, j * DN_W:(j + 1) * DN_W] = proj[3 + j].astype(BF16)
    dz_ref[...] = proj[6].astype(BF16)
    gbc_ref[...] = jnp.dot(hb, ws_ref[...], preferred_element_type=F32)


def _inproj(x2, mod, norm_w, w_in, S):
    T, D = x2.shape
    tm = PROJ_ROWS
    B, nt = T // S, S // tm
    wm = w_in[:, :MAIN_COLS].astype(BF16)
    ws = jnp.zeros((D, LANES), F32).at[:, :2 * DN_HEADS].set(w_in[:, MAIN_COLS:]).astype(BF16)
    return pl.pallas_call(
        _inproj_kernel,
        grid=(T // tm,),
        in_specs=[pl.BlockSpec((tm, D), lambda i: (i, 0)),
                  pl.BlockSpec((1, 6, D), lambda i: (i * tm // S, 0, 0)),
                  pl.BlockSpec((1, D), lambda i: (0, 0)),
                  pl.BlockSpec((D, MAIN_COLS), lambda i: (0, 0)),
                  pl.BlockSpec((D, LANES), lambda i: (0, 0))],
        out_specs=[pl.BlockSpec((tm, 3 * ATTN_W), lambda i: (i, 0))]
        + [pl.BlockSpec((1, d, tm // d, 3 * ATTN_W), lambda i: (i // nt, 0, i % nt, 0)) for d in DILATIONS[1:]]
        + [pl.BlockSpec((tm, DN_CONV), lambda i: (i, 0)),
           pl.BlockSpec((tm, DN_W), lambda i: (i, 0)),
           pl.BlockSpec((tm, LANES), lambda i: (i, 0))],
        out_shape=[jax.ShapeDtypeStruct((T, 3 * ATTN_W), BF16)]
        + [jax.ShapeDtypeStruct((B, d, S // d, 3 * ATTN_W), BF16) for d in DILATIONS[1:]]
        + [jax.ShapeDtypeStruct((T, DN_CONV), BF16),
           jax.ShapeDtypeStruct((T, DN_W), BF16),
           jax.ShapeDtypeStruct((T, LANES), F32)],
        scratch_shapes=[pltpu.VMEM((ATTN_W // LANES, tm, LANES), F32),
                        pltpu.VMEM((ATTN_W // LANES, tm, LANES), F32)],
        compiler_params=_cparams(("arbitrary",)),
        name="inproj",
    )(x2, mod, norm_w.reshape(1, D), wm, ws)


def _attn_kernel(q_ref, kc_ref, kp_ref, vc_ref, vp_ref, o_ref, lse_ref, kf, vf, *, qb):
    n = pl.program_id(2)
    kf[0:ATTN_BLK, :] = kp_ref[0, 0]
    kf[ATTN_BLK:, :] = kc_ref[0, 0]
    vf[0:ATTN_BLK, :] = vp_ref[0, 0]
    vf[ATTN_BLK:, :] = vc_ref[0, 0]
    row = lax.broadcasted_iota(I32, (ATTN_BLK, 2 * ATTN_BLK), 0)
    col = lax.broadcasted_iota(I32, (ATTN_BLK, 2 * ATTN_BLK), 1)
    band = jnp.logical_or(jnp.logical_and(col < ATTN_BLK, col >= row),
                          jnp.logical_and(col >= ATTN_BLK, col - ATTN_BLK <= row))
    lane = lax.broadcasted_iota(I32, (ATTN_BLK, LANES), 1)
    lo = lane < ATTN_HD

    def sub(j, carry):
        r0 = pl.multiple_of(j * ATTN_BLK, ATTN_BLK)
        first_col = jnp.where(jnp.logical_and(n == 0, j == 0), ATTN_BLK, 0)
        mask = jnp.logical_and(band, col >= first_col)
        npair = ATTN_W // LANES
        cols = [slice(hp * LANES, (hp + 1) * LANES) for hp in range(npair)]
        heads = [(hp, half) for hp in range(npair) for half in range(2)]
        scores = []
        for hp, half in heads:
            q2 = q_ref[0, 0, pl.ds(r0, ATTN_BLK), cols[hp]]
            qm = jnp.where(lo if half == 0 else jnp.logical_not(lo), q2, jnp.zeros_like(q2))
            scores.append(jnp.where(mask, _nt(qm, kf[pl.ds(r0, 2 * ATTN_BLK), cols[hp]]), NEG))
        maxes = [jnp.max(s, axis=-1, keepdims=True) for s in scores]
        probs = [jnp.exp2(s - m) for s, m in zip(scores, maxes)]
        dens = [jnp.sum(p, axis=-1, keepdims=True) for p in probs]
        accs = [jnp.dot(p.astype(BF16), vf[pl.ds(r0, 2 * ATTN_BLK), cols[hp]], preferred_element_type=F32)
                for p, (hp, _) in zip(probs, heads)]
        outs = [a / d for a, d in zip(accs, dens)]
        lse_tile = jnp.zeros((ATTN_BLK, LANES), F32)
        for h, (m, d) in enumerate(zip(maxes, dens)):
            lse_tile = jnp.where(lane == h, m + jnp.log2(d), lse_tile)
        for hp in range(npair):
            o_ref[0, 0, pl.ds(r0, ATTN_BLK), cols[hp]] = jnp.where(lo, outs[2 * hp], outs[2 * hp + 1]).astype(BF16)
        lse_ref[0, 0, pl.ds(r0, ATTN_BLK), :] = lse_tile
        return carry

    lax.fori_loop(0, qb // ATTN_BLK, sub, 0, unroll=True)


def _attn_branch(qkv, d):
    B, _, L, _ = qkv.shape
    qb = min(ATTN_ROWS, L)
    nsub = qb // ATTN_BLK
    cur = lambda c: pl.BlockSpec((1, 1, qb, ATTN_W), lambda b, r, n: (b, r, n, c))
    prev = lambda c: pl.BlockSpec((1, 1, ATTN_BLK, ATTN_W),
                                  lambda b, r, n: (b, r, jnp.maximum(n * nsub - 1, 0), c))
    return pl.pallas_call(
        functools.partial(_attn_kernel, qb=qb),
        grid=(B, d, L // qb),
        in_specs=[cur(0), cur(1), prev(1), cur(2), prev(2)],
        out_specs=[pl.BlockSpec((1, 1, qb, ATTN_W), lambda b, r, n: (b, r, n, 0)),
                   pl.BlockSpec((1, 1, qb, LANES), lambda b, r, n: (b, r, n, 0))],
        out_shape=[jax.ShapeDtypeStruct((B, d, L, ATTN_W), BF16),
                   jax.ShapeDtypeStruct((B, d, L, LANES), F32)],
        scratch_shapes=[pltpu.VMEM((qb + ATTN_BLK, ATTN_W), BF16),
                        pltpu.VMEM((qb + ATTN_BLK, ATTN_W), BF16)],
        compiler_params=_cparams(("arbitrary", "arbitrary", "arbitrary")),
        name=f"attn_d{d}",
    )(qkv, qkv, qkv, qkv, qkv)


def _gdn_kernel(x_ref, z_ref, g_ref, cw_ref, prm_ref, nw_ref, o_ref, xext, yc, s0, s1, *, rb, nseq):
    i = pl.program_id(1)

    @pl.when(i == 0)
    def _():
        xext[:, 0:8, :] = jnp.zeros((nseq, 8, DN_CONV), F32)
        s0[...] = jnp.zeros_like(s0)
        s1[...] = jnp.zeros_like(s1)

    @pl.when(i > 0)
    def _():
        xext[:, 0:8, :] = xext[:, rb:rb + 8, :]

    for q in range(nseq):
        xext[q, 8:, :] = x_ref[q].astype(F32)
        y = cw_ref[CONV_K - 1:CONV_K, :] * xext[q, 8:8 + rb, :]
        for j in range(CONV_K - 1):
            off = 8 - (CONV_K - 1) + j
            y = y + cw_ref[j:j + 1, :] * xext[q, off:off + rb, :]
        yc[q] = y * _sigmoid(y)

    C = DN_CHUNK
    H = DN_HEADS
    CW = H * C
    dot = functools.partial(jnp.dot, preferred_element_type=F32)

    def iota(shape, d):
        return lax.broadcasted_iota(I32, shape, d)

    ltri_b = jnp.where(iota((C, C), 0) >= iota((C, C), 1), 1.0, 0.0).astype(BF16)
    lane = iota((C, LANES), 1)
    blane = lane < H
    glane = jnp.logical_and(lane >= H, lane < 2 * H)
    e512 = jnp.where(jnp.logical_or(iota((LANES, DN_W), 1) // DN_D == iota((LANES, DN_W), 0),
                                    iota((LANES, DN_W), 1) // DN_D == iota((LANES, DN_W), 0) - H),
                     1.0, 0.0).astype(BF16)
    e256 = jnp.where(iota((LANES, CW), 1) // C == iota((LANES, CW), 0) - H, 1.0, 0.0).astype(BF16)
    row4 = iota((C, CW), 0)
    col4 = iota((C, CW), 1) % C
    eye4 = jnp.where(row4 == col4, 1.0, 0.0).astype(F32)
    blk = [iota((C, CW), 1) // C == h for h in range(H)]
    bd_cc = iota((CW, CW), 0) // C == iota((CW, CW), 1) // C
    bd_pair = iota((CW, CW), 0) // DN_D == iota((CW, CW), 1) // DN_D
    rt_mask = iota((CW, DN_W), 0) // C == iota((CW, DN_W), 1) // DN_D
    neg_a = -jnp.exp(prm_ref[0:1, :])
    dtb = prm_ref[1:2, :]
    nw = nw_ref[...]

    def hilo(x):
        hi = x.astype(BF16)
        return hi, (x - hi.astype(F32)).astype(BF16)

    def heads(a, w):
        return [a[:, h * w:(h + 1) * w] for h in range(H)]

    def l2n(a, mult):
        return jnp.concatenate(
            [p * (lax.rsqrt(jnp.sum(p * p, axis=-1, keepdims=True) + EPS) * mult) for p in heads(a, DN_D)],
            axis=1)

    def stack4(a):
        return jnp.concatenate([a, a, a, a], axis=0)

    zb = jnp.zeros((), BF16)
    nchunk = rb // C
    chunks = []
    for c, sq in ((c, sq) for c in range(nchunk) for sq in range(nseq)):
        rs = slice(c * C, (c + 1) * C)
        G = g_ref[sq, rs, :]
        xg = G + dtb
        gv = jnp.where(glane, neg_a * (jnp.maximum(xg, 0.0) + jnp.log1p(jnp.exp(-jnp.abs(xg)))), 0.0)
        be = jnp.where(blane, _sigmoid(G), 0.0)
        g_hi, g_lo = hilo(gv)
        gcum = dot(ltri_b, g_hi) + dot(ltri_b, g_lo)
        gtot = gcum[C - 1:C, :]
        eg = jnp.where(glane, jnp.exp(gcum), 0.0)
        ek = jnp.where(glane, jnp.exp(gtot - gcum), 0.0)
        ex = dot(jnp.concatenate([be, eg, ek], axis=0).astype(BF16), e512)
        bexp, egexp, ekexp = ex[0:C], ex[C:2 * C], ex[2 * C:3 * C]
        gexp = dot(g_hi, e256) + dot(g_lo, e256)
        d_hi, d_lo = hilo(jnp.where(row4 > col4, gexp, 0.0))
        diff = dot(ltri_b, d_hi) + dot(ltri_b, d_lo)
        decay = jnp.exp(jnp.where(row4 >= col4, diff, NEG))

        q4 = l2n(yc[sq, rs, 0:DN_W], DN_D ** -0.5)
        k4 = l2n(yc[sq, rs, DN_W:2 * DN_W], 1.0)
        v4 = yc[sq, rs, 2 * DN_W:3 * DN_W]
        kb4 = k4 * bexp
        vb4 = v4 * bexp
        rt = jnp.where(rt_mask, stack4(k4.astype(BF16)), jnp.zeros((), BF16))
        ai = _nt(jnp.concatenate([kb4, q4], axis=0).astype(BF16), rt)
        a4 = jnp.where(row4 > col4, ai[0:C] * decay, 0.0)
        pb = a4.astype(BF16)
        chunks.append(dict(
            sq=sq, rs=rs, q4=q4, k4=k4, kb4=kb4, vb4=vb4, egexp=egexp, ekexp=ekexp,
            intra=ai[C:2 * C] * decay, t4=eye4 - a4, pb=pb,
            bd=jnp.where(bd_cc, stack4(pb), zb)))

    for _ in range(5):
        for ch in chunks:
            ch["pb"] = dot(ch["pb"], ch["bd"]).astype(BF16)
        for ch in chunks:
            ch["bd"] = jnp.where(bd_cc, stack4(ch["pb"]), zb)
        for ch in chunks:
            ch["t4"] = ch["t4"] + dot(ch["t4"].astype(BF16), ch["bd"])

    for ch in chunks:
        q4, k4, kb4, vb4, egexp, ekexp = (ch[n] for n in ("q4", "k4", "kb4", "vb4", "egexp", "ekexp"))
        t4b = ch["t4"].astype(BF16)
        lstk = jnp.concatenate([jnp.where(blk[h], t4b, zb) for h in range(H)], axis=0)
        kbg4 = kb4 * egexp
        rstk = jnp.concatenate(
            [jnp.concatenate([vb, kbg], axis=1) for vb, kbg in zip(heads(vb4, DN_D), heads(kbg4, DN_D))],
            axis=0).astype(BF16)
        uw = dot(lstk, rstk)
        u4 = jnp.concatenate([uw[h * C:(h + 1) * C, 0:DN_D] for h in range(H)], axis=1)
        w4 = jnp.concatenate([uw[h * C:(h + 1) * C, DN_D:2 * DN_D] for h in range(H)], axis=1)
        ib = ch["intra"].astype(BF16)
        ch.update(
            u4=u4, wq=jnp.concatenate([w4, q4 * egexp], axis=0).astype(BF16),
            kd4=(k4 * ekexp).astype(BF16), gl4=egexp[C - 1:C, :],
            lint=jnp.concatenate([jnp.where(blk[h], ib, zb) for h in range(H)], axis=0))

    for ch in chunks:
        sq, rs, u4, wq, kd4, gl4, lint = (ch[n] for n in ("sq", "rs", "u4", "wq", "kd4", "gl4", "lint"))
        ra = dot(wq[:, 0:CW], s0[sq].astype(BF16))
        rc = dot(wq[:, CW:2 * CW], s1[sq].astype(BF16))
        vn = u4 - jnp.concatenate([ra[0:C], rc[0:C]], axis=1)
        vnb = vn.astype(BF16)
        oi = dot(lint, jnp.concatenate(heads(vnb, DN_D), axis=0))
        o = (jnp.concatenate([ra[C:2 * C], rc[C:2 * C]], axis=1)
             + jnp.concatenate([oi[h * C:(h + 1) * C] for h in range(H)], axis=1))
        tn = (((0,), (0,)), ((), ()))
        s0[sq] = s0[sq] * gl4[:, 0:CW] + jnp.where(
            bd_pair, lax.dot_general(kd4[:, 0:CW], vnb[:, 0:CW], tn, preferred_element_type=F32), 0.0)
        s1[sq] = s1[sq] * gl4[:, CW:2 * CW] + jnp.where(
            bd_pair, lax.dot_general(kd4[:, CW:2 * CW], vnb[:, CW:2 * CW], tn, preferred_element_type=F32), 0.0)

        z = z_ref[sq, rs, :].astype(F32)
        on = jnp.concatenate(
            [p * lax.rsqrt(jnp.mean(p * p, axis=-1, keepdims=True) + EPS) * nw for p in heads(o, DN_D)], axis=1)
        o_ref[sq, rs, :] = (on * (z * _sigmoid(z))).astype(BF16)


def _gdn(dqkv, dz, gbc, conv_w, A_log, dt_bias, dn_norm_w, B, S):
    nseq = 2 if B % 2 == 0 else 1
    rb = GDN_ROWS // nseq
    prm = jnp.zeros((2, LANES), F32)
    prm = prm.at[0, DN_HEADS:2 * DN_HEADS].set(A_log.astype(F32))
    prm = prm.at[1, DN_HEADS:2 * DN_HEADS].set(dt_bias.astype(F32))
    out = pl.pallas_call(
        functools.partial(_gdn_kernel, rb=rb, nseq=nseq),
        grid=(B // nseq, S // rb),
        in_specs=[pl.BlockSpec((nseq, rb, DN_CONV), lambda b, i: (b, i, 0)),
                  pl.BlockSpec((nseq, rb, DN_W), lambda b, i: (b, i, 0)),
                  pl.BlockSpec((nseq, rb, LANES), lambda b, i: (b, i, 0)),
                  pl.BlockSpec((CONV_K, DN_CONV), lambda b, i: (0, 0)),
                  pl.BlockSpec((2, LANES), lambda b, i: (0, 0)),
                  pl.BlockSpec((1, DN_D), lambda b, i: (0, 0))],
        out_specs=pl.BlockSpec((nseq, rb, DN_W), lambda b, i: (b, i, 0)),
        out_shape=jax.ShapeDtypeStruct((B, S, DN_W), BF16),
        scratch_shapes=[pltpu.VMEM((nseq, rb + 8, DN_CONV), F32),
                        pltpu.VMEM((nseq, rb, DN_CONV), F32),
                        pltpu.VMEM((nseq, 2 * DN_D, 2 * DN_D), F32),
                        pltpu.VMEM((nseq, 2 * DN_D, 2 * DN_D), F32)],
        compiler_params=_cparams(("arbitrary", "arbitrary")),
        name="gdn",
    )(dqkv.reshape(B, S, DN_CONV), dz.reshape(B, S, DN_W), gbc.reshape(B, S, LANES),
      conv_w, prm, dn_norm_w.reshape(1, DN_D))
    return out.reshape(B * S, DN_W)


def _out_kernel(o1_ref, o2_ref, o3_ref, l1_ref, l2_ref, l3_ref, dn_ref, x_ref, mod_ref, wo_ref,
                n2_ref, wr_ref, br_ref,
                x1_ref, h2_ref, te_ref, rk_ref, gtc_ref, cnt_ref, base, scr, *, tm):
    i = pl.program_id(0)

    @pl.when(i == 0)
    def _():
        base[...] = jnp.zeros_like(base)

    def natural(ref, d):
        if d == 1:
            return ref[0, 0].astype(F32)
        nl = ref.shape[-1] // LANES
        for res in range(d):
            blk = ref[0, res].astype(F32)
            for c in range(nl):
                scr[c, pl.ds(res, tm // d, stride=d), :] = blk[:, c * LANES:(c + 1) * LANES]
        return jnp.concatenate([scr[c] for c in range(nl)], axis=1)

    l1, l2, l3 = (natural(r, d) for r, d in zip((l1_ref, l2_ref, l3_ref), DILATIONS))
    mx = jnp.maximum(jnp.maximum(l1, l2), l3)
    e1, e2, e3 = jnp.exp2(l1 - mx), jnp.exp2(l2 - mx), jnp.exp2(l3 - mx)
    zs = e1 + e2 + e3
    er = lax.broadcasted_iota(I32, (LANES, ATTN_W), 0)
    ec = lax.broadcasted_iota(I32, (LANES, ATTN_W), 1)
    expand = jnp.where(ec // ATTN_HD == er, 1.0, 0.0).astype(BF16)
    attn = jnp.zeros((tm, ATTN_W), F32)
    for e, o_ref, d in zip((e1, e2, e3), (o1_ref, o2_ref, o3_ref), DILATIONS):
        wgt = jnp.dot((e / zs).astype(BF16), expand, preferred_element_type=F32)
        attn = attn + wgt * natural(o_ref, d)
    mix = (jnp.dot(attn.astype(BF16), wo_ref[0:ATTN_W, :], preferred_element_type=F32)
           + jnp.dot(dn_ref[...], wo_ref[ATTN_W:, :], preferred_element_type=F32))
    x1 = x_ref[...] + mod_ref[0, 2:3, :] * mix
    x1_ref[...] = x1
    ms = jnp.mean(x1 * x1, axis=-1, keepdims=True)
    h2 = x1 * lax.rsqrt(ms + EPS) * n2_ref[...]
    h2 = h2 * (1.0 + mod_ref[0, 4:5, :]) + mod_ref[0, 3:4, :]
    _store_slabs(h2_ref, h2)

    def split(a):
        hi = a.astype(BF16)
        return hi, (a - hi.astype(F32)).astype(BF16)

    h_hi, h_lo = split(h2)
    w_hi, w_lo = split(wr_ref[...])
    lg = _nt(w_hi, h_hi) + (_nt(w_hi, h_lo) + _nt(w_lo, h_hi)) + br_ref[...]
    eidx = lax.broadcasted_iota(I32, (N_EXPERTS, tm), 0)
    vals, idxs, sels = [], [], []
    for _ in range(TOP_K):
        m = jnp.max(lg, axis=0, keepdims=True)
        idx = jnp.min(jnp.where(lg == m, eidx, N_EXPERTS), axis=0, keepdims=True)
        sel = eidx == idx
        vals.append(m)
        idxs.append(idx)
        sels.append(sel)
        lg = jnp.where(sel, -jnp.inf, lg)
    ex = [jnp.exp(v - vals[0]) for v in vals]
    den = ex[0] + ex[1] + ex[2] + ex[3]
    gates = [e / den for e in ex]

    msum = jnp.zeros((N_EXPERTS, tm), F32)
    for sel in sels:
        msum = msum + jnp.where(sel, 1.0, 0.0)
    tr = lax.broadcasted_iota(I32, (tm, tm), 0)
    tc = lax.broadcasted_iota(I32, (tm, tm), 1)
    upper = jnp.where(tr <= tc, 1.0, 0.0).astype(BF16)
    incl = jnp.dot(msum.astype(BF16), upper, preferred_element_type=F32)
    pos = base[:, 0:1] + (incl - msum)
    sub8 = lax.broadcasted_iota(I32, (8, tm), 0)
    te = jnp.zeros((8, tm), I32)
    rk = jnp.zeros((8, tm), I32)
    gt = jnp.zeros((8, tm), F32)
    for k in range(TOP_K):
        rank_k = jnp.sum(jnp.where(sels[k], pos, 0.0), axis=0, keepdims=True).astype(I32)
        te = jnp.where(sub8 == k, idxs[k], te)
        rk = jnp.where(sub8 == k, rank_k, rk)
        gt = jnp.where(sub8 == k, gates[k], gt)
    te_ref[...] = te
    rk_ref[...] = rk
    gtc_ref[...] = jnp.transpose(jnp.concatenate([gt, jnp.zeros((LANES - 8, tm), F32)], axis=0))
    base[...] = base[...] + jnp.sum(msum, axis=1, keepdims=True)
    cnt_ref[...] = base[...].astype(I32)


def _outproj(o1, o2, o3, l1, l2, l3, dn, x2, mod, w_out, norm2_w, w_router, b_router, S):
    T, D = x2.shape
    tm = PROJ_ROWS
    nt = S // tm
    row = lambda w: pl.BlockSpec((tm, w), lambda i: (i, 0))
    res = lambda d, w: pl.BlockSpec((1, d, tm // d, w), lambda i: (i // nt, 0, i % nt, 0))
    colb = pl.BlockSpec((8, tm), lambda i: (0, i))
    return pl.pallas_call(
        functools.partial(_out_kernel, tm=tm),
        grid=(T // tm,),
        in_specs=[res(d, ATTN_W) for d in DILATIONS] + [res(d, LANES) for d in DILATIONS]
        + [row(DN_W), row(D),
                  pl.BlockSpec((1, 6, D), lambda i: (i * tm // S, 0, 0)),
                  pl.BlockSpec((D, D), lambda i: (0, 0)),
                  pl.BlockSpec((1, D), lambda i: (0, 0)),
                  pl.BlockSpec((N_EXPERTS, D), lambda i: (0, 0)),
                  pl.BlockSpec((N_EXPERTS, 1), lambda i: (0, 0))],
        out_specs=[row(D), pl.BlockSpec((tm * (D // LANES), LANES), lambda i: (i, 0)), colb, colb,
                   row(LANES), pl.BlockSpec((N_EXPERTS, LANES), lambda i: (0, 0))],
        out_shape=[jax.ShapeDtypeStruct((T, D), F32),
                   jax.ShapeDtypeStruct((T * (D // LANES), LANES), F32),
                   jax.ShapeDtypeStruct((8, T), I32),
                   jax.ShapeDtypeStruct((8, T), I32),
                   jax.ShapeDtypeStruct((T, LANES), F32),
                   jax.ShapeDtypeStruct((N_EXPERTS, LANES), I32)],
        scratch_shapes=[pltpu.VMEM((N_EXPERTS, LANES), F32),
                        pltpu.VMEM((ATTN_W // LANES, tm, LANES), F32)],
        compiler_params=_cparams(("arbitrary",)),
        name="outproj_router",
    )(o1, o2, o3, l1, l2, l3, dn, x2, mod, w_out.astype(BF16), norm2_w.reshape(1, D),
      jnp.transpose(w_router), b_router.reshape(N_EXPERTS, 1))


def _dest_kernel(ps_ref, te_ref, rk_ref, d_ref):
    te = te_ref[...]
    acc = jnp.zeros(te.shape, I32)
    for e in range(N_EXPERTS):
        acc = jnp.where(te == e, ps_ref[e], acc)
    d_ref[...] = acc + rk_ref[...]


def _dest(pstart, te, rk):
    T = te.shape[1]
    tb = DEST_COLS
    return pl.pallas_call(
        _dest_kernel,
        grid_spec=pltpu.PrefetchScalarGridSpec(
            num_scalar_prefetch=1,
            grid=(T // tb,),
            in_specs=[pl.BlockSpec((8, tb), lambda i, ps: (0, i)),
                      pl.BlockSpec((8, tb), lambda i, ps: (0, i))],
            out_specs=pl.BlockSpec((8, tb), lambda i, ps: (0, i))),
        out_shape=jax.ShapeDtypeStruct((8, T), I32),
        compiler_params=_cparams(("arbitrary",)),
        name="dest_rows",
    )(pstart, te, rk)


SC_CORES = 2
SC_SUBCORES = 16
SC_IDX_CHUNK = 128


def _invperm(dest_flat, P):
    N = dest_flat.shape[0]
    nch = N // (SC_SUBCORES * SC_IDX_CHUNK)
    half = P // SC_CORES
    per_out = half // SC_SUBCORES
    assert N % (SC_SUBCORES * SC_IDX_CHUNK) == 0 and P % (SC_CORES * SC_SUBCORES * 8) == 0
    mesh = plsc.VectorSubcoreMesh(core_axis_name="c", subcore_axis_name="s",
                                  num_cores=SC_CORES, num_subcores=SC_SUBCORES)

    @functools.partial(
        pl.kernel, mesh=mesh, out_type=jax.ShapeDtypeStruct((P,), I32),
        scratch_types=[pltpu.VMEM((nch, SC_IDX_CHUNK), I32), pltpu.VMEM((nch, SC_IDX_CHUNK), I32),
                       pltpu.VMEM_SHARED((P,), I32), pltpu.VMEM((per_out,), I32), pltpu.SemaphoreType.DMA])
    def scatter_codes(idx_hbm, val_hbm, out_hbm, idx_v, val_v, table, stage, sem):
        core = lax.axis_index("c")
        sub = lax.axis_index("s")
        pltpu.sync_copy(idx_hbm.at[sub], idx_v)
        pltpu.sync_copy(val_hbm.at[sub], val_v)

        @pl.loop(0, nch)
        def _(j):
            pltpu.async_copy(val_v.at[j], table.at[idx_v.at[j]], sem)

        @pl.loop(0, nch)
        def _(j):
            pltpu.make_async_copy(val_v.at[j], table.at[idx_v.at[j]], sem).wait()

        plsc.subcore_barrier()
        off = pl.multiple_of(core * half + sub * per_out, 8)
        pltpu.sync_copy(table.at[pl.ds(off, per_out)], stage)
        pltpu.sync_copy(stage, out_hbm.at[pl.ds(off, per_out)])

    vals = jnp.arange(N, dtype=I32)
    shape = (SC_SUBCORES, nch, SC_IDX_CHUNK)
    return scatter_codes(dest_flat.reshape(shape), vals.reshape(shape))


def _row_copy(src, dst, sem):
    return pltpu.make_async_copy(src, dst, sem)


def _moe_kernel(be_ref, nv_ref, cc_ref, cn_ref, h2_ref, w1_ref, b1_ref, w2_ref, b2_ref, y4_ref,
                xbuf, ybuf, w1b, w2b, gsem, ssem, *, F, T, D, tme, nb):
    i = pl.program_id(0)
    s = i % 2
    ns = D // LANES

    def rows(first, n):
        return pl.ds(pl.multiple_of(first * ns, ns), n * ns)

    def gather_copy(tok, p, slot):
        return _row_copy(h2_ref.at[rows(tok, 1)], xbuf.at[slot, rows(p, 1)], gsem.at[slot])

    def token_of(code):
        return code & (T - 1) if T & (T - 1) == 0 else code % T

    def issue_rows(start_row, nv):
        ng = nv // ROW_UNROLL

        def grp(g, c):
            for j in range(ROW_UNROLL):
                start_row(g * ROW_UNROLL + j, j % 2)
            return c
        lax.fori_loop(0, ng, grp, 0)

        def one(p, c):
            start_row(p, 0)
            return c
        lax.fori_loop(ng * ROW_UNROLL, nv, one, 0)

    def start_gather(code_ref, nv, slot):
        issue_rows(lambda p, pri: gather_copy(token_of(code_ref[0, 0, p]), p, slot).start(priority=pri), nv)

    def wait_rows(make, nv):
        @pl.when(nv > 0)
        def _():
            make(nv).wait()

    def gathered(n, slot):
        return _row_copy(h2_ref.at[rows(0, n)], xbuf.at[slot, rows(0, n)], gsem.at[slot])

    def scattered(n, slot):
        return _row_copy(ybuf.at[slot, rows(0, n)], y4_ref.at[rows(0, n)], ssem.at[slot])

    @pl.when(i == 0)
    def _():
        start_gather(cc_ref, tme, 0)

    nv = nv_ref[i]
    gathered(tme, s).wait()

    @pl.when(i >= 2)
    def _():
        wait_rows(lambda n: scattered(n, s), nv_ref[i - 2])

    @pl.when(jnp.logical_or(i == 0, be_ref[i] != be_ref[jnp.maximum(i - 1, 0)]))
    def _():
        w1b[...] = w1_ref[0].astype(BF16)
        w2b[...] = w2_ref[0].astype(BF16)

    def compute(slot):
        x = _load_slabs(xbuf, tme, D, lead=(slot,)).astype(BF16)
        for p in range(tme):
            gather_copy(token_of(cn_ref[0, 0, p]), p, 1 - slot).start(priority=p % 2)
        hgu = jnp.dot(x, w1b[...], preferred_element_type=F32) + b1_ref[0]
        gate = jnp.minimum(hgu[:, :F], SWIGLU_LIMIT)
        up = jnp.clip(hgu[:, F:], -SWIGLU_LIMIT, SWIGLU_LIMIT)
        act = gate * _sigmoid(SWIGLU_ALPHA * gate) * (up + 1.0)
        y = jnp.dot(act.astype(BF16), w2b[...], preferred_element_type=F32) + b2_ref[0]
        _store_slabs(ybuf, y, lead=(slot,))

        issue_rows(lambda p, pri: _row_copy(ybuf.at[slot, rows(p, 1)], y4_ref.at[rows(cc_ref[0, 0, p], 1)],
                                            ssem.at[slot]).start(priority=pri), nv)

    for slot in range(2):
        pl.when(jnp.logical_and(nv > 0, s == slot))(functools.partial(compute, slot))

    @pl.when(jnp.logical_and(nv == 0, i + 1 < nb))
    def _():
        gathered(tme, 1 - s).start()

    @pl.when(i == nb - 1)
    def _():
        @pl.when(nv > 0)
        def _():
            gathered(tme, 1 - s).wait()
        wait_rows(lambda n: scattered(n, s), nv)
        if nb > 1:
            wait_rows(lambda n: scattered(n, 1 - s), nv_ref[i - 1])


def _experts(blk_exp, blk_valid, codes, h2s, w1, b1, w2, b2, tme):
    E, D, F2 = w1.shape
    F = F2 // 2
    ns = D // LANES
    T = h2s.shape[0] // ns
    nb = blk_exp.shape[0]
    codes3 = codes.reshape(nb, 1, tme)
    wspec = lambda shape: pl.BlockSpec(shape, lambda i, be, nv: (be[i], 0, 0))
    cspec = lambda off: pl.BlockSpec((1, 1, tme), lambda i, be, nv: (jnp.minimum(i + off, nb - 1), 0, 0),
                                     memory_space=pltpu.SMEM)
    return pl.pallas_call(
        functools.partial(_moe_kernel, F=F, T=T, D=D, tme=tme, nb=nb),
        grid_spec=pltpu.PrefetchScalarGridSpec(
            num_scalar_prefetch=2,
            grid=(nb,),
            in_specs=[cspec(0), cspec(1),
                      pl.BlockSpec(memory_space=pl.ANY),
                      wspec((1, D, F2)), wspec((1, 1, F2)), wspec((1, F, D)), wspec((1, 1, D))],
            out_specs=pl.BlockSpec(memory_space=pl.ANY),
            scratch_shapes=[pltpu.VMEM((2, tme * ns, LANES), F32), pltpu.VMEM((2, tme * ns, LANES), F32),
                            pltpu.VMEM((D, F2), BF16), pltpu.VMEM((F, D), BF16),
                            pltpu.SemaphoreType.DMA((2,)), pltpu.SemaphoreType.DMA((2,))]),
        out_shape=jax.ShapeDtypeStruct((TOP_K * T * ns, LANES), F32),
        compiler_params=_cparams(("arbitrary",)),
        name="experts",
    )(blk_exp, blk_valid, codes3, codes3, h2s, w1, b1.reshape(E, 1, F2), w2, b2.reshape(E, 1, D))


def _comb_kernel(g_ref, x1_ref, mod_ref, fw_ref, y0_ref, y1_ref, y2_ref, y3_ref, o_ref):
    g = g_ref[...]
    n, d = x1_ref.shape
    y = g[:, 0:1] * _load_slabs(y0_ref, n, d)
    for k, y_ref in ((1, y1_ref), (2, y2_ref), (3, y3_ref)):
        y = y + g[:, k:k + 1] * _load_slabs(y_ref, n, d)
    x2 = x1_ref[...] + mod_ref[0, 5:6, :] * y
    ms = jnp.mean(x2 * x2, axis=-1, keepdims=True)
    o_ref[...] = x2 * lax.rsqrt(ms + EPS) * fw_ref[...]


def _combine(gtc, x1, mod, final_w, y4, S):
    T, D = x1.shape
    tmc = COMBINE_ROWS
    nt = T // tmc
    yspec = lambda k: pl.BlockSpec((tmc * (D // LANES), LANES), lambda i: (k * nt + i, 0))
    return pl.pallas_call(
        _comb_kernel,
        grid=(nt,),
        in_specs=[pl.BlockSpec((tmc, LANES), lambda i: (i, 0)),
                  pl.BlockSpec((tmc, D), lambda i: (i, 0)),
                  pl.BlockSpec((1, 6, D), lambda i: (i * tmc // S, 0, 0)),
                  pl.BlockSpec((1, D), lambda i: (0, 0)),
                  yspec(0), yspec(1), yspec(2), yspec(3)],
        out_specs=pl.BlockSpec((tmc, D), lambda i: (i, 0)),
        out_shape=jax.ShapeDtypeStruct((T, D), F32),
        compiler_params=_cparams(("arbitrary",)),
        name="combine",
    )(gtc, x1, mod, final_w.reshape(1, D), y4, y4, y4, y4)


def _layer(x2, mod, B, S, norm1_w, w_in, conv_w, A_log, dt_bias, dn_norm_w, w_out,
           norm2_w, w_router, b_router, w1, b1, w2, b2):
    T, D = x2.shape
    qkv, qkv4, qkv16, dqkv, dz, gbc = _inproj(x2, mod, norm1_w, w_in, S)
    views = (qkv.reshape(B, 1, S, 3 * ATTN_W), qkv4, qkv16)
    branches = [_attn_branch(v, d) for v, d in zip(views, DILATIONS)]
    dn = _gdn(dqkv, dz, gbc, conv_w, A_log, dt_bias, dn_norm_w, B, S)
    (o1, l1), (o2, l2), (o3, l3) = branches
    x1, h2, te, rk, gtc, cnt = _outproj(o1, o2, o3, l1, l2, l3, dn, x2, mod, w_out, norm2_w,
                                        w_router, b_router, S)
    tme = MOE_ROWS
    P = T * TOP_K + N_EXPERTS * tme
    counts = cnt[:, 0]
    padded = (counts + tme - 1) // tme * tme
    pend = jnp.cumsum(padded)
    pstart = (pend - padded).astype(I32)
    blk_start = jnp.arange(P // tme, dtype=I32) * tme
    blk_exp = jnp.minimum(jnp.sum((pend[None, :] <= blk_start[:, None]).astype(I32), axis=1),
                          N_EXPERTS - 1).astype(I32)
    blk_valid = jnp.clip(pstart[blk_exp] + counts[blk_exp] - blk_start, 0, tme).astype(I32)
    dest = _dest(pstart, te, rk)
    codes = _invperm(dest[:TOP_K].reshape(TOP_K * T), P)
    y4 = _experts(blk_exp, blk_valid, codes, h2, w1, b1, w2, b2, tme)
    return x1, gtc, y4


def kernel(x, c, w_ada, b_ada, norm1_w, w_in, conv_w, A_log, dt_bias, dn_norm_w, w_out, norm2_w,
           w_router, b_router, w1, b1, w2, b2, final_norm_w):
    B, S, D = x.shape
    depth = w_ada.shape[0]
    assert depth == 1, "one layer: the final norm is fused into the combine step"
    assert S % (ATTN_BLK * DILATIONS[-1]) == 0 and S % max(PROJ_ROWS, GDN_ROWS, COMBINE_ROWS) == 0
    assert (B * S) % DEST_COLS == 0 and w_ada.shape[2] % ADA_COLS == 0
    x2 = x.reshape(B * S, D)
    mod = _ada(c, w_ada[0], b_ada[0])
    x1, gtc, y4 = _layer(x2, mod, B, S, norm1_w[0], w_in[0], conv_w[0], A_log[0], dt_bias[0],
                         dn_norm_w[0], w_out[0], norm2_w[0], w_router[0], b_router[0],
                         w1[0], b1[0], w2[0], b2[0])
    out = _combine(gtc, x1, mod, final_norm_w, y4, S)
    return out.reshape(B, S, D)
```

```python
import functools

import jax
import jax.numpy as jnp
from jax import lax
from jax.experimental import pallas as pl
from jax.experimental.pallas import tpu as pltpu
from jax.experimental.pallas import tpu_sc as plsc

F32 = jnp.float32
BF16 = jnp.bfloat16
I32 = jnp.int32
HI = lax.Precision.HIGHEST

LANES = 128
ATTN_HEADS = 8
ATTN_HD = 64
ATTN_W = ATTN_HEADS * ATTN_HD
ATTN_BLK = 128
DILATIONS = (1, 4, 16)
DN_HEADS = 4
DN_D = 128
DN_W = DN_HEADS * DN_D
DN_CONV = 3 * DN_W
CONV_K = 4
DN_CHUNK = 64
N_EXPERTS = 32
TOP_K = 4
MOE_ROWS = 512
ROW_UNROLL = 4
SWIGLU_LIMIT = 7.0
SWIGLU_ALPHA = 1.702
EPS = 1e-6
NEG = -1e30
LOG2E = 1.4426950408889634
MAIN_COLS = 3 * ATTN_W + DN_CONV + DN_W

ADA_COLS = 1024
PROJ_ROWS = 1024
ATTN_ROWS = 1024
GDN_ROWS = 1024
DEST_COLS = 2048
COMBINE_ROWS = 1024

V7X_VMEM_BYTES = 64 * 1024 * 1024
VMEM_LIMIT = V7X_VMEM_BYTES - 8 * 1024 * 1024


def _cparams(sem):
    return pltpu.CompilerParams(dimension_semantics=sem, vmem_limit_bytes=VMEM_LIMIT)


def _nt(a, b, **kw):
    return lax.dot_general(a, b, (((1,), (1,)), ((), ())), preferred_element_type=F32, **kw)


def _sigmoid(x):
    return 1.0 / (1.0 + jnp.exp(-x))


def _store_slabs(ref, val, lead=()):
    n, d = val.shape
    ns = d // LANES
    for c in range(ns):
        ref[lead + (pl.ds(c, n, stride=ns), slice(None))] = val[:, c * LANES:(c + 1) * LANES]


def _load_slabs(ref, n, d, lead=()):
    ns = d // LANES
    return jnp.concatenate([ref[lead + (pl.ds(c, n, stride=ns), slice(None))] for c in range(ns)], axis=1)


def _ada_kernel(c_ref, w_ref, b_ref, o_ref):
    c = c_ref[...]
    cond = c * _sigmoid(c)
    o_ref[...] = jnp.dot(cond, w_ref[...], preferred_element_type=F32, precision=HI) + b_ref[...]


def _ada(c, w_ada, b_ada):
    B, D = c.shape
    N = w_ada.shape[1]
    cp = jnp.zeros((8, D), F32).at[:B].set(c)
    tn = ADA_COLS
    out = pl.pallas_call(
        _ada_kernel,
        grid=(N // tn,),
        in_specs=[pl.BlockSpec((8, D), lambda j: (0, 0)),
                  pl.BlockSpec((D, tn), lambda j: (0, j)),
                  pl.BlockSpec((1, tn), lambda j: (0, j))],
        out_specs=pl.BlockSpec((8, tn), lambda j: (0, j)),
        out_shape=jax.ShapeDtypeStruct((8, N), F32),
        compiler_params=_cparams(("arbitrary",)),
        name="ada",
    )(cp, w_ada, b_ada.reshape(1, N))
    return out[:B].reshape(B, 6, D)


def _inproj_kernel(x_ref, mod_ref, nw_ref, wm_ref, ws_ref, qkv_ref, qkv4_ref, qkv16_ref, dqkv_ref, dz_ref,
                   gbc_ref, scr, scr2):
    x = x_ref[...]
    tm = x.shape[0]
    shift = mod_ref[0, 0:1, :]
    scale = mod_ref[0, 1:2, :]
    ms = jnp.mean(x * x, axis=-1, keepdims=True)
    h = x * lax.rsqrt(ms + EPS) * nw_ref[...]
    hb = (h * (1.0 + scale) + shift).astype(BF16)
    nl = ATTN_W // LANES
    proj = [jnp.dot(hb, wm_ref[:, c0:c0 + ATTN_W], preferred_element_type=F32)
            for c0 in range(0, MAIN_COLS, ATTN_W)]
    for j in range(3):
        cs = slice(j * ATTN_W, (j + 1) * ATTN_W)
        r = proj[j]
        if j == 0:
            r = r * (ATTN_HD ** -0.5 * LOG2E)
        qkv_ref[:, cs] = r.astype(BF16)
        d1, d2 = DILATIONS[1], DILATIONS[2]
        step = d2 // d1
        for c in range(nl):
            scr[c] = r[:, c * LANES:(c + 1) * LANES]
        for res in range(d1):
            for c in range(nl):
                scr2[c, res * (tm // d1):(res + 1) * (tm // d1), :] = scr[c, pl.ds(res, tm // d1, stride=d1), :]
            part = jnp.concatenate([scr2[c, res * (tm // d1):(res + 1) * (tm // d1), :] for c in range(nl)],
                                   axis=1)
            qkv4_ref[0, res, :, cs] = part.astype(BF16)
        for res in range(d2):
            r1, q = res % d1, res // d1
            part = jnp.concatenate(
                [scr2[c, pl.ds(r1 * (tm // d1) + q, tm // d2, stride=step), :] for c in range(nl)], axis=1)
            qkv16_ref[0, res, :, cs] = part.astype(BF16)
    for j in range(3):
        dqkv_ref[:, j * DN_W:(j + 1) * DN_W] = proj[3 + j].astype(BF16)
    dz_ref[...] = proj[6].astype(BF16)
    gbc_ref[...] = jnp.dot(hb, ws_ref[...], preferred_element_type=F32)


def _inproj(x2, mod, norm_w, w_in, S):
    T, D = x2.shape
    tm = PROJ_ROWS
    B, nt = T // S, S // tm
    wm = w_in[:, :MAIN_COLS].astype(BF16)
    ws = jnp.zeros((D, LANES), F32).at[:, :2 * DN_HEADS].set(w_in[:, MAIN_COLS:]).astype(BF16)
    return pl.pallas_call(
        _inproj_kernel,
        grid=(T // tm,),
        in_specs=[pl.BlockSpec((tm, D), lambda i: (i, 0)),
                  pl.BlockSpec((1, 6, D), lambda i: (i * tm // S, 0, 0)),
                  pl.BlockSpec((1, D), lambda i: (0, 0)),
                  pl.BlockSpec((D, MAIN_COLS), lambda i: (0, 0)),
                  pl.BlockSpec((D, LANES), lambda i: (0, 0))],
        out_specs=[pl.BlockSpec((tm, 3 * ATTN_W), lambda i: (i, 0))]
        + [pl.BlockSpec((1, d, tm // d, 3 * ATTN_W), lambda i: (i // nt, 0, i % nt, 0)) for d in DILATIONS[1:]]
        + [pl.BlockSpec((tm, DN_CONV), lambda i: (i, 0)),
           pl.BlockSpec((tm, DN_W), lambda i: (i, 0)),
           pl.BlockSpec((tm, LANES), lambda i: (i, 0))],
        out_shape=[jax.ShapeDtypeStruct((T, 3 * ATTN_W), BF16)]
        + [jax.ShapeDtypeStruct((B, d, S // d, 3 * ATTN_W), BF16) for d in DILATIONS[1:]]
        + [jax.ShapeDtypeStruct((T, DN_CONV), BF16),
           jax.ShapeDtypeStruct((T, DN_W), BF16),
           jax.ShapeDtypeStruct((T, LANES), F32)],
        scratch_shapes=[pltpu.VMEM((ATTN_W // LANES, tm, LANES), F32),
                        pltpu.VMEM((ATTN_W // LANES, tm, LANES), F32)],
        compiler_params=_cparams(("arbitrary",)),
        name="inproj",
    )(x2, mod, norm_w.reshape(1, D), wm, ws)


def _attn_kernel(q_ref, kc_ref, kp_ref, vc_ref, vp_ref, o_ref, lse_ref, kf, vf, *, qb):
    n = pl.program_id(2)
    kf[0:ATTN_BLK, :] = kp_ref[0, 0]
    kf[ATTN_BLK:, :] = kc_ref[0, 0]
    vf[0:ATTN_BLK, :] = vp_ref[0, 0]
    vf[ATTN_BLK:, :] = vc_ref[0, 0]
    row = lax.broadcasted_iota(I32, (ATTN_BLK, 2 * ATTN_BLK), 0)
    col = lax.broadcasted_iota(I32, (ATTN_BLK, 2 * ATTN_BLK), 1)
    band = jnp.logical_or(jnp.logical_and(col < ATTN_BLK, col >= row),
                          jnp.logical_and(col >= ATTN_BLK, col - ATTN_BLK <= row))
    lane = lax.broadcasted_iota(I32, (ATTN_BLK, LANES), 1)
    lo = lane < ATTN_HD

    def sub(j, carry):
        r0 = pl.multiple_of(j * ATTN_BLK, ATTN_BLK)
        first_col = jnp.where(jnp.logical_and(n == 0, j == 0), ATTN_BLK, 0)
        mask = jnp.logical_and(band, col >= first_col)
        npair = ATTN_W // LANES
        cols = [slice(hp * LANES, (hp + 1) * LANES) for hp in range(npair)]
        heads = [(hp, half) for hp in range(npair) for half in range(2)]
        scores = []
        for hp, half in heads:
            q2 = q_ref[0, 0, pl.ds(r0, ATTN_BLK), cols[hp]]
            qm = jnp.where(lo if half == 0 else jnp.logical_not(lo), q2, jnp.zeros_like(q2))
            scores.append(jnp.where(mask, _nt(qm, kf[pl.ds(r0, 2 * ATTN_BLK), cols[hp]]), NEG))
        maxes = [jnp.max(s, axis=-1, keepdims=True) for s in scores]
        probs = [jnp.exp2(s - m) for s, m in zip(scores, maxes)]
        dens = [jnp.sum(p, axis=-1, keepdims=True) for p in probs]
        accs = [jnp.dot(p.astype(BF16), vf[pl.ds(r0, 2 * ATTN_BLK), cols[hp]], preferred_element_type=F32)
                for p, (hp, _) in zip(probs, heads)]
        outs = [a / d for a, d in zip(accs, dens)]
        lse_tile = jnp.zeros((ATTN_BLK, LANES), F32)
        for h, (m, d) in enumerate(zip(maxes, dens)):
            lse_tile = jnp.where(lane == h, m + jnp.log2(d), lse_tile)
        for hp in range(npair):
            o_ref[0, 0, pl.ds(r0, ATTN_BLK), cols[hp]] = jnp.where(lo, outs[2 * hp], outs[2 * hp + 1]).astype(BF16)
        lse_ref[0, 0, pl.ds(r0, ATTN_BLK), :] = lse_tile
        return carry

    lax.fori_loop(0, qb // ATTN_BLK, sub, 0, unroll=True)


def _attn_branch(qkv, d):
    B, _, L, _ = qkv.shape
    qb = min(ATTN_ROWS, L)
    nsub = qb // ATTN_BLK
    cur = lambda c: pl.BlockSpec((1, 1, qb, ATTN_W), lambda b, r, n: (b, r, n, c))
    prev = lambda c: pl.BlockSpec((1, 1, ATTN_BLK, ATTN_W),
                                  lambda b, r, n: (b, r, jnp.maximum(n * nsub - 1, 0), c))
    return pl.pallas_call(
        functools.partial(_attn_kernel, qb=qb),
        grid=(B, d, L // qb),
        in_specs=[cur(0), cur(1), prev(1), cur(2), prev(2)],
        out_specs=[pl.BlockSpec((1, 1, qb, ATTN_W), lambda b, r, n: (b, r, n, 0)),
                   pl.BlockSpec((1, 1, qb, LANES), lambda b, r, n: (b, r, n, 0))],
        out_shape=[jax.ShapeDtypeStruct((B, d, L, ATTN_W), BF16),
                   jax.ShapeDtypeStruct((B, d, L, LANES), F32)],
        scratch_shapes=[pltpu.VMEM((qb + ATTN_BLK, ATTN_W), BF16),
                        pltpu.VMEM((qb + ATTN_BLK, ATTN_W), BF16)],
        compiler_params=_cparams(("arbitrary", "arbitrary", "arbitrary")),
        name=f"attn_d{d}",
    )(qkv, qkv, qkv, qkv, qkv)


def _gdn_kernel(x_ref, z_ref, g_ref, cw_ref, prm_ref, nw_ref, o_ref, xext, yc, s0, s1, *, rb, nseq):
    i = pl.program_id(1)

    @pl.when(i == 0)
    def _():
        xext[:, 0:8, :] = jnp.zeros((nseq, 8, DN_CONV), F32)
        s0[...] = jnp.zeros_like(s0)
        s1[...] = jnp.zeros_like(s1)

    @pl.when(i > 0)
    def _():
        xext[:, 0:8, :] = xext[:, rb:rb + 8, :]

    for q in range(nseq):
        xext[q, 8:, :] = x_ref[q].astype(F32)
        y = cw_ref[CONV_K - 1:CONV_K, :] * xext[q, 8:8 + rb, :]
        for j in range(CONV_K - 1):
            off = 8 - (CONV_K - 1) + j
            y = y + cw_ref[j:j + 1, :] * xext[q, off:off + rb, :]
        yc[q] = y * _sigmoid(y)

    C = DN_CHUNK
    H = DN_HEADS
    CW = H * C
    dot = functools.partial(jnp.dot, preferred_element_type=F32)

    def iota(shape, d):
        return lax.broadcasted_iota(I32, shape, d)

    ltri_b = jnp.where(iota((C, C), 0) >= iota((C, C), 1), 1.0, 0.0).astype(BF16)
    lane = iota((C, LANES), 1)
    blane = lane < H
    glane = jnp.logical_and(lane >= H, lane < 2 * H)
    e512 = jnp.where(jnp.logical_or(iota((LANES, DN_W), 1) // DN_D == iota((LANES, DN_W), 0),
                                    iota((LANES, DN_W), 1) // DN_D == iota((LANES, DN_W), 0) - H),
                     1.0, 0.0).astype(BF16)
    e256 = jnp.where(iota((LANES, CW), 1) // C == iota((LANES, CW), 0) - H, 1.0, 0.0).astype(BF16)
    row4 = iota((C, CW), 0)
    col4 = iota((C, CW), 1) % C
    eye4 = jnp.where(row4 == col4, 1.0, 0.0).astype(F32)
    blk = [iota((C, CW), 1) // C == h for h in range(H)]
    bd_cc = iota((CW, CW), 0) // C == iota((CW, CW), 1) // C
    bd_pair = iota((CW, CW), 0) // DN_D == iota((CW, CW), 1) // DN_D
    rt_mask = iota((CW, DN_W), 0) // C == iota((CW, DN_W), 1) // DN_D
    neg_a = -jnp.exp(prm_ref[0:1, :])
    dtb = prm_ref[1:2, :]
    nw = nw_ref[...]

    def hilo(x):
        hi = x.astype(BF16)
        return hi, (x - hi.astype(F32)).astype(BF16)

    def heads(a, w):
        return [a[:, h * w:(h + 1) * w] for h in range(H)]

    def l2n(a, mult):
        return jnp.concatenate(
            [p * (lax.rsqrt(jnp.sum(p * p, axis=-1, keepdims=True) + EPS) * mult) for p in heads(a, DN_D)],
            axis=1)

    def stack4(a):
        return jnp.concatenate([a, a, a, a], axis=0)

    zb = jnp.zeros((), BF16)
    nchunk = rb // C
    chunks = []
    for c, sq in ((c, sq) for c in range(nchunk) for sq in range(nseq)):
        rs = slice(c * C, (c + 1) * C)
        G = g_ref[sq, rs, :]
        xg = G + dtb
        gv = jnp.where(glane, neg_a * (jnp.maximum(xg, 0.0) + jnp.log1p(jnp.exp(-jnp.abs(xg)))), 0.0)
        be = jnp.where(blane, _sigmoid(G), 0.0)
        g_hi, g_lo = hilo(gv)
        gcum = dot(ltri_b, g_hi) + dot(ltri_b, g_lo)
        gtot = gcum[C - 1:C, :]
        eg = jnp.where(glane, jnp.exp(gcum), 0.0)
        ek = jnp.where(glane, jnp.exp(gtot - gcum), 0.0)
        ex = dot(jnp.concatenate([be, eg, ek], axis=0).astype(BF16), e512)
        bexp, egexp, ekexp = ex[0:C], ex[C:2 * C], ex[2 * C:3 * C]
        gexp = dot(g_hi, e256) + dot(g_lo, e256)
        d_hi, d_lo = hilo(jnp.where(row4 > col4, gexp, 0.0))
        diff = dot(ltri_b, d_hi) + dot(ltri_b, d_lo)
        decay = jnp.exp(jnp.where(row4 >= col4, diff, NEG))

        q4 = l2n(yc[sq, rs, 0:DN_W], DN_D ** -0.5)
        k4 = l2n(yc[sq, rs, DN_W:2 * DN_W], 1.0)
        v4 = yc[sq, rs, 2 * DN_W:3 * DN_W]
        kb4 = k4 * bexp
        vb4 = v4 * bexp
        rt = jnp.where(rt_mask, stack4(k4.astype(BF16)), jnp.zeros((), BF16))
        ai = _nt(jnp.concatenate([kb4, q4], axis=0).astype(BF16), rt)
        a4 = jnp.where(row4 > col4, ai[0:C] * decay, 0.0)
        pb = a4.astype(BF16)
        chunks.append(dict(
            sq=sq, rs=rs, q4=q4, k4=k4, kb4=kb4, vb4=vb4, egexp=egexp, ekexp=ekexp,
            intra=ai[C:2 * C] * decay, t4=eye4 - a4, pb=pb,
            bd=jnp.where(bd_cc, stack4(pb), zb)))

    for _ in range(5):
        for ch in chunks:
            ch["pb"] = dot(ch["pb"], ch["bd"]).astype(BF16)
        for ch in chunks:
            ch["bd"] = jnp.where(bd_cc, stack4(ch["pb"]), zb)
        for ch in chunks:
            ch["t4"] = ch["t4"] + dot(ch["t4"].astype(BF16), ch["bd"])

    for ch in chunks:
        q4, k4, kb4, vb4, egexp, ekexp = (ch[n] for n in ("q4", "k4", "kb4", "vb4", "egexp", "ekexp"))
        t4b = ch["t4"].astype(BF16)
        lstk = jnp.concatenate([jnp.where(blk[h], t4b, zb) for h in range(H)], axis=0)
        kbg4 = kb4 * egexp
        rstk = jnp.concatenate(
            [jnp.concatenate([vb, kbg], axis=1) for vb, kbg in zip(heads(vb4, DN_D), heads(kbg4, DN_D))],
            axis=0).astype(BF16)
        uw = dot(lstk, rstk)
        u4 = jnp.concatenate([uw[h * C:(h + 1) * C, 0:DN_D] for h in range(H)], axis=1)
        w4 = jnp.concatenate([uw[h * C:(h + 1) * C, DN_D:2 * DN_D] for h in range(H)], axis=1)
        ib = ch["intra"].astype(BF16)
        ch.update(
            u4=u4, wq=jnp.concatenate([w4, q4 * egexp], axis=0).astype(BF16),
            kd4=(k4 * ekexp).astype(BF16), gl4=egexp[C - 1:C, :],
            lint=jnp.concatenate([jnp.where(blk[h], ib, zb) for h in range(H)], axis=0))

    for ch in chunks:
        sq, rs, u4, wq, kd4, gl4, lint = (ch[n] for n in ("sq", "rs", "u4", "wq", "kd4", "gl4", "lint"))
        ra = dot(wq[:, 0:CW], s0[sq].astype(BF16))
        rc = dot(wq[:, CW:2 * CW], s1[sq].astype(BF16))
        vn = u4 - jnp.concatenate([ra[0:C], rc[0:C]], axis=1)
        vnb = vn.astype(BF16)
        oi = dot(lint, jnp.concatenate(heads(vnb, DN_D), axis=0))
        o = (jnp.concatenate([ra[C:2 * C], rc[C:2 * C]], axis=1)
             + jnp.concatenate([oi[h * C:(h + 1) * C] for h in range(H)], axis=1))
        tn = (((0,), (0,)), ((), ()))
        s0[sq] = s0[sq] * gl4[:, 0:CW] + jnp.where(
            bd_pair, lax.dot_general(kd4[:, 0:CW], vnb[:, 0:CW], tn, preferred_element_type=F32), 0.0)
        s1[sq] = s1[sq] * gl4[:, CW:2 * CW] + jnp.where(
            bd_pair, lax.dot_general(kd4[:, CW:2 * CW], vnb[:, CW:2 * CW], tn, preferred_element_type=F32), 0.0)

        z = z_ref[sq, rs, :].astype(F32)
        on = jnp.concatenate(
            [p * lax.rsqrt(jnp.mean(p * p, axis=-1, keepdims=True) + EPS) * nw for p in heads(o, DN_D)], axis=1)
        o_ref[sq, rs, :] = (on * (z * _sigmoid(z))).astype(BF16)


def _gdn(dqkv, dz, gbc, conv_w, A_log, dt_bias, dn_norm_w, B, S):
    nseq = 2 if B % 2 == 0 else 1
    rb = GDN_ROWS // nseq
    prm = jnp.zeros((2, LANES), F32)
    prm = prm.at[0, DN_HEADS:2 * DN_HEADS].set(A_log.astype(F32))
    prm = prm.at[1, DN_HEADS:2 * DN_HEADS].set(dt_bias.astype(F32))
    out = pl.pallas_call(
        functools.partial(_gdn_kernel, rb=rb, nseq=nseq),
        grid=(B // nseq, S // rb),
        in_specs=[pl.BlockSpec((nseq, rb, DN_CONV), lambda b, i: (b, i, 0)),
                  pl.BlockSpec((nseq, rb, DN_W), lambda b, i: (b, i, 0)),
                  pl.BlockSpec((nseq, rb, LANES), lambda b, i: (b, i, 0)),
                  pl.BlockSpec((CONV_K, DN_CONV), lambda b, i: (0, 0)),
                  pl.BlockSpec((2, LANES), lambda b, i: (0, 0)),
                  pl.BlockSpec((1, DN_D), lambda b, i: (0, 0))],
        out_specs=pl.BlockSpec((nseq, rb, DN_W), lambda b, i: (b, i, 0)),
        out_shape=jax.ShapeDtypeStruct((B, S, DN_W), BF16),
        scratch_shapes=[pltpu.VMEM((nseq, rb + 8, DN_CONV), F32),
                        pltpu.VMEM((nseq, rb, DN_CONV), F32),
                        pltpu.VMEM((nseq, 2 * DN_D, 2 * DN_D), F32),
                        pltpu.VMEM((nseq, 2 * DN_D, 2 * DN_D), F32)],
        compiler_params=_cparams(("arbitrary", "arbitrary")),
        name="gdn",
    )(dqkv.reshape(B, S, DN_CONV), dz.reshape(B, S, DN_W), gbc.reshape(B, S, LANES),
      conv_w, prm, dn_norm_w.reshape(1, DN_D))
    return out.reshape(B * S, DN_W)


def _out_kernel(o1_ref, o2_ref, o3_ref, l1_ref, l2_ref, l3_ref, dn_ref, x_ref, mod_ref, wo_ref,
                n2_ref, wr_ref, br_ref,
                x1_ref, h2_ref, te_ref, rk_ref, gtc_ref, cnt_ref, base, scr, *, tm):
    i = pl.program_id(0)

    @pl.when(i == 0)
    def _():
        base[...] = jnp.zeros_like(base)

    def natural(ref, d):
        if d == 1:
            return ref[0, 0].astype(F32)
        nl = ref.shape[-1] // LANES
        for res in range(d):
            blk = ref[0, res].astype(F32)
            for c in range(nl):
                scr[c, pl.ds(res, tm // d, stride=d), :] = blk[:, c * LANES:(c + 1) * LANES]
        return jnp.concatenate([scr[c] for c in range(nl)], axis=1)

    l1, l2, l3 = (natural(r, d) for r, d in zip((l1_ref, l2_ref, l3_ref), DILATIONS))
    mx = jnp.maximum(jnp.maximum(l1, l2), l3)
    e1, e2, e3 = jnp.exp2(l1 - mx), jnp.exp2(l2 - mx), jnp.exp2(l3 - mx)
    zs = e1 + e2 + e3
    er = lax.broadcasted_iota(I32, (LANES, ATTN_W), 0)
    ec = lax.broadcasted_iota(I32, (LANES, ATTN_W), 1)
    expand = jnp.where(ec // ATTN_HD == er, 1.0, 0.0).astype(BF16)
    attn = jnp.zeros((tm, ATTN_W), F32)
    for e, o_ref, d in zip((e1, e2, e3), (o1_ref, o2_ref, o3_ref), DILATIONS):
        wgt = jnp.dot((e / zs).astype(BF16), expand, preferred_element_type=F32)
        attn = attn + wgt * natural(o_ref, d)
    mix = (jnp.dot(attn.astype(BF16), wo_ref[0:ATTN_W, :], preferred_element_type=F32)
           + jnp.dot(dn_ref[...], wo_ref[ATTN_W:, :], preferred_element_type=F32))
    x1 = x_ref[...] + mod_ref[0, 2:3, :] * mix
    x1_ref[...] = x1
    ms = jnp.mean(x1 * x1, axis=-1, keepdims=True)
    h2 = x1 * lax.rsqrt(ms + EPS) * n2_ref[...]
    h2 = h2 * (1.0 + mod_ref[0, 4:5, :]) + mod_ref[0, 3:4, :]
    _store_slabs(h2_ref, h2)

    def split(a):
        hi = a.astype(BF16)
        return hi, (a - hi.astype(F32)).astype(BF16)

    h_hi, h_lo = split(h2)
    w_hi, w_lo = split(wr_ref[...])
    lg = _nt(w_hi, h_hi) + (_nt(w_hi, h_lo) + _nt(w_lo, h_hi)) + br_ref[...]
    eidx = lax.broadcasted_iota(I32, (N_EXPERTS, tm), 0)
    vals, idxs, sels = [], [], []
    for _ in range(TOP_K):
        m = jnp.max(lg, axis=0, keepdims=True)
        idx = jnp.min(jnp.where(lg == m, eidx, N_EXPERTS), axis=0, keepdims=True)
        sel = eidx == idx
        vals.append(m)
        idxs.append(idx)
        sels.append(sel)
        lg = jnp.where(sel, -jnp.inf, lg)
    ex = [jnp.exp(v - vals[0]) for v in vals]
    den = ex[0] + ex[1] + ex[2] + ex[3]
    gates = [e / den for e in ex]

    msum = jnp.zeros((N_EXPERTS, tm), F32)
    for sel in sels:
        msum = msum + jnp.where(sel, 1.0, 0.0)
    tr = lax.broadcasted_iota(I32, (tm, tm), 0)
    tc = lax.broadcasted_iota(I32, (tm, tm), 1)
    upper = jnp.where(tr <= tc, 1.0, 0.0).astype(BF16)
    incl = jnp.dot(msum.astype(BF16), upper, preferred_element_type=F32)
    pos = base[:, 0:1] + (incl - msum)
    sub8 = lax.broadcasted_iota(I32, (8, tm), 0)
    te = jnp.zeros((8, tm), I32)
    rk = jnp.zeros((8, tm), I32)
    gt = jnp.zeros((8, tm), F32)
    for k in range(TOP_K):
        rank_k = jnp.sum(jnp.where(sels[k], pos, 0.0), axis=0, keepdims=True).astype(I32)
        te = jnp.where(sub8 == k, idxs[k], te)
        rk = jnp.where(sub8 == k, rank_k, rk)
        gt = jnp.where(sub8 == k, gates[k], gt)
    te_ref[...] = te
    rk_ref[...] = rk
    gtc_ref[...] = jnp.transpose(jnp.concatenate([gt, jnp.zeros((LANES - 8, tm), F32)], axis=0))
    base[...] = base[...] + jnp.sum(msum, axis=1, keepdims=True)
    cnt_ref[...] = base[...].astype(I32)


def _outproj(o1, o2, o3, l1, l2, l3, dn, x2, mod, w_out, norm2_w, w_router, b_router, S):
    T, D = x2.shape
    tm = PROJ_ROWS
    nt = S // tm
    row = lambda w: pl.BlockSpec((tm, w), lambda i: (i, 0))
    res = lambda d, w: pl.BlockSpec((1, d, tm // d, w), lambda i: (i // nt, 0, i % nt, 0))
    colb = pl.BlockSpec((8, tm), lambda i: (0, i))
    return pl.pallas_call(
        functools.partial(_out_kernel, tm=tm),
        grid=(T // tm,),
        in_specs=[res(d, ATTN_W) for d in DILATIONS] + [res(d, LANES) for d in DILATIONS]
        + [row(DN_W), row(D),
                  pl.BlockSpec((1, 6, D), lambda i: (i * tm // S, 0, 0)),
                  pl.BlockSpec((D, D), lambda i: (0, 0)),
                  pl.BlockSpec((1, D), lambda i: (0, 0)),
                  pl.BlockSpec((N_EXPERTS, D), lambda i: (0, 0)),
                  pl.BlockSpec((N_EXPERTS, 1), lambda i: (0, 0))],
        out_specs=[row(D), pl.BlockSpec((tm * (D // LANES), LANES), lambda i: (i, 0)), colb, colb,
                   row(LANES), pl.BlockSpec((N_EXPERTS, LANES), lambda i: (0, 0))],
        out_shape=[jax.ShapeDtypeStruct((T, D), F32),
                   jax.ShapeDtypeStruct((T * (D // LANES), LANES), F32),
                   jax.ShapeDtypeStruct((8, T), I32),
                   jax.ShapeDtypeStruct((8, T), I32),
                   jax.ShapeDtypeStruct((T, LANES), F32),
                   jax.ShapeDtypeStruct((N_EXPERTS, LANES), I32)],
        scratch_shapes=[pltpu.VMEM((N_EXPERTS, LANES), F32),
                        pltpu.VMEM((ATTN_W // LANES, tm, LANES), F32)],
        compiler_params=_cparams(("arbitrary",)),
        name="outproj_router",
    )(o1, o2, o3, l1, l2, l3, dn, x2, mod, w_out.astype(BF16), norm2_w.reshape(1, D),
      jnp.transpose(w_router), b_router.reshape(N_EXPERTS, 1))


def _dest_kernel(ps_ref, te_ref, rk_ref, d_ref):
    te = te_ref[...]
    acc = jnp.zeros(te.shape, I32)
    for e in range(N_EXPERTS):
        acc = jnp.where(te == e, ps_ref[e], acc)
    d_ref[...] = acc + rk_ref[...]


def _dest(pstart, te, rk):
    T = te.shape[1]
    tb = DEST_COLS
    return pl.pallas_call(
        _dest_kernel,
        grid_spec=pltpu.PrefetchScalarGridSpec(
            num_scalar_prefetch=1,
            grid=(T // tb,),
            in_specs=[pl.BlockSpec((8, tb), lambda i, ps: (0, i)),
                      pl.BlockSpec((8, tb), lambda i, ps: (0, i))],
            out_specs=pl.BlockSpec((8, tb), lambda i, ps: (0, i))),
        out_shape=jax.ShapeDtypeStruct((8, T), I32),
        compiler_params=_cparams(("arbitrary",)),
        name="dest_rows",
    )(pstart, te, rk)


SC_CORES = 2
SC_SUBCORES = 16
SC_IDX_CHUNK = 128


def _invperm(dest_flat, P):
    N = dest_flat.shape[0]
    nch = N // (SC_SUBCORES * SC_IDX_CHUNK)
    half = P // SC_CORES
    per_out = half // SC_SUBCORES
    assert N % (SC_SUBCORES * SC_IDX_CHUNK) == 0 and P % (SC_CORES * SC_SUBCORES * 8) == 0
    mesh = plsc.VectorSubcoreMesh(core_axis_name="c", subcore_axis_name="s",
                                  num_cores=SC_CORES, num_subcores=SC_SUBCORES)

    @functools.partial(
        pl.kernel, mesh=mesh, out_type=jax.ShapeDtypeStruct((P,), I32),
        scratch_types=[pltpu.VMEM((nch, SC_IDX_CHUNK), I32), pltpu.VMEM((nch, SC_IDX_CHUNK), I32),
                       pltpu.VMEM_SHARED((P,), I32), pltpu.VMEM((per_out,), I32), pltpu.SemaphoreType.DMA])
    def scatter_codes(idx_hbm, val_hbm, out_hbm, idx_v, val_v, table, stage, sem):
        core = lax.axis_index("c")
        sub = lax.axis_index("s")
        pltpu.sync_copy(idx_hbm.at[sub], idx_v)
        pltpu.sync_copy(val_hbm.at[sub], val_v)

        @pl.loop(0, nch)
        def _(j):
            pltpu.async_copy(val_v.at[j], table.at[idx_v.at[j]], sem)

        @pl.loop(0, nch)
        def _(j):
            pltpu.make_async_copy(val_v.at[j], table.at[idx_v.at[j]], sem).wait()

        plsc.subcore_barrier()
        off = pl.multiple_of(core * half + sub * per_out, 8)
        pltpu.sync_copy(table.at[pl.ds(off, per_out)], stage)
        pltpu.sync_copy(stage, out_hbm.at[pl.ds(off, per_out)])

    vals = jnp.arange(N, dtype=I32)
    shape = (SC_SUBCORES, nch, SC_IDX_CHUNK)
    return scatter_codes(dest_flat.reshape(shape), vals.reshape(shape))


def _row_copy(src, dst, sem):
    return pltpu.make_async_copy(src, dst, sem)


def _moe_kernel(be_ref, nv_ref, cc_ref, cn_ref, h2_ref, w1_ref, b1_ref, w2_ref, b2_ref, y4_ref,
                xbuf, ybuf, w1b, w2b, gsem, ssem, *, F, T, D, tme, nb):
    i = pl.program_id(0)
    s = i % 2
    ns = D // LANES

    def rows(first, n):
        return pl.ds(pl.multiple_of(first * ns, ns), n * ns)

    def gather_copy(tok, p, slot):
        return _row_copy(h2_ref.at[rows(tok, 1)], xbuf.at[slot, rows(p, 1)], gsem.at[slot])

    def token_of(code):
        return code & (T - 1) if T & (T - 1) == 0 else code % T

    def issue_rows(start_row, nv):
        ng = nv // ROW_UNROLL

        def grp(g, c):
            for j in range(ROW_UNROLL):
                start_row(g * ROW_UNROLL + j, j % 2)
            return c
        lax.fori_loop(0, ng, grp, 0)

        def one(p, c):
            start_row(p, 0)
            return c
        lax.fori_loop(ng * ROW_UNROLL, nv, one, 0)

    def start_gather(code_ref, nv, slot):
        issue_rows(lambda p, pri: gather_copy(token_of(code_ref[0, 0, p]), p, slot).start(priority=pri), nv)

    def wait_rows(make, nv):
        @pl.when(nv > 0)
        def _():
            make(nv).wait()

    def gathered(n, slot):
        return _row_copy(h2_ref.at[rows(0, n)], xbuf.at[slot, rows(0, n)], gsem.at[slot])

    def scattered(n, slot):
        return _row_copy(ybuf.at[slot, rows(0, n)], y4_ref.at[rows(0, n)], ssem.at[slot])

    @pl.when(i == 0)
    def _():
        start_gather(cc_ref, tme, 0)

    nv = nv_ref[i]
    gathered(tme, s).wait()

    @pl.when(i >= 2)
    def _():
        wait_rows(lambda n: scattered(n, s), nv_ref[i - 2])

    @pl.when(jnp.logical_or(i == 0, be_ref[i] != be_ref[jnp.maximum(i - 1, 0)]))
    def _():
        w1b[...] = w1_ref[0].astype(BF16)
        w2b[...] = w2_ref[0].astype(BF16)

    def compute(slot):
        x = _load_slabs(xbuf, tme, D, lead=(slot,)).astype(BF16)
        for p in range(tme):
            gather_copy(token_of(cn_ref[0, 0, p]), p, 1 - slot).start(priority=p % 2)
        hgu = jnp.dot(x, w1b[...], preferred_element_type=F32) + b1_ref[0]
        gate = jnp.minimum(hgu[:, :F], SWIGLU_LIMIT)
        up = jnp.clip(hgu[:, F:], -SWIGLU_LIMIT, SWIGLU_LIMIT)
        act = gate * _sigmoid(SWIGLU_ALPHA * gate) * (up + 1.0)
        y = jnp.dot(act.astype(BF16), w2b[...], preferred_element_type=F32) + b2_ref[0]
        _store_slabs(ybuf, y, lead=(slot,))

        issue_rows(lambda p, pri: _row_copy(ybuf.at[slot, rows(p, 1)], y4_ref.at[rows(cc_ref[0, 0, p], 1)],
                                            ssem.at[slot]).start(priority=pri), nv)

    for slot in range(2):
        pl.when(jnp.logical_and(nv > 0, s == slot))(functools.partial(compute, slot))

    @pl.when(jnp.logical_and(nv == 0, i + 1 < nb))
    def _():
        gathered(tme, 1 - s).start()

    @pl.when(i == nb - 1)
    def _():
        @pl.when(nv > 0)
        def _():
            gathered(tme, 1 - s).wait()
        wait_rows(lambda n: scattered(n, s), nv)
        if nb > 1:
            wait_rows(lambda n: scattered(n, 1 - s), nv_ref[i - 1])


def _experts(blk_exp, blk_valid, codes, h2s, w1, b1, w2, b2, tme):
    E, D, F2 = w1.shape
    F = F2 // 2
    ns = D // LANES
    T = h2s.shape[0] // ns
    nb = blk_exp.shape[0]
    codes3 = codes.reshape(nb, 1, tme)
    wspec = lambda shape: pl.BlockSpec(shape, lambda i, be, nv: (be[i], 0, 0))
    cspec = lambda off: pl.BlockSpec((1, 1, tme), lambda i, be, nv: (jnp.minimum(i + off, nb - 1), 0, 0),
                                     memory_space=pltpu.SMEM)
    return pl.pallas_call(
        functools.partial(_moe_kernel, F=F, T=T, D=D, tme=tme, nb=nb),
        grid_spec=pltpu.PrefetchScalarGridSpec(
            num_scalar_prefetch=2,
            grid=(nb,),
            in_specs=[cspec(0), cspec(1),
                      pl.BlockSpec(memory_space=pl.ANY),
                      wspec((1, D, F2)), wspec((1, 1, F2)), wspec((1, F, D)), wspec((1, 1, D))],
            out_specs=pl.BlockSpec(memory_space=pl.ANY),
            scratch_shapes=[pltpu.VMEM((2, tme * ns, LANES), F32), pltpu.VMEM((2, tme * ns, LANES), F32),
                            pltpu.VMEM((D, F2), BF16), pltpu.VMEM((F, D), BF16),
                            pltpu.SemaphoreType.DMA((2,)), pltpu.SemaphoreType.DMA((2,))]),
        out_shape=jax.ShapeDtypeStruct((TOP_K * T * ns, LANES), F32),
        compiler_params=_cparams(("arbitrary",)),
        name="experts",
    )(blk_exp, blk_valid, codes3, codes3, h2s, w1, b1.reshape(E, 1, F2), w2, b2.reshape(E, 1, D))


def _comb_kernel(g_ref, x1_ref, mod_ref, fw_ref, y0_ref, y1_ref, y2_ref, y3_ref, o_ref):
    g = g_ref[...]
    n, d = x1_ref.shape
    y = g[:, 0:1] * _load_slabs(y0_ref, n, d)
    for k, y_ref in ((1, y1_ref), (2, y2_ref), (3, y3_ref)):
        y = y + g[:, k:k + 1] * _load_slabs(y_ref, n, d)
    x2 = x1_ref[...] + mod_ref[0, 5:6, :] * y
    ms = jnp.mean(x2 * x2, axis=-1, keepdims=True)
    o_ref[...] = x2 * lax.rsqrt(ms + EPS) * fw_ref[...]


def _combine(gtc, x1, mod, final_w, y4, S):
    T, D = x1.shape
    tmc = COMBINE_ROWS
    nt = T // tmc
    yspec = lambda k: pl.BlockSpec((tmc * (D // LANES), LANES), lambda i: (k * nt + i, 0))
    return pl.pallas_call(
        _comb_kernel,
        grid=(nt,),
        in_specs=[pl.BlockSpec((tmc, LANES), lambda i: (i, 0)),
                  pl.BlockSpec((tmc, D), lambda i: (i, 0)),
                  pl.BlockSpec((1, 6, D), lambda i: (i * tmc // S, 0, 0)),
                  pl.BlockSpec((1, D), lambda i: (0, 0)),
                  yspec(0), yspec(1), yspec(2), yspec(3)],
        out_specs=pl.BlockSpec((tmc, D), lambda i: (i, 0)),
        out_shape=jax.ShapeDtypeStruct((T, D), F32),
        compiler_params=_cparams(("arbitrary",)),
        name="combine",
    )(gtc, x1, mod, final_w.reshape(1, D), y4, y4, y4, y4)


def _layer(x2, mod, B, S, norm1_w, w_in, conv_w, A_log, dt_bias, dn_norm_w, w_out,
           norm2_w, w_router, b_router, w1, b1, w2, b2):
    T, D = x2.shape
    qkv, qkv4, qkv16, dqkv, dz, gbc = _inproj(x2, mod, norm1_w, w_in, S)
    views = (qkv.reshape(B, 1, S, 3 * ATTN_W), qkv4, qkv16)
    branches = [_attn_branch(v, d) for v, d in zip(views, DILATIONS)]
    dn = _gdn(dqkv, dz, gbc, conv_w, A_log, dt_bias, dn_norm_w, B, S)
    (o1, l1), (o2, l2), (o3, l3) = branches
    x1, h2, te, rk, gtc, cnt = _outproj(o1, o2, o3, l1, l2, l3, dn, x2, mod, w_out, norm2_w,
                                        w_router, b_router, S)
    tme = MOE_ROWS
    P = T * TOP_K + N_EXPERTS * tme
    counts = cnt[:, 0]
    padded = (counts + tme - 1) // tme * tme
    pend = jnp.cumsum(padded)
    pstart = (pend - padded).astype(I32)
    blk_start = jnp.arange(P // tme, dtype=I32) * tme
    blk_exp = jnp.minimum(jnp.sum((pend[None, :] <= blk_start[:, None]).astype(I32), axis=1),
                          N_EXPERTS - 1).astype(I32)
    blk_valid = jnp.clip(pstart[blk_exp] + counts[blk_exp] - blk_start, 0, tme).astype(I32)
    dest = _dest(pstart, te, rk)
    codes = _invperm(dest[:TOP_K].reshape(TOP_K * T), P)
    y4 = _experts(blk_exp, blk_valid, codes, h2, w1, b1, w2, b2, tme)
    return x1, gtc, y4


def kernel(x, c, w_ada, b_ada, norm1_w, w_in, conv_w, A_log, dt_bias, dn_norm_w, w_out, norm2_w,
           w_router, b_router, w1, b1, w2, b2, final_norm_w):
    B, S, D = x.shape
    depth = w_ada.shape[0]
    assert depth == 1, "one layer: the final norm is fused into the combine step"
    assert S % (ATTN_BLK * DILATIONS[-1]) == 0 and S % max(PROJ_ROWS, GDN_ROWS, COMBINE_ROWS) == 0
    assert (B * S) % DEST_COLS == 0 and w_ada.shape[2] % ADA_COLS == 0
    x2 = x.reshape(B * S, D)
    mod = _ada(c, w_ada[0], b_ada[0])
    x1, gtc, y4 = _layer(x2, mod, B, S, norm1_w[0], w_in[0], conv_w[0], A_log[0], dt_bias[0],
                         dn_norm_w[0], w_out[0], norm2_w[0], w_router[0], b_router[0],
                         w1[0], b1[0], w2[0], b2[0])
    out = _combine(gtc, x1, mod, final_norm_w, y4, S)
    return out.reshape(B, S, D)
```

```python
import functools

import jax
import jax.numpy as jnp
from jax import lax
from jax.experimental import pallas as pl
from jax.experimental.pallas import tpu as pltpu
from jax.experimental.pallas import tpu_sc as plsc

F32 = jnp.float32
BF16 = jnp.bfloat16
I32 = jnp.int32
HI = lax.Precision.HIGHEST

LANES = 128
ATTN_HEADS = 8
ATTN_HD = 64
ATTN_W = ATTN_HEADS * ATTN_HD
ATTN_BLK = 128
DILATIONS = (1, 4, 16)
DN_HEADS = 4
DN_D = 128
DN_W = DN_HEADS * DN_D
DN_CONV = 3 * DN_W
CONV_K = 4
DN_CHUNK = 64
N_EXPERTS = 32
TOP_K = 4
MOE_ROWS = 512
ROW_UNROLL = 8
SWIGLU_LIMIT = 7.0
SWIGLU_ALPHA = 1.702
EPS = 1e-6
NEG = -1e30
LOG2E = 1.4426950408889634
MAIN_COLS = 3 * ATTN_W + DN_CONV + DN_W

ADA_COLS = 1024
PROJ_ROWS = 1024
ATTN_ROWS = 1024
GDN_ROWS = 1024
DEST_COLS = 2048
COMBINE_ROWS = 1024

V7X_VMEM_BYTES = 64 * 1024 * 1024
VMEM_LIMIT = V7X_VMEM_BYTES - 8 * 1024 * 1024


def _cparams(sem):
    return pltpu.CompilerParams(dimension_semantics=sem, vmem_limit_bytes=VMEM_LIMIT)


def _nt(a, b, **kw):
    return lax.dot_general(a, b, (((1,), (1,)), ((), ())), preferred_element_type=F32, **kw)


def _sigmoid(x):
    return 1.0 / (1.0 + jnp.exp(-x))


def _store_slabs(ref, val, lead=()):
    n, d = val.shape
    ns = d // LANES
    for c in range(ns):
        ref[lead + (pl.ds(c, n, stride=ns), slice(None))] = val[:, c * LANES:(c + 1) * LANES]


def _load_slabs(ref, n, d, lead=()):
    ns = d // LANES
    return jnp.concatenate([ref[lead + (pl.ds(c, n, stride=ns), slice(None))] for c in range(ns)], axis=1)


def _ada_kernel(c_ref, w_ref, b_ref, o_ref):
    c = c_ref[...]
    cond = c * _sigmoid(c)
    o_ref[...] = jnp.dot(cond, w_ref[...], preferred_element_type=F32, precision=HI) + b_ref[...]


def _ada(c, w_ada, b_ada):
    B, D = c.shape
    N = w_ada.shape[1]
    cp = jnp.zeros((8, D), F32).at[:B].set(c)
    tn = ADA_COLS
    out = pl.pallas_call(
        _ada_kernel,
        grid=(N // tn,),
        in_specs=[pl.BlockSpec((8, D), lambda j: (0, 0)),
                  pl.BlockSpec((D, tn), lambda j: (0, j)),
                  pl.BlockSpec((1, tn), lambda j: (0, j))],
        out_specs=pl.BlockSpec((8, tn), lambda j: (0, j)),
        out_shape=jax.ShapeDtypeStruct((8, N), F32),
        compiler_params=_cparams(("arbitrary",)),
        name="ada",
    )(cp, w_ada, b_ada.reshape(1, N))
    return out[:B].reshape(B, 6, D)


def _inproj_kernel(x_ref, mod_ref, nw_ref, wm_ref, ws_ref, qkv_ref, qkv4_ref, qkv16_ref, dqkv_ref, dz_ref,
                   gbc_ref, scr, scr2):
    x = x_ref[...]
    tm = x.shape[0]
    shift = mod_ref[0, 0:1, :]
    scale = mod_ref[0, 1:2, :]
    ms = jnp.mean(x * x, axis=-1, keepdims=True)
    h = x * lax.rsqrt(ms + EPS) * nw_ref[...]
    hb = (h * (1.0 + scale) + shift).astype(BF16)
    nl = ATTN_W // LANES
    proj = [jnp.dot(hb, wm_ref[:, c0:c0 + ATTN_W], preferred_element_type=F32)
            for c0 in range(0, MAIN_COLS, ATTN_W)]
    for j in range(3):
        cs = slice(j * ATTN_W, (j + 1) * ATTN_W)
        r = proj[j]
        if j == 0:
            r = r * (ATTN_HD ** -0.5 * LOG2E)
        qkv_ref[:, cs] = r.astype(BF16)
        d1, d2 = DILATIONS[1], DILATIONS[2]
        step = d2 // d1
        for c in range(nl):
            scr[c] = r[:, c * LANES:(c + 1) * LANES]
        for res in range(d1):
            for c in range(nl):
                scr2[c, res * (tm // d1):(res + 1) * (tm // d1), :] = scr[c, pl.ds(res, tm // d1, stride=d1), :]
            part = jnp.concatenate([scr2[c, res * (tm // d1):(res + 1) * (tm // d1), :] for c in range(nl)],
                                   axis=1)
            qkv4_ref[0, res, :, cs] = part.astype(BF16)
        for res in range(d2):
            r1, q = res % d1, res // d1
            part = jnp.concatenate(
                [scr2[c, pl.ds(r1 * (tm // d1) + q, tm // d2, stride=step), :] for c in range(nl)], axis=1)
            qkv16_ref[0, res, :, cs] = part.astype(BF16)
    for j in range(3):
        dqkv_ref[:, j * DN_W:(j + 1) * DN_W] = proj[3 + j].astype(BF16)
    dz_ref[...] = proj[6].astype(BF16)
    gbc_ref[...] = jnp.dot(hb, ws_ref[...], preferred_element_type=F32)


def _inproj(x2, mod, norm_w, w_in, S):
    T, D = x2.shape
    tm = PROJ_ROWS
    B, nt = T // S, S // tm
    wm = w_in[:, :MAIN_COLS].astype(BF16)
    ws = jnp.zeros((D, LANES), F32).at[:, :2 * DN_HEADS].set(w_in[:, MAIN_COLS:]).astype(BF16)
    return pl.pallas_call(
        _inproj_kernel,
        grid=(T // tm,),
        in_specs=[pl.BlockSpec((tm, D), lambda i: (i, 0)),
                  pl.BlockSpec((1, 6, D), lambda i: (i * tm // S, 0, 0)),
                  pl.BlockSpec((1, D), lambda i: (0, 0)),
                  pl.BlockSpec((D, MAIN_COLS), lambda i: (0, 0)),
                  pl.BlockSpec((D, LANES), lambda i: (0, 0))],
        out_specs=[pl.BlockSpec((tm, 3 * ATTN_W), lambda i: (i, 0))]
        + [pl.BlockSpec((1, d, tm // d, 3 * ATTN_W), lambda i: (i // nt, 0, i % nt, 0)) for d in DILATIONS[1:]]
        + [pl.BlockSpec((tm, DN_CONV), lambda i: (i, 0)),
           pl.BlockSpec((tm, DN_W), lambda i: (i, 0)),
           pl.BlockSpec((tm, LANES), lambda i: (i, 0))],
        out_shape=[jax.ShapeDtypeStruct((T, 3 * ATTN_W), BF16)]
        + [jax.ShapeDtypeStruct((B, d, S // d, 3 * ATTN_W), BF16) for d in DILATIONS[1:]]
        + [jax.ShapeDtypeStruct((T, DN_CONV), BF16),
           jax.ShapeDtypeStruct((T, DN_W), BF16),
           jax.ShapeDtypeStruct((T, LANES), F32)],
        scratch_shapes=[pltpu.VMEM((ATTN_W // LANES, tm, LANES), F32),
                        pltpu.VMEM((ATTN_W // LANES, tm, LANES), F32)],
        compiler_params=_cparams(("arbitrary",)),
        name="inproj",
    )(x2, mod, norm_w.reshape(1, D), wm, ws)


def _attn_kernel(q_ref, kc_ref, kp_ref, vc_ref, vp_ref, o_ref, lse_ref, kf, vf, *, qb):
    n = pl.program_id(2)
    kf[0:ATTN_BLK, :] = kp_ref[0, 0]
    kf[ATTN_BLK:, :] = kc_ref[0, 0]
    vf[0:ATTN_BLK, :] = vp_ref[0, 0]
    vf[ATTN_BLK:, :] = vc_ref[0, 0]
    row = lax.broadcasted_iota(I32, (ATTN_BLK, 2 * ATTN_BLK), 0)
    col = lax.broadcasted_iota(I32, (ATTN_BLK, 2 * ATTN_BLK), 1)
    band = jnp.logical_or(jnp.logical_and(col < ATTN_BLK, col >= row),
                          jnp.logical_and(col >= ATTN_BLK, col - ATTN_BLK <= row))
    lane = lax.broadcasted_iota(I32, (ATTN_BLK, LANES), 1)
    lo = lane < ATTN_HD

    def sub(j, carry):
        r0 = pl.multiple_of(j * ATTN_BLK, ATTN_BLK)
        first_col = jnp.where(jnp.logical_and(n == 0, j == 0), ATTN_BLK, 0)
        mask = jnp.logical_and(band, col >= first_col)
        npair = ATTN_W // LANES
        cols = [slice(hp * LANES, (hp + 1) * LANES) for hp in range(npair)]
        heads = [(hp, half) for hp in range(npair) for half in range(2)]
        scores = []
        for hp, half in heads:
            q2 = q_ref[0, 0, pl.ds(r0, ATTN_BLK), cols[hp]]
            qm = jnp.where(lo if half == 0 else jnp.logical_not(lo), q2, jnp.zeros_like(q2))
            scores.append(jnp.where(mask, _nt(qm, kf[pl.ds(r0, 2 * ATTN_BLK), cols[hp]]), NEG))
        maxes = [jnp.max(s, axis=-1, keepdims=True) for s in scores]
        probs = [jnp.exp2(s - m) for s, m in zip(scores, maxes)]
        dens = [jnp.sum(p, axis=-1, keepdims=True) for p in probs]
        accs = [jnp.dot(p.astype(BF16), vf[pl.ds(r0, 2 * ATTN_BLK), cols[hp]], preferred_element_type=F32)
                for p, (hp, _) in zip(probs, heads)]
        outs = [a / d for a, d in zip(accs, dens)]
        lse_tile = jnp.zeros((ATTN_BLK, LANES), F32)
        for h, (m, d) in enumerate(zip(maxes, dens)):
            lse_tile = jnp.where(lane == h, m + jnp.log2(d), lse_tile)
        for hp in range(npair):
            o_ref[0, 0, pl.ds(r0, ATTN_BLK), cols[hp]] = jnp.where(lo, outs[2 * hp], outs[2 * hp + 1]).astype(BF16)
        lse_ref[0, 0, pl.ds(r0, ATTN_BLK), :] = lse_tile
        return carry

    lax.fori_loop(0, qb // ATTN_BLK, sub, 0, unroll=True)


def _attn_branch(qkv, d):
    B, _, L, _ = qkv.shape
    qb = min(ATTN_ROWS, L)
    nsub = qb // ATTN_BLK
    cur = lambda c: pl.BlockSpec((1, 1, qb, ATTN_W), lambda b, r, n: (b, r, n, c))
    prev = lambda c: pl.BlockSpec((1, 1, ATTN_BLK, ATTN_W),
                                  lambda b, r, n: (b, r, jnp.maximum(n * nsub - 1, 0), c))
    return pl.pallas_call(
        functools.partial(_attn_kernel, qb=qb),
        grid=(B, d, L // qb),
        in_specs=[cur(0), cur(1), prev(1), cur(2), prev(2)],
        out_specs=[pl.BlockSpec((1, 1, qb, ATTN_W), lambda b, r, n: (b, r, n, 0)),
                   pl.BlockSpec((1, 1, qb, LANES), lambda b, r, n: (b, r, n, 0))],
        out_shape=[jax.ShapeDtypeStruct((B, d, L, ATTN_W), BF16),
                   jax.ShapeDtypeStruct((B, d, L, LANES), F32)],
        scratch_shapes=[pltpu.VMEM((qb + ATTN_BLK, ATTN_W), BF16),
                        pltpu.VMEM((qb + ATTN_BLK, ATTN_W), BF16)],
        compiler_params=_cparams(("arbitrary", "arbitrary", "arbitrary")),
        name=f"attn_d{d}",
    )(qkv, qkv, qkv, qkv, qkv)


def _gdn_kernel(x_ref, z_ref, g_ref, cw_ref, prm_ref, nw_ref, o_ref, xext, yc, s0, s1, *, rb, nseq):
    i = pl.program_id(1)

    @pl.when(i == 0)
    def _():
        xext[:, 0:8, :] = jnp.zeros((nseq, 8, DN_CONV), F32)
        s0[...] = jnp.zeros_like(s0)
        s1[...] = jnp.zeros_like(s1)

    @pl.when(i > 0)
    def _():
        xext[:, 0:8, :] = xext[:, rb:rb + 8, :]

    for q in range(nseq):
        xext[q, 8:, :] = x_ref[q].astype(F32)
        y = cw_ref[CONV_K - 1:CONV_K, :] * xext[q, 8:8 + rb, :]
        for j in range(CONV_K - 1):
            off = 8 - (CONV_K - 1) + j
            y = y + cw_ref[j:j + 1, :] * xext[q, off:off + rb, :]
        yc[q] = y * _sigmoid(y)

    C = DN_CHUNK
    H = DN_HEADS
    CW = H * C
    dot = functools.partial(jnp.dot, preferred_element_type=F32)

    def iota(shape, d):
        return lax.broadcasted_iota(I32, shape, d)

    ltri_b = jnp.where(iota((C, C), 0) >= iota((C, C), 1), 1.0, 0.0).astype(BF16)
    lane = iota((C, LANES), 1)
    blane = lane < H
    glane = jnp.logical_and(lane >= H, lane < 2 * H)
    e512 = jnp.where(jnp.logical_or(iota((LANES, DN_W), 1) // DN_D == iota((LANES, DN_W), 0),
                                    iota((LANES, DN_W), 1) // DN_D == iota((LANES, DN_W), 0) - H),
                     1.0, 0.0).astype(BF16)
    e256 = jnp.where(iota((LANES, CW), 1) // C == iota((LANES, CW), 0) - H, 1.0, 0.0).astype(BF16)
    row4 = iota((C, CW), 0)
    col4 = iota((C, CW), 1) % C
    eye4 = jnp.where(row4 == col4, 1.0, 0.0).astype(F32)
    blk = [iota((C, CW), 1) // C == h for h in range(H)]
    bd_cc = iota((CW, CW), 0) // C == iota((CW, CW), 1) // C
    bd_pair = iota((CW, CW), 0) // DN_D == iota((CW, CW), 1) // DN_D
    rt_mask = iota((CW, DN_W), 0) // C == iota((CW, DN_W), 1) // DN_D
    neg_a = -jnp.exp(prm_ref[0:1, :])
    dtb = prm_ref[1:2, :]
    nw = nw_ref[...]

    def hilo(x):
        hi = x.astype(BF16)
        return hi, (x - hi.astype(F32)).astype(BF16)

    def heads(a, w):
        return [a[:, h * w:(h + 1) * w] for h in range(H)]

    def l2n(a, mult):
        return jnp.concatenate(
            [p * (lax.rsqrt(jnp.sum(p * p, axis=-1, keepdims=True) + EPS) * mult) for p in heads(a, DN_D)],
            axis=1)

    def stack4(a):
        return jnp.concatenate([a, a, a, a], axis=0)

    zb = jnp.zeros((), BF16)
    nchunk = rb // C
    chunks = []
    for c, sq in ((c, sq) for c in range(nchunk) for sq in range(nseq)):
        rs = slice(c * C, (c + 1) * C)
        G = g_ref[sq, rs, :]
        xg = G + dtb
        gv = jnp.where(glane, neg_a * (jnp.maximum(xg, 0.0) + jnp.log1p(jnp.exp(-jnp.abs(xg)))), 0.0)
        be = jnp.where(blane, _sigmoid(G), 0.0)
        g_hi, g_lo = hilo(gv)
        gcum = dot(ltri_b, g_hi) + dot(ltri_b, g_lo)
        gtot = gcum[C - 1:C, :]
        eg = jnp.where(glane, jnp.exp(gcum), 0.0)
        ek = jnp.where(glane, jnp.exp(gtot - gcum), 0.0)
        ex = dot(jnp.concatenate([be, eg, ek], axis=0).astype(BF16), e512)
        bexp, egexp, ekexp = ex[0:C], ex[C:2 * C], ex[2 * C:3 * C]
        gexp = dot(g_hi, e256) + dot(g_lo, e256)
        d_hi, d_lo = hilo(jnp.where(row4 > col4, gexp, 0.0))
        diff = dot(ltri_b, d_hi) + dot(ltri_b, d_lo)
        decay = jnp.exp(jnp.where(row4 >= col4, diff, NEG))

        q4 = l2n(yc[sq, rs, 0:DN_W], DN_D ** -0.5)
        k4 = l2n(yc[sq, rs, DN_W:2 * DN_W], 1.0)
        v4 = yc[sq, rs, 2 * DN_W:3 * DN_W]
        kb4 = k4 * bexp
        vb4 = v4 * bexp
        rt = jnp.where(rt_mask, stack4(k4.astype(BF16)), jnp.zeros((), BF16))
        ai = _nt(jnp.concatenate([kb4, q4], axis=0).astype(BF16), rt)
        a4 = jnp.where(row4 > col4, ai[0:C] * decay, 0.0)
        pb = a4.astype(BF16)
        chunks.append(dict(
            sq=sq, rs=rs, q4=q4, k4=k4, kb4=kb4, vb4=vb4, egexp=egexp, ekexp=ekexp,
            intra=ai[C:2 * C] * decay, t4=eye4 - a4, pb=pb,
            bd=jnp.where(bd_cc, stack4(pb), zb)))

    for _ in range(5):
        for ch in chunks:
            ch["pb"] = dot(ch["pb"], ch["bd"]).astype(BF16)
        for ch in chunks:
            ch["bd"] = jnp.where(bd_cc, stack4(ch["pb"]), zb)
        for ch in chunks:
            ch["t4"] = ch["t4"] + dot(ch["t4"].astype(BF16), ch["bd"])

    for ch in chunks:
        q4, k4, kb4, vb4, egexp, ekexp = (ch[n] for n in ("q4", "k4", "kb4", "vb4", "egexp", "ekexp"))
        t4b = ch["t4"].astype(BF16)
        lstk = jnp.concatenate([jnp.where(blk[h], t4b, zb) for h in range(H)], axis=0)
        kbg4 = kb4 * egexp
        rstk = jnp.concatenate(
            [jnp.concatenate([vb, kbg], axis=1) for vb, kbg in zip(heads(vb4, DN_D), heads(kbg4, DN_D))],
            axis=0).astype(BF16)
        uw = dot(lstk, rstk)
        u4 = jnp.concatenate([uw[h * C:(h + 1) * C, 0:DN_D] for h in range(H)], axis=1)
        w4 = jnp.concatenate([uw[h * C:(h + 1) * C, DN_D:2 * DN_D] for h in range(H)], axis=1)
        ib = ch["intra"].astype(BF16)
        ch.update(
            u4=u4, wq=jnp.concatenate([w4, q4 * egexp], axis=0).astype(BF16),
            kd4=(k4 * ekexp).astype(BF16), gl4=egexp[C - 1:C, :],
            lint=jnp.concatenate([jnp.where(blk[h], ib, zb) for h in range(H)], axis=0))

    for ch in chunks:
        sq, rs, u4, wq, kd4, gl4, lint = (ch[n] for n in ("sq", "rs", "u4", "wq", "kd4", "gl4", "lint"))
        ra = dot(wq[:, 0:CW], s0[sq].astype(BF16))
        rc = dot(wq[:, CW:2 * CW], s1[sq].astype(BF16))
        vn = u4 - jnp.concatenate([ra[0:C], rc[0:C]], axis=1)
        vnb = vn.astype(BF16)
        oi = dot(lint, jnp.concatenate(heads(vnb, DN_D), axis=0))
        o = (jnp.concatenate([ra[C:2 * C], rc[C:2 * C]], axis=1)
             + jnp.concatenate([oi[h * C:(h + 1) * C] for h in range(H)], axis=1))
        tn = (((0,), (0,)), ((), ()))
        s0[sq] = s0[sq] * gl4[:, 0:CW] + jnp.where(
            bd_pair, lax.dot_general(kd4[:, 0:CW], vnb[:, 0:CW], tn, preferred_element_type=F32), 0.0)
        s1[sq] = s1[sq] * gl4[:, CW:2 * CW] + jnp.where(
            bd_pair, lax.dot_general(kd4[:, CW:2 * CW], vnb[:, CW:2 * CW], tn, preferred_element_type=F32), 0.0)

        z = z_ref[sq, rs, :].astype(F32)
        on = jnp.concatenate(
            [p * lax.rsqrt(jnp.mean(p * p, axis=-1, keepdims=True) + EPS) * nw for p in heads(o, DN_D)], axis=1)
        o_ref[sq, rs, :] = (on * (z * _sigmoid(z))).astype(BF16)


def _gdn(dqkv, dz, gbc, conv_w, A_log, dt_bias, dn_norm_w, B, S):
    nseq = 2 if B % 2 == 0 else 1
    rb = GDN_ROWS // nseq
    prm = jnp.zeros((2, LANES), F32)
    prm = prm.at[0, DN_HEADS:2 * DN_HEADS].set(A_log.astype(F32))
    prm = prm.at[1, DN_HEADS:2 * DN_HEADS].set(dt_bias.astype(F32))
    out = pl.pallas_call(
        functools.partial(_gdn_kernel, rb=rb, nseq=nseq),
        grid=(B // nseq, S // rb),
        in_specs=[pl.BlockSpec((nseq, rb, DN_CONV), lambda b, i: (b, i, 0)),
                  pl.BlockSpec((nseq, rb, DN_W), lambda b, i: (b, i, 0)),
                  pl.BlockSpec((nseq, rb, LANES), lambda b, i: (b, i, 0)),
                  pl.BlockSpec((CONV_K, DN_CONV), lambda b, i: (0, 0)),
                  pl.BlockSpec((2, LANES), lambda b, i: (0, 0)),
                  pl.BlockSpec((1, DN_D), lambda b, i: (0, 0))],
        out_specs=pl.BlockSpec((nseq, rb, DN_W), lambda b, i: (b, i, 0)),
        out_shape=jax.ShapeDtypeStruct((B, S, DN_W), BF16),
        scratch_shapes=[pltpu.VMEM((nseq, rb + 8, DN_CONV), F32),
                        pltpu.VMEM((nseq, rb, DN_CONV), F32),
                        pltpu.VMEM((nseq, 2 * DN_D, 2 * DN_D), F32),
                        pltpu.VMEM((nseq, 2 * DN_D, 2 * DN_D), F32)],
        compiler_params=_cparams(("arbitrary", "arbitrary")),
        name="gdn",
    )(dqkv.reshape(B, S, DN_CONV), dz.reshape(B, S, DN_W), gbc.reshape(B, S, LANES),
      conv_w, prm, dn_norm_w.reshape(1, DN_D))
    return out.reshape(B * S, DN_W)


def _out_kernel(o1_ref, o2_ref, o3_ref, l1_ref, l2_ref, l3_ref, dn_ref, x_ref, mod_ref, wo_ref,
                n2_ref, wr_ref, br_ref,
                x1_ref, h2_ref, te_ref, rk_ref, gtc_ref, cnt_ref, base, scr, *, tm):
    i = pl.program_id(0)

    @pl.when(i == 0)
    def _():
        base[...] = jnp.zeros_like(base)

    def natural(ref, d):
        if d == 1:
            return ref[0, 0].astype(F32)
        nl = ref.shape[-1] // LANES
        for res in range(d):
            blk = ref[0, res].astype(F32)
            for c in range(nl):
                scr[c, pl.ds(res, tm // d, stride=d), :] = blk[:, c * LANES:(c + 1) * LANES]
        return jnp.concatenate([scr[c] for c in range(nl)], axis=1)

    l1, l2, l3 = (natural(r, d) for r, d in zip((l1_ref, l2_ref, l3_ref), DILATIONS))
    mx = jnp.maximum(jnp.maximum(l1, l2), l3)
    e1, e2, e3 = jnp.exp2(l1 - mx), jnp.exp2(l2 - mx), jnp.exp2(l3 - mx)
    zs = e1 + e2 + e3
    er = lax.broadcasted_iota(I32, (LANES, ATTN_W), 0)
    ec = lax.broadcasted_iota(I32, (LANES, ATTN_W), 1)
    expand = jnp.where(ec // ATTN_HD == er, 1.0, 0.0).astype(BF16)
    attn = jnp.zeros((tm, ATTN_W), F32)
    for e, o_ref, d in zip((e1, e2, e3), (o1_ref, o2_ref, o3_ref), DILATIONS):
        wgt = jnp.dot((e / zs).astype(BF16), expand, preferred_element_type=F32)
        attn = attn + wgt * natural(o_ref, d)
    mix = (jnp.dot(attn.astype(BF16), wo_ref[0:ATTN_W, :], preferred_element_type=F32)
           + jnp.dot(dn_ref[...], wo_ref[ATTN_W:, :], preferred_element_type=F32))
    x1 = x_ref[...] + mod_ref[0, 2:3, :] * mix
    x1_ref[...] = x1
    ms = jnp.mean(x1 * x1, axis=-1, keepdims=True)
    h2 = x1 * lax.rsqrt(ms + EPS) * n2_ref[...]
    h2 = h2 * (1.0 + mod_ref[0, 4:5, :]) + mod_ref[0, 3:4, :]
    _store_slabs(h2_ref, h2)

    def split(a):
        hi = a.astype(BF16)
        return hi, (a - hi.astype(F32)).astype(BF16)

    h_hi, h_lo = split(h2)
    w_hi, w_lo = split(wr_ref[...])
    lg = _nt(w_hi, h_hi) + (_nt(w_hi, h_lo) + _nt(w_lo, h_hi)) + br_ref[...]
    eidx = lax.broadcasted_iota(I32, (N_EXPERTS, tm), 0)
    vals, idxs, sels = [], [], []
    for _ in range(TOP_K):
        m = jnp.max(lg, axis=0, keepdims=True)
        idx = jnp.min(jnp.where(lg == m, eidx, N_EXPERTS), axis=0, keepdims=True)
        sel = eidx == idx
        vals.append(m)
        idxs.append(idx)
        sels.append(sel)
        lg = jnp.where(sel, -jnp.inf, lg)
    ex = [jnp.exp(v - vals[0]) for v in vals]
    den = ex[0] + ex[1] + ex[2] + ex[3]
    gates = [e / den for e in ex]

    msum = jnp.zeros((N_EXPERTS, tm), F32)
    for sel in sels:
        msum = msum + jnp.where(sel, 1.0, 0.0)
    tr = lax.broadcasted_iota(I32, (tm, tm), 0)
    tc = lax.broadcasted_iota(I32, (tm, tm), 1)
    upper = jnp.where(tr <= tc, 1.0, 0.0).astype(BF16)
    incl = jnp.dot(msum.astype(BF16), upper, preferred_element_type=F32)
    pos = base[:, 0:1] + (incl - msum)
    sub8 = lax.broadcasted_iota(I32, (8, tm), 0)
    te = jnp.zeros((8, tm), I32)
    rk = jnp.zeros((8, tm), I32)
    gt = jnp.zeros((8, tm), F32)
    for k in range(TOP_K):
        rank_k = jnp.sum(jnp.where(sels[k], pos, 0.0), axis=0, keepdims=True).astype(I32)
        te = jnp.where(sub8 == k, idxs[k], te)
        rk = jnp.where(sub8 == k, rank_k, rk)
        gt = jnp.where(sub8 == k, gates[k], gt)
    te_ref[...] = te
    rk_ref[...] = rk
    gtc_ref[...] = jnp.transpose(jnp.concatenate([gt, jnp.zeros((LANES - 8, tm), F32)], axis=0))
    base[...] = base[...] + jnp.sum(msum, axis=1, keepdims=True)
    cnt_ref[...] = base[...].astype(I32)


def _outproj(o1, o2, o3, l1, l2, l3, dn, x2, mod, w_out, norm2_w, w_router, b_router, S):
    T, D = x2.shape
    tm = PROJ_ROWS
    nt = S // tm
    row = lambda w: pl.BlockSpec((tm, w), lambda i: (i, 0))
    res = lambda d, w: pl.BlockSpec((1, d, tm // d, w), lambda i: (i // nt, 0, i % nt, 0))
    colb = pl.BlockSpec((8, tm), lambda i: (0, i))
    return pl.pallas_call(
        functools.partial(_out_kernel, tm=tm),
        grid=(T // tm,),
        in_specs=[res(d, ATTN_W) for d in DILATIONS] + [res(d, LANES) for d in DILATIONS]
        + [row(DN_W), row(D),
                  pl.BlockSpec((1, 6, D), lambda i: (i * tm // S, 0, 0)),
                  pl.BlockSpec((D, D), lambda i: (0, 0)),
                  pl.BlockSpec((1, D), lambda i: (0, 0)),
                  pl.BlockSpec((N_EXPERTS, D), lambda i: (0, 0)),
                  pl.BlockSpec((N_EXPERTS, 1), lambda i: (0, 0))],
        out_specs=[row(D), pl.BlockSpec((tm * (D // LANES), LANES), lambda i: (i, 0)), colb, colb,
                   row(LANES), pl.BlockSpec((N_EXPERTS, LANES), lambda i: (0, 0))],
        out_shape=[jax.ShapeDtypeStruct((T, D), F32),
                   jax.ShapeDtypeStruct((T * (D // LANES), LANES), F32),
                   jax.ShapeDtypeStruct((8, T), I32),
                   jax.ShapeDtypeStruct((8, T), I32),
                   jax.ShapeDtypeStruct((T, LANES), F32),
                   jax.ShapeDtypeStruct((N_EXPERTS, LANES), I32)],
        scratch_shapes=[pltpu.VMEM((N_EXPERTS, LANES), F32),
                        pltpu.VMEM((ATTN_W // LANES, tm, LANES), F32)],
        compiler_params=_cparams(("arbitrary",)),
        name="outproj_router",
    )(o1, o2, o3, l1, l2, l3, dn, x2, mod, w_out.astype(BF16), norm2_w.reshape(1, D),
      jnp.transpose(w_router), b_router.reshape(N_EXPERTS, 1))


def _dest_kernel(ps_ref, te_ref, rk_ref, d_ref):
    te = te_ref[...]
    acc = jnp.zeros(te.shape, I32)
    for e in range(N_EXPERTS):
        acc = jnp.where(te == e, ps_ref[e], acc)
    d_ref[...] = acc + rk_ref[...]


def _dest(pstart, te, rk):
    T = te.shape[1]
    tb = DEST_COLS
    return pl.pallas_call(
        _dest_kernel,
        grid_spec=pltpu.PrefetchScalarGridSpec(
            num_scalar_prefetch=1,
            grid=(T // tb,),
            in_specs=[pl.BlockSpec((8, tb), lambda i, ps: (0, i)),
                      pl.BlockSpec((8, tb), lambda i, ps: (0, i))],
            out_specs=pl.BlockSpec((8, tb), lambda i, ps: (0, i))),
        out_shape=jax.ShapeDtypeStruct((8, T), I32),
        compiler_params=_cparams(("arbitrary",)),
        name="dest_rows",
    )(pstart, te, rk)


SC_CORES = 2
SC_SUBCORES = 16
SC_IDX_CHUNK = 128


def _invperm(dest_flat, P):
    N = dest_flat.shape[0]
    nch = N // (SC_SUBCORES * SC_IDX_CHUNK)
    half = P // SC_CORES
    per_out = half // SC_SUBCORES
    assert N % (SC_SUBCORES * SC_IDX_CHUNK) == 0 and P % (SC_CORES * SC_SUBCORES * 8) == 0
    mesh = plsc.VectorSubcoreMesh(core_axis_name="c", subcore_axis_name="s",
                                  num_cores=SC_CORES, num_subcores=SC_SUBCORES)

    @functools.partial(
        pl.kernel, mesh=mesh, out_type=jax.ShapeDtypeStruct((P,), I32),
        scratch_types=[pltpu.VMEM((nch, SC_IDX_CHUNK), I32), pltpu.VMEM((nch, SC_IDX_CHUNK), I32),
                       pltpu.VMEM_SHARED((P,), I32), pltpu.VMEM((per_out,), I32), pltpu.SemaphoreType.DMA])
    def scatter_codes(idx_hbm, val_hbm, out_hbm, idx_v, val_v, table, stage, sem):
        core = lax.axis_index("c")
        sub = lax.axis_index("s")
        pltpu.sync_copy(idx_hbm.at[sub], idx_v)
        pltpu.sync_copy(val_hbm.at[sub], val_v)

        @pl.loop(0, nch)
        def _(j):
            pltpu.async_copy(val_v.at[j], table.at[idx_v.at[j]], sem)

        @pl.loop(0, nch)
        def _(j):
            pltpu.make_async_copy(val_v.at[j], table.at[idx_v.at[j]], sem).wait()

        plsc.subcore_barrier()
        off = pl.multiple_of(core * half + sub * per_out, 8)
        pltpu.sync_copy(table.at[pl.ds(off, per_out)], stage)
        pltpu.sync_copy(stage, out_hbm.at[pl.ds(off, per_out)])

    vals = jnp.arange(N, dtype=I32)
    shape = (SC_SUBCORES, nch, SC_IDX_CHUNK)
    return scatter_codes(dest_flat.reshape(shape), vals.reshape(shape))


def _row_copy(src, dst, sem):
    return pltpu.make_async_copy(src, dst, sem)


def _moe_kernel(be_ref, nv_ref, cc_ref, cn_ref, h2_ref, w1_ref, b1_ref, w2_ref, b2_ref, y4_ref,
                xbuf, ybuf, w1b, w2b, gsem, ssem, *, F, T, D, tme, nb):
    i = pl.program_id(0)
    s = i % 2
    ns = D // LANES

    def rows(first, n):
        return pl.ds(pl.multiple_of(first * ns, ns), n * ns)

    def gather_copy(tok, p, slot):
        return _row_copy(h2_ref.at[rows(tok, 1)], xbuf.at[slot, rows(p, 1)], gsem.at[slot])

    def token_of(code):
        return code & (T - 1) if T & (T - 1) == 0 else code % T

    def issue_rows(start_row, nv):
        ng = nv // ROW_UNROLL

        def grp(g, c):
            for j in range(ROW_UNROLL):
                start_row(g * ROW_UNROLL + j, j % 2)
            return c
        lax.fori_loop(0, ng, grp, 0)

        def one(p, c):
            start_row(p, 0)
            return c
        lax.fori_loop(ng * ROW_UNROLL, nv, one, 0)

    def start_gather(code_ref, nv, slot):
        issue_rows(lambda p, pri: gather_copy(token_of(code_ref[0, 0, p]), p, slot).start(priority=pri), nv)

    def wait_rows(make, nv):
        @pl.when(nv > 0)
        def _():
            make(nv).wait()

    def gathered(n, slot):
        return _row_copy(h2_ref.at[rows(0, n)], xbuf.at[slot, rows(0, n)], gsem.at[slot])

    def scattered(n, slot):
        return _row_copy(ybuf.at[slot, rows(0, n)], y4_ref.at[rows(0, n)], ssem.at[slot])

    @pl.when(i == 0)
    def _():
        start_gather(cc_ref, tme, 0)

    nv = nv_ref[i]
    gathered(tme, s).wait()

    @pl.when(i >= 2)
    def _():
        wait_rows(lambda n: scattered(n, s), nv_ref[i - 2])

    @pl.when(jnp.logical_or(i == 0, be_ref[i] != be_ref[jnp.maximum(i - 1, 0)]))
    def _():
        w1b[...] = w1_ref[0].astype(BF16)
        w2b[...] = w2_ref[0].astype(BF16)

    def compute(slot):
        x = _load_slabs(xbuf, tme, D, lead=(slot,)).astype(BF16)
        hgu = jnp.dot(x, w1b[...], preferred_element_type=F32) + b1_ref[0]
        gate = jnp.minimum(hgu[:, :F], SWIGLU_LIMIT)
        up = jnp.clip(hgu[:, F:], -SWIGLU_LIMIT, SWIGLU_LIMIT)
        act = gate * _sigmoid(SWIGLU_ALPHA * gate) * (up + 1.0)
        y = jnp.dot(act.astype(BF16), w2b[...], preferred_element_type=F32) + b2_ref[0]
        _store_slabs(ybuf, y, lead=(slot,))
        for p in range(tme):
            gather_copy(token_of(cn_ref[0, 0, p]), p, 1 - slot).start(priority=p % 2)

        issue_rows(lambda p, pri: _row_copy(ybuf.at[slot, rows(p, 1)], y4_ref.at[rows(cc_ref[0, 0, p], 1)],
                                            ssem.at[slot]).start(priority=pri), nv)

    for slot in range(2):
        pl.when(jnp.logical_and(nv > 0, s == slot))(functools.partial(compute, slot))

    @pl.when(jnp.logical_and(nv == 0, i + 1 < nb))
    def _():
        gathered(tme, 1 - s).start()

    @pl.when(i == nb - 1)
    def _():
        @pl.when(nv > 0)
        def _():
            gathered(tme, 1 - s).wait()
        wait_rows(lambda n: scattered(n, s), nv)
        if nb > 1:
            wait_rows(lambda n: scattered(n, 1 - s), nv_ref[i - 1])


def _experts(blk_exp, blk_valid, codes, h2s, w1, b1, w2, b2, tme):
    E, D, F2 = w1.shape
    F = F2 // 2
    ns = D // LANES
    T = h2s.shape[0] // ns
    nb = blk_exp.shape[0]
    codes3 = codes.reshape(nb, 1, tme)
    wspec = lambda shape: pl.BlockSpec(shape, lambda i, be, nv: (be[i], 0, 0))
    cspec = lambda off: pl.BlockSpec((1, 1, tme), lambda i, be, nv: (jnp.minimum(i + off, nb - 1), 0, 0),
                                     memory_space=pltpu.SMEM)
    return pl.pallas_call(
        functools.partial(_moe_kernel, F=F, T=T, D=D, tme=tme, nb=nb),
        grid_spec=pltpu.PrefetchScalarGridSpec(
            num_scalar_prefetch=2,
            grid=(nb,),
            in_specs=[cspec(0), cspec(1),
                      pl.BlockSpec(memory_space=pl.ANY),
                      wspec((1, D, F2)), wspec((1, 1, F2)), wspec((1, F, D)), wspec((1, 1, D))],
            out_specs=pl.BlockSpec(memory_space=pl.ANY),
            scratch_shapes=[pltpu.VMEM((2, tme * ns, LANES), F32), pltpu.VMEM((2, tme * ns, LANES), F32),
                            pltpu.VMEM((D, F2), BF16), pltpu.VMEM((F, D), BF16),
                            pltpu.SemaphoreType.DMA((2,)), pltpu.SemaphoreType.DMA((2,))]),
        out_shape=jax.ShapeDtypeStruct((TOP_K * T * ns, LANES), F32),
        compiler_params=_cparams(("arbitrary",)),
        name="experts",
    )(blk_exp, blk_valid, codes3, codes3, h2s, w1, b1.reshape(E, 1, F2), w2, b2.reshape(E, 1, D))


def _comb_kernel(g_ref, x1_ref, mod_ref, fw_ref, y0_ref, y1_ref, y2_ref, y3_ref, o_ref):
    g = g_ref[...]
    n, d = x1_ref.shape
    y = g[:, 0:1] * _load_slabs(y0_ref, n, d)
    for k, y_ref in ((1, y1_ref), (2, y2_ref), (3, y3_ref)):
        y = y + g[:, k:k + 1] * _load_slabs(y_ref, n, d)
    x2 = x1_ref[...] + mod_ref[0, 5:6, :] * y
    ms = jnp.mean(x2 * x2, axis=-1, keepdims=True)
    o_ref[...] = x2 * lax.rsqrt(ms + EPS) * fw_ref[...]


def _combine(gtc, x1, mod, final_w, y4, S):
    T, D = x1.shape
    tmc = COMBINE_ROWS
    nt = T // tmc
    yspec = lambda k: pl.BlockSpec((tmc * (D // LANES), LANES), lambda i: (k * nt + i, 0))
    return pl.pallas_call(
        _comb_kernel,
        grid=(nt,),
        in_specs=[pl.BlockSpec((tmc, LANES), lambda i: (i, 0)),
                  pl.BlockSpec((tmc, D), lambda i: (i, 0)),
                  pl.BlockSpec((1, 6, D), lambda i: (i * tmc // S, 0, 0)),
                  pl.BlockSpec((1, D), lambda i: (0, 0)),
                  yspec(0), yspec(1), yspec(2), yspec(3)],
        out_specs=pl.BlockSpec((tmc, D), lambda i: (i, 0)),
        out_shape=jax.ShapeDtypeStruct((T, D), F32),
        compiler_params=_cparams(("arbitrary",)),
        name="combine",
    )(gtc, x1, mod, final_w.reshape(1, D), y4, y4, y4, y4)


def _layer(x2, mod, B, S, norm1_w, w_in, conv_w, A_log, dt_bias, dn_norm_w, w_out,
           norm2_w, w_router, b_router, w1, b1, w2, b2):
    T, D = x2.shape
    qkv, qkv4, qkv16, dqkv, dz, gbc = _inproj(x2, mod, norm1_w, w_in, S)
    views = (qkv.reshape(B, 1, S, 3 * ATTN_W), qkv4, qkv16)
    branches = [_attn_branch(v, d) for v, d in zip(views, DILATIONS)]
    dn = _gdn(dqkv, dz, gbc, conv_w, A_log, dt_bias, dn_norm_w, B, S)
    (o1, l1), (o2, l2), (o3, l3) = branches
    x1, h2, te, rk, gtc, cnt = _outproj(o1, o2, o3, l1, l2, l3, dn, x2, mod, w_out, norm2_w,
                                        w_router, b_router, S)
    tme = MOE_ROWS
    P = T * TOP_K + N_EXPERTS * tme
    counts = cnt[:, 0]
    padded = (counts + tme - 1) // tme * tme
    pend = jnp.cumsum(padded)
    pstart = (pend - padded).astype(I32)
    blk_start = jnp.arange(P // tme, dtype=I32) * tme
    blk_exp = jnp.minimum(jnp.sum((pend[None, :] <= blk_start[:, None]).astype(I32), axis=1),
                          N_EXPERTS - 1).astype(I32)
    blk_valid = jnp.clip(pstart[blk_exp] + counts[blk_exp] - blk_start, 0, tme).astype(I32)
    dest = _dest(pstart, te, rk)
    codes = _invperm(dest[:TOP_K].reshape(TOP_K * T), P)
    y4 = _experts(blk_exp, blk_valid, codes, h2, w1, b1, w2, b2, tme)
    return x1, gtc, y4


def kernel(x, c, w_ada, b_ada, norm1_w, w_in, conv_w, A_log, dt_bias, dn_norm_w, w_out, norm2_w,
           w_router, b_router, w1, b1, w2, b2, final_norm_w):
    B, S, D = x.shape
    depth = w_ada.shape[0]
    assert depth == 1, "one layer: the final norm is fused into the combine step"
    assert S % (ATTN_BLK * DILATIONS[-1]) == 0 and S % max(PROJ_ROWS, GDN_ROWS, COMBINE_ROWS) == 0
    assert (B * S) % DEST_COLS == 0 and w_ada.shape[2] % ADA_COLS == 0
    x2 = x.reshape(B * S, D)
    mod = _ada(c, w_ada[0], b_ada[0])
    x1, gtc, y4 = _layer(x2, mod, B, S, norm1_w[0], w_in[0], conv_w[0], A_log[0], dt_bias[0],
                         dn_norm_w[0], w_out[0], norm2_w[0], w_router[0], b_router[0],
                         w1[0], b1[0], w2[0], b2[0])
    out = _combine(gtc, x1, mod, final_norm_w, y4, S)
    return out.reshape(B, S, D)
```
